```python
import jax, jax.numpy as jnp
from jax import lax
import numpy as np

D_MODEL = 1024
BATCH = 32
SEQ = 2048
DEPTH = 2

D_MIX = 2 * D_MODEL
D_SSD = D_MIX // 2
SSD_HEAD_DIM = 64
SSD_HEADS = D_SSD // SSD_HEAD_DIM
SSD_GROUPS = 2
SSD_STATE = 128
SSD_CONV = 4
CHUNK = 128
MLA_HEADS = 8
QK_NOPE = 128
QK_ROPE = 64
V_DIM = 128
D_ATT = MLA_HEADS * V_DIM
Q_RANK = 384
KV_RANK = 256
ROPE_BASE = 10000.0
Q_BLOCK = 128
D_FF = 2816
FF_CONV = 3
EPS = 1e-6

D_XBC = D_SSD + 2 * SSD_GROUPS * SSD_STATE
IN_SIZES = [D_SSD, D_XBC, SSD_HEADS, Q_RANK, KV_RANK, QK_ROPE]
IN_SPLITS = np.cumsum(IN_SIZES)[:-1].tolist()
D_IN = int(sum(IN_SIZES))

kernel_name = "hybrid_ssd_mla_convglu_adaln"


def rms_norm(x, g):
    xf = x.astype(jnp.float32)
    y = xf * lax.rsqrt(jnp.mean(xf * xf, axis=-1, keepdims=True) + EPS)
    return y.astype(x.dtype) * g


def modulate(h, shift, scale):
    return h * (1 + scale[:, None, :]) + shift[:, None, :]


def causal_depthwise_conv(u, w, b):
    k = w.shape[0]
    out = lax.conv_general_dilated(
        u, w[:, None, :].astype(u.dtype), window_strides=(1,), padding=[(k - 1, 0)],
        dimension_numbers=('NWC', 'WIO', 'NWC'), feature_group_count=u.shape[-1])
    return out + b


def apply_rope(t, cos, sin):
    t1, t2 = jnp.split(t, 2, axis=-1)
    return jnp.concatenate([t1 * cos - t2 * sin, t2 * cos + t1 * sin], axis=-1)


def ssd_chunked_scan(xs, dt, a_head, bm, cm):
    bsz, s, h, p = xs.shape
    nc = s // CHUNK
    k = h // SSD_GROUPS
    x_dt = (xs * dt[..., None]).reshape(bsz, nc, CHUNK, SSD_GROUPS, k, p)
    a = (dt * a_head).reshape(bsz, nc, CHUNK, SSD_GROUPS, k).transpose(0, 1, 3, 4, 2)
    bm = bm.reshape(bsz, nc, CHUNK, SSD_GROUPS, SSD_STATE)
    cm = cm.reshape(bsz, nc, CHUNK, SSD_GROUPS, SSD_STATE)
    a_cs = jnp.cumsum(a, axis=-1)
    causal = jnp.asarray(np.tril(np.ones((CHUNK, CHUNK), dtype=bool)))
    seg = a_cs[..., :, None] - a_cs[..., None, :]
    decay_in = jnp.exp(jnp.where(causal, seg, -jnp.inf))
    cb = jnp.einsum('bclgn,bcsgn->bcgls', cm, bm)
    y_diag = jnp.einsum('bcgkls,bcsgkp->bclgkp', cb[:, :, :, None] * decay_in, x_dt)
    decay_to_end = jnp.exp(a_cs[..., -1:] - a_cs)
    chunk_states = jnp.einsum('bclgn,bcgkl,bclgkp->bcgkpn', bm, decay_to_end, x_dt)
    chunk_decay = jnp.exp(a_cs[..., -1])

    def step(state, inp):
        st, dec = inp
        return state * dec[..., None, None] + st, state

    init = jnp.zeros((bsz, SSD_GROUPS, k, p, SSD_STATE), chunk_states.dtype)
    _, prev = lax.scan(step, init, (jnp.moveaxis(chunk_states, 1, 0), jnp.moveaxis(chunk_decay, 1, 0)))
    prev = jnp.moveaxis(prev, 0, 1)
    y_off = jnp.einsum('bclgn,bcgkpn,bcgkl->bclgkp', cm, prev, jnp.exp(a_cs))
    return (y_diag + y_off).reshape(bsz, s, h, p)


def ssd_mixer(z, xbc, dt_raw, conv_w, conv_b, dt_bias, a_log, d_skip, ssd_norm):
    bsz, s, _ = z.shape
    xbc = jax.nn.silu(causal_depthwise_conv(xbc, conv_w, conv_b))
    xs, bm, cm = jnp.split(xbc, [D_SSD, D_SSD + SSD_GROUPS * SSD_STATE], axis=-1)
    xs = xs.reshape(bsz, s, SSD_HEADS, SSD_HEAD_DIM)
    bm = bm.reshape(bsz, s, SSD_GROUPS, SSD_STATE)
    cm = cm.reshape(bsz, s, SSD_GROUPS, SSD_STATE)
    dt = jax.nn.softplus(dt_raw + dt_bias)
    a_head = -jnp.exp(a_log)
    y = ssd_chunked_scan(xs, dt, a_head, bm, cm) + xs * d_skip[:, None]
    y = y.reshape(bsz, s, D_SSD) * jax.nn.silu(z)
    yg = y.reshape(bsz, s, SSD_GROUPS, D_SSD // SSD_GROUPS).astype(jnp.float32)
    yg = yg * lax.rsqrt(jnp.mean(yg * yg, axis=-1, keepdims=True) + EPS)
    return yg.reshape(bsz, s, D_SSD).astype(y.dtype) * ssd_norm


def mla_mixer(cq, ckv, k_rope_raw, cos, sin, q_norm, w_uq, kv_norm, w_ukv, attn_norm):
    bsz, s, _ = cq.shape
    q = (rms_norm(cq, q_norm) @ w_uq).reshape(bsz, s, MLA_HEADS, QK_NOPE + QK_ROPE)
    q_nope, q_rope = jnp.split(q, [QK_NOPE], axis=-1)
    q_rope = apply_rope(q_rope, cos[:, :, None, :], sin[:, :, None, :])
    kv = (rms_norm(ckv, kv_norm) @ w_ukv).reshape(bsz, s, MLA_HEADS, QK_NOPE + V_DIM)
    k_nope, v = jnp.split(kv, [QK_NOPE], axis=-1)
    k_rope = apply_rope(k_rope_raw, cos, sin)
    scale = (QK_NOPE + QK_ROPE) ** -0.5
    outs = []
    for i in range(s // Q_BLOCK):
        q0, q1 = i * Q_BLOCK, (i + 1) * Q_BLOCK
        sc = (jnp.einsum('bqhd,bkhd->bhqk', q_nope[:, q0:q1], k_nope[:, :q1])
              + jnp.einsum('bqhr,bkr->bhqk', q_rope[:, q0:q1], k_rope[:, :q1]))
        sc = sc.astype(jnp.float32) * scale
        mask = jnp.asarray(np.arange(q0, q1)[:, None] >= np.arange(q1)[None, :])
        probs = jax.nn.softmax(jnp.where(mask, sc, -jnp.inf), axis=-1).astype(v.dtype)
        outs.append(jnp.einsum('bhqk,bkhd->bqhd', probs, v[:, :q1]))
    o = jnp.concatenate(outs, axis=1).reshape(bsz, s, D_ATT)
    return rms_norm(o, attn_norm)


def conv_glu_ffn(h, w_up, conv_w, conv_b, w_down):
    u = causal_depthwise_conv(h @ w_up, conv_w, conv_b)
    gate, val = jnp.split(u, 2, axis=-1)
    return (jax.nn.silu(gate) * val) @ w_down


def _fwd_setup_inputs(seed: int = 0) -> dict:
    key = jax.random.key(seed)
    ks = jax.random.split(key, 32)

    def nrm(k, shape, scale):
        return jax.random.normal(k, shape, jnp.float32) * scale

    def gain(k, shape):
        return 1.0 + nrm(k, shape, 0.02)

    L = DEPTH
    x = nrm(ks[0], (BATCH, SEQ, D_MODEL), 1.0)
    c = nrm(ks[1], (BATCH, D_MODEL), 1.0)
    offsets = jax.random.randint(ks[2], (BATCH, 1), 0, 4096, dtype=jnp.int32)
    positions = (offsets + jnp.arange(SEQ, dtype=jnp.int32)[None, :]).astype(jnp.int32)
    dt0 = jnp.exp(jax.random.uniform(ks[9], (L, SSD_HEADS), jnp.float32, np.log(1e-3), np.log(1e-1)))
    dt_bias = dt0 + jnp.log(-jnp.expm1(-dt0))
    a_log = jnp.log(jax.random.uniform(ks[10], (L, SSD_HEADS), jnp.float32, 1.0, 16.0))
    return {
        'x': x, 'c': c, 'positions': positions,
        'w_ada': nrm(ks[3], (L, D_MODEL, 6 * D_MODEL), D_MODEL ** -0.5),
        'b_ada': nrm(ks[4], (L, 6 * D_MODEL), 0.02),
        'norm_mix': gain(ks[5], (L, D_MODEL)),
        'w_in': nrm(ks[6], (L, D_MODEL, D_IN), D_MODEL ** -0.5),
        'conv_w': nrm(ks[7], (L, SSD_CONV, D_XBC), SSD_CONV ** -0.5),
        'conv_b': nrm(ks[8], (L, D_XBC), 0.02),
        'dt_bias': dt_bias,
        'a_log': a_log,
        'd_skip': gain(ks[11], (L, SSD_HEADS)),
        'ssd_norm': gain(ks[12], (L, D_SSD)),
        'q_norm': gain(ks[13], (L, Q_RANK)),
        'w_uq': nrm(ks[14], (L, Q_RANK, MLA_HEADS * (QK_NOPE + QK_ROPE)), Q_RANK ** -0.5),
        'kv_norm': gain(ks[15], (L, KV_RANK)),
        'w_ukv': nrm(ks[16], (L, KV_RANK, MLA_HEADS * (QK_NOPE + V_DIM)), KV_RANK ** -0.5),
        'attn_norm': gain(ks[17], (L, D_ATT)),
        'w_out': nrm(ks[18], (L, D_MIX, D_MODEL), D_MIX ** -0.5),
        'norm_mlp': gain(ks[19], (L, D_MODEL)),
        'w_up': nrm(ks[20], (L, D_MODEL, 2 * D_FF), D_MODEL ** -0.5),
        'conv_ff_w': nrm(ks[21], (L, FF_CONV, 2 * D_FF), FF_CONV ** -0.5),
        'conv_ff_b': nrm(ks[22], (L, 2 * D_FF), 0.02),
        'w_down': nrm(ks[23], (L, D_FF, D_MODEL), D_FF ** -0.5),
        'final_norm': gain(ks[24], (D_MODEL,)),
    }


def _fwd_reference(x, c, positions, w_ada, b_ada, norm_mix, w_in, conv_w, conv_b, dt_bias, a_log,
              d_skip, ssd_norm, q_norm, w_uq, kv_norm, w_ukv, attn_norm, w_out, norm_mlp,
              w_up, conv_ff_w, conv_ff_b, w_down, final_norm):
    inv_freq = jnp.asarray(1.0 / (ROPE_BASE ** (np.arange(0, QK_ROPE, 2, dtype=np.float32) / QK_ROPE)))
    angles = positions.astype(jnp.float32)[..., None] * inv_freq
    cos = jnp.cos(angles).astype(x.dtype)
    sin = jnp.sin(angles).astype(x.dtype)
    c_act = jax.nn.silu(c)
    for l in range(DEPTH):
        mod = c_act @ w_ada[l] + b_ada[l]
        sh1, sc1, g1, sh2, sc2, g2 = jnp.split(mod, 6, axis=-1)
        h = modulate(rms_norm(x, norm_mix[l]), sh1, sc1)
        z, xbc, dt_raw, cq, ckv, kr = jnp.split(h @ w_in[l], IN_SPLITS, axis=-1)
        y_ssd = ssd_mixer(z, xbc, dt_raw, conv_w[l], conv_b[l], dt_bias[l], a_log[l], d_skip[l], ssd_norm[l])
        y_att = mla_mixer(cq, ckv, kr, cos, sin, q_norm[l], w_uq[l], kv_norm[l], w_ukv[l], attn_norm[l])
        y = jnp.concatenate([y_ssd, y_att], axis=-1) @ w_out[l]
        x = x + g1[:, None, :] * y
        h = modulate(rms_norm(x, norm_mlp[l]), sh2, sc2)
        x = x + g2[:, None, :] * conv_glu_ffn(h, w_up[l], conv_ff_w[l], conv_ff_b[l], w_down[l])
    return rms_norm(x, final_norm)


import jax as _jax
import jax.numpy as _jnp

TWIN_FORMAT = 'train_step'
FWD_PARAMS = ['x', 'c', 'positions', 'w_ada', 'b_ada', 'norm_mix', 'w_in', 'conv_w', 'conv_b', 'dt_bias', 'a_log', 'd_skip', 'ssd_norm', 'q_norm', 'w_uq', 'kv_norm', 'w_ukv', 'attn_norm', 'w_out', 'norm_mlp', 'w_up', 'conv_ff_w', 'conv_ff_b', 'w_down', 'final_norm']
TWIN_WEIGHTS = ['w_ada', 'b_ada', 'norm_mix', 'w_in', 'conv_w', 'conv_b', 'dt_bias', 'a_log', 'd_skip', 'ssd_norm', 'q_norm', 'w_uq', 'kv_norm', 'w_ukv', 'attn_norm', 'w_out', 'norm_mlp', 'w_up', 'conv_ff_w', 'conv_ff_b', 'w_down', 'final_norm']
TWIN_DIFF_INPUT = 'x'
TWIN_INPUTS = ['x', 'c', 'positions', 'w_ada', 'b_ada', 'norm_mix', 'w_in', 'conv_w', 'conv_b', 'dt_bias', 'a_log', 'd_skip', 'ssd_norm', 'q_norm', 'w_uq', 'kv_norm', 'w_ukv', 'attn_norm', 'w_out', 'norm_mlp', 'w_up', 'conv_ff_w', 'conv_ff_b', 'w_down', 'final_norm', 'loss_target', 'm_w_ada', 'm_b_ada', 'm_norm_mix', 'm_w_in', 'm_conv_w', 'm_conv_b', 'm_dt_bias', 'm_a_log', 'm_d_skip', 'm_ssd_norm', 'm_q_norm', 'm_w_uq', 'm_kv_norm', 'm_w_ukv', 'm_attn_norm', 'm_w_out', 'm_norm_mlp', 'm_w_up', 'm_conv_ff_w', 'm_conv_ff_b', 'm_w_down', 'm_final_norm', 'v_w_ada', 'v_b_ada', 'v_norm_mix', 'v_w_in', 'v_conv_w', 'v_conv_b', 'v_dt_bias', 'v_a_log', 'v_d_skip', 'v_ssd_norm', 'v_q_norm', 'v_w_uq', 'v_kv_norm', 'v_w_ukv', 'v_attn_norm', 'v_w_out', 'v_norm_mlp', 'v_w_up', 'v_conv_ff_w', 'v_conv_ff_b', 'v_w_down', 'v_final_norm']
TWIN_OUTPUTS = ['loss', 'grad_x', 'grad_w_ada', 'grad_b_ada', 'grad_norm_mix', 'grad_w_in', 'grad_conv_w', 'grad_conv_b', 'grad_dt_bias', 'grad_a_log', 'grad_d_skip', 'grad_ssd_norm', 'grad_q_norm', 'grad_w_uq', 'grad_kv_norm', 'grad_w_ukv', 'grad_attn_norm', 'grad_w_out', 'grad_norm_mlp', 'grad_w_up', 'grad_conv_ff_w', 'grad_conv_ff_b', 'grad_w_down', 'grad_final_norm', 'delta_w_ada', 'delta_b_ada', 'delta_norm_mix', 'delta_w_in', 'delta_conv_w', 'delta_conv_b', 'delta_dt_bias', 'delta_a_log', 'delta_d_skip', 'delta_ssd_norm', 'delta_q_norm', 'delta_w_uq', 'delta_kv_norm', 'delta_w_ukv', 'delta_attn_norm', 'delta_w_out', 'delta_norm_mlp', 'delta_w_up', 'delta_conv_ff_w', 'delta_conv_ff_b', 'delta_w_down', 'delta_final_norm', 'new_m_w_ada', 'new_m_b_ada', 'new_m_norm_mix', 'new_m_w_in', 'new_m_conv_w', 'new_m_conv_b', 'new_m_dt_bias', 'new_m_a_log', 'new_m_d_skip', 'new_m_ssd_norm', 'new_m_q_norm', 'new_m_w_uq', 'new_m_kv_norm', 'new_m_w_ukv', 'new_m_attn_norm', 'new_m_w_out', 'new_m_norm_mlp', 'new_m_w_up', 'new_m_conv_ff_w', 'new_m_conv_ff_b', 'new_m_w_down', 'new_m_final_norm', 'new_v_w_ada', 'new_v_b_ada', 'new_v_norm_mix', 'new_v_w_in', 'new_v_conv_w', 'new_v_conv_b', 'new_v_dt_bias', 'new_v_a_log', 'new_v_d_skip', 'new_v_ssd_norm', 'new_v_q_norm', 'new_v_w_uq', 'new_v_kv_norm', 'new_v_w_ukv', 'new_v_attn_norm', 'new_v_w_out', 'new_v_norm_mlp', 'new_v_w_up', 'new_v_conv_ff_w', 'new_v_conv_ff_b', 'new_v_w_down', 'new_v_final_norm']
TWIN_LEAF_KINDS = {'loss': 'loss', 'grad_x': 'grad_x', 'grad_w_ada': 'grad_w', 'grad_b_ada': 'grad_w', 'grad_norm_mix': 'grad_w', 'grad_w_in': 'grad_w', 'grad_conv_w': 'grad_w', 'grad_conv_b': 'grad_w', 'grad_dt_bias': 'grad_w', 'grad_a_log': 'grad_w', 'grad_d_skip': 'grad_w', 'grad_ssd_norm': 'grad_w', 'grad_q_norm': 'grad_w', 'grad_w_uq': 'grad_w', 'grad_kv_norm': 'grad_w', 'grad_w_ukv': 'grad_w', 'grad_attn_norm': 'grad_w', 'grad_w_out': 'grad_w', 'grad_norm_mlp': 'grad_w', 'grad_w_up': 'grad_w', 'grad_conv_ff_w': 'grad_w', 'grad_conv_ff_b': 'grad_w', 'grad_w_down': 'grad_w', 'grad_final_norm': 'grad_w', 'delta_w_ada': 'delta_w', 'delta_b_ada': 'delta_w', 'delta_norm_mix': 'delta_w', 'delta_w_in': 'delta_w', 'delta_conv_w': 'delta_w', 'delta_conv_b': 'delta_w', 'delta_dt_bias': 'delta_w', 'delta_a_log': 'delta_w', 'delta_d_skip': 'delta_w', 'delta_ssd_norm': 'delta_w', 'delta_q_norm': 'delta_w', 'delta_w_uq': 'delta_w', 'delta_kv_norm': 'delta_w', 'delta_w_ukv': 'delta_w', 'delta_attn_norm': 'delta_w', 'delta_w_out': 'delta_w', 'delta_norm_mlp': 'delta_w', 'delta_w_up': 'delta_w', 'delta_conv_ff_w': 'delta_w', 'delta_conv_ff_b': 'delta_w', 'delta_w_down': 'delta_w', 'delta_final_norm': 'delta_w', 'new_m_w_ada': 'new_m', 'new_m_b_ada': 'new_m', 'new_m_norm_mix': 'new_m', 'new_m_w_in': 'new_m', 'new_m_conv_w': 'new_m', 'new_m_conv_b': 'new_m', 'new_m_dt_bias': 'new_m', 'new_m_a_log': 'new_m', 'new_m_d_skip': 'new_m', 'new_m_ssd_norm': 'new_m', 'new_m_q_norm': 'new_m', 'new_m_w_uq': 'new_m', 'new_m_kv_norm': 'new_m', 'new_m_w_ukv': 'new_m', 'new_m_attn_norm': 'new_m', 'new_m_w_out': 'new_m', 'new_m_norm_mlp': 'new_m', 'new_m_w_up': 'new_m', 'new_m_conv_ff_w': 'new_m', 'new_m_conv_ff_b': 'new_m', 'new_m_w_down': 'new_m', 'new_m_final_norm': 'new_m', 'new_v_w_ada': 'new_v', 'new_v_b_ada': 'new_v', 'new_v_norm_mix': 'new_v', 'new_v_w_in': 'new_v', 'new_v_conv_w': 'new_v', 'new_v_conv_b': 'new_v', 'new_v_dt_bias': 'new_v', 'new_v_a_log': 'new_v', 'new_v_d_skip': 'new_v', 'new_v_ssd_norm': 'new_v', 'new_v_q_norm': 'new_v', 'new_v_w_uq': 'new_v', 'new_v_kv_norm': 'new_v', 'new_v_w_ukv': 'new_v', 'new_v_attn_norm': 'new_v', 'new_v_w_out': 'new_v', 'new_v_norm_mlp': 'new_v', 'new_v_w_up': 'new_v', 'new_v_conv_ff_w': 'new_v', 'new_v_conv_ff_b': 'new_v', 'new_v_w_down': 'new_v', 'new_v_final_norm': 'new_v'}


def _forward(args):
    return _fwd_reference(*[args[k] for k in FWD_PARAMS])


def _output_shape():
    out = _jax.eval_shape(lambda: _forward(_fwd_setup_inputs(0)))
    return out.shape, out.dtype

N_MICROBATCH = 1
ADAM_LR = 0.001
ADAM_B1 = 0.9
ADAM_B2 = 0.999
ADAM_EPS = 1e-08
ADAM_WD = 0.01
ADAM_STEP = 10
PER_EXAMPLE_BATCH_AXIS = {'x': 0, 'c': 0, 'positions': 0, 'loss_target': 0}
SHARED_INPUTS = []
_WEIGHT_DTYPES = {'w_ada': _jnp.float32, 'b_ada': _jnp.float32, 'norm_mix': _jnp.float32, 'w_in': _jnp.float32, 'conv_w': _jnp.float32, 'conv_b': _jnp.float32, 'dt_bias': _jnp.float32, 'a_log': _jnp.float32, 'd_skip': _jnp.float32, 'ssd_norm': _jnp.float32, 'q_norm': _jnp.float32, 'w_uq': _jnp.float32, 'kv_norm': _jnp.float32, 'w_ukv': _jnp.float32, 'attn_norm': _jnp.float32, 'w_out': _jnp.float32, 'norm_mlp': _jnp.float32, 'w_up': _jnp.float32, 'conv_ff_w': _jnp.float32, 'conv_ff_b': _jnp.float32, 'w_down': _jnp.float32, 'final_norm': _jnp.float32}
MOMENT_SCALE = {'w_ada': 1.394340e-01, 'b_ada': 2.483226e-01, 'norm_mix': 1.383304e-01, 'w_in': 1.111959e-01, 'conv_w': 8.270623e-02, 'conv_b': 7.853962e-02, 'dt_bias': 3.428578e-01, 'a_log': 4.117839e-01, 'd_skip': 2.820932e-01, 'ssd_norm': 9.667099e-02, 'q_norm': 4.274740e-02, 'w_uq': 2.235746e-02, 'kv_norm': 2.791257e-01, 'w_ukv': 1.042146e-01, 'attn_norm': 1.490389e-01, 'w_out': 1.808330e-01, 'norm_mlp': 1.748871e-01, 'w_up': 7.918433e-02, 'conv_ff_w': 7.975490e-02, 'conv_ff_b': 6.495392e-02, 'w_down': 1.337060e-01, 'final_norm': 6.674094e+01}


def _to_microbatches(a, axis):
    t = _jnp.moveaxis(a, axis, 0)
    t = t.reshape((N_MICROBATCH, t.shape[0] // N_MICROBATCH) + t.shape[1:])
    return _jnp.moveaxis(t, 1, axis + 1)


def setup_inputs(seed: int = 0) -> dict:
    inp = _fwd_setup_inputs(seed)
    key = _jax.random.fold_in(_jax.random.key(seed), 7919)
    shape, _ = _output_shape()
    out = dict(inp)
    out["loss_target"] = _jax.random.normal(_jax.random.fold_in(key, 0), shape, _jnp.float32)
    for i, name in enumerate(TWIN_WEIGHTS):
        w = inp[name].astype(_jnp.float32)
        if MOMENT_SCALE is None:
            s = _jnp.sqrt(_jnp.mean(_jnp.square(w)) + 1e-30)
        else:
            s = MOMENT_SCALE[name]
        km, kv = _jax.random.split(_jax.random.fold_in(key, i + 1))
        out[name] = w
        out["m_" + name] = s * _jax.random.normal(km, w.shape, _jnp.float32)
        out["v_" + name] = (s * s) * _jax.random.uniform(kv, w.shape, _jnp.float32, 0.5, 1.5)
    if N_MICROBATCH > 1:
        for name, axis in PER_EXAMPLE_BATCH_AXIS.items():
            out[name] = _to_microbatches(out[name], axis)
    return {'x': out['x'], 'c': out['c'], 'positions': out['positions'], 'w_ada': out['w_ada'], 'b_ada': out['b_ada'], 'norm_mix': out['norm_mix'], 'w_in': out['w_in'], 'conv_w': out['conv_w'], 'conv_b': out['conv_b'], 'dt_bias': out['dt_bias'], 'a_log': out['a_log'], 'd_skip': out['d_skip'], 'ssd_norm': out['ssd_norm'], 'q_norm': out['q_norm'], 'w_uq': out['w_uq'], 'kv_norm': out['kv_norm'], 'w_ukv': out['w_ukv'], 'attn_norm': out['attn_norm'], 'w_out': out['w_out'], 'norm_mlp': out['norm_mlp'], 'w_up': out['w_up'], 'conv_ff_w': out['conv_ff_w'], 'conv_ff_b': out['conv_ff_b'], 'w_down': out['w_down'], 'final_norm': out['final_norm'], 'loss_target': out['loss_target'], 'm_w_ada': out['m_w_ada'], 'm_b_ada': out['m_b_ada'], 'm_norm_mix': out['m_norm_mix'], 'm_w_in': out['m_w_in'], 'm_conv_w': out['m_conv_w'], 'm_conv_b': out['m_conv_b'], 'm_dt_bias': out['m_dt_bias'], 'm_a_log': out['m_a_log'], 'm_d_skip': out['m_d_skip'], 'm_ssd_norm': out['m_ssd_norm'], 'm_q_norm': out['m_q_norm'], 'm_w_uq': out['m_w_uq'], 'm_kv_norm': out['m_kv_norm'], 'm_w_ukv': out['m_w_ukv'], 'm_attn_norm': out['m_attn_norm'], 'm_w_out': out['m_w_out'], 'm_norm_mlp': out['m_norm_mlp'], 'm_w_up': out['m_w_up'], 'm_conv_ff_w': out['m_conv_ff_w'], 'm_conv_ff_b': out['m_conv_ff_b'], 'm_w_down': out['m_w_down'], 'm_final_norm': out['m_final_norm'], 'v_w_ada': out['v_w_ada'], 'v_b_ada': out['v_b_ada'], 'v_norm_mix': out['v_norm_mix'], 'v_w_in': out['v_w_in'], 'v_conv_w': out['v_conv_w'], 'v_conv_b': out['v_conv_b'], 'v_dt_bias': out['v_dt_bias'], 'v_a_log': out['v_a_log'], 'v_d_skip': out['v_d_skip'], 'v_ssd_norm': out['v_ssd_norm'], 'v_q_norm': out['v_q_norm'], 'v_w_uq': out['v_w_uq'], 'v_kv_norm': out['v_kv_norm'], 'v_w_ukv': out['v_w_ukv'], 'v_attn_norm': out['v_attn_norm'], 'v_w_out': out['v_w_out'], 'v_norm_mlp': out['v_norm_mlp'], 'v_w_up': out['v_w_up'], 'v_conv_ff_w': out['v_conv_ff_w'], 'v_conv_ff_b': out['v_conv_ff_b'], 'v_w_down': out['v_w_down'], 'v_final_norm': out['v_final_norm']}


def _loss(weights, diff, rest, loss_target):
    with _jax.named_scope("forward"):
        args = {**rest, TWIN_DIFF_INPUT: diff, **{k: w.astype(_WEIGHT_DTYPES[k]) for k, w in weights.items()}}
        y = _forward(args)
    with _jax.named_scope("loss_head"):
        err = _jnp.square(y.astype(_jnp.float32) - loss_target)
        return 0.5 * _jnp.sum(_jnp.mean(err, axis=-1)) if err.ndim else 0.5 * err


def _adamw(w, g, m, v):
    m = ADAM_B1 * m + (1.0 - ADAM_B1) * g
    v = ADAM_B2 * v + (1.0 - ADAM_B2) * _jnp.square(g)
    m_hat = m / (1.0 - ADAM_B1 ** ADAM_STEP)
    v_hat = v / (1.0 - ADAM_B2 ** ADAM_STEP)
    delta = -ADAM_LR * (m_hat / (_jnp.sqrt(v_hat) + ADAM_EPS) + ADAM_WD * w)
    return delta, m, v


def reference(x, c, positions, w_ada, b_ada, norm_mix, w_in, conv_w, conv_b, dt_bias, a_log, d_skip, ssd_norm, q_norm, w_uq, kv_norm, w_ukv, attn_norm, w_out, norm_mlp, w_up, conv_ff_w, conv_ff_b, w_down, final_norm, loss_target, m_w_ada, m_b_ada, m_norm_mix, m_w_in, m_conv_w, m_conv_b, m_dt_bias, m_a_log, m_d_skip, m_ssd_norm, m_q_norm, m_w_uq, m_kv_norm, m_w_ukv, m_attn_norm, m_w_out, m_norm_mlp, m_w_up, m_conv_ff_w, m_conv_ff_b, m_w_down, m_final_norm, v_w_ada, v_b_ada, v_norm_mix, v_w_in, v_conv_w, v_conv_b, v_dt_bias, v_a_log, v_d_skip, v_ssd_norm, v_q_norm, v_w_uq, v_kv_norm, v_w_ukv, v_attn_norm, v_w_out, v_norm_mlp, v_w_up, v_conv_ff_w, v_conv_ff_b, v_w_down, v_final_norm):
    given = dict(x=x, c=c, positions=positions, w_ada=w_ada, b_ada=b_ada, norm_mix=norm_mix, w_in=w_in, conv_w=conv_w, conv_b=conv_b, dt_bias=dt_bias, a_log=a_log, d_skip=d_skip, ssd_norm=ssd_norm, q_norm=q_norm, w_uq=w_uq, kv_norm=kv_norm, w_ukv=w_ukv, attn_norm=attn_norm, w_out=w_out, norm_mlp=norm_mlp, w_up=w_up, conv_ff_w=conv_ff_w, conv_ff_b=conv_ff_b, w_down=w_down, final_norm=final_norm, loss_target=loss_target, m_w_ada=m_w_ada, m_b_ada=m_b_ada, m_norm_mix=m_norm_mix, m_w_in=m_w_in, m_conv_w=m_conv_w, m_conv_b=m_conv_b, m_dt_bias=m_dt_bias, m_a_log=m_a_log, m_d_skip=m_d_skip, m_ssd_norm=m_ssd_norm, m_q_norm=m_q_norm, m_w_uq=m_w_uq, m_kv_norm=m_kv_norm, m_w_ukv=m_w_ukv, m_attn_norm=m_attn_norm, m_w_out=m_w_out, m_norm_mlp=m_norm_mlp, m_w_up=m_w_up, m_conv_ff_w=m_conv_ff_w, m_conv_ff_b=m_conv_ff_b, m_w_down=m_w_down, m_final_norm=m_final_norm, v_w_ada=v_w_ada, v_b_ada=v_b_ada, v_norm_mix=v_norm_mix, v_w_in=v_w_in, v_conv_w=v_conv_w, v_conv_b=v_conv_b, v_dt_bias=v_dt_bias, v_a_log=v_a_log, v_d_skip=v_d_skip, v_ssd_norm=v_ssd_norm, v_q_norm=v_q_norm, v_w_uq=v_w_uq, v_kv_norm=v_kv_norm, v_w_ukv=v_w_ukv, v_attn_norm=v_attn_norm, v_w_out=v_w_out, v_norm_mlp=v_norm_mlp, v_w_up=v_w_up, v_conv_ff_w=v_conv_ff_w, v_conv_ff_b=v_conv_ff_b, v_w_down=v_w_down, v_final_norm=v_final_norm)
    weights = {n: given[n] for n in TWIN_WEIGHTS}
    shared = {n: given[n] for n in SHARED_INPUTS}
    per_example = {n: given[n] for n in ['x', 'c', 'positions']}
    grad_fn = _jax.value_and_grad(_loss, argnums=(0, 1))

    def one_microbatch(ex, loss_target):
        ex = dict(ex)
        diff = ex.pop(TWIN_DIFF_INPUT)
        return grad_fn(weights, diff, {**shared, **ex}, loss_target)

    if N_MICROBATCH == 1:
        loss, (grad_w, grad_x) = one_microbatch(per_example, given["loss_target"])
    else:
        def body(carry, xs):
            loss_sum, grad_sum = carry
            l_k, (gw_k, gx_k) = one_microbatch(xs[0], xs[1])
            with _jax.named_scope("update"):
                return (loss_sum + l_k, _jax.tree.map(_jnp.add, grad_sum, gw_k)), gx_k

        init = (_jnp.zeros((), _jnp.float32), _jax.tree.map(_jnp.zeros_like, weights))
        (loss, grad_w), grad_x = _jax.lax.scan(body, init, (per_example, given["loss_target"]))
    with _jax.named_scope("update"):
        delta_w, new_m, new_v = {}, {}, {}
        for n in TWIN_WEIGHTS:
            delta_w[n], new_m[n], new_v[n] = _adamw(weights[n], grad_w[n], given["m_" + n], given["v_" + n])
    return (loss, grad_x, *[grad_w[n] for n in TWIN_WEIGHTS], *[delta_w[n] for n in TWIN_WEIGHTS],
            *[new_m[n] for n in TWIN_WEIGHTS], *[new_v[n] for n in TWIN_WEIGHTS])
```

```python
import functools

import numpy as np
import jax
import jax.numpy as jnp
from jax import lax
from jax.experimental import pallas as pl
from jax.experimental.pallas import tpu as pltpu

f32, bf16 = jnp.float32, jnp.bfloat16
HIGHEST = lax.Precision.HIGHEST

D = 1024
D_SSD = 1024
SSD_HEADS = 16
SSD_HD = 64
SSD_N = 128
CHUNK = 128
D_XBC = 1536
CONV_K = 4
MLA_H = 8
NOPE = 128
ROPE = 64
VD = 128
QK = NOPE + ROPE
Q_RANK = 384
KV_RANK = 256
D_FF = 2816
FF_K = 3
EPS = 1e-6
ROPE_BASE = 10000.0
DEPTH = 2
ADAM_LR, ADAM_B1, ADAM_B2, ADAM_EPS, ADAM_WD, ADAM_STEP = 0.001, 0.9, 0.999, 1e-08, 0.01, 10

N_DEV = 8
IN_COLS = 3456
OFF_XBC, OFF_CQ, OFF_CKV, OFF_KRDT = 1024, 2560, 3072, 3328
DT_LANE = 64
VMEM_LIMIT = 48 * 1024 * 1024


def _cparams(n_grid):
    return pltpu.CompilerParams(dimension_semantics=("arbitrary",) * n_grid, vmem_limit_bytes=VMEM_LIMIT)


def _pick(n, cands):
    for c in cands:
        if n % c == 0:
            return c
    return n


def _silu(x):
    return x * jax.nn.sigmoid(x)


def _dsilu(x):
    s = jax.nn.sigmoid(x)
    return s * (1.0 + x * (1.0 - s))


def _rowsum(x):
    return jnp.sum(x, axis=0, keepdims=True)


def mm(a, b, mode, name, out_dtype=f32, resid=None, gate=None, seq=None):
    if mode == "nn":
        (M, K), N = a.shape, b.shape[1]
    elif mode == "nt":
        (M, K), N = a.shape, b.shape[0]
    else:
        (K, M), N = a.shape, b.shape[1]
    gated = resid is not None
    tm = _pick(seq if gated else M, (512, 256, 128))
    tn = _pick(N, (512, 384, 256, 128))
    tk = _pick(K, (512, 384, 256, 128))
    nk = K // tk
    dims = {"nn": ((1,), (0,)), "nt": ((1,), (1,)), "tn": ((0,), (0,))}[mode]

    def body(a_ref, b_ref, *rest):
        if gated:
            r_ref, g_ref, o_ref, y_ref, acc = rest
        else:
            o_ref, acc = rest
        k = pl.program_id(2)

        @pl.when(k == 0)
        def _():
            acc[...] = jnp.zeros_like(acc)

        acc[...] += lax.dot_general(a_ref[...].astype(bf16), b_ref[...].astype(bf16), (dims, ((), ())),
                                    preferred_element_type=f32)

        @pl.when(k == nk - 1)
        def _():
            if gated:
                y_ref[...] = acc[...]
                o_ref[...] = r_ref[...] + g_ref[0] * acc[...]
            else:
                o_ref[...] = acc[...].astype(out_dtype)

    a_spec = pl.BlockSpec((tk, tm), lambda i, j, k: (k, i)) if mode == "tn" else pl.BlockSpec((tm, tk), lambda i, j, k: (i, k))
    b_spec = pl.BlockSpec((tn, tk), lambda i, j, k: (j, k)) if mode == "nt" else pl.BlockSpec((tk, tn), lambda i, j, k: (k, j))
    o_spec = pl.BlockSpec((tm, tn), lambda i, j, k: (i, j))
    in_specs, args = [a_spec, b_spec], [a, b]
    out_specs, out_shape = o_spec, jax.ShapeDtypeStruct((M, N), out_dtype)
    if gated:
        per = seq // tm
        in_specs += [o_spec, pl.BlockSpec((1, 1, tn), lambda i, j, k: (i // per, 0, j))]
        args += [resid, gate]
        out_specs = [o_spec, o_spec]
        out_shape = [jax.ShapeDtypeStruct((M, N), f32), jax.ShapeDtypeStruct((M, N), f32)]
    return pl.pallas_call(body, name=name, grid=(M // tm, N // tn, nk), in_specs=in_specs, out_specs=out_specs,
                          out_shape=out_shape, scratch_shapes=[pltpu.VMEM((tm, tn), f32)],
                          compiler_params=_cparams(3))(*args)


def _tok(ts, width, cb=0):
    return pl.BlockSpec((1, ts, width), lambda b, s: (b, s, cb))


def _perb(rows, width):
    return pl.BlockSpec((1, rows, width), lambda b, s: (b, 0, 0))


def _const(rows, width):
    return pl.BlockSpec((rows, width), lambda b, s: (0, 0))


def _row_call(body, name, B, S, ts, in_specs, out_specs, out_shape, scratch=()):
    return pl.pallas_call(body, name=name, grid=(B, S // ts), in_specs=in_specs, out_specs=out_specs,
                          out_shape=out_shape, scratch_shapes=list(scratch), compiler_params=_cparams(2))


def _first():
    return (pl.program_id(0) == 0) & (pl.program_id(1) == 0)


def normmod_fwd(x3, mod, g, i_sh, i_sc):
    B, S, C = x3.shape
    ts = _pick(S, (512, 256, 128))

    def body(x_ref, mod_ref, g_ref, h_ref):
        x = x_ref[0]
        r = lax.rsqrt(jnp.mean(x * x, axis=-1, keepdims=True) + EPS)
        n = x * r * g_ref[...]
        h_ref[0] = (n * (1.0 + mod_ref[0, i_sc:i_sc + 1, :]) + mod_ref[0, i_sh:i_sh + 1, :]).astype(bf16)

    return _row_call(body, "normmod_fwd", B, S, ts, [_tok(ts, C), _perb(6, C), _const(1, C)], _tok(ts, C),
                     jax.ShapeDtypeStruct((B, S, C), bf16))(x3, mod, g)


def normmod_bwd(x3, dh3, resid3, mod, g, i_sc):
    B, S, C = x3.shape
    ts = _pick(S, (512, 256, 128))

    def body(x_ref, dh_ref, r_ref, mod_ref, g_ref, dx_ref, dsh_ref, dsc_ref, dg_ref):
        @pl.when(pl.program_id(1) == 0)
        def _():
            dsh_ref[...] = jnp.zeros_like(dsh_ref)
            dsc_ref[...] = jnp.zeros_like(dsc_ref)

        @pl.when(_first())
        def _():
            dg_ref[...] = jnp.zeros_like(dg_ref)

        x, dh, gv = x_ref[0], dh_ref[0], g_ref[...]
        r = lax.rsqrt(jnp.mean(x * x, axis=-1, keepdims=True) + EPS)
        xh = x * r
        dn = dh * (1.0 + mod_ref[0, i_sc:i_sc + 1, :])
        dsh_ref[0] += _rowsum(dh)
        dsc_ref[0] += _rowsum(dh * xh * gv)
        dg_ref[...] += _rowsum(dn * xh)
        dxh = dn * gv
        dx_ref[0] = r * (dxh - xh * jnp.mean(dxh * xh, axis=-1, keepdims=True)) + r_ref[0]

    return _row_call(body, "normmod_bwd", B, S, ts,
                     [_tok(ts, C), _tok(ts, C), _tok(ts, C), _perb(6, C), _const(1, C)],
                     [_tok(ts, C), _perb(1, C), _perb(1, C), _const(1, C)],
                     [jax.ShapeDtypeStruct((B, S, C), f32), jax.ShapeDtypeStruct((B, 1, C), f32),
                      jax.ShapeDtypeStruct((B, 1, C), f32), jax.ShapeDtypeStruct((1, C), f32)])(x3, dh3, resid3, mod, g)


def gate_bwd(dx3, y3, mod, i_g):
    B, S, C = dx3.shape
    ts = _pick(S, (512, 256, 128))

    def body(dx_ref, y_ref, mod_ref, dy_ref, dgate_ref):
        @pl.when(pl.program_id(1) == 0)
        def _():
            dgate_ref[...] = jnp.zeros_like(dgate_ref)

        dx = dx_ref[0]
        dy_ref[0] = (dx * mod_ref[0, i_g:i_g + 1, :]).astype(bf16)
        dgate_ref[0] += _rowsum(dx * y_ref[0])

    return _row_call(body, "gate_bwd", B, S, ts, [_tok(ts, C), _tok(ts, C), _perb(6, C)], [_tok(ts, C), _perb(1, C)],
                     [jax.ShapeDtypeStruct((B, S, C), bf16), jax.ShapeDtypeStruct((B, 1, C), f32)])(dx3, y3, mod)


def rms_fwd(src3, width, cb, n, g, name):
    B, S, _ = src3.shape
    ts = _pick(S, (512, 256, 128))

    def body(x_ref, g_ref, o_ref):
        x = x_ref[0][:, :n]
        r = lax.rsqrt(jnp.mean(x * x, axis=-1, keepdims=True) + EPS)
        o_ref[0] = (x * r * g_ref[...]).astype(bf16)

    return _row_call(body, name, B, S, ts, [_tok(ts, width, cb), _const(1, n)], _tok(ts, n),
                     jax.ShapeDtypeStruct((B, S, n), bf16))(src3, g)


def rms_bwd(src3, width, cb, n, dout3, dcb, g, out_dtype, name):
    B, S, _ = src3.shape
    ts = _pick(S, (512, 256, 128))

    def body(x_ref, do_ref, g_ref, dx_ref, dg_ref):
        @pl.when(_first())
        def _():
            dg_ref[...] = jnp.zeros_like(dg_ref)

        x = x_ref[0][:, :n]
        do = do_ref[0].astype(f32)
        r = lax.rsqrt(jnp.mean(x * x, axis=-1, keepdims=True) + EPS)
        xh = x * r
        dg_ref[...] += _rowsum(do * xh)
        dxh = do * g_ref[...]
        dx_ref[0] = (r * (dxh - xh * jnp.mean(dxh * xh, axis=-1, keepdims=True))).astype(out_dtype)

    return _row_call(body, name, B, S, ts, [_tok(ts, width, cb), _tok(ts, n, dcb), _const(1, n)],
                     [_tok(ts, n), _const(1, n)],
                     [jax.ShapeDtypeStruct((B, S, n), out_dtype), jax.ShapeDtypeStruct((1, n), f32)])(src3, dout3, g)


def final_loss(x3, g, tgt3):
    B, S, C = x3.shape
    ts = _pick(S, (512, 256, 128))

    def body(x_ref, g_ref, t_ref, dx_ref, dg_ref, loss_ref):
        @pl.when(_first())
        def _():
            dg_ref[...] = jnp.zeros_like(dg_ref)
            loss_ref[...] = jnp.zeros_like(loss_ref)

        x, gv = x_ref[0], g_ref[...]
        r = lax.rsqrt(jnp.mean(x * x, axis=-1, keepdims=True) + EPS)
        xh = x * r
        e = xh * gv - t_ref[0]
        loss_ref[...] += 0.5 * jnp.sum(e * e) / C
        dout = e / C
        dg_ref[...] += _rowsum(dout * xh)
        dxh = dout * gv
        dx_ref[0] = r * (dxh - xh * jnp.mean(dxh * xh, axis=-1, keepdims=True))

    return _row_call(body, "final_loss", B, S, ts, [_tok(ts, C), _const(1, C), _tok(ts, C)],
                     [_tok(ts, C), _const(1, C), _const(1, 128)],
                     [jax.ShapeDtypeStruct((B, S, C), f32), jax.ShapeDtypeStruct((1, C), f32),
                      jax.ShapeDtypeStruct((1, 128), f32)])(x3, g, tgt3)


def ssd_out_fwd(yc3, p3, w):
    B, S, C = yc3.shape
    ts = _pick(S, (512, 256, 128))
    half = C // 2

    def body(y_ref, z_ref, w_ref, o_ref):
        y = y_ref[0] * _silu(z_ref[0])
        for lo in (0, half):
            yg = y[:, lo:lo + half]
            r = lax.rsqrt(jnp.mean(yg * yg, axis=-1, keepdims=True) + EPS)
            o_ref[0, :, lo:lo + half] = (yg * r * w_ref[:, lo:lo + half]).astype(bf16)

    return _row_call(body, "ssd_out_fwd", B, S, ts, [_tok(ts, C), _tok(ts, C, 0), _const(1, C)], _tok(ts, C),
                     jax.ShapeDtypeStruct((B, S, C), bf16))(yc3, p3, w)


def ssd_out_bwd(yc3, p3, dcat3, w):
    B, S, C = yc3.shape
    ts = _pick(S, (512, 256, 128))
    half = C // 2

    def body(y_ref, z_ref, do_ref, w_ref, dyc_ref, dz_ref, dw_ref):
        @pl.when(_first())
        def _():
            dw_ref[...] = jnp.zeros_like(dw_ref)

        yc, z, do = y_ref[0], z_ref[0], do_ref[0]
        sz = _silu(z)
        y = yc * sz
        for lo in (0, half):
            sl = slice(lo, lo + half)
            yg, dog, wg = y[:, sl], do[:, sl], w_ref[:, sl]
            r = lax.rsqrt(jnp.mean(yg * yg, axis=-1, keepdims=True) + EPS)
            yh = yg * r
            dw_ref[:, sl] += _rowsum(dog * yh)
            dyh = dog * wg
            dy = r * (dyh - yh * jnp.mean(dyh * yh, axis=-1, keepdims=True))
            dyc_ref[0, :, sl] = dy * sz[:, sl]
            dz_ref[0, :, sl] = (dy * yc[:, sl] * _dsilu(z[:, sl])).astype(bf16)

    return _row_call(body, "ssd_out_bwd", B, S, ts, [_tok(ts, C), _tok(ts, C, 0), _tok(ts, C, 0), _const(1, C)],
                     [_tok(ts, C), _tok(ts, C), _const(1, C)],
                     [jax.ShapeDtypeStruct((B, S, C), f32), jax.ShapeDtypeStruct((B, S, C), bf16),
                      jax.ShapeDtypeStruct((1, C), f32)])(yc3, p3, dcat3, w)


def _rot(t):
    lane = lax.broadcasted_iota(jnp.int32, t.shape, 1)
    return jnp.where(lane < ROPE // 2, -pltpu.roll(t, 128 - ROPE // 2, 1), pltpu.roll(t, ROPE // 2, 1))


def _rope(t, cosf, sinf):
    return t * cosf + _rot(t) * sinf


def _rope_t(d, cosf, sinf):
    return d * cosf - _rot(d * sinf)


def qprep_fwd(qraw3, cosf, sinf):
    B, S, W = qraw3.shape
    ts = _pick(S, (512, 256, 128))

    def body(q_ref, c_ref, s_ref, o_ref):
        c, s = c_ref[0], s_ref[0]
        for h in range(MLA_H):
            o_ref[0, :, h * 256:h * 256 + 128] = q_ref[0, :, h * 256:h * 256 + 128].astype(bf16)
            o_ref[0, :, h * 256 + 128:(h + 1) * 256] = _rope(q_ref[0, :, h * 256 + 128:(h + 1) * 256], c, s).astype(bf16)

    return _row_call(body, "qprep_fwd", B, S, ts, [_tok(ts, W), _tok(ts, 128), _tok(ts, 128)], _tok(ts, W),
                     jax.ShapeDtypeStruct((B, S, W), bf16))(qraw3, cosf, sinf)


def qprep_bwd(dq3, cosf, sinf):
    B, S, W = dq3.shape
    ts = _pick(S, (512, 256, 128))

    def body(d_ref, c_ref, s_ref, o_ref):
        c, s = c_ref[0], s_ref[0]
        for h in range(MLA_H):
            o_ref[0, :, h * 256:h * 256 + 128] = d_ref[0, :, h * 256:h * 256 + 128].astype(bf16)
            o_ref[0, :, h * 256 + 128:(h + 1) * 256] = _rope_t(d_ref[0, :, h * 256 + 128:(h + 1) * 256], c, s).astype(bf16)

    return _row_call(body, "qprep_bwd", B, S, ts, [_tok(ts, W), _tok(ts, 128), _tok(ts, 128)], _tok(ts, W),
                     jax.ShapeDtypeStruct((B, S, W), bf16))(dq3, cosf, sinf)


def kprep_fwd(kv3, p3, cosf, sinf):
    B, S, _ = kv3.shape
    ts = _pick(S, (512, 256, 128))
    Wn = MLA_H * NOPE

    def body(k_ref, v_ref, kr_ref, c_ref, s_ref, ko_ref, vo_ref):
        lane = lax.broadcasted_iota(jnp.int32, (1, 128), 1)
        kr = jnp.where(lane < ROPE, kr_ref[0], 0.0)
        kr = _rope(kr, c_ref[0], s_ref[0]).astype(bf16)
        for h in range(MLA_H):
            ko_ref[0, :, h * 256:h * 256 + 128] = k_ref[0, :, h * 128:(h + 1) * 128].astype(bf16)
            ko_ref[0, :, h * 256 + 128:(h + 1) * 256] = kr
        vo_ref[0] = v_ref[0].astype(bf16)

    return _row_call(body, "kprep_fwd", B, S, ts,
                     [_tok(ts, Wn, 0), _tok(ts, Wn, 1), _tok(ts, 128, OFF_KRDT // 128), _tok(ts, 128), _tok(ts, 128)],
                     [_tok(ts, 2 * Wn), _tok(ts, Wn)],
                     [jax.ShapeDtypeStruct((B, S, 2 * Wn), bf16), jax.ShapeDtypeStruct((B, S, Wn), bf16)])(kv3, kv3, p3, cosf, sinf)


def kprep_bwd(dk3, dv3, ddt3, cosf, sinf):
    B, S, _ = dk3.shape
    ts = _pick(S, (512, 256, 128))
    Wn = MLA_H * NOPE

    def body(dk_ref, dv_ref, ddt_ref, c_ref, s_ref, o_ref, kr_ref):
        acc = jnp.zeros((ts, 128), f32)
        for h in range(MLA_H):
            o_ref[0, :, h * 128:(h + 1) * 128] = dk_ref[0, :, h * 256:h * 256 + 128].astype(bf16)
            acc = acc + dk_ref[0, :, h * 256 + 128:(h + 1) * 256]
        o_ref[0, :, Wn:] = dv_ref[0].astype(bf16)
        lane = lax.broadcasted_iota(jnp.int32, (1, 128), 1)
        dkr = _rope_t(acc, c_ref[0], s_ref[0])
        kr_ref[0] = jnp.where(lane < ROPE, dkr, ddt_ref[0]).astype(bf16)

    return _row_call(body, "kprep_bwd", B, S, ts,
                     [_tok(ts, 2 * Wn), _tok(ts, Wn), _tok(ts, 128), _tok(ts, 128), _tok(ts, 128)],
                     [_tok(ts, 2 * Wn), _tok(ts, 128)],
                     [jax.ShapeDtypeStruct((B, S, 2 * Wn), bf16), jax.ShapeDtypeStruct((B, S, 128), bf16)])(dk3, dv3, ddt3, cosf, sinf)


def _shift_down(u, j):
    if j == 0:
        return u
    row = lax.broadcasted_iota(jnp.int32, u.shape, 0)
    return jnp.where(row < j, 0.0, pltpu.roll(u, j, 0))


def _shift_up(u, j):
    if j == 0:
        return u
    n = u.shape[0]
    row = lax.broadcasted_iota(jnp.int32, u.shape, 0)
    return jnp.where(row >= n - j, 0.0, pltpu.roll(u, n - j, 0))


def _conv(u, w, b, K):
    out = b
    for j in range(K):
        out = out + w[K - 1 - j:K - j, :] * _shift_down(u, j)
    return out


def _conv_bwd(u, du, w, K):
    dins = w[K - 1:K, :] * du
    dws = [None] * K
    dws[K - 1] = _rowsum(du * u)
    for j in range(1, K):
        dins = dins + w[K - 1 - j:K - j, :] * _shift_up(du, j)
        dws[K - 1 - j] = _rowsum(du * _shift_down(u, j))
    return dins, dws


CW = 256


def conv_ssd_fwd(p3, w, b):
    B, S, _ = p3.shape
    nb = D_XBC // CW

    def body(u_ref, w_ref, b_ref, o_ref):
        o_ref[0] = _silu(_conv(u_ref[0], w_ref[...], b_ref[...], CONV_K))

    return pl.pallas_call(body, name="conv_ssd_fwd", grid=(B, nb),
                          in_specs=[pl.BlockSpec((1, S, CW), lambda b, j: (b, 0, OFF_XBC // CW + j)),
                                    pl.BlockSpec((CONV_K, CW), lambda b, j: (0, j)),
                                    pl.BlockSpec((1, CW), lambda b, j: (0, j))],
                          out_specs=pl.BlockSpec((1, S, CW), lambda b, j: (b, 0, j)),
                          out_shape=jax.ShapeDtypeStruct((B, S, D_XBC), f32), compiler_params=_cparams(2))(p3, w, b)


def conv_ssd_bwd(p3, dxc3, w, b):
    B, S, _ = p3.shape
    nb = D_XBC // CW

    def body(u_ref, d_ref, w_ref, b_ref, du_ref, dw_ref, db_ref):
        @pl.when(pl.program_id(1) == 0)
        def _():
            dw_ref[...] = jnp.zeros_like(dw_ref)
            db_ref[...] = jnp.zeros_like(db_ref)

        u, wv = u_ref[0], w_ref[...]
        dpre = d_ref[0] * _dsilu(_conv(u, wv, b_ref[...], CONV_K))
        dins, dws = _conv_bwd(u, dpre, wv, CONV_K)
        du_ref[0] = dins.astype(bf16)
        for k in range(CONV_K):
            dw_ref[k:k + 1, :] += dws[k]
        db_ref[...] += _rowsum(dpre)

    return pl.pallas_call(body, name="conv_ssd_bwd", grid=(nb, B),
                          in_specs=[pl.BlockSpec((1, S, CW), lambda j, b: (b, 0, OFF_XBC // CW + j)),
                                    pl.BlockSpec((1, S, CW), lambda j, b: (b, 0, j)),
                                    pl.BlockSpec((CONV_K, CW), lambda j, b: (0, j)),
                                    pl.BlockSpec((1, CW), lambda j, b: (0, j))],
                          out_specs=[pl.BlockSpec((1, S, CW), lambda j, b: (b, 0, j)),
                                     pl.BlockSpec((CONV_K, CW), lambda j, b: (0, j)),
                                     pl.BlockSpec((1, CW), lambda j, b: (0, j))],
                          out_shape=[jax.ShapeDtypeStruct((B, S, D_XBC), bf16), jax.ShapeDtypeStruct((CONV_K, D_XBC), f32),
                                     jax.ShapeDtypeStruct((1, D_XBC), f32)], compiler_params=_cparams(2))(p3, dxc3, w, b)


def glu_fwd(u3, w, b):
    B, S, _ = u3.shape
    nb = D_FF // CW

    def body(ug_ref, uv_ref, wg_ref, wv_ref, bg_ref, bv_ref, o_ref):
        g = _conv(ug_ref[0], wg_ref[...], bg_ref[...], FF_K)
        v = _conv(uv_ref[0], wv_ref[...], bv_ref[...], FF_K)
        o_ref[0] = (_silu(g) * v).astype(bf16)

    def blk(off):
        return pl.BlockSpec((1, S, CW), lambda b, j: (b, 0, off + j))

    def par(rows, off):
        return pl.BlockSpec((rows, CW), lambda b, j: (0, off + j))

    return pl.pallas_call(body, name="glu_fwd", grid=(B, nb),
                          in_specs=[blk(0), blk(nb), par(FF_K, 0), par(FF_K, nb), par(1, 0), par(1, nb)],
                          out_specs=blk(0), out_shape=jax.ShapeDtypeStruct((B, S, D_FF), bf16),
                          compiler_params=_cparams(2))(u3, u3, w, w, b, b)


def glu_bwd(u3, da3, w, b):
    B, S, _ = u3.shape
    nb = D_FF // CW

    def body(ug_ref, uv_ref, da_ref, wg_ref, wv_ref, bg_ref, bv_ref, dug_ref, duv_ref, dwg_ref, dwv_ref, dbg_ref, dbv_ref):
        @pl.when(pl.program_id(1) == 0)
        def _():
            for r in (dwg_ref, dwv_ref, dbg_ref, dbv_ref):
                r[...] = jnp.zeros_like(r)

        ug, uv, da, wg, wv = ug_ref[0], uv_ref[0], da_ref[0], wg_ref[...], wv_ref[...]
        g = _conv(ug, wg, bg_ref[...], FF_K)
        v = _conv(uv, wv, bv_ref[...], FF_K)
        dg = da * v * _dsilu(g)
        dv = da * _silu(g)
        ding, dwsg = _conv_bwd(ug, dg, wg, FF_K)
        dinv, dwsv = _conv_bwd(uv, dv, wv, FF_K)
        dug_ref[0] = ding.astype(bf16)
        duv_ref[0] = dinv.astype(bf16)
        for k in range(FF_K):
            dwg_ref[k:k + 1, :] += dwsg[k]
            dwv_ref[k:k + 1, :] += dwsv[k]
        dbg_ref[...] += _rowsum(dg)
        dbv_ref[...] += _rowsum(dv)

    def blk(off):
        return pl.BlockSpec((1, S, CW), lambda j, b: (b, 0, off + j))

    def par(rows, off):
        return pl.BlockSpec((rows, CW), lambda j, b: (0, off + j))

    return pl.pallas_call(body, name="glu_bwd", grid=(nb, B),
                          in_specs=[blk(0), blk(nb), blk(0), par(FF_K, 0), par(FF_K, nb), par(1, 0), par(1, nb)],
                          out_specs=[blk(0), blk(0), par(FF_K, 0), par(FF_K, 0), par(1, 0), par(1, 0)],
                          out_shape=[jax.ShapeDtypeStruct((B, S, D_FF), bf16), jax.ShapeDtypeStruct((B, S, D_FF), bf16),
                                     jax.ShapeDtypeStruct((FF_K, D_FF), f32), jax.ShapeDtypeStruct((FF_K, D_FF), f32),
                                     jax.ShapeDtypeStruct((1, D_FF), f32), jax.ShapeDtypeStruct((1, D_FF), f32)],
                          compiler_params=_cparams(2))(u3, u3, da3, w, w, b, b)


def _ssd_decay(dtb, bias_row, alog_row):
    lane = lax.broadcasted_iota(jnp.int32, (1, 128), 1)
    hmask = (lane >= DT_LANE) & (lane < DT_LANE + SSD_HEADS)
    dt = jnp.where(hmask, jax.nn.softplus(dtb + bias_row), 0.0)
    a = dt * jnp.where(hmask, -jnp.exp(alog_row), 0.0)
    r = lax.broadcasted_iota(jnp.int32, (CHUNK, CHUNK), 0)
    c = lax.broadcasted_iota(jnp.int32, (CHUNK, CHUNK), 1)
    cs = jnp.dot((r >= c).astype(f32), a, precision=HIGHEST, preferred_element_type=f32)
    return dt, cs


def _expand(xt):
    return jnp.concatenate([jnp.broadcast_to(xt[DT_LANE + h:DT_LANE + h + 1, :], (SSD_HD, xt.shape[1]))
                            for h in range(SSD_HEADS)], axis=0)


_NT = (((1,), (1,)), ((), ()))
_TN = (((0,), (0,)), ((), ()))
GH = SSD_HEADS // 2
GR = GH * SSD_HD


def ssd_fwd(xc3, p3, bias_row, alog_row, dcol):
    B, S, _ = xc3.shape
    nc = S // CHUNK

    def body(xs_ref, bc_ref, dtb_ref, bias_ref, alog_ref, dcol_ref, y_ref, st_ref, state, yT):
        @pl.when(pl.program_id(1) == 0)
        def _():
            state[...] = jnp.zeros_like(state)

        dt, cs = _ssd_decay(dtb_ref[0], bias_ref[...], alog_ref[...])
        csT = cs.T
        eT = jnp.exp(csT)
        decX = _expand(jnp.exp(csT[:, CHUNK - 1:CHUNK] - csT))
        eX = _expand(eT)
        elastX = eX[:, CHUNK - 1:CHUNK]
        xsT = xs_ref[0].T
        uT = xsT * _expand(dt.T)
        bc = bc_ref[0]
        st_ref[0, 0] = state[...]
        srow = lax.broadcasted_iota(jnp.int32, (CHUNK, CHUNK), 0)
        lcol = lax.broadcasted_iota(jnp.int32, (CHUNK, CHUNK), 1)
        for g in range(2):
            Bg = bc[:, g * SSD_N:(g + 1) * SSD_N].astype(bf16)
            Cg = bc[:, (2 + g) * SSD_N:(3 + g) * SSD_N].astype(bf16)
            GT = lax.dot_general(Bg, Cg, _NT, preferred_element_type=f32)
            rows = slice(g * GR, (g + 1) * GR)
            Sg = state[rows]
            yoffT = lax.dot_general(Sg.astype(bf16), Cg, _NT, preferred_element_type=f32) * eX[rows]
            state[rows] = Sg * elastX[rows] + jnp.dot((uT[rows] * decX[rows]).astype(bf16), Bg, preferred_element_type=f32)
            for k in range(GH):
                h = g * GH + k
                hr = slice(h * SSD_HD, (h + 1) * SSD_HD)
                seg = csT[DT_LANE + h:DT_LANE + h + 1, :] - cs[:, DT_LANE + h:DT_LANE + h + 1]
                LT = jnp.where(lcol >= srow, jnp.exp(jnp.minimum(seg, 0.0)), 0.0)
                yT[hr] = (jnp.dot(uT[hr].astype(bf16), (GT * LT).astype(bf16), preferred_element_type=f32)
                          + yoffT[k * SSD_HD:(k + 1) * SSD_HD] + dcol_ref[hr] * xsT[hr])
        y_ref[0] = yT[...].T

    return pl.pallas_call(body, name="ssd_fwd", grid=(B, nc),
                          in_specs=[pl.BlockSpec((1, CHUNK, D_SSD), lambda b, c: (b, c, 0)),
                                    pl.BlockSpec((1, CHUNK, 512), lambda b, c: (b, c, 2)),
                                    pl.BlockSpec((1, CHUNK, 128), lambda b, c: (b, c, OFF_KRDT // 128)),
                                    _const(1, 128), _const(1, 128), _const(D_SSD, 1)],
                          out_specs=[pl.BlockSpec((1, CHUNK, D_SSD), lambda b, c: (b, c, 0)),
                                     pl.BlockSpec((1, 1, D_SSD, SSD_N), lambda b, c: (b, c, 0, 0))],
                          out_shape=[jax.ShapeDtypeStruct((B, S, D_SSD), f32), jax.ShapeDtypeStruct((B, nc, D_SSD, SSD_N), f32)],
                          scratch_shapes=[pltpu.VMEM((D_SSD, SSD_N), f32), pltpu.VMEM((D_SSD, CHUNK), f32)],
                          compiler_params=_cparams(2))(xc3, xc3, p3, bias_row, alog_row, dcol)


def ssd_bwd(xc3, p3, dy3, states, bias_row, alog_row, bias_col, alog_col, dcol):
    B, S, _ = xc3.shape
    nc = S // CHUNK

    def body(xs_ref, bc_ref, dtb_ref, dy_ref, st_ref, bias_ref, alog_ref, biasc_ref, alogc_ref, dcol_ref,
             dxc_ref, ddt_ref, dalog_ref, dd_ref, dbias_ref, dS, dUT, accA, accD, accB, dcs_diag):
        @pl.when(pl.program_id(1) == 0)
        def _():
            dS[...] = jnp.zeros_like(dS)

        @pl.when(_first())
        def _():
            accA[...] = jnp.zeros_like(accA)
            accD[...] = jnp.zeros_like(accD)
            accB[...] = jnp.zeros_like(accB)

        dtb = dtb_ref[0]
        dt, cs = _ssd_decay(dtb, bias_ref[...], alog_ref[...])
        dtT, csT = dt.T, cs.T
        decX = _expand(jnp.exp(csT[:, CHUNK - 1:CHUNK] - csT))
        eX = _expand(jnp.exp(csT))
        dtX = _expand(dtT)
        elastX = eX[:, CHUNK - 1:CHUNK]
        xsT = xs_ref[0].T
        uT = xsT * dtX
        dYT = dy_ref[0].T
        bc = bc_ref[0]
        lrow = lax.broadcasted_iota(jnp.int32, (CHUNK, CHUNK), 0)
        scol = lax.broadcasted_iota(jnp.int32, (CHUNK, CHUNK), 1)
        dcs_diag[...] = jnp.zeros_like(dcs_diag)
        vparts, zparts = [], []
        for g in range(2):
            Bf = bc[:, g * SSD_N:(g + 1) * SSD_N]
            Bg = Bf.astype(bf16)
            Cg = bc[:, (2 + g) * SSD_N:(3 + g) * SSD_N].astype(bf16)
            G = lax.dot_general(Cg, Bg, _NT, preferred_element_type=f32)
            BgT = Bf.T.astype(bf16)
            rows = slice(g * GR, (g + 1) * GR)
            dSg = dS[rows]
            Sg = st_ref[0, 0, rows, :]
            dUst = jnp.dot(dSg.astype(bf16), BgT, preferred_element_type=f32) * decX[rows]
            yoffT = lax.dot_general(Sg.astype(bf16), Cg, _NT, preferred_element_type=f32) * eX[rows]
            zparts.append(dYT[rows] * yoffT - dUst * uT[rows])
            dG = jnp.zeros((CHUNK, CHUNK), f32)
            for k in range(GH):
                h = g * GH + k
                hr = slice(h * SSD_HD, (h + 1) * SSD_HD)
                seg = cs[:, DT_LANE + h:DT_LANE + h + 1] - csT[DT_LANE + h:DT_LANE + h + 1, :]
                L = jnp.where(lrow >= scol, jnp.exp(jnp.minimum(seg, 0.0)), 0.0)
                M = G * L
                dYh = dYT[hr].astype(bf16)
                dUT[hr] = jnp.dot(dYh, M.astype(bf16), preferred_element_type=f32) + dUst[k * SSD_HD:(k + 1) * SSD_HD]
                dM = lax.dot_general(dYh, uT[hr].astype(bf16), _TN, preferred_element_type=f32)
                dG = dG + dM * L
                Wm = dM * M
                rs = lax.dot_general(jnp.ones((8, CHUNK), f32), Wm, _NT, precision=HIGHEST, preferred_element_type=f32)[0:1]
                dcs_diag[DT_LANE + h:DT_LANE + h + 1, :] = rs - _rowsum(Wm)
            dGb = dG.astype(bf16)
            dYe = (dYT[rows] * eX[rows]).astype(bf16)
            ude = (uT[rows] * decX[rows]).astype(bf16)
            dC = jnp.dot(dGb, Bg, preferred_element_type=f32) + lax.dot_general(dYe, Sg.astype(bf16), _TN, preferred_element_type=f32)
            dB = (lax.dot_general(dGb, Cg, _TN, preferred_element_type=f32)
                  + lax.dot_general(ude, dSg.astype(bf16), _TN, preferred_element_type=f32))
            dxc_ref[0, :, D_SSD + g * SSD_N:D_SSD + (g + 1) * SSD_N] = dB
            dxc_ref[0, :, D_SSD + (2 + g) * SSD_N:D_SSD + (3 + g) * SSD_N] = dC
            vparts.append(elastX[rows] * jnp.sum(dSg * Sg, axis=1, keepdims=True)
                          + jnp.sum(dUst * uT[rows], axis=1, keepdims=True))
            dS[rows] = elastX[rows] * dSg + jnp.dot(dYe, Cg, preferred_element_type=f32)
        dU = dUT[...]
        dcv = dcol_ref[...]
        dxc_ref[0, :, 0:D_SSD] = (dtX * dU + dcv * dYT).T
        lane = lax.broadcasted_iota(jnp.int32, (D_SSD, CHUNK), 1)
        Z = jnp.concatenate(zparts, axis=0) + jnp.where(lane == CHUNK - 1, jnp.concatenate(vparts, axis=0), 0.0)
        hr_ = lax.broadcasted_iota(jnp.int32, (128, D_SSD), 0)
        hc_ = lax.broadcasted_iota(jnp.int32, (128, D_SSD), 1)
        hsel = (hr_ - DT_LANE == jnp.right_shift(hc_, 6)).astype(f32)
        red = jnp.dot(hsel, jnp.concatenate([Z, dU * xsT, dYT * xsT], axis=1), precision=HIGHEST, preferred_element_type=f32)
        dcsT = red[:, 0:CHUNK] + dcs_diag[...]
        daT = jnp.dot(dcsT, (lrow >= scol).astype(f32), precision=HIGHEST, preferred_element_type=f32)
        rowi = lax.broadcasted_iota(jnp.int32, (128, 1), 0)
        hmask = (rowi >= DT_LANE) & (rowi < DT_LANE + SSD_HEADS)
        a_col = jnp.where(hmask, -jnp.exp(alogc_ref[...]), 0.0)
        ddtT = red[:, CHUNK:2 * CHUNK] + a_col * daT
        ddt_rawT = jnp.where(hmask, ddtT * jax.nn.sigmoid(dtb.T + biasc_ref[...]), 0.0)
        ddt_ref[0] = ddt_rawT.T
        accA[...] += daT * dtT
        accD[...] += red[:, 2 * CHUNK:3 * CHUNK]
        accB[...] += ddt_rawT

        @pl.when((pl.program_id(0) == B - 1) & (pl.program_id(1) == nc - 1))
        def _():
            dalog_ref[...] = jnp.broadcast_to(jnp.sum(accA[...], axis=1, keepdims=True) * a_col, (128, 128))
            dd_ref[...] = jnp.broadcast_to(jnp.sum(accD[...], axis=1, keepdims=True), (128, 128))
            dbias_ref[...] = jnp.broadcast_to(jnp.sum(accB[...], axis=1, keepdims=True), (128, 128))

    def rev(width, cb):
        return pl.BlockSpec((1, CHUNK, width), lambda b, c: (b, nc - 1 - c, cb))

    acc_spec = pl.BlockSpec((128, 128), lambda b, c: (0, 0))
    acc_shape = jax.ShapeDtypeStruct((128, 128), f32)
    return pl.pallas_call(body, name="ssd_bwd", grid=(B, nc),
                          in_specs=[rev(D_SSD, 0), rev(512, 2), rev(128, OFF_KRDT // 128), rev(D_SSD, 0),
                                    pl.BlockSpec((1, 1, D_SSD, SSD_N), lambda b, c: (b, nc - 1 - c, 0, 0)),
                                    _const(1, 128), _const(1, 128), _const(128, 1), _const(128, 1), _const(D_SSD, 1)],
                          out_specs=[rev(D_XBC, 0), rev(128, 0), acc_spec, acc_spec, acc_spec],
                          out_shape=[jax.ShapeDtypeStruct((B, S, D_XBC), f32), jax.ShapeDtypeStruct((B, S, 128), f32),
                                     acc_shape, acc_shape, acc_shape],
                          scratch_shapes=[pltpu.VMEM((D_SSD, SSD_N), f32), pltpu.VMEM((D_SSD, CHUNK), f32),
                                          pltpu.VMEM((128, 128), f32), pltpu.VMEM((128, 128), f32), pltpu.VMEM((128, 128), f32),
                                          pltpu.VMEM((128, 128), f32)],
                          compiler_params=_cparams(2))(xc3, xc3, p3, dy3, states, bias_row, alog_row, bias_col, alog_col, dcol)


ATT_SCALE = float(QK) ** -0.5
NEG = -1e30


def _att_block(S):
    return _pick(S, (512, 256, 128))


def attn_fwd(q3, k3, v3):
    B, S, _ = q3.shape
    bq = _att_block(S)
    nq = S // bq

    def body(q_ref, k_ref, v_ref, o_ref, lse_ref, m_s, l_s, acc):
        i, j = pl.program_id(2), pl.program_id(3)

        @pl.when(j == 0)
        def _():
            m_s[...] = jnp.full_like(m_s, NEG)
            l_s[...] = jnp.zeros_like(l_s)
            acc[...] = jnp.zeros_like(acc)

        @pl.when(j <= i)
        def _():
            s = lax.dot_general(q_ref[0], k_ref[0], _NT, preferred_element_type=f32) * ATT_SCALE
            r = lax.broadcasted_iota(jnp.int32, (bq, bq), 0)
            c = lax.broadcasted_iota(jnp.int32, (bq, bq), 1)
            s = jnp.where((j < i) | (r >= c), s, NEG)
            m_new = jnp.maximum(m_s[...], jnp.max(s, axis=-1, keepdims=True))
            alpha = jnp.exp(m_s[...] - m_new)
            p = jnp.exp(s - m_new)
            l_s[...] = alpha * l_s[...] + jnp.sum(p, axis=-1, keepdims=True)
            acc[...] = alpha * acc[...] + jnp.dot(p.astype(bf16), v_ref[0], preferred_element_type=f32)
            m_s[...] = m_new

        @pl.when(j == i)
        def _():
            o_ref[0] = acc[...] / l_s[...]
            lse_ref[0, 0] = m_s[...] + jnp.log(l_s[...])

    return pl.pallas_call(body, name="attn_fwd", grid=(B, MLA_H, nq, nq),
                          in_specs=[pl.BlockSpec((1, bq, 256), lambda b, h, i, j: (b, i, h)),
                                    pl.BlockSpec((1, bq, 256), lambda b, h, i, j: (b, jnp.minimum(j, i), h)),
                                    pl.BlockSpec((1, bq, VD), lambda b, h, i, j: (b, jnp.minimum(j, i), h))],
                          out_specs=[pl.BlockSpec((1, bq, VD), lambda b, h, i, j: (b, i, h)),
                                     pl.BlockSpec((1, 1, bq, 1), lambda b, h, i, j: (b, h, i, 0))],
                          out_shape=[jax.ShapeDtypeStruct((B, S, MLA_H * VD), f32), jax.ShapeDtypeStruct((B, MLA_H, S, 1), f32)],
                          scratch_shapes=[pltpu.VMEM((bq, 1), f32), pltpu.VMEM((bq, 1), f32), pltpu.VMEM((bq, VD), f32)],
                          compiler_params=_cparams(4))(q3, k3, v3)


def attn_bwd_dq(q3, k3, v3, o3, do3, lse):
    B, S, _ = q3.shape
    bq = _att_block(S)
    nq = S // bq

    def body(q_ref, k_ref, v_ref, o_ref, do_ref, lse_ref, dq_ref, dl_ref, acc, dl_s):
        i, j = pl.program_id(2), pl.program_id(3)

        @pl.when(j == 0)
        def _():
            acc[...] = jnp.zeros_like(acc)
            dl_s[...] = jnp.sum(o_ref[0] * do_ref[0], axis=-1, keepdims=True)

        @pl.when(j <= i)
        def _():
            s = lax.dot_general(q_ref[0], k_ref[0], _NT, preferred_element_type=f32) * ATT_SCALE
            r = lax.broadcasted_iota(jnp.int32, (bq, bq), 0)
            c = lax.broadcasted_iota(jnp.int32, (bq, bq), 1)
            p = jnp.where((j < i) | (r >= c), jnp.exp(s - lse_ref[0, 0]), 0.0)
            dp = lax.dot_general(do_ref[0].astype(bf16), v_ref[0], _NT, preferred_element_type=f32)
            ds = p * (dp - dl_s[...]) * ATT_SCALE
            acc[...] += jnp.dot(ds.astype(bf16), k_ref[0], preferred_element_type=f32)

        @pl.when(j == i)
        def _():
            dq_ref[0] = acc[...]
            dl_ref[0, 0] = dl_s[...]

    qspec = pl.BlockSpec((1, bq, 256), lambda b, h, i, j: (b, i, h))
    ospec = pl.BlockSpec((1, bq, VD), lambda b, h, i, j: (b, i, h))
    cspec = pl.BlockSpec((1, 1, bq, 1), lambda b, h, i, j: (b, h, i, 0))
    return pl.pallas_call(body, name="attn_bwd_dq", grid=(B, MLA_H, nq, nq),
                          in_specs=[qspec, pl.BlockSpec((1, bq, 256), lambda b, h, i, j: (b, jnp.minimum(j, i), h)),
                                    pl.BlockSpec((1, bq, VD), lambda b, h, i, j: (b, jnp.minimum(j, i), h)), ospec, ospec, cspec],
                          out_specs=[qspec, cspec],
                          out_shape=[jax.ShapeDtypeStruct((B, S, MLA_H * 256), f32), jax.ShapeDtypeStruct((B, MLA_H, S, 1), f32)],
                          scratch_shapes=[pltpu.VMEM((bq, 256), f32), pltpu.VMEM((bq, 1), f32)],
                          compiler_params=_cparams(4))(q3, k3, v3, o3, do3, lse)


def attn_bwd_dkv(q3, k3, v3, do3, lse_row, delta_row):
    B, S, _ = q3.shape
    bq = _att_block(S)
    nq = S // bq

    def body(q_ref, k_ref, v_ref, do_ref, lse_ref, dl_ref, dk_ref, dv_ref, dk_acc, dv_acc):
        j, i = pl.program_id(2), pl.program_id(3)

        @pl.when(i == 0)
        def _():
            dk_acc[...] = jnp.zeros_like(dk_acc)
            dv_acc[...] = jnp.zeros_like(dv_acc)

        @pl.when(i >= j)
        def _():
            q = q_ref[0]
            do = do_ref[0].astype(bf16)
            st = lax.dot_general(k_ref[0], q, _NT, preferred_element_type=f32) * ATT_SCALE
            r = lax.broadcasted_iota(jnp.int32, (bq, bq), 0)
            c = lax.broadcasted_iota(jnp.int32, (bq, bq), 1)
            pt = jnp.where((i > j) | (c >= r), jnp.exp(st - lse_ref[0, 0]), 0.0)
            dv_acc[...] += jnp.dot(pt.astype(bf16), do, preferred_element_type=f32)
            dpt = lax.dot_general(v_ref[0], do, _NT, preferred_element_type=f32)
            dst = pt * (dpt - dl_ref[0, 0]) * ATT_SCALE
            dk_acc[...] += jnp.dot(dst.astype(bf16), q, preferred_element_type=f32)

        @pl.when(i == nq - 1)
        def _():
            dk_ref[0] = dk_acc[...]
            dv_ref[0] = dv_acc[...]

    kspec = pl.BlockSpec((1, bq, 256), lambda b, h, j, i: (b, j, h))
    vspec = pl.BlockSpec((1, bq, VD), lambda b, h, j, i: (b, j, h))
    rspec = pl.BlockSpec((1, 1, 1, bq), lambda b, h, j, i: (b, h, 0, jnp.maximum(i, j)))
    return pl.pallas_call(body, name="attn_bwd_dkv", grid=(B, MLA_H, nq, nq),
                          in_specs=[pl.BlockSpec((1, bq, 256), lambda b, h, j, i: (b, jnp.maximum(i, j), h)), kspec, vspec,
                                    pl.BlockSpec((1, bq, VD), lambda b, h, j, i: (b, jnp.maximum(i, j), h)), rspec, rspec],
                          out_specs=[kspec, vspec],
                          out_shape=[jax.ShapeDtypeStruct((B, S, MLA_H * 256), f32), jax.ShapeDtypeStruct((B, S, MLA_H * VD), f32)],
                          scratch_shapes=[pltpu.VMEM((bq, 256), f32), pltpu.VMEM((bq, VD), f32)],
                          compiler_params=_cparams(4))(q3, k3, v3, do3, lse_row, delta_row)


def ada_fwd(c_all, w, b):
    n = w.shape[1]

    def body(c_ref, w_ref, b_ref, o_ref):
        o_ref[...] = jnp.dot(_silu(c_ref[...]).astype(bf16), w_ref[...].astype(bf16), preferred_element_type=f32) + b_ref[...]

    return pl.pallas_call(body, name="ada_fwd", out_shape=jax.ShapeDtypeStruct((c_all.shape[0], n), f32),
                          compiler_params=pltpu.CompilerParams(vmem_limit_bytes=VMEM_LIMIT))(c_all, w, b)


def ada_bwd(c_all, dmod):
    n = dmod.shape[1]

    def body(c_ref, d_ref, o_ref):
        o_ref[...] = lax.dot_general(_silu(c_ref[...]).astype(bf16), d_ref[...].astype(bf16), _TN, preferred_element_type=f32)

    return pl.pallas_call(body, name="ada_bwd", out_shape=jax.ShapeDtypeStruct((c_all.shape[1], n), f32),
                          compiler_params=pltpu.CompilerParams(vmem_limit_bytes=VMEM_LIMIT))(c_all, dmod)


def sum_leading(x, name):
    n, R, _ = x.shape
    tr = _pick(R, (512, 256, 128, 64, 32, 16, 8))

    def body(x_ref, o_ref):
        acc = x_ref[0].astype(f32)
        for k in range(1, n):
            acc = acc + x_ref[k].astype(f32)
        o_ref[...] = acc

    return pl.pallas_call(body, name=name, grid=(R // tr,), in_specs=[pl.BlockSpec((n, tr, 128), lambda i: (0, i, 0))],
                          out_specs=pl.BlockSpec((tr, 128), lambda i: (i, 0)), out_shape=jax.ShapeDtypeStruct((R, 128), f32),
                          compiler_params=_cparams(1))(x)


def adamw(w, g, m, v):
    R = w.shape[0]
    tr = _pick(R, (512, 256, 128, 64, 32, 16, 8))

    def body(w_ref, g_ref, m_ref, v_ref, d_ref, mo_ref, vo_ref):
        gv = g_ref[...]
        mn = ADAM_B1 * m_ref[...] + (1.0 - ADAM_B1) * gv
        vn = ADAM_B2 * v_ref[...] + (1.0 - ADAM_B2) * jnp.square(gv)
        m_hat = mn / (1.0 - ADAM_B1 ** ADAM_STEP)
        v_hat = vn / (1.0 - ADAM_B2 ** ADAM_STEP)
        d_ref[...] = -ADAM_LR * (m_hat / (jnp.sqrt(v_hat) + ADAM_EPS) + ADAM_WD * w_ref[...])
        mo_ref[...] = mn
        vo_ref[...] = vn

    spec = pl.BlockSpec((tr, 128), lambda i: (i, 0))
    shp = jax.ShapeDtypeStruct((R, 128), f32)
    return pl.pallas_call(body, name="adamw", grid=(R // tr,), in_specs=[spec] * 4, out_specs=[spec] * 3,
                          out_shape=[shp] * 3, compiler_params=_cparams(1))(w, g, m, v)


def exchange(arrays, scatter, name):
    n = len(arrays)
    blocks = [a.shape[1:] if scatter else a.shape for a in arrays]

    def body(*refs):
        ins, outs = refs[:n], refs[n:2 * n]
        send_sems, recv_sems, local_sems = refs[2 * n:]
        x, y, c = lax.axis_index("x"), lax.axis_index("y"), lax.axis_index("c")
        me = 4 * x + 2 * y + c
        peers = []
        for k in range(1, N_DEV):
            px = 1 - x if k & 4 else x
            py = 1 - y if k & 2 else y
            pc = 1 - c if k & 1 else c
            peers.append(((px, py, pc), 4 * px + 2 * py + pc))
        locals_, sends = [], []
        for a in range(n):
            src = ins[a].at[me] if scatter else ins[a]
            lc = pltpu.make_async_copy(src, outs[a].at[me], local_sems.at[a])
            lc.start()
            locals_.append(lc)
            for k, (peer, pid) in enumerate(peers):
                cp = pltpu.make_async_remote_copy(src_ref=ins[a].at[pid] if scatter else ins[a], dst_ref=outs[a].at[me],
                                                  send_sem=send_sems.at[a, k], recv_sem=recv_sems.at[a, k],
                                                  device_id=peer, device_id_type=pl.DeviceIdType.MESH)
                cp.start()
                sends.append(cp)
        for a in range(n):
            for k, (peer, pid) in enumerate(peers):
                pltpu.make_async_remote_copy(src_ref=ins[a].at[pid] if scatter else ins[a], dst_ref=outs[a].at[pid],
                                             send_sem=send_sems.at[a, k], recv_sem=recv_sems.at[a, k],
                                             device_id=peer, device_id_type=pl.DeviceIdType.MESH).wait_recv()
        for cp in sends:
            cp.wait_send()
        for lc in locals_:
            lc.wait()

    hbm = pl.BlockSpec(memory_space=pltpu.HBM)
    return pl.pallas_call(body, name=name, in_specs=[hbm] * n, out_specs=[hbm] * n,
                          out_shape=[jax.ShapeDtypeStruct((N_DEV,) + tuple(b), a.dtype) for a, b in zip(arrays, blocks)],
                          scratch_shapes=[pltpu.SemaphoreType.DMA((n, N_DEV - 1)), pltpu.SemaphoreType.DMA((n, N_DEV - 1)),
                                          pltpu.SemaphoreType.DMA((n,))],
                          )(*arrays)


BIG = (("w_in", "col"), ("conv_w", "col"), ("w_uq", "col"), ("w_ukv", "col"), ("w_out", "row"), ("w_up", "col"),
       ("conv_ff_w", "col"), ("w_down", "row"))
SMALL = ("b_ada", "norm_mix", "conv_b", "dt_bias", "a_log", "d_skip", "ssd_norm", "q_norm", "kv_norm", "attn_norm",
         "norm_mlp", "conv_ff_b", "final_norm")
CONVS = ("conv_w", "conv_ff_w")
PACK_ALIGN = 2048


def _padded(n):
    return -(-n // PACK_ALIGN) * PACK_ALIGN


def _flat_pad(a):
    f = a.reshape(-1)
    return jnp.pad(f, (0, _padded(f.shape[0]) - f.shape[0]))


PACK_ROWS = 512


def pack(arrs):
    f = jnp.concatenate([_flat_pad(a) for a in arrs])
    n = PACK_ROWS * 128
    return jnp.pad(f, (0, -(-f.shape[0] // n) * n - f.shape[0])).reshape(-1, 128)


def pack8(arrs):
    parts = []
    for a in arrs:
        f = a.reshape(N_DEV, -1)
        parts.append(jnp.pad(f, ((0, 0), (0, _padded(f.shape[1]) - f.shape[1]))))
    f = jnp.concatenate(parts, axis=1)
    n = PACK_ROWS * 128
    return jnp.pad(f, ((0, 0), (0, -(-f.shape[1] // n) * n - f.shape[1]))).reshape(N_DEV, -1, 128)


def unpack(flat, shapes):
    f = flat.reshape(-1)
    out, off = [], 0
    for s in shapes:
        n = int(np.prod(s))
        out.append(f[off:off + n].reshape(s))
        off += _padded(n)
    return out


def shards_to_full(g, shard_shape, kind):
    L, a, b = shard_shape
    if kind == "col":
        return g.transpose(1, 2, 0, 3).reshape(L, a, N_DEV * b)
    return g.transpose(1, 0, 2, 3).reshape(L, N_DEV * a, b)


def full_to_shards(full, shard_shape, kind):
    L, a, b = shard_shape
    if kind == "col":
        return full.reshape(L, a, N_DEV, b).transpose(2, 0, 1, 3)
    return full.reshape(L, N_DEV, a, b).transpose(1, 0, 2, 3)


def w_in_layout(w):
    z = lambda n: jnp.zeros(w.shape[:-1] + (n,), w.dtype)
    return jnp.concatenate([w[..., :2560], w[..., 2576:2960], z(128), w[..., 2960:3216], w[..., 3216:3280],
                            w[..., 2560:2576], z(48)], axis=-1)


def w_in_unlayout(g):
    return jnp.concatenate([g[..., :2560], g[..., 3392:3408], g[..., 2560:2944], g[..., 3072:3328], g[..., 3328:3392]], axis=-1)


def w_uq_layout(w):
    return jnp.pad(w.reshape(Q_RANK, MLA_H, QK), ((0, 0), (0, 0), (0, 256 - QK))).reshape(Q_RANK, MLA_H * 256)


def w_uq_unlayout(g):
    return g.reshape(Q_RANK, MLA_H, 256)[:, :, :QK].reshape(Q_RANK, MLA_H * QK)


def w_ukv_layout(w):
    return w.reshape(KV_RANK, MLA_H, 2, 128).transpose(0, 2, 1, 3).reshape(KV_RANK, 2 * MLA_H * 128)


def w_ukv_unlayout(g):
    return g.reshape(KV_RANK, 2, MLA_H, 128).transpose(0, 2, 1, 3).reshape(KV_RANK, 2 * MLA_H * 128)


def _head_row(v):
    return jnp.zeros((1, 128), f32).at[0, DT_LANE:DT_LANE + SSD_HEADS].set(v)


def layer_fwd(x3, mod, W, P, l, cosf, sinf):
    B, S, _ = x3.shape
    T = B * S
    sv = {}
    h = normmod_fwd(x3, mod, P["norm_mix"][l][None], 0, 1)
    p = mm(h.reshape(T, D), W["w_in"][l], "nn", "mm_in")
    p3 = p.reshape(B, S, IN_COLS)
    bias_row, alog_row = _head_row(P["dt_bias"][l]), _head_row(P["a_log"][l])
    dcol = jnp.repeat(P["d_skip"][l], SSD_HD)[:, None]
    xc3 = conv_ssd_fwd(p3, P["conv_w"][l], P["conv_b"][l][None])
    yc3, states = ssd_fwd(xc3, p3, bias_row, alog_row, dcol)
    y_ssd = ssd_out_fwd(yc3, p3, P["ssd_norm"][l][None])
    cqn = rms_fwd(p3, 512, OFF_CQ // 512, Q_RANK, P["q_norm"][l][None], "rms_q_fwd")
    ckvn = rms_fwd(p3, KV_RANK, OFF_CKV // KV_RANK, KV_RANK, P["kv_norm"][l][None], "rms_kv_fwd")
    qraw = mm(cqn.reshape(T, Q_RANK), W["w_uq"][l], "nn", "mm_uq")
    kvraw = mm(ckvn.reshape(T, KV_RANK), W["w_ukv"][l], "nn", "mm_ukv")
    q3 = qprep_fwd(qraw.reshape(B, S, -1), cosf, sinf)
    k3, v3 = kprep_fwd(kvraw.reshape(B, S, -1), p3, cosf, sinf)
    o3, lse = attn_fwd(q3, k3, v3)
    y_att = rms_fwd(o3, D, 0, D, P["attn_norm"][l][None], "rms_o_fwd")
    cat = jnp.concatenate([y_ssd, y_att], axis=-1).reshape(T, 2 * D)
    x1, y1 = mm(cat, W["w_out"][l], "nn", "mm_out", resid=x3.reshape(T, D), gate=mod[:, 2:3, :], seq=S)
    x13 = x1.reshape(B, S, D)
    h2 = normmod_fwd(x13, mod, P["norm_mlp"][l][None], 3, 4)
    u = mm(h2.reshape(T, D), W["w_up"][l], "nn", "mm_up")
    u3 = u.reshape(B, S, 2 * D_FF)
    a = glu_fwd(u3, P["conv_ff_w"][l], P["conv_ff_b"][l][None])
    x2, y2 = mm(a.reshape(T, D_FF), W["w_down"][l], "nn", "mm_down", resid=x1, gate=mod[:, 5:6, :], seq=S)
    sv.update(x=x3, h=h, p3=p3, xc3=xc3, yc3=yc3, states=states, cqn=cqn, ckvn=ckvn, q3=q3, k3=k3, v3=v3, o3=o3, lse=lse,
              cat=cat, y1=y1, x1=x13, h2=h2, u3=u3, a=a, y2=y2, bias_row=bias_row, alog_row=alog_row, dcol=dcol)
    return x2.reshape(B, S, D), sv


def layer_bwd(dx3, sv, mod, W, P, l, cosf, sinf):
    B, S, _ = dx3.shape
    T = B * S
    g = {}
    dy2, dg2 = gate_bwd(dx3, sv["y2"].reshape(B, S, D), mod, 5)
    dy2 = dy2.reshape(T, D)
    da = mm(dy2, W["w_down"][l], "nt", "mm_down_dx")
    g["w_down"] = mm(sv["a"].reshape(T, D_FF), dy2, "tn", "mm_down_dw")
    dug, duv, dwg, dwv, dbg, dbv = glu_bwd(sv["u3"], da.reshape(B, S, D_FF), P["conv_ff_w"][l], P["conv_ff_b"][l][None])
    g["conv_ff_w"] = jnp.concatenate([dwg, dwv], axis=1)
    g["conv_ff_b"] = jnp.concatenate([dbg, dbv], axis=1)[0]
    du = jnp.concatenate([dug, duv], axis=-1).reshape(T, 2 * D_FF)
    dh2 = mm(du, W["w_up"][l], "nt", "mm_up_dx")
    g["w_up"] = mm(sv["h2"].reshape(T, D), du, "tn", "mm_up_dw")
    dx1, dsh2, dsc2, dnm = normmod_bwd(sv["x1"], dh2.reshape(B, S, D), dx3, mod, P["norm_mlp"][l][None], 4)
    g["norm_mlp"] = dnm[0]
    dy1, dg1 = gate_bwd(dx1, sv["y1"].reshape(B, S, D), mod, 2)
    dy1 = dy1.reshape(T, D)
    dcat = mm(dy1, W["w_out"][l], "nt", "mm_out_dx")
    g["w_out"] = mm(sv["cat"], dy1, "tn", "mm_out_dw")
    dcat3 = dcat.reshape(B, S, 2 * D)
    do3, dan = rms_bwd(sv["o3"], D, 0, D, dcat3, 1, P["attn_norm"][l][None], f32, "rms_o_bwd")
    g["attn_norm"] = dan[0]
    dq3, delta = attn_bwd_dq(sv["q3"], sv["k3"], sv["v3"], sv["o3"], do3, sv["lse"])
    dk3, dv3 = attn_bwd_dkv(sv["q3"], sv["k3"], sv["v3"], do3, sv["lse"].reshape(B, MLA_H, 1, S), delta.reshape(B, MLA_H, 1, S))
    dqraw = qprep_bwd(dq3, cosf, sinf).reshape(T, -1)
    dcqn = mm(dqraw, W["w_uq"][l], "nt", "mm_uq_dx")
    g["w_uq"] = mm(sv["cqn"].reshape(T, Q_RANK), dqraw, "tn", "mm_uq_dw")
    dcq, dqn = rms_bwd(sv["p3"], 512, OFF_CQ // 512, Q_RANK, dcqn.reshape(B, S, Q_RANK), 0, P["q_norm"][l][None], bf16, "rms_q_bwd")
    g["q_norm"] = dqn[0]
    dyc3, dz, dsn = ssd_out_bwd(sv["yc3"], sv["p3"], dcat3, P["ssd_norm"][l][None])
    g["ssd_norm"] = dsn[0]
    bias_col, alog_col = sv["bias_row"].reshape(128, 1), sv["alog_row"].reshape(128, 1)
    dxc3, ddt3, dalog, dd, dbias = ssd_bwd(sv["xc3"], sv["p3"], dyc3, sv["states"], sv["bias_row"], sv["alog_row"],
                                           bias_col, alog_col, sv["dcol"])
    heads = slice(DT_LANE, DT_LANE + SSD_HEADS)
    g["a_log"], g["d_skip"], g["dt_bias"] = dalog[heads, 0], dd[heads, 0], dbias[heads, 0]
    dxbc, dcw, dcb = conv_ssd_bwd(sv["p3"], dxc3, P["conv_w"][l], P["conv_b"][l][None])
    g["conv_w"], g["conv_b"] = dcw, dcb[0]
    dkvraw, dkrdt = kprep_bwd(dk3, dv3, ddt3, cosf, sinf)
    dkvraw = dkvraw.reshape(T, -1)
    dckvn = mm(dkvraw, W["w_ukv"][l], "nt", "mm_ukv_dx")
    g["w_ukv"] = mm(sv["ckvn"].reshape(T, KV_RANK), dkvraw, "tn", "mm_ukv_dw")
    dckv, dkn = rms_bwd(sv["p3"], KV_RANK, OFF_CKV // KV_RANK, KV_RANK, dckvn.reshape(B, S, KV_RANK), 0, P["kv_norm"][l][None],
                        bf16, "rms_kv_bwd")
    g["kv_norm"] = dkn[0]
    dp = jnp.concatenate([dz, dxbc, dcq, jnp.zeros((B, S, 128), bf16), dckv, dkrdt], axis=-1).reshape(T, IN_COLS)
    dh = mm(dp, W["w_in"][l], "nt", "mm_in_dx")
    g["w_in"] = mm(sv["h"].reshape(T, D), dp, "tn", "mm_in_dw")
    dx0, dsh1, dsc1, dnx = normmod_bwd(sv["x"], dh.reshape(B, S, D), dx1, mod, P["norm_mix"][l][None], 1)
    g["norm_mix"] = dnx[0]
    dmod = jnp.concatenate([dsh1, dsc1, dg1, dsh2, dsc2, dg2], axis=1)
    return dx0, dmod, g


def kernel(x, c, positions, w_ada, b_ada, norm_mix, w_in, conv_w, conv_b, dt_bias, a_log, d_skip, ssd_norm, q_norm, w_uq, kv_norm, w_ukv, attn_norm, w_out, norm_mlp, w_up, conv_ff_w, conv_ff_b, w_down, final_norm, loss_target, m_w_ada, m_b_ada, m_norm_mix, m_w_in, m_conv_w, m_conv_b, m_dt_bias, m_a_log, m_d_skip, m_ssd_norm, m_q_norm, m_w_uq, m_kv_norm, m_w_ukv, m_attn_norm, m_w_out, m_norm_mlp, m_w_up, m_conv_ff_w, m_conv_ff_b, m_w_down, m_final_norm, v_w_ada, v_b_ada, v_norm_mix, v_w_in, v_conv_w, v_conv_b, v_dt_bias, v_a_log, v_d_skip, v_ssd_norm, v_q_norm, v_w_uq, v_kv_norm, v_w_ukv, v_attn_norm, v_w_out, v_norm_mlp, v_w_up, v_conv_ff_w, v_conv_ff_b, v_w_down, v_final_norm):
    given = dict(locals())
    B, S, _ = x.shape
    me = 4 * lax.axis_index("x") + 2 * lax.axis_index("y") + lax.axis_index("c")
    P = {n: given[n] for n in SMALL}

    big_shapes = [given[n].shape for n, _ in BIG]
    mats = [(n, kind) for n, kind in BIG if n not in CONVS]
    convs = [(n, kind) for n, kind in BIG if n in CONVS]
    wg, cg, c_all = exchange([pack([given[n].astype(bf16) for n, _ in mats]), pack([given[n] for n, _ in convs]), c],
                             False, "gather_weights")
    full = {}
    for group, gathered in ((mats, wg), (convs, cg)):
        shapes = [given[n].shape for n, _ in group]
        for (n, kind), shp, arr in zip(group, shapes, unpack_gathered(gathered, shapes)):
            full[n] = shards_to_full(arr, shp, kind)
    W = {"w_in": w_in_layout(full["w_in"]), "w_out": full["w_out"], "w_up": full["w_up"], "w_down": full["w_down"],
         "w_uq": jnp.stack([w_uq_layout(full["w_uq"][l]) for l in range(DEPTH)]),
         "w_ukv": jnp.stack([w_ukv_layout(full["w_ukv"][l]) for l in range(DEPTH)])}
    P["conv_w"], P["conv_ff_w"] = full["conv_w"], full["conv_ff_w"]

    n_ada = w_ada.shape[2]
    c_all = c_all.reshape(N_DEV * B, D)
    b_sh = lax.dynamic_slice_in_dim(b_ada, me * n_ada, n_ada, axis=1)
    mod_sh = jnp.stack([ada_fwd(c_all, w_ada[l], b_sh[l][None]) for l in range(DEPTH)])
    (mod_g,) = exchange([mod_sh], False, "gather_mod")
    mod_mine = lax.dynamic_slice_in_dim(mod_g, me * B, B, axis=2)
    mods = mod_mine.transpose(1, 2, 0, 3).reshape(DEPTH, B, 6, D)

    inv_freq = jnp.asarray(1.0 / (ROPE_BASE ** (np.arange(0, ROPE, 2, dtype=np.float32) / ROPE)))
    ang = positions.astype(f32)[..., None] * inv_freq
    zeros = jnp.zeros((B, S, 128 - ROPE), f32)
    cosf = jnp.concatenate([jnp.cos(ang), jnp.cos(ang), zeros], axis=-1)
    sinf = jnp.concatenate([jnp.sin(ang), jnp.sin(ang), zeros], axis=-1)

    xl, saved = x, []
    for l in range(DEPTH):
        xl, sv = layer_fwd(xl, mods[l], W, P, l, cosf, sinf)
        saved.append(sv)
    dxl, d_final, loss_part = final_loss(xl, final_norm[None], loss_target)
    grads, dmods = [None] * DEPTH, [None] * DEPTH
    for l in reversed(range(DEPTH)):
        dxl, dmods[l], grads[l] = layer_bwd(dxl, saved[l], mods[l], W, P, l, cosf, sinf)
    grad_x = dxl

    stack = lambda n: jnp.stack([grads[l][n] for l in range(DEPTH)])
    gfull = {"w_in": w_in_unlayout(stack("w_in")), "conv_w": stack("conv_w"),
             "w_uq": jnp.stack([w_uq_unlayout(grads[l]["w_uq"]) for l in range(DEPTH)]),
             "w_ukv": jnp.stack([w_ukv_unlayout(grads[l]["w_ukv"]) for l in range(DEPTH)]),
             "w_out": stack("w_out"), "w_up": stack("w_up"), "conv_ff_w": stack("conv_ff_w"), "w_down": stack("w_down")}
    send = pack8([full_to_shards(gfull[n], shp, kind) for (n, kind), shp in zip(BIG, big_shapes)])
    (recv,) = exchange([send], True, "scatter_grads")
    g_big = sum_leading(recv, "sum_grads")

    small_names = [n for n in SMALL if n not in ("b_ada", "final_norm")]
    partial = pack([stack(n) for n in small_names] + [d_final[0], loss_part[0]])
    dmod_all = jnp.stack(dmods)
    part_g, dmod_g = exchange([partial, dmod_all], False, "gather_partials")
    small_sum = sum_leading(part_g, "sum_partials")
    small_g = unpack(small_sum, [given[n].shape for n in small_names] + [(D,), (128,)])
    gsmall = dict(zip(small_names + ["final_norm"], small_g[:-1]))
    loss = small_g[-1][0]
    dmod_rows = dmod_g.transpose(0, 2, 1, 3, 4).reshape(N_DEV * B, DEPTH * 6 * D)
    gsmall["b_ada"] = sum_leading(dmod_rows.reshape(N_DEV * B, -1, 128), "sum_b_ada").reshape(DEPTH, 6 * D)
    dmod_cols = dmod_rows.reshape(N_DEV * B, DEPTH, N_DEV, n_ada)
    dmod_sh = lax.dynamic_slice_in_dim(dmod_cols, me, 1, axis=2)[:, :, 0, :]
    g_w_ada = jnp.stack([ada_bwd(c_all, dmod_sh[:, l, :]) for l in range(DEPTH)])

    big_g = unpack(g_big, big_shapes)
    names = [n for n, _ in BIG] + ["w_ada"] + list(SMALL)
    g_list = big_g + [g_w_ada] + [gsmall[n] for n in SMALL]
    shapes = [given[n].shape for n in names]
    delta, new_m, new_v = adamw(pack([given[n] for n in names]), pack(g_list), pack([given["m_" + n] for n in names]),
                                pack([given["v_" + n] for n in names]))
    res = {"grad": dict(zip(names, g_list)), "delta": dict(zip(names, unpack(delta, shapes))),
           "new_m": dict(zip(names, unpack(new_m, shapes))), "new_v": dict(zip(names, unpack(new_v, shapes)))}
    order = ["w_ada", "b_ada", "norm_mix", "w_in", "conv_w", "conv_b", "dt_bias", "a_log", "d_skip", "ssd_norm", "q_norm", "w_uq",
             "kv_norm", "w_ukv", "attn_norm", "w_out", "norm_mlp", "w_up", "conv_ff_w", "conv_ff_b", "w_down", "final_norm"]
    return (loss, grad_x, *[res[k][n] for k in ("grad", "delta", "new_m", "new_v") for n in order])


def unpack_gathered(g, shard_shapes):
    f = g.reshape(N_DEV, -1)
    out, off = [], 0
    for s in shard_shapes:
        n = int(np.prod(s))
        out.append(f[:, off:off + n].reshape((N_DEV,) + tuple(s)))
        off += _padded(n)
    return out
```

```python
import functools

import numpy as np
import jax
import jax.numpy as jnp
from jax import lax
from jax.experimental import pallas as pl
from jax.experimental.pallas import tpu as pltpu

f32, bf16 = jnp.float32, jnp.bfloat16
HIGHEST = lax.Precision.HIGHEST

D = 1024
D_SSD = 1024
SSD_HEADS = 16
SSD_HD = 64
SSD_N = 128
CHUNK = 128
D_XBC = 1536
CONV_K = 4
MLA_H = 8
NOPE = 128
ROPE = 64
VD = 128
QK = NOPE + ROPE
Q_RANK = 384
KV_RANK = 256
D_FF = 2816
FF_K = 3
EPS = 1e-6
ROPE_BASE = 10000.0
DEPTH = 2
ADAM_LR, ADAM_B1, ADAM_B2, ADAM_EPS, ADAM_WD, ADAM_STEP = 0.001, 0.9, 0.999, 1e-08, 0.01, 10

N_DEV = 8
IN_COLS = 3456
OFF_XBC, OFF_CQ, OFF_CKV, OFF_KRDT = 1024, 2560, 3072, 3328
DT_LANE = 64
VMEM_LIMIT = 48 * 1024 * 1024
MM_K_WHOLE = 4096


def _cparams(n_grid):
    return pltpu.CompilerParams(dimension_semantics=("arbitrary",) * n_grid, vmem_limit_bytes=VMEM_LIMIT)


def _pick(n, cands):
    for c in cands:
        if n % c == 0:
            return c
    return n


def _silu(x):
    return x * jax.nn.sigmoid(x)


def _dsilu(x):
    s = jax.nn.sigmoid(x)
    return s * (1.0 + x * (1.0 - s))


def _rowsum(x):
    return jnp.sum(x, axis=0, keepdims=True)


def mm(a, b, mode, name, out_dtype=f32, resid=None, gate=None, seq=None):
    if mode == "nn":
        (M, K), N = a.shape, b.shape[1]
    elif mode == "nt":
        (M, K), N = a.shape, b.shape[0]
    else:
        (K, M), N = a.shape, b.shape[1]
    gated = resid is not None
    tm = _pick(seq if gated else M, (512, 256, 128))
    tn = _pick(N, (512, 384, 256, 128))
    tk = K if K <= MM_K_WHOLE else _pick(K, (2816, 2048, 1024, 512))
    nk = K // tk
    dims = {"nn": ((1,), (0,)), "nt": ((1,), (1,)), "tn": ((0,), (0,))}[mode]

    def body(a_ref, b_ref, *rest):
        if gated:
            r_ref, g_ref, o_ref, y_ref, acc = rest
        else:
            o_ref, acc = rest

        def finish(res):
            if gated:
                y_ref[...] = res
                o_ref[...] = r_ref[...] + g_ref[0] * res
            else:
                o_ref[...] = res.astype(out_dtype)

        prod = lax.dot_general(a_ref[...].astype(bf16), b_ref[...].astype(bf16), (dims, ((), ())), preferred_element_type=f32)
        if nk == 1:
            finish(prod)
        else:
            k = pl.program_id(2)

            @pl.when(k == 0)
            def _():
                acc[...] = prod

            @pl.when(k > 0)
            def _():
                acc[...] += prod

            @pl.when(k == nk - 1)
            def _():
                finish(acc[...])

    a_spec = pl.BlockSpec((tk, tm), lambda i, j, k: (k, i)) if mode == "tn" else pl.BlockSpec((tm, tk), lambda i, j, k: (i, k))
    b_spec = pl.BlockSpec((tn, tk), lambda i, j, k: (j, k)) if mode == "nt" else pl.BlockSpec((tk, tn), lambda i, j, k: (k, j))
    o_spec = pl.BlockSpec((tm, tn), lambda i, j, k: (i, j))
    in_specs, args = [a_spec, b_spec], [a, b]
    out_specs, out_shape = o_spec, jax.ShapeDtypeStruct((M, N), out_dtype)
    if gated:
        per = seq // tm
        in_specs += [o_spec, pl.BlockSpec((1, 1, tn), lambda i, j, k: (i // per, 0, j))]
        args += [resid, gate]
        out_specs = [o_spec, o_spec]
        out_shape = [jax.ShapeDtypeStruct((M, N), f32), jax.ShapeDtypeStruct((M, N), f32)]
    return pl.pallas_call(body, name=name, grid=(M // tm, N // tn, nk), in_specs=in_specs, out_specs=out_specs,
                          out_shape=out_shape, scratch_shapes=[pltpu.VMEM((tm, tn), f32)],
                          compiler_params=_cparams(3))(*args)


def _tok(ts, width, cb=0):
    return pl.BlockSpec((1, ts, width), lambda b, s: (b, s, cb))


def _perb(rows, width):
    return pl.BlockSpec((1, rows, width), lambda b, s: (b, 0, 0))


def _const(rows, width):
    return pl.BlockSpec((rows, width), lambda b, s: (0, 0))


def _row_call(body, name, B, S, ts, in_specs, out_specs, out_shape, scratch=()):
    return pl.pallas_call(body, name=name, grid=(B, S // ts), in_specs=in_specs, out_specs=out_specs,
                          out_shape=out_shape, scratch_shapes=list(scratch), compiler_params=_cparams(2))


def _first():
    return (pl.program_id(0) == 0) & (pl.program_id(1) == 0)


def normmod_fwd(x3, mod, g, i_sh, i_sc):
    B, S, C = x3.shape
    ts = _pick(S, (512, 256, 128))

    def body(x_ref, mod_ref, g_ref, h_ref):
        x = x_ref[0]
        r = lax.rsqrt(jnp.mean(x * x, axis=-1, keepdims=True) + EPS)
        n = x * r * g_ref[...]
        h_ref[0] = (n * (1.0 + mod_ref[0, i_sc:i_sc + 1, :]) + mod_ref[0, i_sh:i_sh + 1, :]).astype(bf16)

    return _row_call(body, "normmod_fwd", B, S, ts, [_tok(ts, C), _perb(6, C), _const(1, C)], _tok(ts, C),
                     jax.ShapeDtypeStruct((B, S, C), bf16))(x3, mod, g)


def normmod_bwd(x3, dh3, resid3, mod, g, i_sc):
    B, S, C = x3.shape
    ts = _pick(S, (512, 256, 128))

    def body(x_ref, dh_ref, r_ref, mod_ref, g_ref, dx_ref, dsh_ref, dsc_ref, dg_ref):
        @pl.when(pl.program_id(1) == 0)
        def _():
            dsh_ref[...] = jnp.zeros_like(dsh_ref)
            dsc_ref[...] = jnp.zeros_like(dsc_ref)

        @pl.when(_first())
        def _():
            dg_ref[...] = jnp.zeros_like(dg_ref)

        x, dh, gv = x_ref[0], dh_ref[0], g_ref[...]
        r = lax.rsqrt(jnp.mean(x * x, axis=-1, keepdims=True) + EPS)
        xh = x * r
        dn = dh * (1.0 + mod_ref[0, i_sc:i_sc + 1, :])
        dsh_ref[0] += _rowsum(dh)
        dsc_ref[0] += _rowsum(dh * xh * gv)
        dg_ref[...] += _rowsum(dn * xh)
        dxh = dn * gv
        dx_ref[0] = r * (dxh - xh * jnp.mean(dxh * xh, axis=-1, keepdims=True)) + r_ref[0]

    return _row_call(body, "normmod_bwd", B, S, ts,
                     [_tok(ts, C), _tok(ts, C), _tok(ts, C), _perb(6, C), _const(1, C)],
                     [_tok(ts, C), _perb(1, C), _perb(1, C), _const(1, C)],
                     [jax.ShapeDtypeStruct((B, S, C), f32), jax.ShapeDtypeStruct((B, 1, C), f32),
                      jax.ShapeDtypeStruct((B, 1, C), f32), jax.ShapeDtypeStruct((1, C), f32)])(x3, dh3, resid3, mod, g)


def gate_bwd(dx3, y3, mod, i_g):
    B, S, C = dx3.shape
    ts = _pick(S, (512, 256, 128))

    def body(dx_ref, y_ref, mod_ref, dy_ref, dgate_ref):
        @pl.when(pl.program_id(1) == 0)
        def _():
            dgate_ref[...] = jnp.zeros_like(dgate_ref)

        dx = dx_ref[0]
        dy_ref[0] = (dx * mod_ref[0, i_g:i_g + 1, :]).astype(bf16)
        dgate_ref[0] += _rowsum(dx * y_ref[0])

    return _row_call(body, "gate_bwd", B, S, ts, [_tok(ts, C), _tok(ts, C), _perb(6, C)], [_tok(ts, C), _perb(1, C)],
                     [jax.ShapeDtypeStruct((B, S, C), bf16), jax.ShapeDtypeStruct((B, 1, C), f32)])(dx3, y3, mod)


def rms_fwd(src3, width, cb, n, g, name):
    B, S, _ = src3.shape
    ts = _pick(S, (512, 256, 128))

    def body(x_ref, g_ref, o_ref):
        x = x_ref[0][:, :n]
        r = lax.rsqrt(jnp.mean(x * x, axis=-1, keepdims=True) + EPS)
        o_ref[0] = (x * r * g_ref[...]).astype(bf16)

    return _row_call(body, name, B, S, ts, [_tok(ts, width, cb), _const(1, n)], _tok(ts, n),
                     jax.ShapeDtypeStruct((B, S, n), bf16))(src3, g)


def rms_bwd(src3, width, cb, n, dout3, dcb, g, out_dtype, name):
    B, S, _ = src3.shape
    ts = _pick(S, (512, 256, 128))

    def body(x_ref, do_ref, g_ref, dx_ref, dg_ref):
        @pl.when(_first())
        def _():
            dg_ref[...] = jnp.zeros_like(dg_ref)

        x = x_ref[0][:, :n]
        do = do_ref[0].astype(f32)
        r = lax.rsqrt(jnp.mean(x * x, axis=-1, keepdims=True) + EPS)
        xh = x * r
        dg_ref[...] += _rowsum(do * xh)
        dxh = do * g_ref[...]
        dx_ref[0] = (r * (dxh - xh * jnp.mean(dxh * xh, axis=-1, keepdims=True))).astype(out_dtype)

    return _row_call(body, name, B, S, ts, [_tok(ts, width, cb), _tok(ts, n, dcb), _const(1, n)],
                     [_tok(ts, n), _const(1, n)],
                     [jax.ShapeDtypeStruct((B, S, n), out_dtype), jax.ShapeDtypeStruct((1, n), f32)])(src3, dout3, g)


def final_loss(x3, g, tgt3):
    B, S, C = x3.shape
    ts = _pick(S, (512, 256, 128))

    def body(x_ref, g_ref, t_ref, dx_ref, dg_ref, loss_ref):
        @pl.when(_first())
        def _():
            dg_ref[...] = jnp.zeros_like(dg_ref)
            loss_ref[...] = jnp.zeros_like(loss_ref)

        x, gv = x_ref[0], g_ref[...]
        r = lax.rsqrt(jnp.mean(x * x, axis=-1, keepdims=True) + EPS)
        xh = x * r
        e = xh * gv - t_ref[0]
        loss_ref[...] += 0.5 * jnp.sum(e * e) / C
        dout = e / C
        dg_ref[...] += _rowsum(dout * xh)
        dxh = dout * gv
        dx_ref[0] = r * (dxh - xh * jnp.mean(dxh * xh, axis=-1, keepdims=True))

    return _row_call(body, "final_loss", B, S, ts, [_tok(ts, C), _const(1, C), _tok(ts, C)],
                     [_tok(ts, C), _const(1, C), _const(1, 128)],
                     [jax.ShapeDtypeStruct((B, S, C), f32), jax.ShapeDtypeStruct((1, C), f32),
                      jax.ShapeDtypeStruct((1, 128), f32)])(x3, g, tgt3)


def ssd_out_fwd(yc3, p3, w):
    B, S, C = yc3.shape
    ts = _pick(S, (512, 256, 128))
    half = C // 2

    def body(y_ref, z_ref, w_ref, o_ref):
        y = y_ref[0] * _silu(z_ref[0])
        for lo in (0, half):
            yg = y[:, lo:lo + half]
            r = lax.rsqrt(jnp.mean(yg * yg, axis=-1, keepdims=True) + EPS)
            o_ref[0, :, lo:lo + half] = (yg * r * w_ref[:, lo:lo + half]).astype(bf16)

    return _row_call(body, "ssd_out_fwd", B, S, ts, [_tok(ts, C), _tok(ts, C, 0), _const(1, C)], _tok(ts, C),
                     jax.ShapeDtypeStruct((B, S, C), bf16))(yc3, p3, w)


def ssd_out_bwd(yc3, p3, dcat3, w):
    B, S, C = yc3.shape
    ts = _pick(S, (512, 256, 128))
    half = C // 2

    def body(y_ref, z_ref, do_ref, w_ref, dyc_ref, dz_ref, dw_ref):
        @pl.when(_first())
        def _():
            dw_ref[...] = jnp.zeros_like(dw_ref)

        yc, z, do = y_ref[0], z_ref[0], do_ref[0]
        sz = _silu(z)
        y = yc * sz
        for lo in (0, half):
            sl = slice(lo, lo + half)
            yg, dog, wg = y[:, sl], do[:, sl], w_ref[:, sl]
            r = lax.rsqrt(jnp.mean(yg * yg, axis=-1, keepdims=True) + EPS)
            yh = yg * r
            dw_ref[:, sl] += _rowsum(dog * yh)
            dyh = dog * wg
            dy = r * (dyh - yh * jnp.mean(dyh * yh, axis=-1, keepdims=True))
            dyc_ref[0, :, sl] = dy * sz[:, sl]
            dz_ref[0, :, sl] = (dy * yc[:, sl] * _dsilu(z[:, sl])).astype(bf16)

    return _row_call(body, "ssd_out_bwd", B, S, ts, [_tok(ts, C), _tok(ts, C, 0), _tok(ts, C, 0), _const(1, C)],
                     [_tok(ts, C), _tok(ts, C), _const(1, C)],
                     [jax.ShapeDtypeStruct((B, S, C), f32), jax.ShapeDtypeStruct((B, S, C), bf16),
                      jax.ShapeDtypeStruct((1, C), f32)])(yc3, p3, dcat3, w)


def _rot(t):
    lane = lax.broadcasted_iota(jnp.int32, t.shape, 1)
    return jnp.where(lane < ROPE // 2, -pltpu.roll(t, 128 - ROPE // 2, 1), pltpu.roll(t, ROPE // 2, 1))


def _rope(t, cosf, sinf):
    return t * cosf + _rot(t) * sinf


def _rope_t(d, cosf, sinf):
    return d * cosf - _rot(d * sinf)


def qprep_fwd(qraw3, cosf, sinf):
    B, S, W = qraw3.shape
    ts = _pick(S, (512, 256, 128))

    def body(q_ref, c_ref, s_ref, o_ref):
        c, s = c_ref[0], s_ref[0]
        for h in range(MLA_H):
            o_ref[0, :, h * 256:h * 256 + 128] = q_ref[0, :, h * 256:h * 256 + 128].astype(bf16)
            o_ref[0, :, h * 256 + 128:(h + 1) * 256] = _rope(q_ref[0, :, h * 256 + 128:(h + 1) * 256], c, s).astype(bf16)

    return _row_call(body, "qprep_fwd", B, S, ts, [_tok(ts, W), _tok(ts, 128), _tok(ts, 128)], _tok(ts, W),
                     jax.ShapeDtypeStruct((B, S, W), bf16))(qraw3, cosf, sinf)


def qprep_bwd(dq3, cosf, sinf):
    B, S, W = dq3.shape
    ts = _pick(S, (512, 256, 128))

    def body(d_ref, c_ref, s_ref, o_ref):
        c, s = c_ref[0], s_ref[0]
        for h in range(MLA_H):
            o_ref[0, :, h * 256:h * 256 + 128] = d_ref[0, :, h * 256:h * 256 + 128].astype(bf16)
            o_ref[0, :, h * 256 + 128:(h + 1) * 256] = _rope_t(d_ref[0, :, h * 256 + 128:(h + 1) * 256], c, s).astype(bf16)

    return _row_call(body, "qprep_bwd", B, S, ts, [_tok(ts, W), _tok(ts, 128), _tok(ts, 128)], _tok(ts, W),
                     jax.ShapeDtypeStruct((B, S, W), bf16))(dq3, cosf, sinf)


def kprep_fwd(kv3, p3, cosf, sinf):
    B, S, _ = kv3.shape
    ts = _pick(S, (512, 256, 128))
    Wn = MLA_H * NOPE

    def body(k_ref, v_ref, kr_ref, c_ref, s_ref, ko_ref, vo_ref):
        lane = lax.broadcasted_iota(jnp.int32, (1, 128), 1)
        kr = jnp.where(lane < ROPE, kr_ref[0], 0.0)
        kr = _rope(kr, c_ref[0], s_ref[0]).astype(bf16)
        for h in range(MLA_H):
            ko_ref[0, :, h * 256:h * 256 + 128] = k_ref[0, :, h * 128:(h + 1) * 128].astype(bf16)
            ko_ref[0, :, h * 256 + 128:(h + 1) * 256] = kr
        vo_ref[0] = v_ref[0].astype(bf16)

    return _row_call(body, "kprep_fwd", B, S, ts,
                     [_tok(ts, Wn, 0), _tok(ts, Wn, 1), _tok(ts, 128, OFF_KRDT // 128), _tok(ts, 128), _tok(ts, 128)],
                     [_tok(ts, 2 * Wn), _tok(ts, Wn)],
                     [jax.ShapeDtypeStruct((B, S, 2 * Wn), bf16), jax.ShapeDtypeStruct((B, S, Wn), bf16)])(kv3, kv3, p3, cosf, sinf)


def kprep_bwd(dk3, dv3, ddt3, cosf, sinf):
    B, S, _ = dk3.shape
    ts = _pick(S, (512, 256, 128))
    Wn = MLA_H * NOPE

    def body(dk_ref, dv_ref, ddt_ref, c_ref, s_ref, o_ref, kr_ref):
        acc = jnp.zeros((ts, 128), f32)
        for h in range(MLA_H):
            o_ref[0, :, h * 128:(h + 1) * 128] = dk_ref[0, :, h * 256:h * 256 + 128].astype(bf16)
            acc = acc + dk_ref[0, :, h * 256 + 128:(h + 1) * 256]
        o_ref[0, :, Wn:] = dv_ref[0].astype(bf16)
        lane = lax.broadcasted_iota(jnp.int32, (1, 128), 1)
        dkr = _rope_t(acc, c_ref[0], s_ref[0])
        kr_ref[0] = jnp.where(lane < ROPE, dkr, ddt_ref[0]).astype(bf16)

    return _row_call(body, "kprep_bwd", B, S, ts,
                     [_tok(ts, 2 * Wn), _tok(ts, Wn), _tok(ts, 128), _tok(ts, 128), _tok(ts, 128)],
                     [_tok(ts, 2 * Wn), _tok(ts, 128)],
                     [jax.ShapeDtypeStruct((B, S, 2 * Wn), bf16), jax.ShapeDtypeStruct((B, S, 128), bf16)])(dk3, dv3, ddt3, cosf, sinf)


def _shift_down(u, j):
    if j == 0:
        return u
    row = lax.broadcasted_iota(jnp.int32, u.shape, 0)
    return jnp.where(row < j, 0.0, pltpu.roll(u, j, 0))


def _shift_up(u, j):
    if j == 0:
        return u
    n = u.shape[0]
    row = lax.broadcasted_iota(jnp.int32, u.shape, 0)
    return jnp.where(row >= n - j, 0.0, pltpu.roll(u, n - j, 0))


def _conv(u, w, b, K):
    out = b
    for j in range(K):
        out = out + w[K - 1 - j:K - j, :] * _shift_down(u, j)
    return out


def _conv_bwd(u, du, w, K):
    dins = w[K - 1:K, :] * du
    dws = [None] * K
    dws[K - 1] = _rowsum(du * u)
    for j in range(1, K):
        dins = dins + w[K - 1 - j:K - j, :] * _shift_up(du, j)
        dws[K - 1 - j] = _rowsum(du * _shift_down(u, j))
    return dins, dws


CW = 256


def conv_ssd_fwd(p3, w, b):
    B, S, _ = p3.shape
    nb = D_XBC // CW

    def body(u_ref, w_ref, b_ref, o_ref):
        o_ref[0] = _silu(_conv(u_ref[0], w_ref[...], b_ref[...], CONV_K))

    return pl.pallas_call(body, name="conv_ssd_fwd", grid=(B, nb),
                          in_specs=[pl.BlockSpec((1, S, CW), lambda b, j: (b, 0, OFF_XBC // CW + j)),
                                    pl.BlockSpec((CONV_K, CW), lambda b, j: (0, j)),
                                    pl.BlockSpec((1, CW), lambda b, j: (0, j))],
                          out_specs=pl.BlockSpec((1, S, CW), lambda b, j: (b, 0, j)),
                          out_shape=jax.ShapeDtypeStruct((B, S, D_XBC), f32), compiler_params=_cparams(2))(p3, w, b)


def conv_ssd_bwd(p3, dxc3, w, b):
    B, S, _ = p3.shape
    nb = D_XBC // CW

    def body(u_ref, d_ref, w_ref, b_ref, du_ref, dw_ref, db_ref):
        @pl.when(pl.program_id(1) == 0)
        def _():
            dw_ref[...] = jnp.zeros_like(dw_ref)
            db_ref[...] = jnp.zeros_like(db_ref)

        u, wv = u_ref[0], w_ref[...]
        dpre = d_ref[0] * _dsilu(_conv(u, wv, b_ref[...], CONV_K))
        dins, dws = _conv_bwd(u, dpre, wv, CONV_K)
        du_ref[0] = dins.astype(bf16)
        for k in range(CONV_K):
            dw_ref[k:k + 1, :] += dws[k]
        db_ref[...] += _rowsum(dpre)

    return pl.pallas_call(body, name="conv_ssd_bwd", grid=(nb, B),
                          in_specs=[pl.BlockSpec((1, S, CW), lambda j, b: (b, 0, OFF_XBC // CW + j)),
                                    pl.BlockSpec((1, S, CW), lambda j, b: (b, 0, j)),
                                    pl.BlockSpec((CONV_K, CW), lambda j, b: (0, j)),
                                    pl.BlockSpec((1, CW), lambda j, b: (0, j))],
                          out_specs=[pl.BlockSpec((1, S, CW), lambda j, b: (b, 0, j)),
                                     pl.BlockSpec((CONV_K, CW), lambda j, b: (0, j)),
                                     pl.BlockSpec((1, CW), lambda j, b: (0, j))],
                          out_shape=[jax.ShapeDtypeStruct((B, S, D_XBC), bf16), jax.ShapeDtypeStruct((CONV_K, D_XBC), f32),
                                     jax.ShapeDtypeStruct((1, D_XBC), f32)], compiler_params=_cparams(2))(p3, dxc3, w, b)


def glu_fwd(u3, w, b):
    B, S, _ = u3.shape
    nb = D_FF // CW

    def body(ug_ref, uv_ref, wg_ref, wv_ref, bg_ref, bv_ref, o_ref):
        g = _conv(ug_ref[0], wg_ref[...], bg_ref[...], FF_K)
        v = _conv(uv_ref[0], wv_ref[...], bv_ref[...], FF_K)
        o_ref[0] = (_silu(g) * v).astype(bf16)

    def blk(off):
        return pl.BlockSpec((1, S, CW), lambda b, j: (b, 0, off + j))

    def par(rows, off):
        return pl.BlockSpec((rows, CW), lambda b, j: (0, off + j))

    return pl.pallas_call(body, name="glu_fwd", grid=(B, nb),
                          in_specs=[blk(0), blk(nb), par(FF_K, 0), par(FF_K, nb), par(1, 0), par(1, nb)],
                          out_specs=blk(0), out_shape=jax.ShapeDtypeStruct((B, S, D_FF), bf16),
                          compiler_params=_cparams(2))(u3, u3, w, w, b, b)


def glu_bwd(u3, da3, w, b):
    B, S, _ = u3.shape
    nb = D_FF // CW

    def body(ug_ref, uv_ref, da_ref, wg_ref, wv_ref, bg_ref, bv_ref, dug_ref, duv_ref, dwg_ref, dwv_ref, dbg_ref, dbv_ref):
        @pl.when(pl.program_id(1) == 0)
        def _():
            for r in (dwg_ref, dwv_ref, dbg_ref, dbv_ref):
                r[...] = jnp.zeros_like(r)

        ug, uv, da, wg, wv = ug_ref[0], uv_ref[0], da_ref[0], wg_ref[...], wv_ref[...]
        g = _conv(ug, wg, bg_ref[...], FF_K)
        v = _conv(uv, wv, bv_ref[...], FF_K)
        dg = da * v * _dsilu(g)
        dv = da * _silu(g)
        ding, dwsg = _conv_bwd(ug, dg, wg, FF_K)
        dinv, dwsv = _conv_bwd(uv, dv, wv, FF_K)
        dug_ref[0] = ding.astype(bf16)
        duv_ref[0] = dinv.astype(bf16)
        for k in range(FF_K):
            dwg_ref[k:k + 1, :] += dwsg[k]
            dwv_ref[k:k + 1, :] += dwsv[k]
        dbg_ref[...] += _rowsum(dg)
        dbv_ref[...] += _rowsum(dv)

    def blk(off):
        return pl.BlockSpec((1, S, CW), lambda j, b: (b, 0, off + j))

    def par(rows, off):
        return pl.BlockSpec((rows, CW), lambda j, b: (0, off + j))

    return pl.pallas_call(body, name="glu_bwd", grid=(nb, B),
                          in_specs=[blk(0), blk(nb), blk(0), par(FF_K, 0), par(FF_K, nb), par(1, 0), par(1, nb)],
                          out_specs=[blk(0), blk(0), par(FF_K, 0), par(FF_K, 0), par(1, 0), par(1, 0)],
                          out_shape=[jax.ShapeDtypeStruct((B, S, D_FF), bf16), jax.ShapeDtypeStruct((B, S, D_FF), bf16),
                                     jax.ShapeDtypeStruct((FF_K, D_FF), f32), jax.ShapeDtypeStruct((FF_K, D_FF), f32),
                                     jax.ShapeDtypeStruct((1, D_FF), f32), jax.ShapeDtypeStruct((1, D_FF), f32)],
                          compiler_params=_cparams(2))(u3, u3, da3, w, w, b, b)


def _ssd_decay(dtb, bias_row, alog_row):
    lane = lax.broadcasted_iota(jnp.int32, (1, 128), 1)
    hmask = (lane >= DT_LANE) & (lane < DT_LANE + SSD_HEADS)
    dt = jnp.where(hmask, jax.nn.softplus(dtb + bias_row), 0.0)
    a = dt * jnp.where(hmask, -jnp.exp(alog_row), 0.0)
    r = lax.broadcasted_iota(jnp.int32, (CHUNK, CHUNK), 0)
    c = lax.broadcasted_iota(jnp.int32, (CHUNK, CHUNK), 1)
    cs = jnp.dot((r >= c).astype(f32), a, precision=HIGHEST, preferred_element_type=f32)
    return dt, cs


def _expand(xt):
    return jnp.concatenate([jnp.broadcast_to(xt[DT_LANE + h:DT_LANE + h + 1, :], (SSD_HD, xt.shape[1]))
                            for h in range(SSD_HEADS)], axis=0)


_NT = (((1,), (1,)), ((), ()))
_TN = (((0,), (0,)), ((), ()))
GH = SSD_HEADS // 2
GR = GH * SSD_HD


def ssd_fwd(xc3, p3, bias_row, alog_row, dcol):
    B, S, _ = xc3.shape
    nc = S // CHUNK

    def body(xs_ref, bc_ref, dtb_ref, bias_ref, alog_ref, dcol_ref, y_ref, st_ref, state, yT):
        @pl.when(pl.program_id(1) == 0)
        def _():
            state[...] = jnp.zeros_like(state)

        dt, cs = _ssd_decay(dtb_ref[0], bias_ref[...], alog_ref[...])
        csT = cs.T
        eT = jnp.exp(csT)
        decX = _expand(jnp.exp(csT[:, CHUNK - 1:CHUNK] - csT))
        eX = _expand(eT)
        elastX = eX[:, CHUNK - 1:CHUNK]
        xsT = xs_ref[0].T
        uT = xsT * _expand(dt.T)
        bc = bc_ref[0]
        st_ref[0, 0] = state[...]
        srow = lax.broadcasted_iota(jnp.int32, (CHUNK, CHUNK), 0)
        lcol = lax.broadcasted_iota(jnp.int32, (CHUNK, CHUNK), 1)
        for g in range(2):
            Bg = bc[:, g * SSD_N:(g + 1) * SSD_N].astype(bf16)
            Cg = bc[:, (2 + g) * SSD_N:(3 + g) * SSD_N].astype(bf16)
            GT = lax.dot_general(Bg, Cg, _NT, preferred_element_type=f32)
            rows = slice(g * GR, (g + 1) * GR)
            Sg = state[rows]
            yoffT = lax.dot_general(Sg.astype(bf16), Cg, _NT, preferred_element_type=f32) * eX[rows]
            state[rows] = Sg * elastX[rows] + jnp.dot((uT[rows] * decX[rows]).astype(bf16), Bg, preferred_element_type=f32)
            for k in range(GH):
                h = g * GH + k
                hr = slice(h * SSD_HD, (h + 1) * SSD_HD)
                seg = csT[DT_LANE + h:DT_LANE + h + 1, :] - cs[:, DT_LANE + h:DT_LANE + h + 1]
                LT = jnp.where(lcol >= srow, jnp.exp(jnp.minimum(seg, 0.0)), 0.0)
                yT[hr] = (jnp.dot(uT[hr].astype(bf16), (GT * LT).astype(bf16), preferred_element_type=f32)
                          + yoffT[k * SSD_HD:(k + 1) * SSD_HD] + dcol_ref[hr] * xsT[hr])
        y_ref[0] = yT[...].T

    return pl.pallas_call(body, name="ssd_fwd", grid=(B, nc),
                          in_specs=[pl.BlockSpec((1, CHUNK, D_SSD), lambda b, c: (b, c, 0)),
                                    pl.BlockSpec((1, CHUNK, 512), lambda b, c: (b, c, 2)),
                                    pl.BlockSpec((1, CHUNK, 128), lambda b, c: (b, c, OFF_KRDT // 128)),
                                    _const(1, 128), _const(1, 128), _const(D_SSD, 1)],
                          out_specs=[pl.BlockSpec((1, CHUNK, D_SSD), lambda b, c: (b, c, 0)),
                                     pl.BlockSpec((1, 1, D_SSD, SSD_N), lambda b, c: (b, c, 0, 0))],
                          out_shape=[jax.ShapeDtypeStruct((B, S, D_SSD), f32), jax.ShapeDtypeStruct((B, nc, D_SSD, SSD_N), f32)],
                          scratch_shapes=[pltpu.VMEM((D_SSD, SSD_N), f32), pltpu.VMEM((D_SSD, CHUNK), f32)],
                          compiler_params=_cparams(2))(xc3, xc3, p3, bias_row, alog_row, dcol)


def ssd_bwd(xc3, p3, dy3, states, bias_row, alog_row, bias_col, alog_col, dcol):
    B, S, _ = xc3.shape
    nc = S // CHUNK

    def body(xs_ref, bc_ref, dtb_ref, dy_ref, st_ref, bias_ref, alog_ref, biasc_ref, alogc_ref, dcol_ref,
             dxc_ref, ddt_ref, dalog_ref, dd_ref, dbias_ref, dS, dUT, accA, accD, accB, dcs_diag):
        @pl.when(pl.program_id(1) == 0)
        def _():
            dS[...] = jnp.zeros_like(dS)

        @pl.when(_first())
        def _():
            accA[...] = jnp.zeros_like(accA)
            accD[...] = jnp.zeros_like(accD)
            accB[...] = jnp.zeros_like(accB)

        dtb = dtb_ref[0]
        dt, cs = _ssd_decay(dtb, bias_ref[...], alog_ref[...])
        dtT, csT = dt.T, cs.T
        decX = _expand(jnp.exp(csT[:, CHUNK - 1:CHUNK] - csT))
        eX = _expand(jnp.exp(csT))
        dtX = _expand(dtT)
        elastX = eX[:, CHUNK - 1:CHUNK]
        xsT = xs_ref[0].T
        uT = xsT * dtX
        dYT = dy_ref[0].T
        bc = bc_ref[0]
        lrow = lax.broadcasted_iota(jnp.int32, (CHUNK, CHUNK), 0)
        scol = lax.broadcasted_iota(jnp.int32, (CHUNK, CHUNK), 1)
        dcs_diag[...] = jnp.zeros_like(dcs_diag)
        vparts, zparts = [], []
        for g in range(2):
            Bf = bc[:, g * SSD_N:(g + 1) * SSD_N]
            Bg = Bf.astype(bf16)
            Cg = bc[:, (2 + g) * SSD_N:(3 + g) * SSD_N].astype(bf16)
            G = lax.dot_general(Cg, Bg, _NT, preferred_element_type=f32)
            BgT = Bf.T.astype(bf16)
            rows = slice(g * GR, (g + 1) * GR)
            dSg = dS[rows]
            Sg = st_ref[0, 0, rows, :]
            dUst = jnp.dot(dSg.astype(bf16), BgT, preferred_element_type=f32) * decX[rows]
            yoffT = lax.dot_general(Sg.astype(bf16), Cg, _NT, preferred_element_type=f32) * eX[rows]
            zparts.append(dYT[rows] * yoffT - dUst * uT[rows])
            dG = jnp.zeros((CHUNK, CHUNK), f32)
            for k in range(GH):
                h = g * GH + k
                hr = slice(h * SSD_HD, (h + 1) * SSD_HD)
                seg = cs[:, DT_LANE + h:DT_LANE + h + 1] - csT[DT_LANE + h:DT_LANE + h + 1, :]
                L = jnp.where(lrow >= scol, jnp.exp(jnp.minimum(seg, 0.0)), 0.0)
                M = G * L
                dYh = dYT[hr].astype(bf16)
                dUT[hr] = jnp.dot(dYh, M.astype(bf16), preferred_element_type=f32) + dUst[k * SSD_HD:(k + 1) * SSD_HD]
                dM = lax.dot_general(dYh, uT[hr].astype(bf16), _TN, preferred_element_type=f32)
                dG = dG + dM * L
                Wm = dM * M
                rs = lax.dot_general(jnp.ones((8, CHUNK), f32), Wm, _NT, precision=HIGHEST, preferred_element_type=f32)[0:1]
                dcs_diag[DT_LANE + h:DT_LANE + h + 1, :] = rs - _rowsum(Wm)
            dGb = dG.astype(bf16)
            dYe = (dYT[rows] * eX[rows]).astype(bf16)
            ude = (uT[rows] * decX[rows]).astype(bf16)
            dC = jnp.dot(dGb, Bg, preferred_element_type=f32) + lax.dot_general(dYe, Sg.astype(bf16), _TN, preferred_element_type=f32)
            dB = (lax.dot_general(dGb, Cg, _TN, preferred_element_type=f32)
                  + lax.dot_general(ude, dSg.astype(bf16), _TN, preferred_element_type=f32))
            dxc_ref[0, :, D_SSD + g * SSD_N:D_SSD + (g + 1) * SSD_N] = dB
            dxc_ref[0, :, D_SSD + (2 + g) * SSD_N:D_SSD + (3 + g) * SSD_N] = dC
            vparts.append(elastX[rows] * jnp.sum(dSg * Sg, axis=1, keepdims=True)
                          + jnp.sum(dUst * uT[rows], axis=1, keepdims=True))
            dS[rows] = elastX[rows] * dSg + jnp.dot(dYe, Cg, preferred_element_type=f32)
        dU = dUT[...]
        dcv = dcol_ref[...]
        dxc_ref[0, :, 0:D_SSD] = (dtX * dU + dcv * dYT).T
        lane = lax.broadcasted_iota(jnp.int32, (D_SSD, CHUNK), 1)
        Z = jnp.concatenate(zparts, axis=0) + jnp.where(lane == CHUNK - 1, jnp.concatenate(vparts, axis=0), 0.0)
        hr_ = lax.broadcasted_iota(jnp.int32, (128, D_SSD), 0)
        hc_ = lax.broadcasted_iota(jnp.int32, (128, D_SSD), 1)
        hsel = (hr_ - DT_LANE == jnp.right_shift(hc_, 6)).astype(f32)
        red = jnp.dot(hsel, jnp.concatenate([Z, dU * xsT, dYT * xsT], axis=1), precision=HIGHEST, preferred_element_type=f32)
        dcsT = red[:, 0:CHUNK] + dcs_diag[...]
        daT = jnp.dot(dcsT, (lrow >= scol).astype(f32), precision=HIGHEST, preferred_element_type=f32)
        rowi = lax.broadcasted_iota(jnp.int32, (128, 1), 0)
        hmask = (rowi >= DT_LANE) & (rowi < DT_LANE + SSD_HEADS)
        a_col = jnp.where(hmask, -jnp.exp(alogc_ref[...]), 0.0)
        ddtT = red[:, CHUNK:2 * CHUNK] + a_col * daT
        ddt_rawT = jnp.where(hmask, ddtT * jax.nn.sigmoid(dtb.T + biasc_ref[...]), 0.0)
        ddt_ref[0] = ddt_rawT.T
        accA[...] += daT * dtT
        accD[...] += red[:, 2 * CHUNK:3 * CHUNK]
        accB[...] += ddt_rawT

        @pl.when((pl.program_id(0) == B - 1) & (pl.program_id(1) == nc - 1))
        def _():
            dalog_ref[...] = jnp.broadcast_to(jnp.sum(accA[...], axis=1, keepdims=True) * a_col, (128, 128))
            dd_ref[...] = jnp.broadcast_to(jnp.sum(accD[...], axis=1, keepdims=True), (128, 128))
            dbias_ref[...] = jnp.broadcast_to(jnp.sum(accB[...], axis=1, keepdims=True), (128, 128))

    def rev(width, cb):
        return pl.BlockSpec((1, CHUNK, width), lambda b, c: (b, nc - 1 - c, cb))

    acc_spec = pl.BlockSpec((128, 128), lambda b, c: (0, 0))
    acc_shape = jax.ShapeDtypeStruct((128, 128), f32)
    return pl.pallas_call(body, name="ssd_bwd", grid=(B, nc),
                          in_specs=[rev(D_SSD, 0), rev(512, 2), rev(128, OFF_KRDT // 128), rev(D_SSD, 0),
                                    pl.BlockSpec((1, 1, D_SSD, SSD_N), lambda b, c: (b, nc - 1 - c, 0, 0)),
                                    _const(1, 128), _const(1, 128), _const(128, 1), _const(128, 1), _const(D_SSD, 1)],
                          out_specs=[rev(D_XBC, 0), rev(128, 0), acc_spec, acc_spec, acc_spec],
                          out_shape=[jax.ShapeDtypeStruct((B, S, D_XBC), f32), jax.ShapeDtypeStruct((B, S, 128), f32),
                                     acc_shape, acc_shape, acc_shape],
                          scratch_shapes=[pltpu.VMEM((D_SSD, SSD_N), f32), pltpu.VMEM((D_SSD, CHUNK), f32),
                                          pltpu.VMEM((128, 128), f32), pltpu.VMEM((128, 128), f32), pltpu.VMEM((128, 128), f32),
                                          pltpu.VMEM((128, 128), f32)],
                          compiler_params=_cparams(2))(xc3, xc3, p3, dy3, states, bias_row, alog_row, bias_col, alog_col, dcol)


ATT_SCALE = float(QK) ** -0.5
NEG = -1e30


def _att_block(S):
    return _pick(S, (512, 256, 128))


def attn_fwd(q3, k3, v3):
    B, S, _ = q3.shape
    bq = _att_block(S)
    nq = S // bq

    def body(q_ref, k_ref, v_ref, o_ref, lse_ref, m_s, l_s, acc):
        i, j = pl.program_id(2), pl.program_id(3)

        @pl.when(j == 0)
        def _():
            m_s[...] = jnp.full_like(m_s, NEG)
            l_s[...] = jnp.zeros_like(l_s)
            acc[...] = jnp.zeros_like(acc)

        @pl.when(j <= i)
        def _():
            s = lax.dot_general(q_ref[0], k_ref[0], _NT, preferred_element_type=f32) * ATT_SCALE
            r = lax.broadcasted_iota(jnp.int32, (bq, bq), 0)
            c = lax.broadcasted_iota(jnp.int32, (bq, bq), 1)
            s = jnp.where((j < i) | (r >= c), s, NEG)
            m_new = jnp.maximum(m_s[...], jnp.max(s, axis=-1, keepdims=True))
            alpha = jnp.exp(m_s[...] - m_new)
            p = jnp.exp(s - m_new)
            l_s[...] = alpha * l_s[...] + jnp.sum(p, axis=-1, keepdims=True)
            acc[...] = alpha * acc[...] + jnp.dot(p.astype(bf16), v_ref[0], preferred_element_type=f32)
            m_s[...] = m_new

        @pl.when(j == i)
        def _():
            o_ref[0] = acc[...] / l_s[...]
            lse_ref[0, 0] = m_s[...] + jnp.log(l_s[...])

    return pl.pallas_call(body, name="attn_fwd", grid=(B, MLA_H, nq, nq),
                          in_specs=[pl.BlockSpec((1, bq, 256), lambda b, h, i, j: (b, i, h)),
                                    pl.BlockSpec((1, bq, 256), lambda b, h, i, j: (b, jnp.minimum(j, i), h)),
                                    pl.BlockSpec((1, bq, VD), lambda b, h, i, j: (b, jnp.minimum(j, i), h))],
                          out_specs=[pl.BlockSpec((1, bq, VD), lambda b, h, i, j: (b, i, h)),
                                     pl.BlockSpec((1, 1, bq, 1), lambda b, h, i, j: (b, h, i, 0))],
                          out_shape=[jax.ShapeDtypeStruct((B, S, MLA_H * VD), f32), jax.ShapeDtypeStruct((B, MLA_H, S, 1), f32)],
                          scratch_shapes=[pltpu.VMEM((bq, 1), f32), pltpu.VMEM((bq, 1), f32), pltpu.VMEM((bq, VD), f32)],
                          compiler_params=_cparams(4))(q3, k3, v3)


def attn_bwd_dq(q3, k3, v3, o3, do3, lse):
    B, S, _ = q3.shape
    bq = _att_block(S)
    nq = S // bq

    def body(q_ref, k_ref, v_ref, o_ref, do_ref, lse_ref, dq_ref, dl_ref, acc, dl_s):
        i, j = pl.program_id(2), pl.program_id(3)

        @pl.when(j == 0)
        def _():
            acc[...] = jnp.zeros_like(acc)
            dl_s[...] = jnp.sum(o_ref[0] * do_ref[0], axis=-1, keepdims=True)

        @pl.when(j <= i)
        def _():
            s = lax.dot_general(q_ref[0], k_ref[0], _NT, preferred_element_type=f32) * ATT_SCALE
            r = lax.broadcasted_iota(jnp.int32, (bq, bq), 0)
            c = lax.broadcasted_iota(jnp.int32, (bq, bq), 1)
            p = jnp.where((j < i) | (r >= c), jnp.exp(s - lse_ref[0, 0]), 0.0)
            dp = lax.dot_general(do_ref[0].astype(bf16), v_ref[0], _NT, preferred_element_type=f32)
            ds = p * (dp - dl_s[...]) * ATT_SCALE
            acc[...] += jnp.dot(ds.astype(bf16), k_ref[0], preferred_element_type=f32)

        @pl.when(j == i)
        def _():
            dq_ref[0] = acc[...]
            dl_ref[0, 0] = dl_s[...]

    qspec = pl.BlockSpec((1, bq, 256), lambda b, h, i, j: (b, i, h))
    ospec = pl.BlockSpec((1, bq, VD), lambda b, h, i, j: (b, i, h))
    cspec = pl.BlockSpec((1, 1, bq, 1), lambda b, h, i, j: (b, h, i, 0))
    return pl.pallas_call(body, name="attn_bwd_dq", grid=(B, MLA_H, nq, nq),
                          in_specs=[qspec, pl.BlockSpec((1, bq, 256), lambda b, h, i, j: (b, jnp.minimum(j, i), h)),
                                    pl.BlockSpec((1, bq, VD), lambda b, h, i, j: (b, jnp.minimum(j, i), h)), ospec, ospec, cspec],
                          out_specs=[qspec, cspec],
                          out_shape=[jax.ShapeDtypeStruct((B, S, MLA_H * 256), f32), jax.ShapeDtypeStruct((B, MLA_H, S, 1), f32)],
                          scratch_shapes=[pltpu.VMEM((bq, 256), f32), pltpu.VMEM((bq, 1), f32)],
                          compiler_params=_cparams(4))(q3, k3, v3, o3, do3, lse)


def attn_bwd_dkv(q3, k3, v3, do3, lse_row, delta_row):
    B, S, _ = q3.shape
    bq = _att_block(S)
    nq = S // bq

    def body(q_ref, k_ref, v_ref, do_ref, lse_ref, dl_ref, dk_ref, dv_ref, dk_acc, dv_acc):
        j, i = pl.program_id(2), pl.program_id(3)

        @pl.when(i == 0)
        def _():
            dk_acc[...] = jnp.zeros_like(dk_acc)
            dv_acc[...] = jnp.zeros_like(dv_acc)

        @pl.when(i >= j)
        def _():
            q = q_ref[0]
            do = do_ref[0].astype(bf16)
            st = lax.dot_general(k_ref[0], q, _NT, preferred_element_type=f32) * ATT_SCALE
            r = lax.broadcasted_iota(jnp.int32, (bq, bq), 0)
            c = lax.broadcasted_iota(jnp.int32, (bq, bq), 1)
            pt = jnp.where((i > j) | (c >= r), jnp.exp(st - lse_ref[0, 0]), 0.0)
            dv_acc[...] += jnp.dot(pt.astype(bf16), do, preferred_element_type=f32)
            dpt = lax.dot_general(v_ref[0], do, _NT, preferred_element_type=f32)
            dst = pt * (dpt - dl_ref[0, 0]) * ATT_SCALE
            dk_acc[...] += jnp.dot(dst.astype(bf16), q, preferred_element_type=f32)

        @pl.when(i == nq - 1)
        def _():
            dk_ref[0] = dk_acc[...]
            dv_ref[0] = dv_acc[...]

    kspec = pl.BlockSpec((1, bq, 256), lambda b, h, j, i: (b, j, h))
    vspec = pl.BlockSpec((1, bq, VD), lambda b, h, j, i: (b, j, h))
    rspec = pl.BlockSpec((1, 1, 1, bq), lambda b, h, j, i: (b, h, 0, jnp.maximum(i, j)))
    return pl.pallas_call(body, name="attn_bwd_dkv", grid=(B, MLA_H, nq, nq),
                          in_specs=[pl.BlockSpec((1, bq, 256), lambda b, h, j, i: (b, jnp.maximum(i, j), h)), kspec, vspec,
                                    pl.BlockSpec((1, bq, VD), lambda b, h, j, i: (b, jnp.maximum(i, j), h)), rspec, rspec],
                          out_specs=[kspec, vspec],
                          out_shape=[jax.ShapeDtypeStruct((B, S, MLA_H * 256), f32), jax.ShapeDtypeStruct((B, S, MLA_H * VD), f32)],
                          scratch_shapes=[pltpu.VMEM((bq, 256), f32), pltpu.VMEM((bq, VD), f32)],
                          compiler_params=_cparams(4))(q3, k3, v3, do3, lse_row, delta_row)


def ada_fwd(c_all, w, b):
    n = w.shape[1]

    def body(c_ref, w_ref, b_ref, o_ref):
        o_ref[...] = jnp.dot(_silu(c_ref[...]).astype(bf16), w_ref[...].astype(bf16), preferred_element_type=f32) + b_ref[...]

    return pl.pallas_call(body, name="ada_fwd", out_shape=jax.ShapeDtypeStruct((c_all.shape[0], n), f32),
                          compiler_params=pltpu.CompilerParams(vmem_limit_bytes=VMEM_LIMIT))(c_all, w, b)


def ada_bwd(c_all, dmod):
    n = dmod.shape[1]

    def body(c_ref, d_ref, o_ref):
        o_ref[...] = lax.dot_general(_silu(c_ref[...]).astype(bf16), d_ref[...].astype(bf16), _TN, preferred_element_type=f32)

    return pl.pallas_call(body, name="ada_bwd", out_shape=jax.ShapeDtypeStruct((c_all.shape[1], n), f32),
                          compiler_params=pltpu.CompilerParams(vmem_limit_bytes=VMEM_LIMIT))(c_all, dmod)


def sum_leading(x, name):
    n, R, _ = x.shape
    tr = _pick(R, (512, 256, 128, 64, 32, 16, 8))

    def body(x_ref, o_ref):
        acc = x_ref[0].astype(f32)
        for k in range(1, n):
            acc = acc + x_ref[k].astype(f32)
        o_ref[...] = acc

    return pl.pallas_call(body, name=name, grid=(R // tr,), in_specs=[pl.BlockSpec((n, tr, 128), lambda i: (0, i, 0))],
                          out_specs=pl.BlockSpec((tr, 128), lambda i: (i, 0)), out_shape=jax.ShapeDtypeStruct((R, 128), f32),
                          compiler_params=_cparams(1))(x)


def _adamw_body(w_ref, g_ref, m_ref, v_ref, d_ref, mo_ref, vo_ref):
    gv = g_ref[...]
    mn = ADAM_B1 * m_ref[...] + (1.0 - ADAM_B1) * gv
    vn = ADAM_B2 * v_ref[...] + (1.0 - ADAM_B2) * jnp.square(gv)
    m_hat = mn / (1.0 - ADAM_B1 ** ADAM_STEP)
    v_hat = vn / (1.0 - ADAM_B2 ** ADAM_STEP)
    d_ref[...] = -ADAM_LR * (m_hat / (jnp.sqrt(v_hat) + ADAM_EPS) + ADAM_WD * w_ref[...])
    mo_ref[...] = mn
    vo_ref[...] = vn


def adamw(w, g, m, v):
    R = w.shape[0]
    tr = _pick(R, (512, 256, 128, 64, 32, 16, 8))
    spec = pl.BlockSpec((tr, 128), lambda i: (i, 0))
    shp = jax.ShapeDtypeStruct((R, 128), f32)
    return pl.pallas_call(functools.partial(_adamw_body), name="adamw", grid=(R // tr,), in_specs=[spec] * 4,
                          out_specs=[spec] * 3, out_shape=[shp] * 3, compiler_params=_cparams(1))(w, g, m, v)


def _row_tile(a):
    return _pick(a, (256, 128, 64, 32, 16, 8)) if a % 8 == 0 else a


def adamw_nd(w, g, m, v):
    L, a, b = w.shape
    ta = _row_tile(a)
    spec = pl.BlockSpec((1, ta, b), lambda l, i: (l, i, 0))
    shp = jax.ShapeDtypeStruct((L, a, b), f32)
    return pl.pallas_call(functools.partial(_adamw_body), name="adamw_nd", grid=(L, a // ta), in_specs=[spec] * 4,
                          out_specs=[spec] * 3, out_shape=[shp] * 3, compiler_params=_cparams(2))(w, g, m, v)


def sum_slots(x):
    n, L, a, b = x.shape
    ta = _row_tile(a)

    def body(x_ref, o_ref):
        acc = x_ref[0]
        for k in range(1, n):
            acc = acc + x_ref[k]
        o_ref[...] = acc

    return pl.pallas_call(body, name="sum_slots", grid=(L, a // ta),
                          in_specs=[pl.BlockSpec((n, 1, ta, b), lambda l, i: (0, l, i, 0))],
                          out_specs=pl.BlockSpec((1, ta, b), lambda l, i: (l, i, 0)),
                          out_shape=jax.ShapeDtypeStruct((L, a, b), f32), compiler_params=_cparams(2))(x)


def exchange(arrays, scatter, name):
    n = len(arrays)
    blocks = [a.shape[1:] if scatter else a.shape for a in arrays]

    def body(*refs):
        ins, outs = refs[:n], refs[n:2 * n]
        send_sems, recv_sems, local_sems = refs[2 * n:]
        x, y, c = lax.axis_index("x"), lax.axis_index("y"), lax.axis_index("c")
        me = 4 * x + 2 * y + c
        peers = []
        for k in range(1, N_DEV):
            px = 1 - x if k & 4 else x
            py = 1 - y if k & 2 else y
            pc = 1 - c if k & 1 else c
            peers.append(((px, py, pc), 4 * px + 2 * py + pc))
        locals_, sends = [], []
        for a in range(n):
            src = ins[a].at[me] if scatter else ins[a]
            lc = pltpu.make_async_copy(src, outs[a].at[me], local_sems.at[a])
            lc.start()
            locals_.append(lc)
            for k, (peer, pid) in enumerate(peers):
                cp = pltpu.make_async_remote_copy(src_ref=ins[a].at[pid] if scatter else ins[a], dst_ref=outs[a].at[me],
                                                  send_sem=send_sems.at[a, k], recv_sem=recv_sems.at[a, k],
                                                  device_id=peer, device_id_type=pl.DeviceIdType.MESH)
                cp.start()
                sends.append(cp)
        for a in range(n):
            for k, (peer, pid) in enumerate(peers):
                pltpu.make_async_remote_copy(src_ref=ins[a].at[pid] if scatter else ins[a], dst_ref=outs[a].at[pid],
                                             send_sem=send_sems.at[a, k], recv_sem=recv_sems.at[a, k],
                                             device_id=peer, device_id_type=pl.DeviceIdType.MESH).wait_recv()
        for cp in sends:
            cp.wait_send()
        for lc in locals_:
            lc.wait()

    hbm = pl.BlockSpec(memory_space=pltpu.HBM)
    return pl.pallas_call(body, name=name, in_specs=[hbm] * n, out_specs=[hbm] * n,
                          out_shape=[jax.ShapeDtypeStruct((N_DEV,) + tuple(b), a.dtype) for a, b in zip(arrays, blocks)],
                          scratch_shapes=[pltpu.SemaphoreType.DMA((n, N_DEV - 1)), pltpu.SemaphoreType.DMA((n, N_DEV - 1)),
                                          pltpu.SemaphoreType.DMA((n,))],
                          )(*arrays)


BIG = (("w_in", "col"), ("conv_w", "col"), ("w_uq", "col"), ("w_ukv", "col"), ("w_out", "row"), ("w_up", "col"),
       ("conv_ff_w", "col"), ("w_down", "row"))
SMALL = ("b_ada", "norm_mix", "conv_b", "dt_bias", "a_log", "d_skip", "ssd_norm", "q_norm", "kv_norm", "attn_norm",
         "norm_mlp", "conv_ff_b", "final_norm")
CONVS = ("conv_w", "conv_ff_w")
PACK_ALIGN = 2048


def _padded(n):
    return -(-n // PACK_ALIGN) * PACK_ALIGN


def _flat_pad(a):
    f = a.reshape(-1)
    return jnp.pad(f, (0, _padded(f.shape[0]) - f.shape[0]))


PACK_ROWS = 512


def pack(arrs):
    f = jnp.concatenate([_flat_pad(a) for a in arrs])
    n = PACK_ROWS * 128
    return jnp.pad(f, (0, -(-f.shape[0] // n) * n - f.shape[0])).reshape(-1, 128)


def unpack(flat, shapes):
    f = flat.reshape(-1)
    out, off = [], 0
    for s in shapes:
        n = int(np.prod(s))
        out.append(f[off:off + n].reshape(s))
        off += _padded(n)
    return out


def shards_to_full(g, shard_shape, kind):
    L, a, b = shard_shape
    if kind == "col":
        return g.transpose(1, 2, 0, 3).reshape(L, a, N_DEV * b)
    return g.transpose(1, 0, 2, 3).reshape(L, N_DEV * a, b)


def full_to_shards(full, shard_shape, kind):
    L, a, b = shard_shape
    if kind == "col":
        return full.reshape(L, a, N_DEV, b).transpose(2, 0, 1, 3)
    return full.reshape(L, N_DEV, a, b).transpose(1, 0, 2, 3)


def w_in_layout(w):
    z = lambda n: jnp.zeros(w.shape[:-1] + (n,), w.dtype)
    return jnp.concatenate([w[..., :2560], w[..., 2576:2960], z(128), w[..., 2960:3216], w[..., 3216:3280],
                            w[..., 2560:2576], z(48)], axis=-1)


def w_in_unlayout(g):
    return jnp.concatenate([g[..., :2560], g[..., 3392:3408], g[..., 2560:2944], g[..., 3072:3328], g[..., 3328:3392]], axis=-1)


def w_uq_layout(w):
    return jnp.pad(w.reshape(Q_RANK, MLA_H, QK), ((0, 0), (0, 0), (0, 256 - QK))).reshape(Q_RANK, MLA_H * 256)


def w_uq_unlayout(g):
    return g.reshape(Q_RANK, MLA_H, 256)[:, :, :QK].reshape(Q_RANK, MLA_H * QK)


def w_ukv_layout(w):
    return w.reshape(KV_RANK, MLA_H, 2, 128).transpose(0, 2, 1, 3).reshape(KV_RANK, 2 * MLA_H * 128)


def w_ukv_unlayout(g):
    return g.reshape(KV_RANK, 2, MLA_H, 128).transpose(0, 2, 1, 3).reshape(KV_RANK, 2 * MLA_H * 128)


def _head_row(v):
    return jnp.zeros((1, 128), f32).at[0, DT_LANE:DT_LANE + SSD_HEADS].set(v)


def layer_fwd(x3, mod, W, P, l, cosf, sinf):
    B, S, _ = x3.shape
    T = B * S
    sv = {}
    h = normmod_fwd(x3, mod, P["norm_mix"][l][None], 0, 1)
    p = mm(h.reshape(T, D), W["w_in"][l], "nn", "mm_in")
    p3 = p.reshape(B, S, IN_COLS)
    bias_row, alog_row = _head_row(P["dt_bias"][l]), _head_row(P["a_log"][l])
    dcol = jnp.repeat(P["d_skip"][l], SSD_HD)[:, None]
    xc3 = conv_ssd_fwd(p3, P["conv_w"][l], P["conv_b"][l][None])
    yc3, states = ssd_fwd(xc3, p3, bias_row, alog_row, dcol)
    y_ssd = ssd_out_fwd(yc3, p3, P["ssd_norm"][l][None])
    cqn = rms_fwd(p3, 512, OFF_CQ // 512, Q_RANK, P["q_norm"][l][None], "rms_q_fwd")
    ckvn = rms_fwd(p3, KV_RANK, OFF_CKV // KV_RANK, KV_RANK, P["kv_norm"][l][None], "rms_kv_fwd")
    qraw = mm(cqn.reshape(T, Q_RANK), W["w_uq"][l], "nn", "mm_uq")
    kvraw = mm(ckvn.reshape(T, KV_RANK), W["w_ukv"][l], "nn", "mm_ukv")
    q3 = qprep_fwd(qraw.reshape(B, S, -1), cosf, sinf)
    k3, v3 = kprep_fwd(kvraw.reshape(B, S, -1), p3, cosf, sinf)
    o3, lse = attn_fwd(q3, k3, v3)
    y_att = rms_fwd(o3, D, 0, D, P["attn_norm"][l][None], "rms_o_fwd")
    cat = jnp.concatenate([y_ssd, y_att], axis=-1).reshape(T, 2 * D)
    x1, y1 = mm(cat, W["w_out"][l], "nn", "mm_out", resid=x3.reshape(T, D), gate=mod[:, 2:3, :], seq=S)
    x13 = x1.reshape(B, S, D)
    h2 = normmod_fwd(x13, mod, P["norm_mlp"][l][None], 3, 4)
    u = mm(h2.reshape(T, D), W["w_up"][l], "nn", "mm_up")
    u3 = u.reshape(B, S, 2 * D_FF)
    a = glu_fwd(u3, P["conv_ff_w"][l], P["conv_ff_b"][l][None])
    x2, y2 = mm(a.reshape(T, D_FF), W["w_down"][l], "nn", "mm_down", resid=x1, gate=mod[:, 5:6, :], seq=S)
    sv.update(x=x3, h=h, p3=p3, xc3=xc3, yc3=yc3, states=states, cqn=cqn, ckvn=ckvn, q3=q3, k3=k3, v3=v3, o3=o3, lse=lse,
              cat=cat, y1=y1, x1=x13, h2=h2, u3=u3, a=a, y2=y2, bias_row=bias_row, alog_row=alog_row, dcol=dcol)
    return x2.reshape(B, S, D), sv


def layer_bwd(dx3, sv, mod, W, P, l, cosf, sinf):
    B, S, _ = dx3.shape
    T = B * S
    g = {}
    dy2, dg2 = gate_bwd(dx3, sv["y2"].reshape(B, S, D), mod, 5)
    dy2 = dy2.reshape(T, D)
    da = mm(dy2, W["w_down"][l], "nt", "mm_down_dx")
    g["w_down"] = mm(sv["a"].reshape(T, D_FF), dy2, "tn", "mm_down_dw")
    dug, duv, dwg, dwv, dbg, dbv = glu_bwd(sv["u3"], da.reshape(B, S, D_FF), P["conv_ff_w"][l], P["conv_ff_b"][l][None])
    g["conv_ff_w"] = jnp.concatenate([dwg, dwv], axis=1)
    g["conv_ff_b"] = jnp.concatenate([dbg, dbv], axis=1)[0]
    du = jnp.concatenate([dug, duv], axis=-1).reshape(T, 2 * D_FF)
    dh2 = mm(du, W["w_up"][l], "nt", "mm_up_dx")
    g["w_up"] = mm(sv["h2"].reshape(T, D), du, "tn", "mm_up_dw")
    dx1, dsh2, dsc2, dnm = normmod_bwd(sv["x1"], dh2.reshape(B, S, D), dx3, mod, P["norm_mlp"][l][None], 4)
    g["norm_mlp"] = dnm[0]
    dy1, dg1 = gate_bwd(dx1, sv["y1"].reshape(B, S, D), mod, 2)
    dy1 = dy1.reshape(T, D)
    dcat = mm(dy1, W["w_out"][l], "nt", "mm_out_dx")
    g["w_out"] = mm(sv["cat"], dy1, "tn", "mm_out_dw")
    dcat3 = dcat.reshape(B, S, 2 * D)
    do3, dan = rms_bwd(sv["o3"], D, 0, D, dcat3, 1, P["attn_norm"][l][None], f32, "rms_o_bwd")
    g["attn_norm"] = dan[0]
    dq3, delta = attn_bwd_dq(sv["q3"], sv["k3"], sv["v3"], sv["o3"], do3, sv["lse"])
    dk3, dv3 = attn_bwd_dkv(sv["q3"], sv["k3"], sv["v3"], do3, sv["lse"].reshape(B, MLA_H, 1, S), delta.reshape(B, MLA_H, 1, S))
    dqraw = qprep_bwd(dq3, cosf, sinf).reshape(T, -1)
    dcqn = mm(dqraw, W["w_uq"][l], "nt", "mm_uq_dx")
    g["w_uq"] = mm(sv["cqn"].reshape(T, Q_RANK), dqraw, "tn", "mm_uq_dw")
    dcq, dqn = rms_bwd(sv["p3"], 512, OFF_CQ // 512, Q_RANK, dcqn.reshape(B, S, Q_RANK), 0, P["q_norm"][l][None], bf16, "rms_q_bwd")
    g["q_norm"] = dqn[0]
    dyc3, dz, dsn = ssd_out_bwd(sv["yc3"], sv["p3"], dcat3, P["ssd_norm"][l][None])
    g["ssd_norm"] = dsn[0]
    bias_col, alog_col = sv["bias_row"].reshape(128, 1), sv["alog_row"].reshape(128, 1)
    dxc3, ddt3, dalog, dd, dbias = ssd_bwd(sv["xc3"], sv["p3"], dyc3, sv["states"], sv["bias_row"], sv["alog_row"],
                                           bias_col, alog_col, sv["dcol"])
    heads = slice(DT_LANE, DT_LANE + SSD_HEADS)
    g["a_log"], g["d_skip"], g["dt_bias"] = dalog[heads, 0], dd[heads, 0], dbias[heads, 0]
    dxbc, dcw, dcb = conv_ssd_bwd(sv["p3"], dxc3, P["conv_w"][l], P["conv_b"][l][None])
    g["conv_w"], g["conv_b"] = dcw, dcb[0]
    dkvraw, dkrdt = kprep_bwd(dk3, dv3, ddt3, cosf, sinf)
    dkvraw = dkvraw.reshape(T, -1)
    dckvn = mm(dkvraw, W["w_ukv"][l], "nt", "mm_ukv_dx")
    g["w_ukv"] = mm(sv["ckvn"].reshape(T, KV_RANK), dkvraw, "tn", "mm_ukv_dw")
    dckv, dkn = rms_bwd(sv["p3"], KV_RANK, OFF_CKV // KV_RANK, KV_RANK, dckvn.reshape(B, S, KV_RANK), 0, P["kv_norm"][l][None],
                        bf16, "rms_kv_bwd")
    g["kv_norm"] = dkn[0]
    dp = jnp.concatenate([dz, dxbc, dcq, jnp.zeros((B, S, 128), bf16), dckv, dkrdt], axis=-1).reshape(T, IN_COLS)
    dh = mm(dp, W["w_in"][l], "nt", "mm_in_dx")
    g["w_in"] = mm(sv["h"].reshape(T, D), dp, "tn", "mm_in_dw")
    dx0, dsh1, dsc1, dnx = normmod_bwd(sv["x"], dh.reshape(B, S, D), dx1, mod, P["norm_mix"][l][None], 1)
    g["norm_mix"] = dnx[0]
    dmod = jnp.concatenate([dsh1, dsc1, dg1, dsh2, dsc2, dg2], axis=1)
    return dx0, dmod, g


def kernel(x, c, positions, w_ada, b_ada, norm_mix, w_in, conv_w, conv_b, dt_bias, a_log, d_skip, ssd_norm, q_norm, w_uq, kv_norm, w_ukv, attn_norm, w_out, norm_mlp, w_up, conv_ff_w, conv_ff_b, w_down, final_norm, loss_target, m_w_ada, m_b_ada, m_norm_mix, m_w_in, m_conv_w, m_conv_b, m_dt_bias, m_a_log, m_d_skip, m_ssd_norm, m_q_norm, m_w_uq, m_kv_norm, m_w_ukv, m_attn_norm, m_w_out, m_norm_mlp, m_w_up, m_conv_ff_w, m_conv_ff_b, m_w_down, m_final_norm, v_w_ada, v_b_ada, v_norm_mix, v_w_in, v_conv_w, v_conv_b, v_dt_bias, v_a_log, v_d_skip, v_ssd_norm, v_q_norm, v_w_uq, v_kv_norm, v_w_ukv, v_attn_norm, v_w_out, v_norm_mlp, v_w_up, v_conv_ff_w, v_conv_ff_b, v_w_down, v_final_norm):
    given = dict(locals())
    B, S, _ = x.shape
    me = 4 * lax.axis_index("x") + 2 * lax.axis_index("y") + lax.axis_index("c")
    P = {n: given[n] for n in SMALL}

    big_shapes = [given[n].shape for n, _ in BIG]
    *gathered, c_all = exchange([given[n] if n in CONVS else given[n].astype(bf16) for n, _ in BIG] + [c],
                                False, "gather_weights")
    full = {n: shards_to_full(arr, shp, kind) for (n, kind), shp, arr in zip(BIG, big_shapes, gathered)}
    W = {"w_in": w_in_layout(full["w_in"]), "w_out": full["w_out"], "w_up": full["w_up"], "w_down": full["w_down"],
         "w_uq": jnp.stack([w_uq_layout(full["w_uq"][l]) for l in range(DEPTH)]),
         "w_ukv": jnp.stack([w_ukv_layout(full["w_ukv"][l]) for l in range(DEPTH)])}
    P["conv_w"], P["conv_ff_w"] = full["conv_w"], full["conv_ff_w"]

    n_ada = w_ada.shape[2]
    c_all = c_all.reshape(N_DEV * B, D)
    b_sh = lax.dynamic_slice_in_dim(b_ada, me * n_ada, n_ada, axis=1)
    mod_sh = jnp.stack([ada_fwd(c_all, w_ada[l], b_sh[l][None]) for l in range(DEPTH)])
    (mod_g,) = exchange([mod_sh], False, "gather_mod")
    mod_mine = lax.dynamic_slice_in_dim(mod_g, me * B, B, axis=2)
    mods = mod_mine.transpose(1, 2, 0, 3).reshape(DEPTH, B, 6, D)

    inv_freq = jnp.asarray(1.0 / (ROPE_BASE ** (np.arange(0, ROPE, 2, dtype=np.float32) / ROPE)))
    ang = positions.astype(f32)[..., None] * inv_freq
    zeros = jnp.zeros((B, S, 128 - ROPE), f32)
    cosf = jnp.concatenate([jnp.cos(ang), jnp.cos(ang), zeros], axis=-1)
    sinf = jnp.concatenate([jnp.sin(ang), jnp.sin(ang), zeros], axis=-1)

    xl, saved = x, []
    for l in range(DEPTH):
        xl, sv = layer_fwd(xl, mods[l], W, P, l, cosf, sinf)
        saved.append(sv)
    dxl, d_final, loss_part = final_loss(xl, final_norm[None], loss_target)
    grads, dmods = [None] * DEPTH, [None] * DEPTH
    for l in reversed(range(DEPTH)):
        dxl, dmods[l], grads[l] = layer_bwd(dxl, saved[l], mods[l], W, P, l, cosf, sinf)
    grad_x = dxl

    stack = lambda n: jnp.stack([grads[l][n] for l in range(DEPTH)])
    gfull = {"w_in": w_in_unlayout(stack("w_in")), "conv_w": stack("conv_w"),
             "w_uq": jnp.stack([w_uq_unlayout(grads[l]["w_uq"]) for l in range(DEPTH)]),
             "w_ukv": jnp.stack([w_ukv_unlayout(grads[l]["w_ukv"]) for l in range(DEPTH)]),
             "w_out": stack("w_out"), "w_up": stack("w_up"), "conv_ff_w": stack("conv_ff_w"), "w_down": stack("w_down")}
    recv = exchange([full_to_shards(gfull[n], shp, kind) for (n, kind), shp in zip(BIG, big_shapes)], True, "scatter_grads")
    big_g = [sum_slots(r) for r in recv]

    small_names = [n for n in SMALL if n not in ("b_ada", "final_norm")]
    partial = pack([stack(n) for n in small_names] + [d_final[0], loss_part[0]])
    dmod_all = jnp.stack(dmods)
    part_g, dmod_g = exchange([partial, dmod_all], False, "gather_partials")
    small_sum = sum_leading(part_g, "sum_partials")
    small_g = unpack(small_sum, [given[n].shape for n in small_names] + [(D,), (128,)])
    gsmall = dict(zip(small_names + ["final_norm"], small_g[:-1]))
    loss = small_g[-1][0]
    dmod_rows = dmod_g.transpose(0, 2, 1, 3, 4).reshape(N_DEV * B, DEPTH * 6 * D)
    gsmall["b_ada"] = sum_leading(dmod_rows.reshape(N_DEV * B, -1, 128), "sum_b_ada").reshape(DEPTH, 6 * D)
    dmod_cols = dmod_rows.reshape(N_DEV * B, DEPTH, N_DEV, n_ada)
    dmod_sh = lax.dynamic_slice_in_dim(dmod_cols, me, 1, axis=2)[:, :, 0, :]
    g_w_ada = jnp.stack([ada_bwd(c_all, dmod_sh[:, l, :]) for l in range(DEPTH)])

    res = {"grad": {}, "delta": {}, "new_m": {}, "new_v": {}}
    for n, gv in zip([n for n, _ in BIG] + ["w_ada"], big_g + [g_w_ada]):
        res["grad"][n] = gv
        res["delta"][n], res["new_m"][n], res["new_v"][n] = adamw_nd(given[n], gv, given["m_" + n], given["v_" + n])
    shapes = [given[n].shape for n in SMALL]
    flat = adamw(pack([given[n] for n in SMALL]), pack([gsmall[n] for n in SMALL]), pack([given["m_" + n] for n in SMALL]),
                 pack([given["v_" + n] for n in SMALL]))
    for n in SMALL:
        res["grad"][n] = gsmall[n]
    for key, arr in zip(("delta", "new_m", "new_v"), flat):
        res[key].update(zip(SMALL, unpack(arr, shapes)))
    order = ["w_ada", "b_ada", "norm_mix", "w_in", "conv_w", "conv_b", "dt_bias", "a_log", "d_skip", "ssd_norm", "q_norm", "w_uq",
             "kv_norm", "w_ukv", "attn_norm", "w_out", "norm_mlp", "w_up", "conv_ff_w", "conv_ff_b", "w_down", "final_norm"]
    return (loss, grad_x, *[res[k][n] for k in ("grad", "delta", "new_m", "new_v") for n in order])
```

```python
import functools

import numpy as np
import jax
import jax.numpy as jnp
from jax import lax
from jax.experimental import pallas as pl
from jax.experimental.pallas import tpu as pltpu

f32, bf16 = jnp.float32, jnp.bfloat16
HIGHEST = lax.Precision.HIGHEST

D = 1024
D_SSD = 1024
SSD_HEADS = 16
SSD_HD = 64
SSD_N = 128
CHUNK = 128
D_XBC = 1536
CONV_K = 4
MLA_H = 8
NOPE = 128
ROPE = 64
VD = 128
QK = NOPE + ROPE
Q_RANK = 384
KV_RANK = 256
D_FF = 2816
FF_K = 3
EPS = 1e-6
ROPE_BASE = 10000.0
DEPTH = 2
ADAM_LR, ADAM_B1, ADAM_B2, ADAM_EPS, ADAM_WD, ADAM_STEP = 0.001, 0.9, 0.999, 1e-08, 0.01, 10

N_DEV = 8
IN_COLS = 3456
OFF_XBC, OFF_CQ, OFF_CKV, OFF_KRDT = 1024, 2560, 3072, 3328
DT_LANE = 64
VMEM_LIMIT = 48 * 1024 * 1024
MM_K_WHOLE = 4096


def _cparams(n_grid):
    return pltpu.CompilerParams(dimension_semantics=("arbitrary",) * n_grid, vmem_limit_bytes=VMEM_LIMIT)


def _pick(n, cands):
    for c in cands:
        if n % c == 0:
            return c
    return n


def _silu(x):
    return x * jax.nn.sigmoid(x)


def _dsilu(x):
    s = jax.nn.sigmoid(x)
    return s * (1.0 + x * (1.0 - s))


def _rowsum(x):
    return jnp.sum(x, axis=0, keepdims=True)


def mm(a, b, mode, name, out_dtype=f32, resid=None, gate=None, seq=None):
    if mode == "nn":
        (M, K), N = a.shape, b.shape[1]
    elif mode == "nt":
        (M, K), N = a.shape, b.shape[0]
    else:
        (K, M), N = a.shape, b.shape[1]
    gated = resid is not None
    tm = _pick(seq if gated else M, (1024, 1408, 512, 384, 256, 128))
    tn = _pick(N, (512, 384, 256, 128))
    tk = K if K <= MM_K_WHOLE else _pick(K, (2816, 2048, 1024, 512))
    nk = K // tk
    dims = {"nn": ((1,), (0,)), "nt": ((1,), (1,)), "tn": ((0,), (0,))}[mode]

    def body(a_ref, b_ref, *rest):
        if gated:
            r_ref, g_ref, o_ref, y_ref, acc = rest
        else:
            o_ref, acc = rest

        def finish(res):
            if gated:
                y_ref[...] = res
                o_ref[...] = r_ref[...] + g_ref[0] * res
            else:
                o_ref[...] = res.astype(out_dtype)

        prod = lax.dot_general(a_ref[...].astype(bf16), b_ref[...].astype(bf16), (dims, ((), ())), preferred_element_type=f32)
        if nk == 1:
            finish(prod)
        else:
            k = pl.program_id(2)

            @pl.when(k == 0)
            def _():
                acc[...] = prod

            @pl.when(k > 0)
            def _():
                acc[...] += prod

            @pl.when(k == nk - 1)
            def _():
                finish(acc[...])

    a_spec = pl.BlockSpec((tk, tm), lambda i, j, k: (k, i)) if mode == "tn" else pl.BlockSpec((tm, tk), lambda i, j, k: (i, k))
    b_spec = pl.BlockSpec((tn, tk), lambda i, j, k: (j, k)) if mode == "nt" else pl.BlockSpec((tk, tn), lambda i, j, k: (k, j))
    o_spec = pl.BlockSpec((tm, tn), lambda i, j, k: (i, j))
    in_specs, args = [a_spec, b_spec], [a, b]
    out_specs, out_shape = o_spec, jax.ShapeDtypeStruct((M, N), out_dtype)
    if gated:
        per = seq // tm
        in_specs += [o_spec, pl.BlockSpec((1, 1, tn), lambda i, j, k: (i // per, 0, j))]
        args += [resid, gate]
        out_specs = [o_spec, o_spec]
        out_shape = [jax.ShapeDtypeStruct((M, N), f32), jax.ShapeDtypeStruct((M, N), f32)]
    return pl.pallas_call(body, name=name, grid=(M // tm, N // tn, nk), in_specs=in_specs, out_specs=out_specs,
                          out_shape=out_shape, scratch_shapes=[pltpu.VMEM((tm, tn), f32)],
                          compiler_params=_cparams(3))(*args)


def _tok(ts, width, cb=0):
    return pl.BlockSpec((1, ts, width), lambda b, s: (b, s, cb))


def _perb(rows, width):
    return pl.BlockSpec((1, rows, width), lambda b, s: (b, 0, 0))


def _const(rows, width):
    return pl.BlockSpec((rows, width), lambda b, s: (0, 0))


def _row_call(body, name, B, S, ts, in_specs, out_specs, out_shape, scratch=()):
    return pl.pallas_call(body, name=name, grid=(B, S // ts), in_specs=in_specs, out_specs=out_specs,
                          out_shape=out_shape, scratch_shapes=list(scratch), compiler_params=_cparams(2))


def _first():
    return (pl.program_id(0) == 0) & (pl.program_id(1) == 0)


def normmod_fwd(x3, mod, g, i_sh, i_sc):
    B, S, C = x3.shape
    ts = _pick(S, (512, 256, 128))

    def body(x_ref, mod_ref, g_ref, h_ref):
        x = x_ref[0]
        r = lax.rsqrt(jnp.mean(x * x, axis=-1, keepdims=True) + EPS)
        n = x * r * g_ref[...]
        h_ref[0] = (n * (1.0 + mod_ref[0, i_sc:i_sc + 1, :]) + mod_ref[0, i_sh:i_sh + 1, :]).astype(bf16)

    return _row_call(body, "normmod_fwd", B, S, ts, [_tok(ts, C), _perb(6, C), _const(1, C)], _tok(ts, C),
                     jax.ShapeDtypeStruct((B, S, C), bf16))(x3, mod, g)


def normmod_bwd(x3, dh3, resid3, mod, g, i_sc):
    B, S, C = x3.shape
    ts = _pick(S, (512, 256, 128))

    def body(x_ref, dh_ref, r_ref, mod_ref, g_ref, dx_ref, dsh_ref, dsc_ref, dg_ref):
        @pl.when(pl.program_id(1) == 0)
        def _():
            dsh_ref[...] = jnp.zeros_like(dsh_ref)
            dsc_ref[...] = jnp.zeros_like(dsc_ref)

        @pl.when(_first())
        def _():
            dg_ref[...] = jnp.zeros_like(dg_ref)

        x, dh, gv = x_ref[0], dh_ref[0], g_ref[...]
        r = lax.rsqrt(jnp.mean(x * x, axis=-1, keepdims=True) + EPS)
        xh = x * r
        dn = dh * (1.0 + mod_ref[0, i_sc:i_sc + 1, :])
        dsh_ref[0] += _rowsum(dh)
        dsc_ref[0] += _rowsum(dh * xh * gv)
        dg_ref[...] += _rowsum(dn * xh)
        dxh = dn * gv
        dx_ref[0] = r * (dxh - xh * jnp.mean(dxh * xh, axis=-1, keepdims=True)) + r_ref[0]

    return _row_call(body, "normmod_bwd", B, S, ts,
                     [_tok(ts, C), _tok(ts, C), _tok(ts, C), _perb(6, C), _const(1, C)],
                     [_tok(ts, C), _perb(1, C), _perb(1, C), _const(1, C)],
                     [jax.ShapeDtypeStruct((B, S, C), f32), jax.ShapeDtypeStruct((B, 1, C), f32),
                      jax.ShapeDtypeStruct((B, 1, C), f32), jax.ShapeDtypeStruct((1, C), f32)])(x3, dh3, resid3, mod, g)


def gate_bwd(dx3, y3, mod, i_g):
    B, S, C = dx3.shape
    ts = _pick(S, (512, 256, 128))

    def body(dx_ref, y_ref, mod_ref, dy_ref, dgate_ref):
        @pl.when(pl.program_id(1) == 0)
        def _():
            dgate_ref[...] = jnp.zeros_like(dgate_ref)

        dx = dx_ref[0]
        dy_ref[0] = (dx * mod_ref[0, i_g:i_g + 1, :]).astype(bf16)
        dgate_ref[0] += _rowsum(dx * y_ref[0])

    return _row_call(body, "gate_bwd", B, S, ts, [_tok(ts, C), _tok(ts, C), _perb(6, C)], [_tok(ts, C), _perb(1, C)],
                     [jax.ShapeDtypeStruct((B, S, C), bf16), jax.ShapeDtypeStruct((B, 1, C), f32)])(dx3, y3, mod)


def rms_fwd(src3, width, cb, n, g, name):
    B, S, _ = src3.shape
    ts = _pick(S, (512, 256, 128))

    def body(x_ref, g_ref, o_ref):
        x = x_ref[0][:, :n]
        r = lax.rsqrt(jnp.mean(x * x, axis=-1, keepdims=True) + EPS)
        o_ref[0] = (x * r * g_ref[...]).astype(bf16)

    return _row_call(body, name, B, S, ts, [_tok(ts, width, cb), _const(1, n)], _tok(ts, n),
                     jax.ShapeDtypeStruct((B, S, n), bf16))(src3, g)


def rms_bwd(src3, width, cb, n, dout3, dcb, g, out_dtype, name):
    B, S, _ = src3.shape
    ts = _pick(S, (512, 256, 128))

    def body(x_ref, do_ref, g_ref, dx_ref, dg_ref):
        @pl.when(_first())
        def _():
            dg_ref[...] = jnp.zeros_like(dg_ref)

        x = x_ref[0][:, :n]
        do = do_ref[0].astype(f32)
        r = lax.rsqrt(jnp.mean(x * x, axis=-1, keepdims=True) + EPS)
        xh = x * r
        dg_ref[...] += _rowsum(do * xh)
        dxh = do * g_ref[...]
        dx_ref[0] = (r * (dxh - xh * jnp.mean(dxh * xh, axis=-1, keepdims=True))).astype(out_dtype)

    return _row_call(body, name, B, S, ts, [_tok(ts, width, cb), _tok(ts, n, dcb), _const(1, n)],
                     [_tok(ts, n), _const(1, n)],
                     [jax.ShapeDtypeStruct((B, S, n), out_dtype), jax.ShapeDtypeStruct((1, n), f32)])(src3, dout3, g)


def final_loss(x3, g, tgt3):
    B, S, C = x3.shape
    ts = _pick(S, (512, 256, 128))

    def body(x_ref, g_ref, t_ref, dx_ref, dg_ref, loss_ref):
        @pl.when(_first())
        def _():
            dg_ref[...] = jnp.zeros_like(dg_ref)
            loss_ref[...] = jnp.zeros_like(loss_ref)

        x, gv = x_ref[0], g_ref[...]
        r = lax.rsqrt(jnp.mean(x * x, axis=-1, keepdims=True) + EPS)
        xh = x * r
        e = xh * gv - t_ref[0]
        loss_ref[...] += 0.5 * jnp.sum(e * e) / C
        dout = e / C
        dg_ref[...] += _rowsum(dout * xh)
        dxh = dout * gv
        dx_ref[0] = r * (dxh - xh * jnp.mean(dxh * xh, axis=-1, keepdims=True))

    return _row_call(body, "final_loss", B, S, ts, [_tok(ts, C), _const(1, C), _tok(ts, C)],
                     [_tok(ts, C), _const(1, C), _const(1, 128)],
                     [jax.ShapeDtypeStruct((B, S, C), f32), jax.ShapeDtypeStruct((1, C), f32),
                      jax.ShapeDtypeStruct((1, 128), f32)])(x3, g, tgt3)


def ssd_out_fwd(yc3, p3, w):
    B, S, C = yc3.shape
    ts = _pick(S, (512, 256, 128))
    half = C // 2

    def body(y_ref, z_ref, w_ref, o_ref):
        y = y_ref[0] * _silu(z_ref[0])
        for lo in (0, half):
            yg = y[:, lo:lo + half]
            r = lax.rsqrt(jnp.mean(yg * yg, axis=-1, keepdims=True) + EPS)
            o_ref[0, :, lo:lo + half] = (yg * r * w_ref[:, lo:lo + half]).astype(bf16)

    return _row_call(body, "ssd_out_fwd", B, S, ts, [_tok(ts, C), _tok(ts, C, 0), _const(1, C)], _tok(ts, C),
                     jax.ShapeDtypeStruct((B, S, C), bf16))(yc3, p3, w)


def ssd_out_bwd(yc3, p3, dcat3, w):
    B, S, C = yc3.shape
    ts = _pick(S, (512, 256, 128))
    half = C // 2

    def body(y_ref, z_ref, do_ref, w_ref, dyc_ref, dz_ref, dw_ref):
        @pl.when(_first())
        def _():
            dw_ref[...] = jnp.zeros_like(dw_ref)

        yc, z, do = y_ref[0], z_ref[0], do_ref[0]
        sz = _silu(z)
        y = yc * sz
        for lo in (0, half):
            sl = slice(lo, lo + half)
            yg, dog, wg = y[:, sl], do[:, sl], w_ref[:, sl]
            r = lax.rsqrt(jnp.mean(yg * yg, axis=-1, keepdims=True) + EPS)
            yh = yg * r
            dw_ref[:, sl] += _rowsum(dog * yh)
            dyh = dog * wg
            dy = r * (dyh - yh * jnp.mean(dyh * yh, axis=-1, keepdims=True))
            dyc_ref[0, :, sl] = dy * sz[:, sl]
            dz_ref[0, :, sl] = (dy * yc[:, sl] * _dsilu(z[:, sl])).astype(bf16)

    return _row_call(body, "ssd_out_bwd", B, S, ts, [_tok(ts, C), _tok(ts, C, 0), _tok(ts, C, 0), _const(1, C)],
                     [_tok(ts, C), _tok(ts, C), _const(1, C)],
                     [jax.ShapeDtypeStruct((B, S, C), f32), jax.ShapeDtypeStruct((B, S, C), bf16),
                      jax.ShapeDtypeStruct((1, C), f32)])(yc3, p3, dcat3, w)


def _rot(t):
    lane = lax.broadcasted_iota(jnp.int32, t.shape, 1)
    return jnp.where(lane < ROPE // 2, -pltpu.roll(t, 128 - ROPE // 2, 1), pltpu.roll(t, ROPE // 2, 1))


def _rope(t, cosf, sinf):
    return t * cosf + _rot(t) * sinf


def _rope_t(d, cosf, sinf):
    return d * cosf - _rot(d * sinf)


def qprep_fwd(qraw3, cosf, sinf):
    B, S, W = qraw3.shape
    ts = _pick(S, (512, 256, 128))

    def body(q_ref, c_ref, s_ref, o_ref):
        c, s = c_ref[0], s_ref[0]
        for h in range(MLA_H):
            o_ref[0, :, h * 256:h * 256 + 128] = q_ref[0, :, h * 256:h * 256 + 128].astype(bf16)
            o_ref[0, :, h * 256 + 128:(h + 1) * 256] = _rope(q_ref[0, :, h * 256 + 128:(h + 1) * 256], c, s).astype(bf16)

    return _row_call(body, "qprep_fwd", B, S, ts, [_tok(ts, W), _tok(ts, 128), _tok(ts, 128)], _tok(ts, W),
                     jax.ShapeDtypeStruct((B, S, W), bf16))(qraw3, cosf, sinf)


def qprep_bwd(dq3, cosf, sinf):
    B, S, W = dq3.shape
    ts = _pick(S, (512, 256, 128))

    def body(d_ref, c_ref, s_ref, o_ref):
        c, s = c_ref[0], s_ref[0]
        for h in range(MLA_H):
            o_ref[0, :, h * 256:h * 256 + 128] = d_ref[0, :, h * 256:h * 256 + 128].astype(bf16)
            o_ref[0, :, h * 256 + 128:(h + 1) * 256] = _rope_t(d_ref[0, :, h * 256 + 128:(h + 1) * 256], c, s).astype(bf16)

    return _row_call(body, "qprep_bwd", B, S, ts, [_tok(ts, W), _tok(ts, 128), _tok(ts, 128)], _tok(ts, W),
                     jax.ShapeDtypeStruct((B, S, W), bf16))(dq3, cosf, sinf)


def kprep_fwd(kv3, p3, cosf, sinf):
    B, S, _ = kv3.shape
    ts = _pick(S, (512, 256, 128))
    Wn = MLA_H * NOPE

    def body(k_ref, v_ref, kr_ref, c_ref, s_ref, ko_ref, vo_ref):
        lane = lax.broadcasted_iota(jnp.int32, (1, 128), 1)
        kr = jnp.where(lane < ROPE, kr_ref[0], 0.0)
        kr = _rope(kr, c_ref[0], s_ref[0]).astype(bf16)
        for h in range(MLA_H):
            ko_ref[0, :, h * 256:h * 256 + 128] = k_ref[0, :, h * 128:(h + 1) * 128].astype(bf16)
            ko_ref[0, :, h * 256 + 128:(h + 1) * 256] = kr
        vo_ref[0] = v_ref[0].astype(bf16)

    return _row_call(body, "kprep_fwd", B, S, ts,
                     [_tok(ts, Wn, 0), _tok(ts, Wn, 1), _tok(ts, 128, OFF_KRDT // 128), _tok(ts, 128), _tok(ts, 128)],
                     [_tok(ts, 2 * Wn), _tok(ts, Wn)],
                     [jax.ShapeDtypeStruct((B, S, 2 * Wn), bf16), jax.ShapeDtypeStruct((B, S, Wn), bf16)])(kv3, kv3, p3, cosf, sinf)


def kprep_bwd(dk3, dv3, ddt3, cosf, sinf):
    B, S, _ = dk3.shape
    ts = _pick(S, (512, 256, 128))
    Wn = MLA_H * NOPE

    def body(dk_ref, dv_ref, ddt_ref, c_ref, s_ref, o_ref, kr_ref):
        acc = jnp.zeros((ts, 128), f32)
        for h in range(MLA_H):
            o_ref[0, :, h * 128:(h + 1) * 128] = dk_ref[0, :, h * 256:h * 256 + 128].astype(bf16)
            acc = acc + dk_ref[0, :, h * 256 + 128:(h + 1) * 256]
        o_ref[0, :, Wn:] = dv_ref[0].astype(bf16)
        lane = lax.broadcasted_iota(jnp.int32, (1, 128), 1)
        dkr = _rope_t(acc, c_ref[0], s_ref[0])
        kr_ref[0] = jnp.where(lane < ROPE, dkr, ddt_ref[0]).astype(bf16)

    return _row_call(body, "kprep_bwd", B, S, ts,
                     [_tok(ts, 2 * Wn), _tok(ts, Wn), _tok(ts, 128), _tok(ts, 128), _tok(ts, 128)],
                     [_tok(ts, 2 * Wn), _tok(ts, 128)],
                     [jax.ShapeDtypeStruct((B, S, 2 * Wn), bf16), jax.ShapeDtypeStruct((B, S, 128), bf16)])(dk3, dv3, ddt3, cosf, sinf)


def _shift_down(u, j):
    if j == 0:
        return u
    row = lax.broadcasted_iota(jnp.int32, u.shape, 0)
    return jnp.where(row < j, 0.0, pltpu.roll(u, j, 0))


def _shift_up(u, j):
    if j == 0:
        return u
    n = u.shape[0]
    row = lax.broadcasted_iota(jnp.int32, u.shape, 0)
    return jnp.where(row >= n - j, 0.0, pltpu.roll(u, n - j, 0))


def _conv(u, w, b, K):
    out = b
    for j in range(K):
        out = out + w[K - 1 - j:K - j, :] * _shift_down(u, j)
    return out


def _conv_bwd(u, du, w, K):
    dins = w[K - 1:K, :] * du
    dws = [None] * K
    dws[K - 1] = _rowsum(du * u)
    for j in range(1, K):
        dins = dins + w[K - 1 - j:K - j, :] * _shift_up(du, j)
        dws[K - 1 - j] = _rowsum(du * _shift_down(u, j))
    return dins, dws


CW = 256


def conv_ssd_fwd(p3, w, b):
    B, S, _ = p3.shape
    nb = D_XBC // CW

    def body(u_ref, w_ref, b_ref, o_ref):
        o_ref[0] = _silu(_conv(u_ref[0], w_ref[...], b_ref[...], CONV_K))

    return pl.pallas_call(body, name="conv_ssd_fwd", grid=(B, nb),
                          in_specs=[pl.BlockSpec((1, S, CW), lambda b, j: (b, 0, OFF_XBC // CW + j)),
                                    pl.BlockSpec((CONV_K, CW), lambda b, j: (0, j)),
                                    pl.BlockSpec((1, CW), lambda b, j: (0, j))],
                          out_specs=pl.BlockSpec((1, S, CW), lambda b, j: (b, 0, j)),
                          out_shape=jax.ShapeDtypeStruct((B, S, D_XBC), f32), compiler_params=_cparams(2))(p3, w, b)


def conv_ssd_bwd(p3, dxc3, w, b):
    B, S, _ = p3.shape
    nb = D_XBC // CW

    def body(u_ref, d_ref, w_ref, b_ref, du_ref, dw_ref, db_ref):
        @pl.when(pl.program_id(1) == 0)
        def _():
            dw_ref[...] = jnp.zeros_like(dw_ref)
            db_ref[...] = jnp.zeros_like(db_ref)

        u, wv = u_ref[0], w_ref[...]
        dpre = d_ref[0] * _dsilu(_conv(u, wv, b_ref[...], CONV_K))
        dins, dws = _conv_bwd(u, dpre, wv, CONV_K)
        du_ref[0] = dins.astype(bf16)
        for k in range(CONV_K):
            dw_ref[k:k + 1, :] += dws[k]
        db_ref[...] += _rowsum(dpre)

    return pl.pallas_call(body, name="conv_ssd_bwd", grid=(nb, B),
                          in_specs=[pl.BlockSpec((1, S, CW), lambda j, b: (b, 0, OFF_XBC // CW + j)),
                                    pl.BlockSpec((1, S, CW), lambda j, b: (b, 0, j)),
                                    pl.BlockSpec((CONV_K, CW), lambda j, b: (0, j)),
                                    pl.BlockSpec((1, CW), lambda j, b: (0, j))],
                          out_specs=[pl.BlockSpec((1, S, CW), lambda j, b: (b, 0, j)),
                                     pl.BlockSpec((CONV_K, CW), lambda j, b: (0, j)),
                                     pl.BlockSpec((1, CW), lambda j, b: (0, j))],
                          out_shape=[jax.ShapeDtypeStruct((B, S, D_XBC), bf16), jax.ShapeDtypeStruct((CONV_K, D_XBC), f32),
                                     jax.ShapeDtypeStruct((1, D_XBC), f32)], compiler_params=_cparams(2))(p3, dxc3, w, b)


def glu_fwd(u3, w, b):
    B, S, _ = u3.shape
    nb = D_FF // CW

    def body(ug_ref, uv_ref, wg_ref, wv_ref, bg_ref, bv_ref, o_ref):
        g = _conv(ug_ref[0].astype(f32), wg_ref[...], bg_ref[...], FF_K)
        v = _conv(uv_ref[0].astype(f32), wv_ref[...], bv_ref[...], FF_K)
        o_ref[0] = (_silu(g) * v).astype(bf16)

    def blk(off):
        return pl.BlockSpec((1, S, CW), lambda b, j: (b, 0, off + j))

    def par(rows, off):
        return pl.BlockSpec((rows, CW), lambda b, j: (0, off + j))

    return pl.pallas_call(body, name="glu_fwd", grid=(B, nb),
                          in_specs=[blk(0), blk(nb), par(FF_K, 0), par(FF_K, nb), par(1, 0), par(1, nb)],
                          out_specs=blk(0), out_shape=jax.ShapeDtypeStruct((B, S, D_FF), bf16),
                          compiler_params=_cparams(2))(u3, u3, w, w, b, b)


def glu_bwd(u3, da3, w, b):
    B, S, _ = u3.shape
    nb = D_FF // CW

    def body(ug_ref, uv_ref, da_ref, wg_ref, wv_ref, bg_ref, bv_ref, dug_ref, duv_ref, dwg_ref, dwv_ref, dbg_ref, dbv_ref):
        @pl.when(pl.program_id(1) == 0)
        def _():
            for r in (dwg_ref, dwv_ref, dbg_ref, dbv_ref):
                r[...] = jnp.zeros_like(r)

        ug, uv, da, wg, wv = ug_ref[0].astype(f32), uv_ref[0].astype(f32), da_ref[0], wg_ref[...], wv_ref[...]
        g = _conv(ug, wg, bg_ref[...], FF_K)
        v = _conv(uv, wv, bv_ref[...], FF_K)
        dg = da * v * _dsilu(g)
        dv = da * _silu(g)
        ding, dwsg = _conv_bwd(ug, dg, wg, FF_K)
        dinv, dwsv = _conv_bwd(uv, dv, wv, FF_K)
        dug_ref[0] = ding.astype(bf16)
        duv_ref[0] = dinv.astype(bf16)
        for k in range(FF_K):
            dwg_ref[k:k + 1, :] += dwsg[k]
            dwv_ref[k:k + 1, :] += dwsv[k]
        dbg_ref[...] += _rowsum(dg)
        dbv_ref[...] += _rowsum(dv)

    def blk(off):
        return pl.BlockSpec((1, S, CW), lambda j, b: (b, 0, off + j))

    def par(rows, off):
        return pl.BlockSpec((rows, CW), lambda j, b: (0, off + j))

    return pl.pallas_call(body, name="glu_bwd", grid=(nb, B),
                          in_specs=[blk(0), blk(nb), blk(0), par(FF_K, 0), par(FF_K, nb), par(1, 0), par(1, nb)],
                          out_specs=[blk(0), blk(0), par(FF_K, 0), par(FF_K, 0), par(1, 0), par(1, 0)],
                          out_shape=[jax.ShapeDtypeStruct((B, S, D_FF), bf16), jax.ShapeDtypeStruct((B, S, D_FF), bf16),
                                     jax.ShapeDtypeStruct((FF_K, D_FF), f32), jax.ShapeDtypeStruct((FF_K, D_FF), f32),
                                     jax.ShapeDtypeStruct((1, D_FF), f32), jax.ShapeDtypeStruct((1, D_FF), f32)],
                          compiler_params=_cparams(2))(u3, u3, da3, w, w, b, b)


def _ssd_decay(dtb, bias_row, alog_row):
    lane = lax.broadcasted_iota(jnp.int32, (1, 128), 1)
    hmask = (lane >= DT_LANE) & (lane < DT_LANE + SSD_HEADS)
    dt = jnp.where(hmask, jax.nn.softplus(dtb + bias_row), 0.0)
    a = dt * jnp.where(hmask, -jnp.exp(alog_row), 0.0)
    r = lax.broadcasted_iota(jnp.int32, (CHUNK, CHUNK), 0)
    c = lax.broadcasted_iota(jnp.int32, (CHUNK, CHUNK), 1)
    cs = jnp.dot((r >= c).astype(f32), a, precision=HIGHEST, preferred_element_type=f32)
    return dt, cs


def _expand(xt):
    return jnp.concatenate([jnp.broadcast_to(xt[DT_LANE + h:DT_LANE + h + 1, :], (SSD_HD, xt.shape[1]))
                            for h in range(SSD_HEADS)], axis=0)


_NT = (((1,), (1,)), ((), ()))
_TN = (((0,), (0,)), ((), ()))
GH = SSD_HEADS // 2
GR = GH * SSD_HD


def ssd_fwd(xc3, p3, bias_row, alog_row, dcol):
    B, S, _ = xc3.shape
    nc = S // CHUNK

    def body(xs_ref, bc_ref, dtb_ref, bias_ref, alog_ref, dcol_ref, y_ref, st_ref, state, yT):
        @pl.when(pl.program_id(1) == 0)
        def _():
            state[...] = jnp.zeros_like(state)

        dt, cs = _ssd_decay(dtb_ref[0], bias_ref[...], alog_ref[...])
        csT = cs.T
        eT = jnp.exp(csT)
        decX = _expand(jnp.exp(csT[:, CHUNK - 1:CHUNK] - csT))
        eX = _expand(eT)
        elastX = eX[:, CHUNK - 1:CHUNK]
        xsT = xs_ref[0].T
        uT = xsT * _expand(dt.T)
        bc = bc_ref[0]
        st_ref[0, 0] = state[...]
        srow = lax.broadcasted_iota(jnp.int32, (CHUNK, CHUNK), 0)
        lcol = lax.broadcasted_iota(jnp.int32, (CHUNK, CHUNK), 1)
        for g in range(2):
            Bg = bc[:, g * SSD_N:(g + 1) * SSD_N].astype(bf16)
            Cg = bc[:, (2 + g) * SSD_N:(3 + g) * SSD_N].astype(bf16)
            GT = lax.dot_general(Bg, Cg, _NT, preferred_element_type=f32)
            rows = slice(g * GR, (g + 1) * GR)
            Sg = state[rows]
            yoffT = lax.dot_general(Sg.astype(bf16), Cg, _NT, preferred_element_type=f32) * eX[rows]
            state[rows] = Sg * elastX[rows] + jnp.dot((uT[rows] * decX[rows]).astype(bf16), Bg, preferred_element_type=f32)
            for k in range(GH):
                h = g * GH + k
                hr = slice(h * SSD_HD, (h + 1) * SSD_HD)
                seg = csT[DT_LANE + h:DT_LANE + h + 1, :] - cs[:, DT_LANE + h:DT_LANE + h + 1]
                LT = jnp.where(lcol >= srow, jnp.exp(jnp.minimum(seg, 0.0)), 0.0)
                yT[hr] = (jnp.dot(uT[hr].astype(bf16), (GT * LT).astype(bf16), preferred_element_type=f32)
                          + yoffT[k * SSD_HD:(k + 1) * SSD_HD] + dcol_ref[hr] * xsT[hr])
        y_ref[0] = yT[...].T

    return pl.pallas_call(body, name="ssd_fwd", grid=(B, nc),
                          in_specs=[pl.BlockSpec((1, CHUNK, D_SSD), lambda b, c: (b, c, 0)),
                                    pl.BlockSpec((1, CHUNK, 512), lambda b, c: (b, c, 2)),
                                    pl.BlockSpec((1, CHUNK, 128), lambda b, c: (b, c, OFF_KRDT // 128)),
                                    _const(1, 128), _const(1, 128), _const(D_SSD, 1)],
                          out_specs=[pl.BlockSpec((1, CHUNK, D_SSD), lambda b, c: (b, c, 0)),
                                     pl.BlockSpec((1, 1, D_SSD, SSD_N), lambda b, c: (b, c, 0, 0))],
                          out_shape=[jax.ShapeDtypeStruct((B, S, D_SSD), f32), jax.ShapeDtypeStruct((B, nc, D_SSD, SSD_N), f32)],
                          scratch_shapes=[pltpu.VMEM((D_SSD, SSD_N), f32), pltpu.VMEM((D_SSD, CHUNK), f32)],
                          compiler_params=_cparams(2))(xc3, xc3, p3, bias_row, alog_row, dcol)


def ssd_bwd(xc3, p3, dy3, states, bias_row, alog_row, bias_col, alog_col, dcol):
    B, S, _ = xc3.shape
    nc = S // CHUNK

    def body(xs_ref, bc_ref, dtb_ref, dy_ref, st_ref, bias_ref, alog_ref, biasc_ref, alogc_ref, dcol_ref,
             dxc_ref, ddt_ref, dalog_ref, dd_ref, dbias_ref, dS, dUT, accA, accD, accB, dcs_diag):
        @pl.when(pl.program_id(1) == 0)
        def _():
            dS[...] = jnp.zeros_like(dS)

        @pl.when(_first())
        def _():
            accA[...] = jnp.zeros_like(accA)
            accD[...] = jnp.zeros_like(accD)
            accB[...] = jnp.zeros_like(accB)

        dtb = dtb_ref[0]
        dt, cs = _ssd_decay(dtb, bias_ref[...], alog_ref[...])
        dtT, csT = dt.T, cs.T
        decX = _expand(jnp.exp(csT[:, CHUNK - 1:CHUNK] - csT))
        eX = _expand(jnp.exp(csT))
        dtX = _expand(dtT)
        elastX = eX[:, CHUNK - 1:CHUNK]
        xsT = xs_ref[0].T
        uT = xsT * dtX
        dYT = dy_ref[0].T
        bc = bc_ref[0]
        lrow = lax.broadcasted_iota(jnp.int32, (CHUNK, CHUNK), 0)
        scol = lax.broadcasted_iota(jnp.int32, (CHUNK, CHUNK), 1)
        dcs_diag[...] = jnp.zeros_like(dcs_diag)
        vparts, zparts = [], []
        for g in range(2):
            Bf = bc[:, g * SSD_N:(g + 1) * SSD_N]
            Bg = Bf.astype(bf16)
            Cg = bc[:, (2 + g) * SSD_N:(3 + g) * SSD_N].astype(bf16)
            G = lax.dot_general(Cg, Bg, _NT, preferred_element_type=f32)
            BgT = Bf.T.astype(bf16)
            rows = slice(g * GR, (g + 1) * GR)
            dSg = dS[rows]
            Sg = st_ref[0, 0, rows, :]
            dUst = jnp.dot(dSg.astype(bf16), BgT, preferred_element_type=f32) * decX[rows]
            yoffT = lax.dot_general(Sg.astype(bf16), Cg, _NT, preferred_element_type=f32) * eX[rows]
            zparts.append(dYT[rows] * yoffT - dUst * uT[rows])
            dG = jnp.zeros((CHUNK, CHUNK), f32)
            for k in range(GH):
                h = g * GH + k
                hr = slice(h * SSD_HD, (h + 1) * SSD_HD)
                seg = cs[:, DT_LANE + h:DT_LANE + h + 1] - csT[DT_LANE + h:DT_LANE + h + 1, :]
                L = jnp.where(lrow >= scol, jnp.exp(jnp.minimum(seg, 0.0)), 0.0)
                M = G * L
                dYh = dYT[hr].astype(bf16)
                dUT[hr] = jnp.dot(dYh, M.astype(bf16), preferred_element_type=f32) + dUst[k * SSD_HD:(k + 1) * SSD_HD]
                dM = lax.dot_general(dYh, uT[hr].astype(bf16), _TN, preferred_element_type=f32)
                dG = dG + dM * L
                Wm = dM * M
                rs = lax.dot_general(jnp.ones((8, CHUNK), f32), Wm, _NT, precision=HIGHEST, preferred_element_type=f32)[0:1]
                dcs_diag[DT_LANE + h:DT_LANE + h + 1, :] = rs - _rowsum(Wm)
            dGb = dG.astype(bf16)
            dYe = (dYT[rows] * eX[rows]).astype(bf16)
            ude = (uT[rows] * decX[rows]).astype(bf16)
            dC = jnp.dot(dGb, Bg, preferred_element_type=f32) + lax.dot_general(dYe, Sg.astype(bf16), _TN, preferred_element_type=f32)
            dB = (lax.dot_general(dGb, Cg, _TN, preferred_element_type=f32)
                  + lax.dot_general(ude, dSg.astype(bf16), _TN, preferred_element_type=f32))
            dxc_ref[0, :, D_SSD + g * SSD_N:D_SSD + (g + 1) * SSD_N] = dB
            dxc_ref[0, :, D_SSD + (2 + g) * SSD_N:D_SSD + (3 + g) * SSD_N] = dC
            vparts.append(elastX[rows] * jnp.sum(dSg * Sg, axis=1, keepdims=True)
                          + jnp.sum(dUst * uT[rows], axis=1, keepdims=True))
            dS[rows] = elastX[rows] * dSg + jnp.dot(dYe, Cg, preferred_element_type=f32)
        dU = dUT[...]
        dcv = dcol_ref[...]
        dxc_ref[0, :, 0:D_SSD] = (dtX * dU + dcv * dYT).T
        lane = lax.broadcasted_iota(jnp.int32, (D_SSD, CHUNK), 1)
        Z = jnp.concatenate(zparts, axis=0) + jnp.where(lane == CHUNK - 1, jnp.concatenate(vparts, axis=0), 0.0)
        hr_ = lax.broadcasted_iota(jnp.int32, (128, D_SSD), 0)
        hc_ = lax.broadcasted_iota(jnp.int32, (128, D_SSD), 1)
        hsel = (hr_ - DT_LANE == jnp.right_shift(hc_, 6)).astype(f32)
        red = jnp.dot(hsel, jnp.concatenate([Z, dU * xsT, dYT * xsT], axis=1), precision=HIGHEST, preferred_element_type=f32)
        dcsT = red[:, 0:CHUNK] + dcs_diag[...]
        daT = jnp.dot(dcsT, (lrow >= scol).astype(f32), precision=HIGHEST, preferred_element_type=f32)
        rowi = lax.broadcasted_iota(jnp.int32, (128, 1), 0)
        hmask = (rowi >= DT_LANE) & (rowi < DT_LANE + SSD_HEADS)
        a_col = jnp.where(hmask, -jnp.exp(alogc_ref[...]), 0.0)
        ddtT = red[:, CHUNK:2 * CHUNK] + a_col * daT
        ddt_rawT = jnp.where(hmask, ddtT * jax.nn.sigmoid(dtb.T + biasc_ref[...]), 0.0)
        ddt_ref[0] = ddt_rawT.T
        accA[...] += daT * dtT
        accD[...] += red[:, 2 * CHUNK:3 * CHUNK]
        accB[...] += ddt_rawT

        @pl.when((pl.program_id(0) == B - 1) & (pl.program_id(1) == nc - 1))
        def _():
            dalog_ref[...] = jnp.broadcast_to(jnp.sum(accA[...], axis=1, keepdims=True) * a_col, (128, 128))
            dd_ref[...] = jnp.broadcast_to(jnp.sum(accD[...], axis=1, keepdims=True), (128, 128))
            dbias_ref[...] = jnp.broadcast_to(jnp.sum(accB[...], axis=1, keepdims=True), (128, 128))

    def rev(width, cb):
        return pl.BlockSpec((1, CHUNK, width), lambda b, c: (b, nc - 1 - c, cb))

    acc_spec = pl.BlockSpec((128, 128), lambda b, c: (0, 0))
    acc_shape = jax.ShapeDtypeStruct((128, 128), f32)
    return pl.pallas_call(body, name="ssd_bwd", grid=(B, nc),
                          in_specs=[rev(D_SSD, 0), rev(512, 2), rev(128, OFF_KRDT // 128), rev(D_SSD, 0),
                                    pl.BlockSpec((1, 1, D_SSD, SSD_N), lambda b, c: (b, nc - 1 - c, 0, 0)),
                                    _const(1, 128), _const(1, 128), _const(128, 1), _const(128, 1), _const(D_SSD, 1)],
                          out_specs=[rev(D_XBC, 0), rev(128, 0), acc_spec, acc_spec, acc_spec],
                          out_shape=[jax.ShapeDtypeStruct((B, S, D_XBC), f32), jax.ShapeDtypeStruct((B, S, 128), f32),
                                     acc_shape, acc_shape, acc_shape],
                          scratch_shapes=[pltpu.VMEM((D_SSD, SSD_N), f32), pltpu.VMEM((D_SSD, CHUNK), f32),
                                          pltpu.VMEM((128, 128), f32), pltpu.VMEM((128, 128), f32), pltpu.VMEM((128, 128), f32),
                                          pltpu.VMEM((128, 128), f32)],
                          compiler_params=_cparams(2))(xc3, xc3, p3, dy3, states, bias_row, alog_row, bias_col, alog_col, dcol)


ATT_SCALE = float(QK) ** -0.5
NEG = -1e30
HP = 2


def _att_block(S):
    return _pick(S, (512, 256, 128))


def attn_fwd(q3, k3, v3):
    B, S, _ = q3.shape
    bq = _att_block(S)
    nq = S // bq

    def body(q_ref, k_ref, v_ref, o_ref, lse_ref, m_s, l_s, acc):
        i, j = pl.program_id(2), pl.program_id(3)

        @pl.when(j == 0)
        def _():
            m_s[...] = jnp.full_like(m_s, NEG)
            l_s[...] = jnp.zeros_like(l_s)
            acc[...] = jnp.zeros_like(acc)

        def step(masked):
            for t in range(HP):
                qk = slice(t * 256, (t + 1) * 256)
                s = lax.dot_general(q_ref[0, :, qk], k_ref[0, :, qk], _NT, preferred_element_type=f32) * ATT_SCALE
                if masked:
                    r = lax.broadcasted_iota(jnp.int32, (bq, bq), 0)
                    c = lax.broadcasted_iota(jnp.int32, (bq, bq), 1)
                    s = jnp.where(r >= c, s, NEG)
                m_old = m_s[t]
                m_new = jnp.maximum(m_old, jnp.max(s, axis=-1, keepdims=True))
                alpha = jnp.exp(m_old - m_new)
                p = jnp.exp(s - m_new)
                l_s[t] = alpha * l_s[t] + jnp.sum(p, axis=-1, keepdims=True)
                acc[t] = alpha * acc[t] + jnp.dot(p.astype(bf16), v_ref[0, :, t * VD:(t + 1) * VD], preferred_element_type=f32)
                m_s[t] = m_new

        @pl.when(j < i)
        def _():
            step(False)

        @pl.when(j == i)
        def _():
            step(True)
            for t in range(HP):
                o_ref[0, :, t * VD:(t + 1) * VD] = acc[t] / l_s[t]
                lse_ref[0, t] = m_s[t] + jnp.log(l_s[t])

    return pl.pallas_call(body, name="attn_fwd", grid=(B, MLA_H // HP, nq, nq),
                          in_specs=[pl.BlockSpec((1, bq, HP * 256), lambda b, h, i, j: (b, i, h)),
                                    pl.BlockSpec((1, bq, HP * 256), lambda b, h, i, j: (b, jnp.minimum(j, i), h)),
                                    pl.BlockSpec((1, bq, HP * VD), lambda b, h, i, j: (b, jnp.minimum(j, i), h))],
                          out_specs=[pl.BlockSpec((1, bq, HP * VD), lambda b, h, i, j: (b, i, h)),
                                     pl.BlockSpec((1, HP, bq, 1), lambda b, h, i, j: (b, h, i, 0))],
                          out_shape=[jax.ShapeDtypeStruct((B, S, MLA_H * VD), f32), jax.ShapeDtypeStruct((B, MLA_H, S, 1), f32)],
                          scratch_shapes=[pltpu.VMEM((HP, bq, 1), f32), pltpu.VMEM((HP, bq, 1), f32), pltpu.VMEM((HP, bq, VD), f32)],
                          compiler_params=_cparams(4))(q3, k3, v3)


def attn_bwd_dq(q3, k3, v3, o3, do3, lse):
    B, S, _ = q3.shape
    bq = _att_block(S)
    nq = S // bq

    def body(q_ref, k_ref, v_ref, o_ref, do_ref, lse_ref, dq_ref, dl_ref, acc, dl_s):
        i, j = pl.program_id(2), pl.program_id(3)

        @pl.when(j == 0)
        def _():
            acc[...] = jnp.zeros_like(acc)
            for t in range(HP):
                vs = slice(t * VD, (t + 1) * VD)
                dl_s[t] = jnp.sum(o_ref[0, :, vs] * do_ref[0, :, vs], axis=-1, keepdims=True)

        def step(masked):
            for t in range(HP):
                qk, vs = slice(t * 256, (t + 1) * 256), slice(t * VD, (t + 1) * VD)
                k = k_ref[0, :, qk]
                s = lax.dot_general(q_ref[0, :, qk], k, _NT, preferred_element_type=f32) * ATT_SCALE
                p = jnp.exp(s - lse_ref[0, t])
                if masked:
                    r = lax.broadcasted_iota(jnp.int32, (bq, bq), 0)
                    c = lax.broadcasted_iota(jnp.int32, (bq, bq), 1)
                    p = jnp.where(r >= c, p, 0.0)
                dp = lax.dot_general(do_ref[0, :, vs].astype(bf16), v_ref[0, :, vs], _NT, preferred_element_type=f32)
                ds = p * (dp - dl_s[t]) * ATT_SCALE
                acc[t] += jnp.dot(ds.astype(bf16), k, preferred_element_type=f32)

        @pl.when(j < i)
        def _():
            step(False)

        @pl.when(j == i)
        def _():
            step(True)
            for t in range(HP):
                dq_ref[0, :, t * 256:(t + 1) * 256] = acc[t]
                dl_ref[0, t] = dl_s[t]

    qspec = pl.BlockSpec((1, bq, HP * 256), lambda b, h, i, j: (b, i, h))
    ospec = pl.BlockSpec((1, bq, HP * VD), lambda b, h, i, j: (b, i, h))
    cspec = pl.BlockSpec((1, HP, bq, 1), lambda b, h, i, j: (b, h, i, 0))
    return pl.pallas_call(body, name="attn_bwd_dq", grid=(B, MLA_H // HP, nq, nq),
                          in_specs=[qspec, pl.BlockSpec((1, bq, HP * 256), lambda b, h, i, j: (b, jnp.minimum(j, i), h)),
                                    pl.BlockSpec((1, bq, HP * VD), lambda b, h, i, j: (b, jnp.minimum(j, i), h)), ospec, ospec, cspec],
                          out_specs=[qspec, cspec],
                          out_shape=[jax.ShapeDtypeStruct((B, S, MLA_H * 256), f32), jax.ShapeDtypeStruct((B, MLA_H, S, 1), f32)],
                          scratch_shapes=[pltpu.VMEM((HP, bq, 256), f32), pltpu.VMEM((HP, bq, 1), f32)],
                          compiler_params=_cparams(4))(q3, k3, v3, o3, do3, lse)


def attn_bwd_dkv(q3, k3, v3, do3, lse_row, delta_row):
    B, S, _ = q3.shape
    bq = _att_block(S)
    nq = S // bq

    def body(q_ref, k_ref, v_ref, do_ref, lse_ref, dl_ref, dk_ref, dv_ref, dk_acc, dv_acc):
        j, i = pl.program_id(2), pl.program_id(3)

        @pl.when(i == 0)
        def _():
            dk_acc[...] = jnp.zeros_like(dk_acc)
            dv_acc[...] = jnp.zeros_like(dv_acc)

        def step(masked):
            for t in range(HP):
                qk, vs = slice(t * 256, (t + 1) * 256), slice(t * VD, (t + 1) * VD)
                q = q_ref[0, :, qk]
                do = do_ref[0, :, vs].astype(bf16)
                st = lax.dot_general(k_ref[0, :, qk], q, _NT, preferred_element_type=f32) * ATT_SCALE
                pt = jnp.exp(st - lse_ref[0, t])
                if masked:
                    r = lax.broadcasted_iota(jnp.int32, (bq, bq), 0)
                    c = lax.broadcasted_iota(jnp.int32, (bq, bq), 1)
                    pt = jnp.where(c >= r, pt, 0.0)
                dv_acc[t] += jnp.dot(pt.astype(bf16), do, preferred_element_type=f32)
                dpt = lax.dot_general(v_ref[0, :, vs], do, _NT, preferred_element_type=f32)
                dst = pt * (dpt - dl_ref[0, t]) * ATT_SCALE
                dk_acc[t] += jnp.dot(dst.astype(bf16), q, preferred_element_type=f32)

        @pl.when(i > j)
        def _():
            step(False)

        @pl.when(i == j)
        def _():
            step(True)

        @pl.when(i == nq - 1)
        def _():
            for t in range(HP):
                dk_ref[0, :, t * 256:(t + 1) * 256] = dk_acc[t]
                dv_ref[0, :, t * VD:(t + 1) * VD] = dv_acc[t]

    kspec = pl.BlockSpec((1, bq, HP * 256), lambda b, h, j, i: (b, j, h))
    vspec = pl.BlockSpec((1, bq, HP * VD), lambda b, h, j, i: (b, j, h))
    rspec = pl.BlockSpec((1, HP, 1, bq), lambda b, h, j, i: (b, h, 0, jnp.maximum(i, j)))
    return pl.pallas_call(body, name="attn_bwd_dkv", grid=(B, MLA_H // HP, nq, nq),
                          in_specs=[pl.BlockSpec((1, bq, HP * 256), lambda b, h, j, i: (b, jnp.maximum(i, j), h)), kspec, vspec,
                                    pl.BlockSpec((1, bq, HP * VD), lambda b, h, j, i: (b, jnp.maximum(i, j), h)), rspec, rspec],
                          out_specs=[kspec, vspec],
                          out_shape=[jax.ShapeDtypeStruct((B, S, MLA_H * 256), f32), jax.ShapeDtypeStruct((B, S, MLA_H * VD), f32)],
                          scratch_shapes=[pltpu.VMEM((HP, bq, 256), f32), pltpu.VMEM((HP, bq, VD), f32)],
                          compiler_params=_cparams(4))(q3, k3, v3, do3, lse_row, delta_row)


def ada_fwd(c_all, w, b):
    n = w.shape[1]

    def body(c_ref, w_ref, b_ref, o_ref):
        o_ref[...] = jnp.dot(_silu(c_ref[...]).astype(bf16), w_ref[...].astype(bf16), preferred_element_type=f32) + b_ref[...]

    return pl.pallas_call(body, name="ada_fwd", out_shape=jax.ShapeDtypeStruct((c_all.shape[0], n), f32),
                          compiler_params=pltpu.CompilerParams(vmem_limit_bytes=VMEM_LIMIT))(c_all, w, b)


def ada_bwd(c_all, dmod):
    n = dmod.shape[1]

    def body(c_ref, d_ref, o_ref):
        o_ref[...] = lax.dot_general(_silu(c_ref[...]).astype(bf16), d_ref[...].astype(bf16), _TN, preferred_element_type=f32)

    return pl.pallas_call(body, name="ada_bwd", out_shape=jax.ShapeDtypeStruct((c_all.shape[1], n), f32),
                          compiler_params=pltpu.CompilerParams(vmem_limit_bytes=VMEM_LIMIT))(c_all, dmod)


def sum_leading(x, name):
    n, R, _ = x.shape
    tr = _pick(R, (512, 256, 128, 64, 32, 16, 8))

    def body(x_ref, o_ref):
        acc = x_ref[0].astype(f32)
        for k in range(1, n):
            acc = acc + x_ref[k].astype(f32)
        o_ref[...] = acc

    return pl.pallas_call(body, name=name, grid=(R // tr,), in_specs=[pl.BlockSpec((n, tr, 128), lambda i: (0, i, 0))],
                          out_specs=pl.BlockSpec((tr, 128), lambda i: (i, 0)), out_shape=jax.ShapeDtypeStruct((R, 128), f32),
                          compiler_params=_cparams(1))(x)


def _adamw_body(w_ref, g_ref, m_ref, v_ref, d_ref, mo_ref, vo_ref):
    gv = g_ref[...]
    mn = ADAM_B1 * m_ref[...] + (1.0 - ADAM_B1) * gv
    vn = ADAM_B2 * v_ref[...] + (1.0 - ADAM_B2) * jnp.square(gv)
    m_hat = mn / (1.0 - ADAM_B1 ** ADAM_STEP)
    v_hat = vn / (1.0 - ADAM_B2 ** ADAM_STEP)
    d_ref[...] = -ADAM_LR * (m_hat / (jnp.sqrt(v_hat) + ADAM_EPS) + ADAM_WD * w_ref[...])
    mo_ref[...] = mn
    vo_ref[...] = vn


def adamw(w, g, m, v):
    R = w.shape[0]
    tr = _pick(R, (512, 256, 128, 64, 32, 16, 8))
    spec = pl.BlockSpec((tr, 128), lambda i: (i, 0))
    shp = jax.ShapeDtypeStruct((R, 128), f32)
    return pl.pallas_call(functools.partial(_adamw_body), name="adamw", grid=(R // tr,), in_specs=[spec] * 4,
                          out_specs=[spec] * 3, out_shape=[shp] * 3, compiler_params=_cparams(1))(w, g, m, v)


def _row_tile(a):
    return _pick(a, (256, 128, 64, 32, 16, 8)) if a % 8 == 0 else a


def adamw_nd(w, g, m, v):
    L, a, b = w.shape
    ta = _row_tile(a)
    spec = pl.BlockSpec((1, ta, b), lambda l, i: (l, i, 0))
    shp = jax.ShapeDtypeStruct((L, a, b), f32)
    return pl.pallas_call(functools.partial(_adamw_body), name="adamw_nd", grid=(L, a // ta), in_specs=[spec] * 4,
                          out_specs=[spec] * 3, out_shape=[shp] * 3, compiler_params=_cparams(2))(w, g, m, v)


def sum_slots(x):
    n, L, a, b = x.shape
    ta = _row_tile(a)

    def body(x_ref, o_ref):
        acc = x_ref[0].astype(f32)
        for k in range(1, n):
            acc = acc + x_ref[k].astype(f32)
        o_ref[...] = acc

    return pl.pallas_call(body, name="sum_slots", grid=(L, a // ta),
                          in_specs=[pl.BlockSpec((n, 1, ta, b), lambda l, i: (0, l, i, 0))],
                          out_specs=pl.BlockSpec((1, ta, b), lambda l, i: (l, i, 0)),
                          out_shape=jax.ShapeDtypeStruct((L, a, b), f32), compiler_params=_cparams(2))(x)


def exchange(arrays, scatter, name):
    n = len(arrays)
    blocks = [a.shape[1:] if scatter else a.shape for a in arrays]

    def body(*refs):
        ins, outs = refs[:n], refs[n:2 * n]
        send_sems, recv_sems, local_sems = refs[2 * n:]
        x, y, c = lax.axis_index("x"), lax.axis_index("y"), lax.axis_index("c")
        me = 4 * x + 2 * y + c
        peers = []
        for k in range(1, N_DEV):
            px = 1 - x if k & 4 else x
            py = 1 - y if k & 2 else y
            pc = 1 - c if k & 1 else c
            peers.append(((px, py, pc), 4 * px + 2 * py + pc))
        locals_, sends = [], []
        for a in range(n):
            src = ins[a].at[me] if scatter else ins[a]
            lc = pltpu.make_async_copy(src, outs[a].at[me], local_sems.at[a])
            lc.start()
            locals_.append(lc)
            for k, (peer, pid) in enumerate(peers):
                cp = pltpu.make_async_remote_copy(src_ref=ins[a].at[pid] if scatter else ins[a], dst_ref=outs[a].at[me],
                                                  send_sem=send_sems.at[a, k], recv_sem=recv_sems.at[a, k],
                                                  device_id=peer, device_id_type=pl.DeviceIdType.MESH)
                cp.start()
                sends.append(cp)
        for a in range(n):
            for k, (peer, pid) in enumerate(peers):
                pltpu.make_async_remote_copy(src_ref=ins[a].at[pid] if scatter else ins[a], dst_ref=outs[a].at[pid],
                                             send_sem=send_sems.at[a, k], recv_sem=recv_sems.at[a, k],
                                             device_id=peer, device_id_type=pl.DeviceIdType.MESH).wait_recv()
        for cp in sends:
            cp.wait_send()
        for lc in locals_:
            lc.wait()

    hbm = pl.BlockSpec(memory_space=pltpu.HBM)
    return pl.pallas_call(body, name=name, in_specs=[hbm] * n, out_specs=[hbm] * n,
                          out_shape=[jax.ShapeDtypeStruct((N_DEV,) + tuple(b), a.dtype) for a, b in zip(arrays, blocks)],
                          scratch_shapes=[pltpu.SemaphoreType.DMA((n, N_DEV - 1)), pltpu.SemaphoreType.DMA((n, N_DEV - 1)),
                                          pltpu.SemaphoreType.DMA((n,))],
                          )(*arrays)


BIG = (("w_in", "col"), ("conv_w", "col"), ("w_uq", "col"), ("w_ukv", "col"), ("w_out", "row"), ("w_up", "col"),
       ("conv_ff_w", "col"), ("w_down", "row"))
SMALL = ("b_ada", "norm_mix", "conv_b", "dt_bias", "a_log", "d_skip", "ssd_norm", "q_norm", "kv_norm", "attn_norm",
         "norm_mlp", "conv_ff_b", "final_norm")
CONVS = ("conv_w", "conv_ff_w")
PACK_ALIGN = 2048


def _padded(n):
    return -(-n // PACK_ALIGN) * PACK_ALIGN


def _flat_pad(a):
    f = a.reshape(-1)
    return jnp.pad(f, (0, _padded(f.shape[0]) - f.shape[0]))


PACK_ROWS = 512


def pack(arrs):
    f = jnp.concatenate([_flat_pad(a) for a in arrs])
    n = PACK_ROWS * 128
    return jnp.pad(f, (0, -(-f.shape[0] // n) * n - f.shape[0])).reshape(-1, 128)


def unpack(flat, shapes):
    f = flat.reshape(-1)
    out, off = [], 0
    for s in shapes:
        n = int(np.prod(s))
        out.append(f[off:off + n].reshape(s))
        off += _padded(n)
    return out


def shards_to_full(g, shard_shape, kind):
    L, a, b = shard_shape
    if kind == "col":
        return g.transpose(1, 2, 0, 3).reshape(L, a, N_DEV * b)
    return g.transpose(1, 0, 2, 3).reshape(L, N_DEV * a, b)


def full_to_shards(full, shard_shape, kind):
    L, a, b = shard_shape
    if kind == "col":
        return full.reshape(L, a, N_DEV, b).transpose(2, 0, 1, 3)
    return full.reshape(L, N_DEV, a, b).transpose(1, 0, 2, 3)


def w_in_layout(w):
    z = lambda n: jnp.zeros(w.shape[:-1] + (n,), w.dtype)
    return jnp.concatenate([w[..., :2560], w[..., 2576:2960], z(128), w[..., 2960:3216], w[..., 3216:3280],
                            w[..., 2560:2576], z(48)], axis=-1)


def w_in_unlayout(g):
    return jnp.concatenate([g[..., :2560], g[..., 3392:3408], g[..., 2560:2944], g[..., 3072:3328], g[..., 3328:3392]], axis=-1)


def w_uq_layout(w):
    return jnp.pad(w.reshape(Q_RANK, MLA_H, QK), ((0, 0), (0, 0), (0, 256 - QK))).reshape(Q_RANK, MLA_H * 256)


def w_uq_unlayout(g):
    return g.reshape(Q_RANK, MLA_H, 256)[:, :, :QK].reshape(Q_RANK, MLA_H * QK)


def w_ukv_layout(w):
    return w.reshape(KV_RANK, MLA_H, 2, 128).transpose(0, 2, 1, 3).reshape(KV_RANK, 2 * MLA_H * 128)


def w_ukv_unlayout(g):
    return g.reshape(KV_RANK, 2, MLA_H, 128).transpose(0, 2, 1, 3).reshape(KV_RANK, 2 * MLA_H * 128)


def _head_row(v):
    return jnp.zeros((1, 128), f32).at[0, DT_LANE:DT_LANE + SSD_HEADS].set(v)


def layer_fwd(x3, mod, W, P, l, cosf, sinf):
    B, S, _ = x3.shape
    T = B * S
    sv = {}
    h = normmod_fwd(x3, mod, P["norm_mix"][l][None], 0, 1)
    p = mm(h.reshape(T, D), W["w_in"][l], "nn", "mm_in")
    p3 = p.reshape(B, S, IN_COLS)
    bias_row, alog_row = _head_row(P["dt_bias"][l]), _head_row(P["a_log"][l])
    dcol = jnp.repeat(P["d_skip"][l], SSD_HD)[:, None]
    xc3 = conv_ssd_fwd(p3, P["conv_w"][l], P["conv_b"][l][None])
    yc3, states = ssd_fwd(xc3, p3, bias_row, alog_row, dcol)
    y_ssd = ssd_out_fwd(yc3, p3, P["ssd_norm"][l][None])
    cqn = rms_fwd(p3, 512, OFF_CQ // 512, Q_RANK, P["q_norm"][l][None], "rms_q_fwd")
    ckvn = rms_fwd(p3, KV_RANK, OFF_CKV // KV_RANK, KV_RANK, P["kv_norm"][l][None], "rms_kv_fwd")
    qraw = mm(cqn.reshape(T, Q_RANK), W["w_uq"][l], "nn", "mm_uq")
    kvraw = mm(ckvn.reshape(T, KV_RANK), W["w_ukv"][l], "nn", "mm_ukv")
    q3 = qprep_fwd(qraw.reshape(B, S, -1), cosf, sinf)
    k3, v3 = kprep_fwd(kvraw.reshape(B, S, -1), p3, cosf, sinf)
    o3, lse = attn_fwd(q3, k3, v3)
    y_att = rms_fwd(o3, D, 0, D, P["attn_norm"][l][None], "rms_o_fwd")
    cat = jnp.concatenate([y_ssd, y_att], axis=-1).reshape(T, 2 * D)
    x1, y1 = mm(cat, W["w_out"][l], "nn", "mm_out", resid=x3.reshape(T, D), gate=mod[:, 2:3, :], seq=S)
    x13 = x1.reshape(B, S, D)
    h2 = normmod_fwd(x13, mod, P["norm_mlp"][l][None], 3, 4)
    u = mm(h2.reshape(T, D), W["w_up"][l], "nn", "mm_up", out_dtype=bf16)
    u3 = u.reshape(B, S, 2 * D_FF)
    a = glu_fwd(u3, P["conv_ff_w"][l], P["conv_ff_b"][l][None])
    x2, y2 = mm(a.reshape(T, D_FF), W["w_down"][l], "nn", "mm_down", resid=x1, gate=mod[:, 5:6, :], seq=S)
    sv.update(x=x3, h=h, p3=p3, xc3=xc3, yc3=yc3, states=states, cqn=cqn, ckvn=ckvn, q3=q3, k3=k3, v3=v3, o3=o3, lse=lse,
              cat=cat, y1=y1, x1=x13, h2=h2, u3=u3, a=a, y2=y2, bias_row=bias_row, alog_row=alog_row, dcol=dcol)
    return x2.reshape(B, S, D), sv


def layer_bwd(dx3, sv, mod, W, P, l, cosf, sinf):
    B, S, _ = dx3.shape
    T = B * S
    g = {}
    dy2, dg2 = gate_bwd(dx3, sv["y2"].reshape(B, S, D), mod, 5)
    dy2 = dy2.reshape(T, D)
    da = mm(dy2, W["w_down"][l], "nt", "mm_down_dx")
    g["w_down"] = mm(sv["a"].reshape(T, D_FF), dy2, "tn", "mm_down_dw")
    dug, duv, dwg, dwv, dbg, dbv = glu_bwd(sv["u3"], da.reshape(B, S, D_FF), P["conv_ff_w"][l], P["conv_ff_b"][l][None])
    g["conv_ff_w"] = jnp.concatenate([dwg, dwv], axis=1)
    g["conv_ff_b"] = jnp.concatenate([dbg, dbv], axis=1)[0]
    du = jnp.concatenate([dug, duv], axis=-1).reshape(T, 2 * D_FF)
    dh2 = mm(du, W["w_up"][l], "nt", "mm_up_dx")
    g["w_up"] = mm(sv["h2"].reshape(T, D), du, "tn", "mm_up_dw")
    dx1, dsh2, dsc2, dnm = normmod_bwd(sv["x1"], dh2.reshape(B, S, D), dx3, mod, P["norm_mlp"][l][None], 4)
    g["norm_mlp"] = dnm[0]
    dy1, dg1 = gate_bwd(dx1, sv["y1"].reshape(B, S, D), mod, 2)
    dy1 = dy1.reshape(T, D)
    dcat = mm(dy1, W["w_out"][l], "nt", "mm_out_dx")
    g["w_out"] = mm(sv["cat"], dy1, "tn", "mm_out_dw")
    dcat3 = dcat.reshape(B, S, 2 * D)
    do3, dan = rms_bwd(sv["o3"], D, 0, D, dcat3, 1, P["attn_norm"][l][None], f32, "rms_o_bwd")
    g["attn_norm"] = dan[0]
    dq3, delta = attn_bwd_dq(sv["q3"], sv["k3"], sv["v3"], sv["o3"], do3, sv["lse"])
    dk3, dv3 = attn_bwd_dkv(sv["q3"], sv["k3"], sv["v3"], do3, sv["lse"].reshape(B, MLA_H, 1, S), delta.reshape(B, MLA_H, 1, S))
    dqraw = qprep_bwd(dq3, cosf, sinf).reshape(T, -1)
    dcqn = mm(dqraw, W["w_uq"][l], "nt", "mm_uq_dx")
    g["w_uq"] = mm(sv["cqn"].reshape(T, Q_RANK), dqraw, "tn", "mm_uq_dw")
    dcq, dqn = rms_bwd(sv["p3"], 512, OFF_CQ // 512, Q_RANK, dcqn.reshape(B, S, Q_RANK), 0, P["q_norm"][l][None], bf16, "rms_q_bwd")
    g["q_norm"] = dqn[0]
    dyc3, dz, dsn = ssd_out_bwd(sv["yc3"], sv["p3"], dcat3, P["ssd_norm"][l][None])
    g["ssd_norm"] = dsn[0]
    bias_col, alog_col = sv["bias_row"].reshape(128, 1), sv["alog_row"].reshape(128, 1)
    dxc3, ddt3, dalog, dd, dbias = ssd_bwd(sv["xc3"], sv["p3"], dyc3, sv["states"], sv["bias_row"], sv["alog_row"],
                                           bias_col, alog_col, sv["dcol"])
    heads = slice(DT_LANE, DT_LANE + SSD_HEADS)
    g["a_log"], g["d_skip"], g["dt_bias"] = dalog[heads, 0], dd[heads, 0], dbias[heads, 0]
    dxbc, dcw, dcb = conv_ssd_bwd(sv["p3"], dxc3, P["conv_w"][l], P["conv_b"][l][None])
    g["conv_w"], g["conv_b"] = dcw, dcb[0]
    dkvraw, dkrdt = kprep_bwd(dk3, dv3, ddt3, cosf, sinf)
    dkvraw = dkvraw.reshape(T, -1)
    dckvn = mm(dkvraw, W["w_ukv"][l], "nt", "mm_ukv_dx")
    g["w_ukv"] = mm(sv["ckvn"].reshape(T, KV_RANK), dkvraw, "tn", "mm_ukv_dw")
    dckv, dkn = rms_bwd(sv["p3"], KV_RANK, OFF_CKV // KV_RANK, KV_RANK, dckvn.reshape(B, S, KV_RANK), 0, P["kv_norm"][l][None],
                        bf16, "rms_kv_bwd")
    g["kv_norm"] = dkn[0]
    dp = jnp.concatenate([dz, dxbc, dcq, jnp.zeros((B, S, 128), bf16), dckv, dkrdt], axis=-1).reshape(T, IN_COLS)
    dh = mm(dp, W["w_in"][l], "nt", "mm_in_dx")
    g["w_in"] = mm(sv["h"].reshape(T, D), dp, "tn", "mm_in_dw")
    dx0, dsh1, dsc1, dnx = normmod_bwd(sv["x"], dh.reshape(B, S, D), dx1, mod, P["norm_mix"][l][None], 1)
    g["norm_mix"] = dnx[0]
    dmod = jnp.concatenate([dsh1, dsc1, dg1, dsh2, dsc2, dg2], axis=1)
    return dx0, dmod, g


def kernel(x, c, positions, w_ada, b_ada, norm_mix, w_in, conv_w, conv_b, dt_bias, a_log, d_skip, ssd_norm, q_norm, w_uq, kv_norm, w_ukv, attn_norm, w_out, norm_mlp, w_up, conv_ff_w, conv_ff_b, w_down, final_norm, loss_target, m_w_ada, m_b_ada, m_norm_mix, m_w_in, m_conv_w, m_conv_b, m_dt_bias, m_a_log, m_d_skip, m_ssd_norm, m_q_norm, m_w_uq, m_kv_norm, m_w_ukv, m_attn_norm, m_w_out, m_norm_mlp, m_w_up, m_conv_ff_w, m_conv_ff_b, m_w_down, m_final_norm, v_w_ada, v_b_ada, v_norm_mix, v_w_in, v_conv_w, v_conv_b, v_dt_bias, v_a_log, v_d_skip, v_ssd_norm, v_q_norm, v_w_uq, v_kv_norm, v_w_ukv, v_attn_norm, v_w_out, v_norm_mlp, v_w_up, v_conv_ff_w, v_conv_ff_b, v_w_down, v_final_norm):
    given = dict(locals())
    B, S, _ = x.shape
    me = 4 * lax.axis_index("x") + 2 * lax.axis_index("y") + lax.axis_index("c")
    P = {n: given[n] for n in SMALL}

    big_shapes = [given[n].shape for n, _ in BIG]
    *gathered, c_all = exchange([given[n] if n in CONVS else given[n].astype(bf16) for n, _ in BIG] + [c],
                                False, "gather_weights")
    full = {n: shards_to_full(arr, shp, kind) for (n, kind), shp, arr in zip(BIG, big_shapes, gathered)}
    W = {"w_in": w_in_layout(full["w_in"]), "w_out": full["w_out"], "w_up": full["w_up"], "w_down": full["w_down"],
         "w_uq": jnp.stack([w_uq_layout(full["w_uq"][l]) for l in range(DEPTH)]),
         "w_ukv": jnp.stack([w_ukv_layout(full["w_ukv"][l]) for l in range(DEPTH)])}
    P["conv_w"], P["conv_ff_w"] = full["conv_w"], full["conv_ff_w"]

    n_ada = w_ada.shape[2]
    c_all = c_all.reshape(N_DEV * B, D)
    b_sh = lax.dynamic_slice_in_dim(b_ada, me * n_ada, n_ada, axis=1)
    mod_sh = jnp.stack([ada_fwd(c_all, w_ada[l], b_sh[l][None]) for l in range(DEPTH)])
    (mod_g,) = exchange([mod_sh], False, "gather_mod")
    mod_mine = lax.dynamic_slice_in_dim(mod_g, me * B, B, axis=2)
    mods = mod_mine.transpose(1, 2, 0, 3).reshape(DEPTH, B, 6, D)

    inv_freq = jnp.asarray(1.0 / (ROPE_BASE ** (np.arange(0, ROPE, 2, dtype=np.float32) / ROPE)))
    ang = positions.astype(f32)[..., None] * inv_freq
    zeros = jnp.zeros((B, S, 128 - ROPE), f32)
    cosf = jnp.concatenate([jnp.cos(ang), jnp.cos(ang), zeros], axis=-1)
    sinf = jnp.concatenate([jnp.sin(ang), jnp.sin(ang), zeros], axis=-1)

    xl, saved = x, []
    for l in range(DEPTH):
        xl, sv = layer_fwd(xl, mods[l], W, P, l, cosf, sinf)
        saved.append(sv)
    dxl, d_final, loss_part = final_loss(xl, final_norm[None], loss_target)
    grads, dmods = [None] * DEPTH, [None] * DEPTH
    for l in reversed(range(DEPTH)):
        dxl, dmods[l], grads[l] = layer_bwd(dxl, saved[l], mods[l], W, P, l, cosf, sinf)
    grad_x = dxl

    stack = lambda n: jnp.stack([grads[l][n] for l in range(DEPTH)])
    gfull = {"w_in": w_in_unlayout(stack("w_in")), "conv_w": stack("conv_w"),
             "w_uq": jnp.stack([w_uq_unlayout(grads[l]["w_uq"]) for l in range(DEPTH)]),
             "w_ukv": jnp.stack([w_ukv_unlayout(grads[l]["w_ukv"]) for l in range(DEPTH)]),
             "w_out": stack("w_out"), "w_up": stack("w_up"), "conv_ff_w": stack("conv_ff_w"), "w_down": stack("w_down")}
    recv = exchange([full_to_shards(gfull[n], shp, kind).astype(bf16) for (n, kind), shp in zip(BIG, big_shapes)],
                    True, "scatter_grads")
    big_g = [sum_slots(r) for r in recv]

    small_names = [n for n in SMALL if n not in ("b_ada", "final_norm")]
    partial = pack([stack(n) for n in small_names] + [d_final[0], loss_part[0]])
    dmod_all = jnp.stack(dmods)
    part_g, dmod_g = exchange([partial, dmod_all], False, "gather_partials")
    small_sum = sum_leading(part_g, "sum_partials")
    small_g = unpack(small_sum, [given[n].shape for n in small_names] + [(D,), (128,)])
    gsmall = dict(zip(small_names + ["final_norm"], small_g[:-1]))
    loss = small_g[-1][0]
    dmod_rows = dmod_g.transpose(0, 2, 1, 3, 4).reshape(N_DEV * B, DEPTH * 6 * D)
    gsmall["b_ada"] = sum_leading(dmod_rows.reshape(N_DEV * B, -1, 128), "sum_b_ada").reshape(DEPTH, 6 * D)
    dmod_cols = dmod_rows.reshape(N_DEV * B, DEPTH, N_DEV, n_ada)
    dmod_sh = lax.dynamic_slice_in_dim(dmod_cols, me, 1, axis=2)[:, :, 0, :]
    g_w_ada = jnp.stack([ada_bwd(c_all, dmod_sh[:, l, :]) for l in range(DEPTH)])

    res = {"grad": {}, "delta": {}, "new_m": {}, "new_v": {}}
    for n, gv in zip([n for n, _ in BIG] + ["w_ada"], big_g + [g_w_ada]):
        res["grad"][n] = gv
        res["delta"][n], res["new_m"][n], res["new_v"][n] = adamw_nd(given[n], gv, given["m_" + n], given["v_" + n])
    shapes = [given[n].shape for n in SMALL]
    flat = adamw(pack([given[n] for n in SMALL]), pack([gsmall[n] for n in SMALL]), pack([given["m_" + n] for n in SMALL]),
                 pack([given["v_" + n] for n in SMALL]))
    for n in SMALL:
        res["grad"][n] = gsmall[n]
    for key, arr in zip(("delta", "new_m", "new_v"), flat):
        res[key].update(zip(SMALL, unpack(arr, shapes)))
    order = ["w_ada", "b_ada", "norm_mix", "w_in", "conv_w", "conv_b", "dt_bias", "a_log", "d_skip", "ssd_norm", "q_norm", "w_uq",
             "kv_norm", "w_ukv", "attn_norm", "w_out", "norm_mlp", "w_up", "conv_ff_w", "conv_ff_b", "w_down", "final_norm"]
    return (loss, grad_x, *[res[k][n] for k in ("grad", "delta", "new_m", "new_v") for n in order])
```

```python
import functools

import numpy as np
import jax
import jax.numpy as jnp
from jax import lax
from jax.experimental import pallas as pl
from jax.experimental.pallas import tpu as pltpu

f32, bf16 = jnp.float32, jnp.bfloat16
HIGHEST = lax.Precision.HIGHEST

D = 1024
D_SSD = 1024
SSD_HEADS = 16
SSD_HD = 64
SSD_N = 128
CHUNK = 128
D_XBC = 1536
CONV_K = 4
MLA_H = 8
NOPE = 128
ROPE = 64
VD = 128
QK = NOPE + ROPE
Q_RANK = 384
KV_RANK = 256
D_FF = 2816
FF_K = 3
EPS = 1e-6
ROPE_BASE = 10000.0
DEPTH = 2
ADAM_LR, ADAM_B1, ADAM_B2, ADAM_EPS, ADAM_WD, ADAM_STEP = 0.001, 0.9, 0.999, 1e-08, 0.01, 10

N_DEV = 8
IN_COLS = 3456
OFF_XBC, OFF_CQ, OFF_CKV, OFF_KRDT = 1024, 2560, 3072, 3328
DT_LANE = 64
VMEM_LIMIT = 48 * 1024 * 1024
MM_K_WHOLE = 4096


def _cparams(n_grid):
    return pltpu.CompilerParams(dimension_semantics=("arbitrary",) * n_grid, vmem_limit_bytes=VMEM_LIMIT)


def _pick(n, cands):
    for c in cands:
        if n % c == 0:
            return c
    return n


def _silu(x):
    return x * jax.nn.sigmoid(x)


def _dsilu(x):
    s = jax.nn.sigmoid(x)
    return s * (1.0 + x * (1.0 - s))


def _rowsum(x):
    return jnp.sum(x, axis=0, keepdims=True)


def mm(a, b, mode, name, out_dtype=f32, resid=None, gate=None, seq=None):
    if mode == "nn":
        (M, K), N = a.shape, b.shape[1]
    elif mode == "nt":
        (M, K), N = a.shape, b.shape[0]
    else:
        (K, M), N = a.shape, b.shape[1]
    gated = resid is not None
    tm = _pick(seq if gated else M, (1024, 1408, 512, 384, 256, 128))
    tn = _pick(N, (512, 384, 256, 128))
    tk = K if K <= MM_K_WHOLE else _pick(K, (2816, 2048, 1024, 512))
    nk = K // tk
    dims = {"nn": ((1,), (0,)), "nt": ((1,), (1,)), "tn": ((0,), (0,))}[mode]

    def body(a_ref, b_ref, *rest):
        if gated:
            r_ref, g_ref, o_ref, y_ref, acc = rest
        else:
            o_ref, acc = rest

        def finish(res):
            if gated:
                y_ref[...] = res
                o_ref[...] = r_ref[...] + g_ref[0] * res
            else:
                o_ref[...] = res.astype(out_dtype)

        prod = lax.dot_general(a_ref[...].astype(bf16), b_ref[...].astype(bf16), (dims, ((), ())), preferred_element_type=f32)
        if nk == 1:
            finish(prod)
        else:
            k = pl.program_id(2)

            @pl.when(k == 0)
            def _():
                acc[...] = prod

            @pl.when(k > 0)
            def _():
                acc[...] += prod

            @pl.when(k == nk - 1)
            def _():
                finish(acc[...])

    a_spec = pl.BlockSpec((tk, tm), lambda i, j, k: (k, i)) if mode == "tn" else pl.BlockSpec((tm, tk), lambda i, j, k: (i, k))
    b_spec = pl.BlockSpec((tn, tk), lambda i, j, k: (j, k)) if mode == "nt" else pl.BlockSpec((tk, tn), lambda i, j, k: (k, j))
    o_spec = pl.BlockSpec((tm, tn), lambda i, j, k: (i, j))
    in_specs, args = [a_spec, b_spec], [a, b]
    out_specs, out_shape = o_spec, jax.ShapeDtypeStruct((M, N), out_dtype)
    if gated:
        per = seq // tm
        in_specs += [o_spec, pl.BlockSpec((1, 1, tn), lambda i, j, k: (i // per, 0, j))]
        args += [resid, gate]
        out_specs = [o_spec, o_spec]
        out_shape = [jax.ShapeDtypeStruct((M, N), f32), jax.ShapeDtypeStruct((M, N), f32)]
    return pl.pallas_call(body, name=name, grid=(M // tm, N // tn, nk), in_specs=in_specs, out_specs=out_specs,
                          out_shape=out_shape, scratch_shapes=[pltpu.VMEM((tm, tn), f32)],
                          compiler_params=_cparams(3))(*args)


def _tok(ts, width, cb=0):
    return pl.BlockSpec((1, ts, width), lambda b, s: (b, s, cb))


def _perb(rows, width):
    return pl.BlockSpec((1, rows, width), lambda b, s: (b, 0, 0))


def _const(rows, width):
    return pl.BlockSpec((rows, width), lambda b, s: (0, 0))


def _row_call(body, name, B, S, ts, in_specs, out_specs, out_shape, scratch=()):
    return pl.pallas_call(body, name=name, grid=(B, S // ts), in_specs=in_specs, out_specs=out_specs,
                          out_shape=out_shape, scratch_shapes=list(scratch), compiler_params=_cparams(2))


def _first():
    return (pl.program_id(0) == 0) & (pl.program_id(1) == 0)


def normmod_fwd(x3, mod, g, i_sh, i_sc):
    B, S, C = x3.shape
    ts = _pick(S, (512, 256, 128))

    def body(x_ref, mod_ref, g_ref, h_ref):
        x = x_ref[0]
        r = lax.rsqrt(jnp.mean(x * x, axis=-1, keepdims=True) + EPS)
        n = x * r * g_ref[...]
        h_ref[0] = (n * (1.0 + mod_ref[0, i_sc:i_sc + 1, :]) + mod_ref[0, i_sh:i_sh + 1, :]).astype(bf16)

    return _row_call(body, "normmod_fwd", B, S, ts, [_tok(ts, C), _perb(6, C), _const(1, C)], _tok(ts, C),
                     jax.ShapeDtypeStruct((B, S, C), bf16))(x3, mod, g)


def normmod_bwd(x3, dh3, resid3, mod, g, i_sc):
    B, S, C = x3.shape
    ts = _pick(S, (512, 256, 128))

    def body(x_ref, dh_ref, r_ref, mod_ref, g_ref, dx_ref, dsh_ref, dsc_ref, dg_ref):
        @pl.when(pl.program_id(1) == 0)
        def _():
            dsh_ref[...] = jnp.zeros_like(dsh_ref)
            dsc_ref[...] = jnp.zeros_like(dsc_ref)

        @pl.when(_first())
        def _():
            dg_ref[...] = jnp.zeros_like(dg_ref)

        x, dh, gv = x_ref[0], dh_ref[0], g_ref[...]
        r = lax.rsqrt(jnp.mean(x * x, axis=-1, keepdims=True) + EPS)
        xh = x * r
        dn = dh * (1.0 + mod_ref[0, i_sc:i_sc + 1, :])
        dsh_ref[0] += _rowsum(dh)
        dsc_ref[0] += _rowsum(dh * xh * gv)
        dg_ref[...] += _rowsum(dn * xh)
        dxh = dn * gv
        dx_ref[0] = r * (dxh - xh * jnp.mean(dxh * xh, axis=-1, keepdims=True)) + r_ref[0]

    return _row_call(body, "normmod_bwd", B, S, ts,
                     [_tok(ts, C), _tok(ts, C), _tok(ts, C), _perb(6, C), _const(1, C)],
                     [_tok(ts, C), _perb(1, C), _perb(1, C), _const(1, C)],
                     [jax.ShapeDtypeStruct((B, S, C), f32), jax.ShapeDtypeStruct((B, 1, C), f32),
                      jax.ShapeDtypeStruct((B, 1, C), f32), jax.ShapeDtypeStruct((1, C), f32)])(x3, dh3, resid3, mod, g)


def gate_bwd(dx3, y3, mod, i_g):
    B, S, C = dx3.shape
    ts = _pick(S, (512, 256, 128))

    def body(dx_ref, y_ref, mod_ref, dy_ref, dgate_ref):
        @pl.when(pl.program_id(1) == 0)
        def _():
            dgate_ref[...] = jnp.zeros_like(dgate_ref)

        dx = dx_ref[0]
        dy_ref[0] = (dx * mod_ref[0, i_g:i_g + 1, :]).astype(bf16)
        dgate_ref[0] += _rowsum(dx * y_ref[0])

    return _row_call(body, "gate_bwd", B, S, ts, [_tok(ts, C), _tok(ts, C), _perb(6, C)], [_tok(ts, C), _perb(1, C)],
                     [jax.ShapeDtypeStruct((B, S, C), bf16), jax.ShapeDtypeStruct((B, 1, C), f32)])(dx3, y3, mod)


def rms_fwd(src3, width, cb, n, g, name):
    B, S, _ = src3.shape
    ts = _pick(S, (512, 256, 128))

    def body(x_ref, g_ref, o_ref):
        x = x_ref[0][:, :n]
        r = lax.rsqrt(jnp.mean(x * x, axis=-1, keepdims=True) + EPS)
        o_ref[0] = (x * r * g_ref[...]).astype(bf16)

    return _row_call(body, name, B, S, ts, [_tok(ts, width, cb), _const(1, n)], _tok(ts, n),
                     jax.ShapeDtypeStruct((B, S, n), bf16))(src3, g)


def rms_bwd(src3, width, cb, n, dout3, dcb, g, out_dtype, name):
    B, S, _ = src3.shape
    ts = _pick(S, (512, 256, 128))

    def body(x_ref, do_ref, g_ref, dx_ref, dg_ref):
        @pl.when(_first())
        def _():
            dg_ref[...] = jnp.zeros_like(dg_ref)

        x = x_ref[0][:, :n]
        do = do_ref[0].astype(f32)
        r = lax.rsqrt(jnp.mean(x * x, axis=-1, keepdims=True) + EPS)
        xh = x * r
        dg_ref[...] += _rowsum(do * xh)
        dxh = do * g_ref[...]
        dx_ref[0] = (r * (dxh - xh * jnp.mean(dxh * xh, axis=-1, keepdims=True))).astype(out_dtype)

    return _row_call(body, name, B, S, ts, [_tok(ts, width, cb), _tok(ts, n, dcb), _const(1, n)],
                     [_tok(ts, n), _const(1, n)],
                     [jax.ShapeDtypeStruct((B, S, n), out_dtype), jax.ShapeDtypeStruct((1, n), f32)])(src3, dout3, g)


def final_loss(x3, g, tgt3):
    B, S, C = x3.shape
    ts = _pick(S, (512, 256, 128))

    def body(x_ref, g_ref, t_ref, dx_ref, dg_ref, loss_ref):
        @pl.when(_first())
        def _():
            dg_ref[...] = jnp.zeros_like(dg_ref)
            loss_ref[...] = jnp.zeros_like(loss_ref)

        x, gv = x_ref[0], g_ref[...]
        r = lax.rsqrt(jnp.mean(x * x, axis=-1, keepdims=True) + EPS)
        xh = x * r
        e = xh * gv - t_ref[0]
        loss_ref[...] += 0.5 * jnp.sum(e * e) / C
        dout = e / C
        dg_ref[...] += _rowsum(dout * xh)
        dxh = dout * gv
        dx_ref[0] = r * (dxh - xh * jnp.mean(dxh * xh, axis=-1, keepdims=True))

    return _row_call(body, "final_loss", B, S, ts, [_tok(ts, C), _const(1, C), _tok(ts, C)],
                     [_tok(ts, C), _const(1, C), _const(1, 128)],
                     [jax.ShapeDtypeStruct((B, S, C), f32), jax.ShapeDtypeStruct((1, C), f32),
                      jax.ShapeDtypeStruct((1, 128), f32)])(x3, g, tgt3)


def ssd_out_fwd(yc3, p3, w):
    B, S, C = yc3.shape
    ts = _pick(S, (512, 256, 128))
    half = C // 2

    def body(y_ref, z_ref, w_ref, o_ref):
        y = y_ref[0] * _silu(z_ref[0])
        for lo in (0, half):
            yg = y[:, lo:lo + half]
            r = lax.rsqrt(jnp.mean(yg * yg, axis=-1, keepdims=True) + EPS)
            o_ref[0, :, lo:lo + half] = (yg * r * w_ref[:, lo:lo + half]).astype(bf16)

    return _row_call(body, "ssd_out_fwd", B, S, ts, [_tok(ts, C), _tok(ts, C, 0), _const(1, C)], _tok(ts, C),
                     jax.ShapeDtypeStruct((B, S, C), bf16))(yc3, p3, w)


def ssd_out_bwd(yc3, p3, dcat3, w):
    B, S, C = yc3.shape
    ts = _pick(S, (512, 256, 128))
    half = C // 2

    def body(y_ref, z_ref, do_ref, w_ref, dyc_ref, dz_ref, dw_ref):
        @pl.when(_first())
        def _():
            dw_ref[...] = jnp.zeros_like(dw_ref)

        yc, z, do = y_ref[0], z_ref[0], do_ref[0]
        sz = _silu(z)
        y = yc * sz
        for lo in (0, half):
            sl = slice(lo, lo + half)
            yg, dog, wg = y[:, sl], do[:, sl], w_ref[:, sl]
            r = lax.rsqrt(jnp.mean(yg * yg, axis=-1, keepdims=True) + EPS)
            yh = yg * r
            dw_ref[:, sl] += _rowsum(dog * yh)
            dyh = dog * wg
            dy = r * (dyh - yh * jnp.mean(dyh * yh, axis=-1, keepdims=True))
            dyc_ref[0, :, sl] = dy * sz[:, sl]
            dz_ref[0, :, sl] = (dy * yc[:, sl] * _dsilu(z[:, sl])).astype(bf16)

    return _row_call(body, "ssd_out_bwd", B, S, ts, [_tok(ts, C), _tok(ts, C, 0), _tok(ts, C, 0), _const(1, C)],
                     [_tok(ts, C), _tok(ts, C), _const(1, C)],
                     [jax.ShapeDtypeStruct((B, S, C), f32), jax.ShapeDtypeStruct((B, S, C), bf16),
                      jax.ShapeDtypeStruct((1, C), f32)])(yc3, p3, dcat3, w)


def _rot(t):
    lane = lax.broadcasted_iota(jnp.int32, t.shape, 1)
    return jnp.where(lane < ROPE // 2, -pltpu.roll(t, 128 - ROPE // 2, 1), pltpu.roll(t, ROPE // 2, 1))


def _rope(t, cosf, sinf):
    return t * cosf + _rot(t) * sinf


def _rope_t(d, cosf, sinf):
    return d * cosf - _rot(d * sinf)


def qprep_fwd(qraw3, cosf, sinf):
    B, S, W = qraw3.shape
    ts = _pick(S, (512, 256, 128))

    def body(q_ref, c_ref, s_ref, o_ref):
        c, s = c_ref[0], s_ref[0]
        for h in range(MLA_H):
            o_ref[0, :, h * 256:h * 256 + 128] = q_ref[0, :, h * 256:h * 256 + 128].astype(bf16)
            o_ref[0, :, h * 256 + 128:(h + 1) * 256] = _rope(q_ref[0, :, h * 256 + 128:(h + 1) * 256], c, s).astype(bf16)

    return _row_call(body, "qprep_fwd", B, S, ts, [_tok(ts, W), _tok(ts, 128), _tok(ts, 128)], _tok(ts, W),
                     jax.ShapeDtypeStruct((B, S, W), bf16))(qraw3, cosf, sinf)


def qprep_bwd(dq3, cosf, sinf):
    B, S, W = dq3.shape
    ts = _pick(S, (512, 256, 128))

    def body(d_ref, c_ref, s_ref, o_ref):
        c, s = c_ref[0], s_ref[0]
        for h in range(MLA_H):
            o_ref[0, :, h * 256:h * 256 + 128] = d_ref[0, :, h * 256:h * 256 + 128].astype(bf16)
            o_ref[0, :, h * 256 + 128:(h + 1) * 256] = _rope_t(d_ref[0, :, h * 256 + 128:(h + 1) * 256], c, s).astype(bf16)

    return _row_call(body, "qprep_bwd", B, S, ts, [_tok(ts, W), _tok(ts, 128), _tok(ts, 128)], _tok(ts, W),
                     jax.ShapeDtypeStruct((B, S, W), bf16))(dq3, cosf, sinf)


def kprep_fwd(kv3, p3, cosf, sinf):
    B, S, _ = kv3.shape
    ts = _pick(S, (512, 256, 128))
    Wn = MLA_H * NOPE

    def body(k_ref, v_ref, kr_ref, c_ref, s_ref, ko_ref, vo_ref):
        lane = lax.broadcasted_iota(jnp.int32, (1, 128), 1)
        kr = jnp.where(lane < ROPE, kr_ref[0], 0.0)
        kr = _rope(kr, c_ref[0], s_ref[0]).astype(bf16)
        for h in range(MLA_H):
            ko_ref[0, :, h * 256:h * 256 + 128] = k_ref[0, :, h * 128:(h + 1) * 128].astype(bf16)
            ko_ref[0, :, h * 256 + 128:(h + 1) * 256] = kr
        vo_ref[0] = v_ref[0].astype(bf16)

    return _row_call(body, "kprep_fwd", B, S, ts,
                     [_tok(ts, Wn, 0), _tok(ts, Wn, 1), _tok(ts, 128, OFF_KRDT // 128), _tok(ts, 128), _tok(ts, 128)],
                     [_tok(ts, 2 * Wn), _tok(ts, Wn)],
                     [jax.ShapeDtypeStruct((B, S, 2 * Wn), bf16), jax.ShapeDtypeStruct((B, S, Wn), bf16)])(kv3, kv3, p3, cosf, sinf)


def kprep_bwd(dk3, dv3, ddt3, cosf, sinf):
    B, S, _ = dk3.shape
    ts = _pick(S, (512, 256, 128))
    Wn = MLA_H * NOPE

    def body(dk_ref, dv_ref, ddt_ref, c_ref, s_ref, o_ref, kr_ref):
        acc = jnp.zeros((ts, 128), f32)
        for h in range(MLA_H):
            o_ref[0, :, h * 128:(h + 1) * 128] = dk_ref[0, :, h * 256:h * 256 + 128].astype(bf16)
            acc = acc + dk_ref[0, :, h * 256 + 128:(h + 1) * 256]
        o_ref[0, :, Wn:] = dv_ref[0].astype(bf16)
        lane = lax.broadcasted_iota(jnp.int32, (1, 128), 1)
        dkr = _rope_t(acc, c_ref[0], s_ref[0])
        kr_ref[0] = jnp.where(lane < ROPE, dkr, ddt_ref[0]).astype(bf16)

    return _row_call(body, "kprep_bwd", B, S, ts,
                     [_tok(ts, 2 * Wn), _tok(ts, Wn), _tok(ts, 128), _tok(ts, 128), _tok(ts, 128)],
                     [_tok(ts, 2 * Wn), _tok(ts, 128)],
                     [jax.ShapeDtypeStruct((B, S, 2 * Wn), bf16), jax.ShapeDtypeStruct((B, S, 128), bf16)])(dk3, dv3, ddt3, cosf, sinf)


def _shift_down(u, j):
    if j == 0:
        return u
    row = lax.broadcasted_iota(jnp.int32, u.shape, 0)
    return jnp.where(row < j, 0.0, pltpu.roll(u, j, 0))


def _shift_up(u, j):
    if j == 0:
        return u
    n = u.shape[0]
    row = lax.broadcasted_iota(jnp.int32, u.shape, 0)
    return jnp.where(row >= n - j, 0.0, pltpu.roll(u, n - j, 0))


def _conv(u, w, b, K):
    out = b
    for j in range(K):
        out = out + w[K - 1 - j:K - j, :] * _shift_down(u, j)
    return out


def _conv_bwd(u, du, w, K):
    dins = w[K - 1:K, :] * du
    dws = [None] * K
    dws[K - 1] = _rowsum(du * u)
    for j in range(1, K):
        dins = dins + w[K - 1 - j:K - j, :] * _shift_up(du, j)
        dws[K - 1 - j] = _rowsum(du * _shift_down(u, j))
    return dins, dws


CW = 256


def conv_ssd_fwd(p3, w, b):
    B, S, _ = p3.shape
    nb = D_XBC // CW

    def body(u_ref, w_ref, b_ref, o_ref):
        o_ref[0] = _silu(_conv(u_ref[0], w_ref[...], b_ref[...], CONV_K))

    return pl.pallas_call(body, name="conv_ssd_fwd", grid=(B, nb),
                          in_specs=[pl.BlockSpec((1, S, CW), lambda b, j: (b, 0, OFF_XBC // CW + j)),
                                    pl.BlockSpec((CONV_K, CW), lambda b, j: (0, j)),
                                    pl.BlockSpec((1, CW), lambda b, j: (0, j))],
                          out_specs=pl.BlockSpec((1, S, CW), lambda b, j: (b, 0, j)),
                          out_shape=jax.ShapeDtypeStruct((B, S, D_XBC), f32), compiler_params=_cparams(2))(p3, w, b)


def conv_ssd_bwd(p3, dxc3, w, b):
    B, S, _ = p3.shape
    nb = D_XBC // CW

    def body(u_ref, d_ref, w_ref, b_ref, du_ref, dw_ref, db_ref):
        @pl.when(pl.program_id(1) == 0)
        def _():
            dw_ref[...] = jnp.zeros_like(dw_ref)
            db_ref[...] = jnp.zeros_like(db_ref)

        u, wv = u_ref[0], w_ref[...]
        dpre = d_ref[0] * _dsilu(_conv(u, wv, b_ref[...], CONV_K))
        dins, dws = _conv_bwd(u, dpre, wv, CONV_K)
        du_ref[0] = dins.astype(bf16)
        for k in range(CONV_K):
            dw_ref[k:k + 1, :] += dws[k]
        db_ref[...] += _rowsum(dpre)

    return pl.pallas_call(body, name="conv_ssd_bwd", grid=(nb, B),
                          in_specs=[pl.BlockSpec((1, S, CW), lambda j, b: (b, 0, OFF_XBC // CW + j)),
                                    pl.BlockSpec((1, S, CW), lambda j, b: (b, 0, j)),
                                    pl.BlockSpec((CONV_K, CW), lambda j, b: (0, j)),
                                    pl.BlockSpec((1, CW), lambda j, b: (0, j))],
                          out_specs=[pl.BlockSpec((1, S, CW), lambda j, b: (b, 0, j)),
                                     pl.BlockSpec((CONV_K, CW), lambda j, b: (0, j)),
                                     pl.BlockSpec((1, CW), lambda j, b: (0, j))],
                          out_shape=[jax.ShapeDtypeStruct((B, S, D_XBC), bf16), jax.ShapeDtypeStruct((CONV_K, D_XBC), f32),
                                     jax.ShapeDtypeStruct((1, D_XBC), f32)], compiler_params=_cparams(2))(p3, dxc3, w, b)


def glu_fwd(u3, w, b):
    B, S, _ = u3.shape
    nb = D_FF // CW

    def body(ug_ref, uv_ref, wg_ref, wv_ref, bg_ref, bv_ref, o_ref):
        g = _conv(ug_ref[0].astype(f32), wg_ref[...], bg_ref[...], FF_K)
        v = _conv(uv_ref[0].astype(f32), wv_ref[...], bv_ref[...], FF_K)
        o_ref[0] = (_silu(g) * v).astype(bf16)

    def blk(off):
        return pl.BlockSpec((1, S, CW), lambda b, j: (b, 0, off + j))

    def par(rows, off):
        return pl.BlockSpec((rows, CW), lambda b, j: (0, off + j))

    return pl.pallas_call(body, name="glu_fwd", grid=(B, nb),
                          in_specs=[blk(0), blk(nb), par(FF_K, 0), par(FF_K, nb), par(1, 0), par(1, nb)],
                          out_specs=blk(0), out_shape=jax.ShapeDtypeStruct((B, S, D_FF), bf16),
                          compiler_params=_cparams(2))(u3, u3, w, w, b, b)


def glu_bwd(u3, da3, w, b):
    B, S, _ = u3.shape
    nb = D_FF // CW

    def body(ug_ref, uv_ref, da_ref, wg_ref, wv_ref, bg_ref, bv_ref, dug_ref, duv_ref, dwg_ref, dwv_ref, dbg_ref, dbv_ref):
        @pl.when(pl.program_id(1) == 0)
        def _():
            for r in (dwg_ref, dwv_ref, dbg_ref, dbv_ref):
                r[...] = jnp.zeros_like(r)

        ug, uv, da, wg, wv = ug_ref[0].astype(f32), uv_ref[0].astype(f32), da_ref[0], wg_ref[...], wv_ref[...]
        g = _conv(ug, wg, bg_ref[...], FF_K)
        v = _conv(uv, wv, bv_ref[...], FF_K)
        dg = da * v * _dsilu(g)
        dv = da * _silu(g)
        ding, dwsg = _conv_bwd(ug, dg, wg, FF_K)
        dinv, dwsv = _conv_bwd(uv, dv, wv, FF_K)
        dug_ref[0] = ding.astype(bf16)
        duv_ref[0] = dinv.astype(bf16)
        for k in range(FF_K):
            dwg_ref[k:k + 1, :] += dwsg[k]
            dwv_ref[k:k + 1, :] += dwsv[k]
        dbg_ref[...] += _rowsum(dg)
        dbv_ref[...] += _rowsum(dv)

    def blk(off):
        return pl.BlockSpec((1, S, CW), lambda j, b: (b, 0, off + j))

    def par(rows, off):
        return pl.BlockSpec((rows, CW), lambda j, b: (0, off + j))

    return pl.pallas_call(body, name="glu_bwd", grid=(nb, B),
                          in_specs=[blk(0), blk(nb), blk(0), par(FF_K, 0), par(FF_K, nb), par(1, 0), par(1, nb)],
                          out_specs=[blk(0), blk(0), par(FF_K, 0), par(FF_K, 0), par(1, 0), par(1, 0)],
                          out_shape=[jax.ShapeDtypeStruct((B, S, D_FF), bf16), jax.ShapeDtypeStruct((B, S, D_FF), bf16),
                                     jax.ShapeDtypeStruct((FF_K, D_FF), f32), jax.ShapeDtypeStruct((FF_K, D_FF), f32),
                                     jax.ShapeDtypeStruct((1, D_FF), f32), jax.ShapeDtypeStruct((1, D_FF), f32)],
                          compiler_params=_cparams(2))(u3, u3, da3, w, w, b, b)


def _ssd_decay(dtb, bias_row, alog_row):
    lane = lax.broadcasted_iota(jnp.int32, (1, 128), 1)
    hmask = (lane >= DT_LANE) & (lane < DT_LANE + SSD_HEADS)
    dt = jnp.where(hmask, jax.nn.softplus(dtb + bias_row), 0.0)
    a = dt * jnp.where(hmask, -jnp.exp(alog_row), 0.0)
    r = lax.broadcasted_iota(jnp.int32, (CHUNK, CHUNK), 0)
    c = lax.broadcasted_iota(jnp.int32, (CHUNK, CHUNK), 1)
    cs = jnp.dot((r >= c).astype(f32), a, precision=HIGHEST, preferred_element_type=f32)
    return dt, cs


def _expand(xt):
    return jnp.concatenate([jnp.broadcast_to(xt[DT_LANE + h:DT_LANE + h + 1, :], (SSD_HD, xt.shape[1]))
                            for h in range(SSD_HEADS)], axis=0)


_NT = (((1,), (1,)), ((), ()))
_TN = (((0,), (0,)), ((), ()))
GH = SSD_HEADS // 2
GR = GH * SSD_HD


def ssd_fwd(xc3, p3, bias_row, alog_row, dcol):
    B, S, _ = xc3.shape
    nc = S // CHUNK

    def body(xs_ref, bc_ref, dtb_ref, bias_ref, alog_ref, dcol_ref, y_ref, st_ref, state, yT):
        @pl.when(pl.program_id(1) == 0)
        def _():
            state[...] = jnp.zeros_like(state)

        dt, cs = _ssd_decay(dtb_ref[0], bias_ref[...], alog_ref[...])
        csT = cs.T
        eT = jnp.exp(csT)
        decX = _expand(jnp.exp(csT[:, CHUNK - 1:CHUNK] - csT))
        eX = _expand(eT)
        elastX = eX[:, CHUNK - 1:CHUNK]
        xsT = xs_ref[0].T
        uT = xsT * _expand(dt.T)
        bc = bc_ref[0]
        st_ref[0, 0] = state[...]
        srow = lax.broadcasted_iota(jnp.int32, (CHUNK, CHUNK), 0)
        lcol = lax.broadcasted_iota(jnp.int32, (CHUNK, CHUNK), 1)
        for g in range(2):
            Bg = bc[:, g * SSD_N:(g + 1) * SSD_N].astype(bf16)
            Cg = bc[:, (2 + g) * SSD_N:(3 + g) * SSD_N].astype(bf16)
            GT = lax.dot_general(Bg, Cg, _NT, preferred_element_type=f32)
            rows = slice(g * GR, (g + 1) * GR)
            Sg = state[rows]
            yoffT = lax.dot_general(Sg.astype(bf16), Cg, _NT, preferred_element_type=f32) * eX[rows]
            state[rows] = Sg * elastX[rows] + jnp.dot((uT[rows] * decX[rows]).astype(bf16), Bg, preferred_element_type=f32)
            for k in range(GH):
                h = g * GH + k
                hr = slice(h * SSD_HD, (h + 1) * SSD_HD)
                seg = csT[DT_LANE + h:DT_LANE + h + 1, :] - cs[:, DT_LANE + h:DT_LANE + h + 1]
                LT = jnp.where(lcol >= srow, jnp.exp(jnp.minimum(seg, 0.0)), 0.0)
                yT[hr] = (jnp.dot(uT[hr].astype(bf16), (GT * LT).astype(bf16), preferred_element_type=f32)
                          + yoffT[k * SSD_HD:(k + 1) * SSD_HD] + dcol_ref[hr] * xsT[hr])
        y_ref[0] = yT[...].T

    return pl.pallas_call(body, name="ssd_fwd", grid=(B, nc),
                          in_specs=[pl.BlockSpec((1, CHUNK, D_SSD), lambda b, c: (b, c, 0)),
                                    pl.BlockSpec((1, CHUNK, 512), lambda b, c: (b, c, 2)),
                                    pl.BlockSpec((1, CHUNK, 128), lambda b, c: (b, c, OFF_KRDT // 128)),
                                    _const(1, 128), _const(1, 128), _const(D_SSD, 1)],
                          out_specs=[pl.BlockSpec((1, CHUNK, D_SSD), lambda b, c: (b, c, 0)),
                                     pl.BlockSpec((1, 1, D_SSD, SSD_N), lambda b, c: (b, c, 0, 0))],
                          out_shape=[jax.ShapeDtypeStruct((B, S, D_SSD), f32), jax.ShapeDtypeStruct((B, nc, D_SSD, SSD_N), f32)],
                          scratch_shapes=[pltpu.VMEM((D_SSD, SSD_N), f32), pltpu.VMEM((D_SSD, CHUNK), f32)],
                          compiler_params=_cparams(2))(xc3, xc3, p3, bias_row, alog_row, dcol)


def ssd_bwd(xc3, p3, dy3, states, bias_row, alog_row, bias_col, alog_col, dcol):
    B, S, _ = xc3.shape
    nc = S // CHUNK

    def body(xs_ref, bc_ref, dtb_ref, dy_ref, st_ref, bias_ref, alog_ref, biasc_ref, alogc_ref, dcol_ref,
             dxc_ref, ddt_ref, dalog_ref, dd_ref, dbias_ref, dS, dUT, accA, accD, accB, dcs_diag):
        @pl.when(pl.program_id(1) == 0)
        def _():
            dS[...] = jnp.zeros_like(dS)

        @pl.when(_first())
        def _():
            accA[...] = jnp.zeros_like(accA)
            accD[...] = jnp.zeros_like(accD)
            accB[...] = jnp.zeros_like(accB)

        dtb = dtb_ref[0]
        dt, cs = _ssd_decay(dtb, bias_ref[...], alog_ref[...])
        dtT, csT = dt.T, cs.T
        decX = _expand(jnp.exp(csT[:, CHUNK - 1:CHUNK] - csT))
        eX = _expand(jnp.exp(csT))
        dtX = _expand(dtT)
        elastX = eX[:, CHUNK - 1:CHUNK]
        xsT = xs_ref[0].T
        uT = xsT * dtX
        dYT = dy_ref[0].T
        bc = bc_ref[0]
        lrow = lax.broadcasted_iota(jnp.int32, (CHUNK, CHUNK), 0)
        scol = lax.broadcasted_iota(jnp.int32, (CHUNK, CHUNK), 1)
        dcs_diag[...] = jnp.zeros_like(dcs_diag)
        vparts, zparts = [], []
        for g in range(2):
            Bf = bc[:, g * SSD_N:(g + 1) * SSD_N]
            Bg = Bf.astype(bf16)
            Cg = bc[:, (2 + g) * SSD_N:(3 + g) * SSD_N].astype(bf16)
            G = lax.dot_general(Cg, Bg, _NT, preferred_element_type=f32)
            BgT = Bf.T.astype(bf16)
            rows = slice(g * GR, (g + 1) * GR)
            dSg = dS[rows]
            Sg = st_ref[0, 0, rows, :]
            dUst = jnp.dot(dSg.astype(bf16), BgT, preferred_element_type=f32) * decX[rows]
            yoffT = lax.dot_general(Sg.astype(bf16), Cg, _NT, preferred_element_type=f32) * eX[rows]
            zparts.append(dYT[rows] * yoffT - dUst * uT[rows])
            dG = jnp.zeros((CHUNK, CHUNK), f32)
            for k in range(GH):
                h = g * GH + k
                hr = slice(h * SSD_HD, (h + 1) * SSD_HD)
                seg = cs[:, DT_LANE + h:DT_LANE + h + 1] - csT[DT_LANE + h:DT_LANE + h + 1, :]
                L = jnp.where(lrow >= scol, jnp.exp(jnp.minimum(seg, 0.0)), 0.0)
                M = G * L
                dYh = dYT[hr].astype(bf16)
                dUT[hr] = jnp.dot(dYh, M.astype(bf16), preferred_element_type=f32) + dUst[k * SSD_HD:(k + 1) * SSD_HD]
                dM = lax.dot_general(dYh, uT[hr].astype(bf16), _TN, preferred_element_type=f32)
                dG = dG + dM * L
                Wm = dM * M
                rs = lax.dot_general(jnp.ones((8, CHUNK), f32), Wm, _NT, precision=HIGHEST, preferred_element_type=f32)[0:1]
                dcs_diag[DT_LANE + h:DT_LANE + h + 1, :] = rs - _rowsum(Wm)
            dGb = dG.astype(bf16)
            dYe = (dYT[rows] * eX[rows]).astype(bf16)
            ude = (uT[rows] * decX[rows]).astype(bf16)
            dC = jnp.dot(dGb, Bg, preferred_element_type=f32) + lax.dot_general(dYe, Sg.astype(bf16), _TN, preferred_element_type=f32)
            dB = (lax.dot_general(dGb, Cg, _TN, preferred_element_type=f32)
                  + lax.dot_general(ude, dSg.astype(bf16), _TN, preferred_element_type=f32))
            dxc_ref[0, :, D_SSD + g * SSD_N:D_SSD + (g + 1) * SSD_N] = dB
            dxc_ref[0, :, D_SSD + (2 + g) * SSD_N:D_SSD + (3 + g) * SSD_N] = dC
            vparts.append(elastX[rows] * jnp.sum(dSg * Sg, axis=1, keepdims=True)
                          + jnp.sum(dUst * uT[rows], axis=1, keepdims=True))
            dS[rows] = elastX[rows] * dSg + jnp.dot(dYe, Cg, preferred_element_type=f32)
        dU = dUT[...]
        dcv = dcol_ref[...]
        dxc_ref[0, :, 0:D_SSD] = (dtX * dU + dcv * dYT).T
        lane = lax.broadcasted_iota(jnp.int32, (D_SSD, CHUNK), 1)
        Z = jnp.concatenate(zparts, axis=0) + jnp.where(lane == CHUNK - 1, jnp.concatenate(vparts, axis=0), 0.0)
        hr_ = lax.broadcasted_iota(jnp.int32, (128, D_SSD), 0)
        hc_ = lax.broadcasted_iota(jnp.int32, (128, D_SSD), 1)
        hsel = (hr_ - DT_LANE == jnp.right_shift(hc_, 6)).astype(f32)
        red = jnp.dot(hsel, jnp.concatenate([Z, dU * xsT, dYT * xsT], axis=1), precision=HIGHEST, preferred_element_type=f32)
        dcsT = red[:, 0:CHUNK] + dcs_diag[...]
        daT = jnp.dot(dcsT, (lrow >= scol).astype(f32), precision=HIGHEST, preferred_element_type=f32)
        rowi = lax.broadcasted_iota(jnp.int32, (128, 1), 0)
        hmask = (rowi >= DT_LANE) & (rowi < DT_LANE + SSD_HEADS)
        a_col = jnp.where(hmask, -jnp.exp(alogc_ref[...]), 0.0)
        ddtT = red[:, CHUNK:2 * CHUNK] + a_col * daT
        ddt_rawT = jnp.where(hmask, ddtT * jax.nn.sigmoid(dtb.T + biasc_ref[...]), 0.0)
        ddt_ref[0] = ddt_rawT.T
        accA[...] += daT * dtT
        accD[...] += red[:, 2 * CHUNK:3 * CHUNK]
        accB[...] += ddt_rawT

        @pl.when((pl.program_id(0) == B - 1) & (pl.program_id(1) == nc - 1))
        def _():
            dalog_ref[...] = jnp.broadcast_to(jnp.sum(accA[...], axis=1, keepdims=True) * a_col, (128, 128))
            dd_ref[...] = jnp.broadcast_to(jnp.sum(accD[...], axis=1, keepdims=True), (128, 128))
            dbias_ref[...] = jnp.broadcast_to(jnp.sum(accB[...], axis=1, keepdims=True), (128, 128))

    def rev(width, cb):
        return pl.BlockSpec((1, CHUNK, width), lambda b, c: (b, nc - 1 - c, cb))

    acc_spec = pl.BlockSpec((128, 128), lambda b, c: (0, 0))
    acc_shape = jax.ShapeDtypeStruct((128, 128), f32)
    return pl.pallas_call(body, name="ssd_bwd", grid=(B, nc),
                          in_specs=[rev(D_SSD, 0), rev(512, 2), rev(128, OFF_KRDT // 128), rev(D_SSD, 0),
                                    pl.BlockSpec((1, 1, D_SSD, SSD_N), lambda b, c: (b, nc - 1 - c, 0, 0)),
                                    _const(1, 128), _const(1, 128), _const(128, 1), _const(128, 1), _const(D_SSD, 1)],
                          out_specs=[rev(D_XBC, 0), rev(128, 0), acc_spec, acc_spec, acc_spec],
                          out_shape=[jax.ShapeDtypeStruct((B, S, D_XBC), f32), jax.ShapeDtypeStruct((B, S, 128), f32),
                                     acc_shape, acc_shape, acc_shape],
                          scratch_shapes=[pltpu.VMEM((D_SSD, SSD_N), f32), pltpu.VMEM((D_SSD, CHUNK), f32),
                                          pltpu.VMEM((128, 128), f32), pltpu.VMEM((128, 128), f32), pltpu.VMEM((128, 128), f32),
                                          pltpu.VMEM((128, 128), f32)],
                          compiler_params=_cparams(2))(xc3, xc3, p3, dy3, states, bias_row, alog_row, bias_col, alog_col, dcol)


ATT_SCALE = float(QK) ** -0.5
NEG = -1e30
HP = 2


def _att_block(S):
    return _pick(S, (512, 256, 128))


def attn_fwd(q3, k3, v3, comm=None):
    B, S, _ = q3.shape
    bq = _att_block(S)
    nq = S // bq

    def body(q_ref, k_ref, v_ref, o_ref, lse_ref, m_s, l_s, acc):
        i, j = pl.program_id(2), pl.program_id(3)

        @pl.when(j == 0)
        def _():
            m_s[...] = jnp.full_like(m_s, NEG)
            l_s[...] = jnp.zeros_like(l_s)
            acc[...] = jnp.zeros_like(acc)

        def step(masked):
            for t in range(HP):
                qk = slice(t * 256, (t + 1) * 256)
                s = lax.dot_general(q_ref[0, :, qk], k_ref[0, :, qk], _NT, preferred_element_type=f32) * ATT_SCALE
                if masked:
                    r = lax.broadcasted_iota(jnp.int32, (bq, bq), 0)
                    c = lax.broadcasted_iota(jnp.int32, (bq, bq), 1)
                    s = jnp.where(r >= c, s, NEG)
                m_old = m_s[t]
                m_new = jnp.maximum(m_old, jnp.max(s, axis=-1, keepdims=True))
                alpha = jnp.exp(m_old - m_new)
                p = jnp.exp(s - m_new)
                l_s[t] = alpha * l_s[t] + jnp.sum(p, axis=-1, keepdims=True)
                acc[t] = alpha * acc[t] + jnp.dot(p.astype(bf16), v_ref[0, :, t * VD:(t + 1) * VD], preferred_element_type=f32)
                m_s[t] = m_new

        @pl.when(j < i)
        def _():
            step(False)

        @pl.when(j == i)
        def _():
            step(True)
            for t in range(HP):
                o_ref[0, :, t * VD:(t + 1) * VD] = acc[t] / l_s[t]
                lse_ref[0, t] = m_s[t] + jnp.log(l_s[t])

    grid = (B, MLA_H // HP, nq, nq)
    body, c_args, c_in, c_out, c_shapes, c_sems = _fuse_exchange(body, comm, 3, 2, 3, grid)
    res = pl.pallas_call(body, name="attn_fwd_x" if comm else "attn_fwd", grid=grid,
                         in_specs=[pl.BlockSpec((1, bq, HP * 256), lambda b, h, i, j: (b, i, h)),
                                   pl.BlockSpec((1, bq, HP * 256), lambda b, h, i, j: (b, jnp.minimum(j, i), h)),
                                   pl.BlockSpec((1, bq, HP * VD), lambda b, h, i, j: (b, jnp.minimum(j, i), h))] + c_in,
                         out_specs=[pl.BlockSpec((1, bq, HP * VD), lambda b, h, i, j: (b, i, h)),
                                    pl.BlockSpec((1, HP, bq, 1), lambda b, h, i, j: (b, h, i, 0))] + c_out,
                         out_shape=[jax.ShapeDtypeStruct((B, S, MLA_H * VD), f32), jax.ShapeDtypeStruct((B, MLA_H, S, 1), f32)] + c_shapes,
                         scratch_shapes=[pltpu.VMEM((HP, bq, 1), f32), pltpu.VMEM((HP, bq, 1), f32), pltpu.VMEM((HP, bq, VD), f32)] + c_sems,
                         compiler_params=_cparams(4))(q3, k3, v3, *c_args)
    return res[0], res[1], list(res[2:])


def attn_bwd_dq(q3, k3, v3, o3, do3, lse, comm=None):
    B, S, _ = q3.shape
    bq = _att_block(S)
    nq = S // bq

    def body(q_ref, k_ref, v_ref, o_ref, do_ref, lse_ref, dq_ref, dl_ref, acc, dl_s):
        i, j = pl.program_id(2), pl.program_id(3)

        @pl.when(j == 0)
        def _():
            acc[...] = jnp.zeros_like(acc)
            for t in range(HP):
                vs = slice(t * VD, (t + 1) * VD)
                dl_s[t] = jnp.sum(o_ref[0, :, vs] * do_ref[0, :, vs], axis=-1, keepdims=True)

        def step(masked):
            for t in range(HP):
                qk, vs = slice(t * 256, (t + 1) * 256), slice(t * VD, (t + 1) * VD)
                k = k_ref[0, :, qk]
                s = lax.dot_general(q_ref[0, :, qk], k, _NT, preferred_element_type=f32) * ATT_SCALE
                p = jnp.exp(s - lse_ref[0, t])
                if masked:
                    r = lax.broadcasted_iota(jnp.int32, (bq, bq), 0)
                    c = lax.broadcasted_iota(jnp.int32, (bq, bq), 1)
                    p = jnp.where(r >= c, p, 0.0)
                dp = lax.dot_general(do_ref[0, :, vs].astype(bf16), v_ref[0, :, vs], _NT, preferred_element_type=f32)
                ds = p * (dp - dl_s[t]) * ATT_SCALE
                acc[t] += jnp.dot(ds.astype(bf16), k, preferred_element_type=f32)

        @pl.when(j < i)
        def _():
            step(False)

        @pl.when(j == i)
        def _():
            step(True)
            for t in range(HP):
                dq_ref[0, :, t * 256:(t + 1) * 256] = acc[t]
                dl_ref[0, t] = dl_s[t]

    qspec = pl.BlockSpec((1, bq, HP * 256), lambda b, h, i, j: (b, i, h))
    ospec = pl.BlockSpec((1, bq, HP * VD), lambda b, h, i, j: (b, i, h))
    cspec = pl.BlockSpec((1, HP, bq, 1), lambda b, h, i, j: (b, h, i, 0))
    grid = (B, MLA_H // HP, nq, nq)
    body, c_args, c_in, c_out, c_shapes, c_sems = _fuse_exchange(body, comm, 6, 2, 2, grid)
    res = pl.pallas_call(body, name="attn_bwd_dq_x" if comm else "attn_bwd_dq", grid=grid,
                         in_specs=[qspec, pl.BlockSpec((1, bq, HP * 256), lambda b, h, i, j: (b, jnp.minimum(j, i), h)),
                                   pl.BlockSpec((1, bq, HP * VD), lambda b, h, i, j: (b, jnp.minimum(j, i), h)), ospec, ospec, cspec] + c_in,
                         out_specs=[qspec, cspec] + c_out,
                         out_shape=[jax.ShapeDtypeStruct((B, S, MLA_H * 256), f32), jax.ShapeDtypeStruct((B, MLA_H, S, 1), f32)] + c_shapes,
                         scratch_shapes=[pltpu.VMEM((HP, bq, 256), f32), pltpu.VMEM((HP, bq, 1), f32)] + c_sems,
                         compiler_params=_cparams(4))(q3, k3, v3, o3, do3, lse, *c_args)
    return res[0], res[1], list(res[2:])


def attn_bwd_dkv(q3, k3, v3, do3, lse_row, delta_row):
    B, S, _ = q3.shape
    bq = _att_block(S)
    nq = S // bq

    def body(q_ref, k_ref, v_ref, do_ref, lse_ref, dl_ref, dk_ref, dv_ref, dk_acc, dv_acc):
        j, i = pl.program_id(2), pl.program_id(3)

        @pl.when(i == 0)
        def _():
            dk_acc[...] = jnp.zeros_like(dk_acc)
            dv_acc[...] = jnp.zeros_like(dv_acc)

        def step(masked):
            for t in range(HP):
                qk, vs = slice(t * 256, (t + 1) * 256), slice(t * VD, (t + 1) * VD)
                q = q_ref[0, :, qk]
                do = do_ref[0, :, vs].astype(bf16)
                st = lax.dot_general(k_ref[0, :, qk], q, _NT, preferred_element_type=f32) * ATT_SCALE
                pt = jnp.exp(st - lse_ref[0, t])
                if masked:
                    r = lax.broadcasted_iota(jnp.int32, (bq, bq), 0)
                    c = lax.broadcasted_iota(jnp.int32, (bq, bq), 1)
                    pt = jnp.where(c >= r, pt, 0.0)
                dv_acc[t] += jnp.dot(pt.astype(bf16), do, preferred_element_type=f32)
                dpt = lax.dot_general(v_ref[0, :, vs], do, _NT, preferred_element_type=f32)
                dst = pt * (dpt - dl_ref[0, t]) * ATT_SCALE
                dk_acc[t] += jnp.dot(dst.astype(bf16), q, preferred_element_type=f32)

        @pl.when(i > j)
        def _():
            step(False)

        @pl.when(i == j)
        def _():
            step(True)

        @pl.when(i == nq - 1)
        def _():
            for t in range(HP):
                dk_ref[0, :, t * 256:(t + 1) * 256] = dk_acc[t]
                dv_ref[0, :, t * VD:(t + 1) * VD] = dv_acc[t]

    kspec = pl.BlockSpec((1, bq, HP * 256), lambda b, h, j, i: (b, j, h))
    vspec = pl.BlockSpec((1, bq, HP * VD), lambda b, h, j, i: (b, j, h))
    rspec = pl.BlockSpec((1, HP, 1, bq), lambda b, h, j, i: (b, h, 0, jnp.maximum(i, j)))
    return pl.pallas_call(body, name="attn_bwd_dkv", grid=(B, MLA_H // HP, nq, nq),
                          in_specs=[pl.BlockSpec((1, bq, HP * 256), lambda b, h, j, i: (b, jnp.maximum(i, j), h)), kspec, vspec,
                                    pl.BlockSpec((1, bq, HP * VD), lambda b, h, j, i: (b, jnp.maximum(i, j), h)), rspec, rspec],
                          out_specs=[kspec, vspec],
                          out_shape=[jax.ShapeDtypeStruct((B, S, MLA_H * 256), f32), jax.ShapeDtypeStruct((B, S, MLA_H * VD), f32)],
                          scratch_shapes=[pltpu.VMEM((HP, bq, 256), f32), pltpu.VMEM((HP, bq, VD), f32)],
                          compiler_params=_cparams(4))(q3, k3, v3, do3, lse_row, delta_row)


def ada_fwd(c_all, w, b):
    n = w.shape[1]

    def body(c_ref, w_ref, b_ref, o_ref):
        o_ref[...] = jnp.dot(_silu(c_ref[...]).astype(bf16), w_ref[...].astype(bf16), preferred_element_type=f32) + b_ref[...]

    return pl.pallas_call(body, name="ada_fwd", out_shape=jax.ShapeDtypeStruct((c_all.shape[0], n), f32),
                          compiler_params=pltpu.CompilerParams(vmem_limit_bytes=VMEM_LIMIT))(c_all, w, b)


def ada_bwd(c_all, dmod):
    n = dmod.shape[1]

    def body(c_ref, d_ref, o_ref):
        o_ref[...] = lax.dot_general(_silu(c_ref[...]).astype(bf16), d_ref[...].astype(bf16), _TN, preferred_element_type=f32)

    return pl.pallas_call(body, name="ada_bwd", out_shape=jax.ShapeDtypeStruct((c_all.shape[1], n), f32),
                          compiler_params=pltpu.CompilerParams(vmem_limit_bytes=VMEM_LIMIT))(c_all, dmod)


def sum_leading(x, name):
    n, R, _ = x.shape
    tr = _pick(R, (512, 256, 128, 64, 32, 16, 8))

    def body(x_ref, o_ref):
        acc = x_ref[0].astype(f32)
        for k in range(1, n):
            acc = acc + x_ref[k].astype(f32)
        o_ref[...] = acc

    return pl.pallas_call(body, name=name, grid=(R // tr,), in_specs=[pl.BlockSpec((n, tr, 128), lambda i: (0, i, 0))],
                          out_specs=pl.BlockSpec((tr, 128), lambda i: (i, 0)), out_shape=jax.ShapeDtypeStruct((R, 128), f32),
                          compiler_params=_cparams(1))(x)


def _adamw_body(w_ref, g_ref, m_ref, v_ref, d_ref, mo_ref, vo_ref):
    gv = g_ref[...]
    mn = ADAM_B1 * m_ref[...] + (1.0 - ADAM_B1) * gv
    vn = ADAM_B2 * v_ref[...] + (1.0 - ADAM_B2) * jnp.square(gv)
    m_hat = mn / (1.0 - ADAM_B1 ** ADAM_STEP)
    v_hat = vn / (1.0 - ADAM_B2 ** ADAM_STEP)
    d_ref[...] = -ADAM_LR * (m_hat / (jnp.sqrt(v_hat) + ADAM_EPS) + ADAM_WD * w_ref[...])
    mo_ref[...] = mn
    vo_ref[...] = vn


def adamw(w, g, m, v):
    R = w.shape[0]
    tr = _pick(R, (512, 256, 128, 64, 32, 16, 8))
    spec = pl.BlockSpec((tr, 128), lambda i: (i, 0))
    shp = jax.ShapeDtypeStruct((R, 128), f32)
    return pl.pallas_call(functools.partial(_adamw_body), name="adamw", grid=(R // tr,), in_specs=[spec] * 4,
                          out_specs=[spec] * 3, out_shape=[shp] * 3, compiler_params=_cparams(1))(w, g, m, v)


def _row_tile(a):
    return _pick(a, (256, 128, 64, 32, 16, 8)) if a % 8 == 0 else a


def adamw_nd(w, g, m, v):
    L, a, b = w.shape
    ta = _row_tile(a)
    spec = pl.BlockSpec((1, ta, b), lambda l, i: (l, i, 0))
    shp = jax.ShapeDtypeStruct((L, a, b), f32)
    return pl.pallas_call(functools.partial(_adamw_body), name="adamw_nd", grid=(L, a // ta), in_specs=[spec] * 4,
                          out_specs=[spec] * 3, out_shape=[shp] * 3, compiler_params=_cparams(2))(w, g, m, v)


def sum_slots(x):
    n, L, a, b = x.shape
    ta = _row_tile(a)

    def body(x_ref, o_ref):
        acc = x_ref[0].astype(f32)
        for k in range(1, n):
            acc = acc + x_ref[k].astype(f32)
        o_ref[...] = acc

    return pl.pallas_call(body, name="sum_slots", grid=(L, a // ta),
                          in_specs=[pl.BlockSpec((n, 1, ta, b), lambda l, i: (0, l, i, 0))],
                          out_specs=pl.BlockSpec((1, ta, b), lambda l, i: (l, i, 0)),
                          out_shape=jax.ShapeDtypeStruct((L, a, b), f32), compiler_params=_cparams(2))(x)


def _exchange_copies(ins, outs, sems, scatter):
    send_sems, recv_sems, local_sems = sems
    x, y, c = lax.axis_index("x"), lax.axis_index("y"), lax.axis_index("c")
    me = 4 * x + 2 * y + c
    locals_, sends, recvs = [], [], []
    for a in range(len(ins)):
        locals_.append(pltpu.make_async_copy(ins[a].at[me] if scatter else ins[a], outs[a].at[me], local_sems.at[a]))
        for k in range(N_DEV - 1):
            px = 1 - x if (k + 1) & 4 else x
            py = 1 - y if (k + 1) & 2 else y
            pc = 1 - c if (k + 1) & 1 else c
            pid = 4 * px + 2 * py + pc
            src = ins[a].at[pid] if scatter else ins[a]
            for slot, group in ((me, sends), (pid, recvs)):
                group.append(pltpu.make_async_remote_copy(src_ref=src, dst_ref=outs[a].at[slot], send_sem=send_sems.at[a, k],
                                                          recv_sem=recv_sems.at[a, k], device_id=(px, py, pc),
                                                          device_id_type=pl.DeviceIdType.MESH))
    return locals_, sends, recvs


def _exchange_start(ins, outs, sems, scatter):
    locals_, sends, _ = _exchange_copies(ins, outs, sems, scatter)
    for cp in locals_ + sends:
        cp.start()


def _exchange_wait(ins, outs, sems, scatter):
    locals_, sends, recvs = _exchange_copies(ins, outs, sems, scatter)
    for cp in recvs:
        cp.wait_recv()
    for cp in sends:
        cp.wait_send()
    for cp in locals_:
        cp.wait()


def _exchange_shapes(arrays, scatter):
    return [jax.ShapeDtypeStruct((N_DEV,) + tuple(a.shape[1:] if scatter else a.shape), a.dtype) for a in arrays]


def _exchange_sems(n):
    return [pltpu.SemaphoreType.DMA((n, N_DEV - 1)), pltpu.SemaphoreType.DMA((n, N_DEV - 1)), pltpu.SemaphoreType.DMA((n,))]


def _fuse_exchange(core, comm, n_in, n_out, n_scr, grid):
    if comm is None:
        return core, [], [], [], [], []
    arrays, scatter = comm
    n = len(arrays)

    def body(*refs):
        a, b, c = n_in + n, n_in + n + n_out, n_in + 2 * n + n_out
        cin, cout, sems = refs[n_in:a], refs[b:c], refs[c + n_scr:]
        ids = [pl.program_id(d) for d in range(len(grid))]
        first = functools.reduce(lambda p, q: p & q, [i == 0 for i in ids])
        last = functools.reduce(lambda p, q: p & q, [i == g - 1 for i, g in zip(ids, grid)])

        @pl.when(first)
        def _():
            _exchange_start(cin, cout, sems, scatter)

        core(*refs[:n_in], *refs[a:b], *refs[c:c + n_scr])

        @pl.when(last)
        def _():
            _exchange_wait(cin, cout, sems, scatter)

    hbm = pl.BlockSpec(memory_space=pltpu.HBM)
    return body, list(arrays), [hbm] * n, [hbm] * n, _exchange_shapes(arrays, scatter), _exchange_sems(n)


def exchange(arrays, scatter, name):
    n = len(arrays)

    def body(*refs):
        ins, outs, sems = refs[:n], refs[n:2 * n], refs[2 * n:]
        _exchange_start(ins, outs, sems, scatter)
        _exchange_wait(ins, outs, sems, scatter)

    hbm = pl.BlockSpec(memory_space=pltpu.HBM)
    return pl.pallas_call(body, name=name, in_specs=[hbm] * n, out_specs=[hbm] * n,
                          out_shape=_exchange_shapes(arrays, scatter), scratch_shapes=_exchange_sems(n))(*arrays)


BIG = (("w_in", "col"), ("conv_w", "col"), ("w_uq", "col"), ("w_ukv", "col"), ("w_out", "row"), ("w_up", "col"),
       ("conv_ff_w", "col"), ("w_down", "row"))
SMALL = ("b_ada", "norm_mix", "conv_b", "dt_bias", "a_log", "d_skip", "ssd_norm", "q_norm", "kv_norm", "attn_norm",
         "norm_mlp", "conv_ff_b", "final_norm")
CONVS = ("conv_w", "conv_ff_w")
PACK_ALIGN = 2048


def _padded(n):
    return -(-n // PACK_ALIGN) * PACK_ALIGN


def _flat_pad(a):
    f = a.reshape(-1)
    return jnp.pad(f, (0, _padded(f.shape[0]) - f.shape[0]))


PACK_ROWS = 512


def pack(arrs):
    f = jnp.concatenate([_flat_pad(a) for a in arrs])
    n = PACK_ROWS * 128
    return jnp.pad(f, (0, -(-f.shape[0] // n) * n - f.shape[0])).reshape(-1, 128)


def unpack(flat, shapes):
    f = flat.reshape(-1)
    out, off = [], 0
    for s in shapes:
        n = int(np.prod(s))
        out.append(f[off:off + n].reshape(s))
        off += _padded(n)
    return out


def shards_to_full(g, kind):
    _, a, b = g.shape
    if kind == "col":
        return g.transpose(1, 0, 2).reshape(a, N_DEV * b)
    return g.reshape(N_DEV * a, b)


def full_to_shards(full, kind):
    if kind == "col":
        a, nb = full.shape
        return full.reshape(a, N_DEV, nb // N_DEV).transpose(1, 0, 2)
    na, b = full.shape
    return full.reshape(N_DEV, na // N_DEV, b)


def layer_weights(gathered):
    full = {n: shards_to_full(g, kind) for (n, kind), g in zip(BIG, gathered)}
    full["w_in"], full["w_uq"], full["w_ukv"] = w_in_layout(full["w_in"]), w_uq_layout(full["w_uq"]), w_ukv_layout(full["w_ukv"])
    return full


def layer_grad_slices(g):
    g = dict(g, w_in=w_in_unlayout(g["w_in"]), w_uq=w_uq_unlayout(g["w_uq"]), w_ukv=w_ukv_unlayout(g["w_ukv"]))
    return [full_to_shards(g[n], kind).astype(bf16) for n, kind in BIG]


def w_in_layout(w):
    z = lambda n: jnp.zeros(w.shape[:-1] + (n,), w.dtype)
    return jnp.concatenate([w[..., :2560], w[..., 2576:2960], z(128), w[..., 2960:3216], w[..., 3216:3280],
                            w[..., 2560:2576], z(48)], axis=-1)


def w_in_unlayout(g):
    return jnp.concatenate([g[..., :2560], g[..., 3392:3408], g[..., 2560:2944], g[..., 3072:3328], g[..., 3328:3392]], axis=-1)


def w_uq_layout(w):
    return jnp.pad(w.reshape(Q_RANK, MLA_H, QK), ((0, 0), (0, 0), (0, 256 - QK))).reshape(Q_RANK, MLA_H * 256)


def w_uq_unlayout(g):
    return g.reshape(Q_RANK, MLA_H, 256)[:, :, :QK].reshape(Q_RANK, MLA_H * QK)


def w_ukv_layout(w):
    return w.reshape(KV_RANK, MLA_H, 2, 128).transpose(0, 2, 1, 3).reshape(KV_RANK, 2 * MLA_H * 128)


def w_ukv_unlayout(g):
    return g.reshape(KV_RANK, 2, MLA_H, 128).transpose(0, 2, 1, 3).reshape(KV_RANK, 2 * MLA_H * 128)


def _head_row(v):
    return jnp.zeros((1, 128), f32).at[0, DT_LANE:DT_LANE + SSD_HEADS].set(v)


def layer_fwd(x3, mod, W, P, l, cosf, sinf, comm=None):
    B, S, _ = x3.shape
    T = B * S
    sv = {}
    h = normmod_fwd(x3, mod, P["norm_mix"][l][None], 0, 1)
    p = mm(h.reshape(T, D), W["w_in"], "nn", "mm_in")
    p3 = p.reshape(B, S, IN_COLS)
    bias_row, alog_row = _head_row(P["dt_bias"][l]), _head_row(P["a_log"][l])
    dcol = jnp.repeat(P["d_skip"][l], SSD_HD)[:, None]
    xc3 = conv_ssd_fwd(p3, W["conv_w"], P["conv_b"][l][None])
    yc3, states = ssd_fwd(xc3, p3, bias_row, alog_row, dcol)
    y_ssd = ssd_out_fwd(yc3, p3, P["ssd_norm"][l][None])
    cqn = rms_fwd(p3, 512, OFF_CQ // 512, Q_RANK, P["q_norm"][l][None], "rms_q_fwd")
    ckvn = rms_fwd(p3, KV_RANK, OFF_CKV // KV_RANK, KV_RANK, P["kv_norm"][l][None], "rms_kv_fwd")
    qraw = mm(cqn.reshape(T, Q_RANK), W["w_uq"], "nn", "mm_uq")
    kvraw = mm(ckvn.reshape(T, KV_RANK), W["w_ukv"], "nn", "mm_ukv")
    q3 = qprep_fwd(qraw.reshape(B, S, -1), cosf, sinf)
    k3, v3 = kprep_fwd(kvraw.reshape(B, S, -1), p3, cosf, sinf)
    o3, lse, comm_out = attn_fwd(q3, k3, v3, comm)
    y_att = rms_fwd(o3, D, 0, D, P["attn_norm"][l][None], "rms_o_fwd")
    cat = jnp.concatenate([y_ssd, y_att], axis=-1).reshape(T, 2 * D)
    x1, y1 = mm(cat, W["w_out"], "nn", "mm_out", resid=x3.reshape(T, D), gate=mod[:, 2:3, :], seq=S)
    x13 = x1.reshape(B, S, D)
    h2 = normmod_fwd(x13, mod, P["norm_mlp"][l][None], 3, 4)
    u = mm(h2.reshape(T, D), W["w_up"], "nn", "mm_up", out_dtype=bf16)
    u3 = u.reshape(B, S, 2 * D_FF)
    a = glu_fwd(u3, W["conv_ff_w"], P["conv_ff_b"][l][None])
    x2, y2 = mm(a.reshape(T, D_FF), W["w_down"], "nn", "mm_down", resid=x1, gate=mod[:, 5:6, :], seq=S)
    sv.update(x=x3, h=h, p3=p3, xc3=xc3, yc3=yc3, states=states, cqn=cqn, ckvn=ckvn, q3=q3, k3=k3, v3=v3, o3=o3, lse=lse,
              cat=cat, y1=y1, x1=x13, h2=h2, u3=u3, a=a, y2=y2, bias_row=bias_row, alog_row=alog_row, dcol=dcol)
    return x2.reshape(B, S, D), sv, comm_out


def layer_bwd(dx3, sv, mod, W, P, l, cosf, sinf, comm=None):
    B, S, _ = dx3.shape
    T = B * S
    g = {}
    dy2, dg2 = gate_bwd(dx3, sv["y2"].reshape(B, S, D), mod, 5)
    dy2 = dy2.reshape(T, D)
    da = mm(dy2, W["w_down"], "nt", "mm_down_dx")
    g["w_down"] = mm(sv["a"].reshape(T, D_FF), dy2, "tn", "mm_down_dw")
    dug, duv, dwg, dwv, dbg, dbv = glu_bwd(sv["u3"], da.reshape(B, S, D_FF), W["conv_ff_w"], P["conv_ff_b"][l][None])
    g["conv_ff_w"] = jnp.concatenate([dwg, dwv], axis=1)
    g["conv_ff_b"] = jnp.concatenate([dbg, dbv], axis=1)[0]
    du = jnp.concatenate([dug, duv], axis=-1).reshape(T, 2 * D_FF)
    dh2 = mm(du, W["w_up"], "nt", "mm_up_dx")
    g["w_up"] = mm(sv["h2"].reshape(T, D), du, "tn", "mm_up_dw")
    dx1, dsh2, dsc2, dnm = normmod_bwd(sv["x1"], dh2.reshape(B, S, D), dx3, mod, P["norm_mlp"][l][None], 4)
    g["norm_mlp"] = dnm[0]
    dy1, dg1 = gate_bwd(dx1, sv["y1"].reshape(B, S, D), mod, 2)
    dy1 = dy1.reshape(T, D)
    dcat = mm(dy1, W["w_out"], "nt", "mm_out_dx")
    g["w_out"] = mm(sv["cat"], dy1, "tn", "mm_out_dw")
    dcat3 = dcat.reshape(B, S, 2 * D)
    do3, dan = rms_bwd(sv["o3"], D, 0, D, dcat3, 1, P["attn_norm"][l][None], f32, "rms_o_bwd")
    g["attn_norm"] = dan[0]
    dq3, delta, comm_out = attn_bwd_dq(sv["q3"], sv["k3"], sv["v3"], sv["o3"], do3, sv["lse"], comm)
    dk3, dv3 = attn_bwd_dkv(sv["q3"], sv["k3"], sv["v3"], do3, sv["lse"].reshape(B, MLA_H, 1, S), delta.reshape(B, MLA_H, 1, S))
    dqraw = qprep_bwd(dq3, cosf, sinf).reshape(T, -1)
    dcqn = mm(dqraw, W["w_uq"], "nt", "mm_uq_dx")
    g["w_uq"] = mm(sv["cqn"].reshape(T, Q_RANK), dqraw, "tn", "mm_uq_dw")
    dcq, dqn = rms_bwd(sv["p3"], 512, OFF_CQ // 512, Q_RANK, dcqn.reshape(B, S, Q_RANK), 0, P["q_norm"][l][None], bf16, "rms_q_bwd")
    g["q_norm"] = dqn[0]
    dyc3, dz, dsn = ssd_out_bwd(sv["yc3"], sv["p3"], dcat3, P["ssd_norm"][l][None])
    g["ssd_norm"] = dsn[0]
    bias_col, alog_col = sv["bias_row"].reshape(128, 1), sv["alog_row"].reshape(128, 1)
    dxc3, ddt3, dalog, dd, dbias = ssd_bwd(sv["xc3"], sv["p3"], dyc3, sv["states"], sv["bias_row"], sv["alog_row"],
                                           bias_col, alog_col, sv["dcol"])
    heads = slice(DT_LANE, DT_LANE + SSD_HEADS)
    g["a_log"], g["d_skip"], g["dt_bias"] = dalog[heads, 0], dd[heads, 0], dbias[heads, 0]
    dxbc, dcw, dcb = conv_ssd_bwd(sv["p3"], dxc3, W["conv_w"], P["conv_b"][l][None])
    g["conv_w"], g["conv_b"] = dcw, dcb[0]
    dkvraw, dkrdt = kprep_bwd(dk3, dv3, ddt3, cosf, sinf)
    dkvraw = dkvraw.reshape(T, -1)
    dckvn = mm(dkvraw, W["w_ukv"], "nt", "mm_ukv_dx")
    g["w_ukv"] = mm(sv["ckvn"].reshape(T, KV_RANK), dkvraw, "tn", "mm_ukv_dw")
    dckv, dkn = rms_bwd(sv["p3"], KV_RANK, OFF_CKV // KV_RANK, KV_RANK, dckvn.reshape(B, S, KV_RANK), 0, P["kv_norm"][l][None],
                        bf16, "rms_kv_bwd")
    g["kv_norm"] = dkn[0]
    dp = jnp.concatenate([dz, dxbc, dcq, jnp.zeros((B, S, 128), bf16), dckv, dkrdt], axis=-1).reshape(T, IN_COLS)
    dh = mm(dp, W["w_in"], "nt", "mm_in_dx")
    g["w_in"] = mm(sv["h"].reshape(T, D), dp, "tn", "mm_in_dw")
    dx0, dsh1, dsc1, dnx = normmod_bwd(sv["x"], dh.reshape(B, S, D), dx1, mod, P["norm_mix"][l][None], 1)
    g["norm_mix"] = dnx[0]
    dmod = jnp.concatenate([dsh1, dsc1, dg1, dsh2, dsc2, dg2], axis=1)
    return dx0, dmod, g, comm_out


def kernel(x, c, positions, w_ada, b_ada, norm_mix, w_in, conv_w, conv_b, dt_bias, a_log, d_skip, ssd_norm, q_norm, w_uq, kv_norm, w_ukv, attn_norm, w_out, norm_mlp, w_up, conv_ff_w, conv_ff_b, w_down, final_norm, loss_target, m_w_ada, m_b_ada, m_norm_mix, m_w_in, m_conv_w, m_conv_b, m_dt_bias, m_a_log, m_d_skip, m_ssd_norm, m_q_norm, m_w_uq, m_kv_norm, m_w_ukv, m_attn_norm, m_w_out, m_norm_mlp, m_w_up, m_conv_ff_w, m_conv_ff_b, m_w_down, m_final_norm, v_w_ada, v_b_ada, v_norm_mix, v_w_in, v_conv_w, v_conv_b, v_dt_bias, v_a_log, v_d_skip, v_ssd_norm, v_q_norm, v_w_uq, v_kv_norm, v_w_ukv, v_attn_norm, v_w_out, v_norm_mlp, v_w_up, v_conv_ff_w, v_conv_ff_b, v_w_down, v_final_norm):
    given = dict(locals())
    B, S, _ = x.shape
    me = 4 * lax.axis_index("x") + 2 * lax.axis_index("y") + lax.axis_index("c")
    P = {n: given[n] for n in SMALL}

    def shards(l):
        return [given[n][l] if n in CONVS else given[n][l].astype(bf16) for n, _ in BIG]

    *gathered, c_all = exchange(shards(0) + [c], False, "gather_weights")
    W = [layer_weights(gathered), None]

    n_ada = w_ada.shape[2]
    c_all = c_all.reshape(N_DEV * B, D)
    b_sh = lax.dynamic_slice_in_dim(b_ada, me * n_ada, n_ada, axis=1)
    mod_sh = jnp.stack([ada_fwd(c_all, w_ada[l], b_sh[l][None]) for l in range(DEPTH)])
    (mod_g,) = exchange([mod_sh], False, "gather_mod")
    mod_mine = lax.dynamic_slice_in_dim(mod_g, me * B, B, axis=2)
    mods = mod_mine.transpose(1, 2, 0, 3).reshape(DEPTH, B, 6, D)

    inv_freq = jnp.asarray(1.0 / (ROPE_BASE ** (np.arange(0, ROPE, 2, dtype=np.float32) / ROPE)))
    ang = positions.astype(f32)[..., None] * inv_freq
    zeros = jnp.zeros((B, S, 128 - ROPE), f32)
    cosf = jnp.concatenate([jnp.cos(ang), jnp.cos(ang), zeros], axis=-1)
    sinf = jnp.concatenate([jnp.sin(ang), jnp.sin(ang), zeros], axis=-1)

    saved = [None] * DEPTH
    xl, saved[0], gathered = layer_fwd(x, mods[0], W[0], P, 0, cosf, sinf, comm=(shards(1), False))
    W[1] = layer_weights(gathered)
    xl, saved[1], _ = layer_fwd(xl, mods[1], W[1], P, 1, cosf, sinf)
    dxl, d_final, loss_part = final_loss(xl, final_norm[None], loss_target)
    grads, dmods, recv = [None] * DEPTH, [None] * DEPTH, [None] * DEPTH
    dxl, dmods[1], grads[1], _ = layer_bwd(dxl, saved[1], mods[1], W[1], P, 1, cosf, sinf)
    grad_x, dmods[0], grads[0], recv[1] = layer_bwd(dxl, saved[0], mods[0], W[0], P, 0, cosf, sinf,
                                                    comm=(layer_grad_slices(grads[1]), True))

    recv[0] = exchange(layer_grad_slices(grads[0]), True, "scatter_grads")
    big_g = [jnp.concatenate([sum_slots(recv[l][i][:, None]) for l in range(DEPTH)]) for i in range(len(BIG))]
    stack = lambda n: jnp.stack([grads[l][n] for l in range(DEPTH)])

    small_names = [n for n in SMALL if n not in ("b_ada", "final_norm")]
    partial = pack([stack(n) for n in small_names] + [d_final[0], loss_part[0]])
    dmod_all = jnp.stack(dmods)
    part_g, dmod_g = exchange([partial, dmod_all], False, "gather_partials")
    small_sum = sum_leading(part_g, "sum_partials")
    small_g = unpack(small_sum, [given[n].shape for n in small_names] + [(D,), (128,)])
    gsmall = dict(zip(small_names + ["final_norm"], small_g[:-1]))
    loss = small_g[-1][0]
    dmod_rows = dmod_g.transpose(0, 2, 1, 3, 4).reshape(N_DEV * B, DEPTH * 6 * D)
    gsmall["b_ada"] = sum_leading(dmod_rows.reshape(N_DEV * B, -1, 128), "sum_b_ada").reshape(DEPTH, 6 * D)
    dmod_cols = dmod_rows.reshape(N_DEV * B, DEPTH, N_DEV, n_ada)
    dmod_sh = lax.dynamic_slice_in_dim(dmod_cols, me, 1, axis=2)[:, :, 0, :]
    g_w_ada = jnp.stack([ada_bwd(c_all, dmod_sh[:, l, :]) for l in range(DEPTH)])

    res = {"grad": {}, "delta": {}, "new_m": {}, "new_v": {}}
    for n, gv in zip([n for n, _ in BIG] + ["w_ada"], big_g + [g_w_ada]):
        res["grad"][n] = gv
        res["delta"][n], res["new_m"][n], res["new_v"][n] = adamw_nd(given[n], gv, given["m_" + n], given["v_" + n])
    shapes = [given[n].shape for n in SMALL]
    flat = adamw(pack([given[n] for n in SMALL]), pack([gsmall[n] for n in SMALL]), pack([given["m_" + n] for n in SMALL]),
                 pack([given["v_" + n] for n in SMALL]))
    for n in SMALL:
        res["grad"][n] = gsmall[n]
    for key, arr in zip(("delta", "new_m", "new_v"), flat):
        res[key].update(zip(SMALL, unpack(arr, shapes)))
    order = ["w_ada", "b_ada", "norm_mix", "w_in", "conv_w", "conv_b", "dt_bias", "a_log", "d_skip", "ssd_norm", "q_norm", "w_uq",
             "kv_norm", "w_ukv", "attn_norm", "w_out", "norm_mlp", "w_up", "conv_ff_w", "conv_ff_b", "w_down", "final_norm"]
    return (loss, grad_x, *[res[k][n] for k in ("grad", "delta", "new_m", "new_v") for n in order])
```

```python
import functools

import numpy as np
import jax
import jax.numpy as jnp
from jax import lax
from jax.experimental import pallas as pl
from jax.experimental.pallas import tpu as pltpu

f32, bf16 = jnp.float32, jnp.bfloat16
HIGHEST = lax.Precision.HIGHEST

D = 1024
D_SSD = 1024
SSD_HEADS = 16
SSD_HD = 64
SSD_N = 128
CHUNK = 128
D_XBC = 1536
CONV_K = 4
MLA_H = 8
NOPE = 128
ROPE = 64
VD = 128
QK = NOPE + ROPE
Q_RANK = 384
KV_RANK = 256
D_FF = 2816
FF_K = 3
EPS = 1e-6
ROPE_BASE = 10000.0
DEPTH = 2
ADAM_LR, ADAM_B1, ADAM_B2, ADAM_EPS, ADAM_WD, ADAM_STEP = 0.001, 0.9, 0.999, 1e-08, 0.01, 10

N_DEV = 8
IN_COLS = 3456
OFF_XBC, OFF_CQ, OFF_CKV, OFF_KRDT = 1024, 2560, 3072, 3328
DT_LANE = 64
VMEM_LIMIT = 48 * 1024 * 1024
MM_K_WHOLE = 4096


def _cparams(n_grid):
    return pltpu.CompilerParams(dimension_semantics=("arbitrary",) * n_grid, vmem_limit_bytes=VMEM_LIMIT)


def _pick(n, cands):
    for c in cands:
        if n % c == 0:
            return c
    return n


def _silu(x):
    return x * jax.nn.sigmoid(x)


def _dsilu(x):
    s = jax.nn.sigmoid(x)
    return s * (1.0 + x * (1.0 - s))


def _rowsum(x):
    return jnp.sum(x, axis=0, keepdims=True)


def mm(a, b, mode, name, out_dtype=f32, resid=None, gate=None, seq=None):
    if mode == "nn":
        (M, K), N = a.shape, b.shape[1]
    elif mode == "nt":
        (M, K), N = a.shape, b.shape[0]
    else:
        (K, M), N = a.shape, b.shape[1]
    gated = resid is not None
    tm = _pick(seq if gated else M, (1024, 1408, 512, 384, 256, 128))
    tn = _pick(N, (512, 384, 256, 128))
    tk = K if K <= MM_K_WHOLE else _pick(K, (2816, 2048, 1024, 512))
    nk = K // tk
    dims = {"nn": ((1,), (0,)), "nt": ((1,), (1,)), "tn": ((0,), (0,))}[mode]

    def body(a_ref, b_ref, *rest):
        if gated:
            r_ref, g_ref, o_ref, y_ref, acc = rest
        else:
            o_ref, acc = rest

        def finish(res):
            if gated:
                y_ref[...] = res
                o_ref[...] = r_ref[...] + g_ref[0] * res
            else:
                o_ref[...] = res.astype(out_dtype)

        prod = lax.dot_general(a_ref[...].astype(bf16), b_ref[...].astype(bf16), (dims, ((), ())), preferred_element_type=f32)
        if nk == 1:
            finish(prod)
        else:
            k = pl.program_id(2)

            @pl.when(k == 0)
            def _():
                acc[...] = prod

            @pl.when(k > 0)
            def _():
                acc[...] += prod

            @pl.when(k == nk - 1)
            def _():
                finish(acc[...])

    a_spec = pl.BlockSpec((tk, tm), lambda i, j, k: (k, i)) if mode == "tn" else pl.BlockSpec((tm, tk), lambda i, j, k: (i, k))
    b_spec = pl.BlockSpec((tn, tk), lambda i, j, k: (j, k)) if mode == "nt" else pl.BlockSpec((tk, tn), lambda i, j, k: (k, j))
    o_spec = pl.BlockSpec((tm, tn), lambda i, j, k: (i, j))
    in_specs, args = [a_spec, b_spec], [a, b]
    out_specs, out_shape = o_spec, jax.ShapeDtypeStruct((M, N), out_dtype)
    if gated:
        per = seq // tm
        in_specs += [o_spec, pl.BlockSpec((1, 1, tn), lambda i, j, k: (i // per, 0, j))]
        args += [resid, gate]
        out_specs = [o_spec, o_spec]
        out_shape = [jax.ShapeDtypeStruct((M, N), f32), jax.ShapeDtypeStruct((M, N), f32)]
    return pl.pallas_call(body, name=name, grid=(M // tm, N // tn, nk), in_specs=in_specs, out_specs=out_specs,
                          out_shape=out_shape, scratch_shapes=[pltpu.VMEM((tm, tn), f32)],
                          compiler_params=_cparams(3))(*args)


def _tok(ts, width, cb=0):
    return pl.BlockSpec((1, ts, width), lambda b, s: (b, s, cb))


def _perb(rows, width):
    return pl.BlockSpec((1, rows, width), lambda b, s: (b, 0, 0))


def _const(rows, width):
    return pl.BlockSpec((rows, width), lambda b, s: (0, 0))


def _row_call(body, name, B, S, ts, in_specs, out_specs, out_shape, scratch=()):
    return pl.pallas_call(body, name=name, grid=(B, S // ts), in_specs=in_specs, out_specs=out_specs,
                          out_shape=out_shape, scratch_shapes=list(scratch), compiler_params=_cparams(2))


def _first():
    return (pl.program_id(0) == 0) & (pl.program_id(1) == 0)


def normmod_fwd(x3, mod, g, i_sh, i_sc):
    B, S, C = x3.shape
    ts = _pick(S, (512, 256, 128))

    def body(x_ref, mod_ref, g_ref, h_ref):
        x = x_ref[0]
        r = lax.rsqrt(jnp.mean(x * x, axis=-1, keepdims=True) + EPS)
        n = x * r * g_ref[...]
        h_ref[0] = (n * (1.0 + mod_ref[0, i_sc:i_sc + 1, :]) + mod_ref[0, i_sh:i_sh + 1, :]).astype(bf16)

    return _row_call(body, "normmod_fwd", B, S, ts, [_tok(ts, C), _perb(6, C), _const(1, C)], _tok(ts, C),
                     jax.ShapeDtypeStruct((B, S, C), bf16))(x3, mod, g)


def normmod_bwd(x3, dh3, resid3, mod, g, i_sc):
    B, S, C = x3.shape
    ts = _pick(S, (512, 256, 128))

    def body(x_ref, dh_ref, r_ref, mod_ref, g_ref, dx_ref, dsh_ref, dsc_ref, dg_ref):
        @pl.when(pl.program_id(1) == 0)
        def _():
            dsh_ref[...] = jnp.zeros_like(dsh_ref)
            dsc_ref[...] = jnp.zeros_like(dsc_ref)

        @pl.when(_first())
        def _():
            dg_ref[...] = jnp.zeros_like(dg_ref)

        x, dh, gv = x_ref[0], dh_ref[0], g_ref[...]
        r = lax.rsqrt(jnp.mean(x * x, axis=-1, keepdims=True) + EPS)
        xh = x * r
        dn = dh * (1.0 + mod_ref[0, i_sc:i_sc + 1, :])
        dsh_ref[0] += _rowsum(dh)
        dsc_ref[0] += _rowsum(dh * xh * gv)
        dg_ref[...] += _rowsum(dn * xh)
        dxh = dn * gv
        dx_ref[0] = r * (dxh - xh * jnp.mean(dxh * xh, axis=-1, keepdims=True)) + r_ref[0]

    return _row_call(body, "normmod_bwd", B, S, ts,
                     [_tok(ts, C), _tok(ts, C), _tok(ts, C), _perb(6, C), _const(1, C)],
                     [_tok(ts, C), _perb(1, C), _perb(1, C), _const(1, C)],
                     [jax.ShapeDtypeStruct((B, S, C), f32), jax.ShapeDtypeStruct((B, 1, C), f32),
                      jax.ShapeDtypeStruct((B, 1, C), f32), jax.ShapeDtypeStruct((1, C), f32)])(x3, dh3, resid3, mod, g)


def gate_bwd(dx3, y3, mod, i_g):
    B, S, C = dx3.shape
    ts = _pick(S, (512, 256, 128))

    def body(dx_ref, y_ref, mod_ref, dy_ref, dgate_ref):
        @pl.when(pl.program_id(1) == 0)
        def _():
            dgate_ref[...] = jnp.zeros_like(dgate_ref)

        dx = dx_ref[0]
        dy_ref[0] = (dx * mod_ref[0, i_g:i_g + 1, :]).astype(bf16)
        dgate_ref[0] += _rowsum(dx * y_ref[0])

    return _row_call(body, "gate_bwd", B, S, ts, [_tok(ts, C), _tok(ts, C), _perb(6, C)], [_tok(ts, C), _perb(1, C)],
                     [jax.ShapeDtypeStruct((B, S, C), bf16), jax.ShapeDtypeStruct((B, 1, C), f32)])(dx3, y3, mod)


def rms_fwd(src3, width, cb, n, g, name):
    B, S, _ = src3.shape
    ts = _pick(S, (512, 256, 128))

    def body(x_ref, g_ref, o_ref):
        x = x_ref[0][:, :n]
        r = lax.rsqrt(jnp.mean(x * x, axis=-1, keepdims=True) + EPS)
        o_ref[0] = (x * r * g_ref[...]).astype(bf16)

    return _row_call(body, name, B, S, ts, [_tok(ts, width, cb), _const(1, n)], _tok(ts, n),
                     jax.ShapeDtypeStruct((B, S, n), bf16))(src3, g)


def rms_bwd(src3, width, cb, n, dout3, dcb, g, out_dtype, name):
    B, S, _ = src3.shape
    ts = _pick(S, (512, 256, 128))

    def body(x_ref, do_ref, g_ref, dx_ref, dg_ref):
        @pl.when(_first())
        def _():
            dg_ref[...] = jnp.zeros_like(dg_ref)

        x = x_ref[0][:, :n]
        do = do_ref[0].astype(f32)
        r = lax.rsqrt(jnp.mean(x * x, axis=-1, keepdims=True) + EPS)
        xh = x * r
        dg_ref[...] += _rowsum(do * xh)
        dxh = do * g_ref[...]
        dx_ref[0] = (r * (dxh - xh * jnp.mean(dxh * xh, axis=-1, keepdims=True))).astype(out_dtype)

    return _row_call(body, name, B, S, ts, [_tok(ts, width, cb), _tok(ts, n, dcb), _const(1, n)],
                     [_tok(ts, n), _const(1, n)],
                     [jax.ShapeDtypeStruct((B, S, n), out_dtype), jax.ShapeDtypeStruct((1, n), f32)])(src3, dout3, g)


def final_loss(x3, g, tgt3):
    B, S, C = x3.shape
    ts = _pick(S, (512, 256, 128))

    def body(x_ref, g_ref, t_ref, dx_ref, dg_ref, loss_ref):
        @pl.when(_first())
        def _():
            dg_ref[...] = jnp.zeros_like(dg_ref)
            loss_ref[...] = jnp.zeros_like(loss_ref)

        x, gv = x_ref[0], g_ref[...]
        r = lax.rsqrt(jnp.mean(x * x, axis=-1, keepdims=True) + EPS)
        xh = x * r
        e = xh * gv - t_ref[0]
        loss_ref[...] += 0.5 * jnp.sum(e * e) / C
        dout = e / C
        dg_ref[...] += _rowsum(dout * xh)
        dxh = dout * gv
        dx_ref[0] = r * (dxh - xh * jnp.mean(dxh * xh, axis=-1, keepdims=True))

    return _row_call(body, "final_loss", B, S, ts, [_tok(ts, C), _const(1, C), _tok(ts, C)],
                     [_tok(ts, C), _const(1, C), _const(1, 128)],
                     [jax.ShapeDtypeStruct((B, S, C), f32), jax.ShapeDtypeStruct((1, C), f32),
                      jax.ShapeDtypeStruct((1, 128), f32)])(x3, g, tgt3)


def ssd_out_fwd(yc3, p3, w):
    B, S, C = yc3.shape
    ts = _pick(S, (512, 256, 128))
    half = C // 2

    def body(y_ref, z_ref, w_ref, o_ref):
        y = y_ref[0] * _silu(z_ref[0])
        for lo in (0, half):
            yg = y[:, lo:lo + half]
            r = lax.rsqrt(jnp.mean(yg * yg, axis=-1, keepdims=True) + EPS)
            o_ref[0, :, lo:lo + half] = (yg * r * w_ref[:, lo:lo + half]).astype(bf16)

    return _row_call(body, "ssd_out_fwd", B, S, ts, [_tok(ts, C), _tok(ts, C, 0), _const(1, C)], _tok(ts, C),
                     jax.ShapeDtypeStruct((B, S, C), bf16))(yc3, p3, w)


def ssd_out_bwd(yc3, p3, dcat3, w):
    B, S, C = yc3.shape
    ts = _pick(S, (512, 256, 128))
    half = C // 2

    def body(y_ref, z_ref, do_ref, w_ref, dyc_ref, dz_ref, dw_ref):
        @pl.when(_first())
        def _():
            dw_ref[...] = jnp.zeros_like(dw_ref)

        yc, z, do = y_ref[0], z_ref[0], do_ref[0]
        sz = _silu(z)
        y = yc * sz
        for lo in (0, half):
            sl = slice(lo, lo + half)
            yg, dog, wg = y[:, sl], do[:, sl], w_ref[:, sl]
            r = lax.rsqrt(jnp.mean(yg * yg, axis=-1, keepdims=True) + EPS)
            yh = yg * r
            dw_ref[:, sl] += _rowsum(dog * yh)
            dyh = dog * wg
            dy = r * (dyh - yh * jnp.mean(dyh * yh, axis=-1, keepdims=True))
            dyc_ref[0, :, sl] = dy * sz[:, sl]
            dz_ref[0, :, sl] = (dy * yc[:, sl] * _dsilu(z[:, sl])).astype(bf16)

    return _row_call(body, "ssd_out_bwd", B, S, ts, [_tok(ts, C), _tok(ts, C, 0), _tok(ts, C, 0), _const(1, C)],
                     [_tok(ts, C), _tok(ts, C), _const(1, C)],
                     [jax.ShapeDtypeStruct((B, S, C), f32), jax.ShapeDtypeStruct((B, S, C), bf16),
                      jax.ShapeDtypeStruct((1, C), f32)])(yc3, p3, dcat3, w)


def _rot(t):
    lane = lax.broadcasted_iota(jnp.int32, t.shape, 1)
    return jnp.where(lane < ROPE // 2, -pltpu.roll(t, 128 - ROPE // 2, 1), pltpu.roll(t, ROPE // 2, 1))


def _rope(t, cosf, sinf):
    return t * cosf + _rot(t) * sinf


def _rope_t(d, cosf, sinf):
    return d * cosf - _rot(d * sinf)


def qprep_fwd(qraw3, cosf, sinf):
    B, S, W = qraw3.shape
    ts = _pick(S, (512, 256, 128))

    def body(q_ref, c_ref, s_ref, o_ref):
        c, s = c_ref[0], s_ref[0]
        for h in range(MLA_H):
            o_ref[0, :, h * 256:h * 256 + 128] = q_ref[0, :, h * 256:h * 256 + 128].astype(bf16)
            o_ref[0, :, h * 256 + 128:(h + 1) * 256] = _rope(q_ref[0, :, h * 256 + 128:(h + 1) * 256], c, s).astype(bf16)

    return _row_call(body, "qprep_fwd", B, S, ts, [_tok(ts, W), _tok(ts, 128), _tok(ts, 128)], _tok(ts, W),
                     jax.ShapeDtypeStruct((B, S, W), bf16))(qraw3, cosf, sinf)


def qprep_bwd(dq3, cosf, sinf):
    B, S, W = dq3.shape
    ts = _pick(S, (512, 256, 128))

    def body(d_ref, c_ref, s_ref, o_ref):
        c, s = c_ref[0], s_ref[0]
        for h in range(MLA_H):
            o_ref[0, :, h * 256:h * 256 + 128] = d_ref[0, :, h * 256:h * 256 + 128].astype(bf16)
            o_ref[0, :, h * 256 + 128:(h + 1) * 256] = _rope_t(d_ref[0, :, h * 256 + 128:(h + 1) * 256], c, s).astype(bf16)

    return _row_call(body, "qprep_bwd", B, S, ts, [_tok(ts, W), _tok(ts, 128), _tok(ts, 128)], _tok(ts, W),
                     jax.ShapeDtypeStruct((B, S, W), bf16))(dq3, cosf, sinf)


def kprep_fwd(kv3, p3, cosf, sinf):
    B, S, _ = kv3.shape
    ts = _pick(S, (512, 256, 128))
    Wn = MLA_H * NOPE

    def body(k_ref, v_ref, kr_ref, c_ref, s_ref, ko_ref, vo_ref):
        lane = lax.broadcasted_iota(jnp.int32, (1, 128), 1)
        kr = jnp.where(lane < ROPE, kr_ref[0], 0.0)
        kr = _rope(kr, c_ref[0], s_ref[0]).astype(bf16)
        for h in range(MLA_H):
            ko_ref[0, :, h * 256:h * 256 + 128] = k_ref[0, :, h * 128:(h + 1) * 128].astype(bf16)
            ko_ref[0, :, h * 256 + 128:(h + 1) * 256] = kr
        vo_ref[0] = v_ref[0].astype(bf16)

    return _row_call(body, "kprep_fwd", B, S, ts,
                     [_tok(ts, Wn, 0), _tok(ts, Wn, 1), _tok(ts, 128, OFF_KRDT // 128), _tok(ts, 128), _tok(ts, 128)],
                     [_tok(ts, 2 * Wn), _tok(ts, Wn)],
                     [jax.ShapeDtypeStruct((B, S, 2 * Wn), bf16), jax.ShapeDtypeStruct((B, S, Wn), bf16)])(kv3, kv3, p3, cosf, sinf)


def kprep_bwd(dk3, dv3, ddt3, cosf, sinf):
    B, S, _ = dk3.shape
    ts = _pick(S, (512, 256, 128))
    Wn = MLA_H * NOPE

    def body(dk_ref, dv_ref, ddt_ref, c_ref, s_ref, o_ref, kr_ref):
        acc = jnp.zeros((ts, 128), f32)
        for h in range(MLA_H):
            o_ref[0, :, h * 128:(h + 1) * 128] = dk_ref[0, :, h * 256:h * 256 + 128].astype(bf16)
            acc = acc + dk_ref[0, :, h * 256 + 128:(h + 1) * 256]
        o_ref[0, :, Wn:] = dv_ref[0].astype(bf16)
        lane = lax.broadcasted_iota(jnp.int32, (1, 128), 1)
        dkr = _rope_t(acc, c_ref[0], s_ref[0])
        kr_ref[0] = jnp.where(lane < ROPE, dkr, ddt_ref[0]).astype(bf16)

    return _row_call(body, "kprep_bwd", B, S, ts,
                     [_tok(ts, 2 * Wn), _tok(ts, Wn), _tok(ts, 128), _tok(ts, 128), _tok(ts, 128)],
                     [_tok(ts, 2 * Wn), _tok(ts, 128)],
                     [jax.ShapeDtypeStruct((B, S, 2 * Wn), bf16), jax.ShapeDtypeStruct((B, S, 128), bf16)])(dk3, dv3, ddt3, cosf, sinf)


def _shift_down(u, j):
    if j == 0:
        return u
    row = lax.broadcasted_iota(jnp.int32, u.shape, 0)
    return jnp.where(row < j, 0.0, pltpu.roll(u, j, 0))


def _shift_up(u, j):
    if j == 0:
        return u
    n = u.shape[0]
    row = lax.broadcasted_iota(jnp.int32, u.shape, 0)
    return jnp.where(row >= n - j, 0.0, pltpu.roll(u, n - j, 0))


def _conv(u, w, b, K):
    out = b
    for j in range(K):
        out = out + w[K - 1 - j:K - j, :] * _shift_down(u, j)
    return out


def _conv_bwd(u, du, w, K):
    dins = w[K - 1:K, :] * du
    dws = [None] * K
    dws[K - 1] = _rowsum(du * u)
    for j in range(1, K):
        dins = dins + w[K - 1 - j:K - j, :] * _shift_up(du, j)
        dws[K - 1 - j] = _rowsum(du * _shift_down(u, j))
    return dins, dws


CW = 256


def conv_ssd_fwd(p3, w, b):
    B, S, _ = p3.shape
    nb = D_XBC // CW

    def body(u_ref, w_ref, b_ref, o_ref):
        o_ref[0] = _silu(_conv(u_ref[0], w_ref[...], b_ref[...], CONV_K))

    return pl.pallas_call(body, name="conv_ssd_fwd", grid=(B, nb),
                          in_specs=[pl.BlockSpec((1, S, CW), lambda b, j: (b, 0, OFF_XBC // CW + j)),
                                    pl.BlockSpec((CONV_K, CW), lambda b, j: (0, j)),
                                    pl.BlockSpec((1, CW), lambda b, j: (0, j))],
                          out_specs=pl.BlockSpec((1, S, CW), lambda b, j: (b, 0, j)),
                          out_shape=jax.ShapeDtypeStruct((B, S, D_XBC), f32), compiler_params=_cparams(2))(p3, w, b)


def conv_ssd_bwd(p3, dxc3, w, b):
    B, S, _ = p3.shape
    nb = D_XBC // CW

    def body(u_ref, d_ref, w_ref, b_ref, du_ref, dw_ref, db_ref):
        @pl.when(pl.program_id(1) == 0)
        def _():
            dw_ref[...] = jnp.zeros_like(dw_ref)
            db_ref[...] = jnp.zeros_like(db_ref)

        u, wv = u_ref[0], w_ref[...]
        dpre = d_ref[0] * _dsilu(_conv(u, wv, b_ref[...], CONV_K))
        dins, dws = _conv_bwd(u, dpre, wv, CONV_K)
        du_ref[0] = dins.astype(bf16)
        for k in range(CONV_K):
            dw_ref[k:k + 1, :] += dws[k]
        db_ref[...] += _rowsum(dpre)

    return pl.pallas_call(body, name="conv_ssd_bwd", grid=(nb, B),
                          in_specs=[pl.BlockSpec((1, S, CW), lambda j, b: (b, 0, OFF_XBC // CW + j)),
                                    pl.BlockSpec((1, S, CW), lambda j, b: (b, 0, j)),
                                    pl.BlockSpec((CONV_K, CW), lambda j, b: (0, j)),
                                    pl.BlockSpec((1, CW), lambda j, b: (0, j))],
                          out_specs=[pl.BlockSpec((1, S, CW), lambda j, b: (b, 0, j)),
                                     pl.BlockSpec((CONV_K, CW), lambda j, b: (0, j)),
                                     pl.BlockSpec((1, CW), lambda j, b: (0, j))],
                          out_shape=[jax.ShapeDtypeStruct((B, S, D_XBC), bf16), jax.ShapeDtypeStruct((CONV_K, D_XBC), f32),
                                     jax.ShapeDtypeStruct((1, D_XBC), f32)], compiler_params=_cparams(2))(p3, dxc3, w, b)


def glu_fwd(u3, w, b):
    B, S, _ = u3.shape
    nb = D_FF // CW

    def body(ug_ref, uv_ref, wg_ref, wv_ref, bg_ref, bv_ref, o_ref):
        g = _conv(ug_ref[0].astype(f32), wg_ref[...], bg_ref[...], FF_K)
        v = _conv(uv_ref[0].astype(f32), wv_ref[...], bv_ref[...], FF_K)
        o_ref[0] = (_silu(g) * v).astype(bf16)

    def blk(off):
        return pl.BlockSpec((1, S, CW), lambda b, j: (b, 0, off + j))

    def par(rows, off):
        return pl.BlockSpec((rows, CW), lambda b, j: (0, off + j))

    return pl.pallas_call(body, name="glu_fwd", grid=(B, nb),
                          in_specs=[blk(0), blk(nb), par(FF_K, 0), par(FF_K, nb), par(1, 0), par(1, nb)],
                          out_specs=blk(0), out_shape=jax.ShapeDtypeStruct((B, S, D_FF), bf16),
                          compiler_params=_cparams(2))(u3, u3, w, w, b, b)


def glu_bwd(u3, da3, w, b):
    B, S, _ = u3.shape
    nb = D_FF // CW

    def body(ug_ref, uv_ref, da_ref, wg_ref, wv_ref, bg_ref, bv_ref, dug_ref, duv_ref, dwg_ref, dwv_ref, dbg_ref, dbv_ref):
        @pl.when(pl.program_id(1) == 0)
        def _():
            for r in (dwg_ref, dwv_ref, dbg_ref, dbv_ref):
                r[...] = jnp.zeros_like(r)

        ug, uv, da, wg, wv = ug_ref[0].astype(f32), uv_ref[0].astype(f32), da_ref[0], wg_ref[...], wv_ref[...]
        g = _conv(ug, wg, bg_ref[...], FF_K)
        v = _conv(uv, wv, bv_ref[...], FF_K)
        dg = da * v * _dsilu(g)
        dv = da * _silu(g)
        ding, dwsg = _conv_bwd(ug, dg, wg, FF_K)
        dinv, dwsv = _conv_bwd(uv, dv, wv, FF_K)
        dug_ref[0] = ding.astype(bf16)
        duv_ref[0] = dinv.astype(bf16)
        for k in range(FF_K):
            dwg_ref[k:k + 1, :] += dwsg[k]
            dwv_ref[k:k + 1, :] += dwsv[k]
        dbg_ref[...] += _rowsum(dg)
        dbv_ref[...] += _rowsum(dv)

    def blk(off):
        return pl.BlockSpec((1, S, CW), lambda j, b: (b, 0, off + j))

    def par(rows, off):
        return pl.BlockSpec((rows, CW), lambda j, b: (0, off + j))

    return pl.pallas_call(body, name="glu_bwd", grid=(nb, B),
                          in_specs=[blk(0), blk(nb), blk(0), par(FF_K, 0), par(FF_K, nb), par(1, 0), par(1, nb)],
                          out_specs=[blk(0), blk(0), par(FF_K, 0), par(FF_K, 0), par(1, 0), par(1, 0)],
                          out_shape=[jax.ShapeDtypeStruct((B, S, D_FF), bf16), jax.ShapeDtypeStruct((B, S, D_FF), bf16),
                                     jax.ShapeDtypeStruct((FF_K, D_FF), f32), jax.ShapeDtypeStruct((FF_K, D_FF), f32),
                                     jax.ShapeDtypeStruct((1, D_FF), f32), jax.ShapeDtypeStruct((1, D_FF), f32)],
                          compiler_params=_cparams(2))(u3, u3, da3, w, w, b, b)


def _ssd_decay(dtb, bias_row, alog_row):
    lane = lax.broadcasted_iota(jnp.int32, (1, 128), 1)
    hmask = (lane >= DT_LANE) & (lane < DT_LANE + SSD_HEADS)
    dt = jnp.where(hmask, jax.nn.softplus(dtb + bias_row), 0.0)
    a = dt * jnp.where(hmask, -jnp.exp(alog_row), 0.0)
    r = lax.broadcasted_iota(jnp.int32, (CHUNK, CHUNK), 0)
    c = lax.broadcasted_iota(jnp.int32, (CHUNK, CHUNK), 1)
    cs = jnp.dot((r >= c).astype(f32), a, precision=HIGHEST, preferred_element_type=f32)
    return dt, cs


def _expand(xt):
    return jnp.concatenate([jnp.broadcast_to(xt[DT_LANE + h:DT_LANE + h + 1, :], (SSD_HD, xt.shape[1]))
                            for h in range(SSD_HEADS)], axis=0)


_NT = (((1,), (1,)), ((), ()))
_TN = (((0,), (0,)), ((), ()))
GH = SSD_HEADS // 2
GR = GH * SSD_HD


def ssd_fwd(xc3, p3, bias_row, alog_row, dcol):
    B, S, _ = xc3.shape
    nc = S // CHUNK

    def body(xs_ref, bc_ref, dtb_ref, bias_ref, alog_ref, dcol_ref, y_ref, st_ref, state, yT):
        @pl.when(pl.program_id(1) == 0)
        def _():
            state[...] = jnp.zeros_like(state)

        dt, cs = _ssd_decay(dtb_ref[0], bias_ref[...], alog_ref[...])
        csT = cs.T
        eT = jnp.exp(csT)
        decX = _expand(jnp.exp(csT[:, CHUNK - 1:CHUNK] - csT))
        eX = _expand(eT)
        elastX = eX[:, CHUNK - 1:CHUNK]
        xsT = xs_ref[0].T
        uT = xsT * _expand(dt.T)
        bc = bc_ref[0]
        st_ref[0, 0] = state[...]
        srow = lax.broadcasted_iota(jnp.int32, (CHUNK, CHUNK), 0)
        lcol = lax.broadcasted_iota(jnp.int32, (CHUNK, CHUNK), 1)
        for g in range(2):
            Bg = bc[:, g * SSD_N:(g + 1) * SSD_N].astype(bf16)
            Cg = bc[:, (2 + g) * SSD_N:(3 + g) * SSD_N].astype(bf16)
            GT = lax.dot_general(Bg, Cg, _NT, preferred_element_type=f32)
            rows = slice(g * GR, (g + 1) * GR)
            Sg = state[rows]
            yoffT = lax.dot_general(Sg.astype(bf16), Cg, _NT, preferred_element_type=f32) * eX[rows]
            state[rows] = Sg * elastX[rows] + jnp.dot((uT[rows] * decX[rows]).astype(bf16), Bg, preferred_element_type=f32)
            for k in range(GH):
                h = g * GH + k
                hr = slice(h * SSD_HD, (h + 1) * SSD_HD)
                seg = csT[DT_LANE + h:DT_LANE + h + 1, :] - cs[:, DT_LANE + h:DT_LANE + h + 1]
                LT = jnp.where(lcol >= srow, jnp.exp(jnp.minimum(seg, 0.0)), 0.0)
                yT[hr] = (jnp.dot(uT[hr].astype(bf16), (GT * LT).astype(bf16), preferred_element_type=f32)
                          + yoffT[k * SSD_HD:(k + 1) * SSD_HD] + dcol_ref[hr] * xsT[hr])
        y_ref[0] = yT[...].T

    return pl.pallas_call(body, name="ssd_fwd", grid=(B, nc),
                          in_specs=[pl.BlockSpec((1, CHUNK, D_SSD), lambda b, c: (b, c, 0)),
                                    pl.BlockSpec((1, CHUNK, 512), lambda b, c: (b, c, 2)),
                                    pl.BlockSpec((1, CHUNK, 128), lambda b, c: (b, c, OFF_KRDT // 128)),
                                    _const(1, 128), _const(1, 128), _const(D_SSD, 1)],
                          out_specs=[pl.BlockSpec((1, CHUNK, D_SSD), lambda b, c: (b, c, 0)),
                                     pl.BlockSpec((1, 1, D_SSD, SSD_N), lambda b, c: (b, c, 0, 0))],
                          out_shape=[jax.ShapeDtypeStruct((B, S, D_SSD), f32), jax.ShapeDtypeStruct((B, nc, D_SSD, SSD_N), f32)],
                          scratch_shapes=[pltpu.VMEM((D_SSD, SSD_N), f32), pltpu.VMEM((D_SSD, CHUNK), f32)],
                          compiler_params=_cparams(2))(xc3, xc3, p3, bias_row, alog_row, dcol)


def ssd_bwd(xc3, p3, dy3, states, bias_row, alog_row, bias_col, alog_col, dcol):
    B, S, _ = xc3.shape
    nc = S // CHUNK

    def body(xs_ref, bc_ref, dtb_ref, dy_ref, st_ref, bias_ref, alog_ref, biasc_ref, alogc_ref, dcol_ref,
             dxc_ref, ddt_ref, dalog_ref, dd_ref, dbias_ref, dS, dUT, accA, accD, accB, dcs_diag):
        @pl.when(pl.program_id(1) == 0)
        def _():
            dS[...] = jnp.zeros_like(dS)

        @pl.when(_first())
        def _():
            accA[...] = jnp.zeros_like(accA)
            accD[...] = jnp.zeros_like(accD)
            accB[...] = jnp.zeros_like(accB)

        dtb = dtb_ref[0]
        dt, cs = _ssd_decay(dtb, bias_ref[...], alog_ref[...])
        dtT, csT = dt.T, cs.T
        decX = _expand(jnp.exp(csT[:, CHUNK - 1:CHUNK] - csT))
        eX = _expand(jnp.exp(csT))
        dtX = _expand(dtT)
        elastX = eX[:, CHUNK - 1:CHUNK]
        xsT = xs_ref[0].T
        uT = xsT * dtX
        dYT = dy_ref[0].T
        bc = bc_ref[0]
        lrow = lax.broadcasted_iota(jnp.int32, (CHUNK, CHUNK), 0)
        scol = lax.broadcasted_iota(jnp.int32, (CHUNK, CHUNK), 1)
        dcs_diag[...] = jnp.zeros_like(dcs_diag)
        rs_cols = jnp.zeros((CHUNK, 128), f32)
        vparts, zparts = [], []
        for g in range(2):
            Bf = bc[:, g * SSD_N:(g + 1) * SSD_N]
            Bg = Bf.astype(bf16)
            Cg = bc[:, (2 + g) * SSD_N:(3 + g) * SSD_N].astype(bf16)
            G = lax.dot_general(Cg, Bg, _NT, preferred_element_type=f32)
            BgT = Bf.T.astype(bf16)
            rows = slice(g * GR, (g + 1) * GR)
            dSg = dS[rows]
            Sg = st_ref[0, 0, rows, :]
            dUst = jnp.dot(dSg.astype(bf16), BgT, preferred_element_type=f32) * decX[rows]
            yoffT = lax.dot_general(Sg.astype(bf16), Cg, _NT, preferred_element_type=f32) * eX[rows]
            zparts.append(dYT[rows] * yoffT - dUst * uT[rows])
            dG = jnp.zeros((CHUNK, CHUNK), f32)
            for k in range(GH):
                h = g * GH + k
                hr = slice(h * SSD_HD, (h + 1) * SSD_HD)
                seg = cs[:, DT_LANE + h:DT_LANE + h + 1] - csT[DT_LANE + h:DT_LANE + h + 1, :]
                L = jnp.where(lrow >= scol, jnp.exp(jnp.minimum(seg, 0.0)), 0.0)
                M = G * L
                dYh = dYT[hr].astype(bf16)
                dUT[hr] = jnp.dot(dYh, M.astype(bf16), preferred_element_type=f32) + dUst[k * SSD_HD:(k + 1) * SSD_HD]
                dM = lax.dot_general(dYh, uT[hr].astype(bf16), _TN, preferred_element_type=f32)
                dG = dG + dM * L
                Wm = dM * M
                rs_cols = jnp.where(scol == DT_LANE + h, jnp.sum(Wm, axis=1, keepdims=True), rs_cols)
                dcs_diag[DT_LANE + h:DT_LANE + h + 1, :] = -_rowsum(Wm)
            dGb = dG.astype(bf16)
            dYe = (dYT[rows] * eX[rows]).astype(bf16)
            ude = (uT[rows] * decX[rows]).astype(bf16)
            dC = jnp.dot(dGb, Bg, preferred_element_type=f32) + lax.dot_general(dYe, Sg.astype(bf16), _TN, preferred_element_type=f32)
            dB = (lax.dot_general(dGb, Cg, _TN, preferred_element_type=f32)
                  + lax.dot_general(ude, dSg.astype(bf16), _TN, preferred_element_type=f32))
            dxc_ref[0, :, D_SSD + g * SSD_N:D_SSD + (g + 1) * SSD_N] = dB
            dxc_ref[0, :, D_SSD + (2 + g) * SSD_N:D_SSD + (3 + g) * SSD_N] = dC
            vparts.append(elastX[rows] * jnp.sum(dSg * Sg, axis=1, keepdims=True)
                          + jnp.sum(dUst * uT[rows], axis=1, keepdims=True))
            dS[rows] = elastX[rows] * dSg + jnp.dot(dYe, Cg, preferred_element_type=f32)
        dU = dUT[...]
        dcv = dcol_ref[...]
        dxc_ref[0, :, 0:D_SSD] = (dtX * dU + dcv * dYT).T
        lane = lax.broadcasted_iota(jnp.int32, (D_SSD, CHUNK), 1)
        Z = jnp.concatenate(zparts, axis=0) + jnp.where(lane == CHUNK - 1, jnp.concatenate(vparts, axis=0), 0.0)
        hr_ = lax.broadcasted_iota(jnp.int32, (128, D_SSD), 0)
        hc_ = lax.broadcasted_iota(jnp.int32, (128, D_SSD), 1)
        hsel = (hr_ - DT_LANE == jnp.right_shift(hc_, 6)).astype(bf16)
        summands = jnp.concatenate([Z, dU * xsT, dYT * xsT], axis=1)
        hi = summands.astype(bf16)
        lo = (summands - hi.astype(f32)).astype(bf16)
        red = jnp.dot(hsel, hi, preferred_element_type=f32) + jnp.dot(hsel, lo, preferred_element_type=f32)
        dcsT = red[:, 0:CHUNK] + dcs_diag[...] + rs_cols.T
        daT = jnp.dot(dcsT, (lrow >= scol).astype(f32), precision=HIGHEST, preferred_element_type=f32)
        rowi = lax.broadcasted_iota(jnp.int32, (128, 1), 0)
        hmask = (rowi >= DT_LANE) & (rowi < DT_LANE + SSD_HEADS)
        a_col = jnp.where(hmask, -jnp.exp(alogc_ref[...]), 0.0)
        ddtT = red[:, CHUNK:2 * CHUNK] + a_col * daT
        ddt_rawT = jnp.where(hmask, ddtT * jax.nn.sigmoid(dtb.T + biasc_ref[...]), 0.0)
        ddt_ref[0] = ddt_rawT.T
        accA[...] += daT * dtT
        accD[...] += red[:, 2 * CHUNK:3 * CHUNK]
        accB[...] += ddt_rawT

        @pl.when((pl.program_id(0) == B - 1) & (pl.program_id(1) == nc - 1))
        def _():
            dalog_ref[...] = jnp.broadcast_to(jnp.sum(accA[...], axis=1, keepdims=True) * a_col, (128, 128))
            dd_ref[...] = jnp.broadcast_to(jnp.sum(accD[...], axis=1, keepdims=True), (128, 128))
            dbias_ref[...] = jnp.broadcast_to(jnp.sum(accB[...], axis=1, keepdims=True), (128, 128))

    def rev(width, cb):
        return pl.BlockSpec((1, CHUNK, width), lambda b, c: (b, nc - 1 - c, cb))

    acc_spec = pl.BlockSpec((128, 128), lambda b, c: (0, 0))
    acc_shape = jax.ShapeDtypeStruct((128, 128), f32)
    return pl.pallas_call(body, name="ssd_bwd", grid=(B, nc),
                          in_specs=[rev(D_SSD, 0), rev(512, 2), rev(128, OFF_KRDT // 128), rev(D_SSD, 0),
                                    pl.BlockSpec((1, 1, D_SSD, SSD_N), lambda b, c: (b, nc - 1 - c, 0, 0)),
                                    _const(1, 128), _const(1, 128), _const(128, 1), _const(128, 1), _const(D_SSD, 1)],
                          out_specs=[rev(D_XBC, 0), rev(128, 0), acc_spec, acc_spec, acc_spec],
                          out_shape=[jax.ShapeDtypeStruct((B, S, D_XBC), f32), jax.ShapeDtypeStruct((B, S, 128), f32),
                                     acc_shape, acc_shape, acc_shape],
                          scratch_shapes=[pltpu.VMEM((D_SSD, SSD_N), f32), pltpu.VMEM((D_SSD, CHUNK), f32),
                                          pltpu.VMEM((128, 128), f32), pltpu.VMEM((128, 128), f32), pltpu.VMEM((128, 128), f32),
                                          pltpu.VMEM((128, 128), f32)],
                          compiler_params=_cparams(2))(xc3, xc3, p3, dy3, states, bias_row, alog_row, bias_col, alog_col, dcol)


ATT_SCALE = float(QK) ** -0.5
NEG = -1e30
HP = 2


def _att_block(S):
    return _pick(S, (512, 256, 128))


def attn_fwd(q3, k3, v3, comm=None):
    B, S, _ = q3.shape
    bq = _att_block(S)
    nq = S // bq

    def body(q_ref, k_ref, v_ref, o_ref, lse_ref, m_s, l_s, acc):
        i, j = pl.program_id(2), pl.program_id(3)

        @pl.when(j == 0)
        def _():
            m_s[...] = jnp.full_like(m_s, NEG)
            l_s[...] = jnp.zeros_like(l_s)
            acc[...] = jnp.zeros_like(acc)

        def step(masked):
            for t in range(HP):
                qk = slice(t * 256, (t + 1) * 256)
                s = lax.dot_general(q_ref[0, :, qk], k_ref[0, :, qk], _NT, preferred_element_type=f32) * ATT_SCALE
                if masked:
                    r = lax.broadcasted_iota(jnp.int32, (bq, bq), 0)
                    c = lax.broadcasted_iota(jnp.int32, (bq, bq), 1)
                    s = jnp.where(r >= c, s, NEG)
                m_old = m_s[t]
                m_new = jnp.maximum(m_old, jnp.max(s, axis=-1, keepdims=True))
                alpha = jnp.exp(m_old - m_new)
                p = jnp.exp(s - m_new)
                l_s[t] = alpha * l_s[t] + jnp.sum(p, axis=-1, keepdims=True)
                acc[t] = alpha * acc[t] + jnp.dot(p.astype(bf16), v_ref[0, :, t * VD:(t + 1) * VD], preferred_element_type=f32)
                m_s[t] = m_new

        @pl.when(j < i)
        def _():
            step(False)

        @pl.when(j == i)
        def _():
            step(True)
            for t in range(HP):
                o_ref[0, :, t * VD:(t + 1) * VD] = acc[t] / l_s[t]
                lse_ref[0, t] = m_s[t] + jnp.log(l_s[t])

    grid = (B, MLA_H // HP, nq, nq)
    body, c_args, c_in, c_out, c_shapes, c_sems = _fuse_exchange(body, comm, 3, 2, 3, grid)
    res = pl.pallas_call(body, name="attn_fwd_x" if comm else "attn_fwd", grid=grid,
                         in_specs=[pl.BlockSpec((1, bq, HP * 256), lambda b, h, i, j: (b, i, h)),
                                   pl.BlockSpec((1, bq, HP * 256), lambda b, h, i, j: (b, jnp.minimum(j, i), h)),
                                   pl.BlockSpec((1, bq, HP * VD), lambda b, h, i, j: (b, jnp.minimum(j, i), h))] + c_in,
                         out_specs=[pl.BlockSpec((1, bq, HP * VD), lambda b, h, i, j: (b, i, h)),
                                    pl.BlockSpec((1, HP, bq, 1), lambda b, h, i, j: (b, h, i, 0))] + c_out,
                         out_shape=[jax.ShapeDtypeStruct((B, S, MLA_H * VD), f32), jax.ShapeDtypeStruct((B, MLA_H, S, 1), f32)] + c_shapes,
                         scratch_shapes=[pltpu.VMEM((HP, bq, 1), f32), pltpu.VMEM((HP, bq, 1), f32), pltpu.VMEM((HP, bq, VD), f32)] + c_sems,
                         compiler_params=_cparams(4))(q3, k3, v3, *c_args)
    return res[0], res[1], list(res[2:])


def attn_bwd_dq(q3, k3, v3, o3, do3, lse, comm=None):
    B, S, _ = q3.shape
    bq = _att_block(S)
    nq = S // bq

    def body(q_ref, k_ref, v_ref, o_ref, do_ref, lse_ref, dq_ref, dl_ref, acc, dl_s):
        i, j = pl.program_id(2), pl.program_id(3)

        @pl.when(j == 0)
        def _():
            acc[...] = jnp.zeros_like(acc)
            for t in range(HP):
                vs = slice(t * VD, (t + 1) * VD)
                dl_s[t] = jnp.sum(o_ref[0, :, vs] * do_ref[0, :, vs], axis=-1, keepdims=True)

        def step(masked):
            for t in range(HP):
                qk, vs = slice(t * 256, (t + 1) * 256), slice(t * VD, (t + 1) * VD)
                k = k_ref[0, :, qk]
                s = lax.dot_general(q_ref[0, :, qk], k, _NT, preferred_element_type=f32) * ATT_SCALE
                p = jnp.exp(s - lse_ref[0, t])
                if masked:
                    r = lax.broadcasted_iota(jnp.int32, (bq, bq), 0)
                    c = lax.broadcasted_iota(jnp.int32, (bq, bq), 1)
                    p = jnp.where(r >= c, p, 0.0)
                dp = lax.dot_general(do_ref[0, :, vs].astype(bf16), v_ref[0, :, vs], _NT, preferred_element_type=f32)
                ds = p * (dp - dl_s[t]) * ATT_SCALE
                acc[t] += jnp.dot(ds.astype(bf16), k, preferred_element_type=f32)

        @pl.when(j < i)
        def _():
            step(False)

        @pl.when(j == i)
        def _():
            step(True)
            for t in range(HP):
                dq_ref[0, :, t * 256:(t + 1) * 256] = acc[t]
                dl_ref[0, t] = dl_s[t]

    qspec = pl.BlockSpec((1, bq, HP * 256), lambda b, h, i, j: (b, i, h))
    ospec = pl.BlockSpec((1, bq, HP * VD), lambda b, h, i, j: (b, i, h))
    cspec = pl.BlockSpec((1, HP, bq, 1), lambda b, h, i, j: (b, h, i, 0))
    grid = (B, MLA_H // HP, nq, nq)
    body, c_args, c_in, c_out, c_shapes, c_sems = _fuse_exchange(body, comm, 6, 2, 2, grid)
    res = pl.pallas_call(body, name="attn_bwd_dq_x" if comm else "attn_bwd_dq", grid=grid,
                         in_specs=[qspec, pl.BlockSpec((1, bq, HP * 256), lambda b, h, i, j: (b, jnp.minimum(j, i), h)),
                                   pl.BlockSpec((1, bq, HP * VD), lambda b, h, i, j: (b, jnp.minimum(j, i), h)), ospec, ospec, cspec] + c_in,
                         out_specs=[qspec, cspec] + c_out,
                         out_shape=[jax.ShapeDtypeStruct((B, S, MLA_H * 256), f32), jax.ShapeDtypeStruct((B, MLA_H, S, 1), f32)] + c_shapes,
                         scratch_shapes=[pltpu.VMEM((HP, bq, 256), f32), pltpu.VMEM((HP, bq, 1), f32)] + c_sems,
                         compiler_params=_cparams(4))(q3, k3, v3, o3, do3, lse, *c_args)
    return res[0], res[1], list(res[2:])


def attn_bwd_dkv(q3, k3, v3, do3, lse_row, delta_row, comm=None):
    B, S, _ = q3.shape
    bq = _att_block(S)
    nq = S // bq

    def body(q_ref, k_ref, v_ref, do_ref, lse_ref, dl_ref, dk_ref, dv_ref, dk_acc, dv_acc):
        j, i = pl.program_id(2), pl.program_id(3)

        @pl.when(i == 0)
        def _():
            dk_acc[...] = jnp.zeros_like(dk_acc)
            dv_acc[...] = jnp.zeros_like(dv_acc)

        def step(masked):
            for t in range(HP):
                qk, vs = slice(t * 256, (t + 1) * 256), slice(t * VD, (t + 1) * VD)
                q = q_ref[0, :, qk]
                do = do_ref[0, :, vs].astype(bf16)
                st = lax.dot_general(k_ref[0, :, qk], q, _NT, preferred_element_type=f32) * ATT_SCALE
                pt = jnp.exp(st - lse_ref[0, t])
                if masked:
                    r = lax.broadcasted_iota(jnp.int32, (bq, bq), 0)
                    c = lax.broadcasted_iota(jnp.int32, (bq, bq), 1)
                    pt = jnp.where(c >= r, pt, 0.0)
                dv_acc[t] += jnp.dot(pt.astype(bf16), do, preferred_element_type=f32)
                dpt = lax.dot_general(v_ref[0, :, vs], do, _NT, preferred_element_type=f32)
                dst = pt * (dpt - dl_ref[0, t]) * ATT_SCALE
                dk_acc[t] += jnp.dot(dst.astype(bf16), q, preferred_element_type=f32)

        @pl.when(i > j)
        def _():
            step(False)

        @pl.when(i == j)
        def _():
            step(True)

        @pl.when(i == nq - 1)
        def _():
            for t in range(HP):
                dk_ref[0, :, t * 256:(t + 1) * 256] = dk_acc[t]
                dv_ref[0, :, t * VD:(t + 1) * VD] = dv_acc[t]

    kspec = pl.BlockSpec((1, bq, HP * 256), lambda b, h, j, i: (b, j, h))
    vspec = pl.BlockSpec((1, bq, HP * VD), lambda b, h, j, i: (b, j, h))
    rspec = pl.BlockSpec((1, HP, 1, bq), lambda b, h, j, i: (b, h, 0, jnp.maximum(i, j)))
    grid = (B, MLA_H // HP, nq, nq)
    body, c_args, c_in, c_out, c_shapes, c_sems = _fuse_exchange(body, comm, 6, 2, 2, grid)
    res = pl.pallas_call(body, name="attn_bwd_dkv_x" if comm else "attn_bwd_dkv", grid=grid,
                         in_specs=[pl.BlockSpec((1, bq, HP * 256), lambda b, h, j, i: (b, jnp.maximum(i, j), h)), kspec, vspec,
                                   pl.BlockSpec((1, bq, HP * VD), lambda b, h, j, i: (b, jnp.maximum(i, j), h)), rspec, rspec] + c_in,
                         out_specs=[kspec, vspec] + c_out,
                         out_shape=[jax.ShapeDtypeStruct((B, S, MLA_H * 256), f32), jax.ShapeDtypeStruct((B, S, MLA_H * VD), f32)] + c_shapes,
                         scratch_shapes=[pltpu.VMEM((HP, bq, 256), f32), pltpu.VMEM((HP, bq, VD), f32)] + c_sems,
                         compiler_params=_cparams(4))(q3, k3, v3, do3, lse_row, delta_row, *c_args)
    return res[0], res[1], list(res[2:])


def ada_fwd(c_all, w, b):
    n = w.shape[1]

    def body(c_ref, w_ref, b_ref, o_ref):
        o_ref[...] = jnp.dot(_silu(c_ref[...]).astype(bf16), w_ref[...].astype(bf16), preferred_element_type=f32) + b_ref[...]

    return pl.pallas_call(body, name="ada_fwd", out_shape=jax.ShapeDtypeStruct((c_all.shape[0], n), f32),
                          compiler_params=pltpu.CompilerParams(vmem_limit_bytes=VMEM_LIMIT))(c_all, w, b)


def ada_bwd(c_all, dmod):
    n = dmod.shape[1]

    def body(c_ref, d_ref, o_ref):
        o_ref[...] = lax.dot_general(_silu(c_ref[...]).astype(bf16), d_ref[...].astype(bf16), _TN, preferred_element_type=f32)

    return pl.pallas_call(body, name="ada_bwd", out_shape=jax.ShapeDtypeStruct((c_all.shape[1], n), f32),
                          compiler_params=pltpu.CompilerParams(vmem_limit_bytes=VMEM_LIMIT))(c_all, dmod)


def sum_leading(x, name):
    n, R, _ = x.shape
    tr = _pick(R, (512, 256, 128, 64, 32, 16, 8))

    def body(x_ref, o_ref):
        acc = x_ref[0].astype(f32)
        for k in range(1, n):
            acc = acc + x_ref[k].astype(f32)
        o_ref[...] = acc

    return pl.pallas_call(body, name=name, grid=(R // tr,), in_specs=[pl.BlockSpec((n, tr, 128), lambda i: (0, i, 0))],
                          out_specs=pl.BlockSpec((tr, 128), lambda i: (i, 0)), out_shape=jax.ShapeDtypeStruct((R, 128), f32),
                          compiler_params=_cparams(1))(x)


def _adamw_body(w_ref, g_ref, m_ref, v_ref, d_ref, mo_ref, vo_ref):
    gv = g_ref[...]
    mn = ADAM_B1 * m_ref[...] + (1.0 - ADAM_B1) * gv
    vn = ADAM_B2 * v_ref[...] + (1.0 - ADAM_B2) * jnp.square(gv)
    m_hat = mn / (1.0 - ADAM_B1 ** ADAM_STEP)
    v_hat = vn / (1.0 - ADAM_B2 ** ADAM_STEP)
    d_ref[...] = -ADAM_LR * (m_hat / (jnp.sqrt(v_hat) + ADAM_EPS) + ADAM_WD * w_ref[...])
    mo_ref[...] = mn
    vo_ref[...] = vn


def adamw(w, g, m, v):
    R = w.shape[0]
    tr = _pick(R, (512, 256, 128, 64, 32, 16, 8))
    spec = pl.BlockSpec((tr, 128), lambda i: (i, 0))
    shp = jax.ShapeDtypeStruct((R, 128), f32)
    return pl.pallas_call(functools.partial(_adamw_body), name="adamw", grid=(R // tr,), in_specs=[spec] * 4,
                          out_specs=[spec] * 3, out_shape=[shp] * 3, compiler_params=_cparams(1))(w, g, m, v)


def _row_tile(a):
    return _pick(a, (256, 128, 64, 32, 16, 8)) if a % 8 == 0 else a


def adamw_nd(w, g, m, v):
    L, a, b = w.shape
    ta = _row_tile(a)
    spec = pl.BlockSpec((1, ta, b), lambda l, i: (l, i, 0))
    shp = jax.ShapeDtypeStruct((L, a, b), f32)
    return pl.pallas_call(functools.partial(_adamw_body), name="adamw_nd", grid=(L, a // ta), in_specs=[spec] * 4,
                          out_specs=[spec] * 3, out_shape=[shp] * 3, compiler_params=_cparams(2))(w, g, m, v)


def sum_slots(x):
    n, L, a, b = x.shape
    ta = _row_tile(a)

    def body(x_ref, o_ref):
        acc = x_ref[0].astype(f32)
        for k in range(1, n):
            acc = acc + x_ref[k].astype(f32)
        o_ref[...] = acc

    return pl.pallas_call(body, name="sum_slots", grid=(L, a // ta),
                          in_specs=[pl.BlockSpec((n, 1, ta, b), lambda l, i: (0, l, i, 0))],
                          out_specs=pl.BlockSpec((1, ta, b), lambda l, i: (l, i, 0)),
                          out_shape=jax.ShapeDtypeStruct((L, a, b), f32), compiler_params=_cparams(2))(x)


def _exchange_copies(ins, outs, sems, scatter):
    send_sems, recv_sems, local_sems = sems
    x, y, c = lax.axis_index("x"), lax.axis_index("y"), lax.axis_index("c")
    me = 4 * x + 2 * y + c
    locals_, sends, recvs = [], [], []
    for a in range(len(ins)):
        locals_.append(pltpu.make_async_copy(ins[a].at[me] if scatter[a] else ins[a], outs[a].at[me], local_sems.at[a]))
        for k in range(N_DEV - 1):
            px = 1 - x if (k + 1) & 4 else x
            py = 1 - y if (k + 1) & 2 else y
            pc = 1 - c if (k + 1) & 1 else c
            pid = 4 * px + 2 * py + pc
            src = ins[a].at[pid] if scatter[a] else ins[a]
            for slot, group in ((me, sends), (pid, recvs)):
                group.append(pltpu.make_async_remote_copy(src_ref=src, dst_ref=outs[a].at[slot], send_sem=send_sems.at[a, k],
                                                          recv_sem=recv_sems.at[a, k], device_id=(px, py, pc),
                                                          device_id_type=pl.DeviceIdType.MESH))
    return locals_, sends, recvs


def _exchange_start(ins, outs, sems, scatter):
    locals_, sends, _ = _exchange_copies(ins, outs, sems, scatter)
    for cp in locals_ + sends:
        cp.start()


def _exchange_wait(ins, outs, sems, scatter):
    locals_, sends, recvs = _exchange_copies(ins, outs, sems, scatter)
    for cp in recvs:
        cp.wait_recv()
    for cp in sends:
        cp.wait_send()
    for cp in locals_:
        cp.wait()


def _exchange_shapes(arrays, scatter):
    return [jax.ShapeDtypeStruct((N_DEV,) + tuple(a.shape[1:] if s else a.shape), a.dtype) for a, s in zip(arrays, scatter)]


def _flags(scatter, n):
    return [scatter] * n if isinstance(scatter, bool) else list(scatter)


def _exchange_sems(n):
    return [pltpu.SemaphoreType.DMA((n, N_DEV - 1)), pltpu.SemaphoreType.DMA((n, N_DEV - 1)), pltpu.SemaphoreType.DMA((n,))]


def _fuse_exchange(core, comm, n_in, n_out, n_scr, grid):
    if comm is None:
        return core, [], [], [], [], []
    arrays, scatter = comm
    n = len(arrays)
    scatter = _flags(scatter, n)

    def body(*refs):
        a, b, c = n_in + n, n_in + n + n_out, n_in + 2 * n + n_out
        cin, cout, sems = refs[n_in:a], refs[b:c], refs[c + n_scr:]
        ids = [pl.program_id(d) for d in range(len(grid))]
        first = functools.reduce(lambda p, q: p & q, [i == 0 for i in ids])
        last = functools.reduce(lambda p, q: p & q, [i == g - 1 for i, g in zip(ids, grid)])

        @pl.when(first)
        def _():
            _exchange_start(cin, cout, sems, scatter)

        core(*refs[:n_in], *refs[a:b], *refs[c:c + n_scr])

        @pl.when(last)
        def _():
            _exchange_wait(cin, cout, sems, scatter)

    hbm = pl.BlockSpec(memory_space=pltpu.HBM)
    return body, list(arrays), [hbm] * n, [hbm] * n, _exchange_shapes(arrays, scatter), _exchange_sems(n)


def exchange(arrays, scatter, name):
    n = len(arrays)
    scatter = _flags(scatter, n)

    def body(*refs):
        ins, outs, sems = refs[:n], refs[n:2 * n], refs[2 * n:]
        _exchange_start(ins, outs, sems, scatter)
        _exchange_wait(ins, outs, sems, scatter)

    hbm = pl.BlockSpec(memory_space=pltpu.HBM)
    return pl.pallas_call(body, name=name, in_specs=[hbm] * n, out_specs=[hbm] * n,
                          out_shape=_exchange_shapes(arrays, scatter), scratch_shapes=_exchange_sems(n))(*arrays)


BIG = (("w_in", "col"), ("conv_w", "col"), ("w_uq", "col"), ("w_ukv", "col"), ("w_out", "row"), ("w_up", "col"),
       ("conv_ff_w", "col"), ("w_down", "row"))
SMALL = ("b_ada", "norm_mix", "conv_b", "dt_bias", "a_log", "d_skip", "ssd_norm", "q_norm", "kv_norm", "attn_norm",
         "norm_mlp", "conv_ff_b", "final_norm")
CONVS = ("conv_w", "conv_ff_w")
PACK_ALIGN = 2048


def _padded(n):
    return -(-n // PACK_ALIGN) * PACK_ALIGN


def _flat_pad(a):
    f = a.reshape(-1)
    return jnp.pad(f, (0, _padded(f.shape[0]) - f.shape[0]))


PACK_ROWS = 512


def pack(arrs):
    f = jnp.concatenate([_flat_pad(a) for a in arrs])
    n = PACK_ROWS * 128
    return jnp.pad(f, (0, -(-f.shape[0] // n) * n - f.shape[0])).reshape(-1, 128)


def unpack(flat, shapes):
    f = flat.reshape(-1)
    out, off = [], 0
    for s in shapes:
        n = int(np.prod(s))
        out.append(f[off:off + n].reshape(s))
        off += _padded(n)
    return out


def shards_to_full(g, kind):
    _, a, b = g.shape
    if kind == "col":
        return g.transpose(1, 0, 2).reshape(a, N_DEV * b)
    return g.reshape(N_DEV * a, b)


def full_to_shards(full, kind):
    if kind == "col":
        a, nb = full.shape
        return full.reshape(a, N_DEV, nb // N_DEV).transpose(1, 0, 2)
    na, b = full.shape
    return full.reshape(N_DEV, na // N_DEV, b)


def w_in_layout(w):
    z = lambda n: jnp.zeros(w.shape[:-1] + (n,), w.dtype)
    return jnp.concatenate([w[..., :2560], w[..., 2576:2960], z(128), w[..., 2960:3216], w[..., 3216:3280],
                            w[..., 2560:2576], z(48)], axis=-1)


def w_in_unlayout(g):
    return jnp.concatenate([g[..., :2560], g[..., 3392:3408], g[..., 2560:2944], g[..., 3072:3328], g[..., 3328:3392]], axis=-1)


def w_uq_layout(w):
    return jnp.pad(w.reshape(Q_RANK, MLA_H, QK), ((0, 0), (0, 0), (0, 256 - QK))).reshape(Q_RANK, MLA_H * 256)


def w_uq_unlayout(g):
    return g.reshape(Q_RANK, MLA_H, 256)[:, :, :QK].reshape(Q_RANK, MLA_H * QK)


def w_ukv_layout(w):
    return w.reshape(KV_RANK, MLA_H, 2, 128).transpose(0, 2, 1, 3).reshape(KV_RANK, 2 * MLA_H * 128)


def w_ukv_unlayout(g):
    return g.reshape(KV_RANK, 2, MLA_H, 128).transpose(0, 2, 1, 3).reshape(KV_RANK, 2 * MLA_H * 128)


LAYOUTS = {"w_in": (w_in_layout, w_in_unlayout), "w_uq": (w_uq_layout, w_uq_unlayout), "w_ukv": (w_ukv_layout, w_ukv_unlayout)}
FIRST, REST = BIG[:4], BIG[4:]


def layer_weights(gathered, entries):
    full = {n: shards_to_full(g, kind) for (n, kind), g in zip(entries, gathered)}
    return {n: LAYOUTS[n][0](w) if n in LAYOUTS else w for n, w in full.items()}


def layer_grad_slices(g, entries):
    return [full_to_shards(LAYOUTS[n][1](g[n]) if n in LAYOUTS else g[n], kind).astype(bf16) for n, kind in entries]


def _head_row(v):
    return jnp.zeros((1, 128), f32).at[0, DT_LANE:DT_LANE + SSD_HEADS].set(v)


def layer_fwd(x3, mod, W, P, l, cosf, sinf, comm=None, late=None):
    B, S, _ = x3.shape
    T = B * S
    sv = {}
    h = normmod_fwd(x3, mod, P["norm_mix"][l][None], 0, 1)
    p = mm(h.reshape(T, D), W["w_in"], "nn", "mm_in")
    p3 = p.reshape(B, S, IN_COLS)
    bias_row, alog_row = _head_row(P["dt_bias"][l]), _head_row(P["a_log"][l])
    dcol = jnp.repeat(P["d_skip"][l], SSD_HD)[:, None]
    xc3 = conv_ssd_fwd(p3, W["conv_w"], P["conv_b"][l][None])
    yc3, states = ssd_fwd(xc3, p3, bias_row, alog_row, dcol)
    y_ssd = ssd_out_fwd(yc3, p3, P["ssd_norm"][l][None])
    cqn = rms_fwd(p3, 512, OFF_CQ // 512, Q_RANK, P["q_norm"][l][None], "rms_q_fwd")
    ckvn = rms_fwd(p3, KV_RANK, OFF_CKV // KV_RANK, KV_RANK, P["kv_norm"][l][None], "rms_kv_fwd")
    qraw = mm(cqn.reshape(T, Q_RANK), W["w_uq"], "nn", "mm_uq")
    kvraw = mm(ckvn.reshape(T, KV_RANK), W["w_ukv"], "nn", "mm_ukv")
    q3 = qprep_fwd(qraw.reshape(B, S, -1), cosf, sinf)
    k3, v3 = kprep_fwd(kvraw.reshape(B, S, -1), p3, cosf, sinf)
    o3, lse, comm_out = attn_fwd(q3, k3, v3, comm)
    if late is not None:
        W = dict(W, **late(comm_out))
    y_att = rms_fwd(o3, D, 0, D, P["attn_norm"][l][None], "rms_o_fwd")
    cat = jnp.concatenate([y_ssd, y_att], axis=-1).reshape(T, 2 * D)
    x1, y1 = mm(cat, W["w_out"], "nn", "mm_out", resid=x3.reshape(T, D), gate=mod[:, 2:3, :], seq=S)
    x13 = x1.reshape(B, S, D)
    h2 = normmod_fwd(x13, mod, P["norm_mlp"][l][None], 3, 4)
    u = mm(h2.reshape(T, D), W["w_up"], "nn", "mm_up", out_dtype=bf16)
    u3 = u.reshape(B, S, 2 * D_FF)
    a = glu_fwd(u3, W["conv_ff_w"], P["conv_ff_b"][l][None])
    x2, y2 = mm(a.reshape(T, D_FF), W["w_down"], "nn", "mm_down", resid=x1, gate=mod[:, 5:6, :], seq=S)
    sv.update(x=x3, h=h, p3=p3, xc3=xc3, yc3=yc3, states=states, cqn=cqn, ckvn=ckvn, q3=q3, k3=k3, v3=v3, o3=o3, lse=lse,
              cat=cat, y1=y1, x1=x13, h2=h2, u3=u3, a=a, y2=y2, bias_row=bias_row, alog_row=alog_row, dcol=dcol)
    return x2.reshape(B, S, D), sv, comm_out, W


def layer_bwd(dx3, sv, mod, W, P, l, cosf, sinf, comm=None, send_rest=False):
    B, S, _ = dx3.shape
    T = B * S
    g = {}
    dy2, dg2 = gate_bwd(dx3, sv["y2"].reshape(B, S, D), mod, 5)
    dy2 = dy2.reshape(T, D)
    da = mm(dy2, W["w_down"], "nt", "mm_down_dx")
    g["w_down"] = mm(sv["a"].reshape(T, D_FF), dy2, "tn", "mm_down_dw")
    dug, duv, dwg, dwv, dbg, dbv = glu_bwd(sv["u3"], da.reshape(B, S, D_FF), W["conv_ff_w"], P["conv_ff_b"][l][None])
    g["conv_ff_w"] = jnp.concatenate([dwg, dwv], axis=1)
    g["conv_ff_b"] = jnp.concatenate([dbg, dbv], axis=1)[0]
    du = jnp.concatenate([dug, duv], axis=-1).reshape(T, 2 * D_FF)
    dh2 = mm(du, W["w_up"], "nt", "mm_up_dx")
    g["w_up"] = mm(sv["h2"].reshape(T, D), du, "tn", "mm_up_dw")
    dx1, dsh2, dsc2, dnm = normmod_bwd(sv["x1"], dh2.reshape(B, S, D), dx3, mod, P["norm_mlp"][l][None], 4)
    g["norm_mlp"] = dnm[0]
    dy1, dg1 = gate_bwd(dx1, sv["y1"].reshape(B, S, D), mod, 2)
    dy1 = dy1.reshape(T, D)
    dcat = mm(dy1, W["w_out"], "nt", "mm_out_dx")
    g["w_out"] = mm(sv["cat"], dy1, "tn", "mm_out_dw")
    dcat3 = dcat.reshape(B, S, 2 * D)
    do3, dan = rms_bwd(sv["o3"], D, 0, D, dcat3, 1, P["attn_norm"][l][None], f32, "rms_o_bwd")
    g["attn_norm"] = dan[0]
    dq3, delta, comm_out = attn_bwd_dq(sv["q3"], sv["k3"], sv["v3"], sv["o3"], do3, sv["lse"], comm)
    comm_rest = (layer_grad_slices(g, REST), True) if send_rest else None
    dk3, dv3, rest_out = attn_bwd_dkv(sv["q3"], sv["k3"], sv["v3"], do3, sv["lse"].reshape(B, MLA_H, 1, S),
                                      delta.reshape(B, MLA_H, 1, S), comm_rest)
    dqraw = qprep_bwd(dq3, cosf, sinf).reshape(T, -1)
    dcqn = mm(dqraw, W["w_uq"], "nt", "mm_uq_dx")
    g["w_uq"] = mm(sv["cqn"].reshape(T, Q_RANK), dqraw, "tn", "mm_uq_dw")
    dcq, dqn = rms_bwd(sv["p3"], 512, OFF_CQ // 512, Q_RANK, dcqn.reshape(B, S, Q_RANK), 0, P["q_norm"][l][None], bf16, "rms_q_bwd")
    g["q_norm"] = dqn[0]
    dyc3, dz, dsn = ssd_out_bwd(sv["yc3"], sv["p3"], dcat3, P["ssd_norm"][l][None])
    g["ssd_norm"] = dsn[0]
    bias_col, alog_col = sv["bias_row"].reshape(128, 1), sv["alog_row"].reshape(128, 1)
    dxc3, ddt3, dalog, dd, dbias = ssd_bwd(sv["xc3"], sv["p3"], dyc3, sv["states"], sv["bias_row"], sv["alog_row"],
                                           bias_col, alog_col, sv["dcol"])
    heads = slice(DT_LANE, DT_LANE + SSD_HEADS)
    g["a_log"], g["d_skip"], g["dt_bias"] = dalog[heads, 0], dd[heads, 0], dbias[heads, 0]
    dxbc, dcw, dcb = conv_ssd_bwd(sv["p3"], dxc3, W["conv_w"], P["conv_b"][l][None])
    g["conv_w"], g["conv_b"] = dcw, dcb[0]
    dkvraw, dkrdt = kprep_bwd(dk3, dv3, ddt3, cosf, sinf)
    dkvraw = dkvraw.reshape(T, -1)
    dckvn = mm(dkvraw, W["w_ukv"], "nt", "mm_ukv_dx")
    g["w_ukv"] = mm(sv["ckvn"].reshape(T, KV_RANK), dkvraw, "tn", "mm_ukv_dw")
    dckv, dkn = rms_bwd(sv["p3"], KV_RANK, OFF_CKV // KV_RANK, KV_RANK, dckvn.reshape(B, S, KV_RANK), 0, P["kv_norm"][l][None],
                        bf16, "rms_kv_bwd")
    g["kv_norm"] = dkn[0]
    dp = jnp.concatenate([dz, dxbc, dcq, jnp.zeros((B, S, 128), bf16), dckv, dkrdt], axis=-1).reshape(T, IN_COLS)
    dh = mm(dp, W["w_in"], "nt", "mm_in_dx")
    g["w_in"] = mm(sv["h"].reshape(T, D), dp, "tn", "mm_in_dw")
    dx0, dsh1, dsc1, dnx = normmod_bwd(sv["x"], dh.reshape(B, S, D), dx1, mod, P["norm_mix"][l][None], 1)
    g["norm_mix"] = dnx[0]
    dmod = jnp.concatenate([dsh1, dsc1, dg1, dsh2, dsc2, dg2], axis=1)
    return dx0, dmod, g, comm_out, rest_out


def kernel(x, c, positions, w_ada, b_ada, norm_mix, w_in, conv_w, conv_b, dt_bias, a_log, d_skip, ssd_norm, q_norm, w_uq, kv_norm, w_ukv, attn_norm, w_out, norm_mlp, w_up, conv_ff_w, conv_ff_b, w_down, final_norm, loss_target, m_w_ada, m_b_ada, m_norm_mix, m_w_in, m_conv_w, m_conv_b, m_dt_bias, m_a_log, m_d_skip, m_ssd_norm, m_q_norm, m_w_uq, m_kv_norm, m_w_ukv, m_attn_norm, m_w_out, m_norm_mlp, m_w_up, m_conv_ff_w, m_conv_ff_b, m_w_down, m_final_norm, v_w_ada, v_b_ada, v_norm_mix, v_w_in, v_conv_w, v_conv_b, v_dt_bias, v_a_log, v_d_skip, v_ssd_norm, v_q_norm, v_w_uq, v_kv_norm, v_w_ukv, v_attn_norm, v_w_out, v_norm_mlp, v_w_up, v_conv_ff_w, v_conv_ff_b, v_w_down, v_final_norm):
    given = dict(locals())
    B, S, _ = x.shape
    me = 4 * lax.axis_index("x") + 2 * lax.axis_index("y") + lax.axis_index("c")
    P = {n: given[n] for n in SMALL}

    def shards(l, entries):
        return [given[n][l] if n in CONVS else given[n][l].astype(bf16) for n, _ in entries]

    *gathered, c_all = exchange(shards(0, FIRST) + [c], False, "gather_weights")
    W = [layer_weights(gathered, FIRST), None]

    n_ada = w_ada.shape[2]
    c_all = c_all.reshape(N_DEV * B, D)
    b_sh = lax.dynamic_slice_in_dim(b_ada, me * n_ada, n_ada, axis=1)
    mod_sh = jnp.stack([ada_fwd(c_all, w_ada[l], b_sh[l][None]) for l in range(DEPTH)])
    (mod_g,) = exchange([mod_sh], False, "gather_mod")
    mod_mine = lax.dynamic_slice_in_dim(mod_g, me * B, B, axis=2)
    mods = mod_mine.transpose(1, 2, 0, 3).reshape(DEPTH, B, 6, D)

    inv_freq = jnp.asarray(1.0 / (ROPE_BASE ** (np.arange(0, ROPE, 2, dtype=np.float32) / ROPE)))
    ang = positions.astype(f32)[..., None] * inv_freq
    zeros = jnp.zeros((B, S, 128 - ROPE), f32)
    cosf = jnp.concatenate([jnp.cos(ang), jnp.cos(ang), zeros], axis=-1)
    sinf = jnp.concatenate([jnp.sin(ang), jnp.sin(ang), zeros], axis=-1)

    saved = [None] * DEPTH
    nr = len(REST)
    xl, saved[0], gathered, W[0] = layer_fwd(x, mods[0], W[0], P, 0, cosf, sinf, comm=(shards(0, REST) + shards(1, BIG), False),
                                             late=lambda got: layer_weights(got[:nr], REST))
    W[1] = layer_weights(gathered[nr:], BIG)
    xl, saved[1], _, _ = layer_fwd(xl, mods[1], W[1], P, 1, cosf, sinf)
    dxl, d_final, loss_part = final_loss(xl, final_norm[None], loss_target)
    grads, dmods, recv = [None] * DEPTH, [None] * DEPTH, [None] * DEPTH
    dxl, dmods[1], grads[1], _, _ = layer_bwd(dxl, saved[1], mods[1], W[1], P, 1, cosf, sinf)
    grad_x, dmods[0], grads[0], recv[1], recv_rest = layer_bwd(dxl, saved[0], mods[0], W[0], P, 0, cosf, sinf,
                                                               comm=(layer_grad_slices(grads[1], BIG), True), send_rest=True)

    stack = lambda n: jnp.stack([grads[l][n] for l in range(DEPTH)])
    small_names = [n for n in SMALL if n not in ("b_ada", "final_norm")]
    partial = pack([stack(n) for n in small_names] + [d_final[0], loss_part[0]])
    dmod_all = jnp.stack(dmods)
    *recv_first, part_g, dmod_g = exchange(layer_grad_slices(grads[0], FIRST) + [partial, dmod_all],
                                           [True] * len(FIRST) + [False, False], "exchange_tail")
    recv[0] = recv_first + recv_rest
    big_g = [jnp.concatenate([sum_slots(recv[l][i][:, None]) for l in range(DEPTH)]) for i in range(len(BIG))]
    small_sum = sum_leading(part_g, "sum_partials")
    small_g = unpack(small_sum, [given[n].shape for n in small_names] + [(D,), (128,)])
    gsmall = dict(zip(small_names + ["final_norm"], small_g[:-1]))
    loss = small_g[-1][0]
    dmod_rows = dmod_g.transpose(0, 2, 1, 3, 4).reshape(N_DEV * B, DEPTH * 6 * D)
    gsmall["b_ada"] = sum_leading(dmod_rows.reshape(N_DEV * B, -1, 128), "sum_b_ada").reshape(DEPTH, 6 * D)
    dmod_cols = dmod_rows.reshape(N_DEV * B, DEPTH, N_DEV, n_ada)
    dmod_sh = lax.dynamic_slice_in_dim(dmod_cols, me, 1, axis=2)[:, :, 0, :]
    g_w_ada = jnp.stack([ada_bwd(c_all, dmod_sh[:, l, :]) for l in range(DEPTH)])

    res = {"grad": {}, "delta": {}, "new_m": {}, "new_v": {}}
    for n, gv in zip([n for n, _ in BIG] + ["w_ada"], big_g + [g_w_ada]):
        res["grad"][n] = gv
        res["delta"][n], res["new_m"][n], res["new_v"][n] = adamw_nd(given[n], gv, given["m_" + n], given["v_" + n])
    shapes = [given[n].shape for n in SMALL]
    flat = adamw(pack([given[n] for n in SMALL]), pack([gsmall[n] for n in SMALL]), pack([given["m_" + n] for n in SMALL]),
                 pack([given["v_" + n] for n in SMALL]))
    for n in SMALL:
        res["grad"][n] = gsmall[n]
    for key, arr in zip(("delta", "new_m", "new_v"), flat):
        res[key].update(zip(SMALL, unpack(arr, shapes)))
    order = ["w_ada", "b_ada", "norm_mix", "w_in", "conv_w", "conv_b", "dt_bias", "a_log", "d_skip", "ssd_norm", "q_norm", "w_uq",
             "kv_norm", "w_ukv", "attn_norm", "w_out", "norm_mlp", "w_up", "conv_ff_w", "conv_ff_b", "w_down", "final_norm"]
    return (loss, grad_x, *[res[k][n] for k in ("grad", "delta", "new_m", "new_v") for n in order])
```

```python
import functools

import numpy as np
import jax
import jax.numpy as jnp
from jax import lax
from jax.experimental import pallas as pl
from jax.experimental.pallas import tpu as pltpu

f32, bf16 = jnp.float32, jnp.bfloat16
HIGHEST = lax.Precision.HIGHEST

D = 1024
D_SSD = 1024
SSD_HEADS = 16
SSD_HD = 64
SSD_N = 128
CHUNK = 128
D_XBC = 1536
CONV_K = 4
MLA_H = 8
NOPE = 128
ROPE = 64
VD = 128
QK = NOPE + ROPE
Q_RANK = 384
KV_RANK = 256
D_FF = 2816
FF_K = 3
EPS = 1e-6
ROPE_BASE = 10000.0
DEPTH = 2
ADAM_LR, ADAM_B1, ADAM_B2, ADAM_EPS, ADAM_WD, ADAM_STEP = 0.001, 0.9, 0.999, 1e-08, 0.01, 10

N_DEV = 8
IN_COLS = 3456
OFF_XBC, OFF_CQ, OFF_CKV, OFF_KRDT = 1024, 2560, 3072, 3328
DT_LANE = 64
VMEM_LIMIT = 48 * 1024 * 1024
MM_K_WHOLE = 4096


def _cparams(n_grid):
    return pltpu.CompilerParams(dimension_semantics=("arbitrary",) * n_grid, vmem_limit_bytes=VMEM_LIMIT)


def _pick(n, cands):
    for c in cands:
        if n % c == 0:
            return c
    return n


def _silu(x):
    return x * jax.nn.sigmoid(x)


def _dsilu(x):
    s = jax.nn.sigmoid(x)
    return s * (1.0 + x * (1.0 - s))


def _rowsum(x):
    return jnp.sum(x, axis=0, keepdims=True)


def mm(a, b, mode, name, out_dtype=f32, resid=None, gate=None, seq=None):
    if mode == "nn":
        (M, K), N = a.shape, b.shape[1]
    elif mode == "nt":
        (M, K), N = a.shape, b.shape[0]
    else:
        (K, M), N = a.shape, b.shape[1]
    gated = resid is not None
    tm = _pick(seq if gated else M, (1024, 1408, 512, 384, 256, 128))
    tn = _pick(N, (512, 384, 256, 128))
    tk = K if K <= MM_K_WHOLE else _pick(K, (2816, 2048, 1024, 512))
    nk = K // tk
    dims = {"nn": ((1,), (0,)), "nt": ((1,), (1,)), "tn": ((0,), (0,))}[mode]

    def body(a_ref, b_ref, *rest):
        if gated:
            r_ref, g_ref, o_ref, y_ref, acc = rest
        else:
            o_ref, acc = rest

        def finish(res):
            if gated:
                y_ref[...] = res
                o_ref[...] = r_ref[...] + g_ref[0] * res
            else:
                o_ref[...] = res.astype(out_dtype)

        prod = lax.dot_general(a_ref[...].astype(bf16), b_ref[...].astype(bf16), (dims, ((), ())), preferred_element_type=f32)
        if nk == 1:
            finish(prod)
        else:
            k = pl.program_id(2)

            @pl.when(k == 0)
            def _():
                acc[...] = prod

            @pl.when(k > 0)
            def _():
                acc[...] += prod

            @pl.when(k == nk - 1)
            def _():
                finish(acc[...])

    a_spec = pl.BlockSpec((tk, tm), lambda i, j, k: (k, i)) if mode == "tn" else pl.BlockSpec((tm, tk), lambda i, j, k: (i, k))
    b_spec = pl.BlockSpec((tn, tk), lambda i, j, k: (j, k)) if mode == "nt" else pl.BlockSpec((tk, tn), lambda i, j, k: (k, j))
    o_spec = pl.BlockSpec((tm, tn), lambda i, j, k: (i, j))
    in_specs, args = [a_spec, b_spec], [a, b]
    out_specs, out_shape = o_spec, jax.ShapeDtypeStruct((M, N), out_dtype)
    if gated:
        per = seq // tm
        in_specs += [o_spec, pl.BlockSpec((1, 1, tn), lambda i, j, k: (i // per, 0, j))]
        args += [resid, gate]
        out_specs = [o_spec, o_spec]
        out_shape = [jax.ShapeDtypeStruct((M, N), f32), jax.ShapeDtypeStruct((M, N), f32)]
    return pl.pallas_call(body, name=name, grid=(M // tm, N // tn, nk), in_specs=in_specs, out_specs=out_specs,
                          out_shape=out_shape, scratch_shapes=[pltpu.VMEM((tm, tn), f32)],
                          compiler_params=_cparams(3))(*args)


def _tok(ts, width, cb=0):
    return pl.BlockSpec((1, ts, width), lambda b, s: (b, s, cb))


def _perb(rows, width):
    return pl.BlockSpec((1, rows, width), lambda b, s: (b, 0, 0))


def _const(rows, width):
    return pl.BlockSpec((rows, width), lambda b, s: (0, 0))


def _row_call(body, name, B, S, ts, in_specs, out_specs, out_shape, scratch=()):
    return pl.pallas_call(body, name=name, grid=(B, S // ts), in_specs=in_specs, out_specs=out_specs,
                          out_shape=out_shape, scratch_shapes=list(scratch), compiler_params=_cparams(2))


def _first():
    return (pl.program_id(0) == 0) & (pl.program_id(1) == 0)


def normmod_fwd(x3, mod, g, i_sh, i_sc):
    B, S, C = x3.shape
    ts = _pick(S, (512, 256, 128))

    def body(x_ref, mod_ref, g_ref, h_ref):
        x = x_ref[0]
        r = lax.rsqrt(jnp.mean(x * x, axis=-1, keepdims=True) + EPS)
        n = x * r * g_ref[...]
        h_ref[0] = (n * (1.0 + mod_ref[0, i_sc:i_sc + 1, :]) + mod_ref[0, i_sh:i_sh + 1, :]).astype(bf16)

    return _row_call(body, "normmod_fwd", B, S, ts, [_tok(ts, C), _perb(6, C), _const(1, C)], _tok(ts, C),
                     jax.ShapeDtypeStruct((B, S, C), bf16))(x3, mod, g)


def normmod_bwd(x3, dh3, resid3, mod, g, i_sc):
    B, S, C = x3.shape
    ts = _pick(S, (512, 256, 128))

    def body(x_ref, dh_ref, r_ref, mod_ref, g_ref, dx_ref, dsh_ref, dsc_ref, dg_ref):
        @pl.when(pl.program_id(1) == 0)
        def _():
            dsh_ref[...] = jnp.zeros_like(dsh_ref)
            dsc_ref[...] = jnp.zeros_like(dsc_ref)

        @pl.when(_first())
        def _():
            dg_ref[...] = jnp.zeros_like(dg_ref)

        x, dh, gv = x_ref[0], dh_ref[0], g_ref[...]
        r = lax.rsqrt(jnp.mean(x * x, axis=-1, keepdims=True) + EPS)
        xh = x * r
        dn = dh * (1.0 + mod_ref[0, i_sc:i_sc + 1, :])
        dsh_ref[0] += _rowsum(dh)
        dsc_ref[0] += _rowsum(dh * xh * gv)
        dg_ref[...] += _rowsum(dn * xh)
        dxh = dn * gv
        dx_ref[0] = r * (dxh - xh * jnp.mean(dxh * xh, axis=-1, keepdims=True)) + r_ref[0]

    return _row_call(body, "normmod_bwd", B, S, ts,
                     [_tok(ts, C), _tok(ts, C), _tok(ts, C), _perb(6, C), _const(1, C)],
                     [_tok(ts, C), _perb(1, C), _perb(1, C), _const(1, C)],
                     [jax.ShapeDtypeStruct((B, S, C), f32), jax.ShapeDtypeStruct((B, 1, C), f32),
                      jax.ShapeDtypeStruct((B, 1, C), f32), jax.ShapeDtypeStruct((1, C), f32)])(x3, dh3, resid3, mod, g)


def gate_bwd(dx3, y3, mod, i_g):
    B, S, C = dx3.shape
    ts = _pick(S, (512, 256, 128))

    def body(dx_ref, y_ref, mod_ref, dy_ref, dgate_ref):
        @pl.when(pl.program_id(1) == 0)
        def _():
            dgate_ref[...] = jnp.zeros_like(dgate_ref)

        dx = dx_ref[0]
        dy_ref[0] = (dx * mod_ref[0, i_g:i_g + 1, :]).astype(bf16)
        dgate_ref[0] += _rowsum(dx * y_ref[0])

    return _row_call(body, "gate_bwd", B, S, ts, [_tok(ts, C), _tok(ts, C), _perb(6, C)], [_tok(ts, C), _perb(1, C)],
                     [jax.ShapeDtypeStruct((B, S, C), bf16), jax.ShapeDtypeStruct((B, 1, C), f32)])(dx3, y3, mod)


def rms_fwd(src3, width, cb, n, g, name):
    B, S, _ = src3.shape
    ts = _pick(S, (512, 256, 128))

    def body(x_ref, g_ref, o_ref):
        x = x_ref[0][:, :n]
        r = lax.rsqrt(jnp.mean(x * x, axis=-1, keepdims=True) + EPS)
        o_ref[0] = (x * r * g_ref[...]).astype(bf16)

    return _row_call(body, name, B, S, ts, [_tok(ts, width, cb), _const(1, n)], _tok(ts, n),
                     jax.ShapeDtypeStruct((B, S, n), bf16))(src3, g)


def rms_bwd(src3, width, cb, n, dout3, dcb, g, out_dtype, name):
    B, S, _ = src3.shape
    ts = _pick(S, (512, 256, 128))

    def body(x_ref, do_ref, g_ref, dx_ref, dg_ref):
        @pl.when(_first())
        def _():
            dg_ref[...] = jnp.zeros_like(dg_ref)

        x = x_ref[0][:, :n]
        do = do_ref[0].astype(f32)
        r = lax.rsqrt(jnp.mean(x * x, axis=-1, keepdims=True) + EPS)
        xh = x * r
        dg_ref[...] += _rowsum(do * xh)
        dxh = do * g_ref[...]
        dx_ref[0] = (r * (dxh - xh * jnp.mean(dxh * xh, axis=-1, keepdims=True))).astype(out_dtype)

    return _row_call(body, name, B, S, ts, [_tok(ts, width, cb), _tok(ts, n, dcb), _const(1, n)],
                     [_tok(ts, n), _const(1, n)],
                     [jax.ShapeDtypeStruct((B, S, n), out_dtype), jax.ShapeDtypeStruct((1, n), f32)])(src3, dout3, g)


def final_loss(x3, g, tgt3):
    B, S, C = x3.shape
    ts = _pick(S, (512, 256, 128))

    def body(x_ref, g_ref, t_ref, dx_ref, dg_ref, loss_ref):
        @pl.when(_first())
        def _():
            dg_ref[...] = jnp.zeros_like(dg_ref)
            loss_ref[...] = jnp.zeros_like(loss_ref)

        x, gv = x_ref[0], g_ref[...]
        r = lax.rsqrt(jnp.mean(x * x, axis=-1, keepdims=True) + EPS)
        xh = x * r
        e = xh * gv - t_ref[0]
        loss_ref[...] += 0.5 * jnp.sum(e * e) / C
        dout = e / C
        dg_ref[...] += _rowsum(dout * xh)
        dxh = dout * gv
        dx_ref[0] = r * (dxh - xh * jnp.mean(dxh * xh, axis=-1, keepdims=True))

    return _row_call(body, "final_loss", B, S, ts, [_tok(ts, C), _const(1, C), _tok(ts, C)],
                     [_tok(ts, C), _const(1, C), _const(1, 128)],
                     [jax.ShapeDtypeStruct((B, S, C), f32), jax.ShapeDtypeStruct((1, C), f32),
                      jax.ShapeDtypeStruct((1, 128), f32)])(x3, g, tgt3)


def ssd_out_fwd(yc3, p3, w):
    B, S, C = yc3.shape
    ts = _pick(S, (512, 256, 128))
    half = C // 2

    def body(y_ref, z_ref, w_ref, o_ref):
        y = y_ref[0] * _silu(z_ref[0])
        for lo in (0, half):
            yg = y[:, lo:lo + half]
            r = lax.rsqrt(jnp.mean(yg * yg, axis=-1, keepdims=True) + EPS)
            o_ref[0, :, lo:lo + half] = (yg * r * w_ref[:, lo:lo + half]).astype(bf16)

    return _row_call(body, "ssd_out_fwd", B, S, ts, [_tok(ts, C), _tok(ts, C, 0), _const(1, C)], _tok(ts, C),
                     jax.ShapeDtypeStruct((B, S, C), bf16))(yc3, p3, w)


def ssd_out_bwd(yc3, p3, dcat3, w):
    B, S, C = yc3.shape
    ts = _pick(S, (512, 256, 128))
    half = C // 2

    def body(y_ref, z_ref, do_ref, w_ref, dyc_ref, dz_ref, dw_ref):
        @pl.when(_first())
        def _():
            dw_ref[...] = jnp.zeros_like(dw_ref)

        yc, z, do = y_ref[0], z_ref[0], do_ref[0]
        sz = _silu(z)
        y = yc * sz
        for lo in (0, half):
            sl = slice(lo, lo + half)
            yg, dog, wg = y[:, sl], do[:, sl], w_ref[:, sl]
            r = lax.rsqrt(jnp.mean(yg * yg, axis=-1, keepdims=True) + EPS)
            yh = yg * r
            dw_ref[:, sl] += _rowsum(dog * yh)
            dyh = dog * wg
            dy = r * (dyh - yh * jnp.mean(dyh * yh, axis=-1, keepdims=True))
            dyc_ref[0, :, sl] = dy * sz[:, sl]
            dz_ref[0, :, sl] = (dy * yc[:, sl] * _dsilu(z[:, sl])).astype(bf16)

    return _row_call(body, "ssd_out_bwd", B, S, ts, [_tok(ts, C), _tok(ts, C, 0), _tok(ts, C, 0), _const(1, C)],
                     [_tok(ts, C), _tok(ts, C), _const(1, C)],
                     [jax.ShapeDtypeStruct((B, S, C), f32), jax.ShapeDtypeStruct((B, S, C), bf16),
                      jax.ShapeDtypeStruct((1, C), f32)])(yc3, p3, dcat3, w)


def _rot(t):
    lane = lax.broadcasted_iota(jnp.int32, t.shape, 1)
    return jnp.where(lane < ROPE // 2, -pltpu.roll(t, 128 - ROPE // 2, 1), pltpu.roll(t, ROPE // 2, 1))


def _rope(t, cosf, sinf):
    return t * cosf + _rot(t) * sinf


def _rope_t(d, cosf, sinf):
    return d * cosf - _rot(d * sinf)


def qprep_fwd(qraw3, cosf, sinf):
    B, S, W = qraw3.shape
    ts = _pick(S, (512, 256, 128))

    def body(q_ref, c_ref, s_ref, o_ref):
        c, s = c_ref[0], s_ref[0]
        for h in range(MLA_H):
            o_ref[0, :, h * 256:h * 256 + 128] = (q_ref[0, :, h * 256:h * 256 + 128] * Q_FOLD).astype(bf16)
            o_ref[0, :, h * 256 + 128:(h + 1) * 256] = (_rope(q_ref[0, :, h * 256 + 128:(h + 1) * 256], c, s) * Q_FOLD).astype(bf16)

    return _row_call(body, "qprep_fwd", B, S, ts, [_tok(ts, W), _tok(ts, 128), _tok(ts, 128)], _tok(ts, W),
                     jax.ShapeDtypeStruct((B, S, W), bf16))(qraw3, cosf, sinf)


def qprep_bwd(dq3, cosf, sinf):
    B, S, W = dq3.shape
    ts = _pick(S, (512, 256, 128))

    def body(d_ref, c_ref, s_ref, o_ref):
        c, s = c_ref[0], s_ref[0]
        for h in range(MLA_H):
            o_ref[0, :, h * 256:h * 256 + 128] = (d_ref[0, :, h * 256:h * 256 + 128] * Q_FOLD).astype(bf16)
            o_ref[0, :, h * 256 + 128:(h + 1) * 256] = (_rope_t(d_ref[0, :, h * 256 + 128:(h + 1) * 256], c, s) * Q_FOLD).astype(bf16)

    return _row_call(body, "qprep_bwd", B, S, ts, [_tok(ts, W), _tok(ts, 128), _tok(ts, 128)], _tok(ts, W),
                     jax.ShapeDtypeStruct((B, S, W), bf16))(dq3, cosf, sinf)


def kprep_fwd(kv3, p3, cosf, sinf):
    B, S, _ = kv3.shape
    ts = _pick(S, (512, 256, 128))
    Wn = MLA_H * NOPE

    def body(k_ref, v_ref, kr_ref, c_ref, s_ref, ko_ref, vo_ref):
        lane = lax.broadcasted_iota(jnp.int32, (1, 128), 1)
        kr = jnp.where(lane < ROPE, kr_ref[0], 0.0)
        kr = _rope(kr, c_ref[0], s_ref[0]).astype(bf16)
        for h in range(MLA_H):
            ko_ref[0, :, h * 256:h * 256 + 128] = k_ref[0, :, h * 128:(h + 1) * 128].astype(bf16)
            ko_ref[0, :, h * 256 + 128:(h + 1) * 256] = kr
        vo_ref[0] = v_ref[0].astype(bf16)

    return _row_call(body, "kprep_fwd", B, S, ts,
                     [_tok(ts, Wn, 0), _tok(ts, Wn, 1), _tok(ts, 128, OFF_KRDT // 128), _tok(ts, 128), _tok(ts, 128)],
                     [_tok(ts, 2 * Wn), _tok(ts, Wn)],
                     [jax.ShapeDtypeStruct((B, S, 2 * Wn), bf16), jax.ShapeDtypeStruct((B, S, Wn), bf16)])(kv3, kv3, p3, cosf, sinf)


def kprep_bwd(dk3, dv3, ddt3, cosf, sinf):
    B, S, _ = dk3.shape
    ts = _pick(S, (512, 256, 128))
    Wn = MLA_H * NOPE

    def body(dk_ref, dv_ref, ddt_ref, c_ref, s_ref, o_ref, kr_ref):
        acc = jnp.zeros((ts, 128), f32)
        for h in range(MLA_H):
            o_ref[0, :, h * 128:(h + 1) * 128] = dk_ref[0, :, h * 256:h * 256 + 128].astype(bf16)
            acc = acc + dk_ref[0, :, h * 256 + 128:(h + 1) * 256]
        o_ref[0, :, Wn:] = dv_ref[0].astype(bf16)
        lane = lax.broadcasted_iota(jnp.int32, (1, 128), 1)
        dkr = _rope_t(acc, c_ref[0], s_ref[0])
        kr_ref[0] = jnp.where(lane < ROPE, dkr, ddt_ref[0]).astype(bf16)

    return _row_call(body, "kprep_bwd", B, S, ts,
                     [_tok(ts, 2 * Wn), _tok(ts, Wn), _tok(ts, 128), _tok(ts, 128), _tok(ts, 128)],
                     [_tok(ts, 2 * Wn), _tok(ts, 128)],
                     [jax.ShapeDtypeStruct((B, S, 2 * Wn), bf16), jax.ShapeDtypeStruct((B, S, 128), bf16)])(dk3, dv3, ddt3, cosf, sinf)


def _shift_down(u, j):
    if j == 0:
        return u
    row = lax.broadcasted_iota(jnp.int32, u.shape, 0)
    return jnp.where(row < j, 0.0, pltpu.roll(u, j, 0))


def _shift_up(u, j):
    if j == 0:
        return u
    n = u.shape[0]
    row = lax.broadcasted_iota(jnp.int32, u.shape, 0)
    return jnp.where(row >= n - j, 0.0, pltpu.roll(u, n - j, 0))


def _conv(u, w, b, K):
    out = b
    for j in range(K):
        out = out + w[K - 1 - j:K - j, :] * _shift_down(u, j)
    return out


def _conv_bwd(u, du, w, K):
    dins = w[K - 1:K, :] * du
    dws = [None] * K
    dws[K - 1] = _rowsum(du * u)
    for j in range(1, K):
        dins = dins + w[K - 1 - j:K - j, :] * _shift_up(du, j)
        dws[K - 1 - j] = _rowsum(du * _shift_down(u, j))
    return dins, dws


CW = 256


def conv_ssd_fwd(p3, w, b):
    B, S, _ = p3.shape
    nb = D_XBC // CW

    def body(u_ref, w_ref, b_ref, o_ref):
        o_ref[0] = _silu(_conv(u_ref[0], w_ref[...], b_ref[...], CONV_K))

    return pl.pallas_call(body, name="conv_ssd_fwd", grid=(B, nb),
                          in_specs=[pl.BlockSpec((1, S, CW), lambda b, j: (b, 0, OFF_XBC // CW + j)),
                                    pl.BlockSpec((CONV_K, CW), lambda b, j: (0, j)),
                                    pl.BlockSpec((1, CW), lambda b, j: (0, j))],
                          out_specs=pl.BlockSpec((1, S, CW), lambda b, j: (b, 0, j)),
                          out_shape=jax.ShapeDtypeStruct((B, S, D_XBC), f32), compiler_params=_cparams(2))(p3, w, b)


def conv_ssd_bwd(p3, dxc3, w, b):
    B, S, _ = p3.shape
    nb = D_XBC // CW

    def body(u_ref, d_ref, w_ref, b_ref, du_ref, dw_ref, db_ref):
        @pl.when(pl.program_id(1) == 0)
        def _():
            dw_ref[...] = jnp.zeros_like(dw_ref)
            db_ref[...] = jnp.zeros_like(db_ref)

        u, wv = u_ref[0], w_ref[...]
        dpre = d_ref[0] * _dsilu(_conv(u, wv, b_ref[...], CONV_K))
        dins, dws = _conv_bwd(u, dpre, wv, CONV_K)
        du_ref[0] = dins.astype(bf16)
        for k in range(CONV_K):
            dw_ref[k:k + 1, :] += dws[k]
        db_ref[...] += _rowsum(dpre)

    return pl.pallas_call(body, name="conv_ssd_bwd", grid=(nb, B),
                          in_specs=[pl.BlockSpec((1, S, CW), lambda j, b: (b, 0, OFF_XBC // CW + j)),
                                    pl.BlockSpec((1, S, CW), lambda j, b: (b, 0, j)),
                                    pl.BlockSpec((CONV_K, CW), lambda j, b: (0, j)),
                                    pl.BlockSpec((1, CW), lambda j, b: (0, j))],
                          out_specs=[pl.BlockSpec((1, S, CW), lambda j, b: (b, 0, j)),
                                     pl.BlockSpec((CONV_K, CW), lambda j, b: (0, j)),
                                     pl.BlockSpec((1, CW), lambda j, b: (0, j))],
                          out_shape=[jax.ShapeDtypeStruct((B, S, D_XBC), bf16), jax.ShapeDtypeStruct((CONV_K, D_XBC), f32),
                                     jax.ShapeDtypeStruct((1, D_XBC), f32)], compiler_params=_cparams(2))(p3, dxc3, w, b)


def glu_fwd(u3, w, b):
    B, S, _ = u3.shape
    nb = D_FF // CW

    def body(ug_ref, uv_ref, wg_ref, wv_ref, bg_ref, bv_ref, o_ref):
        g = _conv(ug_ref[0].astype(f32), wg_ref[...], bg_ref[...], FF_K)
        v = _conv(uv_ref[0].astype(f32), wv_ref[...], bv_ref[...], FF_K)
        o_ref[0] = (_silu(g) * v).astype(bf16)

    def blk(off):
        return pl.BlockSpec((1, S, CW), lambda b, j: (b, 0, off + j))

    def par(rows, off):
        return pl.BlockSpec((rows, CW), lambda b, j: (0, off + j))

    return pl.pallas_call(body, name="glu_fwd", grid=(B, nb),
                          in_specs=[blk(0), blk(nb), par(FF_K, 0), par(FF_K, nb), par(1, 0), par(1, nb)],
                          out_specs=blk(0), out_shape=jax.ShapeDtypeStruct((B, S, D_FF), bf16),
                          compiler_params=_cparams(2))(u3, u3, w, w, b, b)


def glu_bwd(u3, da3, w, b):
    B, S, _ = u3.shape
    nb = D_FF // CW

    def body(ug_ref, uv_ref, da_ref, wg_ref, wv_ref, bg_ref, bv_ref, dug_ref, duv_ref, dwg_ref, dwv_ref, dbg_ref, dbv_ref):
        @pl.when(pl.program_id(1) == 0)
        def _():
            for r in (dwg_ref, dwv_ref, dbg_ref, dbv_ref):
                r[...] = jnp.zeros_like(r)

        ug, uv, da, wg, wv = ug_ref[0].astype(f32), uv_ref[0].astype(f32), da_ref[0], wg_ref[...], wv_ref[...]
        g = _conv(ug, wg, bg_ref[...], FF_K)
        v = _conv(uv, wv, bv_ref[...], FF_K)
        dg = da * v * _dsilu(g)
        dv = da * _silu(g)
        ding, dwsg = _conv_bwd(ug, dg, wg, FF_K)
        dinv, dwsv = _conv_bwd(uv, dv, wv, FF_K)
        dug_ref[0] = ding.astype(bf16)
        duv_ref[0] = dinv.astype(bf16)
        for k in range(FF_K):
            dwg_ref[k:k + 1, :] += dwsg[k]
            dwv_ref[k:k + 1, :] += dwsv[k]
        dbg_ref[...] += _rowsum(dg)
        dbv_ref[...] += _rowsum(dv)

    def blk(off):
        return pl.BlockSpec((1, S, CW), lambda j, b: (b, 0, off + j))

    def par(rows, off):
        return pl.BlockSpec((rows, CW), lambda j, b: (0, off + j))

    return pl.pallas_call(body, name="glu_bwd", grid=(nb, B),
                          in_specs=[blk(0), blk(nb), blk(0), par(FF_K, 0), par(FF_K, nb), par(1, 0), par(1, nb)],
                          out_specs=[blk(0), blk(0), par(FF_K, 0), par(FF_K, 0), par(1, 0), par(1, 0)],
                          out_shape=[jax.ShapeDtypeStruct((B, S, D_FF), bf16), jax.ShapeDtypeStruct((B, S, D_FF), bf16),
                                     jax.ShapeDtypeStruct((FF_K, D_FF), f32), jax.ShapeDtypeStruct((FF_K, D_FF), f32),
                                     jax.ShapeDtypeStruct((1, D_FF), f32), jax.ShapeDtypeStruct((1, D_FF), f32)],
                          compiler_params=_cparams(2))(u3, u3, da3, w, w, b, b)


def _ssd_decay(dtb, bias_row, alog_row):
    lane = lax.broadcasted_iota(jnp.int32, (1, 128), 1)
    hmask = (lane >= DT_LANE) & (lane < DT_LANE + SSD_HEADS)
    dt = jnp.where(hmask, jax.nn.softplus(dtb + bias_row), 0.0)
    a = dt * jnp.where(hmask, -jnp.exp(alog_row), 0.0)
    r = lax.broadcasted_iota(jnp.int32, (CHUNK, CHUNK), 0)
    c = lax.broadcasted_iota(jnp.int32, (CHUNK, CHUNK), 1)
    cs = jnp.dot((r >= c).astype(f32), a, precision=HIGHEST, preferred_element_type=f32)
    return dt, cs


def _expand(xt):
    return jnp.concatenate([jnp.broadcast_to(xt[DT_LANE + h:DT_LANE + h + 1, :], (SSD_HD, xt.shape[1]))
                            for h in range(SSD_HEADS)], axis=0)


_NT = (((1,), (1,)), ((), ()))
_TN = (((0,), (0,)), ((), ()))
GH = SSD_HEADS // 2
GR = GH * SSD_HD


def ssd_fwd(xc3, p3, bias_row, alog_row, dcol):
    B, S, _ = xc3.shape
    nc = S // CHUNK

    def body(xs_ref, bc_ref, dtb_ref, bias_ref, alog_ref, dcol_ref, y_ref, st_ref, state, yT):
        @pl.when(pl.program_id(1) == 0)
        def _():
            state[...] = jnp.zeros_like(state)

        dt, cs = _ssd_decay(dtb_ref[0], bias_ref[...], alog_ref[...])
        csT = cs.T
        eT = jnp.exp(csT)
        decX = _expand(jnp.exp(csT[:, CHUNK - 1:CHUNK] - csT))
        eX = _expand(eT)
        elastX = eX[:, CHUNK - 1:CHUNK]
        xsT = xs_ref[0].T
        uT = xsT * _expand(dt.T)
        bc = bc_ref[0]
        st_ref[0, 0] = state[...]
        srow = lax.broadcasted_iota(jnp.int32, (CHUNK, CHUNK), 0)
        lcol = lax.broadcasted_iota(jnp.int32, (CHUNK, CHUNK), 1)
        for g in range(2):
            Bg = bc[:, g * SSD_N:(g + 1) * SSD_N].astype(bf16)
            Cg = bc[:, (2 + g) * SSD_N:(3 + g) * SSD_N].astype(bf16)
            GT = lax.dot_general(Bg, Cg, _NT, preferred_element_type=f32)
            rows = slice(g * GR, (g + 1) * GR)
            Sg = state[rows]
            yoffT = lax.dot_general(Sg.astype(bf16), Cg, _NT, preferred_element_type=f32) * eX[rows]
            state[rows] = Sg * elastX[rows] + jnp.dot((uT[rows] * decX[rows]).astype(bf16), Bg, preferred_element_type=f32)
            for k in range(GH):
                h = g * GH + k
                hr = slice(h * SSD_HD, (h + 1) * SSD_HD)
                seg = csT[DT_LANE + h:DT_LANE + h + 1, :] - cs[:, DT_LANE + h:DT_LANE + h + 1]
                LT = jnp.where(lcol >= srow, jnp.exp(jnp.minimum(seg, 0.0)), 0.0)
                yT[hr] = (jnp.dot(uT[hr].astype(bf16), (GT * LT).astype(bf16), preferred_element_type=f32)
                          + yoffT[k * SSD_HD:(k + 1) * SSD_HD] + dcol_ref[hr] * xsT[hr])
        y_ref[0] = yT[...].T

    return pl.pallas_call(body, name="ssd_fwd", grid=(B, nc),
                          in_specs=[pl.BlockSpec((1, CHUNK, D_SSD), lambda b, c: (b, c, 0)),
                                    pl.BlockSpec((1, CHUNK, 512), lambda b, c: (b, c, 2)),
                                    pl.BlockSpec((1, CHUNK, 128), lambda b, c: (b, c, OFF_KRDT // 128)),
                                    _const(1, 128), _const(1, 128), _const(D_SSD, 1)],
                          out_specs=[pl.BlockSpec((1, CHUNK, D_SSD), lambda b, c: (b, c, 0)),
                                     pl.BlockSpec((1, 1, D_SSD, SSD_N), lambda b, c: (b, c, 0, 0))],
                          out_shape=[jax.ShapeDtypeStruct((B, S, D_SSD), f32), jax.ShapeDtypeStruct((B, nc, D_SSD, SSD_N), f32)],
                          scratch_shapes=[pltpu.VMEM((D_SSD, SSD_N), f32), pltpu.VMEM((D_SSD, CHUNK), f32)],
                          compiler_params=_cparams(2))(xc3, xc3, p3, bias_row, alog_row, dcol)


def ssd_bwd(xc3, p3, dy3, states, bias_row, alog_row, bias_col, alog_col, dcol, comm=None):
    B, S, _ = xc3.shape
    nc = S // CHUNK

    def body(xs_ref, bc_ref, dtb_ref, dy_ref, st_ref, bias_ref, alog_ref, biasc_ref, alogc_ref, dcol_ref,
             dxc_ref, ddt_ref, dalog_ref, dd_ref, dbias_ref, dS, dUT, accA, accD, accB, dcs_diag):
        @pl.when(pl.program_id(1) == 0)
        def _():
            dS[...] = jnp.zeros_like(dS)

        @pl.when(_first())
        def _():
            accA[...] = jnp.zeros_like(accA)
            accD[...] = jnp.zeros_like(accD)
            accB[...] = jnp.zeros_like(accB)

        dtb = dtb_ref[0]
        dt, cs = _ssd_decay(dtb, bias_ref[...], alog_ref[...])
        dtT, csT = dt.T, cs.T
        decX = _expand(jnp.exp(csT[:, CHUNK - 1:CHUNK] - csT))
        eX = _expand(jnp.exp(csT))
        dtX = _expand(dtT)
        elastX = eX[:, CHUNK - 1:CHUNK]
        xsT = xs_ref[0].T
        uT = xsT * dtX
        dYT = dy_ref[0].T
        bc = bc_ref[0]
        lrow = lax.broadcasted_iota(jnp.int32, (CHUNK, CHUNK), 0)
        scol = lax.broadcasted_iota(jnp.int32, (CHUNK, CHUNK), 1)
        dcs_diag[...] = jnp.zeros_like(dcs_diag)
        rs_cols = jnp.zeros((CHUNK, 128), f32)
        vparts, zparts = [], []
        for g in range(2):
            Bf = bc[:, g * SSD_N:(g + 1) * SSD_N]
            Bg = Bf.astype(bf16)
            Cg = bc[:, (2 + g) * SSD_N:(3 + g) * SSD_N].astype(bf16)
            G = lax.dot_general(Cg, Bg, _NT, preferred_element_type=f32)
            BgT = Bf.T.astype(bf16)
            rows = slice(g * GR, (g + 1) * GR)
            dSg = dS[rows]
            Sg = st_ref[0, 0, rows, :]
            dUst = jnp.dot(dSg.astype(bf16), BgT, preferred_element_type=f32) * decX[rows]
            yoffT = lax.dot_general(Sg.astype(bf16), Cg, _NT, preferred_element_type=f32) * eX[rows]
            zparts.append(dYT[rows] * yoffT - dUst * uT[rows])
            dG = jnp.zeros((CHUNK, CHUNK), f32)
            for k in range(GH):
                h = g * GH + k
                hr = slice(h * SSD_HD, (h + 1) * SSD_HD)
                seg = cs[:, DT_LANE + h:DT_LANE + h + 1] - csT[DT_LANE + h:DT_LANE + h + 1, :]
                L = jnp.where(lrow >= scol, jnp.exp(jnp.minimum(seg, 0.0)), 0.0)
                M = G * L
                dYh = dYT[hr].astype(bf16)
                dUT[hr] = jnp.dot(dYh, M.astype(bf16), preferred_element_type=f32) + dUst[k * SSD_HD:(k + 1) * SSD_HD]
                dM = lax.dot_general(dYh, uT[hr].astype(bf16), _TN, preferred_element_type=f32)
                dG = dG + dM * L
                Wm = dM * M
                rs_cols = jnp.where(scol == DT_LANE + h, jnp.sum(Wm, axis=1, keepdims=True), rs_cols)
                dcs_diag[DT_LANE + h:DT_LANE + h + 1, :] = -_rowsum(Wm)
            dGb = dG.astype(bf16)
            dYe = (dYT[rows] * eX[rows]).astype(bf16)
            ude = (uT[rows] * decX[rows]).astype(bf16)
            dC = jnp.dot(dGb, Bg, preferred_element_type=f32) + lax.dot_general(dYe, Sg.astype(bf16), _TN, preferred_element_type=f32)
            dB = (lax.dot_general(dGb, Cg, _TN, preferred_element_type=f32)
                  + lax.dot_general(ude, dSg.astype(bf16), _TN, preferred_element_type=f32))
            dxc_ref[0, :, D_SSD + g * SSD_N:D_SSD + (g + 1) * SSD_N] = dB
            dxc_ref[0, :, D_SSD + (2 + g) * SSD_N:D_SSD + (3 + g) * SSD_N] = dC
            vparts.append(elastX[rows] * jnp.sum(dSg * Sg, axis=1, keepdims=True)
                          + jnp.sum(dUst * uT[rows], axis=1, keepdims=True))
            dS[rows] = elastX[rows] * dSg + jnp.dot(dYe, Cg, preferred_element_type=f32)
        dU = dUT[...]
        dcv = dcol_ref[...]
        dxc_ref[0, :, 0:D_SSD] = (dtX * dU + dcv * dYT).T
        lane = lax.broadcasted_iota(jnp.int32, (D_SSD, CHUNK), 1)
        Z = jnp.concatenate(zparts, axis=0) + jnp.where(lane == CHUNK - 1, jnp.concatenate(vparts, axis=0), 0.0)
        hr_ = lax.broadcasted_iota(jnp.int32, (128, D_SSD), 0)
        hc_ = lax.broadcasted_iota(jnp.int32, (128, D_SSD), 1)
        hsel = (hr_ - DT_LANE == jnp.right_shift(hc_, 6)).astype(bf16)
        summands = jnp.concatenate([Z, dU * xsT, dYT * xsT], axis=1)
        hi = summands.astype(bf16)
        lo = (summands - hi.astype(f32)).astype(bf16)
        red = jnp.dot(hsel, hi, preferred_element_type=f32) + jnp.dot(hsel, lo, preferred_element_type=f32)
        dcsT = red[:, 0:CHUNK] + dcs_diag[...] + rs_cols.T
        daT = jnp.dot(dcsT, (lrow >= scol).astype(f32), precision=HIGHEST, preferred_element_type=f32)
        rowi = lax.broadcasted_iota(jnp.int32, (128, 1), 0)
        hmask = (rowi >= DT_LANE) & (rowi < DT_LANE + SSD_HEADS)
        a_col = jnp.where(hmask, -jnp.exp(alogc_ref[...]), 0.0)
        ddtT = red[:, CHUNK:2 * CHUNK] + a_col * daT
        ddt_rawT = jnp.where(hmask, ddtT * jax.nn.sigmoid(dtb.T + biasc_ref[...]), 0.0)
        ddt_ref[0] = ddt_rawT.T
        accA[...] += daT * dtT
        accD[...] += red[:, 2 * CHUNK:3 * CHUNK]
        accB[...] += ddt_rawT

        @pl.when((pl.program_id(0) == B - 1) & (pl.program_id(1) == nc - 1))
        def _():
            dalog_ref[...] = jnp.broadcast_to(jnp.sum(accA[...], axis=1, keepdims=True) * a_col, (128, 128))
            dd_ref[...] = jnp.broadcast_to(jnp.sum(accD[...], axis=1, keepdims=True), (128, 128))
            dbias_ref[...] = jnp.broadcast_to(jnp.sum(accB[...], axis=1, keepdims=True), (128, 128))

    def rev(width, cb):
        return pl.BlockSpec((1, CHUNK, width), lambda b, c: (b, nc - 1 - c, cb))

    acc_spec = pl.BlockSpec((128, 128), lambda b, c: (0, 0))
    acc_shape = jax.ShapeDtypeStruct((128, 128), f32)
    body, c_args, c_in, c_out, c_shapes, c_sems = _fuse_exchange(body, comm, 10, 5, 6, (B, nc))
    res = pl.pallas_call(body, name="ssd_bwd_x" if comm else "ssd_bwd", grid=(B, nc),
                         in_specs=[rev(D_SSD, 0), rev(512, 2), rev(128, OFF_KRDT // 128), rev(D_SSD, 0),
                                   pl.BlockSpec((1, 1, D_SSD, SSD_N), lambda b, c: (b, nc - 1 - c, 0, 0)),
                                   _const(1, 128), _const(1, 128), _const(128, 1), _const(128, 1), _const(D_SSD, 1)] + c_in,
                         out_specs=[rev(D_XBC, 0), rev(128, 0), acc_spec, acc_spec, acc_spec] + c_out,
                         out_shape=[jax.ShapeDtypeStruct((B, S, D_XBC), f32), jax.ShapeDtypeStruct((B, S, 128), f32),
                                    acc_shape, acc_shape, acc_shape] + c_shapes,
                         scratch_shapes=[pltpu.VMEM((D_SSD, SSD_N), f32), pltpu.VMEM((D_SSD, CHUNK), f32),
                                         pltpu.VMEM((128, 128), f32), pltpu.VMEM((128, 128), f32), pltpu.VMEM((128, 128), f32),
                                         pltpu.VMEM((128, 128), f32)] + c_sems,
                         compiler_params=_cparams(2))(xc3, xc3, p3, dy3, states, bias_row, alog_row, bias_col, alog_col, dcol, *c_args)
    return (*res[:5], list(res[5:]))


ATT_SCALE = float(QK) ** -0.5
LOG2E = 1.4426950408889634
LN2 = 0.6931471805599453
Q_FOLD = ATT_SCALE * LOG2E
NEG = -1e30
HP = 2


def _att_block(S):
    return _pick(S, (512, 256, 128))


def attn_fwd(q3, k3, v3, comm=None):
    B, S, _ = q3.shape
    bq = _att_block(S)
    nq = S // bq

    def body(q_ref, k_ref, v_ref, o_ref, lse_ref, m_s, l_s, acc):
        i, j = pl.program_id(2), pl.program_id(3)

        @pl.when(j == 0)
        def _():
            m_s[...] = jnp.full_like(m_s, NEG)
            l_s[...] = jnp.zeros_like(l_s)
            acc[...] = jnp.zeros_like(acc)

        def step(masked):
            for t in range(HP):
                qk = slice(t * 256, (t + 1) * 256)
                st = lax.dot_general(k_ref[0, :, qk], q_ref[0, :, qk], _NT, preferred_element_type=f32)
                if masked:
                    r = lax.broadcasted_iota(jnp.int32, (bq, bq), 0)
                    c = lax.broadcasted_iota(jnp.int32, (bq, bq), 1)
                    st = jnp.where(c >= r, st, NEG)
                m_old = m_s[t]
                m_new = jnp.maximum(m_old, jnp.max(st, axis=0, keepdims=True))
                alpha = jnp.exp2(m_old - m_new)
                pt = jnp.exp2(st - m_new)
                l_s[t] = alpha * l_s[t] + jnp.sum(pt, axis=0, keepdims=True)
                acc[t] = alpha * acc[t] + lax.dot_general(v_ref[0, :, t * VD:(t + 1) * VD], pt.astype(bf16), _TN,
                                                          preferred_element_type=f32)
                m_s[t] = m_new

        @pl.when(j < i)
        def _():
            step(False)

        @pl.when(j == i)
        def _():
            step(True)
            for t in range(HP):
                o_ref[0, :, t * VD:(t + 1) * VD] = (acc[t] / l_s[t]).T
                lse_ref[0, t] = m_s[t] + jnp.log2(l_s[t])

    grid = (B, MLA_H // HP, nq, nq)
    body, c_args, c_in, c_out, c_shapes, c_sems = _fuse_exchange(body, comm, 3, 2, 3, grid)
    res = pl.pallas_call(body, name="attn_fwd_x" if comm else "attn_fwd", grid=grid,
                         in_specs=[pl.BlockSpec((1, bq, HP * 256), lambda b, h, i, j: (b, i, h)),
                                   pl.BlockSpec((1, bq, HP * 256), lambda b, h, i, j: (b, jnp.minimum(j, i), h)),
                                   pl.BlockSpec((1, bq, HP * VD), lambda b, h, i, j: (b, jnp.minimum(j, i), h))] + c_in,
                         out_specs=[pl.BlockSpec((1, bq, HP * VD), lambda b, h, i, j: (b, i, h)),
                                    pl.BlockSpec((1, HP, 1, bq), lambda b, h, i, j: (b, h, 0, i))] + c_out,
                         out_shape=[jax.ShapeDtypeStruct((B, S, MLA_H * VD), f32), jax.ShapeDtypeStruct((B, MLA_H, 1, S), f32)] + c_shapes,
                         scratch_shapes=[pltpu.VMEM((HP, 1, bq), f32), pltpu.VMEM((HP, 1, bq), f32), pltpu.VMEM((HP, VD, bq), f32)] + c_sems,
                         compiler_params=_cparams(4))(q3, k3, v3, *c_args)
    return res[0], res[1], list(res[2:])


def attn_delta(o3, do3):
    B, S, _ = o3.shape
    ts = _pick(S, (512, 256, 128))

    def body(o_ref, do_ref, d_ref):
        for h in range(MLA_H):
            vs = slice(h * VD, (h + 1) * VD)
            d_ref[0, h] = jnp.sum(o_ref[0, :, vs] * do_ref[0, :, vs], axis=-1, keepdims=True)

    return _row_call(body, "attn_delta", B, S, ts, [_tok(ts, MLA_H * VD), _tok(ts, MLA_H * VD)],
                     pl.BlockSpec((1, MLA_H, ts, 1), lambda b, s: (b, 0, s, 0)),
                     jax.ShapeDtypeStruct((B, MLA_H, S, 1), f32))(o3, do3)


def attn_bwd(q3, k3, v3, do3, lse_row, delta_row, comm=None):
    B, S, _ = q3.shape
    bq = _att_block(S)
    nq = S // bq

    def body(q_ref, k_ref, v_ref, do_ref, lse_ref, dl_ref, dk_ref, dv_ref, dq_hbm, dk_acc, dv_acc, dq_scr, dq_sem):
        b, hp, j, i = pl.program_id(0), pl.program_id(1), pl.program_id(2), pl.program_id(3)
        rows = pl.ds(pl.multiple_of(i * bq, bq), bq)

        @pl.when((j == 0) & (i == 0))
        def _():
            dq_scr[...] = jnp.zeros_like(dq_scr)

        @pl.when(i == 0)
        def _():
            dk_acc[...] = jnp.zeros_like(dk_acc)
            dv_acc[...] = jnp.zeros_like(dv_acc)

        def step(masked):
            for t in range(HP):
                qk, vs = slice(t * 256, (t + 1) * 256), slice(t * VD, (t + 1) * VD)
                q, k = q_ref[0, :, qk], k_ref[0, :, qk]
                do = do_ref[0, :, vs].astype(bf16)
                pt = jnp.exp2(lax.dot_general(k, q, _NT, preferred_element_type=f32) - lse_ref[0, t])
                if masked:
                    r = lax.broadcasted_iota(jnp.int32, (bq, bq), 0)
                    c = lax.broadcasted_iota(jnp.int32, (bq, bq), 1)
                    pt = jnp.where(c >= r, pt, 0.0)
                dv_acc[t] += jnp.dot(pt.astype(bf16), do, preferred_element_type=f32)
                dpt = lax.dot_general(v_ref[0, :, vs], do, _NT, preferred_element_type=f32)
                dst = (pt * (dpt - dl_ref[0, t])).astype(bf16)
                dk_acc[t] += jnp.dot(dst, q, preferred_element_type=f32)
                dq_scr[t, rows, :] += lax.dot_general(dst, k, _TN, preferred_element_type=f32)

        @pl.when(i > j)
        def _():
            step(False)

        @pl.when(i == j)
        def _():
            step(True)
            for t in range(HP):
                dq_scr[t, rows, :] = dq_scr[t, rows, :] * LN2
                cp = pltpu.make_async_copy(dq_scr.at[t, rows, :],
                                           dq_hbm.at[b, rows, pl.ds(pl.multiple_of((hp * HP + t) * 256, 256), 256)], dq_sem.at[t])
                cp.start()
                cp.wait()

        @pl.when(i == nq - 1)
        def _():
            for t in range(HP):
                dk_ref[0, :, t * 256:(t + 1) * 256] = dk_acc[t] * LN2
                dv_ref[0, :, t * VD:(t + 1) * VD] = dv_acc[t]

    kspec = pl.BlockSpec((1, bq, HP * 256), lambda b, h, j, i: (b, j, h))
    vspec = pl.BlockSpec((1, bq, HP * VD), lambda b, h, j, i: (b, j, h))
    rspec = pl.BlockSpec((1, HP, 1, bq), lambda b, h, j, i: (b, h, 0, jnp.maximum(i, j)))
    grid = (B, MLA_H // HP, nq, nq)
    body, c_args, c_in, c_out, c_shapes, c_sems = _fuse_exchange(body, comm, 6, 3, 4, grid)
    res = pl.pallas_call(body, name="attn_bwd_x" if comm else "attn_bwd", grid=grid,
                         in_specs=[pl.BlockSpec((1, bq, HP * 256), lambda b, h, j, i: (b, jnp.maximum(i, j), h)), kspec, vspec,
                                   pl.BlockSpec((1, bq, HP * VD), lambda b, h, j, i: (b, jnp.maximum(i, j), h)), rspec, rspec] + c_in,
                         out_specs=[kspec, vspec, pl.BlockSpec(memory_space=pltpu.HBM)] + c_out,
                         out_shape=[jax.ShapeDtypeStruct((B, S, MLA_H * 256), f32), jax.ShapeDtypeStruct((B, S, MLA_H * VD), f32),
                                    jax.ShapeDtypeStruct((B, S, MLA_H * 256), f32)] + c_shapes,
                         scratch_shapes=[pltpu.VMEM((HP, bq, 256), f32), pltpu.VMEM((HP, bq, VD), f32), pltpu.VMEM((HP, S, 256), f32),
                                         pltpu.SemaphoreType.DMA((HP,))] + c_sems,
                         compiler_params=_cparams(4))(q3, k3, v3, do3, lse_row, delta_row, *c_args)
    return res[2], res[0], res[1], list(res[3:])


def ada_fwd(c_all, w, b):
    n = w.shape[1]

    def body(c_ref, w_ref, b_ref, o_ref):
        o_ref[...] = jnp.dot(_silu(c_ref[...]).astype(bf16), w_ref[...].astype(bf16), preferred_element_type=f32) + b_ref[...]

    return pl.pallas_call(body, name="ada_fwd", out_shape=jax.ShapeDtypeStruct((c_all.shape[0], n), f32),
                          compiler_params=pltpu.CompilerParams(vmem_limit_bytes=VMEM_LIMIT))(c_all, w, b)


def ada_bwd(c_all, dmod):
    n = dmod.shape[1]

    def body(c_ref, d_ref, o_ref):
        o_ref[...] = lax.dot_general(_silu(c_ref[...]).astype(bf16), d_ref[...].astype(bf16), _TN, preferred_element_type=f32)

    return pl.pallas_call(body, name="ada_bwd", out_shape=jax.ShapeDtypeStruct((c_all.shape[1], n), f32),
                          compiler_params=pltpu.CompilerParams(vmem_limit_bytes=VMEM_LIMIT))(c_all, dmod)


def sum_leading(x, name):
    n, R, _ = x.shape
    tr = _pick(R, (512, 256, 128, 64, 32, 16, 8))

    def body(x_ref, o_ref):
        acc = x_ref[0].astype(f32)
        for k in range(1, n):
            acc = acc + x_ref[k].astype(f32)
        o_ref[...] = acc

    return pl.pallas_call(body, name=name, grid=(R // tr,), in_specs=[pl.BlockSpec((n, tr, 128), lambda i: (0, i, 0))],
                          out_specs=pl.BlockSpec((tr, 128), lambda i: (i, 0)), out_shape=jax.ShapeDtypeStruct((R, 128), f32),
                          compiler_params=_cparams(1))(x)


def _adamw_body(w_ref, g_ref, m_ref, v_ref, d_ref, mo_ref, vo_ref):
    gv = g_ref[...]
    mn = ADAM_B1 * m_ref[...] + (1.0 - ADAM_B1) * gv
    vn = ADAM_B2 * v_ref[...] + (1.0 - ADAM_B2) * jnp.square(gv)
    m_hat = mn / (1.0 - ADAM_B1 ** ADAM_STEP)
    v_hat = vn / (1.0 - ADAM_B2 ** ADAM_STEP)
    d_ref[...] = -ADAM_LR * (m_hat / (jnp.sqrt(v_hat) + ADAM_EPS) + ADAM_WD * w_ref[...])
    mo_ref[...] = mn
    vo_ref[...] = vn


def adamw(w, g, m, v):
    R = w.shape[0]
    tr = _pick(R, (512, 256, 128, 64, 32, 16, 8))
    spec = pl.BlockSpec((tr, 128), lambda i: (i, 0))
    shp = jax.ShapeDtypeStruct((R, 128), f32)
    return pl.pallas_call(functools.partial(_adamw_body), name="adamw", grid=(R // tr,), in_specs=[spec] * 4,
                          out_specs=[spec] * 3, out_shape=[shp] * 3, compiler_params=_cparams(1))(w, g, m, v)


def _row_tile(a):
    return _pick(a, (256, 128, 64, 32, 16, 8)) if a % 8 == 0 else a


def adamw_nd(w, g, m, v):
    L, a, b = w.shape
    ta = _row_tile(a)
    spec = pl.BlockSpec((1, ta, b), lambda l, i: (l, i, 0))
    shp = jax.ShapeDtypeStruct((L, a, b), f32)
    return pl.pallas_call(functools.partial(_adamw_body), name="adamw_nd", grid=(L, a // ta), in_specs=[spec] * 4,
                          out_specs=[spec] * 3, out_shape=[shp] * 3, compiler_params=_cparams(2))(w, g, m, v)


def sum_slots(x):
    n, L, a, b = x.shape
    ta = _row_tile(a)

    def body(x_ref, o_ref):
        acc = x_ref[0].astype(f32)
        for k in range(1, n):
            acc = acc + x_ref[k].astype(f32)
        o_ref[...] = acc

    return pl.pallas_call(body, name="sum_slots", grid=(L, a // ta),
                          in_specs=[pl.BlockSpec((n, 1, ta, b), lambda l, i: (0, l, i, 0))],
                          out_specs=pl.BlockSpec((1, ta, b), lambda l, i: (l, i, 0)),
                          out_shape=jax.ShapeDtypeStruct((L, a, b), f32), compiler_params=_cparams(2))(x)


def _exchange_copies(ins, outs, sems, scatter):
    send_sems, recv_sems, local_sems = sems
    x, y, c = lax.axis_index("x"), lax.axis_index("y"), lax.axis_index("c")
    me = 4 * x + 2 * y + c
    locals_, sends, recvs = [], [], []
    for a in range(len(ins)):
        locals_.append(pltpu.make_async_copy(ins[a].at[me] if scatter[a] else ins[a], outs[a].at[me], local_sems.at[a]))
        for k in range(N_DEV - 1):
            px = 1 - x if (k + 1) & 4 else x
            py = 1 - y if (k + 1) & 2 else y
            pc = 1 - c if (k + 1) & 1 else c
            pid = 4 * px + 2 * py + pc
            src = ins[a].at[pid] if scatter[a] else ins[a]
            for slot, group in ((me, sends), (pid, recvs)):
                group.append(pltpu.make_async_remote_copy(src_ref=src, dst_ref=outs[a].at[slot], send_sem=send_sems.at[a, k],
                                                          recv_sem=recv_sems.at[a, k], device_id=(px, py, pc),
                                                          device_id_type=pl.DeviceIdType.MESH))
    return locals_, sends, recvs


def _exchange_start(ins, outs, sems, scatter):
    locals_, sends, _ = _exchange_copies(ins, outs, sems, scatter)
    for cp in locals_ + sends:
        cp.start()


def _exchange_wait(ins, outs, sems, scatter):
    locals_, sends, recvs = _exchange_copies(ins, outs, sems, scatter)
    for cp in recvs:
        cp.wait_recv()
    for cp in sends:
        cp.wait_send()
    for cp in locals_:
        cp.wait()


def _exchange_shapes(arrays, scatter):
    return [jax.ShapeDtypeStruct((N_DEV,) + tuple(a.shape[1:] if s else a.shape), a.dtype) for a, s in zip(arrays, scatter)]


def _flags(scatter, n):
    return [scatter] * n if isinstance(scatter, bool) else list(scatter)


def _exchange_sems(n):
    return [pltpu.SemaphoreType.DMA((n, N_DEV - 1)), pltpu.SemaphoreType.DMA((n, N_DEV - 1)), pltpu.SemaphoreType.DMA((n,))]


def _fuse_exchange(core, comm, n_in, n_out, n_scr, grid):
    if comm is None:
        return core, [], [], [], [], []
    arrays, scatter = comm
    n = len(arrays)
    scatter = _flags(scatter, n)

    def body(*refs):
        a, b, c = n_in + n, n_in + n + n_out, n_in + 2 * n + n_out
        cin, cout, sems = refs[n_in:a], refs[b:c], refs[c + n_scr:]
        ids = [pl.program_id(d) for d in range(len(grid))]
        first = functools.reduce(lambda p, q: p & q, [i == 0 for i in ids])
        last = functools.reduce(lambda p, q: p & q, [i == g - 1 for i, g in zip(ids, grid)])

        @pl.when(first)
        def _():
            _exchange_start(cin, cout, sems, scatter)

        core(*refs[:n_in], *refs[a:b], *refs[c:c + n_scr])

        @pl.when(last)
        def _():
            _exchange_wait(cin, cout, sems, scatter)

    hbm = pl.BlockSpec(memory_space=pltpu.HBM)
    return body, list(arrays), [hbm] * n, [hbm] * n, _exchange_shapes(arrays, scatter), _exchange_sems(n)


def exchange(arrays, scatter, name):
    n = len(arrays)
    scatter = _flags(scatter, n)

    def body(*refs):
        ins, outs, sems = refs[:n], refs[n:2 * n], refs[2 * n:]
        _exchange_start(ins, outs, sems, scatter)
        _exchange_wait(ins, outs, sems, scatter)

    hbm = pl.BlockSpec(memory_space=pltpu.HBM)
    return pl.pallas_call(body, name=name, in_specs=[hbm] * n, out_specs=[hbm] * n,
                          out_shape=_exchange_shapes(arrays, scatter), scratch_shapes=_exchange_sems(n))(*arrays)


BIG = (("w_in", "col"), ("conv_w", "col"), ("w_uq", "col"), ("w_ukv", "col"), ("w_out", "row"), ("w_up", "col"),
       ("conv_ff_w", "col"), ("w_down", "row"))
SMALL = ("b_ada", "norm_mix", "conv_b", "dt_bias", "a_log", "d_skip", "ssd_norm", "q_norm", "kv_norm", "attn_norm",
         "norm_mlp", "conv_ff_b", "final_norm")
CONVS = ("conv_w", "conv_ff_w")
PACK_ALIGN = 2048


def _padded(n):
    return -(-n // PACK_ALIGN) * PACK_ALIGN


def _flat_pad(a):
    f = a.reshape(-1)
    return jnp.pad(f, (0, _padded(f.shape[0]) - f.shape[0]))


PACK_ROWS = 512


def pack(arrs):
    f = jnp.concatenate([_flat_pad(a) for a in arrs])
    n = PACK_ROWS * 128
    return jnp.pad(f, (0, -(-f.shape[0] // n) * n - f.shape[0])).reshape(-1, 128)


def unpack(flat, shapes):
    f = flat.reshape(-1)
    out, off = [], 0
    for s in shapes:
        n = int(np.prod(s))
        out.append(f[off:off + n].reshape(s))
        off += _padded(n)
    return out


def shards_to_full(g, kind):
    _, a, b = g.shape
    if kind == "col":
        return g.transpose(1, 0, 2).reshape(a, N_DEV * b)
    return g.reshape(N_DEV * a, b)


def full_to_shards(full, kind):
    if kind == "col":
        a, nb = full.shape
        return full.reshape(a, N_DEV, nb // N_DEV).transpose(1, 0, 2)
    na, b = full.shape
    return full.reshape(N_DEV, na // N_DEV, b)


def w_in_layout(w):
    z = lambda n: jnp.zeros(w.shape[:-1] + (n,), w.dtype)
    return jnp.concatenate([w[..., :2560], w[..., 2576:2960], z(128), w[..., 2960:3216], w[..., 3216:3280],
                            w[..., 2560:2576], z(48)], axis=-1)


def w_in_unlayout(g):
    return jnp.concatenate([g[..., :2560], g[..., 3392:3408], g[..., 2560:2944], g[..., 3072:3328], g[..., 3328:3392]], axis=-1)


def w_uq_layout(w):
    return jnp.pad(w.reshape(Q_RANK, MLA_H, QK), ((0, 0), (0, 0), (0, 256 - QK))).reshape(Q_RANK, MLA_H * 256)


def w_uq_unlayout(g):
    return g.reshape(Q_RANK, MLA_H, 256)[:, :, :QK].reshape(Q_RANK, MLA_H * QK)


def w_ukv_layout(w):
    return w.reshape(KV_RANK, MLA_H, 2, 128).transpose(0, 2, 1, 3).reshape(KV_RANK, 2 * MLA_H * 128)


def w_ukv_unlayout(g):
    return g.reshape(KV_RANK, 2, MLA_H, 128).transpose(0, 2, 1, 3).reshape(KV_RANK, 2 * MLA_H * 128)


LAYOUTS = {"w_in": (w_in_layout, w_in_unlayout), "w_uq": (w_uq_layout, w_uq_unlayout), "w_ukv": (w_ukv_layout, w_ukv_unlayout)}
FIRST, REST = BIG[:4], BIG[4:]


def layer_weights(gathered, entries):
    full = {n: shards_to_full(g, kind) for (n, kind), g in zip(entries, gathered)}
    return {n: LAYOUTS[n][0](w) if n in LAYOUTS else w for n, w in full.items()}


def layer_grad_slices(g, entries):
    return [full_to_shards(LAYOUTS[n][1](g[n]) if n in LAYOUTS else g[n], kind).astype(bf16) for n, kind in entries]


def _head_row(v):
    return jnp.zeros((1, 128), f32).at[0, DT_LANE:DT_LANE + SSD_HEADS].set(v)


def layer_fwd(x3, mod, W, P, l, cosf, sinf, comm=None, late=None):
    B, S, _ = x3.shape
    T = B * S
    sv = {}
    h = normmod_fwd(x3, mod, P["norm_mix"][l][None], 0, 1)
    p = mm(h.reshape(T, D), W["w_in"], "nn", "mm_in")
    p3 = p.reshape(B, S, IN_COLS)
    bias_row, alog_row = _head_row(P["dt_bias"][l]), _head_row(P["a_log"][l])
    dcol = jnp.repeat(P["d_skip"][l], SSD_HD)[:, None]
    xc3 = conv_ssd_fwd(p3, W["conv_w"], P["conv_b"][l][None])
    yc3, states = ssd_fwd(xc3, p3, bias_row, alog_row, dcol)
    y_ssd = ssd_out_fwd(yc3, p3, P["ssd_norm"][l][None])
    cqn = rms_fwd(p3, 512, OFF_CQ // 512, Q_RANK, P["q_norm"][l][None], "rms_q_fwd")
    ckvn = rms_fwd(p3, KV_RANK, OFF_CKV // KV_RANK, KV_RANK, P["kv_norm"][l][None], "rms_kv_fwd")
    qraw = mm(cqn.reshape(T, Q_RANK), W["w_uq"], "nn", "mm_uq")
    kvraw = mm(ckvn.reshape(T, KV_RANK), W["w_ukv"], "nn", "mm_ukv")
    q3 = qprep_fwd(qraw.reshape(B, S, -1), cosf, sinf)
    k3, v3 = kprep_fwd(kvraw.reshape(B, S, -1), p3, cosf, sinf)
    o3, lse, comm_out = attn_fwd(q3, k3, v3, comm)
    if late is not None:
        W = dict(W, **late(comm_out))
    y_att = rms_fwd(o3, D, 0, D, P["attn_norm"][l][None], "rms_o_fwd")
    cat = jnp.concatenate([y_ssd, y_att], axis=-1).reshape(T, 2 * D)
    x1, y1 = mm(cat, W["w_out"], "nn", "mm_out", resid=x3.reshape(T, D), gate=mod[:, 2:3, :], seq=S)
    x13 = x1.reshape(B, S, D)
    h2 = normmod_fwd(x13, mod, P["norm_mlp"][l][None], 3, 4)
    u = mm(h2.reshape(T, D), W["w_up"], "nn", "mm_up", out_dtype=bf16)
    u3 = u.reshape(B, S, 2 * D_FF)
    a = glu_fwd(u3, W["conv_ff_w"], P["conv_ff_b"][l][None])
    x2, y2 = mm(a.reshape(T, D_FF), W["w_down"], "nn", "mm_down", resid=x1, gate=mod[:, 5:6, :], seq=S)
    sv.update(x=x3, h=h, p3=p3, xc3=xc3, yc3=yc3, states=states, cqn=cqn, ckvn=ckvn, q3=q3, k3=k3, v3=v3, o3=o3, lse=lse,
              cat=cat, y1=y1, x1=x13, h2=h2, u3=u3, a=a, y2=y2, bias_row=bias_row, alog_row=alog_row, dcol=dcol)
    return x2.reshape(B, S, D), sv, comm_out, W


def layer_bwd(dx3, sv, mod, W, P, l, cosf, sinf, comm=None, send_rest=False):
    B, S, _ = dx3.shape
    T = B * S
    g = {}
    dy2, dg2 = gate_bwd(dx3, sv["y2"].reshape(B, S, D), mod, 5)
    dy2 = dy2.reshape(T, D)
    da = mm(dy2, W["w_down"], "nt", "mm_down_dx")
    g["w_down"] = mm(sv["a"].reshape(T, D_FF), dy2, "tn", "mm_down_dw")
    dug, duv, dwg, dwv, dbg, dbv = glu_bwd(sv["u3"], da.reshape(B, S, D_FF), W["conv_ff_w"], P["conv_ff_b"][l][None])
    g["conv_ff_w"] = jnp.concatenate([dwg, dwv], axis=1)
    g["conv_ff_b"] = jnp.concatenate([dbg, dbv], axis=1)[0]
    du = jnp.concatenate([dug, duv], axis=-1).reshape(T, 2 * D_FF)
    dh2 = mm(du, W["w_up"], "nt", "mm_up_dx")
    g["w_up"] = mm(sv["h2"].reshape(T, D), du, "tn", "mm_up_dw")
    dx1, dsh2, dsc2, dnm = normmod_bwd(sv["x1"], dh2.reshape(B, S, D), dx3, mod, P["norm_mlp"][l][None], 4)
    g["norm_mlp"] = dnm[0]
    dy1, dg1 = gate_bwd(dx1, sv["y1"].reshape(B, S, D), mod, 2)
    dy1 = dy1.reshape(T, D)
    dcat = mm(dy1, W["w_out"], "nt", "mm_out_dx")
    g["w_out"] = mm(sv["cat"], dy1, "tn", "mm_out_dw")
    dcat3 = dcat.reshape(B, S, 2 * D)
    do3, dan = rms_bwd(sv["o3"], D, 0, D, dcat3, 1, P["attn_norm"][l][None], f32, "rms_o_bwd")
    g["attn_norm"] = dan[0]
    delta = attn_delta(sv["o3"], do3)
    dq3, dk3, dv3, comm_out = attn_bwd(sv["q3"], sv["k3"], sv["v3"], do3, sv["lse"], delta.reshape(B, MLA_H, 1, S), comm)
    dqraw = qprep_bwd(dq3, cosf, sinf).reshape(T, -1)
    dcqn = mm(dqraw, W["w_uq"], "nt", "mm_uq_dx")
    g["w_uq"] = mm(sv["cqn"].reshape(T, Q_RANK), dqraw, "tn", "mm_uq_dw")
    dcq, dqn = rms_bwd(sv["p3"], 512, OFF_CQ // 512, Q_RANK, dcqn.reshape(B, S, Q_RANK), 0, P["q_norm"][l][None], bf16, "rms_q_bwd")
    g["q_norm"] = dqn[0]
    dyc3, dz, dsn = ssd_out_bwd(sv["yc3"], sv["p3"], dcat3, P["ssd_norm"][l][None])
    g["ssd_norm"] = dsn[0]
    bias_col, alog_col = sv["bias_row"].reshape(128, 1), sv["alog_row"].reshape(128, 1)
    comm_rest = (layer_grad_slices(g, REST), True) if send_rest else None
    dxc3, ddt3, dalog, dd, dbias, rest_out = ssd_bwd(sv["xc3"], sv["p3"], dyc3, sv["states"], sv["bias_row"], sv["alog_row"],
                                                     bias_col, alog_col, sv["dcol"], comm_rest)
    heads = slice(DT_LANE, DT_LANE + SSD_HEADS)
    g["a_log"], g["d_skip"], g["dt_bias"] = dalog[heads, 0], dd[heads, 0], dbias[heads, 0]
    dxbc, dcw, dcb = conv_ssd_bwd(sv["p3"], dxc3, W["conv_w"], P["conv_b"][l][None])
    g["conv_w"], g["conv_b"] = dcw, dcb[0]
    dkvraw, dkrdt = kprep_bwd(dk3, dv3, ddt3, cosf, sinf)
    dkvraw = dkvraw.reshape(T, -1)
    dckvn = mm(dkvraw, W["w_ukv"], "nt", "mm_ukv_dx")
    g["w_ukv"] = mm(sv["ckvn"].reshape(T, KV_RANK), dkvraw, "tn", "mm_ukv_dw")
    dckv, dkn = rms_bwd(sv["p3"], KV_RANK, OFF_CKV // KV_RANK, KV_RANK, dckvn.reshape(B, S, KV_RANK), 0, P["kv_norm"][l][None],
                        bf16, "rms_kv_bwd")
    g["kv_norm"] = dkn[0]
    dp = jnp.concatenate([dz, dxbc, dcq, jnp.zeros((B, S, 128), bf16), dckv, dkrdt], axis=-1).reshape(T, IN_COLS)
    dh = mm(dp, W["w_in"], "nt", "mm_in_dx")
    g["w_in"] = mm(sv["h"].reshape(T, D), dp, "tn", "mm_in_dw")
    dx0, dsh1, dsc1, dnx = normmod_bwd(sv["x"], dh.reshape(B, S, D), dx1, mod, P["norm_mix"][l][None], 1)
    g["norm_mix"] = dnx[0]
    dmod = jnp.concatenate([dsh1, dsc1, dg1, dsh2, dsc2, dg2], axis=1)
    return dx0, dmod, g, comm_out, rest_out


def kernel(x, c, positions, w_ada, b_ada, norm_mix, w_in, conv_w, conv_b, dt_bias, a_log, d_skip, ssd_norm, q_norm, w_uq, kv_norm, w_ukv, attn_norm, w_out, norm_mlp, w_up, conv_ff_w, conv_ff_b, w_down, final_norm, loss_target, m_w_ada, m_b_ada, m_norm_mix, m_w_in, m_conv_w, m_conv_b, m_dt_bias, m_a_log, m_d_skip, m_ssd_norm, m_q_norm, m_w_uq, m_kv_norm, m_w_ukv, m_attn_norm, m_w_out, m_norm_mlp, m_w_up, m_conv_ff_w, m_conv_ff_b, m_w_down, m_final_norm, v_w_ada, v_b_ada, v_norm_mix, v_w_in, v_conv_w, v_conv_b, v_dt_bias, v_a_log, v_d_skip, v_ssd_norm, v_q_norm, v_w_uq, v_kv_norm, v_w_ukv, v_attn_norm, v_w_out, v_norm_mlp, v_w_up, v_conv_ff_w, v_conv_ff_b, v_w_down, v_final_norm):
    given = dict(locals())
    B, S, _ = x.shape
    me = 4 * lax.axis_index("x") + 2 * lax.axis_index("y") + lax.axis_index("c")
    P = {n: given[n] for n in SMALL}

    def shards(l, entries):
        return [given[n][l] if n in CONVS else given[n][l].astype(bf16) for n, _ in entries]

    *gathered, c_all = exchange(shards(0, FIRST) + [c], False, "gather_weights")
    W = [layer_weights(gathered, FIRST), None]

    n_ada = w_ada.shape[2]
    c_all = c_all.reshape(N_DEV * B, D)
    b_sh = lax.dynamic_slice_in_dim(b_ada, me * n_ada, n_ada, axis=1)
    mod_sh = jnp.stack([ada_fwd(c_all, w_ada[l], b_sh[l][None]) for l in range(DEPTH)])
    (mod_g,) = exchange([mod_sh], False, "gather_mod")
    mod_mine = lax.dynamic_slice_in_dim(mod_g, me * B, B, axis=2)
    mods = mod_mine.transpose(1, 2, 0, 3).reshape(DEPTH, B, 6, D)

    inv_freq = jnp.asarray(1.0 / (ROPE_BASE ** (np.arange(0, ROPE, 2, dtype=np.float32) / ROPE)))
    ang = positions.astype(f32)[..., None] * inv_freq
    zeros = jnp.zeros((B, S, 128 - ROPE), f32)
    cosf = jnp.concatenate([jnp.cos(ang), jnp.cos(ang), zeros], axis=-1)
    sinf = jnp.concatenate([jnp.sin(ang), jnp.sin(ang), zeros], axis=-1)

    saved = [None] * DEPTH
    nr = len(REST)
    xl, saved[0], gathered, W[0] = layer_fwd(x, mods[0], W[0], P, 0, cosf, sinf, comm=(shards(0, REST) + shards(1, BIG), False),
                                             late=lambda got: layer_weights(got[:nr], REST))
    W[1] = layer_weights(gathered[nr:], BIG)
    xl, saved[1], _, _ = layer_fwd(xl, mods[1], W[1], P, 1, cosf, sinf)
    dxl, d_final, loss_part = final_loss(xl, final_norm[None], loss_target)
    grads, dmods, recv = [None] * DEPTH, [None] * DEPTH, [None] * DEPTH
    dxl, dmods[1], grads[1], _, _ = layer_bwd(dxl, saved[1], mods[1], W[1], P, 1, cosf, sinf)
    grad_x, dmods[0], grads[0], recv[1], recv_rest = layer_bwd(dxl, saved[0], mods[0], W[0], P, 0, cosf, sinf,
                                                               comm=(layer_grad_slices(grads[1], BIG), True), send_rest=True)

    stack = lambda n: jnp.stack([grads[l][n] for l in range(DEPTH)])
    small_names = [n for n in SMALL if n not in ("b_ada", "final_norm")]
    partial = pack([stack(n) for n in small_names] + [d_final[0], loss_part[0]])
    dmod_all = jnp.stack(dmods)
    *recv_first, part_g, dmod_g = exchange(layer_grad_slices(grads[0], FIRST) + [partial, dmod_all],
                                           [True] * len(FIRST) + [False, False], "exchange_tail")
    recv[0] = recv_first + recv_rest
    big_g = [jnp.concatenate([sum_slots(recv[l][i][:, None]) for l in range(DEPTH)]) for i in range(len(BIG))]
    small_sum = sum_leading(part_g, "sum_partials")
    small_g = unpack(small_sum, [given[n].shape for n in small_names] + [(D,), (128,)])
    gsmall = dict(zip(small_names + ["final_norm"], small_g[:-1]))
    loss = small_g[-1][0]
    dmod_rows = dmod_g.transpose(0, 2, 1, 3, 4).reshape(N_DEV * B, DEPTH * 6 * D)
    gsmall["b_ada"] = sum_leading(dmod_rows.reshape(N_DEV * B, -1, 128), "sum_b_ada").reshape(DEPTH, 6 * D)
    dmod_cols = dmod_rows.reshape(N_DEV * B, DEPTH, N_DEV, n_ada)
    dmod_sh = lax.dynamic_slice_in_dim(dmod_cols, me, 1, axis=2)[:, :, 0, :]
    g_w_ada = jnp.stack([ada_bwd(c_all, dmod_sh[:, l, :]) for l in range(DEPTH)])

    res = {"grad": {}, "delta": {}, "new_m": {}, "new_v": {}}
    for n, gv in zip([n for n, _ in BIG] + ["w_ada"], big_g + [g_w_ada]):
        res["grad"][n] = gv
        res["delta"][n], res["new_m"][n], res["new_v"][n] = adamw_nd(given[n], gv, given["m_" + n], given["v_" + n])
    shapes = [given[n].shape for n in SMALL]
    flat = adamw(pack([given[n] for n in SMALL]), pack([gsmall[n] for n in SMALL]), pack([given["m_" + n] for n in SMALL]),
                 pack([given["v_" + n] for n in SMALL]))
    for n in SMALL:
        res["grad"][n] = gsmall[n]
    for key, arr in zip(("delta", "new_m", "new_v"), flat):
        res[key].update(zip(SMALL, unpack(arr, shapes)))
    order = ["w_ada", "b_ada", "norm_mix", "w_in", "conv_w", "conv_b", "dt_bias", "a_log", "d_skip", "ssd_norm", "q_norm", "w_uq",
             "kv_norm", "w_ukv", "attn_norm", "w_out", "norm_mlp", "w_up", "conv_ff_w", "conv_ff_b", "w_down", "final_norm"]
    return (loss, grad_x, *[res[k][n] for k in ("grad", "delta", "new_m", "new_v") for n in order])
```

```python
import functools

import numpy as np
import jax
import jax.numpy as jnp
from jax import lax
from jax.experimental import pallas as pl
from jax.experimental.pallas import tpu as pltpu

f32, bf16 = jnp.float32, jnp.bfloat16
HIGHEST = lax.Precision.HIGHEST

D = 1024
D_SSD = 1024
SSD_HEADS = 16
SSD_HD = 64
SSD_N = 128
CHUNK = 128
D_XBC = 1536
CONV_K = 4
MLA_H = 8
NOPE = 128
ROPE = 64
VD = 128
QK = NOPE + ROPE
Q_RANK = 384
KV_RANK = 256
D_FF = 2816
FF_K = 3
EPS = 1e-6
ROPE_BASE = 10000.0
DEPTH = 2
ADAM_LR, ADAM_B1, ADAM_B2, ADAM_EPS, ADAM_WD, ADAM_STEP = 0.001, 0.9, 0.999, 1e-08, 0.01, 10

N_DEV = 8
IN_COLS = 3456
OFF_XBC, OFF_CQ, OFF_CKV, OFF_KRDT = 1024, 2560, 3072, 3328
DT_LANE = 64
VMEM_LIMIT = 48 * 1024 * 1024
MM_K_WHOLE = 4096


def _cparams(n_grid):
    return pltpu.CompilerParams(dimension_semantics=("arbitrary",) * n_grid, vmem_limit_bytes=VMEM_LIMIT)


def _pick(n, cands):
    for c in cands:
        if n % c == 0:
            return c
    return n


def _silu(x):
    return x * jax.nn.sigmoid(x)


def _dsilu(x):
    s = jax.nn.sigmoid(x)
    return s * (1.0 + x * (1.0 - s))


def _rowsum(x):
    return jnp.sum(x, axis=0, keepdims=True)


def mm(a, b, mode, name, out_dtype=f32, resid=None, gate=None, seq=None, comm=None):
    parts = list(a) if isinstance(a, (tuple, list)) else [a]
    np_ = len(parts)
    if mode == "nn":
        (M, Kp), N = parts[0].shape, b.shape[1]
    elif mode == "nt":
        (M, Kp), N = parts[0].shape, b.shape[0]
    else:
        (Kp, M), N = parts[0].shape, b.shape[1]
    K = Kp * np_
    gated = resid is not None
    whole = np_ > 1 or K <= MM_K_WHOLE
    tm = _pick(seq if gated else M, (1024, 1408, 512, 384, 256, 128) if K <= MM_K_WHOLE else (512, 256, 128))
    tn = _pick(N, (512, 384, 256, 128))
    tk = K if whole else _pick(K, (2816, 2048, 1024, 512))
    if mode == "tn":
        tm = _pick(M, (1024, 1408, 512, 384, 256, 128))
    nk = K // tk
    dims = {"nn": ((1,), (0,)), "nt": ((1,), (1,)), "tn": ((0,), (0,))}[mode]

    def body(*refs):
        a_refs, b_ref, rest = refs[:np_], refs[np_], refs[np_ + 1:]
        if gated:
            r_ref, g_ref, o_ref, y_ref, acc = rest
        else:
            o_ref, acc = rest

        def finish(res):
            if gated:
                y_ref[...] = res
                o_ref[...] = r_ref[...] + g_ref[0] * res
            else:
                o_ref[...] = res.astype(out_dtype)

        prod = None
        for p, a_ref in enumerate(a_refs):
            if np_ == 1:
                bv = b_ref[...]
            else:
                bv = b_ref[:, p * Kp:(p + 1) * Kp] if mode == "nt" else b_ref[p * Kp:(p + 1) * Kp, :]
            term = lax.dot_general(a_ref[...].astype(bf16), bv.astype(bf16), (dims, ((), ())), preferred_element_type=f32)
            prod = term if prod is None else prod + term
        if nk == 1:
            finish(prod)
        else:
            k = pl.program_id(2)

            @pl.when(k == 0)
            def _():
                acc[...] = prod

            @pl.when(k > 0)
            def _():
                acc[...] += prod

            @pl.when(k == nk - 1)
            def _():
                finish(acc[...])

    if np_ > 1:
        a_spec = pl.BlockSpec((tm, Kp), lambda i, j, k: (i, 0))
    elif mode == "tn":
        a_spec = pl.BlockSpec((tk, tm), lambda i, j, k: (k, i))
    else:
        a_spec = pl.BlockSpec((tm, tk), lambda i, j, k: (i, k))
    b_spec = pl.BlockSpec((tn, tk), lambda i, j, k: (j, k)) if mode == "nt" else pl.BlockSpec((tk, tn), lambda i, j, k: (k, j))
    o_spec = pl.BlockSpec((tm, tn), lambda i, j, k: (i, j))
    in_specs, args = [a_spec] * np_ + [b_spec], parts + [b]
    out_specs, out_shape = [o_spec], [jax.ShapeDtypeStruct((M, N), out_dtype)]
    if gated:
        per = seq // tm
        in_specs += [o_spec, pl.BlockSpec((1, 1, tn), lambda i, j, k: (i // per, 0, j))]
        args += [resid, gate]
        out_specs = [o_spec, o_spec]
        out_shape = [jax.ShapeDtypeStruct((M, N), f32), jax.ShapeDtypeStruct((M, N), f32)]
    grid = (M // tm, N // tn, nk)
    body, c_args, c_in, c_out, c_shapes, c_sems = _fuse_exchange(body, comm, len(args), len(out_specs), 1, grid)
    res = pl.pallas_call(body, name=name + "_x" if comm else name, grid=grid, in_specs=in_specs + c_in, out_specs=out_specs + c_out,
                         out_shape=out_shape + c_shapes, scratch_shapes=[pltpu.VMEM((tm, tn), f32)] + c_sems,
                         compiler_params=_cparams(3))(*args, *c_args)
    own = res[:len(out_specs)]
    own = own[0] if len(own) == 1 else tuple(own)
    return (own, list(res[len(out_specs):])) if comm else own


def _tok(ts, width, cb=0):
    return pl.BlockSpec((1, ts, width), lambda b, s: (b, s, cb))


def _perb(rows, width):
    return pl.BlockSpec((1, rows, width), lambda b, s: (b, 0, 0))


def _const(rows, width):
    return pl.BlockSpec((rows, width), lambda b, s: (0, 0))


def _row_call(body, name, B, S, ts, in_specs, out_specs, out_shape, scratch=()):
    return pl.pallas_call(body, name=name, grid=(B, S // ts), in_specs=in_specs, out_specs=out_specs,
                          out_shape=out_shape, scratch_shapes=list(scratch), compiler_params=_cparams(2))


def _first():
    return (pl.program_id(0) == 0) & (pl.program_id(1) == 0)


def normmod_fwd(x3, mod, g, i_sh, i_sc):
    B, S, C = x3.shape
    ts = _pick(S, (512, 256, 128))

    def body(x_ref, mod_ref, g_ref, h_ref):
        x = x_ref[0]
        r = lax.rsqrt(jnp.mean(x * x, axis=-1, keepdims=True) + EPS)
        n = x * r * g_ref[...]
        h_ref[0] = (n * (1.0 + mod_ref[0, i_sc:i_sc + 1, :]) + mod_ref[0, i_sh:i_sh + 1, :]).astype(bf16)

    return _row_call(body, "normmod_fwd", B, S, ts, [_tok(ts, C), _perb(6, C), _const(1, C)], _tok(ts, C),
                     jax.ShapeDtypeStruct((B, S, C), bf16))(x3, mod, g)


def normmod_bwd(x3, dh3, resid3, mod, g, i_sc):
    B, S, C = x3.shape
    ts = _pick(S, (512, 256, 128))

    def body(x_ref, dh_ref, r_ref, mod_ref, g_ref, dx_ref, dsh_ref, dsc_ref, dg_ref):
        @pl.when(pl.program_id(1) == 0)
        def _():
            dsh_ref[...] = jnp.zeros_like(dsh_ref)
            dsc_ref[...] = jnp.zeros_like(dsc_ref)

        @pl.when(_first())
        def _():
            dg_ref[...] = jnp.zeros_like(dg_ref)

        x, dh, gv = x_ref[0], dh_ref[0], g_ref[...]
        r = lax.rsqrt(jnp.mean(x * x, axis=-1, keepdims=True) + EPS)
        xh = x * r
        dn = dh * (1.0 + mod_ref[0, i_sc:i_sc + 1, :])
        dsh_ref[0] += _rowsum(dh)
        dsc_ref[0] += _rowsum(dh * xh * gv)
        dg_ref[...] += _rowsum(dn * xh)
        dxh = dn * gv
        dx_ref[0] = r * (dxh - xh * jnp.mean(dxh * xh, axis=-1, keepdims=True)) + r_ref[0]

    return _row_call(body, "normmod_bwd", B, S, ts,
                     [_tok(ts, C), _tok(ts, C), _tok(ts, C), _perb(6, C), _const(1, C)],
                     [_tok(ts, C), _perb(1, C), _perb(1, C), _const(1, C)],
                     [jax.ShapeDtypeStruct((B, S, C), f32), jax.ShapeDtypeStruct((B, 1, C), f32),
                      jax.ShapeDtypeStruct((B, 1, C), f32), jax.ShapeDtypeStruct((1, C), f32)])(x3, dh3, resid3, mod, g)


def gate_bwd(dx3, y3, mod, i_g):
    B, S, C = dx3.shape
    ts = _pick(S, (512, 256, 128))

    def body(dx_ref, y_ref, mod_ref, dy_ref, dgate_ref):
        @pl.when(pl.program_id(1) == 0)
        def _():
            dgate_ref[...] = jnp.zeros_like(dgate_ref)

        dx = dx_ref[0]
        dy_ref[0] = (dx * mod_ref[0, i_g:i_g + 1, :]).astype(bf16)
        dgate_ref[0] += _rowsum(dx * y_ref[0])

    return _row_call(body, "gate_bwd", B, S, ts, [_tok(ts, C), _tok(ts, C), _perb(6, C)], [_tok(ts, C), _perb(1, C)],
                     [jax.ShapeDtypeStruct((B, S, C), bf16), jax.ShapeDtypeStruct((B, 1, C), f32)])(dx3, y3, mod)


def rms_fwd(src3, width, cb, n, g, name):
    B, S, _ = src3.shape
    ts = _pick(S, (512, 256, 128))

    def body(x_ref, g_ref, o_ref):
        x = x_ref[0][:, :n]
        r = lax.rsqrt(jnp.mean(x * x, axis=-1, keepdims=True) + EPS)
        o_ref[0] = (x * r * g_ref[...]).astype(bf16)

    return _row_call(body, name, B, S, ts, [_tok(ts, width, cb), _const(1, n)], _tok(ts, n),
                     jax.ShapeDtypeStruct((B, S, n), bf16))(src3, g)


def rms_bwd(src3, width, cb, n, dout3, dcb, g, out_dtype, name):
    B, S, _ = src3.shape
    ts = _pick(S, (512, 256, 128))

    def body(x_ref, do_ref, g_ref, dx_ref, dg_ref):
        @pl.when(_first())
        def _():
            dg_ref[...] = jnp.zeros_like(dg_ref)

        x = x_ref[0][:, :n]
        do = do_ref[0].astype(f32)
        r = lax.rsqrt(jnp.mean(x * x, axis=-1, keepdims=True) + EPS)
        xh = x * r
        dg_ref[...] += _rowsum(do * xh)
        dxh = do * g_ref[...]
        dx_ref[0] = (r * (dxh - xh * jnp.mean(dxh * xh, axis=-1, keepdims=True))).astype(out_dtype)

    return _row_call(body, name, B, S, ts, [_tok(ts, width, cb), _tok(ts, n, dcb), _const(1, n)],
                     [_tok(ts, n), _const(1, n)],
                     [jax.ShapeDtypeStruct((B, S, n), out_dtype), jax.ShapeDtypeStruct((1, n), f32)])(src3, dout3, g)


def final_loss(x3, g, tgt3):
    B, S, C = x3.shape
    ts = _pick(S, (512, 256, 128))

    def body(x_ref, g_ref, t_ref, dx_ref, dg_ref, loss_ref):
        @pl.when(_first())
        def _():
            dg_ref[...] = jnp.zeros_like(dg_ref)
            loss_ref[...] = jnp.zeros_like(loss_ref)

        x, gv = x_ref[0], g_ref[...]
        r = lax.rsqrt(jnp.mean(x * x, axis=-1, keepdims=True) + EPS)
        xh = x * r
        e = xh * gv - t_ref[0]
        loss_ref[...] += 0.5 * jnp.sum(e * e) / C
        dout = e / C
        dg_ref[...] += _rowsum(dout * xh)
        dxh = dout * gv
        dx_ref[0] = r * (dxh - xh * jnp.mean(dxh * xh, axis=-1, keepdims=True))

    return _row_call(body, "final_loss", B, S, ts, [_tok(ts, C), _const(1, C), _tok(ts, C)],
                     [_tok(ts, C), _const(1, C), _const(1, 128)],
                     [jax.ShapeDtypeStruct((B, S, C), f32), jax.ShapeDtypeStruct((1, C), f32),
                      jax.ShapeDtypeStruct((1, 128), f32)])(x3, g, tgt3)


def ssd_out_fwd(yc3, p3, w):
    B, S, C = yc3.shape
    ts = _pick(S, (512, 256, 128))
    half = C // 2

    def body(y_ref, z_ref, w_ref, o_ref):
        y = y_ref[0] * _silu(z_ref[0])
        for lo in (0, half):
            yg = y[:, lo:lo + half]
            r = lax.rsqrt(jnp.mean(yg * yg, axis=-1, keepdims=True) + EPS)
            o_ref[0, :, lo:lo + half] = (yg * r * w_ref[:, lo:lo + half]).astype(bf16)

    return _row_call(body, "ssd_out_fwd", B, S, ts, [_tok(ts, C), _tok(ts, C, 0), _const(1, C)], _tok(ts, C),
                     jax.ShapeDtypeStruct((B, S, C), bf16))(yc3, p3, w)


def ssd_out_bwd(yc3, p3, dcat3, w):
    B, S, C = yc3.shape
    ts = _pick(S, (512, 256, 128))
    half = C // 2

    def body(y_ref, z_ref, do_ref, w_ref, dyc_ref, dz_ref, dw_ref):
        @pl.when(_first())
        def _():
            dw_ref[...] = jnp.zeros_like(dw_ref)

        yc, z, do = y_ref[0], z_ref[0], do_ref[0]
        sz = _silu(z)
        y = yc * sz
        for lo in (0, half):
            sl = slice(lo, lo + half)
            yg, dog, wg = y[:, sl], do[:, sl], w_ref[:, sl]
            r = lax.rsqrt(jnp.mean(yg * yg, axis=-1, keepdims=True) + EPS)
            yh = yg * r
            dw_ref[:, sl] += _rowsum(dog * yh)
            dyh = dog * wg
            dy = r * (dyh - yh * jnp.mean(dyh * yh, axis=-1, keepdims=True))
            dyc_ref[0, :, sl] = dy * sz[:, sl]
            dz_ref[0, :, sl] = (dy * yc[:, sl] * _dsilu(z[:, sl])).astype(bf16)

    return _row_call(body, "ssd_out_bwd", B, S, ts, [_tok(ts, C), _tok(ts, C, 0), _tok(ts, C, 0), _const(1, C)],
                     [_tok(ts, C), _tok(ts, C), _const(1, C)],
                     [jax.ShapeDtypeStruct((B, S, C), f32), jax.ShapeDtypeStruct((B, S, C), bf16),
                      jax.ShapeDtypeStruct((1, C), f32)])(yc3, p3, dcat3, w)


def _rot(t):
    lane = lax.broadcasted_iota(jnp.int32, t.shape, 1)
    return jnp.where(lane < ROPE // 2, -pltpu.roll(t, 128 - ROPE // 2, 1), pltpu.roll(t, ROPE // 2, 1))


def _rope(t, cosf, sinf):
    return t * cosf + _rot(t) * sinf


def _rope_t(d, cosf, sinf):
    return d * cosf - _rot(d * sinf)


def qprep_fwd(qraw3, cosf, sinf):
    B, S, W = qraw3.shape
    ts = _pick(S, (512, 256, 128))

    def body(q_ref, c_ref, s_ref, o_ref):
        c, s = c_ref[0], s_ref[0]
        for h in range(MLA_H):
            o_ref[0, :, h * 256:h * 256 + 128] = (q_ref[0, :, h * 256:h * 256 + 128] * Q_FOLD).astype(bf16)
            o_ref[0, :, h * 256 + 128:(h + 1) * 256] = (_rope(q_ref[0, :, h * 256 + 128:(h + 1) * 256], c, s) * Q_FOLD).astype(bf16)

    return _row_call(body, "qprep_fwd", B, S, ts, [_tok(ts, W), _tok(ts, 128), _tok(ts, 128)], _tok(ts, W),
                     jax.ShapeDtypeStruct((B, S, W), bf16))(qraw3, cosf, sinf)


def qprep_bwd(dq3, cosf, sinf):
    B, S, W = dq3.shape
    ts = _pick(S, (512, 256, 128))

    def body(d_ref, c_ref, s_ref, o_ref):
        c, s = c_ref[0], s_ref[0]
        for h in range(MLA_H):
            o_ref[0, :, h * 256:h * 256 + 128] = (d_ref[0, :, h * 256:h * 256 + 128] * Q_FOLD).astype(bf16)
            o_ref[0, :, h * 256 + 128:(h + 1) * 256] = (_rope_t(d_ref[0, :, h * 256 + 128:(h + 1) * 256], c, s) * Q_FOLD).astype(bf16)

    return _row_call(body, "qprep_bwd", B, S, ts, [_tok(ts, W), _tok(ts, 128), _tok(ts, 128)], _tok(ts, W),
                     jax.ShapeDtypeStruct((B, S, W), bf16))(dq3, cosf, sinf)


def kprep_fwd(kv3, p3, cosf, sinf):
    B, S, _ = kv3.shape
    ts = _pick(S, (512, 256, 128))
    Wn = MLA_H * NOPE

    def body(k_ref, v_ref, kr_ref, c_ref, s_ref, ko_ref, vo_ref):
        lane = lax.broadcasted_iota(jnp.int32, (1, 128), 1)
        kr = jnp.where(lane < ROPE, kr_ref[0], 0.0)
        kr = _rope(kr, c_ref[0], s_ref[0]).astype(bf16)
        for h in range(MLA_H):
            ko_ref[0, :, h * 256:h * 256 + 128] = k_ref[0, :, h * 128:(h + 1) * 128].astype(bf16)
            ko_ref[0, :, h * 256 + 128:(h + 1) * 256] = kr
        vo_ref[0] = v_ref[0].astype(bf16)

    return _row_call(body, "kprep_fwd", B, S, ts,
                     [_tok(ts, Wn, 0), _tok(ts, Wn, 1), _tok(ts, 128, OFF_KRDT // 128), _tok(ts, 128), _tok(ts, 128)],
                     [_tok(ts, 2 * Wn), _tok(ts, Wn)],
                     [jax.ShapeDtypeStruct((B, S, 2 * Wn), bf16), jax.ShapeDtypeStruct((B, S, Wn), bf16)])(kv3, kv3, p3, cosf, sinf)


def kprep_bwd(dk3, dv3, ddt3, cosf, sinf):
    B, S, _ = dk3.shape
    ts = _pick(S, (512, 256, 128))
    Wn = MLA_H * NOPE

    def body(dk_ref, dv_ref, ddt_ref, c_ref, s_ref, o_ref, kr_ref):
        acc = jnp.zeros((ts, 128), f32)
        for h in range(MLA_H):
            o_ref[0, :, h * 128:(h + 1) * 128] = dk_ref[0, :, h * 256:h * 256 + 128].astype(bf16)
            acc = acc + dk_ref[0, :, h * 256 + 128:(h + 1) * 256]
        o_ref[0, :, Wn:] = dv_ref[0].astype(bf16)
        lane = lax.broadcasted_iota(jnp.int32, (1, 128), 1)
        dkr = _rope_t(acc, c_ref[0], s_ref[0])
        kr_ref[0] = jnp.where(lane < ROPE, dkr, ddt_ref[0]).astype(bf16)

    return _row_call(body, "kprep_bwd", B, S, ts,
                     [_tok(ts, 2 * Wn), _tok(ts, Wn), _tok(ts, 128), _tok(ts, 128), _tok(ts, 128)],
                     [_tok(ts, 2 * Wn), _tok(ts, 128)],
                     [jax.ShapeDtypeStruct((B, S, 2 * Wn), bf16), jax.ShapeDtypeStruct((B, S, 128), bf16)])(dk3, dv3, ddt3, cosf, sinf)


def _shift_down(u, j):
    if j == 0:
        return u
    row = lax.broadcasted_iota(jnp.int32, u.shape, 0)
    return jnp.where(row < j, 0.0, pltpu.roll(u, j, 0))


def _shift_up(u, j):
    if j == 0:
        return u
    n = u.shape[0]
    row = lax.broadcasted_iota(jnp.int32, u.shape, 0)
    return jnp.where(row >= n - j, 0.0, pltpu.roll(u, n - j, 0))


def _conv(u, w, b, K):
    out = b
    for j in range(K):
        out = out + w[K - 1 - j:K - j, :] * _shift_down(u, j)
    return out


def _conv_bwd(u, du, w, K):
    dins = w[K - 1:K, :] * du
    dws = [None] * K
    dws[K - 1] = _rowsum(du * u)
    for j in range(1, K):
        dins = dins + w[K - 1 - j:K - j, :] * _shift_up(du, j)
        dws[K - 1 - j] = _rowsum(du * _shift_down(u, j))
    return dins, dws


CW = 256


def conv_ssd_fwd(p3, w, b):
    B, S, _ = p3.shape
    nb = D_XBC // CW

    def body(u_ref, w_ref, b_ref, o_ref):
        o_ref[0] = _silu(_conv(u_ref[0], w_ref[...], b_ref[...], CONV_K))

    return pl.pallas_call(body, name="conv_ssd_fwd", grid=(B, nb),
                          in_specs=[pl.BlockSpec((1, S, CW), lambda b, j: (b, 0, OFF_XBC // CW + j)),
                                    pl.BlockSpec((CONV_K, CW), lambda b, j: (0, j)),
                                    pl.BlockSpec((1, CW), lambda b, j: (0, j))],
                          out_specs=pl.BlockSpec((1, S, CW), lambda b, j: (b, 0, j)),
                          out_shape=jax.ShapeDtypeStruct((B, S, D_XBC), f32), compiler_params=_cparams(2))(p3, w, b)


def conv_ssd_bwd(p3, dxc3, w, b):
    B, S, _ = p3.shape
    nb = D_XBC // CW

    def body(u_ref, d_ref, w_ref, b_ref, du_ref, dw_ref, db_ref):
        @pl.when(pl.program_id(1) == 0)
        def _():
            dw_ref[...] = jnp.zeros_like(dw_ref)
            db_ref[...] = jnp.zeros_like(db_ref)

        u, wv = u_ref[0], w_ref[...]
        dpre = d_ref[0] * _dsilu(_conv(u, wv, b_ref[...], CONV_K))
        dins, dws = _conv_bwd(u, dpre, wv, CONV_K)
        du_ref[0] = dins.astype(bf16)
        for k in range(CONV_K):
            dw_ref[k:k + 1, :] += dws[k]
        db_ref[...] += _rowsum(dpre)

    return pl.pallas_call(body, name="conv_ssd_bwd", grid=(nb, B),
                          in_specs=[pl.BlockSpec((1, S, CW), lambda j, b: (b, 0, OFF_XBC // CW + j)),
                                    pl.BlockSpec((1, S, CW), lambda j, b: (b, 0, j)),
                                    pl.BlockSpec((CONV_K, CW), lambda j, b: (0, j)),
                                    pl.BlockSpec((1, CW), lambda j, b: (0, j))],
                          out_specs=[pl.BlockSpec((1, S, CW), lambda j, b: (b, 0, j)),
                                     pl.BlockSpec((CONV_K, CW), lambda j, b: (0, j)),
                                     pl.BlockSpec((1, CW), lambda j, b: (0, j))],
                          out_shape=[jax.ShapeDtypeStruct((B, S, D_XBC), bf16), jax.ShapeDtypeStruct((CONV_K, D_XBC), f32),
                                     jax.ShapeDtypeStruct((1, D_XBC), f32)], compiler_params=_cparams(2))(p3, dxc3, w, b)


def glu_fwd(u3, w, b):
    B, S, _ = u3.shape
    nb = D_FF // CW

    def body(ug_ref, uv_ref, wg_ref, wv_ref, bg_ref, bv_ref, o_ref):
        g = _conv(ug_ref[0].astype(f32), wg_ref[...], bg_ref[...], FF_K)
        v = _conv(uv_ref[0].astype(f32), wv_ref[...], bv_ref[...], FF_K)
        o_ref[0] = (_silu(g) * v).astype(bf16)

    def blk(off):
        return pl.BlockSpec((1, S, CW), lambda b, j: (b, 0, off + j))

    def par(rows, off):
        return pl.BlockSpec((rows, CW), lambda b, j: (0, off + j))

    return pl.pallas_call(body, name="glu_fwd", grid=(B, nb),
                          in_specs=[blk(0), blk(nb), par(FF_K, 0), par(FF_K, nb), par(1, 0), par(1, nb)],
                          out_specs=blk(0), out_shape=jax.ShapeDtypeStruct((B, S, D_FF), bf16),
                          compiler_params=_cparams(2))(u3, u3, w, w, b, b)


def glu_bwd(u3, da3, w, b):
    B, S, _ = u3.shape
    nb = D_FF // CW

    def body(ug_ref, uv_ref, da_ref, wg_ref, wv_ref, bg_ref, bv_ref, dug_ref, duv_ref, dwg_ref, dwv_ref, dbg_ref, dbv_ref):
        @pl.when(pl.program_id(1) == 0)
        def _():
            for r in (dwg_ref, dwv_ref, dbg_ref, dbv_ref):
                r[...] = jnp.zeros_like(r)

        ug, uv, da, wg, wv = ug_ref[0].astype(f32), uv_ref[0].astype(f32), da_ref[0], wg_ref[...], wv_ref[...]
        g = _conv(ug, wg, bg_ref[...], FF_K)
        v = _conv(uv, wv, bv_ref[...], FF_K)
        dg = da * v * _dsilu(g)
        dv = da * _silu(g)
        ding, dwsg = _conv_bwd(ug, dg, wg, FF_K)
        dinv, dwsv = _conv_bwd(uv, dv, wv, FF_K)
        dug_ref[0] = ding.astype(bf16)
        duv_ref[0] = dinv.astype(bf16)
        for k in range(FF_K):
            dwg_ref[k:k + 1, :] += dwsg[k]
            dwv_ref[k:k + 1, :] += dwsv[k]
        dbg_ref[...] += _rowsum(dg)
        dbv_ref[...] += _rowsum(dv)

    def blk(off):
        return pl.BlockSpec((1, S, CW), lambda j, b: (b, 0, off + j))

    def par(rows, off):
        return pl.BlockSpec((rows, CW), lambda j, b: (0, off + j))

    return pl.pallas_call(body, name="glu_bwd", grid=(nb, B),
                          in_specs=[blk(0), blk(nb), blk(0), par(FF_K, 0), par(FF_K, nb), par(1, 0), par(1, nb)],
                          out_specs=[blk(0), blk(0), par(FF_K, 0), par(FF_K, 0), par(1, 0), par(1, 0)],
                          out_shape=[jax.ShapeDtypeStruct((B, S, D_FF), bf16), jax.ShapeDtypeStruct((B, S, D_FF), bf16),
                                     jax.ShapeDtypeStruct((FF_K, D_FF), f32), jax.ShapeDtypeStruct((FF_K, D_FF), f32),
                                     jax.ShapeDtypeStruct((1, D_FF), f32), jax.ShapeDtypeStruct((1, D_FF), f32)],
                          compiler_params=_cparams(2))(u3, u3, da3, w, w, b, b)


def _ssd_decay(dtb, bias_row, alog_row):
    lane = lax.broadcasted_iota(jnp.int32, (1, 128), 1)
    hmask = (lane >= DT_LANE) & (lane < DT_LANE + SSD_HEADS)
    dt = jnp.where(hmask, jax.nn.softplus(dtb + bias_row), 0.0)
    a = dt * jnp.where(hmask, -jnp.exp(alog_row), 0.0)
    r = lax.broadcasted_iota(jnp.int32, (CHUNK, CHUNK), 0)
    c = lax.broadcasted_iota(jnp.int32, (CHUNK, CHUNK), 1)
    cs = jnp.dot((r >= c).astype(f32), a, precision=HIGHEST, preferred_element_type=f32)
    return dt, cs


def _expand(xt):
    return jnp.concatenate([jnp.broadcast_to(xt[DT_LANE + h:DT_LANE + h + 1, :], (SSD_HD, xt.shape[1]))
                            for h in range(SSD_HEADS)], axis=0)


_NT = (((1,), (1,)), ((), ()))
_TN = (((0,), (0,)), ((), ()))
GH = SSD_HEADS // 2
GR = GH * SSD_HD


def ssd_fwd(xc3, p3, bias_row, alog_row, dcol):
    B, S, _ = xc3.shape
    nc = S // CHUNK

    def body(xs_ref, bc_ref, dtb_ref, bias_ref, alog_ref, dcol_ref, y_ref, st_ref, state, yT):
        @pl.when(pl.program_id(1) == 0)
        def _():
            state[...] = jnp.zeros_like(state)

        dt, cs = _ssd_decay(dtb_ref[0], bias_ref[...], alog_ref[...])
        csT = cs.T
        eT = jnp.exp(csT)
        decX = _expand(jnp.exp(csT[:, CHUNK - 1:CHUNK] - csT))
        eX = _expand(eT)
        elastX = eX[:, CHUNK - 1:CHUNK]
        xsT = xs_ref[0].T
        uT = xsT * _expand(dt.T)
        bc = bc_ref[0]
        st_ref[0, 0] = state[...]
        srow = lax.broadcasted_iota(jnp.int32, (CHUNK, CHUNK), 0)
        lcol = lax.broadcasted_iota(jnp.int32, (CHUNK, CHUNK), 1)
        for g in range(2):
            Bg = bc[:, g * SSD_N:(g + 1) * SSD_N].astype(bf16)
            Cg = bc[:, (2 + g) * SSD_N:(3 + g) * SSD_N].astype(bf16)
            GT = lax.dot_general(Bg, Cg, _NT, preferred_element_type=f32)
            rows = slice(g * GR, (g + 1) * GR)
            Sg = state[rows]
            yoffT = lax.dot_general(Sg.astype(bf16), Cg, _NT, preferred_element_type=f32) * eX[rows]
            state[rows] = Sg * elastX[rows] + jnp.dot((uT[rows] * decX[rows]).astype(bf16), Bg, preferred_element_type=f32)
            for k in range(GH):
                h = g * GH + k
                hr = slice(h * SSD_HD, (h + 1) * SSD_HD)
                seg = csT[DT_LANE + h:DT_LANE + h + 1, :] - cs[:, DT_LANE + h:DT_LANE + h + 1]
                LT = jnp.where(lcol >= srow, jnp.exp(jnp.minimum(seg, 0.0)), 0.0)
                yT[hr] = (jnp.dot(uT[hr].astype(bf16), (GT * LT).astype(bf16), preferred_element_type=f32)
                          + yoffT[k * SSD_HD:(k + 1) * SSD_HD] + dcol_ref[hr] * xsT[hr])
        y_ref[0] = yT[...].T

    return pl.pallas_call(body, name="ssd_fwd", grid=(B, nc),
                          in_specs=[pl.BlockSpec((1, CHUNK, D_SSD), lambda b, c: (b, c, 0)),
                                    pl.BlockSpec((1, CHUNK, 512), lambda b, c: (b, c, 2)),
                                    pl.BlockSpec((1, CHUNK, 128), lambda b, c: (b, c, OFF_KRDT // 128)),
                                    _const(1, 128), _const(1, 128), _const(D_SSD, 1)],
                          out_specs=[pl.BlockSpec((1, CHUNK, D_SSD), lambda b, c: (b, c, 0)),
                                     pl.BlockSpec((1, 1, D_SSD, SSD_N), lambda b, c: (b, c, 0, 0))],
                          out_shape=[jax.ShapeDtypeStruct((B, S, D_SSD), f32), jax.ShapeDtypeStruct((B, nc, D_SSD, SSD_N), f32)],
                          scratch_shapes=[pltpu.VMEM((D_SSD, SSD_N), f32), pltpu.VMEM((D_SSD, CHUNK), f32)],
                          compiler_params=_cparams(2))(xc3, xc3, p3, bias_row, alog_row, dcol)


def ssd_bwd(xc3, p3, dy3, states, bias_row, alog_row, bias_col, alog_col, dcol, comm=None):
    B, S, _ = xc3.shape
    nc = S // CHUNK

    def body(xs_ref, bc_ref, dtb_ref, dy_ref, st_ref, bias_ref, alog_ref, biasc_ref, alogc_ref, dcol_ref,
             dxc_ref, ddt_ref, dalog_ref, dd_ref, dbias_ref, dS, dUT, accA, accD, accB, dcs_diag):
        @pl.when(pl.program_id(1) == 0)
        def _():
            dS[...] = jnp.zeros_like(dS)

        @pl.when(_first())
        def _():
            accA[...] = jnp.zeros_like(accA)
            accD[...] = jnp.zeros_like(accD)
            accB[...] = jnp.zeros_like(accB)

        dtb = dtb_ref[0]
        dt, cs = _ssd_decay(dtb, bias_ref[...], alog_ref[...])
        dtT, csT = dt.T, cs.T
        decX = _expand(jnp.exp(csT[:, CHUNK - 1:CHUNK] - csT))
        eX = _expand(jnp.exp(csT))
        dtX = _expand(dtT)
        elastX = eX[:, CHUNK - 1:CHUNK]
        xsT = xs_ref[0].T
        uT = xsT * dtX
        dYT = dy_ref[0].T
        bc = bc_ref[0]
        lrow = lax.broadcasted_iota(jnp.int32, (CHUNK, CHUNK), 0)
        scol = lax.broadcasted_iota(jnp.int32, (CHUNK, CHUNK), 1)
        dcs_diag[...] = jnp.zeros_like(dcs_diag)
        rs_cols = jnp.zeros((CHUNK, 128), f32)
        vparts, zparts = [], []
        for g in range(2):
            Bf = bc[:, g * SSD_N:(g + 1) * SSD_N]
            Bg = Bf.astype(bf16)
            Cg = bc[:, (2 + g) * SSD_N:(3 + g) * SSD_N].astype(bf16)
            G = lax.dot_general(Cg, Bg, _NT, preferred_element_type=f32)
            BgT = Bf.T.astype(bf16)
            rows = slice(g * GR, (g + 1) * GR)
            dSg = dS[rows]
            Sg = st_ref[0, 0, rows, :]
            dUst = jnp.dot(dSg.astype(bf16), BgT, preferred_element_type=f32) * decX[rows]
            yoffT = lax.dot_general(Sg.astype(bf16), Cg, _NT, preferred_element_type=f32) * eX[rows]
            zparts.append(dYT[rows] * yoffT - dUst * uT[rows])
            dG = jnp.zeros((CHUNK, CHUNK), f32)
            for k in range(GH):
                h = g * GH + k
                hr = slice(h * SSD_HD, (h + 1) * SSD_HD)
                seg = cs[:, DT_LANE + h:DT_LANE + h + 1] - csT[DT_LANE + h:DT_LANE + h + 1, :]
                L = jnp.where(lrow >= scol, jnp.exp(jnp.minimum(seg, 0.0)), 0.0)
                M = G * L
                dYh = dYT[hr].astype(bf16)
                dUT[hr] = jnp.dot(dYh, M.astype(bf16), preferred_element_type=f32) + dUst[k * SSD_HD:(k + 1) * SSD_HD]
                dM = lax.dot_general(dYh, uT[hr].astype(bf16), _TN, preferred_element_type=f32)
                dG = dG + dM * L
                Wm = dM * M
                rs_cols = jnp.where(scol == DT_LANE + h, jnp.sum(Wm, axis=1, keepdims=True), rs_cols)
                dcs_diag[DT_LANE + h:DT_LANE + h + 1, :] = -_rowsum(Wm)
            dGb = dG.astype(bf16)
            dYe = (dYT[rows] * eX[rows]).astype(bf16)
            ude = (uT[rows] * decX[rows]).astype(bf16)
            dC = jnp.dot(dGb, Bg, preferred_element_type=f32) + lax.dot_general(dYe, Sg.astype(bf16), _TN, preferred_element_type=f32)
            dB = (lax.dot_general(dGb, Cg, _TN, preferred_element_type=f32)
                  + lax.dot_general(ude, dSg.astype(bf16), _TN, preferred_element_type=f32))
            dxc_ref[0, :, D_SSD + g * SSD_N:D_SSD + (g + 1) * SSD_N] = dB
            dxc_ref[0, :, D_SSD + (2 + g) * SSD_N:D_SSD + (3 + g) * SSD_N] = dC
            vparts.append(elastX[rows] * jnp.sum(dSg * Sg, axis=1, keepdims=True)
                          + jnp.sum(dUst * uT[rows], axis=1, keepdims=True))
            dS[rows] = elastX[rows] * dSg + jnp.dot(dYe, Cg, preferred_element_type=f32)
        dU = dUT[...]
        dcv = dcol_ref[...]
        dxc_ref[0, :, 0:D_SSD] = (dtX * dU + dcv * dYT).T
        lane = lax.broadcasted_iota(jnp.int32, (D_SSD, CHUNK), 1)
        Z = jnp.concatenate(zparts, axis=0) + jnp.where(lane == CHUNK - 1, jnp.concatenate(vparts, axis=0), 0.0)
        hr_ = lax.broadcasted_iota(jnp.int32, (128, D_SSD), 0)
        hc_ = lax.broadcasted_iota(jnp.int32, (128, D_SSD), 1)
        hsel = (hr_ - DT_LANE == jnp.right_shift(hc_, 6)).astype(bf16)
        summands = jnp.concatenate([Z, dU * xsT, dYT * xsT], axis=1)
        hi = summands.astype(bf16)
        lo = (summands - hi.astype(f32)).astype(bf16)
        red = jnp.dot(hsel, hi, preferred_element_type=f32) + jnp.dot(hsel, lo, preferred_element_type=f32)
        dcsT = red[:, 0:CHUNK] + dcs_diag[...] + rs_cols.T
        daT = jnp.dot(dcsT, (lrow >= scol).astype(f32), precision=HIGHEST, preferred_element_type=f32)
        rowi = lax.broadcasted_iota(jnp.int32, (128, 1), 0)
        hmask = (rowi >= DT_LANE) & (rowi < DT_LANE + SSD_HEADS)
        a_col = jnp.where(hmask, -jnp.exp(alogc_ref[...]), 0.0)
        ddtT = red[:, CHUNK:2 * CHUNK] + a_col * daT
        ddt_rawT = jnp.where(hmask, ddtT * jax.nn.sigmoid(dtb.T + biasc_ref[...]), 0.0)
        ddt_ref[0] = ddt_rawT.T
        accA[...] += daT * dtT
        accD[...] += red[:, 2 * CHUNK:3 * CHUNK]
        accB[...] += ddt_rawT

        @pl.when((pl.program_id(0) == B - 1) & (pl.program_id(1) == nc - 1))
        def _():
            dalog_ref[...] = jnp.broadcast_to(jnp.sum(accA[...], axis=1, keepdims=True) * a_col, (128, 128))
            dd_ref[...] = jnp.broadcast_to(jnp.sum(accD[...], axis=1, keepdims=True), (128, 128))
            dbias_ref[...] = jnp.broadcast_to(jnp.sum(accB[...], axis=1, keepdims=True), (128, 128))

    def rev(width, cb):
        return pl.BlockSpec((1, CHUNK, width), lambda b, c: (b, nc - 1 - c, cb))

    acc_spec = pl.BlockSpec((128, 128), lambda b, c: (0, 0))
    acc_shape = jax.ShapeDtypeStruct((128, 128), f32)
    body, c_args, c_in, c_out, c_shapes, c_sems = _fuse_exchange(body, comm, 10, 5, 6, (B, nc))
    res = pl.pallas_call(body, name="ssd_bwd_x" if comm else "ssd_bwd", grid=(B, nc),
                         in_specs=[rev(D_SSD, 0), rev(512, 2), rev(128, OFF_KRDT // 128), rev(D_SSD, 0),
                                   pl.BlockSpec((1, 1, D_SSD, SSD_N), lambda b, c: (b, nc - 1 - c, 0, 0)),
                                   _const(1, 128), _const(1, 128), _const(128, 1), _const(128, 1), _const(D_SSD, 1)] + c_in,
                         out_specs=[rev(D_XBC, 0), rev(128, 0), acc_spec, acc_spec, acc_spec] + c_out,
                         out_shape=[jax.ShapeDtypeStruct((B, S, D_XBC), f32), jax.ShapeDtypeStruct((B, S, 128), f32),
                                    acc_shape, acc_shape, acc_shape] + c_shapes,
                         scratch_shapes=[pltpu.VMEM((D_SSD, SSD_N), f32), pltpu.VMEM((D_SSD, CHUNK), f32),
                                         pltpu.VMEM((128, 128), f32), pltpu.VMEM((128, 128), f32), pltpu.VMEM((128, 128), f32),
                                         pltpu.VMEM((128, 128), f32)] + c_sems,
                         compiler_params=_cparams(2))(xc3, xc3, p3, dy3, states, bias_row, alog_row, bias_col, alog_col, dcol, *c_args)
    return (*res[:5], list(res[5:]))


ATT_SCALE = float(QK) ** -0.5
LOG2E = 1.4426950408889634
LN2 = 0.6931471805599453
Q_FOLD = ATT_SCALE * LOG2E
NEG = -1e30
HP = 2


def _att_block(S):
    return _pick(S, (512, 256, 128))


def attn_fwd(q3, k3, v3, comm=None):
    B, S, _ = q3.shape
    bq = _att_block(S)
    nq = S // bq

    def body(q_ref, k_ref, v_ref, o_ref, lse_ref, m_s, l_s, acc):
        i, j = pl.program_id(2), pl.program_id(3)

        @pl.when(j == 0)
        def _():
            m_s[...] = jnp.full_like(m_s, NEG)
            l_s[...] = jnp.zeros_like(l_s)
            acc[...] = jnp.zeros_like(acc)

        def step(masked):
            for t in range(HP):
                qk = slice(t * 256, (t + 1) * 256)
                st = lax.dot_general(k_ref[0, :, qk], q_ref[0, :, qk], _NT, preferred_element_type=f32)
                if masked:
                    r = lax.broadcasted_iota(jnp.int32, (bq, bq), 0)
                    c = lax.broadcasted_iota(jnp.int32, (bq, bq), 1)
                    st = jnp.where(c >= r, st, NEG)
                m_old = m_s[t]
                m_new = jnp.maximum(m_old, jnp.max(st, axis=0, keepdims=True))
                alpha = jnp.exp2(m_old - m_new)
                pt = jnp.exp2(st - m_new)
                l_s[t] = alpha * l_s[t] + jnp.sum(pt, axis=0, keepdims=True)
                acc[t] = alpha * acc[t] + lax.dot_general(v_ref[0, :, t * VD:(t + 1) * VD], pt.astype(bf16), _TN,
                                                          preferred_element_type=f32)
                m_s[t] = m_new

        @pl.when(j < i)
        def _():
            step(False)

        @pl.when(j == i)
        def _():
            step(True)
            for t in range(HP):
                o_ref[0, :, t * VD:(t + 1) * VD] = (acc[t] / l_s[t]).T
                lse_ref[0, t] = m_s[t] + jnp.log2(l_s[t])

    grid = (B, MLA_H // HP, nq, nq)
    body, c_args, c_in, c_out, c_shapes, c_sems = _fuse_exchange(body, comm, 3, 2, 3, grid)
    res = pl.pallas_call(body, name="attn_fwd_x" if comm else "attn_fwd", grid=grid,
                         in_specs=[pl.BlockSpec((1, bq, HP * 256), lambda b, h, i, j: (b, i, h)),
                                   pl.BlockSpec((1, bq, HP * 256), lambda b, h, i, j: (b, jnp.minimum(j, i), h)),
                                   pl.BlockSpec((1, bq, HP * VD), lambda b, h, i, j: (b, jnp.minimum(j, i), h))] + c_in,
                         out_specs=[pl.BlockSpec((1, bq, HP * VD), lambda b, h, i, j: (b, i, h)),
                                    pl.BlockSpec((1, HP, 1, bq), lambda b, h, i, j: (b, h, 0, i))] + c_out,
                         out_shape=[jax.ShapeDtypeStruct((B, S, MLA_H * VD), f32), jax.ShapeDtypeStruct((B, MLA_H, 1, S), f32)] + c_shapes,
                         scratch_shapes=[pltpu.VMEM((HP, 1, bq), f32), pltpu.VMEM((HP, 1, bq), f32), pltpu.VMEM((HP, VD, bq), f32)] + c_sems,
                         compiler_params=_cparams(4))(q3, k3, v3, *c_args)
    return res[0], res[1], list(res[2:])


def attn_delta(o3, do3):
    B, S, _ = o3.shape
    ts = _pick(S, (512, 256, 128))

    def body(o_ref, do_ref, d_ref):
        for h in range(MLA_H):
            vs = slice(h * VD, (h + 1) * VD)
            d_ref[0, h] = jnp.sum(o_ref[0, :, vs] * do_ref[0, :, vs], axis=-1, keepdims=True)

    return _row_call(body, "attn_delta", B, S, ts, [_tok(ts, MLA_H * VD), _tok(ts, MLA_H * VD)],
                     pl.BlockSpec((1, MLA_H, ts, 1), lambda b, s: (b, 0, s, 0)),
                     jax.ShapeDtypeStruct((B, MLA_H, S, 1), f32))(o3, do3)


def attn_bwd(q3, k3, v3, do3, lse_row, delta_row, comm=None):
    B, S, _ = q3.shape
    bq = _att_block(S)
    nq = S // bq

    def body(q_ref, k_ref, v_ref, do_ref, lse_ref, dl_ref, dk_ref, dv_ref, dq_hbm, dk_acc, dv_acc, dq_scr, dq_sem):
        b, hp, j, i = pl.program_id(0), pl.program_id(1), pl.program_id(2), pl.program_id(3)
        rows = pl.ds(pl.multiple_of(i * bq, bq), bq)

        @pl.when((j == 0) & (i == 0))
        def _():
            dq_scr[...] = jnp.zeros_like(dq_scr)

        @pl.when(i == 0)
        def _():
            dk_acc[...] = jnp.zeros_like(dk_acc)
            dv_acc[...] = jnp.zeros_like(dv_acc)

        def step(masked):
            for t in range(HP):
                qk, vs = slice(t * 256, (t + 1) * 256), slice(t * VD, (t + 1) * VD)
                q, k = q_ref[0, :, qk], k_ref[0, :, qk]
                do = do_ref[0, :, vs].astype(bf16)
                pt = jnp.exp2(lax.dot_general(k, q, _NT, preferred_element_type=f32) - lse_ref[0, t])
                if masked:
                    r = lax.broadcasted_iota(jnp.int32, (bq, bq), 0)
                    c = lax.broadcasted_iota(jnp.int32, (bq, bq), 1)
                    pt = jnp.where(c >= r, pt, 0.0)
                dv_acc[t] += jnp.dot(pt.astype(bf16), do, preferred_element_type=f32)
                dpt = lax.dot_general(v_ref[0, :, vs], do, _NT, preferred_element_type=f32)
                dst = (pt * (dpt - dl_ref[0, t])).astype(bf16)
                dk_acc[t] += jnp.dot(dst, q, preferred_element_type=f32)
                dq_scr[t, rows, :] += lax.dot_general(dst, k, _TN, preferred_element_type=f32)

        @pl.when(i > j)
        def _():
            step(False)

        @pl.when(i == j)
        def _():
            step(True)
            for t in range(HP):
                dq_scr[t, rows, :] = dq_scr[t, rows, :] * LN2
                cp = pltpu.make_async_copy(dq_scr.at[t, rows, :],
                                           dq_hbm.at[b, rows, pl.ds(pl.multiple_of((hp * HP + t) * 256, 256), 256)], dq_sem.at[t])
                cp.start()
                cp.wait()

        @pl.when(i == nq - 1)
        def _():
            for t in range(HP):
                dk_ref[0, :, t * 256:(t + 1) * 256] = dk_acc[t] * LN2
                dv_ref[0, :, t * VD:(t + 1) * VD] = dv_acc[t]

    kspec = pl.BlockSpec((1, bq, HP * 256), lambda b, h, j, i: (b, j, h))
    vspec = pl.BlockSpec((1, bq, HP * VD), lambda b, h, j, i: (b, j, h))
    rspec = pl.BlockSpec((1, HP, 1, bq), lambda b, h, j, i: (b, h, 0, jnp.maximum(i, j)))
    grid = (B, MLA_H // HP, nq, nq)
    body, c_args, c_in, c_out, c_shapes, c_sems = _fuse_exchange(body, comm, 6, 3, 4, grid)
    res = pl.pallas_call(body, name="attn_bwd_x" if comm else "attn_bwd", grid=grid,
                         in_specs=[pl.BlockSpec((1, bq, HP * 256), lambda b, h, j, i: (b, jnp.maximum(i, j), h)), kspec, vspec,
                                   pl.BlockSpec((1, bq, HP * VD), lambda b, h, j, i: (b, jnp.maximum(i, j), h)), rspec, rspec] + c_in,
                         out_specs=[kspec, vspec, pl.BlockSpec(memory_space=pltpu.HBM)] + c_out,
                         out_shape=[jax.ShapeDtypeStruct((B, S, MLA_H * 256), f32), jax.ShapeDtypeStruct((B, S, MLA_H * VD), f32),
                                    jax.ShapeDtypeStruct((B, S, MLA_H * 256), f32)] + c_shapes,
                         scratch_shapes=[pltpu.VMEM((HP, bq, 256), f32), pltpu.VMEM((HP, bq, VD), f32), pltpu.VMEM((HP, S, 256), f32),
                                         pltpu.SemaphoreType.DMA((HP,))] + c_sems,
                         compiler_params=_cparams(4))(q3, k3, v3, do3, lse_row, delta_row, *c_args)
    return res[2], res[0], res[1], list(res[3:])


def ada_fwd(c_all, w, b):
    n = w.shape[1]

    def body(c_ref, w_ref, b_ref, o_ref):
        o_ref[...] = jnp.dot(_silu(c_ref[...]).astype(bf16), w_ref[...].astype(bf16), preferred_element_type=f32) + b_ref[...]

    return pl.pallas_call(body, name="ada_fwd", out_shape=jax.ShapeDtypeStruct((c_all.shape[0], n), f32),
                          compiler_params=pltpu.CompilerParams(vmem_limit_bytes=VMEM_LIMIT))(c_all, w, b)


def ada_bwd(c_all, dmod):
    n = dmod.shape[1]

    def body(c_ref, d_ref, o_ref):
        o_ref[...] = lax.dot_general(_silu(c_ref[...]).astype(bf16), d_ref[...].astype(bf16), _TN, preferred_element_type=f32)

    return pl.pallas_call(body, name="ada_bwd", out_shape=jax.ShapeDtypeStruct((c_all.shape[1], n), f32),
                          compiler_params=pltpu.CompilerParams(vmem_limit_bytes=VMEM_LIMIT))(c_all, dmod)


def sum_leading(x, name):
    n, R, _ = x.shape
    tr = _pick(R, (512, 256, 128, 64, 32, 16, 8))

    def body(x_ref, o_ref):
        acc = x_ref[0].astype(f32)
        for k in range(1, n):
            acc = acc + x_ref[k].astype(f32)
        o_ref[...] = acc

    return pl.pallas_call(body, name=name, grid=(R // tr,), in_specs=[pl.BlockSpec((n, tr, 128), lambda i: (0, i, 0))],
                          out_specs=pl.BlockSpec((tr, 128), lambda i: (i, 0)), out_shape=jax.ShapeDtypeStruct((R, 128), f32),
                          compiler_params=_cparams(1))(x)


def _adamw_body(w_ref, g_ref, m_ref, v_ref, d_ref, mo_ref, vo_ref):
    gv = g_ref[...]
    mn = ADAM_B1 * m_ref[...] + (1.0 - ADAM_B1) * gv
    vn = ADAM_B2 * v_ref[...] + (1.0 - ADAM_B2) * jnp.square(gv)
    m_hat = mn / (1.0 - ADAM_B1 ** ADAM_STEP)
    v_hat = vn / (1.0 - ADAM_B2 ** ADAM_STEP)
    d_ref[...] = -ADAM_LR * (m_hat / (jnp.sqrt(v_hat) + ADAM_EPS) + ADAM_WD * w_ref[...])
    mo_ref[...] = mn
    vo_ref[...] = vn


def adamw(w, g, m, v):
    R = w.shape[0]
    tr = _pick(R, (512, 256, 128, 64, 32, 16, 8))
    spec = pl.BlockSpec((tr, 128), lambda i: (i, 0))
    shp = jax.ShapeDtypeStruct((R, 128), f32)
    return pl.pallas_call(functools.partial(_adamw_body), name="adamw", grid=(R // tr,), in_specs=[spec] * 4,
                          out_specs=[spec] * 3, out_shape=[shp] * 3, compiler_params=_cparams(1))(w, g, m, v)


def _row_tile(a):
    return _pick(a, (256, 128, 64, 32, 16, 8)) if a % 8 == 0 else a


def adamw_nd(w, g, m, v):
    L, a, b = w.shape
    ta = _row_tile(a)
    spec = pl.BlockSpec((1, ta, b), lambda l, i: (l, i, 0))
    shp = jax.ShapeDtypeStruct((L, a, b), f32)
    return pl.pallas_call(functools.partial(_adamw_body), name="adamw_nd", grid=(L, a // ta), in_specs=[spec] * 4,
                          out_specs=[spec] * 3, out_shape=[shp] * 3, compiler_params=_cparams(2))(w, g, m, v)


def sum_slots(x):
    n, L, a, b = x.shape
    ta = _row_tile(a)

    def body(x_ref, o_ref):
        acc = x_ref[0].astype(f32)
        for k in range(1, n):
            acc = acc + x_ref[k].astype(f32)
        o_ref[...] = acc

    return pl.pallas_call(body, name="sum_slots", grid=(L, a // ta),
                          in_specs=[pl.BlockSpec((n, 1, ta, b), lambda l, i: (0, l, i, 0))],
                          out_specs=pl.BlockSpec((1, ta, b), lambda l, i: (l, i, 0)),
                          out_shape=jax.ShapeDtypeStruct((L, a, b), f32), compiler_params=_cparams(2))(x)


def _exchange_copies(ins, outs, sems, scatter):
    send_sems, recv_sems, local_sems = sems
    x, y, c = lax.axis_index("x"), lax.axis_index("y"), lax.axis_index("c")
    me = 4 * x + 2 * y + c
    locals_, sends, recvs = [], [], []
    for a in range(len(ins)):
        locals_.append(pltpu.make_async_copy(ins[a].at[me] if scatter[a] else ins[a], outs[a].at[me], local_sems.at[a]))
        for k in range(N_DEV - 1):
            px = 1 - x if (k + 1) & 4 else x
            py = 1 - y if (k + 1) & 2 else y
            pc = 1 - c if (k + 1) & 1 else c
            pid = 4 * px + 2 * py + pc
            src = ins[a].at[pid] if scatter[a] else ins[a]
            for slot, group in ((me, sends), (pid, recvs)):
                group.append(pltpu.make_async_remote_copy(src_ref=src, dst_ref=outs[a].at[slot], send_sem=send_sems.at[a, k],
                                                          recv_sem=recv_sems.at[a, k], device_id=(px, py, pc),
                                                          device_id_type=pl.DeviceIdType.MESH))
    return locals_, sends, recvs


def _exchange_start(ins, outs, sems, scatter):
    locals_, sends, _ = _exchange_copies(ins, outs, sems, scatter)
    for cp in locals_ + sends:
        cp.start()


def _exchange_wait(ins, outs, sems, scatter):
    locals_, sends, recvs = _exchange_copies(ins, outs, sems, scatter)
    for cp in recvs:
        cp.wait_recv()
    for cp in sends:
        cp.wait_send()
    for cp in locals_:
        cp.wait()


def _exchange_shapes(arrays, scatter):
    return [jax.ShapeDtypeStruct((N_DEV,) + tuple(a.shape[1:] if s else a.shape), a.dtype) for a, s in zip(arrays, scatter)]


def _flags(scatter, n):
    return [scatter] * n if isinstance(scatter, bool) else list(scatter)


def _exchange_sems(n):
    return [pltpu.SemaphoreType.DMA((n, N_DEV - 1)), pltpu.SemaphoreType.DMA((n, N_DEV - 1)), pltpu.SemaphoreType.DMA((n,))]


def _fuse_exchange(core, comm, n_in, n_out, n_scr, grid):
    if comm is None:
        return core, [], [], [], [], []
    arrays, scatter = comm
    n = len(arrays)
    scatter = _flags(scatter, n)

    def body(*refs):
        a, b, c = n_in + n, n_in + n + n_out, n_in + 2 * n + n_out
        cin, cout, sems = refs[n_in:a], refs[b:c], refs[c + n_scr:]
        ids = [pl.program_id(d) for d in range(len(grid))]
        first = functools.reduce(lambda p, q: p & q, [i == 0 for i in ids])
        last = functools.reduce(lambda p, q: p & q, [i == g - 1 for i, g in zip(ids, grid)])

        @pl.when(first)
        def _():
            _exchange_start(cin, cout, sems, scatter)

        core(*refs[:n_in], *refs[a:b], *refs[c:c + n_scr])

        @pl.when(last)
        def _():
            _exchange_wait(cin, cout, sems, scatter)

    hbm = pl.BlockSpec(memory_space=pltpu.HBM)
    return body, list(arrays), [hbm] * n, [hbm] * n, _exchange_shapes(arrays, scatter), _exchange_sems(n)


def exchange(arrays, scatter, name):
    n = len(arrays)
    scatter = _flags(scatter, n)

    def body(*refs):
        ins, outs, sems = refs[:n], refs[n:2 * n], refs[2 * n:]
        _exchange_start(ins, outs, sems, scatter)
        _exchange_wait(ins, outs, sems, scatter)

    hbm = pl.BlockSpec(memory_space=pltpu.HBM)
    return pl.pallas_call(body, name=name, in_specs=[hbm] * n, out_specs=[hbm] * n,
                          out_shape=_exchange_shapes(arrays, scatter), scratch_shapes=_exchange_sems(n))(*arrays)


BIG = (("w_in", "col"), ("conv_w", "col"), ("w_uq", "col"), ("w_ukv", "col"), ("w_out", "row"), ("w_up", "col"),
       ("conv_ff_w", "col"), ("w_down", "row"))
SMALL = ("b_ada", "norm_mix", "conv_b", "dt_bias", "a_log", "d_skip", "ssd_norm", "q_norm", "kv_norm", "attn_norm",
         "norm_mlp", "conv_ff_b", "final_norm")
CONVS = ("conv_w", "conv_ff_w")
PACK_ALIGN = 2048


def _padded(n):
    return -(-n // PACK_ALIGN) * PACK_ALIGN


def _flat_pad(a):
    f = a.reshape(-1)
    return jnp.pad(f, (0, _padded(f.shape[0]) - f.shape[0]))


PACK_ROWS = 512


def pack(arrs):
    f = jnp.concatenate([_flat_pad(a) for a in arrs])
    n = PACK_ROWS * 128
    return jnp.pad(f, (0, -(-f.shape[0] // n) * n - f.shape[0])).reshape(-1, 128)


def unpack(flat, shapes):
    f = flat.reshape(-1)
    out, off = [], 0
    for s in shapes:
        n = int(np.prod(s))
        out.append(f[off:off + n].reshape(s))
        off += _padded(n)
    return out


def shards_to_full(g, kind):
    _, a, b = g.shape
    if kind == "col":
        return g.transpose(1, 0, 2).reshape(a, N_DEV * b)
    return g.reshape(N_DEV * a, b)


def full_to_shards(full, kind):
    if kind == "col":
        a, nb = full.shape
        return full.reshape(a, N_DEV, nb // N_DEV).transpose(1, 0, 2)
    na, b = full.shape
    return full.reshape(N_DEV, na // N_DEV, b)


def w_in_layout(w):
    z = lambda n: jnp.zeros(w.shape[:-1] + (n,), w.dtype)
    return jnp.concatenate([w[..., :2560], w[..., 2576:2960], z(128), w[..., 2960:3216], w[..., 3216:3280],
                            w[..., 2560:2576], z(48)], axis=-1)


def w_in_unlayout(g):
    return jnp.concatenate([g[..., :2560], g[..., 3392:3408], g[..., 2560:2944], g[..., 3072:3328], g[..., 3328:3392]], axis=-1)


def w_uq_layout(w):
    return jnp.pad(w.reshape(Q_RANK, MLA_H, QK), ((0, 0), (0, 0), (0, 256 - QK))).reshape(Q_RANK, MLA_H * 256)


def w_uq_unlayout(g):
    return g.reshape(Q_RANK, MLA_H, 256)[:, :, :QK].reshape(Q_RANK, MLA_H * QK)


def w_ukv_layout(w):
    return w.reshape(KV_RANK, MLA_H, 2, 128).transpose(0, 2, 1, 3).reshape(KV_RANK, 2 * MLA_H * 128)


def w_ukv_unlayout(g):
    return g.reshape(KV_RANK, 2, MLA_H, 128).transpose(0, 2, 1, 3).reshape(KV_RANK, 2 * MLA_H * 128)


LAYOUTS = {"w_in": (w_in_layout, w_in_unlayout), "w_uq": (w_uq_layout, w_uq_unlayout), "w_ukv": (w_ukv_layout, w_ukv_unlayout)}
FIRST, REST = BIG[:4], BIG[4:]


def layer_weights(gathered, entries):
    full = {n: shards_to_full(g, kind) for (n, kind), g in zip(entries, gathered)}
    return {n: LAYOUTS[n][0](w) if n in LAYOUTS else w for n, w in full.items()}


def layer_grad_slices(g, entries):
    return [full_to_shards(LAYOUTS[n][1](g[n]) if n in LAYOUTS else g[n], kind).astype(bf16) for n, kind in entries]


def _head_row(v):
    return jnp.zeros((1, 128), f32).at[0, DT_LANE:DT_LANE + SSD_HEADS].set(v)


def layer_fwd(x3, mod, W, P, l, cosf, sinf, comm=None, late=None, comm_up=None):
    B, S, _ = x3.shape
    T = B * S
    sv = {}
    h = normmod_fwd(x3, mod, P["norm_mix"][l][None], 0, 1)
    p = mm(h.reshape(T, D), W["w_in"], "nn", "mm_in")
    p3 = p.reshape(B, S, IN_COLS)
    bias_row, alog_row = _head_row(P["dt_bias"][l]), _head_row(P["a_log"][l])
    dcol = jnp.repeat(P["d_skip"][l], SSD_HD)[:, None]
    xc3 = conv_ssd_fwd(p3, W["conv_w"], P["conv_b"][l][None])
    yc3, states = ssd_fwd(xc3, p3, bias_row, alog_row, dcol)
    y_ssd = ssd_out_fwd(yc3, p3, P["ssd_norm"][l][None])
    cqn = rms_fwd(p3, 512, OFF_CQ // 512, Q_RANK, P["q_norm"][l][None], "rms_q_fwd")
    ckvn = rms_fwd(p3, KV_RANK, OFF_CKV // KV_RANK, KV_RANK, P["kv_norm"][l][None], "rms_kv_fwd")
    qraw = mm(cqn.reshape(T, Q_RANK), W["w_uq"], "nn", "mm_uq")
    kvraw = mm(ckvn.reshape(T, KV_RANK), W["w_ukv"], "nn", "mm_ukv")
    q3 = qprep_fwd(qraw.reshape(B, S, -1), cosf, sinf)
    k3, v3 = kprep_fwd(kvraw.reshape(B, S, -1), p3, cosf, sinf)
    o3, lse, comm_out = attn_fwd(q3, k3, v3, comm)
    if late is not None:
        W = dict(W, **late(comm_out))
    y_att = rms_fwd(o3, D, 0, D, P["attn_norm"][l][None], "rms_o_fwd")
    cat = (y_ssd.reshape(T, D), y_att.reshape(T, D))
    x1, y1 = mm(cat, W["w_out"], "nn", "mm_out", resid=x3.reshape(T, D), gate=mod[:, 2:3, :], seq=S)
    x13 = x1.reshape(B, S, D)
    h2 = normmod_fwd(x13, mod, P["norm_mlp"][l][None], 3, 4)
    u, up_out = mm(h2.reshape(T, D), W["w_up"], "nn", "mm_up", out_dtype=bf16, comm=comm_up), []
    if comm_up is not None:
        u, up_out = u
    u3 = u.reshape(B, S, 2 * D_FF)
    a = glu_fwd(u3, W["conv_ff_w"], P["conv_ff_b"][l][None])
    x2, y2 = mm(a.reshape(T, D_FF), W["w_down"], "nn", "mm_down", resid=x1, gate=mod[:, 5:6, :], seq=S)
    sv.update(x=x3, h=h, p3=p3, xc3=xc3, yc3=yc3, states=states, cqn=cqn, ckvn=ckvn, q3=q3, k3=k3, v3=v3, o3=o3, lse=lse,
              cat=cat, y1=y1, x1=x13, h2=h2, u3=u3, a=a, y2=y2, bias_row=bias_row, alog_row=alog_row, dcol=dcol)
    return x2.reshape(B, S, D), sv, comm_out, W, up_out


def layer_bwd(dx3, sv, mod, W, P, l, cosf, sinf, comm=None, send_rest=False):
    B, S, _ = dx3.shape
    T = B * S
    g = {}
    dy2, dg2 = gate_bwd(dx3, sv["y2"].reshape(B, S, D), mod, 5)
    dy2 = dy2.reshape(T, D)
    da = mm(dy2, W["w_down"], "nt", "mm_down_dx")
    g["w_down"] = mm(sv["a"].reshape(T, D_FF), dy2, "tn", "mm_down_dw")
    dug, duv, dwg, dwv, dbg, dbv = glu_bwd(sv["u3"], da.reshape(B, S, D_FF), W["conv_ff_w"], P["conv_ff_b"][l][None])
    g["conv_ff_w"] = jnp.concatenate([dwg, dwv], axis=1)
    g["conv_ff_b"] = jnp.concatenate([dbg, dbv], axis=1)[0]
    du = (dug.reshape(T, D_FF), duv.reshape(T, D_FF))
    dh2 = mm(du, W["w_up"], "nt", "mm_up_dx")
    g["w_up"] = jnp.concatenate([mm(sv["h2"].reshape(T, D), d, "tn", "mm_up_dw") for d in du], axis=1)
    dx1, dsh2, dsc2, dnm = normmod_bwd(sv["x1"], dh2.reshape(B, S, D), dx3, mod, P["norm_mlp"][l][None], 4)
    g["norm_mlp"] = dnm[0]
    dy1, dg1 = gate_bwd(dx1, sv["y1"].reshape(B, S, D), mod, 2)
    dy1 = dy1.reshape(T, D)
    dcat = mm(dy1, W["w_out"], "nt", "mm_out_dx")
    g["w_out"] = jnp.concatenate([mm(part, dy1, "tn", "mm_out_dw") for part in sv["cat"]], axis=0)
    dcat3 = dcat.reshape(B, S, 2 * D)
    do3, dan = rms_bwd(sv["o3"], D, 0, D, dcat3, 1, P["attn_norm"][l][None], f32, "rms_o_bwd")
    g["attn_norm"] = dan[0]
    delta = attn_delta(sv["o3"], do3)
    dq3, dk3, dv3, comm_out = attn_bwd(sv["q3"], sv["k3"], sv["v3"], do3, sv["lse"], delta.reshape(B, MLA_H, 1, S), comm)
    dqraw = qprep_bwd(dq3, cosf, sinf).reshape(T, -1)
    dcqn = mm(dqraw, W["w_uq"], "nt", "mm_uq_dx")
    g["w_uq"] = mm(sv["cqn"].reshape(T, Q_RANK), dqraw, "tn", "mm_uq_dw")
    dcq, dqn = rms_bwd(sv["p3"], 512, OFF_CQ // 512, Q_RANK, dcqn.reshape(B, S, Q_RANK), 0, P["q_norm"][l][None], bf16, "rms_q_bwd")
    g["q_norm"] = dqn[0]
    dyc3, dz, dsn = ssd_out_bwd(sv["yc3"], sv["p3"], dcat3, P["ssd_norm"][l][None])
    g["ssd_norm"] = dsn[0]
    bias_col, alog_col = sv["bias_row"].reshape(128, 1), sv["alog_row"].reshape(128, 1)
    comm_rest = (layer_grad_slices(g, REST), True) if send_rest else None
    dxc3, ddt3, dalog, dd, dbias, rest_out = ssd_bwd(sv["xc3"], sv["p3"], dyc3, sv["states"], sv["bias_row"], sv["alog_row"],
                                                     bias_col, alog_col, sv["dcol"], comm_rest)
    heads = slice(DT_LANE, DT_LANE + SSD_HEADS)
    g["a_log"], g["d_skip"], g["dt_bias"] = dalog[heads, 0], dd[heads, 0], dbias[heads, 0]
    dxbc, dcw, dcb = conv_ssd_bwd(sv["p3"], dxc3, W["conv_w"], P["conv_b"][l][None])
    g["conv_w"], g["conv_b"] = dcw, dcb[0]
    dkvraw, dkrdt = kprep_bwd(dk3, dv3, ddt3, cosf, sinf)
    dkvraw = dkvraw.reshape(T, -1)
    dckvn = mm(dkvraw, W["w_ukv"], "nt", "mm_ukv_dx")
    g["w_ukv"] = mm(sv["ckvn"].reshape(T, KV_RANK), dkvraw, "tn", "mm_ukv_dw")
    dckv, dkn = rms_bwd(sv["p3"], KV_RANK, OFF_CKV // KV_RANK, KV_RANK, dckvn.reshape(B, S, KV_RANK), 0, P["kv_norm"][l][None],
                        bf16, "rms_kv_bwd")
    g["kv_norm"] = dkn[0]
    dp = jnp.concatenate([dz, dxbc, dcq, jnp.zeros((B, S, 128), bf16), dckv, dkrdt], axis=-1).reshape(T, IN_COLS)
    dh = mm(dp, W["w_in"], "nt", "mm_in_dx")
    g["w_in"] = mm(sv["h"].reshape(T, D), dp, "tn", "mm_in_dw")
    dx0, dsh1, dsc1, dnx = normmod_bwd(sv["x"], dh.reshape(B, S, D), dx1, mod, P["norm_mix"][l][None], 1)
    g["norm_mix"] = dnx[0]
    dmod = jnp.concatenate([dsh1, dsc1, dg1, dsh2, dsc2, dg2], axis=1)
    return dx0, dmod, g, comm_out, rest_out


def kernel(x, c, positions, w_ada, b_ada, norm_mix, w_in, conv_w, conv_b, dt_bias, a_log, d_skip, ssd_norm, q_norm, w_uq, kv_norm, w_ukv, attn_norm, w_out, norm_mlp, w_up, conv_ff_w, conv_ff_b, w_down, final_norm, loss_target, m_w_ada, m_b_ada, m_norm_mix, m_w_in, m_conv_w, m_conv_b, m_dt_bias, m_a_log, m_d_skip, m_ssd_norm, m_q_norm, m_w_uq, m_kv_norm, m_w_ukv, m_attn_norm, m_w_out, m_norm_mlp, m_w_up, m_conv_ff_w, m_conv_ff_b, m_w_down, m_final_norm, v_w_ada, v_b_ada, v_norm_mix, v_w_in, v_conv_w, v_conv_b, v_dt_bias, v_a_log, v_d_skip, v_ssd_norm, v_q_norm, v_w_uq, v_kv_norm, v_w_ukv, v_attn_norm, v_w_out, v_norm_mlp, v_w_up, v_conv_ff_w, v_conv_ff_b, v_w_down, v_final_norm):
    given = dict(locals())
    B, S, _ = x.shape
    me = 4 * lax.axis_index("x") + 2 * lax.axis_index("y") + lax.axis_index("c")
    P = {n: given[n] for n in SMALL}

    def shards(l, entries):
        return [given[n][l] if n in CONVS else given[n][l].astype(bf16) for n, _ in entries]

    *gathered, c_all = exchange(shards(0, FIRST) + [c], False, "gather_weights")
    W = [layer_weights(gathered, FIRST), None]

    n_ada = w_ada.shape[2]
    c_all = c_all.reshape(N_DEV * B, D)
    b_sh = lax.dynamic_slice_in_dim(b_ada, me * n_ada, n_ada, axis=1)
    mod_sh = jnp.stack([ada_fwd(c_all, w_ada[l], b_sh[l][None]) for l in range(DEPTH)])
    (mod_g,) = exchange([mod_sh], False, "gather_mod")
    mod_mine = lax.dynamic_slice_in_dim(mod_g, me * B, B, axis=2)
    mods = mod_mine.transpose(1, 2, 0, 3).reshape(DEPTH, B, 6, D)

    inv_freq = jnp.asarray(1.0 / (ROPE_BASE ** (np.arange(0, ROPE, 2, dtype=np.float32) / ROPE)))
    ang = positions.astype(f32)[..., None] * inv_freq
    zeros = jnp.zeros((B, S, 128 - ROPE), f32)
    cosf = jnp.concatenate([jnp.cos(ang), jnp.cos(ang), zeros], axis=-1)
    sinf = jnp.concatenate([jnp.sin(ang), jnp.sin(ang), zeros], axis=-1)

    saved = [None] * DEPTH
    late = lambda got: layer_weights(got, REST)
    xl, saved[0], _, W[0], gathered = layer_fwd(x, mods[0], W[0], P, 0, cosf, sinf, comm=(shards(0, REST), False), late=late,
                                                comm_up=(shards(1, FIRST), False))
    xl, saved[1], _, W[1], _ = layer_fwd(xl, mods[1], layer_weights(gathered, FIRST), P, 1, cosf, sinf,
                                         comm=(shards(1, REST), False), late=late)
    dxl, d_final, loss_part = final_loss(xl, final_norm[None], loss_target)
    grads, dmods, recv = [None] * DEPTH, [None] * DEPTH, [None] * DEPTH
    dxl, dmods[1], grads[1], _, _ = layer_bwd(dxl, saved[1], mods[1], W[1], P, 1, cosf, sinf)
    grad_x, dmods[0], grads[0], recv[1], recv_rest = layer_bwd(dxl, saved[0], mods[0], W[0], P, 0, cosf, sinf,
                                                               comm=(layer_grad_slices(grads[1], BIG), True), send_rest=True)

    stack = lambda n: jnp.stack([grads[l][n] for l in range(DEPTH)])
    small_names = [n for n in SMALL if n not in ("b_ada", "final_norm")]
    partial = pack([stack(n) for n in small_names] + [d_final[0], loss_part[0]])
    dmod_all = jnp.stack(dmods)
    *recv_first, part_g, dmod_g = exchange(layer_grad_slices(grads[0], FIRST) + [partial, dmod_all],
                                           [True] * len(FIRST) + [False, False], "exchange_tail")
    recv[0] = recv_first + recv_rest
    big_g = [jnp.concatenate([sum_slots(recv[l][i][:, None]) for l in range(DEPTH)]) for i in range(len(BIG))]
    small_sum = sum_leading(part_g, "sum_partials")
    small_g = unpack(small_sum, [given[n].shape for n in small_names] + [(D,), (128,)])
    gsmall = dict(zip(small_names + ["final_norm"], small_g[:-1]))
    loss = small_g[-1][0]
    dmod_rows = dmod_g.transpose(0, 2, 1, 3, 4).reshape(N_DEV * B, DEPTH * 6 * D)
    gsmall["b_ada"] = sum_leading(dmod_rows.reshape(N_DEV * B, -1, 128), "sum_b_ada").reshape(DEPTH, 6 * D)
    dmod_cols = dmod_rows.reshape(N_DEV * B, DEPTH, N_DEV, n_ada)
    dmod_sh = lax.dynamic_slice_in_dim(dmod_cols, me, 1, axis=2)[:, :, 0, :]
    g_w_ada = jnp.stack([ada_bwd(c_all, dmod_sh[:, l, :]) for l in range(DEPTH)])

    res = {"grad": {}, "delta": {}, "new_m": {}, "new_v": {}}
    for n, gv in zip([n for n, _ in BIG] + ["w_ada"], big_g + [g_w_ada]):
        res["grad"][n] = gv
        res["delta"][n], res["new_m"][n], res["new_v"][n] = adamw_nd(given[n], gv, given["m_" + n], given["v_" + n])
    shapes = [given[n].shape for n in SMALL]
    flat = adamw(pack([given[n] for n in SMALL]), pack([gsmall[n] for n in SMALL]), pack([given["m_" + n] for n in SMALL]),
                 pack([given["v_" + n] for n in SMALL]))
    for n in SMALL:
        res["grad"][n] = gsmall[n]
    for key, arr in zip(("delta", "new_m", "new_v"), flat):
        res[key].update(zip(SMALL, unpack(arr, shapes)))
    order = ["w_ada", "b_ada", "norm_mix", "w_in", "conv_w", "conv_b", "dt_bias", "a_log", "d_skip", "ssd_norm", "q_norm", "w_uq",
             "kv_norm", "w_ukv", "attn_norm", "w_out", "norm_mlp", "w_up", "conv_ff_w", "conv_ff_b", "w_down", "final_norm"]
    return (loss, grad_x, *[res[k][n] for k in ("grad", "delta", "new_m", "new_v") for n in order])
```

```python
import functools

import numpy as np
import jax
import jax.numpy as jnp
from jax import lax
from jax.experimental import pallas as pl
from jax.experimental.pallas import tpu as pltpu

f32, bf16 = jnp.float32, jnp.bfloat16
HIGHEST = lax.Precision.HIGHEST

D = 1024
D_SSD = 1024
SSD_HEADS = 16
SSD_HD = 64
SSD_N = 128
CHUNK = 128
D_XBC = 1536
CONV_K = 4
MLA_H = 8
NOPE = 128
ROPE = 64
VD = 128
QK = NOPE + ROPE
Q_RANK = 384
KV_RANK = 256
D_FF = 2816
FF_K = 3
EPS = 1e-6
ROPE_BASE = 10000.0
DEPTH = 2
ADAM_LR, ADAM_B1, ADAM_B2, ADAM_EPS, ADAM_WD, ADAM_STEP = 0.001, 0.9, 0.999, 1e-08, 0.01, 10

N_DEV = 8
IN_COLS = 3456
OFF_XBC, OFF_CQ, OFF_CKV, OFF_KRDT = 1024, 2560, 3072, 3328
DT_LANE = 64
VMEM_LIMIT = 48 * 1024 * 1024
MM_K_WHOLE = 4096


def _cparams(n_grid):
    return pltpu.CompilerParams(dimension_semantics=("arbitrary",) * n_grid, vmem_limit_bytes=VMEM_LIMIT)


def _pick(n, cands):
    for c in cands:
        if n % c == 0:
            return c
    return n


def _silu(x):
    return x * jax.nn.sigmoid(x)


def _dsilu(x):
    s = jax.nn.sigmoid(x)
    return s * (1.0 + x * (1.0 - s))


def _rowsum(x):
    return jnp.sum(x, axis=0, keepdims=True)


def mm(a, b, mode, name, out_dtype=f32, resid=None, gate=None, seq=None, comm=None):
    parts = list(a) if isinstance(a, (tuple, list)) else [a]
    np_ = len(parts)
    if mode == "nn":
        (M, Kp), N = parts[0].shape, b.shape[1]
    elif mode == "nt":
        (M, Kp), N = parts[0].shape, b.shape[0]
    else:
        (Kp, M), N = parts[0].shape, b.shape[1]
    K = Kp * np_
    gated = resid is not None
    whole = np_ > 1 or K <= MM_K_WHOLE
    tm = _pick(seq if gated else M, (1024, 1408, 512, 384, 256, 128) if K <= MM_K_WHOLE else (512, 256, 128))
    tn = _pick(N, (512, 1408, 384, 256, 128))
    tk = K if whole else _pick(K, (2816, 2048, 1024, 512))
    if mode == "tn":
        tm = _pick(M, (1024, 1408, 512, 384, 256, 128))
    nk = K // tk
    dims = {"nn": ((1,), (0,)), "nt": ((1,), (1,)), "tn": ((0,), (0,))}[mode]

    def body(*refs):
        a_refs, b_ref, rest = refs[:np_], refs[np_], refs[np_ + 1:]
        if gated:
            r_ref, g_ref, o_ref, y_ref, acc = rest
        else:
            o_ref, acc = rest

        def finish(res):
            if gated:
                y_ref[...] = res
                o_ref[...] = r_ref[...] + g_ref[0] * res
            else:
                o_ref[...] = res.astype(out_dtype)

        prod = None
        for p, a_ref in enumerate(a_refs):
            if np_ == 1:
                bv = b_ref[...]
            else:
                bv = b_ref[:, p * Kp:(p + 1) * Kp] if mode == "nt" else b_ref[p * Kp:(p + 1) * Kp, :]
            term = lax.dot_general(a_ref[...].astype(bf16), bv.astype(bf16), (dims, ((), ())), preferred_element_type=f32)
            prod = term if prod is None else prod + term
        if nk == 1:
            finish(prod)
        else:
            k = pl.program_id(2)

            @pl.when(k == 0)
            def _():
                acc[...] = prod

            @pl.when(k > 0)
            def _():
                acc[...] += prod

            @pl.when(k == nk - 1)
            def _():
                finish(acc[...])

    if np_ > 1:
        a_spec = pl.BlockSpec((tm, Kp), lambda i, j, k: (i, 0))
    elif mode == "tn":
        a_spec = pl.BlockSpec((tk, tm), lambda i, j, k: (k, i))
    else:
        a_spec = pl.BlockSpec((tm, tk), lambda i, j, k: (i, k))
    b_spec = pl.BlockSpec((tn, tk), lambda i, j, k: (j, k)) if mode == "nt" else pl.BlockSpec((tk, tn), lambda i, j, k: (k, j))
    o_spec = pl.BlockSpec((tm, tn), lambda i, j, k: (i, j))
    in_specs, args = [a_spec] * np_ + [b_spec], parts + [b]
    out_specs, out_shape = [o_spec], [jax.ShapeDtypeStruct((M, N), out_dtype)]
    if gated:
        per = seq // tm
        in_specs += [o_spec, pl.BlockSpec((1, 1, tn), lambda i, j, k: (i // per, 0, j))]
        args += [resid, gate]
        out_specs = [o_spec, o_spec]
        out_shape = [jax.ShapeDtypeStruct((M, N), f32), jax.ShapeDtypeStruct((M, N), f32)]
    grid = (M // tm, N // tn, nk)
    body, c_args, c_in, c_out, c_shapes, c_sems = _fuse_exchange(body, comm, len(args), len(out_specs), 1, grid)
    res = pl.pallas_call(body, name=name + "_x" if comm else name, grid=grid, in_specs=in_specs + c_in, out_specs=out_specs + c_out,
                         out_shape=out_shape + c_shapes, scratch_shapes=[pltpu.VMEM((tm, tn), f32)] + c_sems,
                         compiler_params=_cparams(3))(*args, *c_args)
    own = res[:len(out_specs)]
    own = own[0] if len(own) == 1 else tuple(own)
    return (own, list(res[len(out_specs):])) if comm else own


def _tok(ts, width, cb=0):
    return pl.BlockSpec((1, ts, width), lambda b, s: (b, s, cb))


def _perb(rows, width):
    return pl.BlockSpec((1, rows, width), lambda b, s: (b, 0, 0))


def _const(rows, width):
    return pl.BlockSpec((rows, width), lambda b, s: (0, 0))


def _row_call(body, name, B, S, ts, in_specs, out_specs, out_shape, scratch=(), aliases=None):
    return pl.pallas_call(body, name=name, grid=(B, S // ts), in_specs=in_specs, out_specs=out_specs,
                          out_shape=out_shape, scratch_shapes=list(scratch), input_output_aliases=aliases or {},
                          compiler_params=_cparams(2))


def _first():
    return (pl.program_id(0) == 0) & (pl.program_id(1) == 0)


def normmod_fwd(x3, mod, g, i_sh, i_sc):
    B, S, C = x3.shape
    ts = _pick(S, (512, 256, 128))

    def body(x_ref, mod_ref, g_ref, h_ref):
        x = x_ref[0]
        r = lax.rsqrt(jnp.mean(x * x, axis=-1, keepdims=True) + EPS)
        n = x * r * g_ref[...]
        h_ref[0] = (n * (1.0 + mod_ref[0, i_sc:i_sc + 1, :]) + mod_ref[0, i_sh:i_sh + 1, :]).astype(bf16)

    return _row_call(body, "normmod_fwd", B, S, ts, [_tok(ts, C), _perb(6, C), _const(1, C)], _tok(ts, C),
                     jax.ShapeDtypeStruct((B, S, C), bf16))(x3, mod, g)


def normmod_bwd(x3, dh3, resid3, mod, g, i_sc):
    B, S, C = x3.shape
    ts = _pick(S, (512, 256, 128))

    def body(x_ref, dh_ref, r_ref, mod_ref, g_ref, dx_ref, dsh_ref, dsc_ref, dg_ref):
        @pl.when(pl.program_id(1) == 0)
        def _():
            dsh_ref[...] = jnp.zeros_like(dsh_ref)
            dsc_ref[...] = jnp.zeros_like(dsc_ref)

        @pl.when(_first())
        def _():
            dg_ref[...] = jnp.zeros_like(dg_ref)

        x, dh, gv = x_ref[0], dh_ref[0], g_ref[...]
        r = lax.rsqrt(jnp.mean(x * x, axis=-1, keepdims=True) + EPS)
        xh = x * r
        dn = dh * (1.0 + mod_ref[0, i_sc:i_sc + 1, :])
        dsh_ref[0] += _rowsum(dh)
        dsc_ref[0] += _rowsum(dh * xh * gv)
        dg_ref[...] += _rowsum(dn * xh)
        dxh = dn * gv
        dx_ref[0] = r * (dxh - xh * jnp.mean(dxh * xh, axis=-1, keepdims=True)) + r_ref[0]

    return _row_call(body, "normmod_bwd", B, S, ts,
                     [_tok(ts, C), _tok(ts, C), _tok(ts, C), _perb(6, C), _const(1, C)],
                     [_tok(ts, C), _perb(1, C), _perb(1, C), _const(1, C)],
                     [jax.ShapeDtypeStruct((B, S, C), f32), jax.ShapeDtypeStruct((B, 1, C), f32),
                      jax.ShapeDtypeStruct((B, 1, C), f32), jax.ShapeDtypeStruct((1, C), f32)])(x3, dh3, resid3, mod, g)


def gate_bwd(dx3, y3, mod, i_g):
    B, S, C = dx3.shape
    ts = _pick(S, (512, 256, 128))

    def body(dx_ref, y_ref, mod_ref, dy_ref, dgate_ref):
        @pl.when(pl.program_id(1) == 0)
        def _():
            dgate_ref[...] = jnp.zeros_like(dgate_ref)

        dx = dx_ref[0]
        dy_ref[0] = (dx * mod_ref[0, i_g:i_g + 1, :]).astype(bf16)
        dgate_ref[0] += _rowsum(dx * y_ref[0])

    return _row_call(body, "gate_bwd", B, S, ts, [_tok(ts, C), _tok(ts, C), _perb(6, C)], [_tok(ts, C), _perb(1, C)],
                     [jax.ShapeDtypeStruct((B, S, C), bf16), jax.ShapeDtypeStruct((B, 1, C), f32)])(dx3, y3, mod)


def rms_fwd(src3, width, cb, n, g, name):
    B, S, _ = src3.shape
    ts = _pick(S, (512, 256, 128))

    def body(x_ref, g_ref, o_ref):
        x = x_ref[0][:, :n]
        r = lax.rsqrt(jnp.mean(x * x, axis=-1, keepdims=True) + EPS)
        o_ref[0] = (x * r * g_ref[...]).astype(bf16)

    return _row_call(body, name, B, S, ts, [_tok(ts, width, cb), _const(1, n)], _tok(ts, n),
                     jax.ShapeDtypeStruct((B, S, n), bf16))(src3, g)


def rms_bwd(src3, width, cb, n, dout3, dcb, g, out_dtype, name):
    B, S, _ = src3.shape
    ts = _pick(S, (512, 256, 128))

    def body(x_ref, do_ref, g_ref, dx_ref, dg_ref):
        @pl.when(_first())
        def _():
            dg_ref[...] = jnp.zeros_like(dg_ref)

        x = x_ref[0][:, :n]
        do = do_ref[0].astype(f32)
        r = lax.rsqrt(jnp.mean(x * x, axis=-1, keepdims=True) + EPS)
        xh = x * r
        dg_ref[...] += _rowsum(do * xh)
        dxh = do * g_ref[...]
        dx_ref[0] = (r * (dxh - xh * jnp.mean(dxh * xh, axis=-1, keepdims=True))).astype(out_dtype)

    return _row_call(body, name, B, S, ts, [_tok(ts, width, cb), _tok(ts, n, dcb), _const(1, n)],
                     [_tok(ts, n), _const(1, n)],
                     [jax.ShapeDtypeStruct((B, S, n), out_dtype), jax.ShapeDtypeStruct((1, n), f32)])(src3, dout3, g)


def rms_bwd_into(src3, width, cb, n, dout3, g, dp, name):
    B, S, _ = src3.shape
    ts = _pick(S, (512, 256, 128))

    def body(x_ref, do_ref, g_ref, dp_in, dp_ref, dg_ref):
        @pl.when(_first())
        def _():
            dg_ref[...] = jnp.zeros_like(dg_ref)

        x = x_ref[0][:, :n]
        do = do_ref[0]
        r = lax.rsqrt(jnp.mean(x * x, axis=-1, keepdims=True) + EPS)
        xh = x * r
        dg_ref[...] += _rowsum(do * xh)
        dxh = do * g_ref[...]
        dp_ref[0, :, :n] = (r * (dxh - xh * jnp.mean(dxh * xh, axis=-1, keepdims=True))).astype(bf16)
        if width > n:
            dp_ref[0, :, n:] = jnp.zeros((ts, width - n), bf16)

    return _row_call(body, name, B, S, ts, [_tok(ts, width, cb), _tok(ts, n), _const(1, n), pl.BlockSpec(memory_space=pl.ANY)],
                     [_tok(ts, width, cb), _const(1, n)], [jax.ShapeDtypeStruct(dp.shape, bf16), jax.ShapeDtypeStruct((1, n), f32)],
                     aliases={3: 0})(src3, dout3, g, dp)


def final_loss(x3, g, tgt3):
    B, S, C = x3.shape
    ts = _pick(S, (512, 256, 128))

    def body(x_ref, g_ref, t_ref, dx_ref, dg_ref, loss_ref):
        @pl.when(_first())
        def _():
            dg_ref[...] = jnp.zeros_like(dg_ref)
            loss_ref[...] = jnp.zeros_like(loss_ref)

        x, gv = x_ref[0], g_ref[...]
        r = lax.rsqrt(jnp.mean(x * x, axis=-1, keepdims=True) + EPS)
        xh = x * r
        e = xh * gv - t_ref[0]
        loss_ref[...] += 0.5 * jnp.sum(e * e) / C
        dout = e / C
        dg_ref[...] += _rowsum(dout * xh)
        dxh = dout * gv
        dx_ref[0] = r * (dxh - xh * jnp.mean(dxh * xh, axis=-1, keepdims=True))

    return _row_call(body, "final_loss", B, S, ts, [_tok(ts, C), _const(1, C), _tok(ts, C)],
                     [_tok(ts, C), _const(1, C), _const(1, 128)],
                     [jax.ShapeDtypeStruct((B, S, C), f32), jax.ShapeDtypeStruct((1, C), f32),
                      jax.ShapeDtypeStruct((1, 128), f32)])(x3, g, tgt3)


def ssd_out_fwd(yc3, p3, w):
    B, S, C = yc3.shape
    ts = _pick(S, (512, 256, 128))
    half = C // 2

    def body(y_ref, z_ref, w_ref, o_ref):
        y = y_ref[0] * _silu(z_ref[0])
        for lo in (0, half):
            yg = y[:, lo:lo + half]
            r = lax.rsqrt(jnp.mean(yg * yg, axis=-1, keepdims=True) + EPS)
            o_ref[0, :, lo:lo + half] = (yg * r * w_ref[:, lo:lo + half]).astype(bf16)

    return _row_call(body, "ssd_out_fwd", B, S, ts, [_tok(ts, C), _tok(ts, C, 0), _const(1, C)], _tok(ts, C),
                     jax.ShapeDtypeStruct((B, S, C), bf16))(yc3, p3, w)


def ssd_out_bwd(yc3, p3, dcat3, w):
    B, S, C = yc3.shape
    ts = _pick(S, (512, 256, 128))
    half = C // 2

    def body(y_ref, z_ref, do_ref, w_ref, dyc_ref, dz_ref, dw_ref):
        @pl.when(_first())
        def _():
            dw_ref[...] = jnp.zeros_like(dw_ref)

        yc, z, do = y_ref[0], z_ref[0], do_ref[0]
        sz = _silu(z)
        y = yc * sz
        for lo in (0, half):
            sl = slice(lo, lo + half)
            yg, dog, wg = y[:, sl], do[:, sl], w_ref[:, sl]
            r = lax.rsqrt(jnp.mean(yg * yg, axis=-1, keepdims=True) + EPS)
            yh = yg * r
            dw_ref[:, sl] += _rowsum(dog * yh)
            dyh = dog * wg
            dy = r * (dyh - yh * jnp.mean(dyh * yh, axis=-1, keepdims=True))
            dyc_ref[0, :, sl] = dy * sz[:, sl]
            dz_ref[0, :, sl] = (dy * yc[:, sl] * _dsilu(z[:, sl])).astype(bf16)

    return _row_call(body, "ssd_out_bwd", B, S, ts, [_tok(ts, C), _tok(ts, C, 0), _tok(ts, C, 0), _const(1, C)],
                     [_tok(ts, C), _tok(ts, C, 0), _const(1, C)],
                     [jax.ShapeDtypeStruct((B, S, C), f32), jax.ShapeDtypeStruct((B, S, IN_COLS), bf16),
                      jax.ShapeDtypeStruct((1, C), f32)])(yc3, p3, dcat3, w)


def _rot(t):
    lane = lax.broadcasted_iota(jnp.int32, t.shape, 1)
    return jnp.where(lane < ROPE // 2, -pltpu.roll(t, 128 - ROPE // 2, 1), pltpu.roll(t, ROPE // 2, 1))


def _rope(t, cosf, sinf):
    return t * cosf + _rot(t) * sinf


def _rope_t(d, cosf, sinf):
    return d * cosf - _rot(d * sinf)


def qprep_fwd(qraw3, cosf, sinf):
    B, S, W = qraw3.shape
    ts = _pick(S, (512, 256, 128))

    def body(q_ref, c_ref, s_ref, o_ref):
        c, s = c_ref[0], s_ref[0]
        for h in range(MLA_H):
            o_ref[0, :, h * 256:h * 256 + 128] = (q_ref[0, :, h * 256:h * 256 + 128] * Q_FOLD).astype(bf16)
            o_ref[0, :, h * 256 + 128:(h + 1) * 256] = (_rope(q_ref[0, :, h * 256 + 128:(h + 1) * 256], c, s) * Q_FOLD).astype(bf16)

    return _row_call(body, "qprep_fwd", B, S, ts, [_tok(ts, W), _tok(ts, 128), _tok(ts, 128)], _tok(ts, W),
                     jax.ShapeDtypeStruct((B, S, W), bf16))(qraw3, cosf, sinf)


def qprep_bwd(dq3, cosf, sinf):
    B, S, W = dq3.shape
    ts = _pick(S, (512, 256, 128))

    def body(d_ref, c_ref, s_ref, o_ref):
        c, s = c_ref[0], s_ref[0]
        for h in range(MLA_H):
            o_ref[0, :, h * 256:h * 256 + 128] = (d_ref[0, :, h * 256:h * 256 + 128] * Q_FOLD).astype(bf16)
            o_ref[0, :, h * 256 + 128:(h + 1) * 256] = (_rope_t(d_ref[0, :, h * 256 + 128:(h + 1) * 256], c, s) * Q_FOLD).astype(bf16)

    return _row_call(body, "qprep_bwd", B, S, ts, [_tok(ts, W), _tok(ts, 128), _tok(ts, 128)], _tok(ts, W),
                     jax.ShapeDtypeStruct((B, S, W), bf16))(dq3, cosf, sinf)


def kprep_fwd(kv3, p3, cosf, sinf):
    B, S, _ = kv3.shape
    ts = _pick(S, (512, 256, 128))
    Wn = MLA_H * NOPE

    def body(k_ref, v_ref, kr_ref, c_ref, s_ref, ko_ref, vo_ref):
        lane = lax.broadcasted_iota(jnp.int32, (1, 128), 1)
        kr = jnp.where(lane < ROPE, kr_ref[0], 0.0)
        kr = _rope(kr, c_ref[0], s_ref[0]).astype(bf16)
        for h in range(MLA_H):
            ko_ref[0, :, h * 256:h * 256 + 128] = k_ref[0, :, h * 128:(h + 1) * 128].astype(bf16)
            ko_ref[0, :, h * 256 + 128:(h + 1) * 256] = kr
        vo_ref[0] = v_ref[0].astype(bf16)

    return _row_call(body, "kprep_fwd", B, S, ts,
                     [_tok(ts, Wn, 0), _tok(ts, Wn, 1), _tok(ts, 128, OFF_KRDT // 128), _tok(ts, 128), _tok(ts, 128)],
                     [_tok(ts, 2 * Wn), _tok(ts, Wn)],
                     [jax.ShapeDtypeStruct((B, S, 2 * Wn), bf16), jax.ShapeDtypeStruct((B, S, Wn), bf16)])(kv3, kv3, p3, cosf, sinf)


def kprep_bwd(dk3, dv3, ddt3, cosf, sinf, dp):
    B, S, _ = dk3.shape
    ts = _pick(S, (512, 256, 128))
    Wn = MLA_H * NOPE

    def body(dk_ref, dv_ref, ddt_ref, c_ref, s_ref, dp_in, o_ref, kr_ref):
        acc = jnp.zeros((ts, 128), f32)
        for h in range(MLA_H):
            o_ref[0, :, h * 128:(h + 1) * 128] = dk_ref[0, :, h * 256:h * 256 + 128].astype(bf16)
            acc = acc + dk_ref[0, :, h * 256 + 128:(h + 1) * 256]
        o_ref[0, :, Wn:] = dv_ref[0].astype(bf16)
        lane = lax.broadcasted_iota(jnp.int32, (1, 128), 1)
        dkr = _rope_t(acc, c_ref[0], s_ref[0])
        kr_ref[0] = jnp.where(lane < ROPE, dkr, ddt_ref[0]).astype(bf16)

    return _row_call(body, "kprep_bwd", B, S, ts,
                     [_tok(ts, 2 * Wn), _tok(ts, Wn), _tok(ts, 128), _tok(ts, 128), _tok(ts, 128), pl.BlockSpec(memory_space=pl.ANY)],
                     [_tok(ts, 2 * Wn), _tok(ts, 128, OFF_KRDT // 128)],
                     [jax.ShapeDtypeStruct((B, S, 2 * Wn), bf16), jax.ShapeDtypeStruct(dp.shape, bf16)],
                     aliases={5: 1})(dk3, dv3, ddt3, cosf, sinf, dp)


def _shift_down(u, j):
    if j == 0:
        return u
    row = lax.broadcasted_iota(jnp.int32, u.shape, 0)
    return jnp.where(row < j, 0.0, pltpu.roll(u, j, 0))


def _shift_up(u, j):
    if j == 0:
        return u
    n = u.shape[0]
    row = lax.broadcasted_iota(jnp.int32, u.shape, 0)
    return jnp.where(row >= n - j, 0.0, pltpu.roll(u, n - j, 0))


def _conv(u, w, b, K):
    out = b
    for j in range(K):
        out = out + w[K - 1 - j:K - j, :] * _shift_down(u, j)
    return out


def _conv_bwd(u, du, w, K):
    dins = w[K - 1:K, :] * du
    dws = [None] * K
    dws[K - 1] = _rowsum(du * u)
    for j in range(1, K):
        dins = dins + w[K - 1 - j:K - j, :] * _shift_up(du, j)
        dws[K - 1 - j] = _rowsum(du * _shift_down(u, j))
    return dins, dws


CW = 256


def conv_ssd_fwd(p3, w, b):
    B, S, _ = p3.shape
    nb = D_XBC // CW

    def body(u_ref, w_ref, b_ref, o_ref):
        o_ref[0] = _silu(_conv(u_ref[0], w_ref[...], b_ref[...], CONV_K))

    return pl.pallas_call(body, name="conv_ssd_fwd", grid=(B, nb),
                          in_specs=[pl.BlockSpec((1, S, CW), lambda b, j: (b, 0, OFF_XBC // CW + j)),
                                    pl.BlockSpec((CONV_K, CW), lambda b, j: (0, j)),
                                    pl.BlockSpec((1, CW), lambda b, j: (0, j))],
                          out_specs=pl.BlockSpec((1, S, CW), lambda b, j: (b, 0, j)),
                          out_shape=jax.ShapeDtypeStruct((B, S, D_XBC), f32), compiler_params=_cparams(2))(p3, w, b)


def conv_ssd_bwd(p3, dxc3, w, b, dp):
    B, S, _ = p3.shape
    nb = D_XBC // CW

    def body(u_ref, d_ref, w_ref, b_ref, dp_in, du_ref, dw_ref, db_ref):
        @pl.when(pl.program_id(1) == 0)
        def _():
            dw_ref[...] = jnp.zeros_like(dw_ref)
            db_ref[...] = jnp.zeros_like(db_ref)

        u, wv = u_ref[0], w_ref[...]
        dpre = d_ref[0] * _dsilu(_conv(u, wv, b_ref[...], CONV_K))
        dins, dws = _conv_bwd(u, dpre, wv, CONV_K)
        du_ref[0] = dins.astype(bf16)
        for k in range(CONV_K):
            dw_ref[k:k + 1, :] += dws[k]
        db_ref[...] += _rowsum(dpre)

    return pl.pallas_call(body, name="conv_ssd_bwd", grid=(nb, B),
                          in_specs=[pl.BlockSpec((1, S, CW), lambda j, b: (b, 0, OFF_XBC // CW + j)),
                                    pl.BlockSpec((1, S, CW), lambda j, b: (b, 0, j)),
                                    pl.BlockSpec((CONV_K, CW), lambda j, b: (0, j)),
                                    pl.BlockSpec((1, CW), lambda j, b: (0, j)), pl.BlockSpec(memory_space=pl.ANY)],
                          out_specs=[pl.BlockSpec((1, S, CW), lambda j, b: (b, 0, OFF_XBC // CW + j)),
                                     pl.BlockSpec((CONV_K, CW), lambda j, b: (0, j)),
                                     pl.BlockSpec((1, CW), lambda j, b: (0, j))],
                          out_shape=[jax.ShapeDtypeStruct(dp.shape, bf16), jax.ShapeDtypeStruct((CONV_K, D_XBC), f32),
                                     jax.ShapeDtypeStruct((1, D_XBC), f32)], input_output_aliases={4: 0},
                          compiler_params=_cparams(2))(p3, dxc3, w, b, dp)


def glu_fwd(u3, w, b):
    B, S, _ = u3.shape
    nb = D_FF // CW

    def body(ug_ref, uv_ref, wg_ref, wv_ref, bg_ref, bv_ref, o_ref):
        g = _conv(ug_ref[0].astype(f32), wg_ref[...], bg_ref[...], FF_K)
        v = _conv(uv_ref[0].astype(f32), wv_ref[...], bv_ref[...], FF_K)
        o_ref[0] = (_silu(g) * v).astype(bf16)

    def blk(off):
        return pl.BlockSpec((1, S, CW), lambda b, j: (b, 0, off + j))

    def par(rows, off):
        return pl.BlockSpec((rows, CW), lambda b, j: (0, off + j))

    return pl.pallas_call(body, name="glu_fwd", grid=(B, nb),
                          in_specs=[blk(0), blk(nb), par(FF_K, 0), par(FF_K, nb), par(1, 0), par(1, nb)],
                          out_specs=blk(0), out_shape=jax.ShapeDtypeStruct((B, S, D_FF), bf16),
                          compiler_params=_cparams(2))(u3, u3, w, w, b, b)


def glu_bwd(u3, da3, w, b):
    B, S, _ = u3.shape
    nb = D_FF // CW

    def body(ug_ref, uv_ref, da_ref, wg_ref, wv_ref, bg_ref, bv_ref, dug_ref, duv_ref, dwg_ref, dwv_ref, dbg_ref, dbv_ref):
        @pl.when(pl.program_id(1) == 0)
        def _():
            for r in (dwg_ref, dwv_ref, dbg_ref, dbv_ref):
                r[...] = jnp.zeros_like(r)

        ug, uv, da, wg, wv = ug_ref[0].astype(f32), uv_ref[0].astype(f32), da_ref[0], wg_ref[...], wv_ref[...]
        g = _conv(ug, wg, bg_ref[...], FF_K)
        v = _conv(uv, wv, bv_ref[...], FF_K)
        dg = da * v * _dsilu(g)
        dv = da * _silu(g)
        ding, dwsg = _conv_bwd(ug, dg, wg, FF_K)
        dinv, dwsv = _conv_bwd(uv, dv, wv, FF_K)
        dug_ref[0] = ding.astype(bf16)
        duv_ref[0] = dinv.astype(bf16)
        for k in range(FF_K):
            dwg_ref[k:k + 1, :] += dwsg[k]
            dwv_ref[k:k + 1, :] += dwsv[k]
        dbg_ref[...] += _rowsum(dg)
        dbv_ref[...] += _rowsum(dv)

    def blk(off):
        return pl.BlockSpec((1, S, CW), lambda j, b: (b, 0, off + j))

    def par(rows, off):
        return pl.BlockSpec((rows, CW), lambda j, b: (0, off + j))

    return pl.pallas_call(body, name="glu_bwd", grid=(nb, B),
                          in_specs=[blk(0), blk(nb), blk(0), par(FF_K, 0), par(FF_K, nb), par(1, 0), par(1, nb)],
                          out_specs=[blk(0), blk(0), par(FF_K, 0), par(FF_K, 0), par(1, 0), par(1, 0)],
                          out_shape=[jax.ShapeDtypeStruct((B, S, D_FF), bf16), jax.ShapeDtypeStruct((B, S, D_FF), bf16),
                                     jax.ShapeDtypeStruct((FF_K, D_FF), f32), jax.ShapeDtypeStruct((FF_K, D_FF), f32),
                                     jax.ShapeDtypeStruct((1, D_FF), f32), jax.ShapeDtypeStruct((1, D_FF), f32)],
                          compiler_params=_cparams(2))(u3, u3, da3, w, w, b, b)


def _ssd_decay(dtb, bias_row, alog_row):
    lane = lax.broadcasted_iota(jnp.int32, (1, 128), 1)
    hmask = (lane >= DT_LANE) & (lane < DT_LANE + SSD_HEADS)
    dt = jnp.where(hmask, jax.nn.softplus(dtb + bias_row), 0.0)
    a = dt * jnp.where(hmask, -jnp.exp(alog_row), 0.0)
    r = lax.broadcasted_iota(jnp.int32, (CHUNK, CHUNK), 0)
    c = lax.broadcasted_iota(jnp.int32, (CHUNK, CHUNK), 1)
    cs = jnp.dot((r >= c).astype(f32), a, precision=HIGHEST, preferred_element_type=f32)
    return dt, cs


def _expand(xt):
    return jnp.concatenate([jnp.broadcast_to(xt[DT_LANE + h:DT_LANE + h + 1, :], (SSD_HD, xt.shape[1]))
                            for h in range(SSD_HEADS)], axis=0)


_NT = (((1,), (1,)), ((), ()))
_TN = (((0,), (0,)), ((), ()))
GH = SSD_HEADS // 2
GR = GH * SSD_HD


def ssd_fwd(xc3, p3, bias_row, alog_row, dcol):
    B, S, _ = xc3.shape
    nc = S // CHUNK

    def body(xs_ref, bc_ref, dtb_ref, bias_ref, alog_ref, dcol_ref, y_ref, st_ref, state, yT):
        @pl.when(pl.program_id(1) == 0)
        def _():
            state[...] = jnp.zeros_like(state)

        dt, cs = _ssd_decay(dtb_ref[0], bias_ref[...], alog_ref[...])
        csT = cs.T
        eT = jnp.exp(csT)
        decX = _expand(jnp.exp(csT[:, CHUNK - 1:CHUNK] - csT))
        eX = _expand(eT)
        elastX = eX[:, CHUNK - 1:CHUNK]
        xsT = xs_ref[0].T
        uT = xsT * _expand(dt.T)
        bc = bc_ref[0]
        st_ref[0, 0] = state[...]
        srow = lax.broadcasted_iota(jnp.int32, (CHUNK, CHUNK), 0)
        lcol = lax.broadcasted_iota(jnp.int32, (CHUNK, CHUNK), 1)
        for g in range(2):
            Bg = bc[:, g * SSD_N:(g + 1) * SSD_N].astype(bf16)
            Cg = bc[:, (2 + g) * SSD_N:(3 + g) * SSD_N].astype(bf16)
            GT = lax.dot_general(Bg, Cg, _NT, preferred_element_type=f32)
            rows = slice(g * GR, (g + 1) * GR)
            Sg = state[rows]
            yoffT = lax.dot_general(Sg.astype(bf16), Cg, _NT, preferred_element_type=f32) * eX[rows]
            state[rows] = Sg * elastX[rows] + jnp.dot((uT[rows] * decX[rows]).astype(bf16), Bg, preferred_element_type=f32)
            for k in range(GH):
                h = g * GH + k
                hr = slice(h * SSD_HD, (h + 1) * SSD_HD)
                seg = csT[DT_LANE + h:DT_LANE + h + 1, :] - cs[:, DT_LANE + h:DT_LANE + h + 1]
                LT = jnp.where(lcol >= srow, jnp.exp(jnp.minimum(seg, 0.0)), 0.0)
                yT[hr] = (jnp.dot(uT[hr].astype(bf16), (GT * LT).astype(bf16), preferred_element_type=f32)
                          + yoffT[k * SSD_HD:(k + 1) * SSD_HD] + dcol_ref[hr] * xsT[hr])
        y_ref[0] = yT[...].T

    return pl.pallas_call(body, name="ssd_fwd", grid=(B, nc),
                          in_specs=[pl.BlockSpec((1, CHUNK, D_SSD), lambda b, c: (b, c, 0)),
                                    pl.BlockSpec((1, CHUNK, 512), lambda b, c: (b, c, 2)),
                                    pl.BlockSpec((1, CHUNK, 128), lambda b, c: (b, c, OFF_KRDT // 128)),
                                    _const(1, 128), _const(1, 128), _const(D_SSD, 1)],
                          out_specs=[pl.BlockSpec((1, CHUNK, D_SSD), lambda b, c: (b, c, 0)),
                                     pl.BlockSpec((1, 1, D_SSD, SSD_N), lambda b, c: (b, c, 0, 0))],
                          out_shape=[jax.ShapeDtypeStruct((B, S, D_SSD), f32), jax.ShapeDtypeStruct((B, nc, D_SSD, SSD_N), f32)],
                          scratch_shapes=[pltpu.VMEM((D_SSD, SSD_N), f32), pltpu.VMEM((D_SSD, CHUNK), f32)],
                          compiler_params=_cparams(2))(xc3, xc3, p3, bias_row, alog_row, dcol)


def ssd_bwd(xc3, p3, dy3, states, bias_row, alog_row, bias_col, alog_col, dcol, comm=None):
    B, S, _ = xc3.shape
    nc = S // CHUNK

    def body(xs_ref, bc_ref, dtb_ref, dy_ref, st_ref, bias_ref, alog_ref, biasc_ref, alogc_ref, dcol_ref,
             dxc_ref, ddt_ref, dalog_ref, dd_ref, dbias_ref, dS, dUT, accA, accD, accB, dcs_diag):
        @pl.when(pl.program_id(1) == 0)
        def _():
            dS[...] = jnp.zeros_like(dS)

        @pl.when(_first())
        def _():
            accA[...] = jnp.zeros_like(accA)
            accD[...] = jnp.zeros_like(accD)
            accB[...] = jnp.zeros_like(accB)

        dtb = dtb_ref[0]
        dt, cs = _ssd_decay(dtb, bias_ref[...], alog_ref[...])
        dtT, csT = dt.T, cs.T
        decX = _expand(jnp.exp(csT[:, CHUNK - 1:CHUNK] - csT))
        eX = _expand(jnp.exp(csT))
        dtX = _expand(dtT)
        elastX = eX[:, CHUNK - 1:CHUNK]
        xsT = xs_ref[0].T
        uT = xsT * dtX
        dYT = dy_ref[0].T
        bc = bc_ref[0]
        lrow = lax.broadcasted_iota(jnp.int32, (CHUNK, CHUNK), 0)
        scol = lax.broadcasted_iota(jnp.int32, (CHUNK, CHUNK), 1)
        dcs_diag[...] = jnp.zeros_like(dcs_diag)
        rs_cols = jnp.zeros((CHUNK, 128), f32)
        vparts, zparts = [], []
        for g in range(2):
            Bf = bc[:, g * SSD_N:(g + 1) * SSD_N]
            Bg = Bf.astype(bf16)
            Cg = bc[:, (2 + g) * SSD_N:(3 + g) * SSD_N].astype(bf16)
            G = lax.dot_general(Cg, Bg, _NT, preferred_element_type=f32)
            BgT = Bf.T.astype(bf16)
            rows = slice(g * GR, (g + 1) * GR)
            dSg = dS[rows]
            Sg = st_ref[0, 0, rows, :]
            dUst = jnp.dot(dSg.astype(bf16), BgT, preferred_element_type=f32) * decX[rows]
            yoffT = lax.dot_general(Sg.astype(bf16), Cg, _NT, preferred_element_type=f32) * eX[rows]
            zparts.append(dYT[rows] * yoffT - dUst * uT[rows])
            dG = jnp.zeros((CHUNK, CHUNK), f32)
            for k in range(GH):
                h = g * GH + k
                hr = slice(h * SSD_HD, (h + 1) * SSD_HD)
                seg = cs[:, DT_LANE + h:DT_LANE + h + 1] - csT[DT_LANE + h:DT_LANE + h + 1, :]
                L = jnp.where(lrow >= scol, jnp.exp(jnp.minimum(seg, 0.0)), 0.0)
                M = G * L
                dYh = dYT[hr].astype(bf16)
                dUT[hr] = jnp.dot(dYh, M.astype(bf16), preferred_element_type=f32) + dUst[k * SSD_HD:(k + 1) * SSD_HD]
                dM = lax.dot_general(dYh, uT[hr].astype(bf16), _TN, preferred_element_type=f32)
                dG = dG + dM * L
                Wm = dM * M
                rs_cols = jnp.where(scol == DT_LANE + h, jnp.sum(Wm, axis=1, keepdims=True), rs_cols)
                dcs_diag[DT_LANE + h:DT_LANE + h + 1, :] = -_rowsum(Wm)
            dGb = dG.astype(bf16)
            dYe = (dYT[rows] * eX[rows]).astype(bf16)
            ude = (uT[rows] * decX[rows]).astype(bf16)
            dC = jnp.dot(dGb, Bg, preferred_element_type=f32) + lax.dot_general(dYe, Sg.astype(bf16), _TN, preferred_element_type=f32)
            dB = (lax.dot_general(dGb, Cg, _TN, preferred_element_type=f32)
                  + lax.dot_general(ude, dSg.astype(bf16), _TN, preferred_element_type=f32))
            dxc_ref[0, :, D_SSD + g * SSD_N:D_SSD + (g + 1) * SSD_N] = dB
            dxc_ref[0, :, D_SSD + (2 + g) * SSD_N:D_SSD + (3 + g) * SSD_N] = dC
            vparts.append(elastX[rows] * jnp.sum(dSg * Sg, axis=1, keepdims=True)
                          + jnp.sum(dUst * uT[rows], axis=1, keepdims=True))
            dS[rows] = elastX[rows] * dSg + jnp.dot(dYe, Cg, preferred_element_type=f32)
        dU = dUT[...]
        dcv = dcol_ref[...]
        dxc_ref[0, :, 0:D_SSD] = (dtX * dU + dcv * dYT).T
        lane = lax.broadcasted_iota(jnp.int32, (D_SSD, CHUNK), 1)
        Z = jnp.concatenate(zparts, axis=0) + jnp.where(lane == CHUNK - 1, jnp.concatenate(vparts, axis=0), 0.0)
        hr_ = lax.broadcasted_iota(jnp.int32, (128, D_SSD), 0)
        hc_ = lax.broadcasted_iota(jnp.int32, (128, D_SSD), 1)
        hsel = (hr_ - DT_LANE == jnp.right_shift(hc_, 6)).astype(bf16)
        summands = jnp.concatenate([Z, dU * xsT, dYT * xsT], axis=1)
        hi = summands.astype(bf16)
        lo = (summands - hi.astype(f32)).astype(bf16)
        red = jnp.dot(hsel, hi, preferred_element_type=f32) + jnp.dot(hsel, lo, preferred_element_type=f32)
        dcsT = red[:, 0:CHUNK] + dcs_diag[...] + rs_cols.T
        daT = jnp.dot(dcsT, (lrow >= scol).astype(f32), precision=HIGHEST, preferred_element_type=f32)
        rowi = lax.broadcasted_iota(jnp.int32, (128, 1), 0)
        hmask = (rowi >= DT_LANE) & (rowi < DT_LANE + SSD_HEADS)
        a_col = jnp.where(hmask, -jnp.exp(alogc_ref[...]), 0.0)
        ddtT = red[:, CHUNK:2 * CHUNK] + a_col * daT
        ddt_rawT = jnp.where(hmask, ddtT * jax.nn.sigmoid(dtb.T + biasc_ref[...]), 0.0)
        ddt_ref[0] = ddt_rawT.T
        accA[...] += daT * dtT
        accD[...] += red[:, 2 * CHUNK:3 * CHUNK]
        accB[...] += ddt_rawT

        @pl.when((pl.program_id(0) == B - 1) & (pl.program_id(1) == nc - 1))
        def _():
            dalog_ref[...] = jnp.broadcast_to(jnp.sum(accA[...], axis=1, keepdims=True) * a_col, (128, 128))
            dd_ref[...] = jnp.broadcast_to(jnp.sum(accD[...], axis=1, keepdims=True), (128, 128))
            dbias_ref[...] = jnp.broadcast_to(jnp.sum(accB[...], axis=1, keepdims=True), (128, 128))

    def rev(width, cb):
        return pl.BlockSpec((1, CHUNK, width), lambda b, c: (b, nc - 1 - c, cb))

    acc_spec = pl.BlockSpec((128, 128), lambda b, c: (0, 0))
    acc_shape = jax.ShapeDtypeStruct((128, 128), f32)
    body, c_args, c_in, c_out, c_shapes, c_sems = _fuse_exchange(body, comm, 10, 5, 6, (B, nc))
    res = pl.pallas_call(body, name="ssd_bwd_x" if comm else "ssd_bwd", grid=(B, nc),
                         in_specs=[rev(D_SSD, 0), rev(512, 2), rev(128, OFF_KRDT // 128), rev(D_SSD, 0),
                                   pl.BlockSpec((1, 1, D_SSD, SSD_N), lambda b, c: (b, nc - 1 - c, 0, 0)),
                                   _const(1, 128), _const(1, 128), _const(128, 1), _const(128, 1), _const(D_SSD, 1)] + c_in,
                         out_specs=[rev(D_XBC, 0), rev(128, 0), acc_spec, acc_spec, acc_spec] + c_out,
                         out_shape=[jax.ShapeDtypeStruct((B, S, D_XBC), f32), jax.ShapeDtypeStruct((B, S, 128), f32),
                                    acc_shape, acc_shape, acc_shape] + c_shapes,
                         scratch_shapes=[pltpu.VMEM((D_SSD, SSD_N), f32), pltpu.VMEM((D_SSD, CHUNK), f32),
                                         pltpu.VMEM((128, 128), f32), pltpu.VMEM((128, 128), f32), pltpu.VMEM((128, 128), f32),
                                         pltpu.VMEM((128, 128), f32)] + c_sems,
                         compiler_params=_cparams(2))(xc3, xc3, p3, dy3, states, bias_row, alog_row, bias_col, alog_col, dcol, *c_args)
    return (*res[:5], list(res[5:]))


ATT_SCALE = float(QK) ** -0.5
LOG2E = 1.4426950408889634
LN2 = 0.6931471805599453
Q_FOLD = ATT_SCALE * LOG2E
NEG = -1e30
HP = 2


def _att_block(S):
    return _pick(S, (512, 256, 128))


def attn_fwd(q3, k3, v3, comm=None):
    B, S, _ = q3.shape
    bq = _att_block(S)
    nq = S // bq

    def body(q_ref, k_ref, v_ref, o_ref, lse_ref, m_s, l_s, acc):
        i, j = pl.program_id(2), pl.program_id(3)

        @pl.when(j == 0)
        def _():
            m_s[...] = jnp.full_like(m_s, NEG)
            l_s[...] = jnp.zeros_like(l_s)
            acc[...] = jnp.zeros_like(acc)

        def step(masked):
            for t in range(HP):
                qk = slice(t * 256, (t + 1) * 256)
                st = lax.dot_general(k_ref[0, :, qk], q_ref[0, :, qk], _NT, preferred_element_type=f32)
                if masked:
                    r = lax.broadcasted_iota(jnp.int32, (bq, bq), 0)
                    c = lax.broadcasted_iota(jnp.int32, (bq, bq), 1)
                    st = jnp.where(c >= r, st, NEG)
                m_old = m_s[t]
                m_new = jnp.maximum(m_old, jnp.max(st, axis=0, keepdims=True))
                alpha = jnp.exp2(m_old - m_new)
                pt = jnp.exp2(st - m_new)
                l_s[t] = alpha * l_s[t] + jnp.sum(pt, axis=0, keepdims=True)
                acc[t] = alpha * acc[t] + lax.dot_general(v_ref[0, :, t * VD:(t + 1) * VD], pt.astype(bf16), _TN,
                                                          preferred_element_type=f32)
                m_s[t] = m_new

        @pl.when(j < i)
        def _():
            step(False)

        @pl.when(j == i)
        def _():
            step(True)
            for t in range(HP):
                o_ref[0, :, t * VD:(t + 1) * VD] = (acc[t] / l_s[t]).T
                lse_ref[0, t] = m_s[t] + jnp.log2(l_s[t])

    grid = (B, MLA_H // HP, nq, nq)
    body, c_args, c_in, c_out, c_shapes, c_sems = _fuse_exchange(body, comm, 3, 2, 3, grid)
    res = pl.pallas_call(body, name="attn_fwd_x" if comm else "attn_fwd", grid=grid,
                         in_specs=[pl.BlockSpec((1, bq, HP * 256), lambda b, h, i, j: (b, i, h)),
                                   pl.BlockSpec((1, bq, HP * 256), lambda b, h, i, j: (b, jnp.minimum(j, i), h)),
                                   pl.BlockSpec((1, bq, HP * VD), lambda b, h, i, j: (b, jnp.minimum(j, i), h))] + c_in,
                         out_specs=[pl.BlockSpec((1, bq, HP * VD), lambda b, h, i, j: (b, i, h)),
                                    pl.BlockSpec((1, HP, 1, bq), lambda b, h, i, j: (b, h, 0, i))] + c_out,
                         out_shape=[jax.ShapeDtypeStruct((B, S, MLA_H * VD), f32), jax.ShapeDtypeStruct((B, MLA_H, 1, S), f32)] + c_shapes,
                         scratch_shapes=[pltpu.VMEM((HP, 1, bq), f32), pltpu.VMEM((HP, 1, bq), f32), pltpu.VMEM((HP, VD, bq), f32)] + c_sems,
                         compiler_params=_cparams(4))(q3, k3, v3, *c_args)
    return res[0], res[1], list(res[2:])


def attn_delta(o3, do3):
    B, S, _ = o3.shape
    ts = _pick(S, (512, 256, 128))

    def body(o_ref, do_ref, d_ref):
        for h in range(MLA_H):
            vs = slice(h * VD, (h + 1) * VD)
            d_ref[0, h] = jnp.sum(o_ref[0, :, vs] * do_ref[0, :, vs], axis=-1, keepdims=True)

    return _row_call(body, "attn_delta", B, S, ts, [_tok(ts, MLA_H * VD), _tok(ts, MLA_H * VD)],
                     pl.BlockSpec((1, MLA_H, ts, 1), lambda b, s: (b, 0, s, 0)),
                     jax.ShapeDtypeStruct((B, MLA_H, S, 1), f32))(o3, do3)


def attn_bwd(q3, k3, v3, do3, lse_row, delta_row, comm=None):
    B, S, _ = q3.shape
    bq = _att_block(S)
    nq = S // bq

    def body(q_ref, k_ref, v_ref, do_ref, lse_ref, dl_ref, dk_ref, dv_ref, dq_hbm, dk_acc, dv_acc, dq_scr, dq_sem):
        b, hp, j, i = pl.program_id(0), pl.program_id(1), pl.program_id(2), pl.program_id(3)
        rows = pl.ds(pl.multiple_of(i * bq, bq), bq)

        @pl.when((j == 0) & (i == 0))
        def _():
            dq_scr[...] = jnp.zeros_like(dq_scr)

        @pl.when(i == 0)
        def _():
            dk_acc[...] = jnp.zeros_like(dk_acc)
            dv_acc[...] = jnp.zeros_like(dv_acc)

        def step(masked):
            for t in range(HP):
                qk, vs = slice(t * 256, (t + 1) * 256), slice(t * VD, (t + 1) * VD)
                q, k = q_ref[0, :, qk], k_ref[0, :, qk]
                do = do_ref[0, :, vs].astype(bf16)
                pt = jnp.exp2(lax.dot_general(k, q, _NT, preferred_element_type=f32) - lse_ref[0, t])
                if masked:
                    r = lax.broadcasted_iota(jnp.int32, (bq, bq), 0)
                    c = lax.broadcasted_iota(jnp.int32, (bq, bq), 1)
                    pt = jnp.where(c >= r, pt, 0.0)
                dv_acc[t] += jnp.dot(pt.astype(bf16), do, preferred_element_type=f32)
                dpt = lax.dot_general(v_ref[0, :, vs], do, _NT, preferred_element_type=f32)
                dst = (pt * (dpt - dl_ref[0, t])).astype(bf16)
                dk_acc[t] += jnp.dot(dst, q, preferred_element_type=f32)
                dq_scr[t, rows, :] += lax.dot_general(dst, k, _TN, preferred_element_type=f32)

        @pl.when(i > j)
        def _():
            step(False)

        @pl.when(i == j)
        def _():
            step(True)
            for t in range(HP):
                dq_scr[t, rows, :] = dq_scr[t, rows, :] * LN2
                cp = pltpu.make_async_copy(dq_scr.at[t, rows, :],
                                           dq_hbm.at[b, rows, pl.ds(pl.multiple_of((hp * HP + t) * 256, 256), 256)], dq_sem.at[t])
                cp.start()
                cp.wait()

        @pl.when(i == nq - 1)
        def _():
            for t in range(HP):
                dk_ref[0, :, t * 256:(t + 1) * 256] = dk_acc[t] * LN2
                dv_ref[0, :, t * VD:(t + 1) * VD] = dv_acc[t]

    kspec = pl.BlockSpec((1, bq, HP * 256), lambda b, h, j, i: (b, j, h))
    vspec = pl.BlockSpec((1, bq, HP * VD), lambda b, h, j, i: (b, j, h))
    rspec = pl.BlockSpec((1, HP, 1, bq), lambda b, h, j, i: (b, h, 0, jnp.maximum(i, j)))
    grid = (B, MLA_H // HP, nq, nq)
    body, c_args, c_in, c_out, c_shapes, c_sems = _fuse_exchange(body, comm, 6, 3, 4, grid)
    res = pl.pallas_call(body, name="attn_bwd_x" if comm else "attn_bwd", grid=grid,
                         in_specs=[pl.BlockSpec((1, bq, HP * 256), lambda b, h, j, i: (b, jnp.maximum(i, j), h)), kspec, vspec,
                                   pl.BlockSpec((1, bq, HP * VD), lambda b, h, j, i: (b, jnp.maximum(i, j), h)), rspec, rspec] + c_in,
                         out_specs=[kspec, vspec, pl.BlockSpec(memory_space=pltpu.HBM)] + c_out,
                         out_shape=[jax.ShapeDtypeStruct((B, S, MLA_H * 256), f32), jax.ShapeDtypeStruct((B, S, MLA_H * VD), f32),
                                    jax.ShapeDtypeStruct((B, S, MLA_H * 256), f32)] + c_shapes,
                         scratch_shapes=[pltpu.VMEM((HP, bq, 256), f32), pltpu.VMEM((HP, bq, VD), f32), pltpu.VMEM((HP, S, 256), f32),
                                         pltpu.SemaphoreType.DMA((HP,))] + c_sems,
                         compiler_params=_cparams(4))(q3, k3, v3, do3, lse_row, delta_row, *c_args)
    return res[2], res[0], res[1], list(res[3:])


def ada_fwd(c_all, w, b):
    n = w.shape[1]

    def body(c_ref, w_ref, b_ref, o_ref):
        o_ref[...] = jnp.dot(_silu(c_ref[...]).astype(bf16), w_ref[...].astype(bf16), preferred_element_type=f32) + b_ref[...]

    return pl.pallas_call(body, name="ada_fwd", out_shape=jax.ShapeDtypeStruct((c_all.shape[0], n), f32),
                          compiler_params=pltpu.CompilerParams(vmem_limit_bytes=VMEM_LIMIT))(c_all, w, b)


def ada_bwd(c_all, dmod):
    n = dmod.shape[1]

    def body(c_ref, d_ref, o_ref):
        o_ref[...] = lax.dot_general(_silu(c_ref[...]).astype(bf16), d_ref[...].astype(bf16), _TN, preferred_element_type=f32)

    return pl.pallas_call(body, name="ada_bwd", out_shape=jax.ShapeDtypeStruct((c_all.shape[1], n), f32),
                          compiler_params=pltpu.CompilerParams(vmem_limit_bytes=VMEM_LIMIT))(c_all, dmod)


def sum_leading(x, name):
    n, R, _ = x.shape
    tr = _pick(R, (512, 256, 128, 64, 32, 16, 8))

    def body(x_ref, o_ref):
        acc = x_ref[0].astype(f32)
        for k in range(1, n):
            acc = acc + x_ref[k].astype(f32)
        o_ref[...] = acc

    return pl.pallas_call(body, name=name, grid=(R // tr,), in_specs=[pl.BlockSpec((n, tr, 128), lambda i: (0, i, 0))],
                          out_specs=pl.BlockSpec((tr, 128), lambda i: (i, 0)), out_shape=jax.ShapeDtypeStruct((R, 128), f32),
                          compiler_params=_cparams(1))(x)


def _adamw_body(w_ref, g_ref, m_ref, v_ref, d_ref, mo_ref, vo_ref):
    gv = g_ref[...]
    mn = ADAM_B1 * m_ref[...] + (1.0 - ADAM_B1) * gv
    vn = ADAM_B2 * v_ref[...] + (1.0 - ADAM_B2) * jnp.square(gv)
    m_hat = mn / (1.0 - ADAM_B1 ** ADAM_STEP)
    v_hat = vn / (1.0 - ADAM_B2 ** ADAM_STEP)
    d_ref[...] = -ADAM_LR * (m_hat / (jnp.sqrt(v_hat) + ADAM_EPS) + ADAM_WD * w_ref[...])
    mo_ref[...] = mn
    vo_ref[...] = vn


def adamw(w, g, m, v):
    R = w.shape[0]
    tr = _pick(R, (512, 256, 128, 64, 32, 16, 8))
    spec = pl.BlockSpec((tr, 128), lambda i: (i, 0))
    shp = jax.ShapeDtypeStruct((R, 128), f32)
    return pl.pallas_call(functools.partial(_adamw_body), name="adamw", grid=(R // tr,), in_specs=[spec] * 4,
                          out_specs=[spec] * 3, out_shape=[shp] * 3, compiler_params=_cparams(1))(w, g, m, v)


def _row_tile(a):
    return _pick(a, (256, 128, 64, 32, 16, 8)) if a % 8 == 0 else a


def adamw_nd(w, g, m, v):
    L, a, b = w.shape
    ta = _row_tile(a)
    spec = pl.BlockSpec((1, ta, b), lambda l, i: (l, i, 0))
    shp = jax.ShapeDtypeStruct((L, a, b), f32)
    return pl.pallas_call(functools.partial(_adamw_body), name="adamw_nd", grid=(L, a // ta), in_specs=[spec] * 4,
                          out_specs=[spec] * 3, out_shape=[shp] * 3, compiler_params=_cparams(2))(w, g, m, v)


def sum_slots(x):
    n, L, a, b = x.shape
    ta = _row_tile(a)

    def body(x_ref, o_ref):
        acc = x_ref[0].astype(f32)
        for k in range(1, n):
            acc = acc + x_ref[k].astype(f32)
        o_ref[...] = acc

    return pl.pallas_call(body, name="sum_slots", grid=(L, a // ta),
                          in_specs=[pl.BlockSpec((n, 1, ta, b), lambda l, i: (0, l, i, 0))],
                          out_specs=pl.BlockSpec((1, ta, b), lambda l, i: (l, i, 0)),
                          out_shape=jax.ShapeDtypeStruct((L, a, b), f32), compiler_params=_cparams(2))(x)


def _exchange_copies(ins, outs, sems, scatter):
    send_sems, recv_sems, local_sems = sems
    x, y, c = lax.axis_index("x"), lax.axis_index("y"), lax.axis_index("c")
    me = 4 * x + 2 * y + c
    locals_, sends, recvs = [], [], []
    for a in range(len(ins)):
        locals_.append(pltpu.make_async_copy(ins[a].at[me] if scatter[a] else ins[a], outs[a].at[me], local_sems.at[a]))
        for k in range(N_DEV - 1):
            px = 1 - x if (k + 1) & 4 else x
            py = 1 - y if (k + 1) & 2 else y
            pc = 1 - c if (k + 1) & 1 else c
            pid = 4 * px + 2 * py + pc
            src = ins[a].at[pid] if scatter[a] else ins[a]
            for slot, group in ((me, sends), (pid, recvs)):
                group.append(pltpu.make_async_remote_copy(src_ref=src, dst_ref=outs[a].at[slot], send_sem=send_sems.at[a, k],
                                                          recv_sem=recv_sems.at[a, k], device_id=(px, py, pc),
                                                          device_id_type=pl.DeviceIdType.MESH))
    return locals_, sends, recvs


def _exchange_start(ins, outs, sems, scatter):
    locals_, sends, _ = _exchange_copies(ins, outs, sems, scatter)
    for cp in locals_ + sends:
        cp.start()


def _exchange_wait(ins, outs, sems, scatter):
    locals_, sends, recvs = _exchange_copies(ins, outs, sems, scatter)
    for cp in recvs:
        cp.wait_recv()
    for cp in sends:
        cp.wait_send()
    for cp in locals_:
        cp.wait()


def _exchange_shapes(arrays, scatter):
    return [jax.ShapeDtypeStruct((N_DEV,) + tuple(a.shape[1:] if s else a.shape), a.dtype) for a, s in zip(arrays, scatter)]


def _flags(scatter, n):
    return [scatter] * n if isinstance(scatter, bool) else list(scatter)


def _exchange_sems(n):
    return [pltpu.SemaphoreType.DMA((n, N_DEV - 1)), pltpu.SemaphoreType.DMA((n, N_DEV - 1)), pltpu.SemaphoreType.DMA((n,))]


def _fuse_exchange(core, comm, n_in, n_out, n_scr, grid):
    if comm is None:
        return core, [], [], [], [], []
    arrays, scatter = comm
    n = len(arrays)
    scatter = _flags(scatter, n)

    def body(*refs):
        a, b, c = n_in + n, n_in + n + n_out, n_in + 2 * n + n_out
        cin, cout, sems = refs[n_in:a], refs[b:c], refs[c + n_scr:]
        ids = [pl.program_id(d) for d in range(len(grid))]
        first = functools.reduce(lambda p, q: p & q, [i == 0 for i in ids])
        last = functools.reduce(lambda p, q: p & q, [i == g - 1 for i, g in zip(ids, grid)])

        @pl.when(first)
        def _():
            _exchange_start(cin, cout, sems, scatter)

        core(*refs[:n_in], *refs[a:b], *refs[c:c + n_scr])

        @pl.when(last)
        def _():
            _exchange_wait(cin, cout, sems, scatter)

    hbm = pl.BlockSpec(memory_space=pltpu.HBM)
    return body, list(arrays), [hbm] * n, [hbm] * n, _exchange_shapes(arrays, scatter), _exchange_sems(n)


def exchange(arrays, scatter, name):
    n = len(arrays)
    scatter = _flags(scatter, n)

    def body(*refs):
        ins, outs, sems = refs[:n], refs[n:2 * n], refs[2 * n:]
        _exchange_start(ins, outs, sems, scatter)
        _exchange_wait(ins, outs, sems, scatter)

    hbm = pl.BlockSpec(memory_space=pltpu.HBM)
    return pl.pallas_call(body, name=name, in_specs=[hbm] * n, out_specs=[hbm] * n,
                          out_shape=_exchange_shapes(arrays, scatter), scratch_shapes=_exchange_sems(n))(*arrays)


BIG = (("w_in", "col"), ("conv_w", "col"), ("w_uq", "col"), ("w_ukv", "col"), ("w_out", "row"), ("w_up", "col"),
       ("conv_ff_w", "col"), ("w_down", "row"))
SMALL = ("b_ada", "norm_mix", "conv_b", "dt_bias", "a_log", "d_skip", "ssd_norm", "q_norm", "kv_norm", "attn_norm",
         "norm_mlp", "conv_ff_b", "final_norm")
CONVS = ("conv_w", "conv_ff_w")
PACK_ALIGN = 2048


def _padded(n):
    return -(-n // PACK_ALIGN) * PACK_ALIGN


def _flat_pad(a):
    f = a.reshape(-1)
    return jnp.pad(f, (0, _padded(f.shape[0]) - f.shape[0]))


PACK_ROWS = 512


def pack(arrs):
    f = jnp.concatenate([_flat_pad(a) for a in arrs])
    n = PACK_ROWS * 128
    return jnp.pad(f, (0, -(-f.shape[0] // n) * n - f.shape[0])).reshape(-1, 128)


def unpack(flat, shapes):
    f = flat.reshape(-1)
    out, off = [], 0
    for s in shapes:
        n = int(np.prod(s))
        out.append(f[off:off + n].reshape(s))
        off += _padded(n)
    return out


def shards_to_full(g, kind):
    _, a, b = g.shape
    if kind == "col":
        return g.transpose(1, 0, 2).reshape(a, N_DEV * b)
    return g.reshape(N_DEV * a, b)


def full_to_shards(full, kind):
    if kind == "col":
        a, nb = full.shape
        return full.reshape(a, N_DEV, nb // N_DEV).transpose(1, 0, 2)
    na, b = full.shape
    return full.reshape(N_DEV, na // N_DEV, b)


def w_in_layout(w):
    z = lambda n: jnp.zeros(w.shape[:-1] + (n,), w.dtype)
    return jnp.concatenate([w[..., :2560], w[..., 2576:2960], z(128), w[..., 2960:3216], w[..., 3216:3280],
                            w[..., 2560:2576], z(48)], axis=-1)


def w_in_unlayout(g):
    return jnp.concatenate([g[..., :2560], g[..., 3392:3408], g[..., 2560:2944], g[..., 3072:3328], g[..., 3328:3392]], axis=-1)


def w_uq_layout(w):
    return jnp.pad(w.reshape(Q_RANK, MLA_H, QK), ((0, 0), (0, 0), (0, 256 - QK))).reshape(Q_RANK, MLA_H * 256)


def w_uq_unlayout(g):
    return g.reshape(Q_RANK, MLA_H, 256)[:, :, :QK].reshape(Q_RANK, MLA_H * QK)


def w_ukv_layout(w):
    return w.reshape(KV_RANK, MLA_H, 2, 128).transpose(0, 2, 1, 3).reshape(KV_RANK, 2 * MLA_H * 128)


def w_ukv_unlayout(g):
    return g.reshape(KV_RANK, 2, MLA_H, 128).transpose(0, 2, 1, 3).reshape(KV_RANK, 2 * MLA_H * 128)


LAYOUTS = {"w_in": (w_in_layout, w_in_unlayout), "w_uq": (w_uq_layout, w_uq_unlayout), "w_ukv": (w_ukv_layout, w_ukv_unlayout)}
FIRST, REST = BIG[:4], BIG[4:]


def layer_weights(gathered, entries):
    full = {n: shards_to_full(g, kind) for (n, kind), g in zip(entries, gathered)}
    return {n: LAYOUTS[n][0](w) if n in LAYOUTS else w for n, w in full.items()}


def layer_grad_slices(g, entries):
    return [full_to_shards(LAYOUTS[n][1](g[n]) if n in LAYOUTS else g[n], kind).astype(bf16) for n, kind in entries]


def _head_row(v):
    return jnp.zeros((1, 128), f32).at[0, DT_LANE:DT_LANE + SSD_HEADS].set(v)


def layer_fwd(x3, mod, W, P, l, cosf, sinf, comm=None, late=None, comm_up=None):
    B, S, _ = x3.shape
    T = B * S
    sv = {}
    h = normmod_fwd(x3, mod, P["norm_mix"][l][None], 0, 1)
    p = mm(h.reshape(T, D), W["w_in"], "nn", "mm_in")
    p3 = p.reshape(B, S, IN_COLS)
    bias_row, alog_row = _head_row(P["dt_bias"][l]), _head_row(P["a_log"][l])
    dcol = jnp.repeat(P["d_skip"][l], SSD_HD)[:, None]
    xc3 = conv_ssd_fwd(p3, W["conv_w"], P["conv_b"][l][None])
    yc3, states = ssd_fwd(xc3, p3, bias_row, alog_row, dcol)
    y_ssd = ssd_out_fwd(yc3, p3, P["ssd_norm"][l][None])
    cqn = rms_fwd(p3, 512, OFF_CQ // 512, Q_RANK, P["q_norm"][l][None], "rms_q_fwd")
    ckvn = rms_fwd(p3, KV_RANK, OFF_CKV // KV_RANK, KV_RANK, P["kv_norm"][l][None], "rms_kv_fwd")
    qraw = mm(cqn.reshape(T, Q_RANK), W["w_uq"], "nn", "mm_uq")
    kvraw = mm(ckvn.reshape(T, KV_RANK), W["w_ukv"], "nn", "mm_ukv")
    q3 = qprep_fwd(qraw.reshape(B, S, -1), cosf, sinf)
    k3, v3 = kprep_fwd(kvraw.reshape(B, S, -1), p3, cosf, sinf)
    o3, lse, comm_out = attn_fwd(q3, k3, v3, comm)
    if late is not None:
        W = dict(W, **late(comm_out))
    y_att = rms_fwd(o3, D, 0, D, P["attn_norm"][l][None], "rms_o_fwd")
    cat = (y_ssd.reshape(T, D), y_att.reshape(T, D))
    x1, y1 = mm(cat, W["w_out"], "nn", "mm_out", resid=x3.reshape(T, D), gate=mod[:, 2:3, :], seq=S)
    x13 = x1.reshape(B, S, D)
    h2 = normmod_fwd(x13, mod, P["norm_mlp"][l][None], 3, 4)
    u, up_out = mm(h2.reshape(T, D), W["w_up"], "nn", "mm_up", out_dtype=bf16, comm=comm_up), []
    if comm_up is not None:
        u, up_out = u
    u3 = u.reshape(B, S, 2 * D_FF)
    a = glu_fwd(u3, W["conv_ff_w"], P["conv_ff_b"][l][None])
    x2, y2 = mm(a.reshape(T, D_FF), W["w_down"], "nn", "mm_down", resid=x1, gate=mod[:, 5:6, :], seq=S)
    sv.update(x=x3, h=h, p3=p3, xc3=xc3, yc3=yc3, states=states, cqn=cqn, ckvn=ckvn, q3=q3, k3=k3, v3=v3, o3=o3, lse=lse,
              cat=cat, y1=y1, x1=x13, h2=h2, u3=u3, a=a, y2=y2, bias_row=bias_row, alog_row=alog_row, dcol=dcol)
    return x2.reshape(B, S, D), sv, comm_out, W, up_out


def layer_bwd(dx3, sv, mod, W, P, l, cosf, sinf, comm=None, send_rest=False):
    B, S, _ = dx3.shape
    T = B * S
    g = {}
    dy2, dg2 = gate_bwd(dx3, sv["y2"].reshape(B, S, D), mod, 5)
    dy2 = dy2.reshape(T, D)
    da = mm(dy2, W["w_down"], "nt", "mm_down_dx")
    g["w_down"] = mm(sv["a"].reshape(T, D_FF), dy2, "tn", "mm_down_dw")
    dug, duv, dwg, dwv, dbg, dbv = glu_bwd(sv["u3"], da.reshape(B, S, D_FF), W["conv_ff_w"], P["conv_ff_b"][l][None])
    g["conv_ff_w"] = jnp.concatenate([dwg, dwv], axis=1)
    g["conv_ff_b"] = jnp.concatenate([dbg, dbv], axis=1)[0]
    du = (dug.reshape(T, D_FF), duv.reshape(T, D_FF))
    dh2 = mm(du, W["w_up"], "nt", "mm_up_dx")
    g["w_up"] = jnp.concatenate([mm(sv["h2"].reshape(T, D), d, "tn", "mm_up_dw") for d in du], axis=1)
    dx1, dsh2, dsc2, dnm = normmod_bwd(sv["x1"], dh2.reshape(B, S, D), dx3, mod, P["norm_mlp"][l][None], 4)
    g["norm_mlp"] = dnm[0]
    dy1, dg1 = gate_bwd(dx1, sv["y1"].reshape(B, S, D), mod, 2)
    dy1 = dy1.reshape(T, D)
    dcat = mm(dy1, W["w_out"], "nt", "mm_out_dx")
    g["w_out"] = jnp.concatenate([mm(part, dy1, "tn", "mm_out_dw") for part in sv["cat"]], axis=0)
    dcat3 = dcat.reshape(B, S, 2 * D)
    dyc3, dp, dsn = ssd_out_bwd(sv["yc3"], sv["p3"], dcat3, P["ssd_norm"][l][None])
    g["ssd_norm"] = dsn[0]
    do3, dan = rms_bwd(sv["o3"], D, 0, D, dcat3, 1, P["attn_norm"][l][None], f32, "rms_o_bwd")
    g["attn_norm"] = dan[0]
    delta = attn_delta(sv["o3"], do3)
    dq3, dk3, dv3, comm_out = attn_bwd(sv["q3"], sv["k3"], sv["v3"], do3, sv["lse"], delta.reshape(B, MLA_H, 1, S), comm)
    dqraw = qprep_bwd(dq3, cosf, sinf).reshape(T, -1)
    dcqn = mm(dqraw, W["w_uq"], "nt", "mm_uq_dx")
    g["w_uq"] = mm(sv["cqn"].reshape(T, Q_RANK), dqraw, "tn", "mm_uq_dw")
    dp, dqn = rms_bwd_into(sv["p3"], 512, OFF_CQ // 512, Q_RANK, dcqn.reshape(B, S, Q_RANK), P["q_norm"][l][None], dp, "rms_q_bwd")
    g["q_norm"] = dqn[0]
    bias_col, alog_col = sv["bias_row"].reshape(128, 1), sv["alog_row"].reshape(128, 1)
    comm_rest = (layer_grad_slices(g, REST), True) if send_rest else None
    dxc3, ddt3, dalog, dd, dbias, rest_out = ssd_bwd(sv["xc3"], sv["p3"], dyc3, sv["states"], sv["bias_row"], sv["alog_row"],
                                                     bias_col, alog_col, sv["dcol"], comm_rest)
    heads = slice(DT_LANE, DT_LANE + SSD_HEADS)
    g["a_log"], g["d_skip"], g["dt_bias"] = dalog[heads, 0], dd[heads, 0], dbias[heads, 0]
    dp, dcw, dcb = conv_ssd_bwd(sv["p3"], dxc3, W["conv_w"], P["conv_b"][l][None], dp)
    g["conv_w"], g["conv_b"] = dcw, dcb[0]
    dkvraw, dp = kprep_bwd(dk3, dv3, ddt3, cosf, sinf, dp)
    dkvraw = dkvraw.reshape(T, -1)
    dckvn = mm(dkvraw, W["w_ukv"], "nt", "mm_ukv_dx")
    g["w_ukv"] = mm(sv["ckvn"].reshape(T, KV_RANK), dkvraw, "tn", "mm_ukv_dw")
    dp, dkn = rms_bwd_into(sv["p3"], KV_RANK, OFF_CKV // KV_RANK, KV_RANK, dckvn.reshape(B, S, KV_RANK), P["kv_norm"][l][None], dp,
                           "rms_kv_bwd")
    g["kv_norm"] = dkn[0]
    dp = dp.reshape(T, IN_COLS)
    dh = mm(dp, W["w_in"], "nt", "mm_in_dx")
    g["w_in"] = mm(sv["h"].reshape(T, D), dp, "tn", "mm_in_dw")
    dx0, dsh1, dsc1, dnx = normmod_bwd(sv["x"], dh.reshape(B, S, D), dx1, mod, P["norm_mix"][l][None], 1)
    g["norm_mix"] = dnx[0]
    dmod = jnp.concatenate([dsh1, dsc1, dg1, dsh2, dsc2, dg2], axis=1)
    return dx0, dmod, g, comm_out, rest_out


def kernel(x, c, positions, w_ada, b_ada, norm_mix, w_in, conv_w, conv_b, dt_bias, a_log, d_skip, ssd_norm, q_norm, w_uq, kv_norm, w_ukv, attn_norm, w_out, norm_mlp, w_up, conv_ff_w, conv_ff_b, w_down, final_norm, loss_target, m_w_ada, m_b_ada, m_norm_mix, m_w_in, m_conv_w, m_conv_b, m_dt_bias, m_a_log, m_d_skip, m_ssd_norm, m_q_norm, m_w_uq, m_kv_norm, m_w_ukv, m_attn_norm, m_w_out, m_norm_mlp, m_w_up, m_conv_ff_w, m_conv_ff_b, m_w_down, m_final_norm, v_w_ada, v_b_ada, v_norm_mix, v_w_in, v_conv_w, v_conv_b, v_dt_bias, v_a_log, v_d_skip, v_ssd_norm, v_q_norm, v_w_uq, v_kv_norm, v_w_ukv, v_attn_norm, v_w_out, v_norm_mlp, v_w_up, v_conv_ff_w, v_conv_ff_b, v_w_down, v_final_norm):
    given = dict(locals())
    B, S, _ = x.shape
    me = 4 * lax.axis_index("x") + 2 * lax.axis_index("y") + lax.axis_index("c")
    P = {n: given[n] for n in SMALL}

    def shards(l, entries):
        return [given[n][l] if n in CONVS else given[n][l].astype(bf16) for n, _ in entries]

    *gathered, c_all = exchange(shards(0, FIRST) + [c], False, "gather_weights")
    W = [layer_weights(gathered, FIRST), None]

    n_ada = w_ada.shape[2]
    c_all = c_all.reshape(N_DEV * B, D)
    b_sh = lax.dynamic_slice_in_dim(b_ada, me * n_ada, n_ada, axis=1)
    mod_sh = jnp.stack([ada_fwd(c_all, w_ada[l], b_sh[l][None]) for l in range(DEPTH)])
    (mod_g,) = exchange([mod_sh], False, "gather_mod")
    mod_mine = lax.dynamic_slice_in_dim(mod_g, me * B, B, axis=2)
    mods = mod_mine.transpose(1, 2, 0, 3).reshape(DEPTH, B, 6, D)

    inv_freq = jnp.asarray(1.0 / (ROPE_BASE ** (np.arange(0, ROPE, 2, dtype=np.float32) / ROPE)))
    ang = positions.astype(f32)[..., None] * inv_freq
    zeros = jnp.zeros((B, S, 128 - ROPE), f32)
    cosf = jnp.concatenate([jnp.cos(ang), jnp.cos(ang), zeros], axis=-1)
    sinf = jnp.concatenate([jnp.sin(ang), jnp.sin(ang), zeros], axis=-1)

    saved = [None] * DEPTH
    late = lambda got: layer_weights(got, REST)
    xl, saved[0], _, W[0], gathered = layer_fwd(x, mods[0], W[0], P, 0, cosf, sinf, comm=(shards(0, REST), False), late=late,
                                                comm_up=(shards(1, FIRST), False))
    xl, saved[1], _, W[1], _ = layer_fwd(xl, mods[1], layer_weights(gathered, FIRST), P, 1, cosf, sinf,
                                         comm=(shards(1, REST), False), late=late)
    dxl, d_final, loss_part = final_loss(xl, final_norm[None], loss_target)
    grads, dmods, recv = [None] * DEPTH, [None] * DEPTH, [None] * DEPTH
    dxl, dmods[1], grads[1], _, _ = layer_bwd(dxl, saved[1], mods[1], W[1], P, 1, cosf, sinf)
    grad_x, dmods[0], grads[0], recv[1], recv_rest = layer_bwd(dxl, saved[0], mods[0], W[0], P, 0, cosf, sinf,
                                                               comm=(layer_grad_slices(grads[1], BIG), True), send_rest=True)

    stack = lambda n: jnp.stack([grads[l][n] for l in range(DEPTH)])
    small_names = [n for n in SMALL if n not in ("b_ada", "final_norm")]
    partial = pack([stack(n) for n in small_names] + [d_final[0], loss_part[0]])
    dmod_all = jnp.stack(dmods)
    *recv_first, part_g, dmod_g = exchange(layer_grad_slices(grads[0], FIRST) + [partial, dmod_all],
                                           [True] * len(FIRST) + [False, False], "exchange_tail")
    recv[0] = recv_first + recv_rest
    big_g = [jnp.concatenate([sum_slots(recv[l][i][:, None]) for l in range(DEPTH)]) for i in range(len(BIG))]
    small_sum = sum_leading(part_g, "sum_partials")
    small_g = unpack(small_sum, [given[n].shape for n in small_names] + [(D,), (128,)])
    gsmall = dict(zip(small_names + ["final_norm"], small_g[:-1]))
    loss = small_g[-1][0]
    dmod_rows = dmod_g.transpose(0, 2, 1, 3, 4).reshape(N_DEV * B, DEPTH * 6 * D)
    gsmall["b_ada"] = sum_leading(dmod_rows.reshape(N_DEV * B, -1, 128), "sum_b_ada").reshape(DEPTH, 6 * D)
    dmod_cols = dmod_rows.reshape(N_DEV * B, DEPTH, N_DEV, n_ada)
    dmod_sh = lax.dynamic_slice_in_dim(dmod_cols, me, 1, axis=2)[:, :, 0, :]
    g_w_ada = jnp.stack([ada_bwd(c_all, dmod_sh[:, l, :]) for l in range(DEPTH)])

    res = {"grad": {}, "delta": {}, "new_m": {}, "new_v": {}}
    for n, gv in zip([n for n, _ in BIG] + ["w_ada"], big_g + [g_w_ada]):
        res["grad"][n] = gv
        res["delta"][n], res["new_m"][n], res["new_v"][n] = adamw_nd(given[n], gv, given["m_" + n], given["v_" + n])
    shapes = [given[n].shape for n in SMALL]
    flat = adamw(pack([given[n] for n in SMALL]), pack([gsmall[n] for n in SMALL]), pack([given["m_" + n] for n in SMALL]),
                 pack([given["v_" + n] for n in SMALL]))
    for n in SMALL:
        res["grad"][n] = gsmall[n]
    for key, arr in zip(("delta", "new_m", "new_v"), flat):
        res[key].update(zip(SMALL, unpack(arr, shapes)))
    order = ["w_ada", "b_ada", "norm_mix", "w_in", "conv_w", "conv_b", "dt_bias", "a_log", "d_skip", "ssd_norm", "q_norm", "w_uq",
             "kv_norm", "w_ukv", "attn_norm", "w_out", "norm_mlp", "w_up", "conv_ff_w", "conv_ff_b", "w_down", "final_norm"]
    return (loss, grad_x, *[res[k][n] for k in ("grad", "delta", "new_m", "new_v") for n in order])
```

```python
import functools

import numpy as np
import jax
import jax.numpy as jnp
from jax import lax
from jax.experimental import pallas as pl
from jax.experimental.pallas import tpu as pltpu

f32, bf16 = jnp.float32, jnp.bfloat16
HIGHEST = lax.Precision.HIGHEST

D = 1024
D_SSD = 1024
SSD_HEADS = 16
SSD_HD = 64
SSD_N = 128
CHUNK = 128
D_XBC = 1536
CONV_K = 4
MLA_H = 8
NOPE = 128
ROPE = 64
VD = 128
QK = NOPE + ROPE
Q_RANK = 384
KV_RANK = 256
D_FF = 2816
FF_K = 3
EPS = 1e-6
ROPE_BASE = 10000.0
DEPTH = 2
ADAM_LR, ADAM_B1, ADAM_B2, ADAM_EPS, ADAM_WD, ADAM_STEP = 0.001, 0.9, 0.999, 1e-08, 0.01, 10

N_DEV = 8
IN_COLS = 3456
OFF_XBC, OFF_CQ, OFF_CKV, OFF_KRDT = 1024, 2560, 3072, 3328
DT_LANE = 64
VMEM_LIMIT = 48 * 1024 * 1024
MM_K_WHOLE = 4096


def _cparams(n_grid):
    return pltpu.CompilerParams(dimension_semantics=("arbitrary",) * n_grid, vmem_limit_bytes=VMEM_LIMIT)


def _pick(n, cands):
    for c in cands:
        if n % c == 0:
            return c
    return n


def _silu(x):
    return x * jax.nn.sigmoid(x)


def _dsilu(x):
    s = jax.nn.sigmoid(x)
    return s * (1.0 + x * (1.0 - s))


def _rowsum(x):
    return jnp.sum(x, axis=0, keepdims=True)


def mm(a, b, mode, name, out_dtype=f32, resid=None, gate=None, seq=None, comm=None):
    parts = list(a) if isinstance(a, (tuple, list)) else [a]
    np_ = len(parts)
    if mode == "nn":
        (M, Kp), N = parts[0].shape, b.shape[1]
    elif mode == "nt":
        (M, Kp), N = parts[0].shape, b.shape[0]
    else:
        (Kp, M), N = parts[0].shape, b.shape[1]
    K = Kp * np_
    gated = resid is not None
    whole = np_ > 1 or K <= MM_K_WHOLE
    tm = _pick(seq if gated else M, (1024, 1408, 512, 384, 256, 128) if K <= MM_K_WHOLE else (512, 256, 128))
    tn = _pick(N, (512, 1408, 384, 256, 128))
    tk = K if whole else _pick(K, (2816, 2048, 1024, 512))
    if mode == "tn":
        tm = _pick(M, (1024, 1408, 512, 384, 256, 128))
    nk = K // tk
    dims = {"nn": ((1,), (0,)), "nt": ((1,), (1,)), "tn": ((0,), (0,))}[mode]

    def body(*refs):
        a_refs, b_ref, rest = refs[:np_], refs[np_], refs[np_ + 1:]
        if gated:
            r_ref, g_ref, o_ref, y_ref, acc = rest
        else:
            o_ref, acc = rest

        def finish(res):
            if gated:
                y_ref[...] = res
                o_ref[...] = r_ref[...] + g_ref[0] * res
            else:
                o_ref[...] = res.astype(out_dtype)

        prod = None
        for p, a_ref in enumerate(a_refs):
            if np_ == 1:
                bv = b_ref[...]
            else:
                bv = b_ref[:, p * Kp:(p + 1) * Kp] if mode == "nt" else b_ref[p * Kp:(p + 1) * Kp, :]
            term = lax.dot_general(a_ref[...].astype(bf16), bv.astype(bf16), (dims, ((), ())), preferred_element_type=f32)
            prod = term if prod is None else prod + term
        if nk == 1:
            finish(prod)
        else:
            k = pl.program_id(2)

            @pl.when(k == 0)
            def _():
                acc[...] = prod

            @pl.when(k > 0)
            def _():
                acc[...] += prod

            @pl.when(k == nk - 1)
            def _():
                finish(acc[...])

    if np_ > 1:
        a_spec = pl.BlockSpec((tm, Kp), lambda i, j, k: (i, 0))
    elif mode == "tn":
        a_spec = pl.BlockSpec((tk, tm), lambda i, j, k: (k, i))
    else:
        a_spec = pl.BlockSpec((tm, tk), lambda i, j, k: (i, k))
    b_spec = pl.BlockSpec((tn, tk), lambda i, j, k: (j, k)) if mode == "nt" else pl.BlockSpec((tk, tn), lambda i, j, k: (k, j))
    o_spec = pl.BlockSpec((tm, tn), lambda i, j, k: (i, j))
    in_specs, args = [a_spec] * np_ + [b_spec], parts + [b]
    out_specs, out_shape = [o_spec], [jax.ShapeDtypeStruct((M, N), out_dtype)]
    if gated:
        per = seq // tm
        in_specs += [o_spec, pl.BlockSpec((1, 1, tn), lambda i, j, k: (i // per, 0, j))]
        args += [resid, gate]
        out_specs = [o_spec, o_spec]
        out_shape = [jax.ShapeDtypeStruct((M, N), f32), jax.ShapeDtypeStruct((M, N), f32)]
    grid = (M // tm, N // tn, nk)
    body, c_args, c_in, c_out, c_shapes, c_sems = _fuse_exchange(body, comm, len(args), len(out_specs), 1, grid)
    res = pl.pallas_call(body, name=name + "_x" if comm else name, grid=grid, in_specs=in_specs + c_in, out_specs=out_specs + c_out,
                         out_shape=out_shape + c_shapes, scratch_shapes=[pltpu.VMEM((tm, tn), f32)] + c_sems,
                         compiler_params=_cparams(3))(*args, *c_args)
    own = res[:len(out_specs)]
    own = own[0] if len(own) == 1 else tuple(own)
    return (own, list(res[len(out_specs):])) if comm else own


def _tok(ts, width, cb=0):
    return pl.BlockSpec((1, ts, width), lambda b, s: (b, s, cb))


def _perb(rows, width):
    return pl.BlockSpec((1, rows, width), lambda b, s: (b, 0, 0))


def _const(rows, width):
    return pl.BlockSpec((rows, width), lambda b, s: (0, 0))


def _row_call(body, name, B, S, ts, in_specs, out_specs, out_shape, scratch=(), aliases=None):
    return pl.pallas_call(body, name=name, grid=(B, S // ts), in_specs=in_specs, out_specs=out_specs,
                          out_shape=out_shape, scratch_shapes=list(scratch), input_output_aliases=aliases or {},
                          compiler_params=_cparams(2))


def _first():
    return (pl.program_id(0) == 0) & (pl.program_id(1) == 0)


def normmod_fwd(x3, mod, g, i_sh, i_sc):
    B, S, C = x3.shape
    ts = _pick(S, (512, 256, 128))

    def body(x_ref, mod_ref, g_ref, h_ref):
        x = x_ref[0]
        r = lax.rsqrt(jnp.mean(x * x, axis=-1, keepdims=True) + EPS)
        n = x * r * g_ref[...]
        h_ref[0] = (n * (1.0 + mod_ref[0, i_sc:i_sc + 1, :]) + mod_ref[0, i_sh:i_sh + 1, :]).astype(bf16)

    return _row_call(body, "normmod_fwd", B, S, ts, [_tok(ts, C), _perb(6, C), _const(1, C)], _tok(ts, C),
                     jax.ShapeDtypeStruct((B, S, C), bf16))(x3, mod, g)


def normmod_bwd(x3, dh3, resid3, mod, g, i_sc):
    B, S, C = x3.shape
    ts = _pick(S, (512, 256, 128))

    def body(x_ref, dh_ref, r_ref, mod_ref, g_ref, dx_ref, dsh_ref, dsc_ref, dg_ref):
        @pl.when(pl.program_id(1) == 0)
        def _():
            dsh_ref[...] = jnp.zeros_like(dsh_ref)
            dsc_ref[...] = jnp.zeros_like(dsc_ref)

        @pl.when(_first())
        def _():
            dg_ref[...] = jnp.zeros_like(dg_ref)

        x, dh, gv = x_ref[0], dh_ref[0], g_ref[...]
        r = lax.rsqrt(jnp.mean(x * x, axis=-1, keepdims=True) + EPS)
        xh = x * r
        dn = dh * (1.0 + mod_ref[0, i_sc:i_sc + 1, :])
        dsh_ref[0] += _rowsum(dh)
        dsc_ref[0] += _rowsum(dh * xh * gv)
        dg_ref[...] += _rowsum(dn * xh)
        dxh = dn * gv
        dx_ref[0] = r * (dxh - xh * jnp.mean(dxh * xh, axis=-1, keepdims=True)) + r_ref[0]

    return _row_call(body, "normmod_bwd", B, S, ts,
                     [_tok(ts, C), _tok(ts, C), _tok(ts, C), _perb(6, C), _const(1, C)],
                     [_tok(ts, C), _perb(1, C), _perb(1, C), _const(1, C)],
                     [jax.ShapeDtypeStruct((B, S, C), f32), jax.ShapeDtypeStruct((B, 1, C), f32),
                      jax.ShapeDtypeStruct((B, 1, C), f32), jax.ShapeDtypeStruct((1, C), f32)])(x3, dh3, resid3, mod, g)


def gate_bwd(dx3, y3, mod, i_g):
    B, S, C = dx3.shape
    ts = _pick(S, (512, 256, 128))

    def body(dx_ref, y_ref, mod_ref, dy_ref, dgate_ref):
        @pl.when(pl.program_id(1) == 0)
        def _():
            dgate_ref[...] = jnp.zeros_like(dgate_ref)

        dx = dx_ref[0]
        dy_ref[0] = (dx * mod_ref[0, i_g:i_g + 1, :]).astype(bf16)
        dgate_ref[0] += _rowsum(dx * y_ref[0])

    return _row_call(body, "gate_bwd", B, S, ts, [_tok(ts, C), _tok(ts, C), _perb(6, C)], [_tok(ts, C), _perb(1, C)],
                     [jax.ShapeDtypeStruct((B, S, C), bf16), jax.ShapeDtypeStruct((B, 1, C), f32)])(dx3, y3, mod)


def rms_fwd(src3, width, cb, n, g, name):
    B, S, _ = src3.shape
    ts = _pick(S, (512, 256, 128))

    def body(x_ref, g_ref, o_ref):
        x = x_ref[0][:, :n]
        r = lax.rsqrt(jnp.mean(x * x, axis=-1, keepdims=True) + EPS)
        o_ref[0] = (x * r * g_ref[...]).astype(bf16)

    return _row_call(body, name, B, S, ts, [_tok(ts, width, cb), _const(1, n)], _tok(ts, n),
                     jax.ShapeDtypeStruct((B, S, n), bf16))(src3, g)


def rms_bwd(src3, width, cb, n, dout3, dcb, g, out_dtype, name):
    B, S, _ = src3.shape
    ts = _pick(S, (512, 256, 128))

    def body(x_ref, do_ref, g_ref, dx_ref, dg_ref):
        @pl.when(_first())
        def _():
            dg_ref[...] = jnp.zeros_like(dg_ref)

        x = x_ref[0][:, :n]
        do = do_ref[0].astype(f32)
        r = lax.rsqrt(jnp.mean(x * x, axis=-1, keepdims=True) + EPS)
        xh = x * r
        dg_ref[...] += _rowsum(do * xh)
        dxh = do * g_ref[...]
        dx_ref[0] = (r * (dxh - xh * jnp.mean(dxh * xh, axis=-1, keepdims=True))).astype(out_dtype)

    return _row_call(body, name, B, S, ts, [_tok(ts, width, cb), _tok(ts, n, dcb), _const(1, n)],
                     [_tok(ts, n), _const(1, n)],
                     [jax.ShapeDtypeStruct((B, S, n), out_dtype), jax.ShapeDtypeStruct((1, n), f32)])(src3, dout3, g)


def rms_bwd_into(src3, width, cb, n, dout3, g, dp, name):
    B, S, _ = src3.shape
    ts = _pick(S, (512, 256, 128))

    def body(x_ref, do_ref, g_ref, dp_in, dp_ref, dg_ref):
        @pl.when(_first())
        def _():
            dg_ref[...] = jnp.zeros_like(dg_ref)

        x = x_ref[0][:, :n]
        do = do_ref[0]
        r = lax.rsqrt(jnp.mean(x * x, axis=-1, keepdims=True) + EPS)
        xh = x * r
        dg_ref[...] += _rowsum(do * xh)
        dxh = do * g_ref[...]
        dp_ref[0, :, :n] = (r * (dxh - xh * jnp.mean(dxh * xh, axis=-1, keepdims=True))).astype(bf16)
        if width > n:
            dp_ref[0, :, n:] = jnp.zeros((ts, width - n), bf16)

    return _row_call(body, name, B, S, ts, [_tok(ts, width, cb), _tok(ts, n), _const(1, n), pl.BlockSpec(memory_space=pl.ANY)],
                     [_tok(ts, width, cb), _const(1, n)], [jax.ShapeDtypeStruct(dp.shape, bf16), jax.ShapeDtypeStruct((1, n), f32)],
                     aliases={3: 0})(src3, dout3, g, dp)


def final_loss(x3, g, tgt3):
    B, S, C = x3.shape
    ts = _pick(S, (512, 256, 128))

    def body(x_ref, g_ref, t_ref, dx_ref, dg_ref, loss_ref):
        @pl.when(_first())
        def _():
            dg_ref[...] = jnp.zeros_like(dg_ref)
            loss_ref[...] = jnp.zeros_like(loss_ref)

        x, gv = x_ref[0], g_ref[...]
        r = lax.rsqrt(jnp.mean(x * x, axis=-1, keepdims=True) + EPS)
        xh = x * r
        e = xh * gv - t_ref[0]
        loss_ref[...] += 0.5 * jnp.sum(e * e) / C
        dout = e / C
        dg_ref[...] += _rowsum(dout * xh)
        dxh = dout * gv
        dx_ref[0] = r * (dxh - xh * jnp.mean(dxh * xh, axis=-1, keepdims=True))

    return _row_call(body, "final_loss", B, S, ts, [_tok(ts, C), _const(1, C), _tok(ts, C)],
                     [_tok(ts, C), _const(1, C), _const(1, 128)],
                     [jax.ShapeDtypeStruct((B, S, C), f32), jax.ShapeDtypeStruct((1, C), f32),
                      jax.ShapeDtypeStruct((1, 128), f32)])(x3, g, tgt3)


def ssd_out_fwd(yc3, p3, w):
    B, S, C = yc3.shape
    ts = _pick(S, (512, 256, 128))
    half = C // 2

    def body(y_ref, z_ref, w_ref, o_ref):
        y = y_ref[0] * _silu(z_ref[0])
        for lo in (0, half):
            yg = y[:, lo:lo + half]
            r = lax.rsqrt(jnp.mean(yg * yg, axis=-1, keepdims=True) + EPS)
            o_ref[0, :, lo:lo + half] = (yg * r * w_ref[:, lo:lo + half]).astype(bf16)

    return _row_call(body, "ssd_out_fwd", B, S, ts, [_tok(ts, C), _tok(ts, C, 0), _const(1, C)], _tok(ts, C),
                     jax.ShapeDtypeStruct((B, S, C), bf16))(yc3, p3, w)


def ssd_out_bwd(yc3, p3, dcat3, w):
    B, S, C = yc3.shape
    ts = _pick(S, (512, 256, 128))
    half = C // 2

    def body(y_ref, z_ref, do_ref, w_ref, dyc_ref, dz_ref, dw_ref):
        @pl.when(_first())
        def _():
            dw_ref[...] = jnp.zeros_like(dw_ref)

        yc, z, do = y_ref[0], z_ref[0], do_ref[0]
        sz = _silu(z)
        y = yc * sz
        for lo in (0, half):
            sl = slice(lo, lo + half)
            yg, dog, wg = y[:, sl], do[:, sl], w_ref[:, sl]
            r = lax.rsqrt(jnp.mean(yg * yg, axis=-1, keepdims=True) + EPS)
            yh = yg * r
            dw_ref[:, sl] += _rowsum(dog * yh)
            dyh = dog * wg
            dy = r * (dyh - yh * jnp.mean(dyh * yh, axis=-1, keepdims=True))
            dyc_ref[0, :, sl] = dy * sz[:, sl]
            dz_ref[0, :, sl] = (dy * yc[:, sl] * _dsilu(z[:, sl])).astype(bf16)

    return _row_call(body, "ssd_out_bwd", B, S, ts, [_tok(ts, C), _tok(ts, C, 0), _tok(ts, C, 0), _const(1, C)],
                     [_tok(ts, C), _tok(ts, C, 0), _const(1, C)],
                     [jax.ShapeDtypeStruct((B, S, C), f32), jax.ShapeDtypeStruct((B, S, IN_COLS), bf16),
                      jax.ShapeDtypeStruct((1, C), f32)])(yc3, p3, dcat3, w)


def _rot(t):
    lane = lax.broadcasted_iota(jnp.int32, t.shape, 1)
    return jnp.where(lane < ROPE // 2, -pltpu.roll(t, 128 - ROPE // 2, 1), pltpu.roll(t, ROPE // 2, 1))


def _rope(t, cosf, sinf):
    return t * cosf + _rot(t) * sinf


def _rope_t(d, cosf, sinf):
    return d * cosf - _rot(d * sinf)


def qprep_fwd(qraw3, cosf, sinf):
    B, S, W = qraw3.shape
    ts = _pick(S, (512, 256, 128))

    def body(q_ref, c_ref, s_ref, o_ref):
        c, s = c_ref[0], s_ref[0]
        for h in range(MLA_H):
            o_ref[0, :, h * 256:h * 256 + 128] = (q_ref[0, :, h * 256:h * 256 + 128] * Q_FOLD).astype(bf16)
            o_ref[0, :, h * 256 + 128:(h + 1) * 256] = (_rope(q_ref[0, :, h * 256 + 128:(h + 1) * 256], c, s) * Q_FOLD).astype(bf16)

    return _row_call(body, "qprep_fwd", B, S, ts, [_tok(ts, W), _tok(ts, 128), _tok(ts, 128)], _tok(ts, W),
                     jax.ShapeDtypeStruct((B, S, W), bf16))(qraw3, cosf, sinf)


def qprep_bwd(dq3, cosf, sinf):
    B, S, W = dq3.shape
    ts = _pick(S, (512, 256, 128))

    def body(d_ref, c_ref, s_ref, o_ref):
        c, s = c_ref[0], s_ref[0]
        for h in range(MLA_H):
            o_ref[0, :, h * 256:h * 256 + 128] = (d_ref[0, :, h * 256:h * 256 + 128] * Q_FOLD).astype(bf16)
            o_ref[0, :, h * 256 + 128:(h + 1) * 256] = (_rope_t(d_ref[0, :, h * 256 + 128:(h + 1) * 256], c, s) * Q_FOLD).astype(bf16)

    return _row_call(body, "qprep_bwd", B, S, ts, [_tok(ts, W), _tok(ts, 128), _tok(ts, 128)], _tok(ts, W),
                     jax.ShapeDtypeStruct((B, S, W), bf16))(dq3, cosf, sinf)


def kprep_fwd(kv3, p3, cosf, sinf):
    B, S, _ = kv3.shape
    ts = _pick(S, (512, 256, 128))
    Wn = MLA_H * NOPE

    def body(k_ref, v_ref, kr_ref, c_ref, s_ref, ko_ref, vo_ref):
        lane = lax.broadcasted_iota(jnp.int32, (1, 128), 1)
        kr = jnp.where(lane < ROPE, kr_ref[0], 0.0)
        kr = _rope(kr, c_ref[0], s_ref[0]).astype(bf16)
        for h in range(MLA_H):
            ko_ref[0, :, h * 256:h * 256 + 128] = k_ref[0, :, h * 128:(h + 1) * 128].astype(bf16)
            ko_ref[0, :, h * 256 + 128:(h + 1) * 256] = kr
        vo_ref[0] = v_ref[0].astype(bf16)

    return _row_call(body, "kprep_fwd", B, S, ts,
                     [_tok(ts, Wn, 0), _tok(ts, Wn, 1), _tok(ts, 128, OFF_KRDT // 128), _tok(ts, 128), _tok(ts, 128)],
                     [_tok(ts, 2 * Wn), _tok(ts, Wn)],
                     [jax.ShapeDtypeStruct((B, S, 2 * Wn), bf16), jax.ShapeDtypeStruct((B, S, Wn), bf16)])(kv3, kv3, p3, cosf, sinf)


def kprep_bwd(dk3, dv3, ddt3, cosf, sinf, dp):
    B, S, _ = dk3.shape
    ts = _pick(S, (512, 256, 128))
    Wn = MLA_H * NOPE

    def body(dk_ref, dv_ref, ddt_ref, c_ref, s_ref, dp_in, o_ref, kr_ref):
        acc = jnp.zeros((ts, 128), f32)
        for h in range(MLA_H):
            o_ref[0, :, h * 128:(h + 1) * 128] = dk_ref[0, :, h * 256:h * 256 + 128].astype(bf16)
            acc = acc + dk_ref[0, :, h * 256 + 128:(h + 1) * 256]
        o_ref[0, :, Wn:] = dv_ref[0].astype(bf16)
        lane = lax.broadcasted_iota(jnp.int32, (1, 128), 1)
        dkr = _rope_t(acc, c_ref[0], s_ref[0])
        kr_ref[0] = jnp.where(lane < ROPE, dkr, ddt_ref[0]).astype(bf16)

    return _row_call(body, "kprep_bwd", B, S, ts,
                     [_tok(ts, 2 * Wn), _tok(ts, Wn), _tok(ts, 128), _tok(ts, 128), _tok(ts, 128), pl.BlockSpec(memory_space=pl.ANY)],
                     [_tok(ts, 2 * Wn), _tok(ts, 128, OFF_KRDT // 128)],
                     [jax.ShapeDtypeStruct((B, S, 2 * Wn), bf16), jax.ShapeDtypeStruct(dp.shape, bf16)],
                     aliases={5: 1})(dk3, dv3, ddt3, cosf, sinf, dp)


def _shift_down(u, j):
    if j == 0:
        return u
    row = lax.broadcasted_iota(jnp.int32, u.shape, 0)
    return jnp.where(row < j, 0.0, pltpu.roll(u, j, 0))


def _shift_up(u, j):
    if j == 0:
        return u
    n = u.shape[0]
    row = lax.broadcasted_iota(jnp.int32, u.shape, 0)
    return jnp.where(row >= n - j, 0.0, pltpu.roll(u, n - j, 0))


def _conv(u, w, b, K):
    out = b
    for j in range(K):
        out = out + w[K - 1 - j:K - j, :] * _shift_down(u, j)
    return out


def _conv_bwd(u, du, w, K):
    dins = w[K - 1:K, :] * du
    dws = [None] * K
    dws[K - 1] = _rowsum(du * u)
    for j in range(1, K):
        sd = _shift_up(du, j)
        dins = dins + w[K - 1 - j:K - j, :] * sd
        dws[K - 1 - j] = _rowsum(sd * u)
    return dins, dws


CW = 256


def conv_ssd_fwd(p3, w, b):
    B, S, _ = p3.shape
    nb = D_XBC // CW

    def body(u_ref, w_ref, b_ref, o_ref, pre_ref):
        pre = _conv(u_ref[0], w_ref[...], b_ref[...], CONV_K)
        o_ref[0] = _silu(pre)
        pre_ref[0] = pre.astype(bf16)

    out = pl.BlockSpec((1, S, CW), lambda b, j: (b, 0, j))
    return pl.pallas_call(body, name="conv_ssd_fwd", grid=(B, nb),
                          in_specs=[pl.BlockSpec((1, S, CW), lambda b, j: (b, 0, OFF_XBC // CW + j)),
                                    pl.BlockSpec((CONV_K, CW), lambda b, j: (0, j)),
                                    pl.BlockSpec((1, CW), lambda b, j: (0, j))],
                          out_specs=[out, out],
                          out_shape=[jax.ShapeDtypeStruct((B, S, D_XBC), f32), jax.ShapeDtypeStruct((B, S, D_XBC), bf16)],
                          compiler_params=_cparams(2))(p3, w, b)


def conv_ssd_bwd(p3, pre3, dxc3, w, dp):
    B, S, _ = p3.shape
    nb = D_XBC // CW

    def body(u_ref, pre_ref, d_ref, w_ref, dp_in, du_ref, dw_ref, db_ref):
        @pl.when(pl.program_id(1) == 0)
        def _():
            dw_ref[...] = jnp.zeros_like(dw_ref)
            db_ref[...] = jnp.zeros_like(db_ref)

        u, wv = u_ref[0], w_ref[...]
        dpre = d_ref[0] * _dsilu(pre_ref[0].astype(f32))
        dins, dws = _conv_bwd(u, dpre, wv, CONV_K)
        du_ref[0] = dins.astype(bf16)
        for k in range(CONV_K):
            dw_ref[k:k + 1, :] += dws[k]
        db_ref[...] += _rowsum(dpre)

    return pl.pallas_call(body, name="conv_ssd_bwd", grid=(nb, B),
                          in_specs=[pl.BlockSpec((1, S, CW), lambda j, b: (b, 0, OFF_XBC // CW + j)),
                                    pl.BlockSpec((1, S, CW), lambda j, b: (b, 0, j)),
                                    pl.BlockSpec((1, S, CW), lambda j, b: (b, 0, j)),
                                    pl.BlockSpec((CONV_K, CW), lambda j, b: (0, j)), pl.BlockSpec(memory_space=pl.ANY)],
                          out_specs=[pl.BlockSpec((1, S, CW), lambda j, b: (b, 0, OFF_XBC // CW + j)),
                                     pl.BlockSpec((CONV_K, CW), lambda j, b: (0, j)),
                                     pl.BlockSpec((1, CW), lambda j, b: (0, j))],
                          out_shape=[jax.ShapeDtypeStruct(dp.shape, bf16), jax.ShapeDtypeStruct((CONV_K, D_XBC), f32),
                                     jax.ShapeDtypeStruct((1, D_XBC), f32)], input_output_aliases={4: 0},
                          compiler_params=_cparams(2))(p3, pre3, dxc3, w, dp)


def glu_fwd(u3, w, b):
    B, S, _ = u3.shape
    nb = D_FF // CW

    def body(ug_ref, uv_ref, wg_ref, wv_ref, bg_ref, bv_ref, o_ref, g_ref, v_ref):
        g = _conv(ug_ref[0].astype(f32), wg_ref[...], bg_ref[...], FF_K)
        v = _conv(uv_ref[0].astype(f32), wv_ref[...], bv_ref[...], FF_K)
        o_ref[0] = (_silu(g) * v).astype(bf16)
        g_ref[0] = g.astype(bf16)
        v_ref[0] = v.astype(bf16)

    def blk(off):
        return pl.BlockSpec((1, S, CW), lambda b, j: (b, 0, off + j))

    def par(rows, off):
        return pl.BlockSpec((rows, CW), lambda b, j: (0, off + j))

    shp = jax.ShapeDtypeStruct((B, S, D_FF), bf16)
    return pl.pallas_call(body, name="glu_fwd", grid=(B, nb),
                          in_specs=[blk(0), blk(nb), par(FF_K, 0), par(FF_K, nb), par(1, 0), par(1, nb)],
                          out_specs=[blk(0)] * 3, out_shape=[shp] * 3, compiler_params=_cparams(2))(u3, u3, w, w, b, b)


def glu_bwd(u3, g3, v3, da3, w):
    B, S, _ = u3.shape
    nb = D_FF // CW

    def body(ug_ref, uv_ref, g_ref, v_ref, da_ref, wg_ref, wv_ref, dug_ref, duv_ref, dwg_ref, dwv_ref, dbg_ref, dbv_ref):
        @pl.when(pl.program_id(1) == 0)
        def _():
            for r in (dwg_ref, dwv_ref, dbg_ref, dbv_ref):
                r[...] = jnp.zeros_like(r)

        ug, uv, da, wg, wv = ug_ref[0].astype(f32), uv_ref[0].astype(f32), da_ref[0].astype(f32), wg_ref[...], wv_ref[...]
        g, v = g_ref[0].astype(f32), v_ref[0].astype(f32)
        dg = da * v * _dsilu(g)
        dv = da * _silu(g)
        ding, dwsg = _conv_bwd(ug, dg, wg, FF_K)
        dinv, dwsv = _conv_bwd(uv, dv, wv, FF_K)
        dug_ref[0] = ding.astype(bf16)
        duv_ref[0] = dinv.astype(bf16)
        for k in range(FF_K):
            dwg_ref[k:k + 1, :] += dwsg[k]
            dwv_ref[k:k + 1, :] += dwsv[k]
        dbg_ref[...] += _rowsum(dg)
        dbv_ref[...] += _rowsum(dv)

    def blk(off):
        return pl.BlockSpec((1, S, CW), lambda j, b: (b, 0, off + j))

    def par(rows, off):
        return pl.BlockSpec((rows, CW), lambda j, b: (0, off + j))

    return pl.pallas_call(body, name="glu_bwd", grid=(nb, B),
                          in_specs=[blk(0), blk(nb), blk(0), blk(0), blk(0), par(FF_K, 0), par(FF_K, nb)],
                          out_specs=[blk(0), blk(0), par(FF_K, 0), par(FF_K, 0), par(1, 0), par(1, 0)],
                          out_shape=[jax.ShapeDtypeStruct((B, S, D_FF), bf16), jax.ShapeDtypeStruct((B, S, D_FF), bf16),
                                     jax.ShapeDtypeStruct((FF_K, D_FF), f32), jax.ShapeDtypeStruct((FF_K, D_FF), f32),
                                     jax.ShapeDtypeStruct((1, D_FF), f32), jax.ShapeDtypeStruct((1, D_FF), f32)],
                          compiler_params=_cparams(2))(u3, u3, g3, v3, da3, w, w)


def _ssd_decay(dtb, bias_row, alog_row):
    lane = lax.broadcasted_iota(jnp.int32, (1, 128), 1)
    hmask = (lane >= DT_LANE) & (lane < DT_LANE + SSD_HEADS)
    dt = jnp.where(hmask, jax.nn.softplus(dtb + bias_row), 0.0)
    a = dt * jnp.where(hmask, -jnp.exp(alog_row), 0.0)
    r = lax.broadcasted_iota(jnp.int32, (CHUNK, CHUNK), 0)
    c = lax.broadcasted_iota(jnp.int32, (CHUNK, CHUNK), 1)
    cs = jnp.dot((r >= c).astype(f32), a, precision=HIGHEST, preferred_element_type=f32)
    return dt, cs


def _expand(xt):
    return jnp.concatenate([jnp.broadcast_to(xt[DT_LANE + h:DT_LANE + h + 1, :], (SSD_HD, xt.shape[1]))
                            for h in range(SSD_HEADS)], axis=0)


_NT = (((1,), (1,)), ((), ()))
_TN = (((0,), (0,)), ((), ()))
GH = SSD_HEADS // 2
GR = GH * SSD_HD


def ssd_fwd(xc3, p3, bias_row, alog_row, dcol):
    B, S, _ = xc3.shape
    nc = S // CHUNK

    def body(xs_ref, bc_ref, dtb_ref, bias_ref, alog_ref, dcol_ref, y_ref, st_ref, state, yT):
        @pl.when(pl.program_id(1) == 0)
        def _():
            state[...] = jnp.zeros_like(state)

        dt, cs = _ssd_decay(dtb_ref[0], bias_ref[...], alog_ref[...])
        csT = cs.T
        eT = jnp.exp(csT)
        decX = _expand(jnp.exp(csT[:, CHUNK - 1:CHUNK] - csT))
        eX = _expand(eT)
        elastX = eX[:, CHUNK - 1:CHUNK]
        xsT = xs_ref[0].T
        uT = xsT * _expand(dt.T)
        bc = bc_ref[0]
        st_ref[0, 0] = state[...]
        srow = lax.broadcasted_iota(jnp.int32, (CHUNK, CHUNK), 0)
        lcol = lax.broadcasted_iota(jnp.int32, (CHUNK, CHUNK), 1)
        for g in range(2):
            Bg = bc[:, g * SSD_N:(g + 1) * SSD_N].astype(bf16)
            Cg = bc[:, (2 + g) * SSD_N:(3 + g) * SSD_N].astype(bf16)
            GT = lax.dot_general(Bg, Cg, _NT, preferred_element_type=f32)
            rows = slice(g * GR, (g + 1) * GR)
            Sg = state[rows]
            yoffT = lax.dot_general(Sg.astype(bf16), Cg, _NT, preferred_element_type=f32) * eX[rows]
            state[rows] = Sg * elastX[rows] + jnp.dot((uT[rows] * decX[rows]).astype(bf16), Bg, preferred_element_type=f32)
            for k in range(GH):
                h = g * GH + k
                hr = slice(h * SSD_HD, (h + 1) * SSD_HD)
                seg = csT[DT_LANE + h:DT_LANE + h + 1, :] - cs[:, DT_LANE + h:DT_LANE + h + 1]
                LT = jnp.where(lcol >= srow, jnp.exp(jnp.minimum(seg, 0.0)), 0.0)
                yT[hr] = (jnp.dot(uT[hr].astype(bf16), (GT * LT).astype(bf16), preferred_element_type=f32)
                          + yoffT[k * SSD_HD:(k + 1) * SSD_HD] + dcol_ref[hr] * xsT[hr])
        y_ref[0] = yT[...].T

    return pl.pallas_call(body, name="ssd_fwd", grid=(B, nc),
                          in_specs=[pl.BlockSpec((1, CHUNK, D_SSD), lambda b, c: (b, c, 0)),
                                    pl.BlockSpec((1, CHUNK, 512), lambda b, c: (b, c, 2)),
                                    pl.BlockSpec((1, CHUNK, 128), lambda b, c: (b, c, OFF_KRDT // 128)),
                                    _const(1, 128), _const(1, 128), _const(D_SSD, 1)],
                          out_specs=[pl.BlockSpec((1, CHUNK, D_SSD), lambda b, c: (b, c, 0)),
                                     pl.BlockSpec((1, 1, D_SSD, SSD_N), lambda b, c: (b, c, 0, 0))],
                          out_shape=[jax.ShapeDtypeStruct((B, S, D_SSD), f32), jax.ShapeDtypeStruct((B, nc, D_SSD, SSD_N), f32)],
                          scratch_shapes=[pltpu.VMEM((D_SSD, SSD_N), f32), pltpu.VMEM((D_SSD, CHUNK), f32)],
                          compiler_params=_cparams(2))(xc3, xc3, p3, bias_row, alog_row, dcol)


def ssd_bwd(xc3, p3, dy3, states, bias_row, alog_row, bias_col, alog_col, dcol, comm=None):
    B, S, _ = xc3.shape
    nc = S // CHUNK

    def body(xs_ref, bc_ref, dtb_ref, dy_ref, st_ref, bias_ref, alog_ref, biasc_ref, alogc_ref, dcol_ref,
             dxc_ref, ddt_ref, dalog_ref, dd_ref, dbias_ref, dS, dUT, accA, accD, accB, dcs_diag):
        @pl.when(pl.program_id(1) == 0)
        def _():
            dS[...] = jnp.zeros_like(dS)

        @pl.when(_first())
        def _():
            accA[...] = jnp.zeros_like(accA)
            accD[...] = jnp.zeros_like(accD)
            accB[...] = jnp.zeros_like(accB)

        dtb = dtb_ref[0]
        dt, cs = _ssd_decay(dtb, bias_ref[...], alog_ref[...])
        dtT, csT = dt.T, cs.T
        decX = _expand(jnp.exp(csT[:, CHUNK - 1:CHUNK] - csT))
        eX = _expand(jnp.exp(csT))
        dtX = _expand(dtT)
        elastX = eX[:, CHUNK - 1:CHUNK]
        xsT = xs_ref[0].T
        uT = xsT * dtX
        dYT = dy_ref[0].T
        bc = bc_ref[0]
        lrow = lax.broadcasted_iota(jnp.int32, (CHUNK, CHUNK), 0)
        scol = lax.broadcasted_iota(jnp.int32, (CHUNK, CHUNK), 1)
        dcs_diag[...] = jnp.zeros_like(dcs_diag)
        rs_cols = jnp.zeros((CHUNK, 128), f32)
        vparts, zparts = [], []
        for g in range(2):
            Bf = bc[:, g * SSD_N:(g + 1) * SSD_N]
            Bg = Bf.astype(bf16)
            Cg = bc[:, (2 + g) * SSD_N:(3 + g) * SSD_N].astype(bf16)
            G = lax.dot_general(Cg, Bg, _NT, preferred_element_type=f32)
            BgT = Bf.T.astype(bf16)
            rows = slice(g * GR, (g + 1) * GR)
            dSg = dS[rows]
            Sg = st_ref[0, 0, rows, :]
            dUst = jnp.dot(dSg.astype(bf16), BgT, preferred_element_type=f32) * decX[rows]
            yoffT = lax.dot_general(Sg.astype(bf16), Cg, _NT, preferred_element_type=f32) * eX[rows]
            zparts.append(dYT[rows] * yoffT - dUst * uT[rows])
            dG = jnp.zeros((CHUNK, CHUNK), f32)
            for k in range(GH):
                h = g * GH + k
                hr = slice(h * SSD_HD, (h + 1) * SSD_HD)
                seg = cs[:, DT_LANE + h:DT_LANE + h + 1] - csT[DT_LANE + h:DT_LANE + h + 1, :]
                L = jnp.where(lrow >= scol, jnp.exp(jnp.minimum(seg, 0.0)), 0.0)
                M = G * L
                dYh = dYT[hr].astype(bf16)
                dUT[hr] = jnp.dot(dYh, M.astype(bf16), preferred_element_type=f32) + dUst[k * SSD_HD:(k + 1) * SSD_HD]
                dM = lax.dot_general(dYh, uT[hr].astype(bf16), _TN, preferred_element_type=f32)
                dG = dG + dM * L
                Wm = dM * M
                rs_cols = jnp.where(scol == DT_LANE + h, jnp.sum(Wm, axis=1, keepdims=True), rs_cols)
                dcs_diag[DT_LANE + h:DT_LANE + h + 1, :] = -_rowsum(Wm)
            dGb = dG.astype(bf16)
            dYe = (dYT[rows] * eX[rows]).astype(bf16)
            ude = (uT[rows] * decX[rows]).astype(bf16)
            dC = jnp.dot(dGb, Bg, preferred_element_type=f32) + lax.dot_general(dYe, Sg.astype(bf16), _TN, preferred_element_type=f32)
            dB = (lax.dot_general(dGb, Cg, _TN, preferred_element_type=f32)
                  + lax.dot_general(ude, dSg.astype(bf16), _TN, preferred_element_type=f32))
            dxc_ref[0, :, D_SSD + g * SSD_N:D_SSD + (g + 1) * SSD_N] = dB
            dxc_ref[0, :, D_SSD + (2 + g) * SSD_N:D_SSD + (3 + g) * SSD_N] = dC
            vparts.append(elastX[rows] * jnp.sum(dSg * Sg, axis=1, keepdims=True)
                          + jnp.sum(dUst * uT[rows], axis=1, keepdims=True))
            dS[rows] = elastX[rows] * dSg + jnp.dot(dYe, Cg, preferred_element_type=f32)
        dU = dUT[...]
        dcv = dcol_ref[...]
        dxc_ref[0, :, 0:D_SSD] = (dtX * dU + dcv * dYT).T
        lane = lax.broadcasted_iota(jnp.int32, (D_SSD, CHUNK), 1)
        Z = jnp.concatenate(zparts, axis=0) + jnp.where(lane == CHUNK - 1, jnp.concatenate(vparts, axis=0), 0.0)
        hr_ = lax.broadcasted_iota(jnp.int32, (128, D_SSD), 0)
        hc_ = lax.broadcasted_iota(jnp.int32, (128, D_SSD), 1)
        hsel = (hr_ - DT_LANE == jnp.right_shift(hc_, 6)).astype(bf16)
        summands = jnp.concatenate([Z, dU * xsT, dYT * xsT], axis=1)
        hi = summands.astype(bf16)
        lo = (summands - hi.astype(f32)).astype(bf16)
        red = jnp.dot(hsel, hi, preferred_element_type=f32) + jnp.dot(hsel, lo, preferred_element_type=f32)
        dcsT = red[:, 0:CHUNK] + dcs_diag[...] + rs_cols.T
        daT = jnp.dot(dcsT, (lrow >= scol).astype(f32), precision=HIGHEST, preferred_element_type=f32)
        rowi = lax.broadcasted_iota(jnp.int32, (128, 1), 0)
        hmask = (rowi >= DT_LANE) & (rowi < DT_LANE + SSD_HEADS)
        a_col = jnp.where(hmask, -jnp.exp(alogc_ref[...]), 0.0)
        ddtT = red[:, CHUNK:2 * CHUNK] + a_col * daT
        ddt_rawT = jnp.where(hmask, ddtT * jax.nn.sigmoid(dtb.T + biasc_ref[...]), 0.0)
        ddt_ref[0] = ddt_rawT.T
        accA[...] += daT * dtT
        accD[...] += red[:, 2 * CHUNK:3 * CHUNK]
        accB[...] += ddt_rawT

        @pl.when((pl.program_id(0) == B - 1) & (pl.program_id(1) == nc - 1))
        def _():
            dalog_ref[...] = jnp.broadcast_to(jnp.sum(accA[...], axis=1, keepdims=True) * a_col, (128, 128))
            dd_ref[...] = jnp.broadcast_to(jnp.sum(accD[...], axis=1, keepdims=True), (128, 128))
            dbias_ref[...] = jnp.broadcast_to(jnp.sum(accB[...], axis=1, keepdims=True), (128, 128))

    def rev(width, cb):
        return pl.BlockSpec((1, CHUNK, width), lambda b, c: (b, nc - 1 - c, cb))

    acc_spec = pl.BlockSpec((128, 128), lambda b, c: (0, 0))
    acc_shape = jax.ShapeDtypeStruct((128, 128), f32)
    body, c_args, c_in, c_out, c_shapes, c_sems = _fuse_exchange(body, comm, 10, 5, 6, (B, nc))
    res = pl.pallas_call(body, name="ssd_bwd_x" if comm else "ssd_bwd", grid=(B, nc),
                         in_specs=[rev(D_SSD, 0), rev(512, 2), rev(128, OFF_KRDT // 128), rev(D_SSD, 0),
                                   pl.BlockSpec((1, 1, D_SSD, SSD_N), lambda b, c: (b, nc - 1 - c, 0, 0)),
                                   _const(1, 128), _const(1, 128), _const(128, 1), _const(128, 1), _const(D_SSD, 1)] + c_in,
                         out_specs=[rev(D_XBC, 0), rev(128, 0), acc_spec, acc_spec, acc_spec] + c_out,
                         out_shape=[jax.ShapeDtypeStruct((B, S, D_XBC), f32), jax.ShapeDtypeStruct((B, S, 128), f32),
                                    acc_shape, acc_shape, acc_shape] + c_shapes,
                         scratch_shapes=[pltpu.VMEM((D_SSD, SSD_N), f32), pltpu.VMEM((D_SSD, CHUNK), f32),
                                         pltpu.VMEM((128, 128), f32), pltpu.VMEM((128, 128), f32), pltpu.VMEM((128, 128), f32),
                                         pltpu.VMEM((128, 128), f32)] + c_sems,
                         compiler_params=_cparams(2))(xc3, xc3, p3, dy3, states, bias_row, alog_row, bias_col, alog_col, dcol, *c_args)
    return (*res[:5], list(res[5:]))


ATT_SCALE = float(QK) ** -0.5
LOG2E = 1.4426950408889634
LN2 = 0.6931471805599453
Q_FOLD = ATT_SCALE * LOG2E
NEG = -1e30
HP = 2


def _att_block(S):
    return _pick(S, (512, 256, 128))


def attn_fwd(q3, k3, v3, comm=None):
    B, S, _ = q3.shape
    bq = _att_block(S)
    nq = S // bq

    def body(q_ref, k_ref, v_ref, o_ref, lse_ref, m_s, l_s, acc):
        i, j = pl.program_id(2), pl.program_id(3)

        @pl.when(j == 0)
        def _():
            m_s[...] = jnp.full_like(m_s, NEG)
            l_s[...] = jnp.zeros_like(l_s)
            acc[...] = jnp.zeros_like(acc)

        def step(masked):
            for t in range(HP):
                qk = slice(t * 256, (t + 1) * 256)
                st = lax.dot_general(k_ref[0, :, qk], q_ref[0, :, qk], _NT, preferred_element_type=f32)
                if masked:
                    r = lax.broadcasted_iota(jnp.int32, (bq, bq), 0)
                    c = lax.broadcasted_iota(jnp.int32, (bq, bq), 1)
                    st = jnp.where(c >= r, st, NEG)
                m_old = m_s[t]
                m_new = jnp.maximum(m_old, jnp.max(st, axis=0, keepdims=True))
                alpha = jnp.exp2(m_old - m_new)
                pt = jnp.exp2(st - m_new)
                l_s[t] = alpha * l_s[t] + jnp.sum(pt, axis=0, keepdims=True)
                acc[t] = alpha * acc[t] + lax.dot_general(v_ref[0, :, t * VD:(t + 1) * VD], pt.astype(bf16), _TN,
                                                          preferred_element_type=f32)
                m_s[t] = m_new

        @pl.when(j < i)
        def _():
            step(False)

        @pl.when(j == i)
        def _():
            step(True)
            for t in range(HP):
                o_ref[0, :, t * VD:(t + 1) * VD] = (acc[t] / l_s[t]).T
                lse_ref[0, t] = m_s[t] + jnp.log2(l_s[t])

    grid = (B, MLA_H // HP, nq, nq)
    body, c_args, c_in, c_out, c_shapes, c_sems = _fuse_exchange(body, comm, 3, 2, 3, grid)
    res = pl.pallas_call(body, name="attn_fwd_x" if comm else "attn_fwd", grid=grid,
                         in_specs=[pl.BlockSpec((1, bq, HP * 256), lambda b, h, i, j: (b, i, h)),
                                   pl.BlockSpec((1, bq, HP * 256), lambda b, h, i, j: (b, jnp.minimum(j, i), h)),
                                   pl.BlockSpec((1, bq, HP * VD), lambda b, h, i, j: (b, jnp.minimum(j, i), h))] + c_in,
                         out_specs=[pl.BlockSpec((1, bq, HP * VD), lambda b, h, i, j: (b, i, h)),
                                    pl.BlockSpec((1, HP, 1, bq), lambda b, h, i, j: (b, h, 0, i))] + c_out,
                         out_shape=[jax.ShapeDtypeStruct((B, S, MLA_H * VD), f32), jax.ShapeDtypeStruct((B, MLA_H, 1, S), f32)] + c_shapes,
                         scratch_shapes=[pltpu.VMEM((HP, 1, bq), f32), pltpu.VMEM((HP, 1, bq), f32), pltpu.VMEM((HP, VD, bq), f32)] + c_sems,
                         compiler_params=_cparams(4))(q3, k3, v3, *c_args)
    return res[0], res[1], list(res[2:])


def attn_delta(o3, do3):
    B, S, _ = o3.shape
    ts = _pick(S, (512, 256, 128))

    def body(o_ref, do_ref, d_ref):
        for h in range(MLA_H):
            vs = slice(h * VD, (h + 1) * VD)
            d_ref[0, h] = jnp.sum(o_ref[0, :, vs] * do_ref[0, :, vs], axis=-1, keepdims=True)

    return _row_call(body, "attn_delta", B, S, ts, [_tok(ts, MLA_H * VD), _tok(ts, MLA_H * VD)],
                     pl.BlockSpec((1, MLA_H, ts, 1), lambda b, s: (b, 0, s, 0)),
                     jax.ShapeDtypeStruct((B, MLA_H, S, 1), f32))(o3, do3)


def attn_bwd(q3, k3, v3, do3, lse_row, delta_row, comm=None):
    B, S, _ = q3.shape
    bq = _att_block(S)
    nq = S // bq

    def body(q_ref, k_ref, v_ref, do_ref, lse_ref, dl_ref, dk_ref, dv_ref, dq_hbm, dk_acc, dv_acc, dq_scr, dq_sem):
        b, hp, j, i = pl.program_id(0), pl.program_id(1), pl.program_id(2), pl.program_id(3)
        rows = pl.ds(pl.multiple_of(i * bq, bq), bq)

        @pl.when((j == 0) & (i == 0))
        def _():
            dq_scr[...] = jnp.zeros_like(dq_scr)

        @pl.when(i == 0)
        def _():
            dk_acc[...] = jnp.zeros_like(dk_acc)
            dv_acc[...] = jnp.zeros_like(dv_acc)

        def step(masked):
            for t in range(HP):
                qk, vs = slice(t * 256, (t + 1) * 256), slice(t * VD, (t + 1) * VD)
                q, k = q_ref[0, :, qk], k_ref[0, :, qk]
                do = do_ref[0, :, vs].astype(bf16)
                pt = jnp.exp2(lax.dot_general(k, q, _NT, preferred_element_type=f32) - lse_ref[0, t])
                if masked:
                    r = lax.broadcasted_iota(jnp.int32, (bq, bq), 0)
                    c = lax.broadcasted_iota(jnp.int32, (bq, bq), 1)
                    pt = jnp.where(c >= r, pt, 0.0)
                dv_acc[t] += jnp.dot(pt.astype(bf16), do, preferred_element_type=f32)
                dpt = lax.dot_general(v_ref[0, :, vs], do, _NT, preferred_element_type=f32)
                dst = (pt * (dpt - dl_ref[0, t])).astype(bf16)
                dk_acc[t] += jnp.dot(dst, q, preferred_element_type=f32)
                dq_scr[t, rows, :] += lax.dot_general(dst, k, _TN, preferred_element_type=f32)

        @pl.when(i > j)
        def _():
            step(False)

        @pl.when(i == j)
        def _():
            step(True)
            for t in range(HP):
                dq_scr[t, rows, :] = dq_scr[t, rows, :] * LN2
                cp = pltpu.make_async_copy(dq_scr.at[t, rows, :],
                                           dq_hbm.at[b, rows, pl.ds(pl.multiple_of((hp * HP + t) * 256, 256), 256)], dq_sem.at[t])
                cp.start()
                cp.wait()

        @pl.when(i == nq - 1)
        def _():
            for t in range(HP):
                dk_ref[0, :, t * 256:(t + 1) * 256] = dk_acc[t] * LN2
                dv_ref[0, :, t * VD:(t + 1) * VD] = dv_acc[t]

    kspec = pl.BlockSpec((1, bq, HP * 256), lambda b, h, j, i: (b, j, h))
    vspec = pl.BlockSpec((1, bq, HP * VD), lambda b, h, j, i: (b, j, h))
    rspec = pl.BlockSpec((1, HP, 1, bq), lambda b, h, j, i: (b, h, 0, jnp.maximum(i, j)))
    grid = (B, MLA_H // HP, nq, nq)
    body, c_args, c_in, c_out, c_shapes, c_sems = _fuse_exchange(body, comm, 6, 3, 4, grid)
    res = pl.pallas_call(body, name="attn_bwd_x" if comm else "attn_bwd", grid=grid,
                         in_specs=[pl.BlockSpec((1, bq, HP * 256), lambda b, h, j, i: (b, jnp.maximum(i, j), h)), kspec, vspec,
                                   pl.BlockSpec((1, bq, HP * VD), lambda b, h, j, i: (b, jnp.maximum(i, j), h)), rspec, rspec] + c_in,
                         out_specs=[kspec, vspec, pl.BlockSpec(memory_space=pltpu.HBM)] + c_out,
                         out_shape=[jax.ShapeDtypeStruct((B, S, MLA_H * 256), f32), jax.ShapeDtypeStruct((B, S, MLA_H * VD), f32),
                                    jax.ShapeDtypeStruct((B, S, MLA_H * 256), f32)] + c_shapes,
                         scratch_shapes=[pltpu.VMEM((HP, bq, 256), f32), pltpu.VMEM((HP, bq, VD), f32), pltpu.VMEM((HP, S, 256), f32),
                                         pltpu.SemaphoreType.DMA((HP,))] + c_sems,
                         compiler_params=_cparams(4))(q3, k3, v3, do3, lse_row, delta_row, *c_args)
    return res[2], res[0], res[1], list(res[3:])


def ada_fwd(c_all, w, b):
    n = w.shape[1]

    def body(c_ref, w_ref, b_ref, o_ref):
        o_ref[...] = jnp.dot(_silu(c_ref[...]).astype(bf16), w_ref[...].astype(bf16), preferred_element_type=f32) + b_ref[...]

    return pl.pallas_call(body, name="ada_fwd", out_shape=jax.ShapeDtypeStruct((c_all.shape[0], n), f32),
                          compiler_params=pltpu.CompilerParams(vmem_limit_bytes=VMEM_LIMIT))(c_all, w, b)


def ada_bwd(c_all, dmod):
    n = dmod.shape[1]

    def body(c_ref, d_ref, o_ref):
        o_ref[...] = lax.dot_general(_silu(c_ref[...]).astype(bf16), d_ref[...].astype(bf16), _TN, preferred_element_type=f32)

    return pl.pallas_call(body, name="ada_bwd", out_shape=jax.ShapeDtypeStruct((c_all.shape[1], n), f32),
                          compiler_params=pltpu.CompilerParams(vmem_limit_bytes=VMEM_LIMIT))(c_all, dmod)


def sum_leading(x, name):
    n, R, _ = x.shape
    tr = _pick(R, (512, 256, 128, 64, 32, 16, 8))

    def body(x_ref, o_ref):
        acc = x_ref[0].astype(f32)
        for k in range(1, n):
            acc = acc + x_ref[k].astype(f32)
        o_ref[...] = acc

    return pl.pallas_call(body, name=name, grid=(R // tr,), in_specs=[pl.BlockSpec((n, tr, 128), lambda i: (0, i, 0))],
                          out_specs=pl.BlockSpec((tr, 128), lambda i: (i, 0)), out_shape=jax.ShapeDtypeStruct((R, 128), f32),
                          compiler_params=_cparams(1))(x)


def _adamw_body(w_ref, g_ref, m_ref, v_ref, d_ref, mo_ref, vo_ref):
    gv = g_ref[...]
    mn = ADAM_B1 * m_ref[...] + (1.0 - ADAM_B1) * gv
    vn = ADAM_B2 * v_ref[...] + (1.0 - ADAM_B2) * jnp.square(gv)
    m_hat = mn / (1.0 - ADAM_B1 ** ADAM_STEP)
    v_hat = vn / (1.0 - ADAM_B2 ** ADAM_STEP)
    d_ref[...] = -ADAM_LR * (m_hat / (jnp.sqrt(v_hat) + ADAM_EPS) + ADAM_WD * w_ref[...])
    mo_ref[...] = mn
    vo_ref[...] = vn


def adamw(w, g, m, v):
    R = w.shape[0]
    tr = _pick(R, (512, 256, 128, 64, 32, 16, 8))
    spec = pl.BlockSpec((tr, 128), lambda i: (i, 0))
    shp = jax.ShapeDtypeStruct((R, 128), f32)
    return pl.pallas_call(functools.partial(_adamw_body), name="adamw", grid=(R // tr,), in_specs=[spec] * 4,
                          out_specs=[spec] * 3, out_shape=[shp] * 3, compiler_params=_cparams(1))(w, g, m, v)


def _row_tile(a):
    return _pick(a, (256, 128, 64, 32, 16, 8)) if a % 8 == 0 else a


def adamw_nd(w, g, m, v):
    L, a, b = w.shape
    ta = _row_tile(a)
    spec = pl.BlockSpec((1, ta, b), lambda l, i: (l, i, 0))
    shp = jax.ShapeDtypeStruct((L, a, b), f32)
    return pl.pallas_call(functools.partial(_adamw_body), name="adamw_nd", grid=(L, a // ta), in_specs=[spec] * 4,
                          out_specs=[spec] * 3, out_shape=[shp] * 3, compiler_params=_cparams(2))(w, g, m, v)


def sum_slots(x):
    n, L, a, b = x.shape
    ta = _row_tile(a)

    def body(x_ref, o_ref):
        acc = x_ref[0].astype(f32)
        for k in range(1, n):
            acc = acc + x_ref[k].astype(f32)
        o_ref[...] = acc

    return pl.pallas_call(body, name="sum_slots", grid=(L, a // ta),
                          in_specs=[pl.BlockSpec((n, 1, ta, b), lambda l, i: (0, l, i, 0))],
                          out_specs=pl.BlockSpec((1, ta, b), lambda l, i: (l, i, 0)),
                          out_shape=jax.ShapeDtypeStruct((L, a, b), f32), compiler_params=_cparams(2))(x)


def _exchange_copies(ins, outs, sems, scatter):
    send_sems, recv_sems, local_sems = sems
    x, y, c = lax.axis_index("x"), lax.axis_index("y"), lax.axis_index("c")
    me = 4 * x + 2 * y + c
    locals_, sends, recvs = [], [], []
    for a in range(len(ins)):
        locals_.append(pltpu.make_async_copy(ins[a].at[me] if scatter[a] else ins[a], outs[a].at[me], local_sems.at[a]))
        for k in range(N_DEV - 1):
            px = 1 - x if (k + 1) & 4 else x
            py = 1 - y if (k + 1) & 2 else y
            pc = 1 - c if (k + 1) & 1 else c
            pid = 4 * px + 2 * py + pc
            src = ins[a].at[pid] if scatter[a] else ins[a]
            for slot, group in ((me, sends), (pid, recvs)):
                group.append(pltpu.make_async_remote_copy(src_ref=src, dst_ref=outs[a].at[slot], send_sem=send_sems.at[a, k],
                                                          recv_sem=recv_sems.at[a, k], device_id=(px, py, pc),
                                                          device_id_type=pl.DeviceIdType.MESH))
    return locals_, sends, recvs


def _exchange_start(ins, outs, sems, scatter):
    locals_, sends, _ = _exchange_copies(ins, outs, sems, scatter)
    for cp in locals_ + sends:
        cp.start()


def _exchange_wait(ins, outs, sems, scatter):
    locals_, sends, recvs = _exchange_copies(ins, outs, sems, scatter)
    for cp in recvs:
        cp.wait_recv()
    for cp in sends:
        cp.wait_send()
    for cp in locals_:
        cp.wait()


def _exchange_shapes(arrays, scatter):
    return [jax.ShapeDtypeStruct((N_DEV,) + tuple(a.shape[1:] if s else a.shape), a.dtype) for a, s in zip(arrays, scatter)]


def _flags(scatter, n):
    return [scatter] * n if isinstance(scatter, bool) else list(scatter)


def _exchange_sems(n):
    return [pltpu.SemaphoreType.DMA((n, N_DEV - 1)), pltpu.SemaphoreType.DMA((n, N_DEV - 1)), pltpu.SemaphoreType.DMA((n,))]


def _fuse_exchange(core, comm, n_in, n_out, n_scr, grid):
    if comm is None:
        return core, [], [], [], [], []
    arrays, scatter = comm
    n = len(arrays)
    scatter = _flags(scatter, n)

    def body(*refs):
        a, b, c = n_in + n, n_in + n + n_out, n_in + 2 * n + n_out
        cin, cout, sems = refs[n_in:a], refs[b:c], refs[c + n_scr:]
        ids = [pl.program_id(d) for d in range(len(grid))]
        first = functools.reduce(lambda p, q: p & q, [i == 0 for i in ids])
        last = functools.reduce(lambda p, q: p & q, [i == g - 1 for i, g in zip(ids, grid)])

        @pl.when(first)
        def _():
            _exchange_start(cin, cout, sems, scatter)

        core(*refs[:n_in], *refs[a:b], *refs[c:c + n_scr])

        @pl.when(last)
        def _():
            _exchange_wait(cin, cout, sems, scatter)

    hbm = pl.BlockSpec(memory_space=pltpu.HBM)
    return body, list(arrays), [hbm] * n, [hbm] * n, _exchange_shapes(arrays, scatter), _exchange_sems(n)


def exchange(arrays, scatter, name):
    n = len(arrays)
    scatter = _flags(scatter, n)

    def body(*refs):
        ins, outs, sems = refs[:n], refs[n:2 * n], refs[2 * n:]
        _exchange_start(ins, outs, sems, scatter)
        _exchange_wait(ins, outs, sems, scatter)

    hbm = pl.BlockSpec(memory_space=pltpu.HBM)
    return pl.pallas_call(body, name=name, in_specs=[hbm] * n, out_specs=[hbm] * n,
                          out_shape=_exchange_shapes(arrays, scatter), scratch_shapes=_exchange_sems(n))(*arrays)


BIG = (("w_in", "col"), ("conv_w", "col"), ("w_uq", "col"), ("w_ukv", "col"), ("w_out", "row"), ("w_up", "col"),
       ("conv_ff_w", "col"), ("w_down", "row"))
SMALL = ("b_ada", "norm_mix", "conv_b", "dt_bias", "a_log", "d_skip", "ssd_norm", "q_norm", "kv_norm", "attn_norm",
         "norm_mlp", "conv_ff_b", "final_norm")
CONVS = ("conv_w", "conv_ff_w")
PACK_ALIGN = 2048


def _padded(n):
    return -(-n // PACK_ALIGN) * PACK_ALIGN


def _flat_pad(a):
    f = a.reshape(-1)
    return jnp.pad(f, (0, _padded(f.shape[0]) - f.shape[0]))


PACK_ROWS = 512


def pack(arrs):
    f = jnp.concatenate([_flat_pad(a) for a in arrs])
    n = PACK_ROWS * 128
    return jnp.pad(f, (0, -(-f.shape[0] // n) * n - f.shape[0])).reshape(-1, 128)


def unpack(flat, shapes):
    f = flat.reshape(-1)
    out, off = [], 0
    for s in shapes:
        n = int(np.prod(s))
        out.append(f[off:off + n].reshape(s))
        off += _padded(n)
    return out


def shards_to_full(g, kind):
    _, a, b = g.shape
    if kind == "col":
        return g.transpose(1, 0, 2).reshape(a, N_DEV * b)
    return g.reshape(N_DEV * a, b)


def full_to_shards(full, kind):
    if kind == "col":
        a, nb = full.shape
        return full.reshape(a, N_DEV, nb // N_DEV).transpose(1, 0, 2)
    na, b = full.shape
    return full.reshape(N_DEV, na // N_DEV, b)


def w_in_layout(w):
    z = lambda n: jnp.zeros(w.shape[:-1] + (n,), w.dtype)
    return jnp.concatenate([w[..., :2560], w[..., 2576:2960], z(128), w[..., 2960:3216], w[..., 3216:3280],
                            w[..., 2560:2576], z(48)], axis=-1)


def w_in_unlayout(g):
    return jnp.concatenate([g[..., :2560], g[..., 3392:3408], g[..., 2560:2944], g[..., 3072:3328], g[..., 3328:3392]], axis=-1)


def w_uq_layout(w):
    return jnp.pad(w.reshape(Q_RANK, MLA_H, QK), ((0, 0), (0, 0), (0, 256 - QK))).reshape(Q_RANK, MLA_H * 256)


def w_uq_unlayout(g):
    return g.reshape(Q_RANK, MLA_H, 256)[:, :, :QK].reshape(Q_RANK, MLA_H * QK)


def w_ukv_layout(w):
    return w.reshape(KV_RANK, MLA_H, 2, 128).transpose(0, 2, 1, 3).reshape(KV_RANK, 2 * MLA_H * 128)


def w_ukv_unlayout(g):
    return g.reshape(KV_RANK, 2, MLA_H, 128).transpose(0, 2, 1, 3).reshape(KV_RANK, 2 * MLA_H * 128)


LAYOUTS = {"w_in": (w_in_layout, w_in_unlayout), "w_uq": (w_uq_layout, w_uq_unlayout), "w_ukv": (w_ukv_layout, w_ukv_unlayout)}
FIRST, REST = BIG[:4], BIG[4:]


def layer_weights(gathered, entries):
    full = {n: shards_to_full(g, kind) for (n, kind), g in zip(entries, gathered)}
    return {n: LAYOUTS[n][0](w) if n in LAYOUTS else w for n, w in full.items()}


def layer_grad_slices(g, entries):
    return [full_to_shards(LAYOUTS[n][1](g[n]) if n in LAYOUTS else g[n], kind).astype(bf16) for n, kind in entries]


def _head_row(v):
    return jnp.zeros((1, 128), f32).at[0, DT_LANE:DT_LANE + SSD_HEADS].set(v)


def layer_fwd(x3, mod, W, P, l, cosf, sinf, comm=None, late=None, comm_up=None):
    B, S, _ = x3.shape
    T = B * S
    sv = {}
    h = normmod_fwd(x3, mod, P["norm_mix"][l][None], 0, 1)
    p = mm(h.reshape(T, D), W["w_in"], "nn", "mm_in")
    p3 = p.reshape(B, S, IN_COLS)
    bias_row, alog_row = _head_row(P["dt_bias"][l]), _head_row(P["a_log"][l])
    dcol = jnp.repeat(P["d_skip"][l], SSD_HD)[:, None]
    xc3, xpre = conv_ssd_fwd(p3, W["conv_w"], P["conv_b"][l][None])
    yc3, states = ssd_fwd(xc3, p3, bias_row, alog_row, dcol)
    y_ssd = ssd_out_fwd(yc3, p3, P["ssd_norm"][l][None])
    cqn = rms_fwd(p3, 512, OFF_CQ // 512, Q_RANK, P["q_norm"][l][None], "rms_q_fwd")
    ckvn = rms_fwd(p3, KV_RANK, OFF_CKV // KV_RANK, KV_RANK, P["kv_norm"][l][None], "rms_kv_fwd")
    qraw = mm(cqn.reshape(T, Q_RANK), W["w_uq"], "nn", "mm_uq")
    kvraw = mm(ckvn.reshape(T, KV_RANK), W["w_ukv"], "nn", "mm_ukv")
    q3 = qprep_fwd(qraw.reshape(B, S, -1), cosf, sinf)
    k3, v3 = kprep_fwd(kvraw.reshape(B, S, -1), p3, cosf, sinf)
    o3, lse, comm_out = attn_fwd(q3, k3, v3, comm)
    if late is not None:
        W = dict(W, **late(comm_out))
    y_att = rms_fwd(o3, D, 0, D, P["attn_norm"][l][None], "rms_o_fwd")
    cat = (y_ssd.reshape(T, D), y_att.reshape(T, D))
    x1, y1 = mm(cat, W["w_out"], "nn", "mm_out", resid=x3.reshape(T, D), gate=mod[:, 2:3, :], seq=S)
    x13 = x1.reshape(B, S, D)
    h2 = normmod_fwd(x13, mod, P["norm_mlp"][l][None], 3, 4)
    u, up_out = mm(h2.reshape(T, D), W["w_up"], "nn", "mm_up", out_dtype=bf16, comm=comm_up), []
    if comm_up is not None:
        u, up_out = u
    u3 = u.reshape(B, S, 2 * D_FF)
    a, ffg, ffv = glu_fwd(u3, W["conv_ff_w"], P["conv_ff_b"][l][None])
    x2, y2 = mm(a.reshape(T, D_FF), W["w_down"], "nn", "mm_down", resid=x1, gate=mod[:, 5:6, :], seq=S)
    sv.update(x=x3, h=h, p3=p3, xc3=xc3, xpre=xpre, yc3=yc3, states=states, cqn=cqn, ckvn=ckvn, q3=q3, k3=k3, v3=v3, o3=o3, lse=lse,
              cat=cat, y1=y1, x1=x13, h2=h2, u3=u3, ffg=ffg, ffv=ffv, a=a, y2=y2, bias_row=bias_row, alog_row=alog_row, dcol=dcol)
    return x2.reshape(B, S, D), sv, comm_out, W, up_out


def layer_bwd(dx3, sv, mod, W, P, l, cosf, sinf, comm=None, send_rest=False):
    B, S, _ = dx3.shape
    T = B * S
    g = {}
    dy2, dg2 = gate_bwd(dx3, sv["y2"].reshape(B, S, D), mod, 5)
    dy2 = dy2.reshape(T, D)
    da = mm(dy2, W["w_down"], "nt", "mm_down_dx", out_dtype=bf16)
    g["w_down"] = mm(sv["a"].reshape(T, D_FF), dy2, "tn", "mm_down_dw")
    dug, duv, dwg, dwv, dbg, dbv = glu_bwd(sv["u3"], sv["ffg"], sv["ffv"], da.reshape(B, S, D_FF), W["conv_ff_w"])
    g["conv_ff_w"] = jnp.concatenate([dwg, dwv], axis=1)
    g["conv_ff_b"] = jnp.concatenate([dbg, dbv], axis=1)[0]
    du = (dug.reshape(T, D_FF), duv.reshape(T, D_FF))
    dh2 = mm(du, W["w_up"], "nt", "mm_up_dx")
    g["w_up"] = jnp.concatenate([mm(sv["h2"].reshape(T, D), d, "tn", "mm_up_dw") for d in du], axis=1)
    dx1, dsh2, dsc2, dnm = normmod_bwd(sv["x1"], dh2.reshape(B, S, D), dx3, mod, P["norm_mlp"][l][None], 4)
    g["norm_mlp"] = dnm[0]
    dy1, dg1 = gate_bwd(dx1, sv["y1"].reshape(B, S, D), mod, 2)
    dy1 = dy1.reshape(T, D)
    dcat = mm(dy1, W["w_out"], "nt", "mm_out_dx")
    g["w_out"] = jnp.concatenate([mm(part, dy1, "tn", "mm_out_dw") for part in sv["cat"]], axis=0)
    dcat3 = dcat.reshape(B, S, 2 * D)
    dyc3, dp, dsn = ssd_out_bwd(sv["yc3"], sv["p3"], dcat3, P["ssd_norm"][l][None])
    g["ssd_norm"] = dsn[0]
    do3, dan = rms_bwd(sv["o3"], D, 0, D, dcat3, 1, P["attn_norm"][l][None], f32, "rms_o_bwd")
    g["attn_norm"] = dan[0]
    delta = attn_delta(sv["o3"], do3)
    dq3, dk3, dv3, comm_out = attn_bwd(sv["q3"], sv["k3"], sv["v3"], do3, sv["lse"], delta.reshape(B, MLA_H, 1, S), comm)
    dqraw = qprep_bwd(dq3, cosf, sinf).reshape(T, -1)
    dcqn = mm(dqraw, W["w_uq"], "nt", "mm_uq_dx")
    g["w_uq"] = mm(sv["cqn"].reshape(T, Q_RANK), dqraw, "tn", "mm_uq_dw")
    dp, dqn = rms_bwd_into(sv["p3"], 512, OFF_CQ // 512, Q_RANK, dcqn.reshape(B, S, Q_RANK), P["q_norm"][l][None], dp, "rms_q_bwd")
    g["q_norm"] = dqn[0]
    bias_col, alog_col = sv["bias_row"].reshape(128, 1), sv["alog_row"].reshape(128, 1)
    comm_rest = (layer_grad_slices(g, REST), True) if send_rest else None
    dxc3, ddt3, dalog, dd, dbias, rest_out = ssd_bwd(sv["xc3"], sv["p3"], dyc3, sv["states"], sv["bias_row"], sv["alog_row"],
                                                     bias_col, alog_col, sv["dcol"], comm_rest)
    heads = slice(DT_LANE, DT_LANE + SSD_HEADS)
    g["a_log"], g["d_skip"], g["dt_bias"] = dalog[heads, 0], dd[heads, 0], dbias[heads, 0]
    dp, dcw, dcb = conv_ssd_bwd(sv["p3"], sv["xpre"], dxc3, W["conv_w"], dp)
    g["conv_w"], g["conv_b"] = dcw, dcb[0]
    dkvraw, dp = kprep_bwd(dk3, dv3, ddt3, cosf, sinf, dp)
    dkvraw = dkvraw.reshape(T, -1)
    dckvn = mm(dkvraw, W["w_ukv"], "nt", "mm_ukv_dx")
    g["w_ukv"] = mm(sv["ckvn"].reshape(T, KV_RANK), dkvraw, "tn", "mm_ukv_dw")
    dp, dkn = rms_bwd_into(sv["p3"], KV_RANK, OFF_CKV // KV_RANK, KV_RANK, dckvn.reshape(B, S, KV_RANK), P["kv_norm"][l][None], dp,
                           "rms_kv_bwd")
    g["kv_norm"] = dkn[0]
    dp = dp.reshape(T, IN_COLS)
    dh = mm(dp, W["w_in"], "nt", "mm_in_dx")
    g["w_in"] = mm(sv["h"].reshape(T, D), dp, "tn", "mm_in_dw")
    dx0, dsh1, dsc1, dnx = normmod_bwd(sv["x"], dh.reshape(B, S, D), dx1, mod, P["norm_mix"][l][None], 1)
    g["norm_mix"] = dnx[0]
    dmod = jnp.concatenate([dsh1, dsc1, dg1, dsh2, dsc2, dg2], axis=1)
    return dx0, dmod, g, comm_out, rest_out


def kernel(x, c, positions, w_ada, b_ada, norm_mix, w_in, conv_w, conv_b, dt_bias, a_log, d_skip, ssd_norm, q_norm, w_uq, kv_norm, w_ukv, attn_norm, w_out, norm_mlp, w_up, conv_ff_w, conv_ff_b, w_down, final_norm, loss_target, m_w_ada, m_b_ada, m_norm_mix, m_w_in, m_conv_w, m_conv_b, m_dt_bias, m_a_log, m_d_skip, m_ssd_norm, m_q_norm, m_w_uq, m_kv_norm, m_w_ukv, m_attn_norm, m_w_out, m_norm_mlp, m_w_up, m_conv_ff_w, m_conv_ff_b, m_w_down, m_final_norm, v_w_ada, v_b_ada, v_norm_mix, v_w_in, v_conv_w, v_conv_b, v_dt_bias, v_a_log, v_d_skip, v_ssd_norm, v_q_norm, v_w_uq, v_kv_norm, v_w_ukv, v_attn_norm, v_w_out, v_norm_mlp, v_w_up, v_conv_ff_w, v_conv_ff_b, v_w_down, v_final_norm):
    given = dict(locals())
    B, S, _ = x.shape
    me = 4 * lax.axis_index("x") + 2 * lax.axis_index("y") + lax.axis_index("c")
    P = {n: given[n] for n in SMALL}

    def shards(l, entries):
        return [given[n][l] if n in CONVS else given[n][l].astype(bf16) for n, _ in entries]

    *gathered, c_all = exchange(shards(0, FIRST) + [c], False, "gather_weights")
    W = [layer_weights(gathered, FIRST), None]

    n_ada = w_ada.shape[2]
    c_all = c_all.reshape(N_DEV * B, D)
    b_sh = lax.dynamic_slice_in_dim(b_ada, me * n_ada, n_ada, axis=1)
    mod_sh = jnp.stack([ada_fwd(c_all, w_ada[l], b_sh[l][None]) for l in range(DEPTH)])
    (mod_g,) = exchange([mod_sh], False, "gather_mod")
    mod_mine = lax.dynamic_slice_in_dim(mod_g, me * B, B, axis=2)
    mods = mod_mine.transpose(1, 2, 0, 3).reshape(DEPTH, B, 6, D)

    inv_freq = jnp.asarray(1.0 / (ROPE_BASE ** (np.arange(0, ROPE, 2, dtype=np.float32) / ROPE)))
    ang = positions.astype(f32)[..., None] * inv_freq
    zeros = jnp.zeros((B, S, 128 - ROPE), f32)
    cosf = jnp.concatenate([jnp.cos(ang), jnp.cos(ang), zeros], axis=-1)
    sinf = jnp.concatenate([jnp.sin(ang), jnp.sin(ang), zeros], axis=-1)

    saved = [None] * DEPTH
    late = lambda got: layer_weights(got, REST)
    xl, saved[0], _, W[0], gathered = layer_fwd(x, mods[0], W[0], P, 0, cosf, sinf, comm=(shards(0, REST), False), late=late,
                                                comm_up=(shards(1, FIRST), False))
    xl, saved[1], _, W[1], _ = layer_fwd(xl, mods[1], layer_weights(gathered, FIRST), P, 1, cosf, sinf,
                                         comm=(shards(1, REST), False), late=late)
    dxl, d_final, loss_part = final_loss(xl, final_norm[None], loss_target)
    grads, dmods, recv = [None] * DEPTH, [None] * DEPTH, [None] * DEPTH
    dxl, dmods[1], grads[1], _, _ = layer_bwd(dxl, saved[1], mods[1], W[1], P, 1, cosf, sinf)
    grad_x, dmods[0], grads[0], recv[1], recv_rest = layer_bwd(dxl, saved[0], mods[0], W[0], P, 0, cosf, sinf,
                                                               comm=(layer_grad_slices(grads[1], BIG), True), send_rest=True)

    stack = lambda n: jnp.stack([grads[l][n] for l in range(DEPTH)])
    small_names = [n for n in SMALL if n not in ("b_ada", "final_norm")]
    partial = pack([stack(n) for n in small_names] + [d_final[0], loss_part[0]])
    dmod_all = jnp.stack(dmods)
    *recv_first, part_g, dmod_g = exchange(layer_grad_slices(grads[0], FIRST) + [partial, dmod_all],
                                           [True] * len(FIRST) + [False, False], "exchange_tail")
    recv[0] = recv_first + recv_rest
    big_g = [jnp.concatenate([sum_slots(recv[l][i][:, None]) for l in range(DEPTH)]) for i in range(len(BIG))]
    small_sum = sum_leading(part_g, "sum_partials")
    small_g = unpack(small_sum, [given[n].shape for n in small_names] + [(D,), (128,)])
    gsmall = dict(zip(small_names + ["final_norm"], small_g[:-1]))
    loss = small_g[-1][0]
    dmod_rows = dmod_g.transpose(0, 2, 1, 3, 4).reshape(N_DEV * B, DEPTH * 6 * D)
    gsmall["b_ada"] = sum_leading(dmod_rows.reshape(N_DEV * B, -1, 128), "sum_b_ada").reshape(DEPTH, 6 * D)
    dmod_cols = dmod_rows.reshape(N_DEV * B, DEPTH, N_DEV, n_ada)
    dmod_sh = lax.dynamic_slice_in_dim(dmod_cols, me, 1, axis=2)[:, :, 0, :]
    g_w_ada = jnp.stack([ada_bwd(c_all, dmod_sh[:, l, :]) for l in range(DEPTH)])

    res = {"grad": {}, "delta": {}, "new_m": {}, "new_v": {}}
    for n, gv in zip([n for n, _ in BIG] + ["w_ada"], big_g + [g_w_ada]):
        res["grad"][n] = gv
        res["delta"][n], res["new_m"][n], res["new_v"][n] = adamw_nd(given[n], gv, given["m_" + n], given["v_" + n])
    shapes = [given[n].shape for n in SMALL]
    flat = adamw(pack([given[n] for n in SMALL]), pack([gsmall[n] for n in SMALL]), pack([given["m_" + n] for n in SMALL]),
                 pack([given["v_" + n] for n in SMALL]))
    for n in SMALL:
        res["grad"][n] = gsmall[n]
    for key, arr in zip(("delta", "new_m", "new_v"), flat):
        res[key].update(zip(SMALL, unpack(arr, shapes)))
    order = ["w_ada", "b_ada", "norm_mix", "w_in", "conv_w", "conv_b", "dt_bias", "a_log", "d_skip", "ssd_norm", "q_norm", "w_uq",
             "kv_norm", "w_ukv", "attn_norm", "w_out", "norm_mlp", "w_up", "conv_ff_w", "conv_ff_b", "w_down", "final_norm"]
    return (loss, grad_x, *[res[k][n] for k in ("grad", "delta", "new_m", "new_v") for n in order])
```

```python
import functools

import numpy as np
import jax
import jax.numpy as jnp
from jax import lax
from jax.experimental import pallas as pl
from jax.experimental.pallas import tpu as pltpu

f32, bf16 = jnp.float32, jnp.bfloat16
HIGHEST = lax.Precision.HIGHEST

D = 1024
D_SSD = 1024
SSD_HEADS = 16
SSD_HD = 64
SSD_N = 128
CHUNK = 128
D_XBC = 1536
CONV_K = 4
MLA_H = 8
NOPE = 128
ROPE = 64
VD = 128
QK = NOPE + ROPE
Q_RANK = 384
KV_RANK = 256
D_FF = 2816
FF_K = 3
EPS = 1e-6
ROPE_BASE = 10000.0
DEPTH = 2
ADAM_LR, ADAM_B1, ADAM_B2, ADAM_EPS, ADAM_WD, ADAM_STEP = 0.001, 0.9, 0.999, 1e-08, 0.01, 10

N_DEV = 8
IN_COLS = 3456
OFF_XBC, OFF_CQ, OFF_CKV, OFF_KRDT = 1024, 2560, 3072, 3328
DT_LANE = 64
VMEM_LIMIT = 48 * 1024 * 1024
MM_K_WHOLE = 4096


def _cparams(n_grid):
    return pltpu.CompilerParams(dimension_semantics=("arbitrary",) * n_grid, vmem_limit_bytes=VMEM_LIMIT)


def _pick(n, cands):
    for c in cands:
        if n % c == 0:
            return c
    return n


def _silu(x):
    return x * jax.nn.sigmoid(x)


def _dsilu(x):
    s = jax.nn.sigmoid(x)
    return s * (1.0 + x * (1.0 - s))


def _rowsum(x):
    return jnp.sum(x, axis=0, keepdims=True)


def mm(a, b, mode, name, out_dtype=f32, resid=None, gate=None, seq=None, comm=None):
    parts = list(a) if isinstance(a, (tuple, list)) else [a]
    np_ = len(parts)
    if mode == "nn":
        (M, Kp), N = parts[0].shape, b.shape[1]
    elif mode == "nt":
        (M, Kp), N = parts[0].shape, b.shape[0]
    else:
        (Kp, M), N = parts[0].shape, b.shape[1]
    K = Kp * np_
    gated = resid is not None
    whole = np_ > 1 or K <= MM_K_WHOLE
    tm = _pick(seq if gated else M, (1024, 1408, 512, 384, 256, 128) if K <= MM_K_WHOLE else (512, 256, 128))
    tn = _pick(N, (512, 1408, 384, 256, 128))
    tk = K if whole else _pick(K, (2816, 2048, 1024, 512))
    if mode == "tn":
        tm = _pick(M, (1024, 1408, 512, 384, 256, 128))
    nk = K // tk
    dims = {"nn": ((1,), (0,)), "nt": ((1,), (1,)), "tn": ((0,), (0,))}[mode]

    def body(*refs):
        a_refs, b_ref, rest = refs[:np_], refs[np_], refs[np_ + 1:]
        if gated:
            r_ref, g_ref, o_ref, y_ref, acc = rest
        else:
            o_ref, acc = rest

        def finish(res):
            if gated:
                y_ref[...] = res
                o_ref[...] = r_ref[...] + g_ref[0] * res
            else:
                o_ref[...] = res.astype(out_dtype)

        prod = None
        for p, a_ref in enumerate(a_refs):
            if np_ == 1:
                bv = b_ref[...]
            else:
                bv = b_ref[:, p * Kp:(p + 1) * Kp] if mode == "nt" else b_ref[p * Kp:(p + 1) * Kp, :]
            term = lax.dot_general(a_ref[...].astype(bf16), bv.astype(bf16), (dims, ((), ())), preferred_element_type=f32)
            prod = term if prod is None else prod + term
        if nk == 1:
            finish(prod)
        else:
            k = pl.program_id(2)

            @pl.when(k == 0)
            def _():
                acc[...] = prod

            @pl.when(k > 0)
            def _():
                acc[...] += prod

            @pl.when(k == nk - 1)
            def _():
                finish(acc[...])

    if np_ > 1:
        a_spec = pl.BlockSpec((tm, Kp), lambda i, j, k: (i, 0))
    elif mode == "tn":
        a_spec = pl.BlockSpec((tk, tm), lambda i, j, k: (k, i))
    else:
        a_spec = pl.BlockSpec((tm, tk), lambda i, j, k: (i, k))
    b_spec = pl.BlockSpec((tn, tk), lambda i, j, k: (j, k)) if mode == "nt" else pl.BlockSpec((tk, tn), lambda i, j, k: (k, j))
    o_spec = pl.BlockSpec((tm, tn), lambda i, j, k: (i, j))
    in_specs, args = [a_spec] * np_ + [b_spec], parts + [b]
    out_specs, out_shape = [o_spec], [jax.ShapeDtypeStruct((M, N), out_dtype)]
    if gated:
        per = seq // tm
        in_specs += [o_spec, pl.BlockSpec((1, 1, tn), lambda i, j, k: (i // per, 0, j))]
        args += [resid, gate]
        out_specs = [o_spec, o_spec]
        out_shape = [jax.ShapeDtypeStruct((M, N), f32), jax.ShapeDtypeStruct((M, N), f32)]
    grid = (M // tm, N // tn, nk)
    body, c_args, c_in, c_out, c_shapes, c_sems = _fuse_exchange(body, comm, len(args), len(out_specs), 1, grid)
    res = pl.pallas_call(body, name=name + "_x" if comm else name, grid=grid, in_specs=in_specs + c_in, out_specs=out_specs + c_out,
                         out_shape=out_shape + c_shapes, scratch_shapes=[pltpu.VMEM((tm, tn), f32)] + c_sems,
                         compiler_params=_cparams(3))(*args, *c_args)
    own = res[:len(out_specs)]
    own = own[0] if len(own) == 1 else tuple(own)
    return (own, list(res[len(out_specs):])) if comm else own


def _tok(ts, width, cb=0):
    return pl.BlockSpec((1, ts, width), lambda b, s: (b, s, cb))


def _perb(rows, width):
    return pl.BlockSpec((1, rows, width), lambda b, s: (b, 0, 0))


def _const(rows, width):
    return pl.BlockSpec((rows, width), lambda b, s: (0, 0))


def _row_call(body, name, B, S, ts, in_specs, out_specs, out_shape, scratch=(), aliases=None):
    return pl.pallas_call(body, name=name, grid=(B, S // ts), in_specs=in_specs, out_specs=out_specs,
                          out_shape=out_shape, scratch_shapes=list(scratch), input_output_aliases=aliases or {},
                          compiler_params=_cparams(2))


def _first():
    return (pl.program_id(0) == 0) & (pl.program_id(1) == 0)


def normmod_fwd(x3, mod, g, i_sh, i_sc):
    B, S, C = x3.shape
    ts = _pick(S, (512, 256, 128))

    def body(x_ref, mod_ref, g_ref, h_ref):
        x = x_ref[0]
        r = lax.rsqrt(jnp.mean(x * x, axis=-1, keepdims=True) + EPS)
        n = x * r * g_ref[...]
        h_ref[0] = (n * (1.0 + mod_ref[0, i_sc:i_sc + 1, :]) + mod_ref[0, i_sh:i_sh + 1, :]).astype(bf16)

    return _row_call(body, "normmod_fwd", B, S, ts, [_tok(ts, C), _perb(6, C), _const(1, C)], _tok(ts, C),
                     jax.ShapeDtypeStruct((B, S, C), bf16))(x3, mod, g)


def normmod_bwd(x3, dh3, resid3, mod, g, i_sc):
    B, S, C = x3.shape
    ts = _pick(S, (512, 256, 128))

    def body(x_ref, dh_ref, r_ref, mod_ref, g_ref, dx_ref, dsh_ref, dsc_ref, dg_ref):
        @pl.when(pl.program_id(1) == 0)
        def _():
            dsh_ref[...] = jnp.zeros_like(dsh_ref)
            dsc_ref[...] = jnp.zeros_like(dsc_ref)

        @pl.when(_first())
        def _():
            dg_ref[...] = jnp.zeros_like(dg_ref)

        x, dh, gv = x_ref[0], dh_ref[0], g_ref[...]
        r = lax.rsqrt(jnp.mean(x * x, axis=-1, keepdims=True) + EPS)
        xh = x * r
        dn = dh * (1.0 + mod_ref[0, i_sc:i_sc + 1, :])
        dsh_ref[0] += _rowsum(dh)
        dsc_ref[0] += _rowsum(dh * xh * gv)
        dg_ref[...] += _rowsum(dn * xh)
        dxh = dn * gv
        dx_ref[0] = r * (dxh - xh * jnp.mean(dxh * xh, axis=-1, keepdims=True)) + r_ref[0]

    return _row_call(body, "normmod_bwd", B, S, ts,
                     [_tok(ts, C), _tok(ts, C), _tok(ts, C), _perb(6, C), _const(1, C)],
                     [_tok(ts, C), _perb(1, C), _perb(1, C), _const(1, C)],
                     [jax.ShapeDtypeStruct((B, S, C), f32), jax.ShapeDtypeStruct((B, 1, C), f32),
                      jax.ShapeDtypeStruct((B, 1, C), f32), jax.ShapeDtypeStruct((1, C), f32)])(x3, dh3, resid3, mod, g)


def gate_bwd(dx3, y3, mod, i_g):
    B, S, C = dx3.shape
    ts = _pick(S, (512, 256, 128))

    def body(dx_ref, y_ref, mod_ref, dy_ref, dgate_ref):
        @pl.when(pl.program_id(1) == 0)
        def _():
            dgate_ref[...] = jnp.zeros_like(dgate_ref)

        dx = dx_ref[0]
        dy_ref[0] = (dx * mod_ref[0, i_g:i_g + 1, :]).astype(bf16)
        dgate_ref[0] += _rowsum(dx * y_ref[0])

    return _row_call(body, "gate_bwd", B, S, ts, [_tok(ts, C), _tok(ts, C), _perb(6, C)], [_tok(ts, C), _perb(1, C)],
                     [jax.ShapeDtypeStruct((B, S, C), bf16), jax.ShapeDtypeStruct((B, 1, C), f32)])(dx3, y3, mod)


def rms_fwd(src3, width, cb, n, g, name):
    B, S, _ = src3.shape
    ts = _pick(S, (512, 256, 128))

    def body(x_ref, g_ref, o_ref):
        x = x_ref[0][:, :n]
        r = lax.rsqrt(jnp.mean(x * x, axis=-1, keepdims=True) + EPS)
        o_ref[0] = (x * r * g_ref[...]).astype(bf16)

    return _row_call(body, name, B, S, ts, [_tok(ts, width, cb), _const(1, n)], _tok(ts, n),
                     jax.ShapeDtypeStruct((B, S, n), bf16))(src3, g)


def rms_bwd_into(src3, width, cb, n, dout3, g, dp, name):
    B, S, _ = src3.shape
    ts = _pick(S, (512, 256, 128))

    def body(x_ref, do_ref, g_ref, dp_in, dp_ref, dg_ref):
        @pl.when(_first())
        def _():
            dg_ref[...] = jnp.zeros_like(dg_ref)

        x = x_ref[0][:, :n]
        do = do_ref[0]
        r = lax.rsqrt(jnp.mean(x * x, axis=-1, keepdims=True) + EPS)
        xh = x * r
        dg_ref[...] += _rowsum(do * xh)
        dxh = do * g_ref[...]
        dp_ref[0, :, :n] = (r * (dxh - xh * jnp.mean(dxh * xh, axis=-1, keepdims=True))).astype(bf16)
        if width > n:
            dp_ref[0, :, n:] = jnp.zeros((ts, width - n), bf16)

    return _row_call(body, name, B, S, ts, [_tok(ts, width, cb), _tok(ts, n), _const(1, n), pl.BlockSpec(memory_space=pl.ANY)],
                     [_tok(ts, width, cb), _const(1, n)], [jax.ShapeDtypeStruct(dp.shape, bf16), jax.ShapeDtypeStruct((1, n), f32)],
                     aliases={3: 0})(src3, dout3, g, dp)


def final_loss(x3, g, tgt3):
    B, S, C = x3.shape
    ts = _pick(S, (512, 256, 128))

    def body(x_ref, g_ref, t_ref, dx_ref, dg_ref, loss_ref):
        @pl.when(_first())
        def _():
            dg_ref[...] = jnp.zeros_like(dg_ref)
            loss_ref[...] = jnp.zeros_like(loss_ref)

        x, gv = x_ref[0], g_ref[...]
        r = lax.rsqrt(jnp.mean(x * x, axis=-1, keepdims=True) + EPS)
        xh = x * r
        e = xh * gv - t_ref[0]
        loss_ref[...] += 0.5 * jnp.sum(e * e) / C
        dout = e / C
        dg_ref[...] += _rowsum(dout * xh)
        dxh = dout * gv
        dx_ref[0] = r * (dxh - xh * jnp.mean(dxh * xh, axis=-1, keepdims=True))

    return _row_call(body, "final_loss", B, S, ts, [_tok(ts, C), _const(1, C), _tok(ts, C)],
                     [_tok(ts, C), _const(1, C), _const(1, 128)],
                     [jax.ShapeDtypeStruct((B, S, C), f32), jax.ShapeDtypeStruct((1, C), f32),
                      jax.ShapeDtypeStruct((1, 128), f32)])(x3, g, tgt3)


def ssd_out_fwd(yc3, p3, w):
    B, S, C = yc3.shape
    ts = _pick(S, (512, 256, 128))
    half = C // 2

    def body(y_ref, z_ref, w_ref, o_ref):
        y = y_ref[0] * _silu(z_ref[0])
        for lo in (0, half):
            yg = y[:, lo:lo + half]
            r = lax.rsqrt(jnp.mean(yg * yg, axis=-1, keepdims=True) + EPS)
            o_ref[0, :, lo:lo + half] = (yg * r * w_ref[:, lo:lo + half]).astype(bf16)

    return _row_call(body, "ssd_out_fwd", B, S, ts, [_tok(ts, C), _tok(ts, C, 0), _const(1, C)], _tok(ts, C),
                     jax.ShapeDtypeStruct((B, S, C), bf16))(yc3, p3, w)


def ssd_out_bwd(yc3, p3, dcat3, w):
    B, S, C = yc3.shape
    ts = _pick(S, (512, 256, 128))
    half = C // 2

    def body(y_ref, z_ref, do_ref, w_ref, dyc_ref, dz_ref, dw_ref):
        @pl.when(_first())
        def _():
            dw_ref[...] = jnp.zeros_like(dw_ref)

        yc, z, do = y_ref[0], z_ref[0], do_ref[0]
        sz = _silu(z)
        y = yc * sz
        for lo in (0, half):
            sl = slice(lo, lo + half)
            yg, dog, wg = y[:, sl], do[:, sl], w_ref[:, sl]
            r = lax.rsqrt(jnp.mean(yg * yg, axis=-1, keepdims=True) + EPS)
            yh = yg * r
            dw_ref[:, sl] += _rowsum(dog * yh)
            dyh = dog * wg
            dy = r * (dyh - yh * jnp.mean(dyh * yh, axis=-1, keepdims=True))
            dyc_ref[0, :, sl] = dy * sz[:, sl]
            dz_ref[0, :, sl] = (dy * yc[:, sl] * _dsilu(z[:, sl])).astype(bf16)

    return _row_call(body, "ssd_out_bwd", B, S, ts, [_tok(ts, C), _tok(ts, C, 0), _tok(ts, C, 0), _const(1, C)],
                     [_tok(ts, C), _tok(ts, C, 0), _const(1, C)],
                     [jax.ShapeDtypeStruct((B, S, C), f32), jax.ShapeDtypeStruct((B, S, IN_COLS), bf16),
                      jax.ShapeDtypeStruct((1, C), f32)])(yc3, p3, dcat3, w)


def _rot(t):
    lane = lax.broadcasted_iota(jnp.int32, t.shape, 1)
    return jnp.where(lane < ROPE // 2, -pltpu.roll(t, 128 - ROPE // 2, 1), pltpu.roll(t, ROPE // 2, 1))


def _rope(t, cosf, sinf):
    return t * cosf + _rot(t) * sinf


def _rope_t(d, cosf, sinf):
    return d * cosf - _rot(d * sinf)


def qprep_fwd(qraw3, cosf, sinf):
    B, S, W = qraw3.shape
    ts = _pick(S, (512, 256, 128))

    def body(q_ref, c_ref, s_ref, o_ref):
        c, s = c_ref[0], s_ref[0]
        for h in range(MLA_H):
            o_ref[0, :, h * 256:h * 256 + 128] = (q_ref[0, :, h * 256:h * 256 + 128] * Q_FOLD).astype(bf16)
            o_ref[0, :, h * 256 + 128:(h + 1) * 256] = (_rope(q_ref[0, :, h * 256 + 128:(h + 1) * 256], c, s) * Q_FOLD).astype(bf16)

    return _row_call(body, "qprep_fwd", B, S, ts, [_tok(ts, W), _tok(ts, 128), _tok(ts, 128)], _tok(ts, W),
                     jax.ShapeDtypeStruct((B, S, W), bf16))(qraw3, cosf, sinf)


def kprep_fwd(kv3, p3, cosf, sinf):
    B, S, _ = kv3.shape
    ts = _pick(S, (512, 256, 128))
    Wn = MLA_H * NOPE

    def body(k_ref, v_ref, kr_ref, c_ref, s_ref, ko_ref, vo_ref):
        lane = lax.broadcasted_iota(jnp.int32, (1, 128), 1)
        kr = jnp.where(lane < ROPE, kr_ref[0], 0.0)
        kr = _rope(kr, c_ref[0], s_ref[0]).astype(bf16)
        for h in range(MLA_H):
            ko_ref[0, :, h * 256:h * 256 + 128] = k_ref[0, :, h * 128:(h + 1) * 128].astype(bf16)
            ko_ref[0, :, h * 256 + 128:(h + 1) * 256] = kr
        vo_ref[0] = v_ref[0].astype(bf16)

    return _row_call(body, "kprep_fwd", B, S, ts,
                     [_tok(ts, Wn, 0), _tok(ts, Wn, 1), _tok(ts, 128, OFF_KRDT // 128), _tok(ts, 128), _tok(ts, 128)],
                     [_tok(ts, 2 * Wn), _tok(ts, Wn)],
                     [jax.ShapeDtypeStruct((B, S, 2 * Wn), bf16), jax.ShapeDtypeStruct((B, S, Wn), bf16)])(kv3, kv3, p3, cosf, sinf)


def kprep_bwd(dk3, dv3, ddt3, cosf, sinf, dp):
    B, S, _ = dk3.shape
    ts = _pick(S, (512, 256, 128))
    Wn = MLA_H * NOPE

    def body(dk_ref, dv_ref, ddt_ref, c_ref, s_ref, dp_in, o_ref, kr_ref):
        acc = jnp.zeros((ts, 128), f32)
        for h in range(MLA_H):
            o_ref[0, :, h * 128:(h + 1) * 128] = dk_ref[0, :, h * 256:h * 256 + 128].astype(bf16)
            acc = acc + dk_ref[0, :, h * 256 + 128:(h + 1) * 256]
        o_ref[0, :, Wn:] = dv_ref[0].astype(bf16)
        lane = lax.broadcasted_iota(jnp.int32, (1, 128), 1)
        dkr = _rope_t(acc, c_ref[0], s_ref[0])
        kr_ref[0] = jnp.where(lane < ROPE, dkr, ddt_ref[0]).astype(bf16)

    return _row_call(body, "kprep_bwd", B, S, ts,
                     [_tok(ts, 2 * Wn), _tok(ts, Wn), _tok(ts, 128), _tok(ts, 128), _tok(ts, 128), pl.BlockSpec(memory_space=pl.ANY)],
                     [_tok(ts, 2 * Wn), _tok(ts, 128, OFF_KRDT // 128)],
                     [jax.ShapeDtypeStruct((B, S, 2 * Wn), bf16), jax.ShapeDtypeStruct(dp.shape, bf16)],
                     aliases={5: 1})(dk3, dv3, ddt3, cosf, sinf, dp)


def _shift_down(u, j):
    if j == 0:
        return u
    row = lax.broadcasted_iota(jnp.int32, u.shape, 0)
    return jnp.where(row < j, 0.0, pltpu.roll(u, j, 0))


def _shift_up(u, j):
    if j == 0:
        return u
    n = u.shape[0]
    row = lax.broadcasted_iota(jnp.int32, u.shape, 0)
    return jnp.where(row >= n - j, 0.0, pltpu.roll(u, n - j, 0))


def _conv(u, w, b, K):
    out = b
    for j in range(K):
        out = out + w[K - 1 - j:K - j, :] * _shift_down(u, j)
    return out


def _conv_bwd(u, du, w, K):
    dins = w[K - 1:K, :] * du
    dws = [None] * K
    dws[K - 1] = _rowsum(du * u)
    for j in range(1, K):
        sd = _shift_up(du, j)
        dins = dins + w[K - 1 - j:K - j, :] * sd
        dws[K - 1 - j] = _rowsum(sd * u)
    return dins, dws


CW = 256


def conv_ssd_fwd(p3, w, b):
    B, S, _ = p3.shape
    nb = D_XBC // CW

    def body(u_ref, w_ref, b_ref, o_ref, pre_ref):
        pre = _conv(u_ref[0], w_ref[...], b_ref[...], CONV_K)
        o_ref[0] = _silu(pre)
        pre_ref[0] = pre.astype(bf16)

    out = pl.BlockSpec((1, S, CW), lambda b, j: (b, 0, j))
    return pl.pallas_call(body, name="conv_ssd_fwd", grid=(B, nb),
                          in_specs=[pl.BlockSpec((1, S, CW), lambda b, j: (b, 0, OFF_XBC // CW + j)),
                                    pl.BlockSpec((CONV_K, CW), lambda b, j: (0, j)),
                                    pl.BlockSpec((1, CW), lambda b, j: (0, j))],
                          out_specs=[out, out],
                          out_shape=[jax.ShapeDtypeStruct((B, S, D_XBC), f32), jax.ShapeDtypeStruct((B, S, D_XBC), bf16)],
                          compiler_params=_cparams(2))(p3, w, b)


def conv_ssd_bwd(p3, pre3, dxc3, w, dp):
    B, S, _ = p3.shape
    nb = D_XBC // CW

    def body(u_ref, pre_ref, d_ref, w_ref, dp_in, du_ref, dw_ref, db_ref):
        @pl.when(pl.program_id(1) == 0)
        def _():
            dw_ref[...] = jnp.zeros_like(dw_ref)
            db_ref[...] = jnp.zeros_like(db_ref)

        u, wv = u_ref[0], w_ref[...]
        dpre = d_ref[0] * _dsilu(pre_ref[0].astype(f32))
        dins, dws = _conv_bwd(u, dpre, wv, CONV_K)
        du_ref[0] = dins.astype(bf16)
        for k in range(CONV_K):
            dw_ref[k:k + 1, :] += dws[k]
        db_ref[...] += _rowsum(dpre)

    return pl.pallas_call(body, name="conv_ssd_bwd", grid=(nb, B),
                          in_specs=[pl.BlockSpec((1, S, CW), lambda j, b: (b, 0, OFF_XBC // CW + j)),
                                    pl.BlockSpec((1, S, CW), lambda j, b: (b, 0, j)),
                                    pl.BlockSpec((1, S, CW), lambda j, b: (b, 0, j)),
                                    pl.BlockSpec((CONV_K, CW), lambda j, b: (0, j)), pl.BlockSpec(memory_space=pl.ANY)],
                          out_specs=[pl.BlockSpec((1, S, CW), lambda j, b: (b, 0, OFF_XBC // CW + j)),
                                     pl.BlockSpec((CONV_K, CW), lambda j, b: (0, j)),
                                     pl.BlockSpec((1, CW), lambda j, b: (0, j))],
                          out_shape=[jax.ShapeDtypeStruct(dp.shape, bf16), jax.ShapeDtypeStruct((CONV_K, D_XBC), f32),
                                     jax.ShapeDtypeStruct((1, D_XBC), f32)], input_output_aliases={4: 0},
                          compiler_params=_cparams(2))(p3, pre3, dxc3, w, dp)


def glu_fwd(u3, w, b):
    B, S, _ = u3.shape
    nb = D_FF // CW

    def body(ug_ref, uv_ref, wg_ref, wv_ref, bg_ref, bv_ref, o_ref, g_ref, v_ref):
        g = _conv(ug_ref[0].astype(f32), wg_ref[...], bg_ref[...], FF_K)
        v = _conv(uv_ref[0].astype(f32), wv_ref[...], bv_ref[...], FF_K)
        o_ref[0] = (_silu(g) * v).astype(bf16)
        g_ref[0] = g.astype(bf16)
        v_ref[0] = v.astype(bf16)

    def blk(off):
        return pl.BlockSpec((1, S, CW), lambda b, j: (b, 0, off + j))

    def par(rows, off):
        return pl.BlockSpec((rows, CW), lambda b, j: (0, off + j))

    shp = jax.ShapeDtypeStruct((B, S, D_FF), bf16)
    return pl.pallas_call(body, name="glu_fwd", grid=(B, nb),
                          in_specs=[blk(0), blk(nb), par(FF_K, 0), par(FF_K, nb), par(1, 0), par(1, nb)],
                          out_specs=[blk(0)] * 3, out_shape=[shp] * 3, compiler_params=_cparams(2))(u3, u3, w, w, b, b)


def glu_bwd(u3, g3, v3, da3, w):
    B, S, _ = u3.shape
    nb = D_FF // CW

    def body(ug_ref, uv_ref, g_ref, v_ref, da_ref, wg_ref, wv_ref, dug_ref, duv_ref, dwg_ref, dwv_ref, dbg_ref, dbv_ref):
        @pl.when(pl.program_id(1) == 0)
        def _():
            for r in (dwg_ref, dwv_ref, dbg_ref, dbv_ref):
                r[...] = jnp.zeros_like(r)

        ug, uv, da, wg, wv = ug_ref[0].astype(f32), uv_ref[0].astype(f32), da_ref[0].astype(f32), wg_ref[...], wv_ref[...]
        g, v = g_ref[0].astype(f32), v_ref[0].astype(f32)
        dg = da * v * _dsilu(g)
        dv = da * _silu(g)
        ding, dwsg = _conv_bwd(ug, dg, wg, FF_K)
        dinv, dwsv = _conv_bwd(uv, dv, wv, FF_K)
        dug_ref[0] = ding.astype(bf16)
        duv_ref[0] = dinv.astype(bf16)
        for k in range(FF_K):
            dwg_ref[k:k + 1, :] += dwsg[k]
            dwv_ref[k:k + 1, :] += dwsv[k]
        dbg_ref[...] += _rowsum(dg)
        dbv_ref[...] += _rowsum(dv)

    def blk(off):
        return pl.BlockSpec((1, S, CW), lambda j, b: (b, 0, off + j))

    def par(rows, off):
        return pl.BlockSpec((rows, CW), lambda j, b: (0, off + j))

    return pl.pallas_call(body, name="glu_bwd", grid=(nb, B),
                          in_specs=[blk(0), blk(nb), blk(0), blk(0), blk(0), par(FF_K, 0), par(FF_K, nb)],
                          out_specs=[blk(0), blk(0), par(FF_K, 0), par(FF_K, 0), par(1, 0), par(1, 0)],
                          out_shape=[jax.ShapeDtypeStruct((B, S, D_FF), bf16), jax.ShapeDtypeStruct((B, S, D_FF), bf16),
                                     jax.ShapeDtypeStruct((FF_K, D_FF), f32), jax.ShapeDtypeStruct((FF_K, D_FF), f32),
                                     jax.ShapeDtypeStruct((1, D_FF), f32), jax.ShapeDtypeStruct((1, D_FF), f32)],
                          compiler_params=_cparams(2))(u3, u3, g3, v3, da3, w, w)


def _ssd_decay(dtb, bias_row, alog_row):
    lane = lax.broadcasted_iota(jnp.int32, (1, 128), 1)
    hmask = (lane >= DT_LANE) & (lane < DT_LANE + SSD_HEADS)
    dt = jnp.where(hmask, jax.nn.softplus(dtb + bias_row), 0.0)
    a = dt * jnp.where(hmask, -jnp.exp(alog_row), 0.0)
    r = lax.broadcasted_iota(jnp.int32, (CHUNK, CHUNK), 0)
    c = lax.broadcasted_iota(jnp.int32, (CHUNK, CHUNK), 1)
    cs = jnp.dot((r >= c).astype(f32), a, precision=HIGHEST, preferred_element_type=f32)
    return dt, cs


def _expand(xt):
    return jnp.concatenate([jnp.broadcast_to(xt[DT_LANE + h:DT_LANE + h + 1, :], (SSD_HD, xt.shape[1]))
                            for h in range(SSD_HEADS)], axis=0)


_NT = (((1,), (1,)), ((), ()))
_TN = (((0,), (0,)), ((), ()))
GH = SSD_HEADS // 2
GR = GH * SSD_HD


def ssd_fwd(xc3, p3, bias_row, alog_row, dcol):
    B, S, _ = xc3.shape
    nc = S // CHUNK

    def body(xs_ref, bc_ref, dtb_ref, bias_ref, alog_ref, dcol_ref, y_ref, st_ref, state, yT):
        @pl.when(pl.program_id(1) == 0)
        def _():
            state[...] = jnp.zeros_like(state)

        dt, cs = _ssd_decay(dtb_ref[0], bias_ref[...], alog_ref[...])
        csT = cs.T
        eT = jnp.exp(csT)
        decX = _expand(jnp.exp(csT[:, CHUNK - 1:CHUNK] - csT))
        eX = _expand(eT)
        elastX = eX[:, CHUNK - 1:CHUNK]
        xsT = xs_ref[0].T
        uT = xsT * _expand(dt.T)
        bc = bc_ref[0]
        st_ref[0, 0] = state[...]
        srow = lax.broadcasted_iota(jnp.int32, (CHUNK, CHUNK), 0)
        lcol = lax.broadcasted_iota(jnp.int32, (CHUNK, CHUNK), 1)
        for g in range(2):
            Bg = bc[:, g * SSD_N:(g + 1) * SSD_N].astype(bf16)
            Cg = bc[:, (2 + g) * SSD_N:(3 + g) * SSD_N].astype(bf16)
            GT = lax.dot_general(Bg, Cg, _NT, preferred_element_type=f32)
            rows = slice(g * GR, (g + 1) * GR)
            Sg = state[rows]
            yoffT = lax.dot_general(Sg.astype(bf16), Cg, _NT, preferred_element_type=f32) * eX[rows]
            state[rows] = Sg * elastX[rows] + jnp.dot((uT[rows] * decX[rows]).astype(bf16), Bg, preferred_element_type=f32)
            for k in range(GH):
                h = g * GH + k
                hr = slice(h * SSD_HD, (h + 1) * SSD_HD)
                seg = csT[DT_LANE + h:DT_LANE + h + 1, :] - cs[:, DT_LANE + h:DT_LANE + h + 1]
                LT = jnp.where(lcol >= srow, jnp.exp(jnp.minimum(seg, 0.0)), 0.0)
                yT[hr] = (jnp.dot(uT[hr].astype(bf16), (GT * LT).astype(bf16), preferred_element_type=f32)
                          + yoffT[k * SSD_HD:(k + 1) * SSD_HD] + dcol_ref[hr] * xsT[hr])
        y_ref[0] = yT[...].T

    return pl.pallas_call(body, name="ssd_fwd", grid=(B, nc),
                          in_specs=[pl.BlockSpec((1, CHUNK, D_SSD), lambda b, c: (b, c, 0)),
                                    pl.BlockSpec((1, CHUNK, 512), lambda b, c: (b, c, 2)),
                                    pl.BlockSpec((1, CHUNK, 128), lambda b, c: (b, c, OFF_KRDT // 128)),
                                    _const(1, 128), _const(1, 128), _const(D_SSD, 1)],
                          out_specs=[pl.BlockSpec((1, CHUNK, D_SSD), lambda b, c: (b, c, 0)),
                                     pl.BlockSpec((1, 1, D_SSD, SSD_N), lambda b, c: (b, c, 0, 0))],
                          out_shape=[jax.ShapeDtypeStruct((B, S, D_SSD), f32), jax.ShapeDtypeStruct((B, nc, D_SSD, SSD_N), f32)],
                          scratch_shapes=[pltpu.VMEM((D_SSD, SSD_N), f32), pltpu.VMEM((D_SSD, CHUNK), f32)],
                          compiler_params=_cparams(2))(xc3, xc3, p3, bias_row, alog_row, dcol)


def ssd_bwd(xc3, p3, dy3, states, bias_row, alog_row, bias_col, alog_col, dcol, comm=None):
    B, S, _ = xc3.shape
    nc = S // CHUNK

    def body(xs_ref, bc_ref, dtb_ref, dy_ref, st_ref, bias_ref, alog_ref, biasc_ref, alogc_ref, dcol_ref,
             dxc_ref, ddt_ref, dalog_ref, dd_ref, dbias_ref, dS, dUT, accA, accD, accB, dcs_diag):
        @pl.when(pl.program_id(1) == 0)
        def _():
            dS[...] = jnp.zeros_like(dS)

        @pl.when(_first())
        def _():
            accA[...] = jnp.zeros_like(accA)
            accD[...] = jnp.zeros_like(accD)
            accB[...] = jnp.zeros_like(accB)

        dtb = dtb_ref[0]
        dt, cs = _ssd_decay(dtb, bias_ref[...], alog_ref[...])
        dtT, csT = dt.T, cs.T
        decX = _expand(jnp.exp(csT[:, CHUNK - 1:CHUNK] - csT))
        eX = _expand(jnp.exp(csT))
        dtX = _expand(dtT)
        elastX = eX[:, CHUNK - 1:CHUNK]
        xsT = xs_ref[0].T
        uT = xsT * dtX
        dYT = dy_ref[0].T
        bc = bc_ref[0]
        lrow = lax.broadcasted_iota(jnp.int32, (CHUNK, CHUNK), 0)
        scol = lax.broadcasted_iota(jnp.int32, (CHUNK, CHUNK), 1)
        dcs_diag[...] = jnp.zeros_like(dcs_diag)
        rs_cols = jnp.zeros((CHUNK, 128), f32)
        vparts, zparts = [], []
        for g in range(2):
            Bf = bc[:, g * SSD_N:(g + 1) * SSD_N]
            Bg = Bf.astype(bf16)
            Cg = bc[:, (2 + g) * SSD_N:(3 + g) * SSD_N].astype(bf16)
            G = lax.dot_general(Cg, Bg, _NT, preferred_element_type=f32)
            BgT = Bf.T.astype(bf16)
            rows = slice(g * GR, (g + 1) * GR)
            dSg = dS[rows]
            Sg = st_ref[0, 0, rows, :]
            dUst = jnp.dot(dSg.astype(bf16), BgT, preferred_element_type=f32) * decX[rows]
            yoffT = lax.dot_general(Sg.astype(bf16), Cg, _NT, preferred_element_type=f32) * eX[rows]
            zparts.append(dYT[rows] * yoffT - dUst * uT[rows])
            dG = jnp.zeros((CHUNK, CHUNK), f32)
            for k in range(GH):
                h = g * GH + k
                hr = slice(h * SSD_HD, (h + 1) * SSD_HD)
                seg = cs[:, DT_LANE + h:DT_LANE + h + 1] - csT[DT_LANE + h:DT_LANE + h + 1, :]
                L = jnp.where(lrow >= scol, jnp.exp(jnp.minimum(seg, 0.0)), 0.0)
                M = G * L
                dYh = dYT[hr].astype(bf16)
                dUT[hr] = jnp.dot(dYh, M.astype(bf16), preferred_element_type=f32) + dUst[k * SSD_HD:(k + 1) * SSD_HD]
                dM = lax.dot_general(dYh, uT[hr].astype(bf16), _TN, preferred_element_type=f32)
                dG = dG + dM * L
                Wm = dM * M
                rs_cols = jnp.where(scol == DT_LANE + h, jnp.sum(Wm, axis=1, keepdims=True), rs_cols)
                dcs_diag[DT_LANE + h:DT_LANE + h + 1, :] = -_rowsum(Wm)
            dGb = dG.astype(bf16)
            dYe = (dYT[rows] * eX[rows]).astype(bf16)
            ude = (uT[rows] * decX[rows]).astype(bf16)
            dC = jnp.dot(dGb, Bg, preferred_element_type=f32) + lax.dot_general(dYe, Sg.astype(bf16), _TN, preferred_element_type=f32)
            dB = (lax.dot_general(dGb, Cg, _TN, preferred_element_type=f32)
                  + lax.dot_general(ude, dSg.astype(bf16), _TN, preferred_element_type=f32))
            dxc_ref[0, :, D_SSD + g * SSD_N:D_SSD + (g + 1) * SSD_N] = dB
            dxc_ref[0, :, D_SSD + (2 + g) * SSD_N:D_SSD + (3 + g) * SSD_N] = dC
            vparts.append(elastX[rows] * jnp.sum(dSg * Sg, axis=1, keepdims=True)
                          + jnp.sum(dUst * uT[rows], axis=1, keepdims=True))
            dS[rows] = elastX[rows] * dSg + jnp.dot(dYe, Cg, preferred_element_type=f32)
        dU = dUT[...]
        dcv = dcol_ref[...]
        dxc_ref[0, :, 0:D_SSD] = (dtX * dU + dcv * dYT).T
        lane = lax.broadcasted_iota(jnp.int32, (D_SSD, CHUNK), 1)
        Z = jnp.concatenate(zparts, axis=0) + jnp.where(lane == CHUNK - 1, jnp.concatenate(vparts, axis=0), 0.0)
        hr_ = lax.broadcasted_iota(jnp.int32, (128, D_SSD), 0)
        hc_ = lax.broadcasted_iota(jnp.int32, (128, D_SSD), 1)
        hsel = (hr_ - DT_LANE == jnp.right_shift(hc_, 6)).astype(bf16)
        summands = jnp.concatenate([Z, dU * xsT, dYT * xsT], axis=1)
        hi = summands.astype(bf16)
        lo = (summands - hi.astype(f32)).astype(bf16)
        red = jnp.dot(hsel, hi, preferred_element_type=f32) + jnp.dot(hsel, lo, preferred_element_type=f32)
        dcsT = red[:, 0:CHUNK] + dcs_diag[...] + rs_cols.T
        daT = jnp.dot(dcsT, (lrow >= scol).astype(f32), precision=HIGHEST, preferred_element_type=f32)
        rowi = lax.broadcasted_iota(jnp.int32, (128, 1), 0)
        hmask = (rowi >= DT_LANE) & (rowi < DT_LANE + SSD_HEADS)
        a_col = jnp.where(hmask, -jnp.exp(alogc_ref[...]), 0.0)
        ddtT = red[:, CHUNK:2 * CHUNK] + a_col * daT
        ddt_rawT = jnp.where(hmask, ddtT * jax.nn.sigmoid(dtb.T + biasc_ref[...]), 0.0)
        ddt_ref[0] = ddt_rawT.T
        accA[...] += daT * dtT
        accD[...] += red[:, 2 * CHUNK:3 * CHUNK]
        accB[...] += ddt_rawT

        @pl.when((pl.program_id(0) == B - 1) & (pl.program_id(1) == nc - 1))
        def _():
            dalog_ref[...] = jnp.broadcast_to(jnp.sum(accA[...], axis=1, keepdims=True) * a_col, (128, 128))
            dd_ref[...] = jnp.broadcast_to(jnp.sum(accD[...], axis=1, keepdims=True), (128, 128))
            dbias_ref[...] = jnp.broadcast_to(jnp.sum(accB[...], axis=1, keepdims=True), (128, 128))

    def rev(width, cb):
        return pl.BlockSpec((1, CHUNK, width), lambda b, c: (b, nc - 1 - c, cb))

    acc_spec = pl.BlockSpec((128, 128), lambda b, c: (0, 0))
    acc_shape = jax.ShapeDtypeStruct((128, 128), f32)
    body, c_args, c_in, c_out, c_shapes, c_sems = _fuse_exchange(body, comm, 10, 5, 6, (B, nc))
    res = pl.pallas_call(body, name="ssd_bwd_x" if comm else "ssd_bwd", grid=(B, nc),
                         in_specs=[rev(D_SSD, 0), rev(512, 2), rev(128, OFF_KRDT // 128), rev(D_SSD, 0),
                                   pl.BlockSpec((1, 1, D_SSD, SSD_N), lambda b, c: (b, nc - 1 - c, 0, 0)),
                                   _const(1, 128), _const(1, 128), _const(128, 1), _const(128, 1), _const(D_SSD, 1)] + c_in,
                         out_specs=[rev(D_XBC, 0), rev(128, 0), acc_spec, acc_spec, acc_spec] + c_out,
                         out_shape=[jax.ShapeDtypeStruct((B, S, D_XBC), f32), jax.ShapeDtypeStruct((B, S, 128), f32),
                                    acc_shape, acc_shape, acc_shape] + c_shapes,
                         scratch_shapes=[pltpu.VMEM((D_SSD, SSD_N), f32), pltpu.VMEM((D_SSD, CHUNK), f32),
                                         pltpu.VMEM((128, 128), f32), pltpu.VMEM((128, 128), f32), pltpu.VMEM((128, 128), f32),
                                         pltpu.VMEM((128, 128), f32)] + c_sems,
                         compiler_params=_cparams(2))(xc3, xc3, p3, dy3, states, bias_row, alog_row, bias_col, alog_col, dcol, *c_args)
    return (*res[:5], list(res[5:]))


ATT_SCALE = float(QK) ** -0.5
LOG2E = 1.4426950408889634
LN2 = 0.6931471805599453
Q_FOLD = ATT_SCALE * LOG2E
NEG = -1e30
HP = 2


def _att_block(S):
    return _pick(S, (512, 256, 128))


def attn_fwd(q3, k3, v3, comm=None):
    B, S, _ = q3.shape
    bq = _att_block(S)
    nq = S // bq

    def body(q_ref, k_ref, v_ref, o_ref, lse_ref, m_s, l_s, acc):
        i, j = pl.program_id(2), pl.program_id(3)

        @pl.when(j == 0)
        def _():
            m_s[...] = jnp.full_like(m_s, NEG)
            l_s[...] = jnp.zeros_like(l_s)
            acc[...] = jnp.zeros_like(acc)

        def step(masked):
            for t in range(HP):
                qk = slice(t * 256, (t + 1) * 256)
                st = lax.dot_general(k_ref[0, :, qk], q_ref[0, :, qk], _NT, preferred_element_type=f32)
                if masked:
                    r = lax.broadcasted_iota(jnp.int32, (bq, bq), 0)
                    c = lax.broadcasted_iota(jnp.int32, (bq, bq), 1)
                    st = jnp.where(c >= r, st, NEG)
                m_old = m_s[t]
                m_new = jnp.maximum(m_old, jnp.max(st, axis=0, keepdims=True))
                alpha = jnp.exp2(m_old - m_new)
                pt = jnp.exp2(st - m_new)
                l_s[t] = alpha * l_s[t] + jnp.sum(pt, axis=0, keepdims=True)
                acc[t] = alpha * acc[t] + lax.dot_general(v_ref[0, :, t * VD:(t + 1) * VD], pt.astype(bf16), _TN,
                                                          preferred_element_type=f32)
                m_s[t] = m_new

        @pl.when(j < i)
        def _():
            step(False)

        @pl.when(j == i)
        def _():
            step(True)
            for t in range(HP):
                o_ref[0, :, t * VD:(t + 1) * VD] = (acc[t] / l_s[t]).T
                lse_ref[0, t] = m_s[t] + jnp.log2(l_s[t])

    grid = (B, MLA_H // HP, nq, nq)
    body, c_args, c_in, c_out, c_shapes, c_sems = _fuse_exchange(body, comm, 3, 2, 3, grid)
    res = pl.pallas_call(body, name="attn_fwd_x" if comm else "attn_fwd", grid=grid,
                         in_specs=[pl.BlockSpec((1, bq, HP * 256), lambda b, h, i, j: (b, i, h)),
                                   pl.BlockSpec((1, bq, HP * 256), lambda b, h, i, j: (b, jnp.minimum(j, i), h)),
                                   pl.BlockSpec((1, bq, HP * VD), lambda b, h, i, j: (b, jnp.minimum(j, i), h))] + c_in,
                         out_specs=[pl.BlockSpec((1, bq, HP * VD), lambda b, h, i, j: (b, i, h)),
                                    pl.BlockSpec((1, HP, 1, bq), lambda b, h, i, j: (b, h, 0, i))] + c_out,
                         out_shape=[jax.ShapeDtypeStruct((B, S, MLA_H * VD), f32), jax.ShapeDtypeStruct((B, MLA_H, 1, S), f32)] + c_shapes,
                         scratch_shapes=[pltpu.VMEM((HP, 1, bq), f32), pltpu.VMEM((HP, 1, bq), f32), pltpu.VMEM((HP, VD, bq), f32)] + c_sems,
                         compiler_params=_cparams(4))(q3, k3, v3, *c_args)
    return res[0], res[1], list(res[2:])


def rms_o_bwd(o3, dcat3, g):
    B, S, C = o3.shape
    ts = _pick(S, (512, 256, 128))

    def body(x_ref, do_ref, g_ref, dx_ref, dg_ref, d_ref):
        @pl.when(_first())
        def _():
            dg_ref[...] = jnp.zeros_like(dg_ref)

        x, do = x_ref[0], do_ref[0]
        r = lax.rsqrt(jnp.mean(x * x, axis=-1, keepdims=True) + EPS)
        xh = x * r
        dg_ref[...] += _rowsum(do * xh)
        dxh = do * g_ref[...]
        dx = r * (dxh - xh * jnp.mean(dxh * xh, axis=-1, keepdims=True))
        dx_ref[0] = dx
        for h in range(MLA_H):
            vs = slice(h * VD, (h + 1) * VD)
            d_ref[0, h] = jnp.sum(dx[:, vs] * x[:, vs], axis=-1, keepdims=True)

    return _row_call(body, "rms_o_bwd", B, S, ts, [_tok(ts, C), _tok(ts, C, 1), _const(1, C)],
                     [_tok(ts, C), _const(1, C), pl.BlockSpec((1, MLA_H, ts, 1), lambda b, s: (b, 0, s, 0))],
                     [jax.ShapeDtypeStruct((B, S, C), f32), jax.ShapeDtypeStruct((1, C), f32),
                      jax.ShapeDtypeStruct((B, MLA_H, S, 1), f32)])(o3, dcat3, g)


def attn_bwd(q3, k3, v3, do3, lse_row, delta_row, cosf, sinf, comm=None):
    B, S, _ = q3.shape
    bq = _att_block(S)
    nq = S // bq

    def body(q_ref, k_ref, v_ref, do_ref, lse_ref, dl_ref, cos_ref, sin_ref, dk_ref, dv_ref, dq_hbm, dk_acc, dv_acc, dq_scr,
             stage, dq_sem):
        b, hp, j, i = pl.program_id(0), pl.program_id(1), pl.program_id(2), pl.program_id(3)
        rows = pl.ds(pl.multiple_of(i * bq, bq), bq)

        @pl.when((j == 0) & (i == 0))
        def _():
            dq_scr[...] = jnp.zeros_like(dq_scr)

        @pl.when(i == 0)
        def _():
            dk_acc[...] = jnp.zeros_like(dk_acc)
            dv_acc[...] = jnp.zeros_like(dv_acc)

        def step(masked):
            for t in range(HP):
                qk, vs = slice(t * 256, (t + 1) * 256), slice(t * VD, (t + 1) * VD)
                q, k = q_ref[0, :, qk], k_ref[0, :, qk]
                do = do_ref[0, :, vs].astype(bf16)
                pt = jnp.exp2(lax.dot_general(k, q, _NT, preferred_element_type=f32) - lse_ref[0, t])
                if masked:
                    r = lax.broadcasted_iota(jnp.int32, (bq, bq), 0)
                    c = lax.broadcasted_iota(jnp.int32, (bq, bq), 1)
                    pt = jnp.where(c >= r, pt, 0.0)
                dv_acc[t] += jnp.dot(pt.astype(bf16), do, preferred_element_type=f32)
                dpt = lax.dot_general(v_ref[0, :, vs], do, _NT, preferred_element_type=f32)
                dst = (pt * (dpt - dl_ref[0, t])).astype(bf16)
                dk_acc[t] += jnp.dot(dst, q, preferred_element_type=f32)
                dq_scr[t, rows, :] += lax.dot_general(dst, k, _TN, preferred_element_type=f32)

        @pl.when(i > j)
        def _():
            step(False)

        @pl.when(i == j)
        def _():
            step(True)
            for t in range(HP):
                d = dq_scr[t, rows, :] * (LN2 * Q_FOLD)
                stage[t, :, 0:NOPE] = d[:, 0:NOPE].astype(bf16)
                stage[t, :, NOPE:] = _rope_t(d[:, NOPE:], cos_ref[0], sin_ref[0]).astype(bf16)
                cp = pltpu.make_async_copy(stage.at[t], dq_hbm.at[b, rows, pl.ds(pl.multiple_of((hp * HP + t) * 256, 256), 256)],
                                           dq_sem.at[t])
                cp.start()
                cp.wait()

        @pl.when(i == nq - 1)
        def _():
            for t in range(HP):
                dk_ref[0, :, t * 256:(t + 1) * 256] = dk_acc[t] * LN2
                dv_ref[0, :, t * VD:(t + 1) * VD] = dv_acc[t]

    kspec = pl.BlockSpec((1, bq, HP * 256), lambda b, h, j, i: (b, j, h))
    vspec = pl.BlockSpec((1, bq, HP * VD), lambda b, h, j, i: (b, j, h))
    rspec = pl.BlockSpec((1, HP, 1, bq), lambda b, h, j, i: (b, h, 0, jnp.maximum(i, j)))
    tspec = pl.BlockSpec((1, bq, 128), lambda b, h, j, i: (b, jnp.maximum(i, j), 0))
    grid = (B, MLA_H // HP, nq, nq)
    body, c_args, c_in, c_out, c_shapes, c_sems = _fuse_exchange(body, comm, 8, 3, 5, grid)
    res = pl.pallas_call(body, name="attn_bwd_x" if comm else "attn_bwd", grid=grid,
                         in_specs=[pl.BlockSpec((1, bq, HP * 256), lambda b, h, j, i: (b, jnp.maximum(i, j), h)), kspec, vspec,
                                   pl.BlockSpec((1, bq, HP * VD), lambda b, h, j, i: (b, jnp.maximum(i, j), h)), rspec, rspec,
                                   tspec, tspec] + c_in,
                         out_specs=[kspec, vspec, pl.BlockSpec(memory_space=pltpu.HBM)] + c_out,
                         out_shape=[jax.ShapeDtypeStruct((B, S, MLA_H * 256), f32), jax.ShapeDtypeStruct((B, S, MLA_H * VD), f32),
                                    jax.ShapeDtypeStruct((B, S, MLA_H * 256), bf16)] + c_shapes,
                         scratch_shapes=[pltpu.VMEM((HP, bq, 256), f32), pltpu.VMEM((HP, bq, VD), f32), pltpu.VMEM((HP, S, 256), f32),
                                         pltpu.VMEM((HP, bq, 256), bf16), pltpu.SemaphoreType.DMA((HP,))] + c_sems,
                         compiler_params=_cparams(4))(q3, k3, v3, do3, lse_row, delta_row, cosf, sinf, *c_args)
    return res[2], res[0], res[1], list(res[3:])


def ada_fwd(c_all, w, b):
    n = w.shape[1]

    def body(c_ref, w_ref, b_ref, o_ref):
        o_ref[...] = jnp.dot(_silu(c_ref[...]).astype(bf16), w_ref[...].astype(bf16), preferred_element_type=f32) + b_ref[...]

    return pl.pallas_call(body, name="ada_fwd", out_shape=jax.ShapeDtypeStruct((c_all.shape[0], n), f32),
                          compiler_params=pltpu.CompilerParams(vmem_limit_bytes=VMEM_LIMIT))(c_all, w, b)


def ada_bwd(c_all, dmod):
    n = dmod.shape[1]

    def body(c_ref, d_ref, o_ref):
        o_ref[...] = lax.dot_general(_silu(c_ref[...]).astype(bf16), d_ref[...].astype(bf16), _TN, preferred_element_type=f32)

    return pl.pallas_call(body, name="ada_bwd", out_shape=jax.ShapeDtypeStruct((c_all.shape[1], n), f32),
                          compiler_params=pltpu.CompilerParams(vmem_limit_bytes=VMEM_LIMIT))(c_all, dmod)


def sum_leading(x, name):
    n, R, _ = x.shape
    tr = _pick(R, (512, 256, 128, 64, 32, 16, 8))

    def body(x_ref, o_ref):
        acc = x_ref[0].astype(f32)
        for k in range(1, n):
            acc = acc + x_ref[k].astype(f32)
        o_ref[...] = acc

    return pl.pallas_call(body, name=name, grid=(R // tr,), in_specs=[pl.BlockSpec((n, tr, 128), lambda i: (0, i, 0))],
                          out_specs=pl.BlockSpec((tr, 128), lambda i: (i, 0)), out_shape=jax.ShapeDtypeStruct((R, 128), f32),
                          compiler_params=_cparams(1))(x)


def _adamw_body(w_ref, g_ref, m_ref, v_ref, d_ref, mo_ref, vo_ref):
    gv = g_ref[...]
    mn = ADAM_B1 * m_ref[...] + (1.0 - ADAM_B1) * gv
    vn = ADAM_B2 * v_ref[...] + (1.0 - ADAM_B2) * jnp.square(gv)
    m_hat = mn / (1.0 - ADAM_B1 ** ADAM_STEP)
    v_hat = vn / (1.0 - ADAM_B2 ** ADAM_STEP)
    d_ref[...] = -ADAM_LR * (m_hat / (jnp.sqrt(v_hat) + ADAM_EPS) + ADAM_WD * w_ref[...])
    mo_ref[...] = mn
    vo_ref[...] = vn


def adamw(w, g, m, v):
    R = w.shape[0]
    tr = _pick(R, (512, 256, 128, 64, 32, 16, 8))
    spec = pl.BlockSpec((tr, 128), lambda i: (i, 0))
    shp = jax.ShapeDtypeStruct((R, 128), f32)
    return pl.pallas_call(functools.partial(_adamw_body), name="adamw", grid=(R // tr,), in_specs=[spec] * 4,
                          out_specs=[spec] * 3, out_shape=[shp] * 3, compiler_params=_cparams(1))(w, g, m, v)


def _row_tile(a):
    return _pick(a, (256, 128, 64, 32, 16, 8)) if a % 8 == 0 else a


def adamw_nd(w, g, m, v):
    L, a, b = w.shape
    ta = _row_tile(a)
    spec = pl.BlockSpec((1, ta, b), lambda l, i: (l, i, 0))
    shp = jax.ShapeDtypeStruct((L, a, b), f32)
    return pl.pallas_call(functools.partial(_adamw_body), name="adamw_nd", grid=(L, a // ta), in_specs=[spec] * 4,
                          out_specs=[spec] * 3, out_shape=[shp] * 3, compiler_params=_cparams(2))(w, g, m, v)


def sum_slots(x):
    n, L, a, b = x.shape
    ta = _row_tile(a)

    def body(x_ref, o_ref):
        acc = x_ref[0].astype(f32)
        for k in range(1, n):
            acc = acc + x_ref[k].astype(f32)
        o_ref[...] = acc

    return pl.pallas_call(body, name="sum_slots", grid=(L, a // ta),
                          in_specs=[pl.BlockSpec((n, 1, ta, b), lambda l, i: (0, l, i, 0))],
                          out_specs=pl.BlockSpec((1, ta, b), lambda l, i: (l, i, 0)),
                          out_shape=jax.ShapeDtypeStruct((L, a, b), f32), compiler_params=_cparams(2))(x)


def _exchange_copies(ins, outs, sems, scatter):
    send_sems, recv_sems, local_sems = sems
    x, y, c = lax.axis_index("x"), lax.axis_index("y"), lax.axis_index("c")
    me = 4 * x + 2 * y + c
    locals_, sends, recvs = [], [], []
    for a in range(len(ins)):
        locals_.append(pltpu.make_async_copy(ins[a].at[me] if scatter[a] else ins[a], outs[a].at[me], local_sems.at[a]))
        for k in range(N_DEV - 1):
            px = 1 - x if (k + 1) & 4 else x
            py = 1 - y if (k + 1) & 2 else y
            pc = 1 - c if (k + 1) & 1 else c
            pid = 4 * px + 2 * py + pc
            src = ins[a].at[pid] if scatter[a] else ins[a]
            for slot, group in ((me, sends), (pid, recvs)):
                group.append(pltpu.make_async_remote_copy(src_ref=src, dst_ref=outs[a].at[slot], send_sem=send_sems.at[a, k],
                                                          recv_sem=recv_sems.at[a, k], device_id=(px, py, pc),
                                                          device_id_type=pl.DeviceIdType.MESH))
    return locals_, sends, recvs


def _exchange_start(ins, outs, sems, scatter):
    locals_, sends, _ = _exchange_copies(ins, outs, sems, scatter)
    for cp in locals_ + sends:
        cp.start()


def _exchange_wait(ins, outs, sems, scatter):
    locals_, sends, recvs = _exchange_copies(ins, outs, sems, scatter)
    for cp in recvs:
        cp.wait_recv()
    for cp in sends:
        cp.wait_send()
    for cp in locals_:
        cp.wait()


def _exchange_shapes(arrays, scatter):
    return [jax.ShapeDtypeStruct((N_DEV,) + tuple(a.shape[1:] if s else a.shape), a.dtype) for a, s in zip(arrays, scatter)]


def _flags(scatter, n):
    return [scatter] * n if isinstance(scatter, bool) else list(scatter)


def _exchange_sems(n):
    return [pltpu.SemaphoreType.DMA((n, N_DEV - 1)), pltpu.SemaphoreType.DMA((n, N_DEV - 1)), pltpu.SemaphoreType.DMA((n,))]


def _fuse_exchange(core, comm, n_in, n_out, n_scr, grid):
    if comm is None:
        return core, [], [], [], [], []
    arrays, scatter = comm
    n = len(arrays)
    scatter = _flags(scatter, n)

    def body(*refs):
        a, b, c = n_in + n, n_in + n + n_out, n_in + 2 * n + n_out
        cin, cout, sems = refs[n_in:a], refs[b:c], refs[c + n_scr:]
        ids = [pl.program_id(d) for d in range(len(grid))]
        first = functools.reduce(lambda p, q: p & q, [i == 0 for i in ids])
        last = functools.reduce(lambda p, q: p & q, [i == g - 1 for i, g in zip(ids, grid)])

        @pl.when(first)
        def _():
            _exchange_start(cin, cout, sems, scatter)

        core(*refs[:n_in], *refs[a:b], *refs[c:c + n_scr])

        @pl.when(last)
        def _():
            _exchange_wait(cin, cout, sems, scatter)

    hbm = pl.BlockSpec(memory_space=pltpu.HBM)
    return body, list(arrays), [hbm] * n, [hbm] * n, _exchange_shapes(arrays, scatter), _exchange_sems(n)


def exchange(arrays, scatter, name):
    n = len(arrays)
    scatter = _flags(scatter, n)

    def body(*refs):
        ins, outs, sems = refs[:n], refs[n:2 * n], refs[2 * n:]
        _exchange_start(ins, outs, sems, scatter)
        _exchange_wait(ins, outs, sems, scatter)

    hbm = pl.BlockSpec(memory_space=pltpu.HBM)
    return pl.pallas_call(body, name=name, in_specs=[hbm] * n, out_specs=[hbm] * n,
                          out_shape=_exchange_shapes(arrays, scatter), scratch_shapes=_exchange_sems(n))(*arrays)


BIG = (("w_in", "col"), ("conv_w", "col"), ("w_uq", "col"), ("w_ukv", "col"), ("w_out", "row"), ("w_up", "col"),
       ("conv_ff_w", "col"), ("w_down", "row"))
SMALL = ("b_ada", "norm_mix", "conv_b", "dt_bias", "a_log", "d_skip", "ssd_norm", "q_norm", "kv_norm", "attn_norm",
         "norm_mlp", "conv_ff_b", "final_norm")
CONVS = ("conv_w", "conv_ff_w")
PACK_ALIGN = 2048


def _padded(n):
    return -(-n // PACK_ALIGN) * PACK_ALIGN


def _flat_pad(a):
    f = a.reshape(-1)
    return jnp.pad(f, (0, _padded(f.shape[0]) - f.shape[0]))


PACK_ROWS = 512


def pack(arrs):
    f = jnp.concatenate([_flat_pad(a) for a in arrs])
    n = PACK_ROWS * 128
    return jnp.pad(f, (0, -(-f.shape[0] // n) * n - f.shape[0])).reshape(-1, 128)


def unpack(flat, shapes):
    f = flat.reshape(-1)
    out, off = [], 0
    for s in shapes:
        n = int(np.prod(s))
        out.append(f[off:off + n].reshape(s))
        off += _padded(n)
    return out


def shards_to_full(g, kind):
    _, a, b = g.shape
    if kind == "col":
        return g.transpose(1, 0, 2).reshape(a, N_DEV * b)
    return g.reshape(N_DEV * a, b)


def full_to_shards(full, kind):
    if kind == "col":
        a, nb = full.shape
        return full.reshape(a, N_DEV, nb // N_DEV).transpose(1, 0, 2)
    na, b = full.shape
    return full.reshape(N_DEV, na // N_DEV, b)


def w_in_layout(w):
    z = lambda n: jnp.zeros(w.shape[:-1] + (n,), w.dtype)
    return jnp.concatenate([w[..., :2560], w[..., 2576:2960], z(128), w[..., 2960:3216], w[..., 3216:3280],
                            w[..., 2560:2576], z(48)], axis=-1)


def w_in_unlayout(g):
    return jnp.concatenate([g[..., :2560], g[..., 3392:3408], g[..., 2560:2944], g[..., 3072:3328], g[..., 3328:3392]], axis=-1)


def w_uq_layout(w):
    return jnp.pad(w.reshape(Q_RANK, MLA_H, QK), ((0, 0), (0, 0), (0, 256 - QK))).reshape(Q_RANK, MLA_H * 256)


def w_uq_unlayout(g):
    return g.reshape(Q_RANK, MLA_H, 256)[:, :, :QK].reshape(Q_RANK, MLA_H * QK)


def w_ukv_layout(w):
    return w.reshape(KV_RANK, MLA_H, 2, 128).transpose(0, 2, 1, 3).reshape(KV_RANK, 2 * MLA_H * 128)


def w_ukv_unlayout(g):
    return g.reshape(KV_RANK, 2, MLA_H, 128).transpose(0, 2, 1, 3).reshape(KV_RANK, 2 * MLA_H * 128)


LAYOUTS = {"w_in": (w_in_layout, w_in_unlayout), "w_uq": (w_uq_layout, w_uq_unlayout), "w_ukv": (w_ukv_layout, w_ukv_unlayout)}
FIRST, REST = BIG[:4], BIG[4:]


def layer_weights(gathered, entries):
    full = {n: shards_to_full(g, kind) for (n, kind), g in zip(entries, gathered)}
    return {n: LAYOUTS[n][0](w) if n in LAYOUTS else w for n, w in full.items()}


def layer_grad_slices(g, entries):
    return [full_to_shards(LAYOUTS[n][1](g[n]) if n in LAYOUTS else g[n], kind).astype(bf16) for n, kind in entries]


def _head_row(v):
    return jnp.zeros((1, 128), f32).at[0, DT_LANE:DT_LANE + SSD_HEADS].set(v)


def layer_fwd(x3, mod, W, P, l, cosf, sinf, comm=None, late=None, comm_up=None):
    B, S, _ = x3.shape
    T = B * S
    sv = {}
    h = normmod_fwd(x3, mod, P["norm_mix"][l][None], 0, 1)
    p = mm(h.reshape(T, D), W["w_in"], "nn", "mm_in")
    p3 = p.reshape(B, S, IN_COLS)
    bias_row, alog_row = _head_row(P["dt_bias"][l]), _head_row(P["a_log"][l])
    dcol = jnp.repeat(P["d_skip"][l], SSD_HD)[:, None]
    xc3, xpre = conv_ssd_fwd(p3, W["conv_w"], P["conv_b"][l][None])
    yc3, states = ssd_fwd(xc3, p3, bias_row, alog_row, dcol)
    y_ssd = ssd_out_fwd(yc3, p3, P["ssd_norm"][l][None])
    cqn = rms_fwd(p3, 512, OFF_CQ // 512, Q_RANK, P["q_norm"][l][None], "rms_q_fwd")
    ckvn = rms_fwd(p3, KV_RANK, OFF_CKV // KV_RANK, KV_RANK, P["kv_norm"][l][None], "rms_kv_fwd")
    qraw = mm(cqn.reshape(T, Q_RANK), W["w_uq"], "nn", "mm_uq")
    kvraw = mm(ckvn.reshape(T, KV_RANK), W["w_ukv"], "nn", "mm_ukv")
    q3 = qprep_fwd(qraw.reshape(B, S, -1), cosf, sinf)
    k3, v3 = kprep_fwd(kvraw.reshape(B, S, -1), p3, cosf, sinf)
    o3, lse, comm_out = attn_fwd(q3, k3, v3, comm)
    if late is not None:
        W = dict(W, **late(comm_out))
    y_att = rms_fwd(o3, D, 0, D, P["attn_norm"][l][None], "rms_o_fwd")
    cat = (y_ssd.reshape(T, D), y_att.reshape(T, D))
    x1, y1 = mm(cat, W["w_out"], "nn", "mm_out", resid=x3.reshape(T, D), gate=mod[:, 2:3, :], seq=S)
    x13 = x1.reshape(B, S, D)
    h2 = normmod_fwd(x13, mod, P["norm_mlp"][l][None], 3, 4)
    u, up_out = mm(h2.reshape(T, D), W["w_up"], "nn", "mm_up", out_dtype=bf16, comm=comm_up), []
    if comm_up is not None:
        u, up_out = u
    u3 = u.reshape(B, S, 2 * D_FF)
    a, ffg, ffv = glu_fwd(u3, W["conv_ff_w"], P["conv_ff_b"][l][None])
    x2, y2 = mm(a.reshape(T, D_FF), W["w_down"], "nn", "mm_down", resid=x1, gate=mod[:, 5:6, :], seq=S)
    sv.update(x=x3, h=h, p3=p3, xc3=xc3, xpre=xpre, yc3=yc3, states=states, cqn=cqn, ckvn=ckvn, q3=q3, k3=k3, v3=v3, o3=o3, lse=lse,
              cat=cat, y1=y1, x1=x13, h2=h2, u3=u3, ffg=ffg, ffv=ffv, a=a, y2=y2, bias_row=bias_row, alog_row=alog_row, dcol=dcol)
    return x2.reshape(B, S, D), sv, comm_out, W, up_out


def layer_bwd(dx3, sv, mod, W, P, l, cosf, sinf, comm=None, send_rest=False):
    B, S, _ = dx3.shape
    T = B * S
    g = {}
    dy2, dg2 = gate_bwd(dx3, sv["y2"].reshape(B, S, D), mod, 5)
    dy2 = dy2.reshape(T, D)
    da = mm(dy2, W["w_down"], "nt", "mm_down_dx", out_dtype=bf16)
    g["w_down"] = mm(sv["a"].reshape(T, D_FF), dy2, "tn", "mm_down_dw")
    dug, duv, dwg, dwv, dbg, dbv = glu_bwd(sv["u3"], sv["ffg"], sv["ffv"], da.reshape(B, S, D_FF), W["conv_ff_w"])
    g["conv_ff_w"] = jnp.concatenate([dwg, dwv], axis=1)
    g["conv_ff_b"] = jnp.concatenate([dbg, dbv], axis=1)[0]
    du = (dug.reshape(T, D_FF), duv.reshape(T, D_FF))
    dh2 = mm(du, W["w_up"], "nt", "mm_up_dx")
    g["w_up"] = jnp.concatenate([mm(sv["h2"].reshape(T, D), d, "tn", "mm_up_dw") for d in du], axis=1)
    dx1, dsh2, dsc2, dnm = normmod_bwd(sv["x1"], dh2.reshape(B, S, D), dx3, mod, P["norm_mlp"][l][None], 4)
    g["norm_mlp"] = dnm[0]
    dy1, dg1 = gate_bwd(dx1, sv["y1"].reshape(B, S, D), mod, 2)
    dy1 = dy1.reshape(T, D)
    dcat = mm(dy1, W["w_out"], "nt", "mm_out_dx")
    g["w_out"] = jnp.concatenate([mm(part, dy1, "tn", "mm_out_dw") for part in sv["cat"]], axis=0)
    dcat3 = dcat.reshape(B, S, 2 * D)
    dyc3, dp, dsn = ssd_out_bwd(sv["yc3"], sv["p3"], dcat3, P["ssd_norm"][l][None])
    g["ssd_norm"] = dsn[0]
    do3, dan, delta = rms_o_bwd(sv["o3"], dcat3, P["attn_norm"][l][None])
    g["attn_norm"] = dan[0]
    dqraw, dk3, dv3, comm_out = attn_bwd(sv["q3"], sv["k3"], sv["v3"], do3, sv["lse"], delta.reshape(B, MLA_H, 1, S), cosf, sinf, comm)
    dqraw = dqraw.reshape(T, -1)
    dcqn = mm(dqraw, W["w_uq"], "nt", "mm_uq_dx")
    g["w_uq"] = mm(sv["cqn"].reshape(T, Q_RANK), dqraw, "tn", "mm_uq_dw")
    dp, dqn = rms_bwd_into(sv["p3"], 512, OFF_CQ // 512, Q_RANK, dcqn.reshape(B, S, Q_RANK), P["q_norm"][l][None], dp, "rms_q_bwd")
    g["q_norm"] = dqn[0]
    bias_col, alog_col = sv["bias_row"].reshape(128, 1), sv["alog_row"].reshape(128, 1)
    comm_rest = (layer_grad_slices(g, REST), True) if send_rest else None
    dxc3, ddt3, dalog, dd, dbias, rest_out = ssd_bwd(sv["xc3"], sv["p3"], dyc3, sv["states"], sv["bias_row"], sv["alog_row"],
                                                     bias_col, alog_col, sv["dcol"], comm_rest)
    heads = slice(DT_LANE, DT_LANE + SSD_HEADS)
    g["a_log"], g["d_skip"], g["dt_bias"] = dalog[heads, 0], dd[heads, 0], dbias[heads, 0]
    dp, dcw, dcb = conv_ssd_bwd(sv["p3"], sv["xpre"], dxc3, W["conv_w"], dp)
    g["conv_w"], g["conv_b"] = dcw, dcb[0]
    dkvraw, dp = kprep_bwd(dk3, dv3, ddt3, cosf, sinf, dp)
    dkvraw = dkvraw.reshape(T, -1)
    dckvn = mm(dkvraw, W["w_ukv"], "nt", "mm_ukv_dx")
    g["w_ukv"] = mm(sv["ckvn"].reshape(T, KV_RANK), dkvraw, "tn", "mm_ukv_dw")
    dp, dkn = rms_bwd_into(sv["p3"], KV_RANK, OFF_CKV // KV_RANK, KV_RANK, dckvn.reshape(B, S, KV_RANK), P["kv_norm"][l][None], dp,
                           "rms_kv_bwd")
    g["kv_norm"] = dkn[0]
    dp = dp.reshape(T, IN_COLS)
    dh = mm(dp, W["w_in"], "nt", "mm_in_dx")
    g["w_in"] = mm(sv["h"].reshape(T, D), dp, "tn", "mm_in_dw")
    dx0, dsh1, dsc1, dnx = normmod_bwd(sv["x"], dh.reshape(B, S, D), dx1, mod, P["norm_mix"][l][None], 1)
    g["norm_mix"] = dnx[0]
    dmod = jnp.concatenate([dsh1, dsc1, dg1, dsh2, dsc2, dg2], axis=1)
    return dx0, dmod, g, comm_out, rest_out


def kernel(x, c, positions, w_ada, b_ada, norm_mix, w_in, conv_w, conv_b, dt_bias, a_log, d_skip, ssd_norm, q_norm, w_uq, kv_norm, w_ukv, attn_norm, w_out, norm_mlp, w_up, conv_ff_w, conv_ff_b, w_down, final_norm, loss_target, m_w_ada, m_b_ada, m_norm_mix, m_w_in, m_conv_w, m_conv_b, m_dt_bias, m_a_log, m_d_skip, m_ssd_norm, m_q_norm, m_w_uq, m_kv_norm, m_w_ukv, m_attn_norm, m_w_out, m_norm_mlp, m_w_up, m_conv_ff_w, m_conv_ff_b, m_w_down, m_final_norm, v_w_ada, v_b_ada, v_norm_mix, v_w_in, v_conv_w, v_conv_b, v_dt_bias, v_a_log, v_d_skip, v_ssd_norm, v_q_norm, v_w_uq, v_kv_norm, v_w_ukv, v_attn_norm, v_w_out, v_norm_mlp, v_w_up, v_conv_ff_w, v_conv_ff_b, v_w_down, v_final_norm):
    given = dict(locals())
    B, S, _ = x.shape
    me = 4 * lax.axis_index("x") + 2 * lax.axis_index("y") + lax.axis_index("c")
    P = {n: given[n] for n in SMALL}

    def shards(l, entries):
        return [given[n][l] if n in CONVS else given[n][l].astype(bf16) for n, _ in entries]

    *gathered, c_all = exchange(shards(0, FIRST) + [c], False, "gather_weights")
    W = [layer_weights(gathered, FIRST), None]

    n_ada = w_ada.shape[2]
    c_all = c_all.reshape(N_DEV * B, D)
    b_sh = lax.dynamic_slice_in_dim(b_ada, me * n_ada, n_ada, axis=1)
    mod_sh = jnp.stack([ada_fwd(c_all, w_ada[l], b_sh[l][None]) for l in range(DEPTH)])
    (mod_g,) = exchange([mod_sh], False, "gather_mod")
    mod_mine = lax.dynamic_slice_in_dim(mod_g, me * B, B, axis=2)
    mods = mod_mine.transpose(1, 2, 0, 3).reshape(DEPTH, B, 6, D)

    inv_freq = jnp.asarray(1.0 / (ROPE_BASE ** (np.arange(0, ROPE, 2, dtype=np.float32) / ROPE)))
    ang = positions.astype(f32)[..., None] * inv_freq
    zeros = jnp.zeros((B, S, 128 - ROPE), f32)
    cosf = jnp.concatenate([jnp.cos(ang), jnp.cos(ang), zeros], axis=-1)
    sinf = jnp.concatenate([jnp.sin(ang), jnp.sin(ang), zeros], axis=-1)

    saved = [None] * DEPTH
    late = lambda got: layer_weights(got, REST)
    xl, saved[0], _, W[0], gathered = layer_fwd(x, mods[0], W[0], P, 0, cosf, sinf, comm=(shards(0, REST), False), late=late,
                                                comm_up=(shards(1, FIRST), False))
    xl, saved[1], _, W[1], _ = layer_fwd(xl, mods[1], layer_weights(gathered, FIRST), P, 1, cosf, sinf,
                                         comm=(shards(1, REST), False), late=late)
    dxl, d_final, loss_part = final_loss(xl, final_norm[None], loss_target)
    grads, dmods, recv = [None] * DEPTH, [None] * DEPTH, [None] * DEPTH
    dxl, dmods[1], grads[1], _, _ = layer_bwd(dxl, saved[1], mods[1], W[1], P, 1, cosf, sinf)
    grad_x, dmods[0], grads[0], recv[1], recv_rest = layer_bwd(dxl, saved[0], mods[0], W[0], P, 0, cosf, sinf,
                                                               comm=(layer_grad_slices(grads[1], BIG), True), send_rest=True)

    stack = lambda n: jnp.stack([grads[l][n] for l in range(DEPTH)])
    small_names = [n for n in SMALL if n not in ("b_ada", "final_norm")]
    partial = pack([stack(n) for n in small_names] + [d_final[0], loss_part[0]])
    dmod_all = jnp.stack(dmods)
    *recv_first, part_g, dmod_g = exchange(layer_grad_slices(grads[0], FIRST) + [partial, dmod_all],
                                           [True] * len(FIRST) + [False, False], "exchange_tail")
    recv[0] = recv_first + recv_rest
    big_g = [jnp.concatenate([sum_slots(recv[l][i][:, None]) for l in range(DEPTH)]) for i in range(len(BIG))]
    small_sum = sum_leading(part_g, "sum_partials")
    small_g = unpack(small_sum, [given[n].shape for n in small_names] + [(D,), (128,)])
    gsmall = dict(zip(small_names + ["final_norm"], small_g[:-1]))
    loss = small_g[-1][0]
    dmod_rows = dmod_g.transpose(0, 2, 1, 3, 4).reshape(N_DEV * B, DEPTH * 6 * D)
    gsmall["b_ada"] = sum_leading(dmod_rows.reshape(N_DEV * B, -1, 128), "sum_b_ada").reshape(DEPTH, 6 * D)
    dmod_cols = dmod_rows.reshape(N_DEV * B, DEPTH, N_DEV, n_ada)
    dmod_sh = lax.dynamic_slice_in_dim(dmod_cols, me, 1, axis=2)[:, :, 0, :]
    g_w_ada = jnp.stack([ada_bwd(c_all, dmod_sh[:, l, :]) for l in range(DEPTH)])

    res = {"grad": {}, "delta": {}, "new_m": {}, "new_v": {}}
    for n, gv in zip([n for n, _ in BIG] + ["w_ada"], big_g + [g_w_ada]):
        res["grad"][n] = gv
        res["delta"][n], res["new_m"][n], res["new_v"][n] = adamw_nd(given[n], gv, given["m_" + n], given["v_" + n])
    shapes = [given[n].shape for n in SMALL]
    flat = adamw(pack([given[n] for n in SMALL]), pack([gsmall[n] for n in SMALL]), pack([given["m_" + n] for n in SMALL]),
                 pack([given["v_" + n] for n in SMALL]))
    for n in SMALL:
        res["grad"][n] = gsmall[n]
    for key, arr in zip(("delta", "new_m", "new_v"), flat):
        res[key].update(zip(SMALL, unpack(arr, shapes)))
    order = ["w_ada", "b_ada", "norm_mix", "w_in", "conv_w", "conv_b", "dt_bias", "a_log", "d_skip", "ssd_norm", "q_norm", "w_uq",
             "kv_norm", "w_ukv", "attn_norm", "w_out", "norm_mlp", "w_up", "conv_ff_w", "conv_ff_b", "w_down", "final_norm"]
    return (loss, grad_x, *[res[k][n] for k in ("grad", "delta", "new_m", "new_v") for n in order])
```

```python
import functools

import numpy as np
import jax
import jax.numpy as jnp
from jax import lax
from jax.experimental import pallas as pl
from jax.experimental.pallas import tpu as pltpu

f32, bf16 = jnp.float32, jnp.bfloat16
HIGHEST = lax.Precision.HIGHEST

D = 1024
D_SSD = 1024
SSD_HEADS = 16
SSD_HD = 64
SSD_N = 128
CHUNK = 128
D_XBC = 1536
CONV_K = 4
MLA_H = 8
NOPE = 128
ROPE = 64
VD = 128
QK = NOPE + ROPE
Q_RANK = 384
KV_RANK = 256
D_FF = 2816
FF_K = 3
EPS = 1e-6
ROPE_BASE = 10000.0
DEPTH = 2
ADAM_LR, ADAM_B1, ADAM_B2, ADAM_EPS, ADAM_WD, ADAM_STEP = 0.001, 0.9, 0.999, 1e-08, 0.01, 10

N_DEV = 8
IN_COLS = 3456
OFF_XBC, OFF_CQ, OFF_CKV, OFF_KRDT = 1024, 2560, 3072, 3328
DT_LANE = 64
VMEM_LIMIT = 48 * 1024 * 1024
MM_K_WHOLE = 4096


def _cparams(n_grid):
    return pltpu.CompilerParams(dimension_semantics=("arbitrary",) * n_grid, vmem_limit_bytes=VMEM_LIMIT)


def _pick(n, cands):
    for c in cands:
        if n % c == 0:
            return c
    return n


def _silu(x):
    return x * jax.nn.sigmoid(x)


def _dsilu(x):
    s = jax.nn.sigmoid(x)
    return s * (1.0 + x * (1.0 - s))


def _rowsum(x):
    return jnp.sum(x, axis=0, keepdims=True)


def mm(a, b, mode, name, out_dtype=f32, resid=None, gate=None, seq=None, comm=None, rope=None):
    parts = list(a) if isinstance(a, (tuple, list)) else [a]
    np_ = len(parts)
    if mode == "nn":
        (M, Kp), N = parts[0].shape, b.shape[1]
    elif mode == "nt":
        (M, Kp), N = parts[0].shape, b.shape[0]
    else:
        (Kp, M), N = parts[0].shape, b.shape[1]
    K = Kp * np_
    gated = resid is not None
    whole = np_ > 1 or K <= MM_K_WHOLE
    tm = _pick(seq if gated else M, (1024, 1408, 512, 384, 256, 128) if K <= MM_K_WHOLE else (512, 256, 128))
    tn = _pick(N, (512, 1408, 384, 256, 128))
    tk = K if whole else _pick(K, (2816, 2048, 1024, 512))
    if mode == "tn":
        tm = _pick(M, (1024, 1408, 512, 384, 256, 128))
    nk = K // tk
    dims = {"nn": ((1,), (0,)), "nt": ((1,), (1,)), "tn": ((0,), (0,))}[mode]

    def body(*refs):
        a_refs, b_ref, rest = refs[:np_], refs[np_], refs[np_ + 1:]
        if rope is not None:
            cos_ref, sin_ref, rest = rest[0], rest[1], rest[2:]
        if gated:
            r_ref, g_ref, o_ref, y_ref, acc = rest
        else:
            o_ref, acc = rest

        def finish(res):
            if gated:
                y_ref[...] = res
                o_ref[...] = r_ref[...] + g_ref[0] * res
            elif rope is not None:
                for h in range(tn // 256):
                    lo = h * 256
                    o_ref[:, lo:lo + NOPE] = (res[:, lo:lo + NOPE] * Q_FOLD).astype(out_dtype)
                    o_ref[:, lo + NOPE:lo + 256] = (_rope(res[:, lo + NOPE:lo + 256], cos_ref[...], sin_ref[...]) * Q_FOLD).astype(out_dtype)
            else:
                o_ref[...] = res.astype(out_dtype)

        prod = None
        for p, a_ref in enumerate(a_refs):
            if np_ == 1:
                bv = b_ref[...]
            else:
                bv = b_ref[:, p * Kp:(p + 1) * Kp] if mode == "nt" else b_ref[p * Kp:(p + 1) * Kp, :]
            term = lax.dot_general(a_ref[...].astype(bf16), bv.astype(bf16), (dims, ((), ())), preferred_element_type=f32)
            prod = term if prod is None else prod + term
        if nk == 1:
            finish(prod)
        else:
            k = pl.program_id(2)

            @pl.when(k == 0)
            def _():
                acc[...] = prod

            @pl.when(k > 0)
            def _():
                acc[...] += prod

            @pl.when(k == nk - 1)
            def _():
                finish(acc[...])

    if np_ > 1:
        a_spec = pl.BlockSpec((tm, Kp), lambda i, j, k: (i, 0))
    elif mode == "tn":
        a_spec = pl.BlockSpec((tk, tm), lambda i, j, k: (k, i))
    else:
        a_spec = pl.BlockSpec((tm, tk), lambda i, j, k: (i, k))
    b_spec = pl.BlockSpec((tn, tk), lambda i, j, k: (j, k)) if mode == "nt" else pl.BlockSpec((tk, tn), lambda i, j, k: (k, j))
    o_spec = pl.BlockSpec((tm, tn), lambda i, j, k: (i, j))
    in_specs, args = [a_spec] * np_ + [b_spec], parts + [b]
    if rope is not None:
        in_specs += [pl.BlockSpec((tm, 128), lambda i, j, k: (i, 0))] * 2
        args += list(rope)
    out_specs, out_shape = [o_spec], [jax.ShapeDtypeStruct((M, N), out_dtype)]
    if gated:
        per = seq // tm
        in_specs += [o_spec, pl.BlockSpec((1, 1, tn), lambda i, j, k: (i // per, 0, j))]
        args += [resid, gate]
        out_specs = [o_spec, o_spec]
        out_shape = [jax.ShapeDtypeStruct((M, N), f32), jax.ShapeDtypeStruct((M, N), f32)]
    grid = (M // tm, N // tn, nk)
    body, c_args, c_in, c_out, c_shapes, c_sems = _fuse_exchange(body, comm, len(args), len(out_specs), 1, grid)
    res = pl.pallas_call(body, name=name + "_x" if comm else name, grid=grid, in_specs=in_specs + c_in, out_specs=out_specs + c_out,
                         out_shape=out_shape + c_shapes, scratch_shapes=[pltpu.VMEM((tm, tn), f32)] + c_sems,
                         compiler_params=_cparams(3))(*args, *c_args)
    own = res[:len(out_specs)]
    own = own[0] if len(own) == 1 else tuple(own)
    return (own, list(res[len(out_specs):])) if comm else own


def _tok(ts, width, cb=0):
    return pl.BlockSpec((1, ts, width), lambda b, s: (b, s, cb))


def _perb(rows, width):
    return pl.BlockSpec((1, rows, width), lambda b, s: (b, 0, 0))


def _const(rows, width):
    return pl.BlockSpec((rows, width), lambda b, s: (0, 0))


def _row_call(body, name, B, S, ts, in_specs, out_specs, out_shape, scratch=(), aliases=None):
    return pl.pallas_call(body, name=name, grid=(B, S // ts), in_specs=in_specs, out_specs=out_specs,
                          out_shape=out_shape, scratch_shapes=list(scratch), input_output_aliases=aliases or {},
                          compiler_params=_cparams(2))


def _first():
    return (pl.program_id(0) == 0) & (pl.program_id(1) == 0)


def normmod_fwd(x3, mod, g, i_sh, i_sc):
    B, S, C = x3.shape
    ts = _pick(S, (512, 256, 128))

    def body(x_ref, mod_ref, g_ref, h_ref):
        x = x_ref[0]
        r = lax.rsqrt(jnp.mean(x * x, axis=-1, keepdims=True) + EPS)
        n = x * r * g_ref[...]
        h_ref[0] = (n * (1.0 + mod_ref[0, i_sc:i_sc + 1, :]) + mod_ref[0, i_sh:i_sh + 1, :]).astype(bf16)

    return _row_call(body, "normmod_fwd", B, S, ts, [_tok(ts, C), _perb(6, C), _const(1, C)], _tok(ts, C),
                     jax.ShapeDtypeStruct((B, S, C), bf16))(x3, mod, g)


def normmod_bwd(x3, dh3, resid3, mod, g, i_sc):
    B, S, C = x3.shape
    ts = _pick(S, (512, 256, 128))

    def body(x_ref, dh_ref, r_ref, mod_ref, g_ref, dx_ref, dsh_ref, dsc_ref, dg_ref):
        @pl.when(pl.program_id(1) == 0)
        def _():
            dsh_ref[...] = jnp.zeros_like(dsh_ref)
            dsc_ref[...] = jnp.zeros_like(dsc_ref)

        @pl.when(_first())
        def _():
            dg_ref[...] = jnp.zeros_like(dg_ref)

        x, dh, gv = x_ref[0], dh_ref[0].astype(f32), g_ref[...]
        r = lax.rsqrt(jnp.mean(x * x, axis=-1, keepdims=True) + EPS)
        xh = x * r
        dn = dh * (1.0 + mod_ref[0, i_sc:i_sc + 1, :])
        dsh_ref[0] += _rowsum(dh)
        dsc_ref[0] += _rowsum(dh * xh * gv)
        dg_ref[...] += _rowsum(dn * xh)
        dxh = dn * gv
        dx_ref[0] = r * (dxh - xh * jnp.mean(dxh * xh, axis=-1, keepdims=True)) + r_ref[0]

    return _row_call(body, "normmod_bwd", B, S, ts,
                     [_tok(ts, C), _tok(ts, C), _tok(ts, C), _perb(6, C), _const(1, C)],
                     [_tok(ts, C), _perb(1, C), _perb(1, C), _const(1, C)],
                     [jax.ShapeDtypeStruct((B, S, C), f32), jax.ShapeDtypeStruct((B, 1, C), f32),
                      jax.ShapeDtypeStruct((B, 1, C), f32), jax.ShapeDtypeStruct((1, C), f32)])(x3, dh3, resid3, mod, g)


def gate_bwd(dx3, y3, mod, i_g):
    B, S, C = dx3.shape
    ts = _pick(S, (512, 256, 128))

    def body(dx_ref, y_ref, mod_ref, dy_ref, dgate_ref):
        @pl.when(pl.program_id(1) == 0)
        def _():
            dgate_ref[...] = jnp.zeros_like(dgate_ref)

        dx = dx_ref[0]
        dy_ref[0] = (dx * mod_ref[0, i_g:i_g + 1, :]).astype(bf16)
        dgate_ref[0] += _rowsum(dx * y_ref[0])

    return _row_call(body, "gate_bwd", B, S, ts, [_tok(ts, C), _tok(ts, C), _perb(6, C)], [_tok(ts, C), _perb(1, C)],
                     [jax.ShapeDtypeStruct((B, S, C), bf16), jax.ShapeDtypeStruct((B, 1, C), f32)])(dx3, y3, mod)


def rms_fwd(src3, width, cb, n, g, name):
    B, S, _ = src3.shape
    ts = _pick(S, (512, 256, 128))

    def body(x_ref, g_ref, o_ref):
        x = x_ref[0][:, :n]
        r = lax.rsqrt(jnp.mean(x * x, axis=-1, keepdims=True) + EPS)
        o_ref[0] = (x * r * g_ref[...]).astype(bf16)

    return _row_call(body, name, B, S, ts, [_tok(ts, width, cb), _const(1, n)], _tok(ts, n),
                     jax.ShapeDtypeStruct((B, S, n), bf16))(src3, g)


def rms_bwd_into(src3, width, cb, n, dout3, g, dp, name):
    B, S, _ = src3.shape
    ts = _pick(S, (512, 256, 128))

    def body(x_ref, do_ref, g_ref, dp_in, dp_ref, dg_ref):
        @pl.when(_first())
        def _():
            dg_ref[...] = jnp.zeros_like(dg_ref)

        x = x_ref[0][:, :n]
        do = do_ref[0]
        r = lax.rsqrt(jnp.mean(x * x, axis=-1, keepdims=True) + EPS)
        xh = x * r
        dg_ref[...] += _rowsum(do * xh)
        dxh = do * g_ref[...]
        dp_ref[0, :, :n] = (r * (dxh - xh * jnp.mean(dxh * xh, axis=-1, keepdims=True))).astype(bf16)
        if width > n:
            dp_ref[0, :, n:] = jnp.zeros((ts, width - n), bf16)

    return _row_call(body, name, B, S, ts, [_tok(ts, width, cb), _tok(ts, n), _const(1, n), pl.BlockSpec(memory_space=pl.ANY)],
                     [_tok(ts, width, cb), _const(1, n)], [jax.ShapeDtypeStruct(dp.shape, bf16), jax.ShapeDtypeStruct((1, n), f32)],
                     aliases={3: 0})(src3, dout3, g, dp)


def final_loss(x3, g, tgt3):
    B, S, C = x3.shape
    ts = _pick(S, (512, 256, 128))

    def body(x_ref, g_ref, t_ref, dx_ref, dg_ref, loss_ref):
        @pl.when(_first())
        def _():
            dg_ref[...] = jnp.zeros_like(dg_ref)
            loss_ref[...] = jnp.zeros_like(loss_ref)

        x, gv = x_ref[0], g_ref[...]
        r = lax.rsqrt(jnp.mean(x * x, axis=-1, keepdims=True) + EPS)
        xh = x * r
        e = xh * gv - t_ref[0]
        loss_ref[...] += 0.5 * jnp.sum(e * e) / C
        dout = e / C
        dg_ref[...] += _rowsum(dout * xh)
        dxh = dout * gv
        dx_ref[0] = r * (dxh - xh * jnp.mean(dxh * xh, axis=-1, keepdims=True))

    return _row_call(body, "final_loss", B, S, ts, [_tok(ts, C), _const(1, C), _tok(ts, C)],
                     [_tok(ts, C), _const(1, C), _const(1, 128)],
                     [jax.ShapeDtypeStruct((B, S, C), f32), jax.ShapeDtypeStruct((1, C), f32),
                      jax.ShapeDtypeStruct((1, 128), f32)])(x3, g, tgt3)


def ssd_out_fwd(yc3, p3, w):
    B, S, C = yc3.shape
    ts = _pick(S, (512, 256, 128))
    half = C // 2

    def body(y_ref, z_ref, w_ref, o_ref):
        y = y_ref[0] * _silu(z_ref[0])
        for lo in (0, half):
            yg = y[:, lo:lo + half]
            r = lax.rsqrt(jnp.mean(yg * yg, axis=-1, keepdims=True) + EPS)
            o_ref[0, :, lo:lo + half] = (yg * r * w_ref[:, lo:lo + half]).astype(bf16)

    return _row_call(body, "ssd_out_fwd", B, S, ts, [_tok(ts, C), _tok(ts, C, 0), _const(1, C)], _tok(ts, C),
                     jax.ShapeDtypeStruct((B, S, C), bf16))(yc3, p3, w)


def ssd_out_bwd(yc3, p3, dcat3, w):
    B, S, C = yc3.shape
    ts = _pick(S, (512, 256, 128))
    half = C // 2

    def body(y_ref, z_ref, do_ref, w_ref, dyc_ref, dz_ref, dw_ref):
        @pl.when(_first())
        def _():
            dw_ref[...] = jnp.zeros_like(dw_ref)

        yc, z, do = y_ref[0], z_ref[0], do_ref[0].astype(f32)
        sz = _silu(z)
        y = yc * sz
        for lo in (0, half):
            sl = slice(lo, lo + half)
            yg, dog, wg = y[:, sl], do[:, sl], w_ref[:, sl]
            r = lax.rsqrt(jnp.mean(yg * yg, axis=-1, keepdims=True) + EPS)
            yh = yg * r
            dw_ref[:, sl] += _rowsum(dog * yh)
            dyh = dog * wg
            dy = r * (dyh - yh * jnp.mean(dyh * yh, axis=-1, keepdims=True))
            dyc_ref[0, :, sl] = dy * sz[:, sl]
            dz_ref[0, :, sl] = (dy * yc[:, sl] * _dsilu(z[:, sl])).astype(bf16)

    return _row_call(body, "ssd_out_bwd", B, S, ts, [_tok(ts, C), _tok(ts, C, 0), _tok(ts, C, 0), _const(1, C)],
                     [_tok(ts, C), _tok(ts, C, 0), _const(1, C)],
                     [jax.ShapeDtypeStruct((B, S, C), f32), jax.ShapeDtypeStruct((B, S, IN_COLS), bf16),
                      jax.ShapeDtypeStruct((1, C), f32)])(yc3, p3, dcat3, w)


def _rot(t):
    lane = lax.broadcasted_iota(jnp.int32, t.shape, 1)
    return jnp.where(lane < ROPE // 2, -pltpu.roll(t, 128 - ROPE // 2, 1), pltpu.roll(t, ROPE // 2, 1))


def _rope(t, cosf, sinf):
    return t * cosf + _rot(t) * sinf


def _rope_t(d, cosf, sinf):
    return d * cosf - _rot(d * sinf)


def kprep_fwd(kv3, p3, cosf, sinf):
    B, S, _ = kv3.shape
    ts = _pick(S, (512, 256, 128))
    Wn = MLA_H * NOPE

    def body(k_ref, v_ref, kr_ref, c_ref, s_ref, ko_ref, vo_ref):
        lane = lax.broadcasted_iota(jnp.int32, (1, 128), 1)
        kr = jnp.where(lane < ROPE, kr_ref[0], 0.0)
        kr = _rope(kr, c_ref[0], s_ref[0]).astype(bf16)
        for h in range(MLA_H):
            ko_ref[0, :, h * 256:h * 256 + 128] = k_ref[0, :, h * 128:(h + 1) * 128].astype(bf16)
            ko_ref[0, :, h * 256 + 128:(h + 1) * 256] = kr
        vo_ref[0] = v_ref[0].astype(bf16)

    return _row_call(body, "kprep_fwd", B, S, ts,
                     [_tok(ts, Wn, 0), _tok(ts, Wn, 1), _tok(ts, 128, OFF_KRDT // 128), _tok(ts, 128), _tok(ts, 128)],
                     [_tok(ts, 2 * Wn), _tok(ts, Wn)],
                     [jax.ShapeDtypeStruct((B, S, 2 * Wn), bf16), jax.ShapeDtypeStruct((B, S, Wn), bf16)])(kv3, kv3, p3, cosf, sinf)


def kprep_bwd(dkr3, ddt3, cosf, sinf, dp):
    B, S, W = dkr3.shape
    ts = _pick(S, (512, 256, 128))

    def body(dk_ref, ddt_ref, c_ref, s_ref, dp_in, kr_ref):
        acc = dk_ref[0, :, 0:128]
        for h in range(1, W // 128):
            acc = acc + dk_ref[0, :, h * 128:(h + 1) * 128]
        lane = lax.broadcasted_iota(jnp.int32, (1, 128), 1)
        kr_ref[0] = jnp.where(lane < ROPE, _rope_t(acc, c_ref[0], s_ref[0]), ddt_ref[0]).astype(bf16)

    return _row_call(body, "kprep_bwd", B, S, ts,
                     [_tok(ts, W), _tok(ts, 128), _tok(ts, 128), _tok(ts, 128), pl.BlockSpec(memory_space=pl.ANY)],
                     _tok(ts, 128, OFF_KRDT // 128), jax.ShapeDtypeStruct(dp.shape, bf16), aliases={4: 0})(dkr3, ddt3, cosf, sinf, dp)


def _shift_down(u, j):
    if j == 0:
        return u
    row = lax.broadcasted_iota(jnp.int32, u.shape, 0)
    return jnp.where(row < j, 0.0, pltpu.roll(u, j, 0))


def _shift_up(u, j):
    if j == 0:
        return u
    n = u.shape[0]
    row = lax.broadcasted_iota(jnp.int32, u.shape, 0)
    return jnp.where(row >= n - j, 0.0, pltpu.roll(u, n - j, 0))


def _conv(u, w, b, K):
    out = b
    for j in range(K):
        out = out + w[K - 1 - j:K - j, :] * _shift_down(u, j)
    return out


def _conv_bwd(u, du, w, K):
    dins = w[K - 1:K, :] * du
    dws = [None] * K
    dws[K - 1] = _rowsum(du * u)
    for j in range(1, K):
        sd = _shift_up(du, j)
        dins = dins + w[K - 1 - j:K - j, :] * sd
        dws[K - 1 - j] = _rowsum(sd * u)
    return dins, dws


CW = 256


def conv_ssd_fwd(p3, w, b):
    B, S, _ = p3.shape
    nb = D_XBC // CW

    def body(u_ref, w_ref, b_ref, o_ref, pre_ref):
        pre = _conv(u_ref[0], w_ref[...], b_ref[...], CONV_K)
        o_ref[0] = _silu(pre)
        pre_ref[0] = pre.astype(bf16)

    out = pl.BlockSpec((1, S, CW), lambda b, j: (b, 0, j))
    return pl.pallas_call(body, name="conv_ssd_fwd", grid=(B, nb),
                          in_specs=[pl.BlockSpec((1, S, CW), lambda b, j: (b, 0, OFF_XBC // CW + j)),
                                    pl.BlockSpec((CONV_K, CW), lambda b, j: (0, j)),
                                    pl.BlockSpec((1, CW), lambda b, j: (0, j))],
                          out_specs=[out, out],
                          out_shape=[jax.ShapeDtypeStruct((B, S, D_XBC), f32), jax.ShapeDtypeStruct((B, S, D_XBC), bf16)],
                          compiler_params=_cparams(2))(p3, w, b)


def conv_ssd_bwd(p3, pre3, dxc3, w, dp):
    B, S, _ = p3.shape
    nb = D_XBC // CW

    def body(u_ref, pre_ref, d_ref, w_ref, dp_in, du_ref, dw_ref, db_ref):
        @pl.when(pl.program_id(1) == 0)
        def _():
            dw_ref[...] = jnp.zeros_like(dw_ref)
            db_ref[...] = jnp.zeros_like(db_ref)

        u, wv = u_ref[0], w_ref[...]
        dpre = d_ref[0] * _dsilu(pre_ref[0].astype(f32))
        dins, dws = _conv_bwd(u, dpre, wv, CONV_K)
        du_ref[0] = dins.astype(bf16)
        for k in range(CONV_K):
            dw_ref[k:k + 1, :] += dws[k]
        db_ref[...] += _rowsum(dpre)

    return pl.pallas_call(body, name="conv_ssd_bwd", grid=(nb, B),
                          in_specs=[pl.BlockSpec((1, S, CW), lambda j, b: (b, 0, OFF_XBC // CW + j)),
                                    pl.BlockSpec((1, S, CW), lambda j, b: (b, 0, j)),
                                    pl.BlockSpec((1, S, CW), lambda j, b: (b, 0, j)),
                                    pl.BlockSpec((CONV_K, CW), lambda j, b: (0, j)), pl.BlockSpec(memory_space=pl.ANY)],
                          out_specs=[pl.BlockSpec((1, S, CW), lambda j, b: (b, 0, OFF_XBC // CW + j)),
                                     pl.BlockSpec((CONV_K, CW), lambda j, b: (0, j)),
                                     pl.BlockSpec((1, CW), lambda j, b: (0, j))],
                          out_shape=[jax.ShapeDtypeStruct(dp.shape, bf16), jax.ShapeDtypeStruct((CONV_K, D_XBC), f32),
                                     jax.ShapeDtypeStruct((1, D_XBC), f32)], input_output_aliases={4: 0},
                          compiler_params=_cparams(2))(p3, pre3, dxc3, w, dp)


def glu_fwd(u3, w, b):
    B, S, _ = u3.shape
    nb = D_FF // CW

    def body(ug_ref, uv_ref, wg_ref, wv_ref, bg_ref, bv_ref, o_ref, g_ref, v_ref):
        g = _conv(ug_ref[0].astype(f32), wg_ref[...], bg_ref[...], FF_K)
        v = _conv(uv_ref[0].astype(f32), wv_ref[...], bv_ref[...], FF_K)
        o_ref[0] = (_silu(g) * v).astype(bf16)
        g_ref[0] = g.astype(bf16)
        v_ref[0] = v.astype(bf16)

    def blk(off):
        return pl.BlockSpec((1, S, CW), lambda b, j: (b, 0, off + j))

    def par(rows, off):
        return pl.BlockSpec((rows, CW), lambda b, j: (0, off + j))

    shp = jax.ShapeDtypeStruct((B, S, D_FF), bf16)
    return pl.pallas_call(body, name="glu_fwd", grid=(B, nb),
                          in_specs=[blk(0), blk(nb), par(FF_K, 0), par(FF_K, nb), par(1, 0), par(1, nb)],
                          out_specs=[blk(0)] * 3, out_shape=[shp] * 3, compiler_params=_cparams(2))(u3, u3, w, w, b, b)


def glu_bwd(u3, g3, v3, da3, w):
    B, S, _ = u3.shape
    nb = D_FF // CW

    def body(ug_ref, uv_ref, g_ref, v_ref, da_ref, wg_ref, wv_ref, dug_ref, duv_ref, dwg_ref, dwv_ref, dbg_ref, dbv_ref):
        @pl.when(pl.program_id(1) == 0)
        def _():
            for r in (dwg_ref, dwv_ref, dbg_ref, dbv_ref):
                r[...] = jnp.zeros_like(r)

        ug, uv, da, wg, wv = ug_ref[0].astype(f32), uv_ref[0].astype(f32), da_ref[0].astype(f32), wg_ref[...], wv_ref[...]
        g, v = g_ref[0].astype(f32), v_ref[0].astype(f32)
        dg = da * v * _dsilu(g)
        dv = da * _silu(g)
        ding, dwsg = _conv_bwd(ug, dg, wg, FF_K)
        dinv, dwsv = _conv_bwd(uv, dv, wv, FF_K)
        dug_ref[0] = ding.astype(bf16)
        duv_ref[0] = dinv.astype(bf16)
        for k in range(FF_K):
            dwg_ref[k:k + 1, :] += dwsg[k]
            dwv_ref[k:k + 1, :] += dwsv[k]
        dbg_ref[...] += _rowsum(dg)
        dbv_ref[...] += _rowsum(dv)

    def blk(off):
        return pl.BlockSpec((1, S, CW), lambda j, b: (b, 0, off + j))

    def par(rows, off):
        return pl.BlockSpec((rows, CW), lambda j, b: (0, off + j))

    return pl.pallas_call(body, name="glu_bwd", grid=(nb, B),
                          in_specs=[blk(0), blk(nb), blk(0), blk(0), blk(0), par(FF_K, 0), par(FF_K, nb)],
                          out_specs=[blk(0), blk(0), par(FF_K, 0), par(FF_K, 0), par(1, 0), par(1, 0)],
                          out_shape=[jax.ShapeDtypeStruct((B, S, D_FF), bf16), jax.ShapeDtypeStruct((B, S, D_FF), bf16),
                                     jax.ShapeDtypeStruct((FF_K, D_FF), f32), jax.ShapeDtypeStruct((FF_K, D_FF), f32),
                                     jax.ShapeDtypeStruct((1, D_FF), f32), jax.ShapeDtypeStruct((1, D_FF), f32)],
                          compiler_params=_cparams(2))(u3, u3, g3, v3, da3, w, w)


def _ssd_decay(dtb, bias_row, alog_row):
    lane = lax.broadcasted_iota(jnp.int32, (1, 128), 1)
    hmask = (lane >= DT_LANE) & (lane < DT_LANE + SSD_HEADS)
    dt = jnp.where(hmask, jax.nn.softplus(dtb + bias_row), 0.0)
    a = dt * jnp.where(hmask, -jnp.exp(alog_row), 0.0)
    r = lax.broadcasted_iota(jnp.int32, (CHUNK, CHUNK), 0)
    c = lax.broadcasted_iota(jnp.int32, (CHUNK, CHUNK), 1)
    cs = jnp.dot((r >= c).astype(f32), a, precision=HIGHEST, preferred_element_type=f32)
    return dt, cs


def _expand(xt):
    return jnp.concatenate([jnp.broadcast_to(xt[DT_LANE + h:DT_LANE + h + 1, :], (SSD_HD, xt.shape[1]))
                            for h in range(SSD_HEADS)], axis=0)


_NT = (((1,), (1,)), ((), ()))
_TN = (((0,), (0,)), ((), ()))
GH = SSD_HEADS // 2
GR = GH * SSD_HD


def ssd_fwd(xc3, p3, bias_row, alog_row, dcol):
    B, S, _ = xc3.shape
    nc = S // CHUNK

    def body(xs_ref, bc_ref, dtb_ref, bias_ref, alog_ref, dcol_ref, y_ref, st_ref, state, yT):
        @pl.when(pl.program_id(1) == 0)
        def _():
            state[...] = jnp.zeros_like(state)

        dt, cs = _ssd_decay(dtb_ref[0], bias_ref[...], alog_ref[...])
        csT = cs.T
        eT = jnp.exp(csT)
        decX = _expand(jnp.exp(csT[:, CHUNK - 1:CHUNK] - csT))
        eX = _expand(eT)
        elastX = eX[:, CHUNK - 1:CHUNK]
        xsT = xs_ref[0].T
        uT = xsT * _expand(dt.T)
        bc = bc_ref[0]
        st_ref[0, 0] = state[...]
        srow = lax.broadcasted_iota(jnp.int32, (CHUNK, CHUNK), 0)
        lcol = lax.broadcasted_iota(jnp.int32, (CHUNK, CHUNK), 1)
        for g in range(2):
            Bg = bc[:, g * SSD_N:(g + 1) * SSD_N].astype(bf16)
            Cg = bc[:, (2 + g) * SSD_N:(3 + g) * SSD_N].astype(bf16)
            GT = lax.dot_general(Bg, Cg, _NT, preferred_element_type=f32)
            rows = slice(g * GR, (g + 1) * GR)
            Sg = state[rows]
            yoffT = lax.dot_general(Sg.astype(bf16), Cg, _NT, preferred_element_type=f32) * eX[rows]
            state[rows] = Sg * elastX[rows] + jnp.dot((uT[rows] * decX[rows]).astype(bf16), Bg, preferred_element_type=f32)
            for k in range(GH):
                h = g * GH + k
                hr = slice(h * SSD_HD, (h + 1) * SSD_HD)
                seg = csT[DT_LANE + h:DT_LANE + h + 1, :] - cs[:, DT_LANE + h:DT_LANE + h + 1]
                LT = jnp.where(lcol >= srow, jnp.exp(jnp.minimum(seg, 0.0)), 0.0)
                yT[hr] = (jnp.dot(uT[hr].astype(bf16), (GT * LT).astype(bf16), preferred_element_type=f32)
                          + yoffT[k * SSD_HD:(k + 1) * SSD_HD] + dcol_ref[hr] * xsT[hr])
        y_ref[0] = yT[...].T

    return pl.pallas_call(body, name="ssd_fwd", grid=(B, nc),
                          in_specs=[pl.BlockSpec((1, CHUNK, D_SSD), lambda b, c: (b, c, 0)),
                                    pl.BlockSpec((1, CHUNK, 512), lambda b, c: (b, c, 2)),
                                    pl.BlockSpec((1, CHUNK, 128), lambda b, c: (b, c, OFF_KRDT // 128)),
                                    _const(1, 128), _const(1, 128), _const(D_SSD, 1)],
                          out_specs=[pl.BlockSpec((1, CHUNK, D_SSD), lambda b, c: (b, c, 0)),
                                     pl.BlockSpec((1, 1, D_SSD, SSD_N), lambda b, c: (b, c, 0, 0))],
                          out_shape=[jax.ShapeDtypeStruct((B, S, D_SSD), f32), jax.ShapeDtypeStruct((B, nc, D_SSD, SSD_N), f32)],
                          scratch_shapes=[pltpu.VMEM((D_SSD, SSD_N), f32), pltpu.VMEM((D_SSD, CHUNK), f32)],
                          compiler_params=_cparams(2))(xc3, xc3, p3, bias_row, alog_row, dcol)


def ssd_bwd(xc3, p3, dy3, states, bias_row, alog_row, bias_col, alog_col, dcol, comm=None):
    B, S, _ = xc3.shape
    nc = S // CHUNK

    def body(xs_ref, bc_ref, dtb_ref, dy_ref, st_ref, bias_ref, alog_ref, biasc_ref, alogc_ref, dcol_ref,
             dxc_ref, ddt_ref, dalog_ref, dd_ref, dbias_ref, dS, dUT, accA, accD, accB, dcs_diag):
        @pl.when(pl.program_id(1) == 0)
        def _():
            dS[...] = jnp.zeros_like(dS)

        @pl.when(_first())
        def _():
            accA[...] = jnp.zeros_like(accA)
            accD[...] = jnp.zeros_like(accD)
            accB[...] = jnp.zeros_like(accB)

        dtb = dtb_ref[0]
        dt, cs = _ssd_decay(dtb, bias_ref[...], alog_ref[...])
        dtT, csT = dt.T, cs.T
        decX = _expand(jnp.exp(csT[:, CHUNK - 1:CHUNK] - csT))
        eX = _expand(jnp.exp(csT))
        dtX = _expand(dtT)
        elastX = eX[:, CHUNK - 1:CHUNK]
        xsT = xs_ref[0].T
        uT = xsT * dtX
        dYT = dy_ref[0].T
        bc = bc_ref[0]
        lrow = lax.broadcasted_iota(jnp.int32, (CHUNK, CHUNK), 0)
        scol = lax.broadcasted_iota(jnp.int32, (CHUNK, CHUNK), 1)
        dcs_diag[...] = jnp.zeros_like(dcs_diag)
        rs_cols = jnp.zeros((CHUNK, 128), f32)
        vparts, zparts = [], []
        for g in range(2):
            Bf = bc[:, g * SSD_N:(g + 1) * SSD_N]
            Bg = Bf.astype(bf16)
            Cg = bc[:, (2 + g) * SSD_N:(3 + g) * SSD_N].astype(bf16)
            G = lax.dot_general(Cg, Bg, _NT, preferred_element_type=f32)
            BgT = Bf.T.astype(bf16)
            rows = slice(g * GR, (g + 1) * GR)
            dSg = dS[rows]
            Sg = st_ref[0, 0, rows, :]
            dUst = jnp.dot(dSg.astype(bf16), BgT, preferred_element_type=f32) * decX[rows]
            yoffT = lax.dot_general(Sg.astype(bf16), Cg, _NT, preferred_element_type=f32) * eX[rows]
            zparts.append(dYT[rows] * yoffT - dUst * uT[rows])
            dG = jnp.zeros((CHUNK, CHUNK), f32)
            for k in range(GH):
                h = g * GH + k
                hr = slice(h * SSD_HD, (h + 1) * SSD_HD)
                seg = cs[:, DT_LANE + h:DT_LANE + h + 1] - csT[DT_LANE + h:DT_LANE + h + 1, :]
                L = jnp.where(lrow >= scol, jnp.exp(jnp.minimum(seg, 0.0)), 0.0)
                M = G * L
                dYh = dYT[hr].astype(bf16)
                dUT[hr] = jnp.dot(dYh, M.astype(bf16), preferred_element_type=f32) + dUst[k * SSD_HD:(k + 1) * SSD_HD]
                dM = lax.dot_general(dYh, uT[hr].astype(bf16), _TN, preferred_element_type=f32)
                dG = dG + dM * L
                Wm = dM * M
                rs_cols = jnp.where(scol == DT_LANE + h, jnp.sum(Wm, axis=1, keepdims=True), rs_cols)
                dcs_diag[DT_LANE + h:DT_LANE + h + 1, :] = -_rowsum(Wm)
            dGb = dG.astype(bf16)
            dYe = (dYT[rows] * eX[rows]).astype(bf16)
            ude = (uT[rows] * decX[rows]).astype(bf16)
            dC = jnp.dot(dGb, Bg, preferred_element_type=f32) + lax.dot_general(dYe, Sg.astype(bf16), _TN, preferred_element_type=f32)
            dB = (lax.dot_general(dGb, Cg, _TN, preferred_element_type=f32)
                  + lax.dot_general(ude, dSg.astype(bf16), _TN, preferred_element_type=f32))
            dxc_ref[0, :, D_SSD + g * SSD_N:D_SSD + (g + 1) * SSD_N] = dB
            dxc_ref[0, :, D_SSD + (2 + g) * SSD_N:D_SSD + (3 + g) * SSD_N] = dC
            vparts.append(elastX[rows] * jnp.sum(dSg * Sg, axis=1, keepdims=True)
                          + jnp.sum(dUst * uT[rows], axis=1, keepdims=True))
            dS[rows] = elastX[rows] * dSg + jnp.dot(dYe, Cg, preferred_element_type=f32)
        dU = dUT[...]
        dcv = dcol_ref[...]
        dxc_ref[0, :, 0:D_SSD] = (dtX * dU + dcv * dYT).T
        lane = lax.broadcasted_iota(jnp.int32, (D_SSD, CHUNK), 1)
        Z = jnp.concatenate(zparts, axis=0) + jnp.where(lane == CHUNK - 1, jnp.concatenate(vparts, axis=0), 0.0)
        hr_ = lax.broadcasted_iota(jnp.int32, (128, D_SSD), 0)
        hc_ = lax.broadcasted_iota(jnp.int32, (128, D_SSD), 1)
        hsel = (hr_ - DT_LANE == jnp.right_shift(hc_, 6)).astype(bf16)
        summands = jnp.concatenate([Z, dU * xsT, dYT * xsT], axis=1)
        hi = summands.astype(bf16)
        lo = (summands - hi.astype(f32)).astype(bf16)
        red = jnp.dot(hsel, hi, preferred_element_type=f32) + jnp.dot(hsel, lo, preferred_element_type=f32)
        dcsT = red[:, 0:CHUNK] + dcs_diag[...] + rs_cols.T
        daT = jnp.dot(dcsT, (lrow >= scol).astype(f32), precision=HIGHEST, preferred_element_type=f32)
        rowi = lax.broadcasted_iota(jnp.int32, (128, 1), 0)
        hmask = (rowi >= DT_LANE) & (rowi < DT_LANE + SSD_HEADS)
        a_col = jnp.where(hmask, -jnp.exp(alogc_ref[...]), 0.0)
        ddtT = red[:, CHUNK:2 * CHUNK] + a_col * daT
        ddt_rawT = jnp.where(hmask, ddtT * jax.nn.sigmoid(dtb.T + biasc_ref[...]), 0.0)
        ddt_ref[0] = ddt_rawT.T
        accA[...] += daT * dtT
        accD[...] += red[:, 2 * CHUNK:3 * CHUNK]
        accB[...] += ddt_rawT

        @pl.when((pl.program_id(0) == B - 1) & (pl.program_id(1) == nc - 1))
        def _():
            dalog_ref[...] = jnp.broadcast_to(jnp.sum(accA[...], axis=1, keepdims=True) * a_col, (128, 128))
            dd_ref[...] = jnp.broadcast_to(jnp.sum(accD[...], axis=1, keepdims=True), (128, 128))
            dbias_ref[...] = jnp.broadcast_to(jnp.sum(accB[...], axis=1, keepdims=True), (128, 128))

    def rev(width, cb):
        return pl.BlockSpec((1, CHUNK, width), lambda b, c: (b, nc - 1 - c, cb))

    acc_spec = pl.BlockSpec((128, 128), lambda b, c: (0, 0))
    acc_shape = jax.ShapeDtypeStruct((128, 128), f32)
    body, c_args, c_in, c_out, c_shapes, c_sems = _fuse_exchange(body, comm, 10, 5, 6, (B, nc))
    res = pl.pallas_call(body, name="ssd_bwd_x" if comm else "ssd_bwd", grid=(B, nc),
                         in_specs=[rev(D_SSD, 0), rev(512, 2), rev(128, OFF_KRDT // 128), rev(D_SSD, 0),
                                   pl.BlockSpec((1, 1, D_SSD, SSD_N), lambda b, c: (b, nc - 1 - c, 0, 0)),
                                   _const(1, 128), _const(1, 128), _const(128, 1), _const(128, 1), _const(D_SSD, 1)] + c_in,
                         out_specs=[rev(D_XBC, 0), rev(128, 0), acc_spec, acc_spec, acc_spec] + c_out,
                         out_shape=[jax.ShapeDtypeStruct((B, S, D_XBC), f32), jax.ShapeDtypeStruct((B, S, 128), f32),
                                    acc_shape, acc_shape, acc_shape] + c_shapes,
                         scratch_shapes=[pltpu.VMEM((D_SSD, SSD_N), f32), pltpu.VMEM((D_SSD, CHUNK), f32),
                                         pltpu.VMEM((128, 128), f32), pltpu.VMEM((128, 128), f32), pltpu.VMEM((128, 128), f32),
                                         pltpu.VMEM((128, 128), f32)] + c_sems,
                         compiler_params=_cparams(2))(xc3, xc3, p3, dy3, states, bias_row, alog_row, bias_col, alog_col, dcol, *c_args)
    return (*res[:5], list(res[5:]))


ATT_SCALE = float(QK) ** -0.5
LOG2E = 1.4426950408889634
LN2 = 0.6931471805599453
Q_FOLD = ATT_SCALE * LOG2E
NEG = -1e30
HP = 2


def _att_block(S):
    return _pick(S, (512, 256, 128))


def attn_fwd(q3, k3, v3, comm=None):
    B, S, _ = q3.shape
    bq = _att_block(S)
    nq = S // bq

    def body(q_ref, k_ref, v_ref, o_ref, lse_ref, m_s, l_s, acc):
        i, j = pl.program_id(2), pl.program_id(3)

        @pl.when(j == 0)
        def _():
            m_s[...] = jnp.full_like(m_s, NEG)
            l_s[...] = jnp.zeros_like(l_s)
            acc[...] = jnp.zeros_like(acc)

        def step(masked):
            for t in range(HP):
                qk = slice(t * 256, (t + 1) * 256)
                st = lax.dot_general(k_ref[0, :, qk], q_ref[0, :, qk], _NT, preferred_element_type=f32)
                if masked:
                    r = lax.broadcasted_iota(jnp.int32, (bq, bq), 0)
                    c = lax.broadcasted_iota(jnp.int32, (bq, bq), 1)
                    st = jnp.where(c >= r, st, NEG)
                m_old = m_s[t]
                m_new = jnp.maximum(m_old, jnp.max(st, axis=0, keepdims=True))
                alpha = jnp.exp2(m_old - m_new)
                pt = jnp.exp2(st - m_new)
                l_s[t] = alpha * l_s[t] + jnp.sum(pt, axis=0, keepdims=True)
                acc[t] = alpha * acc[t] + lax.dot_general(v_ref[0, :, t * VD:(t + 1) * VD], pt.astype(bf16), _TN,
                                                          preferred_element_type=f32)
                m_s[t] = m_new

        @pl.when(j < i)
        def _():
            step(False)

        @pl.when(j == i)
        def _():
            step(True)
            for t in range(HP):
                o_ref[0, :, t * VD:(t + 1) * VD] = (acc[t] / l_s[t]).T
                lse_ref[0, t] = m_s[t] + jnp.log2(l_s[t])

    grid = (B, MLA_H // HP, nq, nq)
    body, c_args, c_in, c_out, c_shapes, c_sems = _fuse_exchange(body, comm, 3, 2, 3, grid)
    res = pl.pallas_call(body, name="attn_fwd_x" if comm else "attn_fwd", grid=grid,
                         in_specs=[pl.BlockSpec((1, bq, HP * 256), lambda b, h, i, j: (b, i, h)),
                                   pl.BlockSpec((1, bq, HP * 256), lambda b, h, i, j: (b, jnp.minimum(j, i), h)),
                                   pl.BlockSpec((1, bq, HP * VD), lambda b, h, i, j: (b, jnp.minimum(j, i), h))] + c_in,
                         out_specs=[pl.BlockSpec((1, bq, HP * VD), lambda b, h, i, j: (b, i, h)),
                                    pl.BlockSpec((1, HP, 1, bq), lambda b, h, i, j: (b, h, 0, i))] + c_out,
                         out_shape=[jax.ShapeDtypeStruct((B, S, MLA_H * VD), f32), jax.ShapeDtypeStruct((B, MLA_H, 1, S), f32)] + c_shapes,
                         scratch_shapes=[pltpu.VMEM((HP, 1, bq), f32), pltpu.VMEM((HP, 1, bq), f32), pltpu.VMEM((HP, VD, bq), f32)] + c_sems,
                         compiler_params=_cparams(4))(q3, k3, v3, *c_args)
    return res[0], res[1], list(res[2:])


def rms_o_bwd(o3, dcat3, g):
    B, S, C = o3.shape
    ts = _pick(S, (512, 256, 128))

    def body(x_ref, do_ref, g_ref, dx_ref, dg_ref, d_ref):
        @pl.when(_first())
        def _():
            dg_ref[...] = jnp.zeros_like(dg_ref)

        x, do = x_ref[0], do_ref[0].astype(f32)
        r = lax.rsqrt(jnp.mean(x * x, axis=-1, keepdims=True) + EPS)
        xh = x * r
        dg_ref[...] += _rowsum(do * xh)
        dxh = do * g_ref[...]
        dx = r * (dxh - xh * jnp.mean(dxh * xh, axis=-1, keepdims=True))
        dx_ref[0] = dx
        for h in range(MLA_H):
            vs = slice(h * VD, (h + 1) * VD)
            d_ref[0, h] = jnp.sum(dx[:, vs] * x[:, vs], axis=-1, keepdims=True)

    return _row_call(body, "rms_o_bwd", B, S, ts, [_tok(ts, C), _tok(ts, C, 1), _const(1, C)],
                     [_tok(ts, C), _const(1, C), pl.BlockSpec((1, MLA_H, ts, 1), lambda b, s: (b, 0, s, 0))],
                     [jax.ShapeDtypeStruct((B, S, C), f32), jax.ShapeDtypeStruct((1, C), f32),
                      jax.ShapeDtypeStruct((B, MLA_H, S, 1), f32)])(o3, dcat3, g)


def attn_bwd(q3, k3, v3, do3, lse_row, delta_row, cosf, sinf, comm=None):
    B, S, _ = q3.shape
    bq = _att_block(S)
    nq = S // bq

    def body(q_ref, k_ref, v_ref, do_ref, lse_ref, dl_ref, cos_ref, sin_ref, dkn_ref, dv_ref, dkr_ref, dq_hbm, dk_acc, dv_acc,
             dq_scr, stage, dq_sem):
        b, hp, j, i = pl.program_id(0), pl.program_id(1), pl.program_id(2), pl.program_id(3)
        rows = pl.ds(pl.multiple_of(i * bq, bq), bq)

        @pl.when((j == 0) & (i == 0))
        def _():
            dq_scr[...] = jnp.zeros_like(dq_scr)

        @pl.when(i == 0)
        def _():
            dk_acc[...] = jnp.zeros_like(dk_acc)
            dv_acc[...] = jnp.zeros_like(dv_acc)

        def step(masked):
            for t in range(HP):
                qk, vs = slice(t * 256, (t + 1) * 256), slice(t * VD, (t + 1) * VD)
                q, k = q_ref[0, :, qk], k_ref[0, :, qk]
                do = do_ref[0, :, vs].astype(bf16)
                pt = jnp.exp2(lax.dot_general(k, q, _NT, preferred_element_type=f32) - lse_ref[0, t])
                if masked:
                    r = lax.broadcasted_iota(jnp.int32, (bq, bq), 0)
                    c = lax.broadcasted_iota(jnp.int32, (bq, bq), 1)
                    pt = jnp.where(c >= r, pt, 0.0)
                dv_acc[t] += jnp.dot(pt.astype(bf16), do, preferred_element_type=f32)
                dpt = lax.dot_general(v_ref[0, :, vs], do, _NT, preferred_element_type=f32)
                dst = (pt * (dpt - dl_ref[0, t])).astype(bf16)
                dk_acc[t] += jnp.dot(dst, q, preferred_element_type=f32)
                dq_scr[t, rows, :] += lax.dot_general(dst, k, _TN, preferred_element_type=f32)

        @pl.when(i > j)
        def _():
            step(False)

        @pl.when(i == j)
        def _():
            step(True)
            for t in range(HP):
                d = dq_scr[t, rows, :] * (LN2 * Q_FOLD)
                stage[t, :, 0:NOPE] = d[:, 0:NOPE].astype(bf16)
                stage[t, :, NOPE:] = _rope_t(d[:, NOPE:], cos_ref[0], sin_ref[0]).astype(bf16)
                cp = pltpu.make_async_copy(stage.at[t], dq_hbm.at[b, rows, pl.ds(pl.multiple_of((hp * HP + t) * 256, 256), 256)],
                                           dq_sem.at[t])
                cp.start()
                cp.wait()

        @pl.when(i == nq - 1)
        def _():
            kr = jnp.zeros((bq, 128), f32)
            for t in range(HP):
                dkn_ref[0, :, t * NOPE:(t + 1) * NOPE] = (dk_acc[t, :, 0:NOPE] * LN2).astype(bf16)
                dv_ref[0, :, t * VD:(t + 1) * VD] = dv_acc[t].astype(bf16)
                kr = kr + dk_acc[t, :, NOPE:]
            dkr_ref[0] = kr * LN2

    kspec = pl.BlockSpec((1, bq, HP * 256), lambda b, h, j, i: (b, j, h))
    vspec = pl.BlockSpec((1, bq, HP * VD), lambda b, h, j, i: (b, j, h))
    krspec = pl.BlockSpec((1, bq, 128), lambda b, h, j, i: (b, j, h))
    rspec = pl.BlockSpec((1, HP, 1, bq), lambda b, h, j, i: (b, h, 0, jnp.maximum(i, j)))
    tspec = pl.BlockSpec((1, bq, 128), lambda b, h, j, i: (b, jnp.maximum(i, j), 0))
    grid = (B, MLA_H // HP, nq, nq)
    body, c_args, c_in, c_out, c_shapes, c_sems = _fuse_exchange(body, comm, 8, 4, 5, grid)
    res = pl.pallas_call(body, name="attn_bwd_x" if comm else "attn_bwd", grid=grid,
                         in_specs=[pl.BlockSpec((1, bq, HP * 256), lambda b, h, j, i: (b, jnp.maximum(i, j), h)), kspec, vspec,
                                   pl.BlockSpec((1, bq, HP * VD), lambda b, h, j, i: (b, jnp.maximum(i, j), h)), rspec, rspec,
                                   tspec, tspec] + c_in,
                         out_specs=[vspec, vspec, krspec, pl.BlockSpec(memory_space=pltpu.HBM)] + c_out,
                         out_shape=[jax.ShapeDtypeStruct((B, S, MLA_H * NOPE), bf16), jax.ShapeDtypeStruct((B, S, MLA_H * VD), bf16),
                                    jax.ShapeDtypeStruct((B, S, MLA_H // HP * 128), f32),
                                    jax.ShapeDtypeStruct((B, S, MLA_H * 256), bf16)] + c_shapes,
                         scratch_shapes=[pltpu.VMEM((HP, bq, 256), f32), pltpu.VMEM((HP, bq, VD), f32), pltpu.VMEM((HP, S, 256), f32),
                                         pltpu.VMEM((HP, bq, 256), bf16), pltpu.SemaphoreType.DMA((HP,))] + c_sems,
                         compiler_params=_cparams(4))(q3, k3, v3, do3, lse_row, delta_row, cosf, sinf, *c_args)
    return res[3], res[0], res[1], res[2], list(res[4:])


def ada_fwd(c_all, w, b):
    n = w.shape[1]

    def body(c_ref, w_ref, b_ref, o_ref):
        o_ref[...] = jnp.dot(_silu(c_ref[...]).astype(bf16), w_ref[...].astype(bf16), preferred_element_type=f32) + b_ref[...]

    return pl.pallas_call(body, name="ada_fwd", out_shape=jax.ShapeDtypeStruct((c_all.shape[0], n), f32),
                          compiler_params=pltpu.CompilerParams(vmem_limit_bytes=VMEM_LIMIT))(c_all, w, b)


def ada_bwd(c_all, dmod):
    n = dmod.shape[1]

    def body(c_ref, d_ref, o_ref):
        o_ref[...] = lax.dot_general(_silu(c_ref[...]).astype(bf16), d_ref[...].astype(bf16), _TN, preferred_element_type=f32)

    return pl.pallas_call(body, name="ada_bwd", out_shape=jax.ShapeDtypeStruct((c_all.shape[1], n), f32),
                          compiler_params=pltpu.CompilerParams(vmem_limit_bytes=VMEM_LIMIT))(c_all, dmod)


def sum_leading(x, name):
    n, R, _ = x.shape
    tr = _pick(R, (512, 256, 128, 64, 32, 16, 8))

    def body(x_ref, o_ref):
        acc = x_ref[0].astype(f32)
        for k in range(1, n):
            acc = acc + x_ref[k].astype(f32)
        o_ref[...] = acc

    return pl.pallas_call(body, name=name, grid=(R // tr,), in_specs=[pl.BlockSpec((n, tr, 128), lambda i: (0, i, 0))],
                          out_specs=pl.BlockSpec((tr, 128), lambda i: (i, 0)), out_shape=jax.ShapeDtypeStruct((R, 128), f32),
                          compiler_params=_cparams(1))(x)


def _adamw_body(w_ref, g_ref, m_ref, v_ref, d_ref, mo_ref, vo_ref):
    gv = g_ref[...]
    mn = ADAM_B1 * m_ref[...] + (1.0 - ADAM_B1) * gv
    vn = ADAM_B2 * v_ref[...] + (1.0 - ADAM_B2) * jnp.square(gv)
    m_hat = mn / (1.0 - ADAM_B1 ** ADAM_STEP)
    v_hat = vn / (1.0 - ADAM_B2 ** ADAM_STEP)
    d_ref[...] = -ADAM_LR * (m_hat / (jnp.sqrt(v_hat) + ADAM_EPS) + ADAM_WD * w_ref[...])
    mo_ref[...] = mn
    vo_ref[...] = vn


def adamw(w, g, m, v):
    R = w.shape[0]
    tr = _pick(R, (512, 256, 128, 64, 32, 16, 8))
    spec = pl.BlockSpec((tr, 128), lambda i: (i, 0))
    shp = jax.ShapeDtypeStruct((R, 128), f32)
    return pl.pallas_call(functools.partial(_adamw_body), name="adamw", grid=(R // tr,), in_specs=[spec] * 4,
                          out_specs=[spec] * 3, out_shape=[shp] * 3, compiler_params=_cparams(1))(w, g, m, v)


def _row_tile(a):
    return _pick(a, (256, 128, 64, 32, 16, 8)) if a % 8 == 0 else a


def adamw_nd(w, g, m, v):
    L, a, b = w.shape
    ta = _row_tile(a)
    spec = pl.BlockSpec((1, ta, b), lambda l, i: (l, i, 0))
    shp = jax.ShapeDtypeStruct((L, a, b), f32)
    return pl.pallas_call(functools.partial(_adamw_body), name="adamw_nd", grid=(L, a // ta), in_specs=[spec] * 4,
                          out_specs=[spec] * 3, out_shape=[shp] * 3, compiler_params=_cparams(2))(w, g, m, v)


def sum_slots(x):
    n, L, a, b = x.shape
    ta = _row_tile(a)

    def body(x_ref, o_ref):
        acc = x_ref[0].astype(f32)
        for k in range(1, n):
            acc = acc + x_ref[k].astype(f32)
        o_ref[...] = acc

    return pl.pallas_call(body, name="sum_slots", grid=(L, a // ta),
                          in_specs=[pl.BlockSpec((n, 1, ta, b), lambda l, i: (0, l, i, 0))],
                          out_specs=pl.BlockSpec((1, ta, b), lambda l, i: (l, i, 0)),
                          out_shape=jax.ShapeDtypeStruct((L, a, b), f32), compiler_params=_cparams(2))(x)


def _exchange_copies(ins, outs, sems, scatter):
    send_sems, recv_sems, local_sems = sems
    x, y, c = lax.axis_index("x"), lax.axis_index("y"), lax.axis_index("c")
    me = 4 * x + 2 * y + c
    locals_, sends, recvs = [], [], []
    for a in range(len(ins)):
        locals_.append(pltpu.make_async_copy(ins[a].at[me] if scatter[a] else ins[a], outs[a].at[me], local_sems.at[a]))
        for k in range(N_DEV - 1):
            px = 1 - x if (k + 1) & 4 else x
            py = 1 - y if (k + 1) & 2 else y
            pc = 1 - c if (k + 1) & 1 else c
            pid = 4 * px + 2 * py + pc
            src = ins[a].at[pid] if scatter[a] else ins[a]
            for slot, group in ((me, sends), (pid, recvs)):
                group.append(pltpu.make_async_remote_copy(src_ref=src, dst_ref=outs[a].at[slot], send_sem=send_sems.at[a, k],
                                                          recv_sem=recv_sems.at[a, k], device_id=(px, py, pc),
                                                          device_id_type=pl.DeviceIdType.MESH))
    return locals_, sends, recvs


def _exchange_start(ins, outs, sems, scatter):
    locals_, sends, _ = _exchange_copies(ins, outs, sems, scatter)
    for cp in locals_ + sends:
        cp.start()


def _exchange_wait(ins, outs, sems, scatter):
    locals_, sends, recvs = _exchange_copies(ins, outs, sems, scatter)
    for cp in recvs:
        cp.wait_recv()
    for cp in sends:
        cp.wait_send()
    for cp in locals_:
        cp.wait()


def _exchange_shapes(arrays, scatter):
    return [jax.ShapeDtypeStruct((N_DEV,) + tuple(a.shape[1:] if s else a.shape), a.dtype) for a, s in zip(arrays, scatter)]


def _flags(scatter, n):
    return [scatter] * n if isinstance(scatter, bool) else list(scatter)


def _exchange_sems(n):
    return [pltpu.SemaphoreType.DMA((n, N_DEV - 1)), pltpu.SemaphoreType.DMA((n, N_DEV - 1)), pltpu.SemaphoreType.DMA((n,))]


def _fuse_exchange(core, comm, n_in, n_out, n_scr, grid):
    if comm is None:
        return core, [], [], [], [], []
    arrays, scatter = comm
    n = len(arrays)
    scatter = _flags(scatter, n)

    def body(*refs):
        a, b, c = n_in + n, n_in + n + n_out, n_in + 2 * n + n_out
        cin, cout, sems = refs[n_in:a], refs[b:c], refs[c + n_scr:]
        ids = [pl.program_id(d) for d in range(len(grid))]
        first = functools.reduce(lambda p, q: p & q, [i == 0 for i in ids])
        last = functools.reduce(lambda p, q: p & q, [i == g - 1 for i, g in zip(ids, grid)])

        @pl.when(first)
        def _():
            _exchange_start(cin, cout, sems, scatter)

        core(*refs[:n_in], *refs[a:b], *refs[c:c + n_scr])

        @pl.when(last)
        def _():
            _exchange_wait(cin, cout, sems, scatter)

    hbm = pl.BlockSpec(memory_space=pltpu.HBM)
    return body, list(arrays), [hbm] * n, [hbm] * n, _exchange_shapes(arrays, scatter), _exchange_sems(n)


def exchange(arrays, scatter, name):
    n = len(arrays)
    scatter = _flags(scatter, n)

    def body(*refs):
        ins, outs, sems = refs[:n], refs[n:2 * n], refs[2 * n:]
        _exchange_start(ins, outs, sems, scatter)
        _exchange_wait(ins, outs, sems, scatter)

    hbm = pl.BlockSpec(memory_space=pltpu.HBM)
    return pl.pallas_call(body, name=name, in_specs=[hbm] * n, out_specs=[hbm] * n,
                          out_shape=_exchange_shapes(arrays, scatter), scratch_shapes=_exchange_sems(n))(*arrays)


BIG = (("w_in", "col"), ("conv_w", "col"), ("w_uq", "col"), ("w_ukv", "col"), ("w_out", "row"), ("w_up", "col"),
       ("conv_ff_w", "col"), ("w_down", "row"))
SMALL = ("b_ada", "norm_mix", "conv_b", "dt_bias", "a_log", "d_skip", "ssd_norm", "q_norm", "kv_norm", "attn_norm",
         "norm_mlp", "conv_ff_b", "final_norm")
CONVS = ("conv_w", "conv_ff_w")
PACK_ALIGN = 2048


def _padded(n):
    return -(-n // PACK_ALIGN) * PACK_ALIGN


def _flat_pad(a):
    f = a.reshape(-1)
    return jnp.pad(f, (0, _padded(f.shape[0]) - f.shape[0]))


PACK_ROWS = 512


def pack(arrs):
    f = jnp.concatenate([_flat_pad(a) for a in arrs])
    n = PACK_ROWS * 128
    return jnp.pad(f, (0, -(-f.shape[0] // n) * n - f.shape[0])).reshape(-1, 128)


def unpack(flat, shapes):
    f = flat.reshape(-1)
    out, off = [], 0
    for s in shapes:
        n = int(np.prod(s))
        out.append(f[off:off + n].reshape(s))
        off += _padded(n)
    return out


def shards_to_full(g, kind):
    _, a, b = g.shape
    if kind == "col":
        return g.transpose(1, 0, 2).reshape(a, N_DEV * b)
    return g.reshape(N_DEV * a, b)


def full_to_shards(full, kind):
    if kind == "col":
        a, nb = full.shape
        return full.reshape(a, N_DEV, nb // N_DEV).transpose(1, 0, 2)
    na, b = full.shape
    return full.reshape(N_DEV, na // N_DEV, b)


def w_in_layout(w):
    z = lambda n: jnp.zeros(w.shape[:-1] + (n,), w.dtype)
    return jnp.concatenate([w[..., :2560], w[..., 2576:2960], z(128), w[..., 2960:3216], w[..., 3216:3280],
                            w[..., 2560:2576], z(48)], axis=-1)


def w_in_unlayout(g):
    return jnp.concatenate([g[..., :2560], g[..., 3392:3408], g[..., 2560:2944], g[..., 3072:3328], g[..., 3328:3392]], axis=-1)


def w_uq_layout(w):
    return jnp.pad(w.reshape(Q_RANK, MLA_H, QK), ((0, 0), (0, 0), (0, 256 - QK))).reshape(Q_RANK, MLA_H * 256)


def w_uq_unlayout(g):
    return g.reshape(Q_RANK, MLA_H, 256)[:, :, :QK].reshape(Q_RANK, MLA_H * QK)


def w_ukv_layout(w):
    return w.reshape(KV_RANK, MLA_H, 2, 128).transpose(0, 2, 1, 3).reshape(KV_RANK, 2 * MLA_H * 128)


def w_ukv_unlayout(g):
    return g.reshape(KV_RANK, 2, MLA_H, 128).transpose(0, 2, 1, 3).reshape(KV_RANK, 2 * MLA_H * 128)


LAYOUTS = {"w_in": (w_in_layout, w_in_unlayout), "w_uq": (w_uq_layout, w_uq_unlayout), "w_ukv": (w_ukv_layout, w_ukv_unlayout)}
FIRST, REST = BIG[:4], BIG[4:]


def layer_weights(gathered, entries):
    full = {n: shards_to_full(g, kind) for (n, kind), g in zip(entries, gathered)}
    return {n: LAYOUTS[n][0](w) if n in LAYOUTS else w for n, w in full.items()}


def layer_grad_slices(g, entries):
    return [full_to_shards(LAYOUTS[n][1](g[n]) if n in LAYOUTS else g[n], kind).astype(bf16) for n, kind in entries]


def _head_row(v):
    return jnp.zeros((1, 128), f32).at[0, DT_LANE:DT_LANE + SSD_HEADS].set(v)


def layer_fwd(x3, mod, W, P, l, cosf, sinf, comm=None, late=None, comm_up=None):
    B, S, _ = x3.shape
    T = B * S
    sv = {}
    h = normmod_fwd(x3, mod, P["norm_mix"][l][None], 0, 1)
    p = mm(h.reshape(T, D), W["w_in"], "nn", "mm_in")
    p3 = p.reshape(B, S, IN_COLS)
    bias_row, alog_row = _head_row(P["dt_bias"][l]), _head_row(P["a_log"][l])
    dcol = jnp.repeat(P["d_skip"][l], SSD_HD)[:, None]
    xc3, xpre = conv_ssd_fwd(p3, W["conv_w"], P["conv_b"][l][None])
    yc3, states = ssd_fwd(xc3, p3, bias_row, alog_row, dcol)
    y_ssd = ssd_out_fwd(yc3, p3, P["ssd_norm"][l][None])
    cqn = rms_fwd(p3, 512, OFF_CQ // 512, Q_RANK, P["q_norm"][l][None], "rms_q_fwd")
    ckvn = rms_fwd(p3, KV_RANK, OFF_CKV // KV_RANK, KV_RANK, P["kv_norm"][l][None], "rms_kv_fwd")
    q3 = mm(cqn.reshape(T, Q_RANK), W["w_uq"], "nn", "mm_uq", out_dtype=bf16,
            rope=(cosf.reshape(T, 128), sinf.reshape(T, 128))).reshape(B, S, -1)
    kvraw = mm(ckvn.reshape(T, KV_RANK), W["w_ukv"], "nn", "mm_ukv")
    k3, v3 = kprep_fwd(kvraw.reshape(B, S, -1), p3, cosf, sinf)
    o3, lse, comm_out = attn_fwd(q3, k3, v3, comm)
    if late is not None:
        W = dict(W, **late(comm_out))
    y_att = rms_fwd(o3, D, 0, D, P["attn_norm"][l][None], "rms_o_fwd")
    cat = (y_ssd.reshape(T, D), y_att.reshape(T, D))
    x1, y1 = mm(cat, W["w_out"], "nn", "mm_out", resid=x3.reshape(T, D), gate=mod[:, 2:3, :], seq=S)
    x13 = x1.reshape(B, S, D)
    h2 = normmod_fwd(x13, mod, P["norm_mlp"][l][None], 3, 4)
    u, up_out = mm(h2.reshape(T, D), W["w_up"], "nn", "mm_up", out_dtype=bf16, comm=comm_up), []
    if comm_up is not None:
        u, up_out = u
    u3 = u.reshape(B, S, 2 * D_FF)
    a, ffg, ffv = glu_fwd(u3, W["conv_ff_w"], P["conv_ff_b"][l][None])
    x2, y2 = mm(a.reshape(T, D_FF), W["w_down"], "nn", "mm_down", resid=x1, gate=mod[:, 5:6, :], seq=S)
    sv.update(x=x3, h=h, p3=p3, xc3=xc3, xpre=xpre, yc3=yc3, states=states, cqn=cqn, ckvn=ckvn, q3=q3, k3=k3, v3=v3, o3=o3, lse=lse,
              cat=cat, y1=y1, x1=x13, h2=h2, u3=u3, ffg=ffg, ffv=ffv, a=a, y2=y2, bias_row=bias_row, alog_row=alog_row, dcol=dcol)
    return x2.reshape(B, S, D), sv, comm_out, W, up_out


def layer_bwd(dx3, sv, mod, W, P, l, cosf, sinf, comm=None, send_rest=False):
    B, S, _ = dx3.shape
    T = B * S
    g = {}
    dy2, dg2 = gate_bwd(dx3, sv["y2"].reshape(B, S, D), mod, 5)
    dy2 = dy2.reshape(T, D)
    da = mm(dy2, W["w_down"], "nt", "mm_down_dx", out_dtype=bf16)
    g["w_down"] = mm(sv["a"].reshape(T, D_FF), dy2, "tn", "mm_down_dw")
    dug, duv, dwg, dwv, dbg, dbv = glu_bwd(sv["u3"], sv["ffg"], sv["ffv"], da.reshape(B, S, D_FF), W["conv_ff_w"])
    g["conv_ff_w"] = jnp.concatenate([dwg, dwv], axis=1)
    g["conv_ff_b"] = jnp.concatenate([dbg, dbv], axis=1)[0]
    du = (dug.reshape(T, D_FF), duv.reshape(T, D_FF))
    dh2 = mm(du, W["w_up"], "nt", "mm_up_dx", out_dtype=bf16)
    g["w_up"] = jnp.concatenate([mm(sv["h2"].reshape(T, D), d, "tn", "mm_up_dw") for d in du], axis=1)
    dx1, dsh2, dsc2, dnm = normmod_bwd(sv["x1"], dh2.reshape(B, S, D), dx3, mod, P["norm_mlp"][l][None], 4)
    g["norm_mlp"] = dnm[0]
    dy1, dg1 = gate_bwd(dx1, sv["y1"].reshape(B, S, D), mod, 2)
    dy1 = dy1.reshape(T, D)
    dcat = mm(dy1, W["w_out"], "nt", "mm_out_dx", out_dtype=bf16)
    g["w_out"] = jnp.concatenate([mm(part, dy1, "tn", "mm_out_dw") for part in sv["cat"]], axis=0)
    dcat3 = dcat.reshape(B, S, 2 * D)
    dyc3, dp, dsn = ssd_out_bwd(sv["yc3"], sv["p3"], dcat3, P["ssd_norm"][l][None])
    g["ssd_norm"] = dsn[0]
    do3, dan, delta = rms_o_bwd(sv["o3"], dcat3, P["attn_norm"][l][None])
    g["attn_norm"] = dan[0]
    dqraw, dkn3, dv3, dkr3, comm_out = attn_bwd(sv["q3"], sv["k3"], sv["v3"], do3, sv["lse"], delta.reshape(B, MLA_H, 1, S),
                                                cosf, sinf, comm)
    dqraw = dqraw.reshape(T, -1)
    dcqn = mm(dqraw, W["w_uq"], "nt", "mm_uq_dx")
    g["w_uq"] = mm(sv["cqn"].reshape(T, Q_RANK), dqraw, "tn", "mm_uq_dw")
    dp, dqn = rms_bwd_into(sv["p3"], 512, OFF_CQ // 512, Q_RANK, dcqn.reshape(B, S, Q_RANK), P["q_norm"][l][None], dp, "rms_q_bwd")
    g["q_norm"] = dqn[0]
    bias_col, alog_col = sv["bias_row"].reshape(128, 1), sv["alog_row"].reshape(128, 1)
    comm_rest = (layer_grad_slices(g, REST), True) if send_rest else None
    dxc3, ddt3, dalog, dd, dbias, rest_out = ssd_bwd(sv["xc3"], sv["p3"], dyc3, sv["states"], sv["bias_row"], sv["alog_row"],
                                                     bias_col, alog_col, sv["dcol"], comm_rest)
    heads = slice(DT_LANE, DT_LANE + SSD_HEADS)
    g["a_log"], g["d_skip"], g["dt_bias"] = dalog[heads, 0], dd[heads, 0], dbias[heads, 0]
    dp, dcw, dcb = conv_ssd_bwd(sv["p3"], sv["xpre"], dxc3, W["conv_w"], dp)
    g["conv_w"], g["conv_b"] = dcw, dcb[0]
    dp = kprep_bwd(dkr3, ddt3, cosf, sinf, dp)
    dkv = (dkn3.reshape(T, -1), dv3.reshape(T, -1))
    dckvn = mm(dkv, W["w_ukv"], "nt", "mm_ukv_dx")
    g["w_ukv"] = jnp.concatenate([mm(sv["ckvn"].reshape(T, KV_RANK), d, "tn", "mm_ukv_dw") for d in dkv], axis=1)
    dp, dkn = rms_bwd_into(sv["p3"], KV_RANK, OFF_CKV // KV_RANK, KV_RANK, dckvn.reshape(B, S, KV_RANK), P["kv_norm"][l][None], dp,
                           "rms_kv_bwd")
    g["kv_norm"] = dkn[0]
    dp = dp.reshape(T, IN_COLS)
    dh = mm(dp, W["w_in"], "nt", "mm_in_dx", out_dtype=bf16)
    g["w_in"] = mm(sv["h"].reshape(T, D), dp, "tn", "mm_in_dw")
    dx0, dsh1, dsc1, dnx = normmod_bwd(sv["x"], dh.reshape(B, S, D), dx1, mod, P["norm_mix"][l][None], 1)
    g["norm_mix"] = dnx[0]
    dmod = jnp.concatenate([dsh1, dsc1, dg1, dsh2, dsc2, dg2], axis=1)
    return dx0, dmod, g, comm_out, rest_out


def kernel(x, c, positions, w_ada, b_ada, norm_mix, w_in, conv_w, conv_b, dt_bias, a_log, d_skip, ssd_norm, q_norm, w_uq, kv_norm, w_ukv, attn_norm, w_out, norm_mlp, w_up, conv_ff_w, conv_ff_b, w_down, final_norm, loss_target, m_w_ada, m_b_ada, m_norm_mix, m_w_in, m_conv_w, m_conv_b, m_dt_bias, m_a_log, m_d_skip, m_ssd_norm, m_q_norm, m_w_uq, m_kv_norm, m_w_ukv, m_attn_norm, m_w_out, m_norm_mlp, m_w_up, m_conv_ff_w, m_conv_ff_b, m_w_down, m_final_norm, v_w_ada, v_b_ada, v_norm_mix, v_w_in, v_conv_w, v_conv_b, v_dt_bias, v_a_log, v_d_skip, v_ssd_norm, v_q_norm, v_w_uq, v_kv_norm, v_w_ukv, v_attn_norm, v_w_out, v_norm_mlp, v_w_up, v_conv_ff_w, v_conv_ff_b, v_w_down, v_final_norm):
    given = dict(locals())
    B, S, _ = x.shape
    me = 4 * lax.axis_index("x") + 2 * lax.axis_index("y") + lax.axis_index("c")
    P = {n: given[n] for n in SMALL}

    def shards(l, entries):
        return [given[n][l] if n in CONVS else given[n][l].astype(bf16) for n, _ in entries]

    *gathered, c_all = exchange(shards(0, FIRST) + [c], False, "gather_weights")
    W = [layer_weights(gathered, FIRST), None]

    n_ada = w_ada.shape[2]
    c_all = c_all.reshape(N_DEV * B, D)
    b_sh = lax.dynamic_slice_in_dim(b_ada, me * n_ada, n_ada, axis=1)
    mod_sh = jnp.stack([ada_fwd(c_all, w_ada[l], b_sh[l][None]) for l in range(DEPTH)])
    (mod_g,) = exchange([mod_sh], False, "gather_mod")
    mod_mine = lax.dynamic_slice_in_dim(mod_g, me * B, B, axis=2)
    mods = mod_mine.transpose(1, 2, 0, 3).reshape(DEPTH, B, 6, D)

    inv_freq = jnp.asarray(1.0 / (ROPE_BASE ** (np.arange(0, ROPE, 2, dtype=np.float32) / ROPE)))
    ang = positions.astype(f32)[..., None] * inv_freq
    zeros = jnp.zeros((B, S, 128 - ROPE), f32)
    cosf = jnp.concatenate([jnp.cos(ang), jnp.cos(ang), zeros], axis=-1)
    sinf = jnp.concatenate([jnp.sin(ang), jnp.sin(ang), zeros], axis=-1)

    saved = [None] * DEPTH
    late = lambda got: layer_weights(got, REST)
    xl, saved[0], _, W[0], gathered = layer_fwd(x, mods[0], W[0], P, 0, cosf, sinf, comm=(shards(0, REST), False), late=late,
                                                comm_up=(shards(1, FIRST), False))
    xl, saved[1], _, W[1], _ = layer_fwd(xl, mods[1], layer_weights(gathered, FIRST), P, 1, cosf, sinf,
                                         comm=(shards(1, REST), False), late=late)
    dxl, d_final, loss_part = final_loss(xl, final_norm[None], loss_target)
    grads, dmods, recv = [None] * DEPTH, [None] * DEPTH, [None] * DEPTH
    dxl, dmods[1], grads[1], _, _ = layer_bwd(dxl, saved[1], mods[1], W[1], P, 1, cosf, sinf)
    grad_x, dmods[0], grads[0], recv[1], recv_rest = layer_bwd(dxl, saved[0], mods[0], W[0], P, 0, cosf, sinf,
                                                               comm=(layer_grad_slices(grads[1], BIG), True), send_rest=True)

    stack = lambda n: jnp.stack([grads[l][n] for l in range(DEPTH)])
    small_names = [n for n in SMALL if n not in ("b_ada", "final_norm")]
    partial = pack([stack(n) for n in small_names] + [d_final[0], loss_part[0]])
    dmod_all = jnp.stack(dmods)
    *recv_first, part_g, dmod_g = exchange(layer_grad_slices(grads[0], FIRST) + [partial, dmod_all],
                                           [True] * len(FIRST) + [False, False], "exchange_tail")
    recv[0] = recv_first + recv_rest
    big_g = [jnp.concatenate([sum_slots(recv[l][i][:, None]) for l in range(DEPTH)]) for i in range(len(BIG))]
    small_sum = sum_leading(part_g, "sum_partials")
    small_g = unpack(small_sum, [given[n].shape for n in small_names] + [(D,), (128,)])
    gsmall = dict(zip(small_names + ["final_norm"], small_g[:-1]))
    loss = small_g[-1][0]
    dmod_rows = dmod_g.transpose(0, 2, 1, 3, 4).reshape(N_DEV * B, DEPTH * 6 * D)
    gsmall["b_ada"] = sum_leading(dmod_rows.reshape(N_DEV * B, -1, 128), "sum_b_ada").reshape(DEPTH, 6 * D)
    dmod_cols = dmod_rows.reshape(N_DEV * B, DEPTH, N_DEV, n_ada)
    dmod_sh = lax.dynamic_slice_in_dim(dmod_cols, me, 1, axis=2)[:, :, 0, :]
    g_w_ada = jnp.stack([ada_bwd(c_all, dmod_sh[:, l, :]) for l in range(DEPTH)])

    res = {"grad": {}, "delta": {}, "new_m": {}, "new_v": {}}
    for n, gv in zip([n for n, _ in BIG] + ["w_ada"], big_g + [g_w_ada]):
        res["grad"][n] = gv
        res["delta"][n], res["new_m"][n], res["new_v"][n] = adamw_nd(given[n], gv, given["m_" + n], given["v_" + n])
    shapes = [given[n].shape for n in SMALL]
    flat = adamw(pack([given[n] for n in SMALL]), pack([gsmall[n] for n in SMALL]), pack([given["m_" + n] for n in SMALL]),
                 pack([given["v_" + n] for n in SMALL]))
    for n in SMALL:
        res["grad"][n] = gsmall[n]
    for key, arr in zip(("delta", "new_m", "new_v"), flat):
        res[key].update(zip(SMALL, unpack(arr, shapes)))
    order = ["w_ada", "b_ada", "norm_mix", "w_in", "conv_w", "conv_b", "dt_bias", "a_log", "d_skip", "ssd_norm", "q_norm", "w_uq",
             "kv_norm", "w_ukv", "attn_norm", "w_out", "norm_mlp", "w_up", "conv_ff_w", "conv_ff_b", "w_down", "final_norm"]
    return (loss, grad_x, *[res[k][n] for k in ("grad", "delta", "new_m", "new_v") for n in order])
```

```python
import functools

import numpy as np
import jax
import jax.numpy as jnp
from jax import lax
from jax.experimental import pallas as pl
from jax.experimental.pallas import tpu as pltpu

f32, bf16 = jnp.float32, jnp.bfloat16
HIGHEST = lax.Precision.HIGHEST

D = 1024
D_SSD = 1024
SSD_HEADS = 16
SSD_HD = 64
SSD_N = 128
CHUNK = 128
D_XBC = 1536
CONV_K = 4
MLA_H = 8
NOPE = 128
ROPE = 64
VD = 128
QK = NOPE + ROPE
Q_RANK = 384
KV_RANK = 256
D_FF = 2816
FF_K = 3
EPS = 1e-6
ROPE_BASE = 10000.0
DEPTH = 2
ADAM_LR, ADAM_B1, ADAM_B2, ADAM_EPS, ADAM_WD, ADAM_STEP = 0.001, 0.9, 0.999, 1e-08, 0.01, 10

N_DEV = 8
IN_COLS = 3456
OFF_XBC, OFF_CQ, OFF_CKV, OFF_KRDT = 1024, 2560, 3072, 3328
DT_LANE = 64
VMEM_LIMIT = 48 * 1024 * 1024
MM_K_WHOLE = 4096


def _cparams(n_grid):
    return pltpu.CompilerParams(dimension_semantics=("arbitrary",) * n_grid, vmem_limit_bytes=VMEM_LIMIT)


def _pick(n, cands):
    for c in cands:
        if n % c == 0:
            return c
    return n


def _silu(x):
    return x * jax.nn.sigmoid(x)


def _dsilu(x):
    s = jax.nn.sigmoid(x)
    return s * (1.0 + x * (1.0 - s))


def _rowsum(x):
    return jnp.sum(x, axis=0, keepdims=True)


def mm(a, b, mode, name, out_dtype=f32, resid=None, gate=None, seq=None, comm=None, rope=None):
    parts = list(a) if isinstance(a, (tuple, list)) else [a]
    np_ = len(parts)
    if mode == "nn":
        (M, Kp), N = parts[0].shape, b.shape[1]
    elif mode == "nt":
        (M, Kp), N = parts[0].shape, b.shape[0]
    else:
        (Kp, M), N = parts[0].shape, b.shape[1]
    K = Kp * np_
    gated = resid is not None
    whole = np_ > 1 or K <= MM_K_WHOLE
    tm = _pick(seq if gated else M, (1024, 1408, 512, 384, 256, 128) if K <= MM_K_WHOLE else (512, 256, 128))
    tn = _pick(N, (512, 1408, 384, 256, 128))
    tk = K if whole else _pick(K, (2816, 2048, 1024, 512))
    if mode == "tn":
        tm = _pick(M, (1024, 1408, 512, 384, 256, 128))
    nk = K // tk
    dims = {"nn": ((1,), (0,)), "nt": ((1,), (1,)), "tn": ((0,), (0,))}[mode]

    def body(*refs):
        a_refs, b_ref, rest = refs[:np_], refs[np_], refs[np_ + 1:]
        if rope is not None:
            cos_ref, sin_ref, rest = rest[0], rest[1], rest[2:]
        if gated:
            r_ref, g_ref, o_ref, y_ref, acc = rest
        else:
            o_ref, acc = rest

        def finish(res):
            if gated:
                y_ref[...] = res
                o_ref[...] = r_ref[...] + g_ref[0] * res
            elif rope is not None:
                for h in range(tn // 256):
                    lo = h * 256
                    o_ref[:, lo:lo + NOPE] = (res[:, lo:lo + NOPE] * Q_FOLD).astype(out_dtype)
                    o_ref[:, lo + NOPE:lo + 256] = (_rope(res[:, lo + NOPE:lo + 256], cos_ref[...], sin_ref[...]) * Q_FOLD).astype(out_dtype)
            else:
                o_ref[...] = res.astype(out_dtype)

        prod = None
        for p, a_ref in enumerate(a_refs):
            if np_ == 1:
                bv = b_ref[...]
            else:
                bv = b_ref[:, p * Kp:(p + 1) * Kp] if mode == "nt" else b_ref[p * Kp:(p + 1) * Kp, :]
            term = lax.dot_general(a_ref[...].astype(bf16), bv.astype(bf16), (dims, ((), ())), preferred_element_type=f32)
            prod = term if prod is None else prod + term
        if nk == 1:
            finish(prod)
        else:
            k = pl.program_id(2)

            @pl.when(k == 0)
            def _():
                acc[...] = prod

            @pl.when(k > 0)
            def _():
                acc[...] += prod

            @pl.when(k == nk - 1)
            def _():
                finish(acc[...])

    if np_ > 1:
        a_spec = pl.BlockSpec((tm, Kp), lambda i, j, k: (i, 0))
    elif mode == "tn":
        a_spec = pl.BlockSpec((tk, tm), lambda i, j, k: (k, i))
    else:
        a_spec = pl.BlockSpec((tm, tk), lambda i, j, k: (i, k))
    b_spec = pl.BlockSpec((tn, tk), lambda i, j, k: (j, k)) if mode == "nt" else pl.BlockSpec((tk, tn), lambda i, j, k: (k, j))
    o_spec = pl.BlockSpec((tm, tn), lambda i, j, k: (i, j))
    in_specs, args = [a_spec] * np_ + [b_spec], parts + [b]
    if rope is not None:
        in_specs += [pl.BlockSpec((tm, 128), lambda i, j, k: (i, 0))] * 2
        args += list(rope)
    out_specs, out_shape = [o_spec], [jax.ShapeDtypeStruct((M, N), out_dtype)]
    if gated:
        per = seq // tm
        in_specs += [o_spec, pl.BlockSpec((1, 1, tn), lambda i, j, k: (i // per, 0, j))]
        args += [resid, gate]
        out_specs = [o_spec, o_spec]
        out_shape = [jax.ShapeDtypeStruct((M, N), f32), jax.ShapeDtypeStruct((M, N), f32)]
    grid = (M // tm, N // tn, nk)
    body, c_args, c_in, c_out, c_shapes, c_sems = _fuse_exchange(body, comm, len(args), len(out_specs), 1, grid)
    res = pl.pallas_call(body, name=name + "_x" if comm else name, grid=grid, in_specs=in_specs + c_in, out_specs=out_specs + c_out,
                         out_shape=out_shape + c_shapes, scratch_shapes=[pltpu.VMEM((tm, tn), f32)] + c_sems,
                         compiler_params=_cparams(3))(*args, *c_args)
    own = res[:len(out_specs)]
    own = own[0] if len(own) == 1 else tuple(own)
    return (own, list(res[len(out_specs):])) if comm else own


def _tok(ts, width, cb=0):
    return pl.BlockSpec((1, ts, width), lambda b, s: (b, s, cb))


def _perb(rows, width):
    return pl.BlockSpec((1, rows, width), lambda b, s: (b, 0, 0))


def _const(rows, width):
    return pl.BlockSpec((rows, width), lambda b, s: (0, 0))


def _row_call(body, name, B, S, ts, in_specs, out_specs, out_shape, scratch=(), aliases=None):
    return pl.pallas_call(body, name=name, grid=(B, S // ts), in_specs=in_specs, out_specs=out_specs,
                          out_shape=out_shape, scratch_shapes=list(scratch), input_output_aliases=aliases or {},
                          compiler_params=_cparams(2))


def _first():
    return (pl.program_id(0) == 0) & (pl.program_id(1) == 0)


def normmod_fwd(x3, mod, g, i_sh, i_sc):
    B, S, C = x3.shape
    ts = _pick(S, (512, 256, 128))

    def body(x_ref, mod_ref, g_ref, h_ref):
        x = x_ref[0]
        r = lax.rsqrt(jnp.mean(x * x, axis=-1, keepdims=True) + EPS)
        n = x * r * g_ref[...]
        h_ref[0] = (n * (1.0 + mod_ref[0, i_sc:i_sc + 1, :]) + mod_ref[0, i_sh:i_sh + 1, :]).astype(bf16)

    return _row_call(body, "normmod_fwd", B, S, ts, [_tok(ts, C), _perb(6, C), _const(1, C)], _tok(ts, C),
                     jax.ShapeDtypeStruct((B, S, C), bf16))(x3, mod, g)


def normmod_bwd(x3, dh3, resid3, mod, g, i_sc, y3=None, i_g=None):
    B, S, C = x3.shape
    ts = _pick(S, (512, 256, 128))
    gated = y3 is not None

    def body(x_ref, dh_ref, r_ref, mod_ref, g_ref, *rest):
        if gated:
            y_ref, dx_ref, dsh_ref, dsc_ref, dg_ref, dy_ref, dgate_ref = rest
        else:
            dx_ref, dsh_ref, dsc_ref, dg_ref = rest

        @pl.when(pl.program_id(1) == 0)
        def _():
            dsh_ref[...] = jnp.zeros_like(dsh_ref)
            dsc_ref[...] = jnp.zeros_like(dsc_ref)
            if gated:
                dgate_ref[...] = jnp.zeros_like(dgate_ref)

        @pl.when(_first())
        def _():
            dg_ref[...] = jnp.zeros_like(dg_ref)

        x, dh, gv = x_ref[0], dh_ref[0].astype(f32), g_ref[...]
        r = lax.rsqrt(jnp.mean(x * x, axis=-1, keepdims=True) + EPS)
        xh = x * r
        dn = dh * (1.0 + mod_ref[0, i_sc:i_sc + 1, :])
        dsh_ref[0] += _rowsum(dh)
        dsc_ref[0] += _rowsum(dh * xh * gv)
        dg_ref[...] += _rowsum(dn * xh)
        dxh = dn * gv
        dx = r * (dxh - xh * jnp.mean(dxh * xh, axis=-1, keepdims=True)) + r_ref[0]
        dx_ref[0] = dx
        if gated:
            dy_ref[0] = (dx * mod_ref[0, i_g:i_g + 1, :]).astype(bf16)
            dgate_ref[0] += _rowsum(dx * y_ref[0])

    in_specs = [_tok(ts, C), _tok(ts, C), _tok(ts, C), _perb(6, C), _const(1, C)]
    out_specs = [_tok(ts, C), _perb(1, C), _perb(1, C), _const(1, C)]
    out_shape = [jax.ShapeDtypeStruct((B, S, C), f32), jax.ShapeDtypeStruct((B, 1, C), f32),
                 jax.ShapeDtypeStruct((B, 1, C), f32), jax.ShapeDtypeStruct((1, C), f32)]
    args = [x3, dh3, resid3, mod, g]
    if gated:
        in_specs.append(_tok(ts, C))
        args.append(y3)
        out_specs += [_tok(ts, C), _perb(1, C)]
        out_shape += [jax.ShapeDtypeStruct((B, S, C), bf16), jax.ShapeDtypeStruct((B, 1, C), f32)]
    return _row_call(body, "normmod_gate_bwd" if gated else "normmod_bwd", B, S, ts, in_specs, out_specs, out_shape)(*args)


def gate_bwd(dx3, y3, mod, i_g):
    B, S, C = dx3.shape
    ts = _pick(S, (512, 256, 128))

    def body(dx_ref, y_ref, mod_ref, dy_ref, dgate_ref):
        @pl.when(pl.program_id(1) == 0)
        def _():
            dgate_ref[...] = jnp.zeros_like(dgate_ref)

        dx = dx_ref[0]
        dy_ref[0] = (dx * mod_ref[0, i_g:i_g + 1, :]).astype(bf16)
        dgate_ref[0] += _rowsum(dx * y_ref[0])

    return _row_call(body, "gate_bwd", B, S, ts, [_tok(ts, C), _tok(ts, C), _perb(6, C)], [_tok(ts, C), _perb(1, C)],
                     [jax.ShapeDtypeStruct((B, S, C), bf16), jax.ShapeDtypeStruct((B, 1, C), f32)])(dx3, y3, mod)


def rms_fwd(src3, width, cb, n, g, name):
    B, S, _ = src3.shape
    ts = _pick(S, (512, 256, 128))

    def body(x_ref, g_ref, o_ref):
        x = x_ref[0][:, :n]
        r = lax.rsqrt(jnp.mean(x * x, axis=-1, keepdims=True) + EPS)
        o_ref[0] = (x * r * g_ref[...]).astype(bf16)

    return _row_call(body, name, B, S, ts, [_tok(ts, width, cb), _const(1, n)], _tok(ts, n),
                     jax.ShapeDtypeStruct((B, S, n), bf16))(src3, g)


def rms_bwd_into(src3, width, cb, n, dout3, g, dp, name):
    B, S, _ = src3.shape
    ts = _pick(S, (512, 256, 128))

    def body(x_ref, do_ref, g_ref, dp_in, dp_ref, dg_ref):
        @pl.when(_first())
        def _():
            dg_ref[...] = jnp.zeros_like(dg_ref)

        x = x_ref[0][:, :n]
        do = do_ref[0]
        r = lax.rsqrt(jnp.mean(x * x, axis=-1, keepdims=True) + EPS)
        xh = x * r
        dg_ref[...] += _rowsum(do * xh)
        dxh = do * g_ref[...]
        dp_ref[0, :, :n] = (r * (dxh - xh * jnp.mean(dxh * xh, axis=-1, keepdims=True))).astype(bf16)
        if width > n:
            dp_ref[0, :, n:] = jnp.zeros((ts, width - n), bf16)

    return _row_call(body, name, B, S, ts, [_tok(ts, width, cb), _tok(ts, n), _const(1, n), pl.BlockSpec(memory_space=pl.ANY)],
                     [_tok(ts, width, cb), _const(1, n)], [jax.ShapeDtypeStruct(dp.shape, bf16), jax.ShapeDtypeStruct((1, n), f32)],
                     aliases={3: 0})(src3, dout3, g, dp)


def final_loss(x3, g, tgt3):
    B, S, C = x3.shape
    ts = _pick(S, (512, 256, 128))

    def body(x_ref, g_ref, t_ref, dx_ref, dg_ref, loss_ref):
        @pl.when(_first())
        def _():
            dg_ref[...] = jnp.zeros_like(dg_ref)
            loss_ref[...] = jnp.zeros_like(loss_ref)

        x, gv = x_ref[0], g_ref[...]
        r = lax.rsqrt(jnp.mean(x * x, axis=-1, keepdims=True) + EPS)
        xh = x * r
        e = xh * gv - t_ref[0]
        loss_ref[...] += 0.5 * jnp.sum(e * e) / C
        dout = e / C
        dg_ref[...] += _rowsum(dout * xh)
        dxh = dout * gv
        dx_ref[0] = r * (dxh - xh * jnp.mean(dxh * xh, axis=-1, keepdims=True))

    return _row_call(body, "final_loss", B, S, ts, [_tok(ts, C), _const(1, C), _tok(ts, C)],
                     [_tok(ts, C), _const(1, C), _const(1, 128)],
                     [jax.ShapeDtypeStruct((B, S, C), f32), jax.ShapeDtypeStruct((1, C), f32),
                      jax.ShapeDtypeStruct((1, 128), f32)])(x3, g, tgt3)


def ssd_out_fwd(yc3, p3, w):
    B, S, C = yc3.shape
    ts = _pick(S, (512, 256, 128))
    half = C // 2

    def body(y_ref, z_ref, w_ref, o_ref):
        y = y_ref[0] * _silu(z_ref[0])
        for lo in (0, half):
            yg = y[:, lo:lo + half]
            r = lax.rsqrt(jnp.mean(yg * yg, axis=-1, keepdims=True) + EPS)
            o_ref[0, :, lo:lo + half] = (yg * r * w_ref[:, lo:lo + half]).astype(bf16)

    return _row_call(body, "ssd_out_fwd", B, S, ts, [_tok(ts, C), _tok(ts, C, 0), _const(1, C)], _tok(ts, C),
                     jax.ShapeDtypeStruct((B, S, C), bf16))(yc3, p3, w)


def ssd_out_bwd(yc3, p3, dcat3, w):
    B, S, C = yc3.shape
    ts = _pick(S, (512, 256, 128))
    half = C // 2

    def body(y_ref, z_ref, do_ref, w_ref, dyc_ref, dz_ref, dw_ref):
        @pl.when(_first())
        def _():
            dw_ref[...] = jnp.zeros_like(dw_ref)

        yc, z, do = y_ref[0], z_ref[0], do_ref[0].astype(f32)
        sz = _silu(z)
        y = yc * sz
        for lo in (0, half):
            sl = slice(lo, lo + half)
            yg, dog, wg = y[:, sl], do[:, sl], w_ref[:, sl]
            r = lax.rsqrt(jnp.mean(yg * yg, axis=-1, keepdims=True) + EPS)
            yh = yg * r
            dw_ref[:, sl] += _rowsum(dog * yh)
            dyh = dog * wg
            dy = r * (dyh - yh * jnp.mean(dyh * yh, axis=-1, keepdims=True))
            dyc_ref[0, :, sl] = dy * sz[:, sl]
            dz_ref[0, :, sl] = (dy * yc[:, sl] * _dsilu(z[:, sl])).astype(bf16)

    return _row_call(body, "ssd_out_bwd", B, S, ts, [_tok(ts, C), _tok(ts, C, 0), _tok(ts, C, 0), _const(1, C)],
                     [_tok(ts, C), _tok(ts, C, 0), _const(1, C)],
                     [jax.ShapeDtypeStruct((B, S, C), f32), jax.ShapeDtypeStruct((B, S, IN_COLS), bf16),
                      jax.ShapeDtypeStruct((1, C), f32)])(yc3, p3, dcat3, w)


def _rot(t):
    lane = lax.broadcasted_iota(jnp.int32, t.shape, 1)
    return jnp.where(lane < ROPE // 2, -pltpu.roll(t, 128 - ROPE // 2, 1), pltpu.roll(t, ROPE // 2, 1))


def _rope(t, cosf, sinf):
    return t * cosf + _rot(t) * sinf


def _rope_t(d, cosf, sinf):
    return d * cosf - _rot(d * sinf)


def k_proj(ckvn, w, p3, cosf, sinf):
    B, S, _ = ckvn.shape
    ts = _pick(S, (1024, 512, 256, 128))

    def body(x_ref, w_ref, kr_ref, c_ref, s_ref, o_ref):
        lane = lax.broadcasted_iota(jnp.int32, (1, 128), 1)
        kr = _rope(jnp.where(lane < ROPE, kr_ref[0], 0.0), c_ref[0], s_ref[0]).astype(bf16)
        res = jnp.dot(x_ref[0], w_ref[...], preferred_element_type=f32)
        for h in range(2):
            o_ref[0, :, h * 256:h * 256 + NOPE] = res[:, h * NOPE:(h + 1) * NOPE].astype(bf16)
            o_ref[0, :, h * 256 + NOPE:(h + 1) * 256] = kr

    return pl.pallas_call(body, name="k_proj", grid=(B, S // ts, MLA_H // 2),
                          in_specs=[pl.BlockSpec((1, ts, KV_RANK), lambda b, s, j: (b, s, 0)),
                                    pl.BlockSpec((KV_RANK, 2 * NOPE), lambda b, s, j: (0, j)),
                                    pl.BlockSpec((1, ts, 128), lambda b, s, j: (b, s, OFF_KRDT // 128)),
                                    pl.BlockSpec((1, ts, 128), lambda b, s, j: (b, s, 0)),
                                    pl.BlockSpec((1, ts, 128), lambda b, s, j: (b, s, 0))],
                          out_specs=pl.BlockSpec((1, ts, 512), lambda b, s, j: (b, s, j)),
                          out_shape=jax.ShapeDtypeStruct((B, S, MLA_H * 256), bf16), compiler_params=_cparams(3))(ckvn, w, p3, cosf, sinf)


def kprep_bwd(dkr3, ddt3, cosf, sinf, dp):
    B, S, W = dkr3.shape
    ts = _pick(S, (512, 256, 128))

    def body(dk_ref, ddt_ref, c_ref, s_ref, dp_in, kr_ref):
        acc = dk_ref[0, :, 0:128]
        for h in range(1, W // 128):
            acc = acc + dk_ref[0, :, h * 128:(h + 1) * 128]
        lane = lax.broadcasted_iota(jnp.int32, (1, 128), 1)
        kr_ref[0] = jnp.where(lane < ROPE, _rope_t(acc, c_ref[0], s_ref[0]), ddt_ref[0]).astype(bf16)

    return _row_call(body, "kprep_bwd", B, S, ts,
                     [_tok(ts, W), _tok(ts, 128), _tok(ts, 128), _tok(ts, 128), pl.BlockSpec(memory_space=pl.ANY)],
                     _tok(ts, 128, OFF_KRDT // 128), jax.ShapeDtypeStruct(dp.shape, bf16), aliases={4: 0})(dkr3, ddt3, cosf, sinf, dp)


def _shift_down(u, j):
    if j == 0:
        return u
    row = lax.broadcasted_iota(jnp.int32, u.shape, 0)
    return jnp.where(row < j, 0.0, pltpu.roll(u, j, 0))


def _shift_up(u, j):
    if j == 0:
        return u
    n = u.shape[0]
    row = lax.broadcasted_iota(jnp.int32, u.shape, 0)
    return jnp.where(row >= n - j, 0.0, pltpu.roll(u, n - j, 0))


def _conv(u, w, b, K):
    out = b
    for j in range(K):
        out = out + w[K - 1 - j:K - j, :] * _shift_down(u, j)
    return out


def _conv_bwd(u, du, w, K):
    dins = w[K - 1:K, :] * du
    dws = [None] * K
    dws[K - 1] = _rowsum(du * u)
    for j in range(1, K):
        sd = _shift_up(du, j)
        dins = dins + w[K - 1 - j:K - j, :] * sd
        dws[K - 1 - j] = _rowsum(sd * u)
    return dins, dws


CW = 256


def conv_ssd_fwd(p3, w, b):
    B, S, _ = p3.shape
    nb = D_XBC // CW

    def body(u_ref, w_ref, b_ref, o_ref, pre_ref):
        pre = _conv(u_ref[0], w_ref[...], b_ref[...], CONV_K)
        o_ref[0] = _silu(pre)
        pre_ref[0] = pre.astype(bf16)

    out = pl.BlockSpec((1, S, CW), lambda b, j: (b, 0, j))
    return pl.pallas_call(body, name="conv_ssd_fwd", grid=(B, nb),
                          in_specs=[pl.BlockSpec((1, S, CW), lambda b, j: (b, 0, OFF_XBC // CW + j)),
                                    pl.BlockSpec((CONV_K, CW), lambda b, j: (0, j)),
                                    pl.BlockSpec((1, CW), lambda b, j: (0, j))],
                          out_specs=[out, out],
                          out_shape=[jax.ShapeDtypeStruct((B, S, D_XBC), f32), jax.ShapeDtypeStruct((B, S, D_XBC), bf16)],
                          compiler_params=_cparams(2))(p3, w, b)


def conv_ssd_bwd(p3, pre3, dxc3, w, dp):
    B, S, _ = p3.shape
    nb = D_XBC // CW

    def body(u_ref, pre_ref, d_ref, w_ref, dp_in, du_ref, dw_ref, db_ref):
        @pl.when(pl.program_id(1) == 0)
        def _():
            dw_ref[...] = jnp.zeros_like(dw_ref)
            db_ref[...] = jnp.zeros_like(db_ref)

        u, wv = u_ref[0], w_ref[...]
        dpre = d_ref[0] * _dsilu(pre_ref[0].astype(f32))
        dins, dws = _conv_bwd(u, dpre, wv, CONV_K)
        du_ref[0] = dins.astype(bf16)
        for k in range(CONV_K):
            dw_ref[k:k + 1, :] += dws[k]
        db_ref[...] += _rowsum(dpre)

    return pl.pallas_call(body, name="conv_ssd_bwd", grid=(nb, B),
                          in_specs=[pl.BlockSpec((1, S, CW), lambda j, b: (b, 0, OFF_XBC // CW + j)),
                                    pl.BlockSpec((1, S, CW), lambda j, b: (b, 0, j)),
                                    pl.BlockSpec((1, S, CW), lambda j, b: (b, 0, j)),
                                    pl.BlockSpec((CONV_K, CW), lambda j, b: (0, j)), pl.BlockSpec(memory_space=pl.ANY)],
                          out_specs=[pl.BlockSpec((1, S, CW), lambda j, b: (b, 0, OFF_XBC // CW + j)),
                                     pl.BlockSpec((CONV_K, CW), lambda j, b: (0, j)),
                                     pl.BlockSpec((1, CW), lambda j, b: (0, j))],
                          out_shape=[jax.ShapeDtypeStruct(dp.shape, bf16), jax.ShapeDtypeStruct((CONV_K, D_XBC), f32),
                                     jax.ShapeDtypeStruct((1, D_XBC), f32)], input_output_aliases={4: 0},
                          compiler_params=_cparams(2))(p3, pre3, dxc3, w, dp)


def glu_fwd(u3, w, b):
    B, S, _ = u3.shape
    nb = D_FF // CW

    def body(ug_ref, uv_ref, wg_ref, wv_ref, bg_ref, bv_ref, o_ref, g_ref, v_ref):
        g = _conv(ug_ref[0].astype(f32), wg_ref[...], bg_ref[...], FF_K)
        v = _conv(uv_ref[0].astype(f32), wv_ref[...], bv_ref[...], FF_K)
        o_ref[0] = (_silu(g) * v).astype(bf16)
        g_ref[0] = g.astype(bf16)
        v_ref[0] = v.astype(bf16)

    def blk(off):
        return pl.BlockSpec((1, S, CW), lambda b, j: (b, 0, off + j))

    def par(rows, off):
        return pl.BlockSpec((rows, CW), lambda b, j: (0, off + j))

    shp = jax.ShapeDtypeStruct((B, S, D_FF), bf16)
    return pl.pallas_call(body, name="glu_fwd", grid=(B, nb),
                          in_specs=[blk(0), blk(nb), par(FF_K, 0), par(FF_K, nb), par(1, 0), par(1, nb)],
                          out_specs=[blk(0)] * 3, out_shape=[shp] * 3, compiler_params=_cparams(2))(u3, u3, w, w, b, b)


def glu_bwd(u3, g3, v3, da3, w):
    B, S, _ = u3.shape
    nb = D_FF // CW

    def body(ug_ref, uv_ref, g_ref, v_ref, da_ref, wg_ref, wv_ref, dug_ref, duv_ref, dwg_ref, dwv_ref, dbg_ref, dbv_ref):
        @pl.when(pl.program_id(1) == 0)
        def _():
            for r in (dwg_ref, dwv_ref, dbg_ref, dbv_ref):
                r[...] = jnp.zeros_like(r)

        ug, uv, da, wg, wv = ug_ref[0].astype(f32), uv_ref[0].astype(f32), da_ref[0].astype(f32), wg_ref[...], wv_ref[...]
        g, v = g_ref[0].astype(f32), v_ref[0].astype(f32)
        dg = da * v * _dsilu(g)
        dv = da * _silu(g)
        ding, dwsg = _conv_bwd(ug, dg, wg, FF_K)
        dinv, dwsv = _conv_bwd(uv, dv, wv, FF_K)
        dug_ref[0] = ding.astype(bf16)
        duv_ref[0] = dinv.astype(bf16)
        for k in range(FF_K):
            dwg_ref[k:k + 1, :] += dwsg[k]
            dwv_ref[k:k + 1, :] += dwsv[k]
        dbg_ref[...] += _rowsum(dg)
        dbv_ref[...] += _rowsum(dv)

    def blk(off):
        return pl.BlockSpec((1, S, CW), lambda j, b: (b, 0, off + j))

    def par(rows, off):
        return pl.BlockSpec((rows, CW), lambda j, b: (0, off + j))

    return pl.pallas_call(body, name="glu_bwd", grid=(nb, B),
                          in_specs=[blk(0), blk(nb), blk(0), blk(0), blk(0), par(FF_K, 0), par(FF_K, nb)],
                          out_specs=[blk(0), blk(0), par(FF_K, 0), par(FF_K, 0), par(1, 0), par(1, 0)],
                          out_shape=[jax.ShapeDtypeStruct((B, S, D_FF), bf16), jax.ShapeDtypeStruct((B, S, D_FF), bf16),
                                     jax.ShapeDtypeStruct((FF_K, D_FF), f32), jax.ShapeDtypeStruct((FF_K, D_FF), f32),
                                     jax.ShapeDtypeStruct((1, D_FF), f32), jax.ShapeDtypeStruct((1, D_FF), f32)],
                          compiler_params=_cparams(2))(u3, u3, g3, v3, da3, w, w)


def _ssd_decay(dtb, bias_row, alog_row):
    lane = lax.broadcasted_iota(jnp.int32, (1, 128), 1)
    hmask = (lane >= DT_LANE) & (lane < DT_LANE + SSD_HEADS)
    dt = jnp.where(hmask, jax.nn.softplus(dtb + bias_row), 0.0)
    a = dt * jnp.where(hmask, -jnp.exp(alog_row), 0.0)
    r = lax.broadcasted_iota(jnp.int32, (CHUNK, CHUNK), 0)
    c = lax.broadcasted_iota(jnp.int32, (CHUNK, CHUNK), 1)
    cs = jnp.dot((r >= c).astype(f32), a, precision=HIGHEST, preferred_element_type=f32)
    return dt, cs


def _expand(xt):
    return jnp.concatenate([jnp.broadcast_to(xt[DT_LANE + h:DT_LANE + h + 1, :], (SSD_HD, xt.shape[1]))
                            for h in range(SSD_HEADS)], axis=0)


_NT = (((1,), (1,)), ((), ()))
_TN = (((0,), (0,)), ((), ()))
GH = SSD_HEADS // 2
GR = GH * SSD_HD


def ssd_fwd(xc3, p3, bias_row, alog_row, dcol):
    B, S, _ = xc3.shape
    nc = S // CHUNK

    def body(xs_ref, bc_ref, dtb_ref, bias_ref, alog_ref, dcol_ref, y_ref, st_ref, state, yT):
        @pl.when(pl.program_id(1) == 0)
        def _():
            state[...] = jnp.zeros_like(state)

        dt, cs = _ssd_decay(dtb_ref[0], bias_ref[...], alog_ref[...])
        csT = cs.T
        eT = jnp.exp(csT)
        decX = _expand(jnp.exp(csT[:, CHUNK - 1:CHUNK] - csT))
        eX = _expand(eT)
        elastX = eX[:, CHUNK - 1:CHUNK]
        xsT = xs_ref[0].T
        uT = xsT * _expand(dt.T)
        bc = bc_ref[0]
        st_ref[0, 0] = state[...]
        srow = lax.broadcasted_iota(jnp.int32, (CHUNK, CHUNK), 0)
        lcol = lax.broadcasted_iota(jnp.int32, (CHUNK, CHUNK), 1)
        for g in range(2):
            Bg = bc[:, g * SSD_N:(g + 1) * SSD_N].astype(bf16)
            Cg = bc[:, (2 + g) * SSD_N:(3 + g) * SSD_N].astype(bf16)
            GT = lax.dot_general(Bg, Cg, _NT, preferred_element_type=f32)
            rows = slice(g * GR, (g + 1) * GR)
            Sg = state[rows]
            yoffT = lax.dot_general(Sg.astype(bf16), Cg, _NT, preferred_element_type=f32) * eX[rows]
            state[rows] = Sg * elastX[rows] + jnp.dot((uT[rows] * decX[rows]).astype(bf16), Bg, preferred_element_type=f32)
            for k in range(GH):
                h = g * GH + k
                hr = slice(h * SSD_HD, (h + 1) * SSD_HD)
                seg = csT[DT_LANE + h:DT_LANE + h + 1, :] - cs[:, DT_LANE + h:DT_LANE + h + 1]
                LT = jnp.where(lcol >= srow, jnp.exp(jnp.minimum(seg, 0.0)), 0.0)
                yT[hr] = (jnp.dot(uT[hr].astype(bf16), (GT * LT).astype(bf16), preferred_element_type=f32)
                          + yoffT[k * SSD_HD:(k + 1) * SSD_HD] + dcol_ref[hr] * xsT[hr])
        y_ref[0] = yT[...].T

    return pl.pallas_call(body, name="ssd_fwd", grid=(B, nc),
                          in_specs=[pl.BlockSpec((1, CHUNK, D_SSD), lambda b, c: (b, c, 0)),
                                    pl.BlockSpec((1, CHUNK, 512), lambda b, c: (b, c, 2)),
                                    pl.BlockSpec((1, CHUNK, 128), lambda b, c: (b, c, OFF_KRDT // 128)),
                                    _const(1, 128), _const(1, 128), _const(D_SSD, 1)],
                          out_specs=[pl.BlockSpec((1, CHUNK, D_SSD), lambda b, c: (b, c, 0)),
                                     pl.BlockSpec((1, 1, D_SSD, SSD_N), lambda b, c: (b, c, 0, 0))],
                          out_shape=[jax.ShapeDtypeStruct((B, S, D_SSD), f32), jax.ShapeDtypeStruct((B, nc, D_SSD, SSD_N), f32)],
                          scratch_shapes=[pltpu.VMEM((D_SSD, SSD_N), f32), pltpu.VMEM((D_SSD, CHUNK), f32)],
                          compiler_params=_cparams(2))(xc3, xc3, p3, bias_row, alog_row, dcol)


def ssd_bwd(xc3, p3, dy3, states, bias_row, alog_row, bias_col, alog_col, dcol, comm=None):
    B, S, _ = xc3.shape
    nc = S // CHUNK

    def body(xs_ref, bc_ref, dtb_ref, dy_ref, st_ref, bias_ref, alog_ref, biasc_ref, alogc_ref, dcol_ref,
             dxc_ref, ddt_ref, dalog_ref, dd_ref, dbias_ref, dS, dUT, accA, accD, accB, dcs_diag):
        @pl.when(pl.program_id(1) == 0)
        def _():
            dS[...] = jnp.zeros_like(dS)

        @pl.when(_first())
        def _():
            accA[...] = jnp.zeros_like(accA)
            accD[...] = jnp.zeros_like(accD)
            accB[...] = jnp.zeros_like(accB)

        dtb = dtb_ref[0]
        dt, cs = _ssd_decay(dtb, bias_ref[...], alog_ref[...])
        dtT, csT = dt.T, cs.T
        decX = _expand(jnp.exp(csT[:, CHUNK - 1:CHUNK] - csT))
        eX = _expand(jnp.exp(csT))
        dtX = _expand(dtT)
        elastX = eX[:, CHUNK - 1:CHUNK]
        xsT = xs_ref[0].T
        uT = xsT * dtX
        dYT = dy_ref[0].T
        bc = bc_ref[0]
        lrow = lax.broadcasted_iota(jnp.int32, (CHUNK, CHUNK), 0)
        scol = lax.broadcasted_iota(jnp.int32, (CHUNK, CHUNK), 1)
        dcs_diag[...] = jnp.zeros_like(dcs_diag)
        rs_cols = jnp.zeros((CHUNK, 128), f32)
        vparts, zparts = [], []
        for g in range(2):
            Bf = bc[:, g * SSD_N:(g + 1) * SSD_N]
            Bg = Bf.astype(bf16)
            Cg = bc[:, (2 + g) * SSD_N:(3 + g) * SSD_N].astype(bf16)
            G = lax.dot_general(Cg, Bg, _NT, preferred_element_type=f32)
            BgT = Bf.T.astype(bf16)
            rows = slice(g * GR, (g + 1) * GR)
            dSg = dS[rows]
            Sg = st_ref[0, 0, rows, :]
            dUst = jnp.dot(dSg.astype(bf16), BgT, preferred_element_type=f32) * decX[rows]
            yoffT = lax.dot_general(Sg.astype(bf16), Cg, _NT, preferred_element_type=f32) * eX[rows]
            zparts.append(dYT[rows] * yoffT - dUst * uT[rows])
            dG = jnp.zeros((CHUNK, CHUNK), f32)
            for k in range(GH):
                h = g * GH + k
                hr = slice(h * SSD_HD, (h + 1) * SSD_HD)
                seg = cs[:, DT_LANE + h:DT_LANE + h + 1] - csT[DT_LANE + h:DT_LANE + h + 1, :]
                L = jnp.where(lrow >= scol, jnp.exp(jnp.minimum(seg, 0.0)), 0.0)
                M = G * L
                dYh = dYT[hr].astype(bf16)
                dUT[hr] = jnp.dot(dYh, M.astype(bf16), preferred_element_type=f32) + dUst[k * SSD_HD:(k + 1) * SSD_HD]
                dM = lax.dot_general(dYh, uT[hr].astype(bf16), _TN, preferred_element_type=f32)
                dG = dG + dM * L
                Wm = dM * M
                rs_cols = jnp.where(scol == DT_LANE + h, jnp.sum(Wm, axis=1, keepdims=True), rs_cols)
                dcs_diag[DT_LANE + h:DT_LANE + h + 1, :] = -_rowsum(Wm)
            dGb = dG.astype(bf16)
            dYe = (dYT[rows] * eX[rows]).astype(bf16)
            ude = (uT[rows] * decX[rows]).astype(bf16)
            dC = jnp.dot(dGb, Bg, preferred_element_type=f32) + lax.dot_general(dYe, Sg.astype(bf16), _TN, preferred_element_type=f32)
            dB = (lax.dot_general(dGb, Cg, _TN, preferred_element_type=f32)
                  + lax.dot_general(ude, dSg.astype(bf16), _TN, preferred_element_type=f32))
            dxc_ref[0, :, D_SSD + g * SSD_N:D_SSD + (g + 1) * SSD_N] = dB
            dxc_ref[0, :, D_SSD + (2 + g) * SSD_N:D_SSD + (3 + g) * SSD_N] = dC
            vparts.append(elastX[rows] * jnp.sum(dSg * Sg, axis=1, keepdims=True)
                          + jnp.sum(dUst * uT[rows], axis=1, keepdims=True))
            dS[rows] = elastX[rows] * dSg + jnp.dot(dYe, Cg, preferred_element_type=f32)
        dU = dUT[...]
        dcv = dcol_ref[...]
        dxc_ref[0, :, 0:D_SSD] = (dtX * dU + dcv * dYT).T
        lane = lax.broadcasted_iota(jnp.int32, (D_SSD, CHUNK), 1)
        Z = jnp.concatenate(zparts, axis=0) + jnp.where(lane == CHUNK - 1, jnp.concatenate(vparts, axis=0), 0.0)
        hr_ = lax.broadcasted_iota(jnp.int32, (128, D_SSD), 0)
        hc_ = lax.broadcasted_iota(jnp.int32, (128, D_SSD), 1)
        hsel = (hr_ - DT_LANE == jnp.right_shift(hc_, 6)).astype(bf16)
        summands = jnp.concatenate([Z, dU * xsT, dYT * xsT], axis=1)
        hi = summands.astype(bf16)
        lo = (summands - hi.astype(f32)).astype(bf16)
        red = jnp.dot(hsel, hi, preferred_element_type=f32) + jnp.dot(hsel, lo, preferred_element_type=f32)
        dcsT = red[:, 0:CHUNK] + dcs_diag[...] + rs_cols.T
        daT = jnp.dot(dcsT, (lrow >= scol).astype(f32), precision=HIGHEST, preferred_element_type=f32)
        rowi = lax.broadcasted_iota(jnp.int32, (128, 1), 0)
        hmask = (rowi >= DT_LANE) & (rowi < DT_LANE + SSD_HEADS)
        a_col = jnp.where(hmask, -jnp.exp(alogc_ref[...]), 0.0)
        ddtT = red[:, CHUNK:2 * CHUNK] + a_col * daT
        ddt_rawT = jnp.where(hmask, ddtT * jax.nn.sigmoid(dtb.T + biasc_ref[...]), 0.0)
        ddt_ref[0] = ddt_rawT.T
        accA[...] += daT * dtT
        accD[...] += red[:, 2 * CHUNK:3 * CHUNK]
        accB[...] += ddt_rawT

        @pl.when((pl.program_id(0) == B - 1) & (pl.program_id(1) == nc - 1))
        def _():
            dalog_ref[...] = jnp.broadcast_to(jnp.sum(accA[...], axis=1, keepdims=True) * a_col, (128, 128))
            dd_ref[...] = jnp.broadcast_to(jnp.sum(accD[...], axis=1, keepdims=True), (128, 128))
            dbias_ref[...] = jnp.broadcast_to(jnp.sum(accB[...], axis=1, keepdims=True), (128, 128))

    def rev(width, cb):
        return pl.BlockSpec((1, CHUNK, width), lambda b, c: (b, nc - 1 - c, cb))

    acc_spec = pl.BlockSpec((128, 128), lambda b, c: (0, 0))
    acc_shape = jax.ShapeDtypeStruct((128, 128), f32)
    body, c_args, c_in, c_out, c_shapes, c_sems = _fuse_exchange(body, comm, 10, 5, 6, (B, nc))
    res = pl.pallas_call(body, name="ssd_bwd_x" if comm else "ssd_bwd", grid=(B, nc),
                         in_specs=[rev(D_SSD, 0), rev(512, 2), rev(128, OFF_KRDT // 128), rev(D_SSD, 0),
                                   pl.BlockSpec((1, 1, D_SSD, SSD_N), lambda b, c: (b, nc - 1 - c, 0, 0)),
                                   _const(1, 128), _const(1, 128), _const(128, 1), _const(128, 1), _const(D_SSD, 1)] + c_in,
                         out_specs=[rev(D_XBC, 0), rev(128, 0), acc_spec, acc_spec, acc_spec] + c_out,
                         out_shape=[jax.ShapeDtypeStruct((B, S, D_XBC), f32), jax.ShapeDtypeStruct((B, S, 128), f32),
                                    acc_shape, acc_shape, acc_shape] + c_shapes,
                         scratch_shapes=[pltpu.VMEM((D_SSD, SSD_N), f32), pltpu.VMEM((D_SSD, CHUNK), f32),
                                         pltpu.VMEM((128, 128), f32), pltpu.VMEM((128, 128), f32), pltpu.VMEM((128, 128), f32),
                                         pltpu.VMEM((128, 128), f32)] + c_sems,
                         compiler_params=_cparams(2))(xc3, xc3, p3, dy3, states, bias_row, alog_row, bias_col, alog_col, dcol, *c_args)
    return (*res[:5], list(res[5:]))


ATT_SCALE = float(QK) ** -0.5
LOG2E = 1.4426950408889634
LN2 = 0.6931471805599453
Q_FOLD = ATT_SCALE * LOG2E
NEG = -1e30
HP = 2


def _att_block(S):
    return _pick(S, (512, 256, 128))


def attn_fwd(q3, k3, v3, comm=None):
    B, S, _ = q3.shape
    bq = _att_block(S)
    nq = S // bq

    def body(q_ref, k_ref, v_ref, o_ref, lse_ref, m_s, l_s, acc):
        i, j = pl.program_id(2), pl.program_id(3)

        @pl.when(j == 0)
        def _():
            m_s[...] = jnp.full_like(m_s, NEG)
            l_s[...] = jnp.zeros_like(l_s)
            acc[...] = jnp.zeros_like(acc)

        def step(masked):
            for t in range(HP):
                qk = slice(t * 256, (t + 1) * 256)
                st = lax.dot_general(k_ref[0, :, qk], q_ref[0, :, qk], _NT, preferred_element_type=f32)
                if masked:
                    r = lax.broadcasted_iota(jnp.int32, (bq, bq), 0)
                    c = lax.broadcasted_iota(jnp.int32, (bq, bq), 1)
                    st = jnp.where(c >= r, st, NEG)
                m_old = m_s[t]
                m_new = jnp.maximum(m_old, jnp.max(st, axis=0, keepdims=True))
                alpha = jnp.exp2(m_old - m_new)
                pt = jnp.exp2(st - m_new)
                l_s[t] = alpha * l_s[t] + jnp.sum(pt, axis=0, keepdims=True)
                acc[t] = alpha * acc[t] + lax.dot_general(v_ref[0, :, t * VD:(t + 1) * VD], pt.astype(bf16), _TN,
                                                          preferred_element_type=f32)
                m_s[t] = m_new

        @pl.when(j < i)
        def _():
            step(False)

        @pl.when(j == i)
        def _():
            step(True)
            for t in range(HP):
                o_ref[0, :, t * VD:(t + 1) * VD] = (acc[t] / l_s[t]).T
                lse_ref[0, t] = m_s[t] + jnp.log2(l_s[t])

    grid = (B, MLA_H // HP, nq, nq)
    body, c_args, c_in, c_out, c_shapes, c_sems = _fuse_exchange(body, comm, 3, 2, 3, grid)
    res = pl.pallas_call(body, name="attn_fwd_x" if comm else "attn_fwd", grid=grid,
                         in_specs=[pl.BlockSpec((1, bq, HP * 256), lambda b, h, i, j: (b, i, h)),
                                   pl.BlockSpec((1, bq, HP * 256), lambda b, h, i, j: (b, jnp.minimum(j, i), h)),
                                   pl.BlockSpec((1, bq, HP * VD), lambda b, h, i, j: (b, jnp.minimum(j, i), h))] + c_in,
                         out_specs=[pl.BlockSpec((1, bq, HP * VD), lambda b, h, i, j: (b, i, h)),
                                    pl.BlockSpec((1, HP, 1, bq), lambda b, h, i, j: (b, h, 0, i))] + c_out,
                         out_shape=[jax.ShapeDtypeStruct((B, S, MLA_H * VD), f32), jax.ShapeDtypeStruct((B, MLA_H, 1, S), f32)] + c_shapes,
                         scratch_shapes=[pltpu.VMEM((HP, 1, bq), f32), pltpu.VMEM((HP, 1, bq), f32), pltpu.VMEM((HP, VD, bq), f32)] + c_sems,
                         compiler_params=_cparams(4))(q3, k3, v3, *c_args)
    return res[0], res[1], list(res[2:])


def rms_o_bwd(o3, dcat3, g):
    B, S, C = o3.shape
    ts = _pick(S, (512, 256, 128))

    def body(x_ref, do_ref, g_ref, dx_ref, dg_ref, d_ref):
        @pl.when(_first())
        def _():
            dg_ref[...] = jnp.zeros_like(dg_ref)

        x, do = x_ref[0], do_ref[0].astype(f32)
        r = lax.rsqrt(jnp.mean(x * x, axis=-1, keepdims=True) + EPS)
        xh = x * r
        dg_ref[...] += _rowsum(do * xh)
        dxh = do * g_ref[...]
        dx = r * (dxh - xh * jnp.mean(dxh * xh, axis=-1, keepdims=True))
        dx_ref[0] = dx
        for h in range(MLA_H):
            vs = slice(h * VD, (h + 1) * VD)
            d_ref[0, h] = jnp.sum(dx[:, vs] * x[:, vs], axis=-1, keepdims=True)

    return _row_call(body, "rms_o_bwd", B, S, ts, [_tok(ts, C), _tok(ts, C, 1), _const(1, C)],
                     [_tok(ts, C), _const(1, C), pl.BlockSpec((1, MLA_H, ts, 1), lambda b, s: (b, 0, s, 0))],
                     [jax.ShapeDtypeStruct((B, S, C), f32), jax.ShapeDtypeStruct((1, C), f32),
                      jax.ShapeDtypeStruct((B, MLA_H, S, 1), f32)])(o3, dcat3, g)


def attn_bwd(q3, k3, v3, do3, lse_row, delta_row, cosf, sinf, comm=None):
    B, S, _ = q3.shape
    bq = _att_block(S)
    nq = S // bq

    def body(q_ref, k_ref, v_ref, do_ref, lse_ref, dl_ref, cos_ref, sin_ref, dkn_ref, dv_ref, dkr_ref, dq_hbm, dk_acc, dv_acc,
             dq_scr, stage, dq_sem):
        b, hp, j, i = pl.program_id(0), pl.program_id(1), pl.program_id(2), pl.program_id(3)
        rows = pl.ds(pl.multiple_of(i * bq, bq), bq)

        @pl.when((j == 0) & (i == 0))
        def _():
            dq_scr[...] = jnp.zeros_like(dq_scr)

        @pl.when(i == 0)
        def _():
            dk_acc[...] = jnp.zeros_like(dk_acc)
            dv_acc[...] = jnp.zeros_like(dv_acc)

        def step(masked):
            for t in range(HP):
                qk, vs = slice(t * 256, (t + 1) * 256), slice(t * VD, (t + 1) * VD)
                q, k = q_ref[0, :, qk], k_ref[0, :, qk]
                do = do_ref[0, :, vs].astype(bf16)
                pt = jnp.exp2(lax.dot_general(k, q, _NT, preferred_element_type=f32) - lse_ref[0, t])
                if masked:
                    r = lax.broadcasted_iota(jnp.int32, (bq, bq), 0)
                    c = lax.broadcasted_iota(jnp.int32, (bq, bq), 1)
                    pt = jnp.where(c >= r, pt, 0.0)
                dv_acc[t] += jnp.dot(pt.astype(bf16), do, preferred_element_type=f32)
                dpt = lax.dot_general(v_ref[0, :, vs], do, _NT, preferred_element_type=f32)
                dst = (pt * (dpt - dl_ref[0, t])).astype(bf16)
                dk_acc[t] += jnp.dot(dst, q, preferred_element_type=f32)
                dq_scr[t, rows, :] += lax.dot_general(dst, k, _TN, preferred_element_type=f32)

        @pl.when(i > j)
        def _():
            step(False)

        @pl.when(i == j)
        def _():
            step(True)
            for t in range(HP):
                d = dq_scr[t, rows, :] * (LN2 * Q_FOLD)
                stage[t, :, 0:NOPE] = d[:, 0:NOPE].astype(bf16)
                stage[t, :, NOPE:] = _rope_t(d[:, NOPE:], cos_ref[0], sin_ref[0]).astype(bf16)
                cp = pltpu.make_async_copy(stage.at[t], dq_hbm.at[b, rows, pl.ds(pl.multiple_of((hp * HP + t) * 256, 256), 256)],
                                           dq_sem.at[t])
                cp.start()
                cp.wait()

        @pl.when(i == nq - 1)
        def _():
            kr = jnp.zeros((bq, 128), f32)
            for t in range(HP):
                dkn_ref[0, :, t * NOPE:(t + 1) * NOPE] = (dk_acc[t, :, 0:NOPE] * LN2).astype(bf16)
                dv_ref[0, :, t * VD:(t + 1) * VD] = dv_acc[t].astype(bf16)
                kr = kr + dk_acc[t, :, NOPE:]
            dkr_ref[0] = kr * LN2

    kspec = pl.BlockSpec((1, bq, HP * 256), lambda b, h, j, i: (b, j, h))
    vspec = pl.BlockSpec((1, bq, HP * VD), lambda b, h, j, i: (b, j, h))
    krspec = pl.BlockSpec((1, bq, 128), lambda b, h, j, i: (b, j, h))
    rspec = pl.BlockSpec((1, HP, 1, bq), lambda b, h, j, i: (b, h, 0, jnp.maximum(i, j)))
    tspec = pl.BlockSpec((1, bq, 128), lambda b, h, j, i: (b, jnp.maximum(i, j), 0))
    grid = (B, MLA_H // HP, nq, nq)
    body, c_args, c_in, c_out, c_shapes, c_sems = _fuse_exchange(body, comm, 8, 4, 5, grid)
    res = pl.pallas_call(body, name="attn_bwd_x" if comm else "attn_bwd", grid=grid,
                         in_specs=[pl.BlockSpec((1, bq, HP * 256), lambda b, h, j, i: (b, jnp.maximum(i, j), h)), kspec, vspec,
                                   pl.BlockSpec((1, bq, HP * VD), lambda b, h, j, i: (b, jnp.maximum(i, j), h)), rspec, rspec,
                                   tspec, tspec] + c_in,
                         out_specs=[vspec, vspec, krspec, pl.BlockSpec(memory_space=pltpu.HBM)] + c_out,
                         out_shape=[jax.ShapeDtypeStruct((B, S, MLA_H * NOPE), bf16), jax.ShapeDtypeStruct((B, S, MLA_H * VD), bf16),
                                    jax.ShapeDtypeStruct((B, S, MLA_H // HP * 128), f32),
                                    jax.ShapeDtypeStruct((B, S, MLA_H * 256), bf16)] + c_shapes,
                         scratch_shapes=[pltpu.VMEM((HP, bq, 256), f32), pltpu.VMEM((HP, bq, VD), f32), pltpu.VMEM((HP, S, 256), f32),
                                         pltpu.VMEM((HP, bq, 256), bf16), pltpu.SemaphoreType.DMA((HP,))] + c_sems,
                         compiler_params=_cparams(4))(q3, k3, v3, do3, lse_row, delta_row, cosf, sinf, *c_args)
    return res[3], res[0], res[1], res[2], list(res[4:])


def ada_fwd(c_all, w, b):
    n = w.shape[1]

    def body(c_ref, w_ref, b_ref, o_ref):
        o_ref[...] = jnp.dot(_silu(c_ref[...]).astype(bf16), w_ref[...].astype(bf16), preferred_element_type=f32) + b_ref[...]

    return pl.pallas_call(body, name="ada_fwd", out_shape=jax.ShapeDtypeStruct((c_all.shape[0], n), f32),
                          compiler_params=pltpu.CompilerParams(vmem_limit_bytes=VMEM_LIMIT))(c_all, w, b)


def ada_bwd(c_all, dmod):
    n = dmod.shape[1]

    def body(c_ref, d_ref, o_ref):
        o_ref[...] = lax.dot_general(_silu(c_ref[...]).astype(bf16), d_ref[...].astype(bf16), _TN, preferred_element_type=f32)

    return pl.pallas_call(body, name="ada_bwd", out_shape=jax.ShapeDtypeStruct((c_all.shape[1], n), f32),
                          compiler_params=pltpu.CompilerParams(vmem_limit_bytes=VMEM_LIMIT))(c_all, dmod)


def sum_leading(x, name):
    n, R, _ = x.shape
    tr = _pick(R, (512, 256, 128, 64, 32, 16, 8))

    def body(x_ref, o_ref):
        acc = x_ref[0].astype(f32)
        for k in range(1, n):
            acc = acc + x_ref[k].astype(f32)
        o_ref[...] = acc

    return pl.pallas_call(body, name=name, grid=(R // tr,), in_specs=[pl.BlockSpec((n, tr, 128), lambda i: (0, i, 0))],
                          out_specs=pl.BlockSpec((tr, 128), lambda i: (i, 0)), out_shape=jax.ShapeDtypeStruct((R, 128), f32),
                          compiler_params=_cparams(1))(x)


def _adamw_body(w_ref, g_ref, m_ref, v_ref, d_ref, mo_ref, vo_ref):
    gv = g_ref[...]
    mn = ADAM_B1 * m_ref[...] + (1.0 - ADAM_B1) * gv
    vn = ADAM_B2 * v_ref[...] + (1.0 - ADAM_B2) * jnp.square(gv)
    m_hat = mn / (1.0 - ADAM_B1 ** ADAM_STEP)
    v_hat = vn / (1.0 - ADAM_B2 ** ADAM_STEP)
    d_ref[...] = -ADAM_LR * (m_hat / (jnp.sqrt(v_hat) + ADAM_EPS) + ADAM_WD * w_ref[...])
    mo_ref[...] = mn
    vo_ref[...] = vn


def adamw(w, g, m, v):
    R = w.shape[0]
    tr = _pick(R, (512, 256, 128, 64, 32, 16, 8))
    spec = pl.BlockSpec((tr, 128), lambda i: (i, 0))
    shp = jax.ShapeDtypeStruct((R, 128), f32)
    return pl.pallas_call(functools.partial(_adamw_body), name="adamw", grid=(R // tr,), in_specs=[spec] * 4,
                          out_specs=[spec] * 3, out_shape=[shp] * 3, compiler_params=_cparams(1))(w, g, m, v)


def _row_tile(a):
    return _pick(a, (256, 128, 64, 32, 16, 8)) if a % 8 == 0 else a


def adamw_nd(w, g, m, v):
    L, a, b = w.shape
    ta = _row_tile(a)
    spec = pl.BlockSpec((1, ta, b), lambda l, i: (l, i, 0))
    shp = jax.ShapeDtypeStruct((L, a, b), f32)
    return pl.pallas_call(functools.partial(_adamw_body), name="adamw_nd", grid=(L, a // ta), in_specs=[spec] * 4,
                          out_specs=[spec] * 3, out_shape=[shp] * 3, compiler_params=_cparams(2))(w, g, m, v)


def sum_slots(x):
    n, L, a, b = x.shape
    ta = _row_tile(a)

    def body(x_ref, o_ref):
        acc = x_ref[0].astype(f32)
        for k in range(1, n):
            acc = acc + x_ref[k].astype(f32)
        o_ref[...] = acc

    return pl.pallas_call(body, name="sum_slots", grid=(L, a // ta),
                          in_specs=[pl.BlockSpec((n, 1, ta, b), lambda l, i: (0, l, i, 0))],
                          out_specs=pl.BlockSpec((1, ta, b), lambda l, i: (l, i, 0)),
                          out_shape=jax.ShapeDtypeStruct((L, a, b), f32), compiler_params=_cparams(2))(x)


def _exchange_copies(ins, outs, sems, scatter):
    send_sems, recv_sems, local_sems = sems
    x, y, c = lax.axis_index("x"), lax.axis_index("y"), lax.axis_index("c")
    me = 4 * x + 2 * y + c
    locals_, sends, recvs = [], [], []
    for a in range(len(ins)):
        locals_.append(pltpu.make_async_copy(ins[a].at[me] if scatter[a] else ins[a], outs[a].at[me], local_sems.at[a]))
        for k in range(N_DEV - 1):
            px = 1 - x if (k + 1) & 4 else x
            py = 1 - y if (k + 1) & 2 else y
            pc = 1 - c if (k + 1) & 1 else c
            pid = 4 * px + 2 * py + pc
            src = ins[a].at[pid] if scatter[a] else ins[a]
            for slot, group in ((me, sends), (pid, recvs)):
                group.append(pltpu.make_async_remote_copy(src_ref=src, dst_ref=outs[a].at[slot], send_sem=send_sems.at[a, k],
                                                          recv_sem=recv_sems.at[a, k], device_id=(px, py, pc),
                                                          device_id_type=pl.DeviceIdType.MESH))
    return locals_, sends, recvs


def _exchange_start(ins, outs, sems, scatter):
    locals_, sends, _ = _exchange_copies(ins, outs, sems, scatter)
    for cp in locals_ + sends:
        cp.start()


def _exchange_wait(ins, outs, sems, scatter):
    locals_, sends, recvs = _exchange_copies(ins, outs, sems, scatter)
    for cp in recvs:
        cp.wait_recv()
    for cp in sends:
        cp.wait_send()
    for cp in locals_:
        cp.wait()


def _exchange_shapes(arrays, scatter):
    return [jax.ShapeDtypeStruct((N_DEV,) + tuple(a.shape[1:] if s else a.shape), a.dtype) for a, s in zip(arrays, scatter)]


def _flags(scatter, n):
    return [scatter] * n if isinstance(scatter, bool) else list(scatter)


def _exchange_sems(n):
    return [pltpu.SemaphoreType.DMA((n, N_DEV - 1)), pltpu.SemaphoreType.DMA((n, N_DEV - 1)), pltpu.SemaphoreType.DMA((n,))]


def _fuse_exchange(core, comm, n_in, n_out, n_scr, grid):
    if comm is None:
        return core, [], [], [], [], []
    arrays, scatter = comm
    n = len(arrays)
    scatter = _flags(scatter, n)

    def body(*refs):
        a, b, c = n_in + n, n_in + n + n_out, n_in + 2 * n + n_out
        cin, cout, sems = refs[n_in:a], refs[b:c], refs[c + n_scr:]
        ids = [pl.program_id(d) for d in range(len(grid))]
        first = functools.reduce(lambda p, q: p & q, [i == 0 for i in ids])
        last = functools.reduce(lambda p, q: p & q, [i == g - 1 for i, g in zip(ids, grid)])

        @pl.when(first)
        def _():
            _exchange_start(cin, cout, sems, scatter)

        core(*refs[:n_in], *refs[a:b], *refs[c:c + n_scr])

        @pl.when(last)
        def _():
            _exchange_wait(cin, cout, sems, scatter)

    hbm = pl.BlockSpec(memory_space=pltpu.HBM)
    return body, list(arrays), [hbm] * n, [hbm] * n, _exchange_shapes(arrays, scatter), _exchange_sems(n)


def exchange(arrays, scatter, name):
    n = len(arrays)
    scatter = _flags(scatter, n)

    def body(*refs):
        ins, outs, sems = refs[:n], refs[n:2 * n], refs[2 * n:]
        _exchange_start(ins, outs, sems, scatter)
        _exchange_wait(ins, outs, sems, scatter)

    hbm = pl.BlockSpec(memory_space=pltpu.HBM)
    return pl.pallas_call(body, name=name, in_specs=[hbm] * n, out_specs=[hbm] * n,
                          out_shape=_exchange_shapes(arrays, scatter), scratch_shapes=_exchange_sems(n))(*arrays)


BIG = (("w_in", "col"), ("conv_w", "col"), ("w_uq", "col"), ("w_ukv", "col"), ("w_out", "row"), ("w_up", "col"),
       ("conv_ff_w", "col"), ("w_down", "row"))
SMALL = ("b_ada", "norm_mix", "conv_b", "dt_bias", "a_log", "d_skip", "ssd_norm", "q_norm", "kv_norm", "attn_norm",
         "norm_mlp", "conv_ff_b", "final_norm")
CONVS = ("conv_w", "conv_ff_w")
PACK_ALIGN = 2048


def _padded(n):
    return -(-n // PACK_ALIGN) * PACK_ALIGN


def _flat_pad(a):
    f = a.reshape(-1)
    return jnp.pad(f, (0, _padded(f.shape[0]) - f.shape[0]))


PACK_ROWS = 512


def pack(arrs):
    f = jnp.concatenate([_flat_pad(a) for a in arrs])
    n = PACK_ROWS * 128
    return jnp.pad(f, (0, -(-f.shape[0] // n) * n - f.shape[0])).reshape(-1, 128)


def unpack(flat, shapes):
    f = flat.reshape(-1)
    out, off = [], 0
    for s in shapes:
        n = int(np.prod(s))
        out.append(f[off:off + n].reshape(s))
        off += _padded(n)
    return out


def shards_to_full(g, kind):
    _, a, b = g.shape
    if kind == "col":
        return g.transpose(1, 0, 2).reshape(a, N_DEV * b)
    return g.reshape(N_DEV * a, b)


def full_to_shards(full, kind):
    if kind == "col":
        a, nb = full.shape
        return full.reshape(a, N_DEV, nb // N_DEV).transpose(1, 0, 2)
    na, b = full.shape
    return full.reshape(N_DEV, na // N_DEV, b)


def w_in_layout(w):
    z = lambda n: jnp.zeros(w.shape[:-1] + (n,), w.dtype)
    return jnp.concatenate([w[..., :2560], w[..., 2576:2960], z(128), w[..., 2960:3216], w[..., 3216:3280],
                            w[..., 2560:2576], z(48)], axis=-1)


def w_in_unlayout(g):
    return jnp.concatenate([g[..., :2560], g[..., 3392:3408], g[..., 2560:2944], g[..., 3072:3328], g[..., 3328:3392]], axis=-1)


def w_uq_layout(w):
    return jnp.pad(w.reshape(Q_RANK, MLA_H, QK), ((0, 0), (0, 0), (0, 256 - QK))).reshape(Q_RANK, MLA_H * 256)


def w_uq_unlayout(g):
    return g.reshape(Q_RANK, MLA_H, 256)[:, :, :QK].reshape(Q_RANK, MLA_H * QK)


def w_ukv_layout(w):
    return w.reshape(KV_RANK, MLA_H, 2, 128).transpose(0, 2, 1, 3).reshape(KV_RANK, 2 * MLA_H * 128)


def w_ukv_unlayout(g):
    return g.reshape(KV_RANK, 2, MLA_H, 128).transpose(0, 2, 1, 3).reshape(KV_RANK, 2 * MLA_H * 128)


LAYOUTS = {"w_in": (w_in_layout, w_in_unlayout), "w_uq": (w_uq_layout, w_uq_unlayout), "w_ukv": (w_ukv_layout, w_ukv_unlayout)}
FIRST, REST = BIG[:4], BIG[4:]


def layer_weights(gathered, entries):
    full = {n: shards_to_full(g, kind) for (n, kind), g in zip(entries, gathered)}
    return {n: LAYOUTS[n][0](w) if n in LAYOUTS else w for n, w in full.items()}


def layer_grad_slices(g, entries):
    return [full_to_shards(LAYOUTS[n][1](g[n]) if n in LAYOUTS else g[n], kind).astype(bf16) for n, kind in entries]


def _head_row(v):
    return jnp.zeros((1, 128), f32).at[0, DT_LANE:DT_LANE + SSD_HEADS].set(v)


def layer_fwd(x3, mod, W, P, l, cosf, sinf, comm=None, late=None, comm_up=None):
    B, S, _ = x3.shape
    T = B * S
    sv = {}
    h = normmod_fwd(x3, mod, P["norm_mix"][l][None], 0, 1)
    p = mm(h.reshape(T, D), W["w_in"], "nn", "mm_in")
    p3 = p.reshape(B, S, IN_COLS)
    bias_row, alog_row = _head_row(P["dt_bias"][l]), _head_row(P["a_log"][l])
    dcol = jnp.repeat(P["d_skip"][l], SSD_HD)[:, None]
    xc3, xpre = conv_ssd_fwd(p3, W["conv_w"], P["conv_b"][l][None])
    yc3, states = ssd_fwd(xc3, p3, bias_row, alog_row, dcol)
    y_ssd = ssd_out_fwd(yc3, p3, P["ssd_norm"][l][None])
    cqn = rms_fwd(p3, 512, OFF_CQ // 512, Q_RANK, P["q_norm"][l][None], "rms_q_fwd")
    ckvn = rms_fwd(p3, KV_RANK, OFF_CKV // KV_RANK, KV_RANK, P["kv_norm"][l][None], "rms_kv_fwd")
    q3 = mm(cqn.reshape(T, Q_RANK), W["w_uq"], "nn", "mm_uq", out_dtype=bf16,
            rope=(cosf.reshape(T, 128), sinf.reshape(T, 128))).reshape(B, S, -1)
    k3 = k_proj(ckvn, W["w_ukv"], p3, cosf, sinf)
    v3 = mm(ckvn.reshape(T, KV_RANK), W["w_ukv"][:, MLA_H * NOPE:], "nn", "mm_uv", out_dtype=bf16).reshape(B, S, -1)
    o3, lse, comm_out = attn_fwd(q3, k3, v3, comm)
    if late is not None:
        W = dict(W, **late(comm_out))
    y_att = rms_fwd(o3, D, 0, D, P["attn_norm"][l][None], "rms_o_fwd")
    cat = (y_ssd.reshape(T, D), y_att.reshape(T, D))
    x1, y1 = mm(cat, W["w_out"], "nn", "mm_out", resid=x3.reshape(T, D), gate=mod[:, 2:3, :], seq=S)
    x13 = x1.reshape(B, S, D)
    h2 = normmod_fwd(x13, mod, P["norm_mlp"][l][None], 3, 4)
    u, up_out = mm(h2.reshape(T, D), W["w_up"], "nn", "mm_up", out_dtype=bf16, comm=comm_up), []
    if comm_up is not None:
        u, up_out = u
    u3 = u.reshape(B, S, 2 * D_FF)
    a, ffg, ffv = glu_fwd(u3, W["conv_ff_w"], P["conv_ff_b"][l][None])
    x2, y2 = mm(a.reshape(T, D_FF), W["w_down"], "nn", "mm_down", resid=x1, gate=mod[:, 5:6, :], seq=S)
    sv.update(x=x3, h=h, p3=p3, xc3=xc3, xpre=xpre, yc3=yc3, states=states, cqn=cqn, ckvn=ckvn, q3=q3, k3=k3, v3=v3, o3=o3, lse=lse,
              cat=cat, y1=y1, x1=x13, h2=h2, u3=u3, ffg=ffg, ffv=ffv, a=a, y2=y2, bias_row=bias_row, alog_row=alog_row, dcol=dcol)
    return x2.reshape(B, S, D), sv, comm_out, W, up_out


def layer_bwd(dx3, sv, mod, W, P, l, cosf, sinf, comm=None, send_rest=False):
    B, S, _ = dx3.shape
    T = B * S
    g = {}
    dy2, dg2 = gate_bwd(dx3, sv["y2"].reshape(B, S, D), mod, 5)
    dy2 = dy2.reshape(T, D)
    da = mm(dy2, W["w_down"], "nt", "mm_down_dx", out_dtype=bf16)
    g["w_down"] = mm(sv["a"].reshape(T, D_FF), dy2, "tn", "mm_down_dw")
    dug, duv, dwg, dwv, dbg, dbv = glu_bwd(sv["u3"], sv["ffg"], sv["ffv"], da.reshape(B, S, D_FF), W["conv_ff_w"])
    g["conv_ff_w"] = jnp.concatenate([dwg, dwv], axis=1)
    g["conv_ff_b"] = jnp.concatenate([dbg, dbv], axis=1)[0]
    du = (dug.reshape(T, D_FF), duv.reshape(T, D_FF))
    dh2 = mm(du, W["w_up"], "nt", "mm_up_dx", out_dtype=bf16)
    g["w_up"] = jnp.concatenate([mm(sv["h2"].reshape(T, D), d, "tn", "mm_up_dw") for d in du], axis=1)
    dx1, dsh2, dsc2, dnm, dy1, dg1 = normmod_bwd(sv["x1"], dh2.reshape(B, S, D), dx3, mod, P["norm_mlp"][l][None], 4,
                                                 y3=sv["y1"].reshape(B, S, D), i_g=2)
    g["norm_mlp"] = dnm[0]
    dy1 = dy1.reshape(T, D)
    dcat = mm(dy1, W["w_out"], "nt", "mm_out_dx", out_dtype=bf16)
    g["w_out"] = jnp.concatenate([mm(part, dy1, "tn", "mm_out_dw") for part in sv["cat"]], axis=0)
    dcat3 = dcat.reshape(B, S, 2 * D)
    dyc3, dp, dsn = ssd_out_bwd(sv["yc3"], sv["p3"], dcat3, P["ssd_norm"][l][None])
    g["ssd_norm"] = dsn[0]
    do3, dan, delta = rms_o_bwd(sv["o3"], dcat3, P["attn_norm"][l][None])
    g["attn_norm"] = dan[0]
    dqraw, dkn3, dv3, dkr3, comm_out = attn_bwd(sv["q3"], sv["k3"], sv["v3"], do3, sv["lse"], delta.reshape(B, MLA_H, 1, S),
                                                cosf, sinf, comm)
    dqraw = dqraw.reshape(T, -1)
    dcqn = mm(dqraw, W["w_uq"], "nt", "mm_uq_dx")
    g["w_uq"] = mm(sv["cqn"].reshape(T, Q_RANK), dqraw, "tn", "mm_uq_dw")
    dp, dqn = rms_bwd_into(sv["p3"], 512, OFF_CQ // 512, Q_RANK, dcqn.reshape(B, S, Q_RANK), P["q_norm"][l][None], dp, "rms_q_bwd")
    g["q_norm"] = dqn[0]
    bias_col, alog_col = sv["bias_row"].reshape(128, 1), sv["alog_row"].reshape(128, 1)
    comm_rest = (layer_grad_slices(g, REST), True) if send_rest else None
    dxc3, ddt3, dalog, dd, dbias, rest_out = ssd_bwd(sv["xc3"], sv["p3"], dyc3, sv["states"], sv["bias_row"], sv["alog_row"],
                                                     bias_col, alog_col, sv["dcol"], comm_rest)
    heads = slice(DT_LANE, DT_LANE + SSD_HEADS)
    g["a_log"], g["d_skip"], g["dt_bias"] = dalog[heads, 0], dd[heads, 0], dbias[heads, 0]
    dp, dcw, dcb = conv_ssd_bwd(sv["p3"], sv["xpre"], dxc3, W["conv_w"], dp)
    g["conv_w"], g["conv_b"] = dcw, dcb[0]
    dp = kprep_bwd(dkr3, ddt3, cosf, sinf, dp)
    dkv = (dkn3.reshape(T, -1), dv3.reshape(T, -1))
    dckvn = mm(dkv, W["w_ukv"], "nt", "mm_ukv_dx")
    g["w_ukv"] = jnp.concatenate([mm(sv["ckvn"].reshape(T, KV_RANK), d, "tn", "mm_ukv_dw") for d in dkv], axis=1)
    dp, dkn = rms_bwd_into(sv["p3"], KV_RANK, OFF_CKV // KV_RANK, KV_RANK, dckvn.reshape(B, S, KV_RANK), P["kv_norm"][l][None], dp,
                           "rms_kv_bwd")
    g["kv_norm"] = dkn[0]
    dp = dp.reshape(T, IN_COLS)
    dh = mm(dp, W["w_in"], "nt", "mm_in_dx", out_dtype=bf16)
    g["w_in"] = mm(sv["h"].reshape(T, D), dp, "tn", "mm_in_dw")
    dx0, dsh1, dsc1, dnx = normmod_bwd(sv["x"], dh.reshape(B, S, D), dx1, mod, P["norm_mix"][l][None], 1)
    g["norm_mix"] = dnx[0]
    dmod = jnp.concatenate([dsh1, dsc1, dg1, dsh2, dsc2, dg2], axis=1)
    return dx0, dmod, g, comm_out, rest_out


def kernel(x, c, positions, w_ada, b_ada, norm_mix, w_in, conv_w, conv_b, dt_bias, a_log, d_skip, ssd_norm, q_norm, w_uq, kv_norm, w_ukv, attn_norm, w_out, norm_mlp, w_up, conv_ff_w, conv_ff_b, w_down, final_norm, loss_target, m_w_ada, m_b_ada, m_norm_mix, m_w_in, m_conv_w, m_conv_b, m_dt_bias, m_a_log, m_d_skip, m_ssd_norm, m_q_norm, m_w_uq, m_kv_norm, m_w_ukv, m_attn_norm, m_w_out, m_norm_mlp, m_w_up, m_conv_ff_w, m_conv_ff_b, m_w_down, m_final_norm, v_w_ada, v_b_ada, v_norm_mix, v_w_in, v_conv_w, v_conv_b, v_dt_bias, v_a_log, v_d_skip, v_ssd_norm, v_q_norm, v_w_uq, v_kv_norm, v_w_ukv, v_attn_norm, v_w_out, v_norm_mlp, v_w_up, v_conv_ff_w, v_conv_ff_b, v_w_down, v_final_norm):
    given = dict(locals())
    B, S, _ = x.shape
    me = 4 * lax.axis_index("x") + 2 * lax.axis_index("y") + lax.axis_index("c")
    P = {n: given[n] for n in SMALL}

    def shards(l, entries):
        return [given[n][l] if n in CONVS else given[n][l].astype(bf16) for n, _ in entries]

    *gathered, c_all = exchange(shards(0, FIRST) + [c], False, "gather_weights")
    W = [layer_weights(gathered, FIRST), None]

    n_ada = w_ada.shape[2]
    c_all = c_all.reshape(N_DEV * B, D)
    b_sh = lax.dynamic_slice_in_dim(b_ada, me * n_ada, n_ada, axis=1)
    mod_sh = jnp.stack([ada_fwd(c_all, w_ada[l], b_sh[l][None]) for l in range(DEPTH)])
    (mod_g,) = exchange([mod_sh], False, "gather_mod")
    mod_mine = lax.dynamic_slice_in_dim(mod_g, me * B, B, axis=2)
    mods = mod_mine.transpose(1, 2, 0, 3).reshape(DEPTH, B, 6, D)

    inv_freq = jnp.asarray(1.0 / (ROPE_BASE ** (np.arange(0, ROPE, 2, dtype=np.float32) / ROPE)))
    ang = positions.astype(f32)[..., None] * inv_freq
    zeros = jnp.zeros((B, S, 128 - ROPE), f32)
    cosf = jnp.concatenate([jnp.cos(ang), jnp.cos(ang), zeros], axis=-1)
    sinf = jnp.concatenate([jnp.sin(ang), jnp.sin(ang), zeros], axis=-1)

    saved = [None] * DEPTH
    late = lambda got: layer_weights(got, REST)
    xl, saved[0], _, W[0], gathered = layer_fwd(x, mods[0], W[0], P, 0, cosf, sinf, comm=(shards(0, REST), False), late=late,
                                                comm_up=(shards(1, FIRST), False))
    xl, saved[1], _, W[1], _ = layer_fwd(xl, mods[1], layer_weights(gathered, FIRST), P, 1, cosf, sinf,
                                         comm=(shards(1, REST), False), late=late)
    dxl, d_final, loss_part = final_loss(xl, final_norm[None], loss_target)
    grads, dmods, recv = [None] * DEPTH, [None] * DEPTH, [None] * DEPTH
    dxl, dmods[1], grads[1], _, _ = layer_bwd(dxl, saved[1], mods[1], W[1], P, 1, cosf, sinf)
    grad_x, dmods[0], grads[0], recv[1], recv_rest = layer_bwd(dxl, saved[0], mods[0], W[0], P, 0, cosf, sinf,
                                                               comm=(layer_grad_slices(grads[1], BIG), True), send_rest=True)

    stack = lambda n: jnp.stack([grads[l][n] for l in range(DEPTH)])
    small_names = [n for n in SMALL if n not in ("b_ada", "final_norm")]
    partial = pack([stack(n) for n in small_names] + [d_final[0], loss_part[0]])
    dmod_all = jnp.stack(dmods)
    *recv_first, part_g, dmod_g = exchange(layer_grad_slices(grads[0], FIRST) + [partial, dmod_all],
                                           [True] * len(FIRST) + [False, False], "exchange_tail")
    recv[0] = recv_first + recv_rest
    big_g = [jnp.concatenate([sum_slots(recv[l][i][:, None]) for l in range(DEPTH)]) for i in range(len(BIG))]
    small_sum = sum_leading(part_g, "sum_partials")
    small_g = unpack(small_sum, [given[n].shape for n in small_names] + [(D,), (128,)])
    gsmall = dict(zip(small_names + ["final_norm"], small_g[:-1]))
    loss = small_g[-1][0]
    dmod_rows = dmod_g.transpose(0, 2, 1, 3, 4).reshape(N_DEV * B, DEPTH * 6 * D)
    gsmall["b_ada"] = sum_leading(dmod_rows.reshape(N_DEV * B, -1, 128), "sum_b_ada").reshape(DEPTH, 6 * D)
    dmod_cols = dmod_rows.reshape(N_DEV * B, DEPTH, N_DEV, n_ada)
    dmod_sh = lax.dynamic_slice_in_dim(dmod_cols, me, 1, axis=2)[:, :, 0, :]
    g_w_ada = jnp.stack([ada_bwd(c_all, dmod_sh[:, l, :]) for l in range(DEPTH)])

    res = {"grad": {}, "delta": {}, "new_m": {}, "new_v": {}}
    for n, gv in zip([n for n, _ in BIG] + ["w_ada"], big_g + [g_w_ada]):
        res["grad"][n] = gv
        res["delta"][n], res["new_m"][n], res["new_v"][n] = adamw_nd(given[n], gv, given["m_" + n], given["v_" + n])
    shapes = [given[n].shape for n in SMALL]
    flat = adamw(pack([given[n] for n in SMALL]), pack([gsmall[n] for n in SMALL]), pack([given["m_" + n] for n in SMALL]),
                 pack([given["v_" + n] for n in SMALL]))
    for n in SMALL:
        res["grad"][n] = gsmall[n]
    for key, arr in zip(("delta", "new_m", "new_v"), flat):
        res[key].update(zip(SMALL, unpack(arr, shapes)))
    order = ["w_ada", "b_ada", "norm_mix", "w_in", "conv_w", "conv_b", "dt_bias", "a_log", "d_skip", "ssd_norm", "q_norm", "w_uq",
             "kv_norm", "w_ukv", "attn_norm", "w_out", "norm_mlp", "w_up", "conv_ff_w", "conv_ff_b", "w_down", "final_norm"]
    return (loss, grad_x, *[res[k][n] for k in ("grad", "delta", "new_m", "new_v") for n in order])
```

```python
import functools

import numpy as np
import jax
import jax.numpy as jnp
from jax import lax
from jax.experimental import pallas as pl
from jax.experimental.pallas import tpu as pltpu

f32, bf16 = jnp.float32, jnp.bfloat16
HIGHEST = lax.Precision.HIGHEST

D = 1024
D_SSD = 1024
SSD_HEADS = 16
SSD_HD = 64
SSD_N = 128
CHUNK = 128
D_XBC = 1536
CONV_K = 4
MLA_H = 8
NOPE = 128
ROPE = 64
VD = 128
QK = NOPE + ROPE
Q_RANK = 384
KV_RANK = 256
D_FF = 2816
FF_K = 3
EPS = 1e-6
ROPE_BASE = 10000.0
DEPTH = 2
ADAM_LR, ADAM_B1, ADAM_B2, ADAM_EPS, ADAM_WD, ADAM_STEP = 0.001, 0.9, 0.999, 1e-08, 0.01, 10

N_DEV = 8
IN_COLS = 3456
OFF_XBC, OFF_CQ, OFF_CKV, OFF_KRDT = 1024, 2560, 3072, 3328
DT_LANE = 64
VMEM_LIMIT = 48 * 1024 * 1024
MM_K_WHOLE = 4096


def _cparams(n_grid):
    return pltpu.CompilerParams(dimension_semantics=("arbitrary",) * n_grid, vmem_limit_bytes=VMEM_LIMIT)


def _pick(n, cands):
    for c in cands:
        if n % c == 0:
            return c
    return n


def _silu(x):
    return x * jax.nn.sigmoid(x)


def _dsilu(x):
    s = jax.nn.sigmoid(x)
    return s * (1.0 + x * (1.0 - s))


def _rowsum(x):
    return jnp.sum(x, axis=0, keepdims=True)


def mm(a, b, mode, name, out_dtype=f32, resid=None, gate=None, seq=None, comm=None, rope=None):
    parts = list(a) if isinstance(a, (tuple, list)) else [a]
    np_ = len(parts)
    if mode == "nn":
        (M, Kp), N = parts[0].shape, b.shape[1]
    elif mode == "nt":
        (M, Kp), N = parts[0].shape, b.shape[0]
    else:
        (Kp, M), N = parts[0].shape, b.shape[1]
    K = Kp * np_
    gated = resid is not None
    whole = np_ > 1 or K <= MM_K_WHOLE
    tm = _pick(seq if gated else M, (1024, 1408, 512, 384, 256, 128) if K <= MM_K_WHOLE else (512, 256, 128))
    tn = _pick(N, (512, 1408, 384, 256, 128))
    tk = K if whole else _pick(K, (2816, 2048, 1024, 512))
    if mode == "tn":
        tm = _pick(M, (1024, 1408, 512, 384, 256, 128))
    nk = K // tk
    dims = {"nn": ((1,), (0,)), "nt": ((1,), (1,)), "tn": ((0,), (0,))}[mode]

    def body(*refs):
        a_refs, b_ref, rest = refs[:np_], refs[np_], refs[np_ + 1:]
        if rope is not None:
            cos_ref, sin_ref, rest = rest[0], rest[1], rest[2:]
        if gated:
            r_ref, g_ref, o_ref, y_ref, acc = rest
        else:
            o_ref, acc = rest

        def finish(res):
            if gated:
                y_ref[...] = res
                o_ref[...] = r_ref[...] + g_ref[0] * res
            elif rope is not None:
                for h in range(tn // 256):
                    lo = h * 256
                    o_ref[:, lo:lo + NOPE] = (res[:, lo:lo + NOPE] * Q_FOLD).astype(out_dtype)
                    o_ref[:, lo + NOPE:lo + 256] = (_rope(res[:, lo + NOPE:lo + 256], cos_ref[...], sin_ref[...]) * Q_FOLD).astype(out_dtype)
            else:
                o_ref[...] = res.astype(out_dtype)

        prod = None
        for p, a_ref in enumerate(a_refs):
            if np_ == 1:
                bv = b_ref[...]
            else:
                bv = b_ref[:, p * Kp:(p + 1) * Kp] if mode == "nt" else b_ref[p * Kp:(p + 1) * Kp, :]
            term = lax.dot_general(a_ref[...].astype(bf16), bv.astype(bf16), (dims, ((), ())), preferred_element_type=f32)
            prod = term if prod is None else prod + term
        if nk == 1:
            finish(prod)
        else:
            k = pl.program_id(2)

            @pl.when(k == 0)
            def _():
                acc[...] = prod

            @pl.when(k > 0)
            def _():
                acc[...] += prod

            @pl.when(k == nk - 1)
            def _():
                finish(acc[...])

    if np_ > 1:
        a_spec = pl.BlockSpec((tm, Kp), lambda i, j, k: (i, 0))
    elif mode == "tn":
        a_spec = pl.BlockSpec((tk, tm), lambda i, j, k: (k, i))
    else:
        a_spec = pl.BlockSpec((tm, tk), lambda i, j, k: (i, k))
    b_spec = pl.BlockSpec((tn, tk), lambda i, j, k: (j, k)) if mode == "nt" else pl.BlockSpec((tk, tn), lambda i, j, k: (k, j))
    o_spec = pl.BlockSpec((tm, tn), lambda i, j, k: (i, j))
    in_specs, args = [a_spec] * np_ + [b_spec], parts + [b]
    if rope is not None:
        in_specs += [pl.BlockSpec((tm, 128), lambda i, j, k: (i, 0))] * 2
        args += list(rope)
    out_specs, out_shape = [o_spec], [jax.ShapeDtypeStruct((M, N), out_dtype)]
    if gated:
        per = seq // tm
        in_specs += [o_spec, pl.BlockSpec((1, 1, tn), lambda i, j, k: (i // per, 0, j))]
        args += [resid, gate]
        out_specs = [o_spec, o_spec]
        out_shape = [jax.ShapeDtypeStruct((M, N), f32), jax.ShapeDtypeStruct((M, N), f32)]
    grid = (M // tm, N // tn, nk)
    body, c_args, c_in, c_out, c_shapes, c_sems = _fuse_exchange(body, comm, len(args), len(out_specs), 1, grid)
    res = pl.pallas_call(body, name=name + "_x" if comm else name, grid=grid, in_specs=in_specs + c_in, out_specs=out_specs + c_out,
                         out_shape=out_shape + c_shapes, scratch_shapes=[pltpu.VMEM((tm, tn), f32)] + c_sems,
                         compiler_params=_cparams(3))(*args, *c_args)
    own = res[:len(out_specs)]
    own = own[0] if len(own) == 1 else tuple(own)
    return (own, list(res[len(out_specs):])) if comm else own


def _tok(ts, width, cb=0):
    return pl.BlockSpec((1, ts, width), lambda b, s: (b, s, cb))


def _perb(rows, width):
    return pl.BlockSpec((1, rows, width), lambda b, s: (b, 0, 0))


def _const(rows, width):
    return pl.BlockSpec((rows, width), lambda b, s: (0, 0))


def _row_call(body, name, B, S, ts, in_specs, out_specs, out_shape, scratch=(), aliases=None):
    return pl.pallas_call(body, name=name, grid=(B, S // ts), in_specs=in_specs, out_specs=out_specs,
                          out_shape=out_shape, scratch_shapes=list(scratch), input_output_aliases=aliases or {},
                          compiler_params=_cparams(2))


def _first():
    return (pl.program_id(0) == 0) & (pl.program_id(1) == 0)


def normmod_fwd(x3, mod, g, i_sh, i_sc):
    B, S, C = x3.shape
    ts = _pick(S, (512, 256, 128))

    def body(x_ref, mod_ref, g_ref, h_ref):
        x = x_ref[0]
        r = lax.rsqrt(jnp.mean(x * x, axis=-1, keepdims=True) + EPS)
        n = x * r * g_ref[...]
        h_ref[0] = (n * (1.0 + mod_ref[0, i_sc:i_sc + 1, :]) + mod_ref[0, i_sh:i_sh + 1, :]).astype(bf16)

    return _row_call(body, "normmod_fwd", B, S, ts, [_tok(ts, C), _perb(6, C), _const(1, C)], _tok(ts, C),
                     jax.ShapeDtypeStruct((B, S, C), bf16))(x3, mod, g)


def normmod_bwd(x3, dh3, resid3, mod, g, i_sc, y3=None, i_g=None):
    B, S, C = x3.shape
    ts = _pick(S, (512, 256, 128))
    gated = y3 is not None

    def body(x_ref, dh_ref, r_ref, mod_ref, g_ref, *rest):
        if gated:
            y_ref, dx_ref, dsh_ref, dsc_ref, dg_ref, dy_ref, dgate_ref = rest
        else:
            dx_ref, dsh_ref, dsc_ref, dg_ref = rest

        @pl.when(pl.program_id(1) == 0)
        def _():
            dsh_ref[...] = jnp.zeros_like(dsh_ref)
            dsc_ref[...] = jnp.zeros_like(dsc_ref)
            if gated:
                dgate_ref[...] = jnp.zeros_like(dgate_ref)

        @pl.when(_first())
        def _():
            dg_ref[...] = jnp.zeros_like(dg_ref)

        x, dh, gv = x_ref[0], dh_ref[0].astype(f32), g_ref[...]
        r = lax.rsqrt(jnp.mean(x * x, axis=-1, keepdims=True) + EPS)
        xh = x * r
        dn = dh * (1.0 + mod_ref[0, i_sc:i_sc + 1, :])
        dsh_ref[0] += _rowsum(dh)
        dsc_ref[0] += _rowsum(dh * xh * gv)
        dg_ref[...] += _rowsum(dn * xh)
        dxh = dn * gv
        dx = r * (dxh - xh * jnp.mean(dxh * xh, axis=-1, keepdims=True)) + r_ref[0]
        dx_ref[0] = dx
        if gated:
            dy_ref[0] = (dx * mod_ref[0, i_g:i_g + 1, :]).astype(bf16)
            dgate_ref[0] += _rowsum(dx * y_ref[0])

    in_specs = [_tok(ts, C), _tok(ts, C), _tok(ts, C), _perb(6, C), _const(1, C)]
    out_specs = [_tok(ts, C), _perb(1, C), _perb(1, C), _const(1, C)]
    out_shape = [jax.ShapeDtypeStruct((B, S, C), f32), jax.ShapeDtypeStruct((B, 1, C), f32),
                 jax.ShapeDtypeStruct((B, 1, C), f32), jax.ShapeDtypeStruct((1, C), f32)]
    args = [x3, dh3, resid3, mod, g]
    if gated:
        in_specs.append(_tok(ts, C))
        args.append(y3)
        out_specs += [_tok(ts, C), _perb(1, C)]
        out_shape += [jax.ShapeDtypeStruct((B, S, C), bf16), jax.ShapeDtypeStruct((B, 1, C), f32)]
    return _row_call(body, "normmod_gate_bwd" if gated else "normmod_bwd", B, S, ts, in_specs, out_specs, out_shape)(*args)


def gate_bwd(dx3, y3, mod, i_g):
    B, S, C = dx3.shape
    ts = _pick(S, (512, 256, 128))

    def body(dx_ref, y_ref, mod_ref, dy_ref, dgate_ref):
        @pl.when(pl.program_id(1) == 0)
        def _():
            dgate_ref[...] = jnp.zeros_like(dgate_ref)

        dx = dx_ref[0]
        dy_ref[0] = (dx * mod_ref[0, i_g:i_g + 1, :]).astype(bf16)
        dgate_ref[0] += _rowsum(dx * y_ref[0])

    return _row_call(body, "gate_bwd", B, S, ts, [_tok(ts, C), _tok(ts, C), _perb(6, C)], [_tok(ts, C), _perb(1, C)],
                     [jax.ShapeDtypeStruct((B, S, C), bf16), jax.ShapeDtypeStruct((B, 1, C), f32)])(dx3, y3, mod)


def rms_fwd(src3, width, cb, n, g, name):
    B, S, _ = src3.shape
    ts = _pick(S, (512, 256, 128))

    def body(x_ref, g_ref, o_ref):
        x = x_ref[0][:, :n]
        r = lax.rsqrt(jnp.mean(x * x, axis=-1, keepdims=True) + EPS)
        o_ref[0] = (x * r * g_ref[...]).astype(bf16)

    return _row_call(body, name, B, S, ts, [_tok(ts, width, cb), _const(1, n)], _tok(ts, n),
                     jax.ShapeDtypeStruct((B, S, n), bf16))(src3, g)


def rms_bwd_into(src3, width, cb, n, dout3, g, dp, name):
    B, S, _ = src3.shape
    ts = _pick(S, (512, 256, 128))

    def body(x_ref, do_ref, g_ref, dp_in, dp_ref, dg_ref):
        @pl.when(_first())
        def _():
            dg_ref[...] = jnp.zeros_like(dg_ref)

        x = x_ref[0][:, :n]
        do = do_ref[0]
        r = lax.rsqrt(jnp.mean(x * x, axis=-1, keepdims=True) + EPS)
        xh = x * r
        dg_ref[...] += _rowsum(do * xh)
        dxh = do * g_ref[...]
        dp_ref[0, :, :n] = (r * (dxh - xh * jnp.mean(dxh * xh, axis=-1, keepdims=True))).astype(bf16)
        if width > n:
            dp_ref[0, :, n:] = jnp.zeros((ts, width - n), bf16)

    return _row_call(body, name, B, S, ts, [_tok(ts, width, cb), _tok(ts, n), _const(1, n), pl.BlockSpec(memory_space=pl.ANY)],
                     [_tok(ts, width, cb), _const(1, n)], [jax.ShapeDtypeStruct(dp.shape, bf16), jax.ShapeDtypeStruct((1, n), f32)],
                     aliases={3: 0})(src3, dout3, g, dp)


def final_loss(x3, g, tgt3):
    B, S, C = x3.shape
    ts = _pick(S, (512, 256, 128))

    def body(x_ref, g_ref, t_ref, dx_ref, dg_ref, loss_ref):
        @pl.when(_first())
        def _():
            dg_ref[...] = jnp.zeros_like(dg_ref)
            loss_ref[...] = jnp.zeros_like(loss_ref)

        x, gv = x_ref[0], g_ref[...]
        r = lax.rsqrt(jnp.mean(x * x, axis=-1, keepdims=True) + EPS)
        xh = x * r
        e = xh * gv - t_ref[0]
        loss_ref[...] += 0.5 * jnp.sum(e * e) / C
        dout = e / C
        dg_ref[...] += _rowsum(dout * xh)
        dxh = dout * gv
        dx_ref[0] = r * (dxh - xh * jnp.mean(dxh * xh, axis=-1, keepdims=True))

    return _row_call(body, "final_loss", B, S, ts, [_tok(ts, C), _const(1, C), _tok(ts, C)],
                     [_tok(ts, C), _const(1, C), _const(1, 128)],
                     [jax.ShapeDtypeStruct((B, S, C), f32), jax.ShapeDtypeStruct((1, C), f32),
                      jax.ShapeDtypeStruct((1, 128), f32)])(x3, g, tgt3)


def ssd_out_fwd(yc3, p3, w):
    B, S, C = yc3.shape
    ts = _pick(S, (512, 256, 128))
    half = C // 2

    def body(y_ref, z_ref, w_ref, o_ref):
        y = y_ref[0] * _silu(z_ref[0])
        for lo in (0, half):
            yg = y[:, lo:lo + half]
            r = lax.rsqrt(jnp.mean(yg * yg, axis=-1, keepdims=True) + EPS)
            o_ref[0, :, lo:lo + half] = (yg * r * w_ref[:, lo:lo + half]).astype(bf16)

    return _row_call(body, "ssd_out_fwd", B, S, ts, [_tok(ts, C), _tok(ts, C, 0), _const(1, C)], _tok(ts, C),
                     jax.ShapeDtypeStruct((B, S, C), bf16))(yc3, p3, w)


def ssd_out_bwd(yc3, p3, dcat3, w):
    B, S, C = yc3.shape
    ts = _pick(S, (512, 256, 128))
    half = C // 2

    def body(y_ref, z_ref, do_ref, w_ref, dyc_ref, dz_ref, dw_ref):
        @pl.when(_first())
        def _():
            dw_ref[...] = jnp.zeros_like(dw_ref)

        yc, z, do = y_ref[0], z_ref[0], do_ref[0].astype(f32)
        sz = _silu(z)
        y = yc * sz
        for lo in (0, half):
            sl = slice(lo, lo + half)
            yg, dog, wg = y[:, sl], do[:, sl], w_ref[:, sl]
            r = lax.rsqrt(jnp.mean(yg * yg, axis=-1, keepdims=True) + EPS)
            yh = yg * r
            dw_ref[:, sl] += _rowsum(dog * yh)
            dyh = dog * wg
            dy = r * (dyh - yh * jnp.mean(dyh * yh, axis=-1, keepdims=True))
            dyc_ref[0, :, sl] = dy * sz[:, sl]
            dz_ref[0, :, sl] = (dy * yc[:, sl] * _dsilu(z[:, sl])).astype(bf16)

    return _row_call(body, "ssd_out_bwd", B, S, ts, [_tok(ts, C), _tok(ts, C, 0), _tok(ts, C, 0), _const(1, C)],
                     [_tok(ts, C), _tok(ts, C, 0), _const(1, C)],
                     [jax.ShapeDtypeStruct((B, S, C), f32), jax.ShapeDtypeStruct((B, S, IN_COLS), bf16),
                      jax.ShapeDtypeStruct((1, C), f32)])(yc3, p3, dcat3, w)


def _rot(t):
    lane = lax.broadcasted_iota(jnp.int32, t.shape, 1)
    return jnp.where(lane < ROPE // 2, -pltpu.roll(t, 128 - ROPE // 2, 1), pltpu.roll(t, ROPE // 2, 1))


def _rope(t, cosf, sinf):
    return t * cosf + _rot(t) * sinf


def _rope_t(d, cosf, sinf):
    return d * cosf - _rot(d * sinf)


def k_proj(ckvn, w, p3, cosf, sinf):
    B, S, _ = ckvn.shape
    ts = _pick(S, (1024, 512, 256, 128))

    def body(x_ref, w_ref, kr_ref, c_ref, s_ref, o_ref):
        lane = lax.broadcasted_iota(jnp.int32, (1, 128), 1)
        kr = _rope(jnp.where(lane < ROPE, kr_ref[0], 0.0), c_ref[0], s_ref[0]).astype(bf16)
        res = jnp.dot(x_ref[0], w_ref[...], preferred_element_type=f32)
        for h in range(2):
            o_ref[0, :, h * 256:h * 256 + NOPE] = res[:, h * NOPE:(h + 1) * NOPE].astype(bf16)
            o_ref[0, :, h * 256 + NOPE:(h + 1) * 256] = kr

    return pl.pallas_call(body, name="k_proj", grid=(B, S // ts, MLA_H // 2),
                          in_specs=[pl.BlockSpec((1, ts, KV_RANK), lambda b, s, j: (b, s, 0)),
                                    pl.BlockSpec((KV_RANK, 2 * NOPE), lambda b, s, j: (0, j)),
                                    pl.BlockSpec((1, ts, 128), lambda b, s, j: (b, s, OFF_KRDT // 128)),
                                    pl.BlockSpec((1, ts, 128), lambda b, s, j: (b, s, 0)),
                                    pl.BlockSpec((1, ts, 128), lambda b, s, j: (b, s, 0))],
                          out_specs=pl.BlockSpec((1, ts, 512), lambda b, s, j: (b, s, j)),
                          out_shape=jax.ShapeDtypeStruct((B, S, MLA_H * 256), bf16), compiler_params=_cparams(3))(ckvn, w, p3, cosf, sinf)


def kprep_bwd(dkr3, ddt3, cosf, sinf, dp):
    B, S, W = dkr3.shape
    ts = _pick(S, (512, 256, 128))

    def body(dk_ref, ddt_ref, c_ref, s_ref, dp_in, kr_ref):
        acc = dk_ref[0, :, 0:128]
        for h in range(1, W // 128):
            acc = acc + dk_ref[0, :, h * 128:(h + 1) * 128]
        lane = lax.broadcasted_iota(jnp.int32, (1, 128), 1)
        kr_ref[0] = jnp.where(lane < ROPE, _rope_t(acc, c_ref[0], s_ref[0]), ddt_ref[0]).astype(bf16)

    return _row_call(body, "kprep_bwd", B, S, ts,
                     [_tok(ts, W), _tok(ts, 128), _tok(ts, 128), _tok(ts, 128), pl.BlockSpec(memory_space=pl.ANY)],
                     _tok(ts, 128, OFF_KRDT // 128), jax.ShapeDtypeStruct(dp.shape, bf16), aliases={4: 0})(dkr3, ddt3, cosf, sinf, dp)


def _shift_down(u, j):
    if j == 0:
        return u
    row = lax.broadcasted_iota(jnp.int32, u.shape, 0)
    return jnp.where(row < j, 0.0, pltpu.roll(u, j, 0))


def _shift_up(u, j):
    if j == 0:
        return u
    n = u.shape[0]
    row = lax.broadcasted_iota(jnp.int32, u.shape, 0)
    return jnp.where(row >= n - j, 0.0, pltpu.roll(u, n - j, 0))


def _conv(u, w, b, K):
    out = b
    for j in range(K):
        out = out + w[K - 1 - j:K - j, :] * _shift_down(u, j)
    return out


def _conv_bwd(u, du, w, K):
    dins = w[K - 1:K, :] * du
    dws = [None] * K
    dws[K - 1] = _rowsum(du * u)
    for j in range(1, K):
        sd = _shift_up(du, j)
        dins = dins + w[K - 1 - j:K - j, :] * sd
        dws[K - 1 - j] = _rowsum(sd * u)
    return dins, dws


CW = 256


def conv_ssd_fwd(p3, w, b):
    B, S, _ = p3.shape
    nb = D_XBC // CW

    def body(u_ref, w_ref, b_ref, o_ref, pre_ref):
        pre = _conv(u_ref[0], w_ref[...], b_ref[...], CONV_K)
        o_ref[0] = _silu(pre)
        pre_ref[0] = pre.astype(bf16)

    out = pl.BlockSpec((1, S, CW), lambda b, j: (b, 0, j))
    return pl.pallas_call(body, name="conv_ssd_fwd", grid=(B, nb),
                          in_specs=[pl.BlockSpec((1, S, CW), lambda b, j: (b, 0, OFF_XBC // CW + j)),
                                    pl.BlockSpec((CONV_K, CW), lambda b, j: (0, j)),
                                    pl.BlockSpec((1, CW), lambda b, j: (0, j))],
                          out_specs=[out, out],
                          out_shape=[jax.ShapeDtypeStruct((B, S, D_XBC), f32), jax.ShapeDtypeStruct((B, S, D_XBC), bf16)],
                          compiler_params=_cparams(2))(p3, w, b)


def conv_ssd_bwd(p3, pre3, dxc3, w, dp):
    B, S, _ = p3.shape
    nb = D_XBC // CW

    def body(u_ref, pre_ref, d_ref, w_ref, dp_in, du_ref, dw_ref, db_ref):
        @pl.when(pl.program_id(1) == 0)
        def _():
            dw_ref[...] = jnp.zeros_like(dw_ref)
            db_ref[...] = jnp.zeros_like(db_ref)

        u, wv = u_ref[0], w_ref[...]
        dpre = d_ref[0] * _dsilu(pre_ref[0].astype(f32))
        dins, dws = _conv_bwd(u, dpre, wv, CONV_K)
        du_ref[0] = dins.astype(bf16)
        for k in range(CONV_K):
            dw_ref[k:k + 1, :] += dws[k]
        db_ref[...] += _rowsum(dpre)

    return pl.pallas_call(body, name="conv_ssd_bwd", grid=(nb, B),
                          in_specs=[pl.BlockSpec((1, S, CW), lambda j, b: (b, 0, OFF_XBC // CW + j)),
                                    pl.BlockSpec((1, S, CW), lambda j, b: (b, 0, j)),
                                    pl.BlockSpec((1, S, CW), lambda j, b: (b, 0, j)),
                                    pl.BlockSpec((CONV_K, CW), lambda j, b: (0, j)), pl.BlockSpec(memory_space=pl.ANY)],
                          out_specs=[pl.BlockSpec((1, S, CW), lambda j, b: (b, 0, OFF_XBC // CW + j)),
                                     pl.BlockSpec((CONV_K, CW), lambda j, b: (0, j)),
                                     pl.BlockSpec((1, CW), lambda j, b: (0, j))],
                          out_shape=[jax.ShapeDtypeStruct(dp.shape, bf16), jax.ShapeDtypeStruct((CONV_K, D_XBC), f32),
                                     jax.ShapeDtypeStruct((1, D_XBC), f32)], input_output_aliases={4: 0},
                          compiler_params=_cparams(2))(p3, pre3, dxc3, w, dp)


def glu_fwd(u3, w, b):
    B, S, _ = u3.shape
    nb = D_FF // CW

    def body(ug_ref, uv_ref, wg_ref, wv_ref, bg_ref, bv_ref, o_ref, g_ref, v_ref):
        g = _conv(ug_ref[0].astype(f32), wg_ref[...], bg_ref[...], FF_K)
        v = _conv(uv_ref[0].astype(f32), wv_ref[...], bv_ref[...], FF_K)
        o_ref[0] = (_silu(g) * v).astype(bf16)
        g_ref[0] = g.astype(bf16)
        v_ref[0] = v.astype(bf16)

    def blk(off):
        return pl.BlockSpec((1, S, CW), lambda b, j: (b, 0, off + j))

    def par(rows, off):
        return pl.BlockSpec((rows, CW), lambda b, j: (0, off + j))

    shp = jax.ShapeDtypeStruct((B, S, D_FF), bf16)
    return pl.pallas_call(body, name="glu_fwd", grid=(B, nb),
                          in_specs=[blk(0), blk(nb), par(FF_K, 0), par(FF_K, nb), par(1, 0), par(1, nb)],
                          out_specs=[blk(0)] * 3, out_shape=[shp] * 3, compiler_params=_cparams(2))(u3, u3, w, w, b, b)


def glu_bwd(u3, g3, v3, da3, w):
    B, S, _ = u3.shape
    nb = D_FF // CW

    def body(ug_ref, uv_ref, g_ref, v_ref, da_ref, wg_ref, wv_ref, dug_ref, duv_ref, dwg_ref, dwv_ref, dbg_ref, dbv_ref):
        @pl.when(pl.program_id(1) == 0)
        def _():
            for r in (dwg_ref, dwv_ref, dbg_ref, dbv_ref):
                r[...] = jnp.zeros_like(r)

        ug, uv, da, wg, wv = ug_ref[0].astype(f32), uv_ref[0].astype(f32), da_ref[0].astype(f32), wg_ref[...], wv_ref[...]
        g, v = g_ref[0].astype(f32), v_ref[0].astype(f32)
        dg = da * v * _dsilu(g)
        dv = da * _silu(g)
        ding, dwsg = _conv_bwd(ug, dg, wg, FF_K)
        dinv, dwsv = _conv_bwd(uv, dv, wv, FF_K)
        dug_ref[0] = ding.astype(bf16)
        duv_ref[0] = dinv.astype(bf16)
        for k in range(FF_K):
            dwg_ref[k:k + 1, :] += dwsg[k]
            dwv_ref[k:k + 1, :] += dwsv[k]
        dbg_ref[...] += _rowsum(dg)
        dbv_ref[...] += _rowsum(dv)

    def blk(off):
        return pl.BlockSpec((1, S, CW), lambda j, b: (b, 0, off + j))

    def par(rows, off):
        return pl.BlockSpec((rows, CW), lambda j, b: (0, off + j))

    return pl.pallas_call(body, name="glu_bwd", grid=(nb, B),
                          in_specs=[blk(0), blk(nb), blk(0), blk(0), blk(0), par(FF_K, 0), par(FF_K, nb)],
                          out_specs=[blk(0), blk(0), par(FF_K, 0), par(FF_K, 0), par(1, 0), par(1, 0)],
                          out_shape=[jax.ShapeDtypeStruct((B, S, D_FF), bf16), jax.ShapeDtypeStruct((B, S, D_FF), bf16),
                                     jax.ShapeDtypeStruct((FF_K, D_FF), f32), jax.ShapeDtypeStruct((FF_K, D_FF), f32),
                                     jax.ShapeDtypeStruct((1, D_FF), f32), jax.ShapeDtypeStruct((1, D_FF), f32)],
                          compiler_params=_cparams(2))(u3, u3, g3, v3, da3, w, w)


def _ssd_decay(dtb, bias_row, alog_row):
    lane = lax.broadcasted_iota(jnp.int32, (1, 128), 1)
    hmask = (lane >= DT_LANE) & (lane < DT_LANE + SSD_HEADS)
    dt = jnp.where(hmask, jax.nn.softplus(dtb + bias_row), 0.0)
    a = dt * jnp.where(hmask, -jnp.exp(alog_row), 0.0)
    r = lax.broadcasted_iota(jnp.int32, (CHUNK, CHUNK), 0)
    c = lax.broadcasted_iota(jnp.int32, (CHUNK, CHUNK), 1)
    cs = jnp.dot((r >= c).astype(f32), a, precision=HIGHEST, preferred_element_type=f32)
    return dt, cs


def _expand(xt):
    return jnp.concatenate([jnp.broadcast_to(xt[DT_LANE + h:DT_LANE + h + 1, :], (SSD_HD, xt.shape[1]))
                            for h in range(SSD_HEADS)], axis=0)


_NT = (((1,), (1,)), ((), ()))
_TN = (((0,), (0,)), ((), ()))
GH = SSD_HEADS // 2
GR = GH * SSD_HD


def ssd_fwd(xc3, p3, bias_row, alog_row, dcol):
    B, S, _ = xc3.shape
    nc = S // CHUNK

    def body(xs_ref, bc_ref, dtb_ref, bias_ref, alog_ref, dcol_ref, y_ref, st_ref, state, yT):
        @pl.when(pl.program_id(1) == 0)
        def _():
            state[...] = jnp.zeros_like(state)

        dt, cs = _ssd_decay(dtb_ref[0], bias_ref[...], alog_ref[...])
        csT = cs.T
        eT = jnp.exp(csT)
        decX = _expand(jnp.exp(csT[:, CHUNK - 1:CHUNK] - csT))
        eX = _expand(eT)
        elastX = eX[:, CHUNK - 1:CHUNK]
        xsT = xs_ref[0].T
        uT = xsT * _expand(dt.T)
        bc = bc_ref[0]
        st_ref[0, 0] = state[...]
        srow = lax.broadcasted_iota(jnp.int32, (CHUNK, CHUNK), 0)
        lcol = lax.broadcasted_iota(jnp.int32, (CHUNK, CHUNK), 1)
        for g in range(2):
            Bg = bc[:, g * SSD_N:(g + 1) * SSD_N].astype(bf16)
            Cg = bc[:, (2 + g) * SSD_N:(3 + g) * SSD_N].astype(bf16)
            GT = lax.dot_general(Bg, Cg, _NT, preferred_element_type=f32)
            rows = slice(g * GR, (g + 1) * GR)
            Sg = state[rows]
            yoffT = lax.dot_general(Sg.astype(bf16), Cg, _NT, preferred_element_type=f32) * eX[rows]
            state[rows] = Sg * elastX[rows] + jnp.dot((uT[rows] * decX[rows]).astype(bf16), Bg, preferred_element_type=f32)
            for k in range(GH):
                h = g * GH + k
                hr = slice(h * SSD_HD, (h + 1) * SSD_HD)
                seg = csT[DT_LANE + h:DT_LANE + h + 1, :] - cs[:, DT_LANE + h:DT_LANE + h + 1]
                LT = jnp.where(lcol >= srow, jnp.exp(jnp.minimum(seg, 0.0)), 0.0)
                yT[hr] = (jnp.dot(uT[hr].astype(bf16), (GT * LT).astype(bf16), preferred_element_type=f32)
                          + yoffT[k * SSD_HD:(k + 1) * SSD_HD] + dcol_ref[hr] * xsT[hr])
        y_ref[0] = yT[...].T

    return pl.pallas_call(body, name="ssd_fwd", grid=(B, nc),
                          in_specs=[pl.BlockSpec((1, CHUNK, D_SSD), lambda b, c: (b, c, 0)),
                                    pl.BlockSpec((1, CHUNK, 512), lambda b, c: (b, c, 2)),
                                    pl.BlockSpec((1, CHUNK, 128), lambda b, c: (b, c, OFF_KRDT // 128)),
                                    _const(1, 128), _const(1, 128), _const(D_SSD, 1)],
                          out_specs=[pl.BlockSpec((1, CHUNK, D_SSD), lambda b, c: (b, c, 0)),
                                     pl.BlockSpec((1, 1, D_SSD, SSD_N), lambda b, c: (b, c, 0, 0))],
                          out_shape=[jax.ShapeDtypeStruct((B, S, D_SSD), f32), jax.ShapeDtypeStruct((B, nc, D_SSD, SSD_N), f32)],
                          scratch_shapes=[pltpu.VMEM((D_SSD, SSD_N), f32), pltpu.VMEM((D_SSD, CHUNK), f32)],
                          compiler_params=_cparams(2))(xc3, xc3, p3, bias_row, alog_row, dcol)


def ssd_bwd(xc3, p3, dy3, states, bias_row, alog_row, bias_col, alog_col, dcol, comm=None):
    B, S, _ = xc3.shape
    nc = S // CHUNK

    def body(xs_ref, bc_ref, dtb_ref, dy_ref, st_ref, bias_ref, alog_ref, biasc_ref, alogc_ref, dcol_ref,
             dxc_ref, ddt_ref, dalog_ref, dd_ref, dbias_ref, dS, dUT, accA, accD, accB, dcs_diag):
        @pl.when(pl.program_id(1) == 0)
        def _():
            dS[...] = jnp.zeros_like(dS)

        @pl.when(_first())
        def _():
            accA[...] = jnp.zeros_like(accA)
            accD[...] = jnp.zeros_like(accD)
            accB[...] = jnp.zeros_like(accB)

        dtb = dtb_ref[0]
        dt, cs = _ssd_decay(dtb, bias_ref[...], alog_ref[...])
        dtT, csT = dt.T, cs.T
        decX = _expand(jnp.exp(csT[:, CHUNK - 1:CHUNK] - csT))
        eX = _expand(jnp.exp(csT))
        dtX = _expand(dtT)
        elastX = eX[:, CHUNK - 1:CHUNK]
        xsT = xs_ref[0].T
        uT = xsT * dtX
        dYT = dy_ref[0].T
        bc = bc_ref[0]
        lrow = lax.broadcasted_iota(jnp.int32, (CHUNK, CHUNK), 0)
        scol = lax.broadcasted_iota(jnp.int32, (CHUNK, CHUNK), 1)
        dcs_diag[...] = jnp.zeros_like(dcs_diag)
        rs_cols = jnp.zeros((CHUNK, 128), f32)
        vparts, zparts = [], []
        for g in range(2):
            Bf = bc[:, g * SSD_N:(g + 1) * SSD_N]
            Bg = Bf.astype(bf16)
            Cg = bc[:, (2 + g) * SSD_N:(3 + g) * SSD_N].astype(bf16)
            G = lax.dot_general(Cg, Bg, _NT, preferred_element_type=f32)
            BgT = Bf.T.astype(bf16)
            rows = slice(g * GR, (g + 1) * GR)
            dSg = dS[rows]
            Sg = st_ref[0, 0, rows, :]
            dUst = jnp.dot(dSg.astype(bf16), BgT, preferred_element_type=f32) * decX[rows]
            yoffT = lax.dot_general(Sg.astype(bf16), Cg, _NT, preferred_element_type=f32) * eX[rows]
            zparts.append(dYT[rows] * yoffT - dUst * uT[rows])
            dG = jnp.zeros((CHUNK, CHUNK), f32)
            for k in range(GH):
                h = g * GH + k
                hr = slice(h * SSD_HD, (h + 1) * SSD_HD)
                seg = cs[:, DT_LANE + h:DT_LANE + h + 1] - csT[DT_LANE + h:DT_LANE + h + 1, :]
                L = jnp.where(lrow >= scol, jnp.exp(jnp.minimum(seg, 0.0)), 0.0)
                M = G * L
                dYh = dYT[hr].astype(bf16)
                dUT[hr] = jnp.dot(dYh, M.astype(bf16), preferred_element_type=f32) + dUst[k * SSD_HD:(k + 1) * SSD_HD]
                dM = lax.dot_general(dYh, uT[hr].astype(bf16), _TN, preferred_element_type=f32)
                dG = dG + dM * L
                Wm = dM * M
                rs_cols = jnp.where(scol == DT_LANE + h, jnp.sum(Wm, axis=1, keepdims=True), rs_cols)
                dcs_diag[DT_LANE + h:DT_LANE + h + 1, :] = -_rowsum(Wm)
            dGb = dG.astype(bf16)
            dYe = (dYT[rows] * eX[rows]).astype(bf16)
            ude = (uT[rows] * decX[rows]).astype(bf16)
            dC = jnp.dot(dGb, Bg, preferred_element_type=f32) + lax.dot_general(dYe, Sg.astype(bf16), _TN, preferred_element_type=f32)
            dB = (lax.dot_general(dGb, Cg, _TN, preferred_element_type=f32)
                  + lax.dot_general(ude, dSg.astype(bf16), _TN, preferred_element_type=f32))
            dxc_ref[0, :, D_SSD + g * SSD_N:D_SSD + (g + 1) * SSD_N] = dB
            dxc_ref[0, :, D_SSD + (2 + g) * SSD_N:D_SSD + (3 + g) * SSD_N] = dC
            vparts.append(elastX[rows] * jnp.sum(dSg * Sg, axis=1, keepdims=True)
                          + jnp.sum(dUst * uT[rows], axis=1, keepdims=True))
            dS[rows] = elastX[rows] * dSg + jnp.dot(dYe, Cg, preferred_element_type=f32)
        dU = dUT[...]
        dcv = dcol_ref[...]
        dxc_ref[0, :, 0:D_SSD] = (dtX * dU + dcv * dYT).T
        lane = lax.broadcasted_iota(jnp.int32, (D_SSD, CHUNK), 1)
        Z = jnp.concatenate(zparts, axis=0) + jnp.where(lane == CHUNK - 1, jnp.concatenate(vparts, axis=0), 0.0)
        hr_ = lax.broadcasted_iota(jnp.int32, (128, D_SSD), 0)
        hc_ = lax.broadcasted_iota(jnp.int32, (128, D_SSD), 1)
        hsel = (hr_ - DT_LANE == jnp.right_shift(hc_, 6)).astype(bf16)
        summands = jnp.concatenate([Z, dU * xsT, dYT * xsT], axis=1)
        hi = summands.astype(bf16)
        lo = (summands - hi.astype(f32)).astype(bf16)
        red = jnp.dot(hsel, hi, preferred_element_type=f32) + jnp.dot(hsel, lo, preferred_element_type=f32)
        dcsT = red[:, 0:CHUNK] + dcs_diag[...] + rs_cols.T
        daT = jnp.dot(dcsT, (lrow >= scol).astype(f32), precision=HIGHEST, preferred_element_type=f32)
        rowi = lax.broadcasted_iota(jnp.int32, (128, 1), 0)
        hmask = (rowi >= DT_LANE) & (rowi < DT_LANE + SSD_HEADS)
        a_col = jnp.where(hmask, -jnp.exp(alogc_ref[...]), 0.0)
        ddtT = red[:, CHUNK:2 * CHUNK] + a_col * daT
        ddt_rawT = jnp.where(hmask, ddtT * jax.nn.sigmoid(dtb.T + biasc_ref[...]), 0.0)
        ddt_ref[0] = ddt_rawT.T
        accA[...] += daT * dtT
        accD[...] += red[:, 2 * CHUNK:3 * CHUNK]
        accB[...] += ddt_rawT

        @pl.when((pl.program_id(0) == B - 1) & (pl.program_id(1) == nc - 1))
        def _():
            dalog_ref[...] = jnp.broadcast_to(jnp.sum(accA[...], axis=1, keepdims=True) * a_col, (128, 128))
            dd_ref[...] = jnp.broadcast_to(jnp.sum(accD[...], axis=1, keepdims=True), (128, 128))
            dbias_ref[...] = jnp.broadcast_to(jnp.sum(accB[...], axis=1, keepdims=True), (128, 128))

    def rev(width, cb):
        return pl.BlockSpec((1, CHUNK, width), lambda b, c: (b, nc - 1 - c, cb))

    acc_spec = pl.BlockSpec((128, 128), lambda b, c: (0, 0))
    acc_shape = jax.ShapeDtypeStruct((128, 128), f32)
    body, c_args, c_in, c_out, c_shapes, c_sems = _fuse_exchange(body, comm, 10, 5, 6, (B, nc))
    res = pl.pallas_call(body, name="ssd_bwd_x" if comm else "ssd_bwd", grid=(B, nc),
                         in_specs=[rev(D_SSD, 0), rev(512, 2), rev(128, OFF_KRDT // 128), rev(D_SSD, 0),
                                   pl.BlockSpec((1, 1, D_SSD, SSD_N), lambda b, c: (b, nc - 1 - c, 0, 0)),
                                   _const(1, 128), _const(1, 128), _const(128, 1), _const(128, 1), _const(D_SSD, 1)] + c_in,
                         out_specs=[rev(D_XBC, 0), rev(128, 0), acc_spec, acc_spec, acc_spec] + c_out,
                         out_shape=[jax.ShapeDtypeStruct((B, S, D_XBC), f32), jax.ShapeDtypeStruct((B, S, 128), f32),
                                    acc_shape, acc_shape, acc_shape] + c_shapes,
                         scratch_shapes=[pltpu.VMEM((D_SSD, SSD_N), f32), pltpu.VMEM((D_SSD, CHUNK), f32),
                                         pltpu.VMEM((128, 128), f32), pltpu.VMEM((128, 128), f32), pltpu.VMEM((128, 128), f32),
                                         pltpu.VMEM((128, 128), f32)] + c_sems,
                         compiler_params=_cparams(2))(xc3, xc3, p3, dy3, states, bias_row, alog_row, bias_col, alog_col, dcol, *c_args)
    return (*res[:5], list(res[5:]))


ATT_SCALE = float(QK) ** -0.5
LOG2E = 1.4426950408889634
LN2 = 0.6931471805599453
Q_FOLD = ATT_SCALE * LOG2E
NEG = -1e30
HP = 4


def _att_block(S):
    return _pick(S, (512, 256, 128))


def attn_fwd(q3, k3, v3, comm=None):
    B, S, _ = q3.shape
    bq = _att_block(S)
    nq = S // bq

    def body(q_ref, k_ref, v_ref, o_ref, lse_ref, m_s, l_s, acc):
        i, j = pl.program_id(2), pl.program_id(3)

        @pl.when(j == 0)
        def _():
            m_s[...] = jnp.full_like(m_s, NEG)
            l_s[...] = jnp.zeros_like(l_s)
            acc[...] = jnp.zeros_like(acc)

        def step(masked):
            for t in range(HP):
                qk = slice(t * 256, (t + 1) * 256)
                st = lax.dot_general(k_ref[0, :, qk], q_ref[0, :, qk], _NT, preferred_element_type=f32)
                if masked:
                    r = lax.broadcasted_iota(jnp.int32, (bq, bq), 0)
                    c = lax.broadcasted_iota(jnp.int32, (bq, bq), 1)
                    st = jnp.where(c >= r, st, NEG)
                m_old = m_s[t]
                m_new = jnp.maximum(m_old, jnp.max(st, axis=0, keepdims=True))
                alpha = jnp.exp2(m_old - m_new)
                pt = jnp.exp2(st - m_new)
                l_s[t] = alpha * l_s[t] + jnp.sum(pt, axis=0, keepdims=True)
                acc[t] = alpha * acc[t] + lax.dot_general(v_ref[0, :, t * VD:(t + 1) * VD], pt.astype(bf16), _TN,
                                                          preferred_element_type=f32)
                m_s[t] = m_new

        @pl.when(j < i)
        def _():
            step(False)

        @pl.when(j == i)
        def _():
            step(True)
            for t in range(HP):
                o_ref[0, :, t * VD:(t + 1) * VD] = (acc[t] / l_s[t]).T
                lse_ref[0, t] = m_s[t] + jnp.log2(l_s[t])

    grid = (B, MLA_H // HP, nq, nq)
    body, c_args, c_in, c_out, c_shapes, c_sems = _fuse_exchange(body, comm, 3, 2, 3, grid)
    res = pl.pallas_call(body, name="attn_fwd_x" if comm else "attn_fwd", grid=grid,
                         in_specs=[pl.BlockSpec((1, bq, HP * 256), lambda b, h, i, j: (b, i, h)),
                                   pl.BlockSpec((1, bq, HP * 256), lambda b, h, i, j: (b, jnp.minimum(j, i), h)),
                                   pl.BlockSpec((1, bq, HP * VD), lambda b, h, i, j: (b, jnp.minimum(j, i), h))] + c_in,
                         out_specs=[pl.BlockSpec((1, bq, HP * VD), lambda b, h, i, j: (b, i, h)),
                                    pl.BlockSpec((1, HP, 1, bq), lambda b, h, i, j: (b, h, 0, i))] + c_out,
                         out_shape=[jax.ShapeDtypeStruct((B, S, MLA_H * VD), f32), jax.ShapeDtypeStruct((B, MLA_H, 1, S), f32)] + c_shapes,
                         scratch_shapes=[pltpu.VMEM((HP, 1, bq), f32), pltpu.VMEM((HP, 1, bq), f32), pltpu.VMEM((HP, VD, bq), f32)] + c_sems,
                         compiler_params=_cparams(4))(q3, k3, v3, *c_args)
    return res[0], res[1], list(res[2:])


def rms_o_bwd(o3, dcat3, g):
    B, S, C = o3.shape
    ts = _pick(S, (512, 256, 128))

    def body(x_ref, do_ref, g_ref, dx_ref, dg_ref, d_ref):
        @pl.when(_first())
        def _():
            dg_ref[...] = jnp.zeros_like(dg_ref)

        x, do = x_ref[0], do_ref[0].astype(f32)
        r = lax.rsqrt(jnp.mean(x * x, axis=-1, keepdims=True) + EPS)
        xh = x * r
        dg_ref[...] += _rowsum(do * xh)
        dxh = do * g_ref[...]
        dx = r * (dxh - xh * jnp.mean(dxh * xh, axis=-1, keepdims=True))
        dx_ref[0] = dx
        for h in range(MLA_H):
            vs = slice(h * VD, (h + 1) * VD)
            d_ref[0, h] = jnp.sum(dx[:, vs] * x[:, vs], axis=-1, keepdims=True)

    return _row_call(body, "rms_o_bwd", B, S, ts, [_tok(ts, C), _tok(ts, C, 1), _const(1, C)],
                     [_tok(ts, C), _const(1, C), pl.BlockSpec((1, MLA_H, ts, 1), lambda b, s: (b, 0, s, 0))],
                     [jax.ShapeDtypeStruct((B, S, C), f32), jax.ShapeDtypeStruct((1, C), f32),
                      jax.ShapeDtypeStruct((B, MLA_H, S, 1), f32)])(o3, dcat3, g)


def attn_bwd(q3, k3, v3, do3, lse_row, delta_row, cosf, sinf, comm=None):
    B, S, _ = q3.shape
    bq = _att_block(S)
    nq = S // bq

    def body(q_ref, k_ref, v_ref, do_ref, lse_ref, dl_ref, cos_ref, sin_ref, dkn_ref, dv_ref, dkr_ref, dq_hbm, dk_acc, dv_acc,
             dq_scr, stage, dq_sem):
        b, hp, j, i = pl.program_id(0), pl.program_id(1), pl.program_id(2), pl.program_id(3)
        rows = pl.ds(pl.multiple_of(i * bq, bq), bq)

        @pl.when((j == 0) & (i == 0))
        def _():
            dq_scr[...] = jnp.zeros_like(dq_scr)

        @pl.when(i == 0)
        def _():
            dk_acc[...] = jnp.zeros_like(dk_acc)
            dv_acc[...] = jnp.zeros_like(dv_acc)

        def step(masked):
            for t in range(HP):
                qk, vs = slice(t * 256, (t + 1) * 256), slice(t * VD, (t + 1) * VD)
                q, k = q_ref[0, :, qk], k_ref[0, :, qk]
                do = do_ref[0, :, vs].astype(bf16)
                pt = jnp.exp2(lax.dot_general(k, q, _NT, preferred_element_type=f32) - lse_ref[0, t])
                if masked:
                    r = lax.broadcasted_iota(jnp.int32, (bq, bq), 0)
                    c = lax.broadcasted_iota(jnp.int32, (bq, bq), 1)
                    pt = jnp.where(c >= r, pt, 0.0)
                dv_acc[t] += jnp.dot(pt.astype(bf16), do, preferred_element_type=f32)
                dpt = lax.dot_general(v_ref[0, :, vs], do, _NT, preferred_element_type=f32)
                dst = (pt * (dpt - dl_ref[0, t])).astype(bf16)
                dk_acc[t] += jnp.dot(dst, q, preferred_element_type=f32)
                dq_scr[t, rows, :] += lax.dot_general(dst, k, _TN, preferred_element_type=f32)

        @pl.when(i > j)
        def _():
            step(False)

        @pl.when(i == j)
        def _():
            step(True)
            for t in range(HP):
                d = dq_scr[t, rows, :] * (LN2 * Q_FOLD)
                stage[t, :, 0:NOPE] = d[:, 0:NOPE].astype(bf16)
                stage[t, :, NOPE:] = _rope_t(d[:, NOPE:], cos_ref[0], sin_ref[0]).astype(bf16)
                cp = pltpu.make_async_copy(stage.at[t], dq_hbm.at[b, rows, pl.ds(pl.multiple_of((hp * HP + t) * 256, 256), 256)],
                                           dq_sem.at[t])
                cp.start()
                cp.wait()

        @pl.when(i == nq - 1)
        def _():
            kr = jnp.zeros((bq, 128), f32)
            for t in range(HP):
                dkn_ref[0, :, t * NOPE:(t + 1) * NOPE] = (dk_acc[t, :, 0:NOPE] * LN2).astype(bf16)
                dv_ref[0, :, t * VD:(t + 1) * VD] = dv_acc[t].astype(bf16)
                kr = kr + dk_acc[t, :, NOPE:]
            dkr_ref[0] = kr * LN2

    kspec = pl.BlockSpec((1, bq, HP * 256), lambda b, h, j, i: (b, j, h))
    vspec = pl.BlockSpec((1, bq, HP * VD), lambda b, h, j, i: (b, j, h))
    krspec = pl.BlockSpec((1, bq, 128), lambda b, h, j, i: (b, j, h))
    rspec = pl.BlockSpec((1, HP, 1, bq), lambda b, h, j, i: (b, h, 0, jnp.maximum(i, j)))
    tspec = pl.BlockSpec((1, bq, 128), lambda b, h, j, i: (b, jnp.maximum(i, j), 0))
    grid = (B, MLA_H // HP, nq, nq)
    body, c_args, c_in, c_out, c_shapes, c_sems = _fuse_exchange(body, comm, 8, 4, 5, grid)
    res = pl.pallas_call(body, name="attn_bwd_x" if comm else "attn_bwd", grid=grid,
                         in_specs=[pl.BlockSpec((1, bq, HP * 256), lambda b, h, j, i: (b, jnp.maximum(i, j), h)), kspec, vspec,
                                   pl.BlockSpec((1, bq, HP * VD), lambda b, h, j, i: (b, jnp.maximum(i, j), h)), rspec, rspec,
                                   tspec, tspec] + c_in,
                         out_specs=[vspec, vspec, krspec, pl.BlockSpec(memory_space=pltpu.HBM)] + c_out,
                         out_shape=[jax.ShapeDtypeStruct((B, S, MLA_H * NOPE), bf16), jax.ShapeDtypeStruct((B, S, MLA_H * VD), bf16),
                                    jax.ShapeDtypeStruct((B, S, MLA_H // HP * 128), f32),
                                    jax.ShapeDtypeStruct((B, S, MLA_H * 256), bf16)] + c_shapes,
                         scratch_shapes=[pltpu.VMEM((HP, bq, 256), f32), pltpu.VMEM((HP, bq, VD), f32), pltpu.VMEM((HP, S, 256), f32),
                                         pltpu.VMEM((HP, bq, 256), bf16), pltpu.SemaphoreType.DMA((HP,))] + c_sems,
                         compiler_params=_cparams(4))(q3, k3, v3, do3, lse_row, delta_row, cosf, sinf, *c_args)
    return res[3], res[0], res[1], res[2], list(res[4:])


def ada_fwd(c_all, w, b):
    n = w.shape[1]

    def body(c_ref, w_ref, b_ref, o_ref):
        o_ref[...] = jnp.dot(_silu(c_ref[...]).astype(bf16), w_ref[...].astype(bf16), preferred_element_type=f32) + b_ref[...]

    return pl.pallas_call(body, name="ada_fwd", out_shape=jax.ShapeDtypeStruct((c_all.shape[0], n), f32),
                          compiler_params=pltpu.CompilerParams(vmem_limit_bytes=VMEM_LIMIT))(c_all, w, b)


def ada_bwd(c_all, dmod):
    n = dmod.shape[1]

    def body(c_ref, d_ref, o_ref):
        o_ref[...] = lax.dot_general(_silu(c_ref[...]).astype(bf16), d_ref[...].astype(bf16), _TN, preferred_element_type=f32)

    return pl.pallas_call(body, name="ada_bwd", out_shape=jax.ShapeDtypeStruct((c_all.shape[1], n), f32),
                          compiler_params=pltpu.CompilerParams(vmem_limit_bytes=VMEM_LIMIT))(c_all, dmod)


def sum_leading(x, name):
    n, R, _ = x.shape
    tr = _pick(R, (512, 256, 128, 64, 32, 16, 8))

    def body(x_ref, o_ref):
        acc = x_ref[0].astype(f32)
        for k in range(1, n):
            acc = acc + x_ref[k].astype(f32)
        o_ref[...] = acc

    return pl.pallas_call(body, name=name, grid=(R // tr,), in_specs=[pl.BlockSpec((n, tr, 128), lambda i: (0, i, 0))],
                          out_specs=pl.BlockSpec((tr, 128), lambda i: (i, 0)), out_shape=jax.ShapeDtypeStruct((R, 128), f32),
                          compiler_params=_cparams(1))(x)


def _adamw_body(w_ref, g_ref, m_ref, v_ref, d_ref, mo_ref, vo_ref):
    gv = g_ref[...]
    mn = ADAM_B1 * m_ref[...] + (1.0 - ADAM_B1) * gv
    vn = ADAM_B2 * v_ref[...] + (1.0 - ADAM_B2) * jnp.square(gv)
    m_hat = mn / (1.0 - ADAM_B1 ** ADAM_STEP)
    v_hat = vn / (1.0 - ADAM_B2 ** ADAM_STEP)
    d_ref[...] = -ADAM_LR * (m_hat / (jnp.sqrt(v_hat) + ADAM_EPS) + ADAM_WD * w_ref[...])
    mo_ref[...] = mn
    vo_ref[...] = vn


def adamw(w, g, m, v):
    R = w.shape[0]
    tr = _pick(R, (512, 256, 128, 64, 32, 16, 8))
    spec = pl.BlockSpec((tr, 128), lambda i: (i, 0))
    shp = jax.ShapeDtypeStruct((R, 128), f32)
    return pl.pallas_call(functools.partial(_adamw_body), name="adamw", grid=(R // tr,), in_specs=[spec] * 4,
                          out_specs=[spec] * 3, out_shape=[shp] * 3, compiler_params=_cparams(1))(w, g, m, v)


def _row_tile(a):
    return _pick(a, (256, 128, 64, 32, 16, 8)) if a % 8 == 0 else a


def adamw_nd(w, g, m, v):
    L, a, b = w.shape
    ta = _row_tile(a)
    spec = pl.BlockSpec((1, ta, b), lambda l, i: (l, i, 0))
    shp = jax.ShapeDtypeStruct((L, a, b), f32)
    return pl.pallas_call(functools.partial(_adamw_body), name="adamw_nd", grid=(L, a // ta), in_specs=[spec] * 4,
                          out_specs=[spec] * 3, out_shape=[shp] * 3, compiler_params=_cparams(2))(w, g, m, v)


def sum_slots(x):
    n, L, a, b = x.shape
    ta = _row_tile(a)

    def body(x_ref, o_ref):
        acc = x_ref[0].astype(f32)
        for k in range(1, n):
            acc = acc + x_ref[k].astype(f32)
        o_ref[...] = acc

    return pl.pallas_call(body, name="sum_slots", grid=(L, a // ta),
                          in_specs=[pl.BlockSpec((n, 1, ta, b), lambda l, i: (0, l, i, 0))],
                          out_specs=pl.BlockSpec((1, ta, b), lambda l, i: (l, i, 0)),
                          out_shape=jax.ShapeDtypeStruct((L, a, b), f32), compiler_params=_cparams(2))(x)


def _exchange_copies(ins, outs, sems, scatter):
    send_sems, recv_sems, local_sems = sems
    x, y, c = lax.axis_index("x"), lax.axis_index("y"), lax.axis_index("c")
    me = 4 * x + 2 * y + c
    locals_, sends, recvs = [], [], []
    for a in range(len(ins)):
        locals_.append(pltpu.make_async_copy(ins[a].at[me] if scatter[a] else ins[a], outs[a].at[me], local_sems.at[a]))
        for k in range(N_DEV - 1):
            px = 1 - x if (k + 1) & 4 else x
            py = 1 - y if (k + 1) & 2 else y
            pc = 1 - c if (k + 1) & 1 else c
            pid = 4 * px + 2 * py + pc
            src = ins[a].at[pid] if scatter[a] else ins[a]
            for slot, group in ((me, sends), (pid, recvs)):
                group.append(pltpu.make_async_remote_copy(src_ref=src, dst_ref=outs[a].at[slot], send_sem=send_sems.at[a, k],
                                                          recv_sem=recv_sems.at[a, k], device_id=(px, py, pc),
                                                          device_id_type=pl.DeviceIdType.MESH))
    return locals_, sends, recvs


def _exchange_start(ins, outs, sems, scatter):
    locals_, sends, _ = _exchange_copies(ins, outs, sems, scatter)
    for cp in locals_ + sends:
        cp.start()


def _exchange_wait(ins, outs, sems, scatter):
    locals_, sends, recvs = _exchange_copies(ins, outs, sems, scatter)
    for cp in recvs:
        cp.wait_recv()
    for cp in sends:
        cp.wait_send()
    for cp in locals_:
        cp.wait()


def _exchange_shapes(arrays, scatter):
    return [jax.ShapeDtypeStruct((N_DEV,) + tuple(a.shape[1:] if s else a.shape), a.dtype) for a, s in zip(arrays, scatter)]


def _flags(scatter, n):
    return [scatter] * n if isinstance(scatter, bool) else list(scatter)


def _exchange_sems(n):
    return [pltpu.SemaphoreType.DMA((n, N_DEV - 1)), pltpu.SemaphoreType.DMA((n, N_DEV - 1)), pltpu.SemaphoreType.DMA((n,))]


def _fuse_exchange(core, comm, n_in, n_out, n_scr, grid):
    if comm is None:
        return core, [], [], [], [], []
    arrays, scatter = comm
    n = len(arrays)
    scatter = _flags(scatter, n)

    def body(*refs):
        a, b, c = n_in + n, n_in + n + n_out, n_in + 2 * n + n_out
        cin, cout, sems = refs[n_in:a], refs[b:c], refs[c + n_scr:]
        ids = [pl.program_id(d) for d in range(len(grid))]
        first = functools.reduce(lambda p, q: p & q, [i == 0 for i in ids])
        last = functools.reduce(lambda p, q: p & q, [i == g - 1 for i, g in zip(ids, grid)])

        @pl.when(first)
        def _():
            _exchange_start(cin, cout, sems, scatter)

        core(*refs[:n_in], *refs[a:b], *refs[c:c + n_scr])

        @pl.when(last)
        def _():
            _exchange_wait(cin, cout, sems, scatter)

    hbm = pl.BlockSpec(memory_space=pltpu.HBM)
    return body, list(arrays), [hbm] * n, [hbm] * n, _exchange_shapes(arrays, scatter), _exchange_sems(n)


def exchange(arrays, scatter, name):
    n = len(arrays)
    scatter = _flags(scatter, n)

    def body(*refs):
        ins, outs, sems = refs[:n], refs[n:2 * n], refs[2 * n:]
        _exchange_start(ins, outs, sems, scatter)
        _exchange_wait(ins, outs, sems, scatter)

    hbm = pl.BlockSpec(memory_space=pltpu.HBM)
    return pl.pallas_call(body, name=name, in_specs=[hbm] * n, out_specs=[hbm] * n,
                          out_shape=_exchange_shapes(arrays, scatter), scratch_shapes=_exchange_sems(n))(*arrays)


BIG = (("w_in", "col"), ("conv_w", "col"), ("w_uq", "col"), ("w_ukv", "col"), ("w_out", "row"), ("w_up", "col"),
       ("conv_ff_w", "col"), ("w_down", "row"))
SMALL = ("b_ada", "norm_mix", "conv_b", "dt_bias", "a_log", "d_skip", "ssd_norm", "q_norm", "kv_norm", "attn_norm",
         "norm_mlp", "conv_ff_b", "final_norm")
CONVS = ("conv_w", "conv_ff_w")
PACK_ALIGN = 2048


def _padded(n):
    return -(-n // PACK_ALIGN) * PACK_ALIGN


def _flat_pad(a):
    f = a.reshape(-1)
    return jnp.pad(f, (0, _padded(f.shape[0]) - f.shape[0]))


PACK_ROWS = 512


def pack(arrs):
    f = jnp.concatenate([_flat_pad(a) for a in arrs])
    n = PACK_ROWS * 128
    return jnp.pad(f, (0, -(-f.shape[0] // n) * n - f.shape[0])).reshape(-1, 128)


def unpack(flat, shapes):
    f = flat.reshape(-1)
    out, off = [], 0
    for s in shapes:
        n = int(np.prod(s))
        out.append(f[off:off + n].reshape(s))
        off += _padded(n)
    return out


def shards_to_full(g, kind):
    _, a, b = g.shape
    if kind == "col":
        return g.transpose(1, 0, 2).reshape(a, N_DEV * b)
    return g.reshape(N_DEV * a, b)


def full_to_shards(full, kind):
    if kind == "col":
        a, nb = full.shape
        return full.reshape(a, N_DEV, nb // N_DEV).transpose(1, 0, 2)
    na, b = full.shape
    return full.reshape(N_DEV, na // N_DEV, b)


def w_in_layout(w):
    z = lambda n: jnp.zeros(w.shape[:-1] + (n,), w.dtype)
    return jnp.concatenate([w[..., :2560], w[..., 2576:2960], z(128), w[..., 2960:3216], w[..., 3216:3280],
                            w[..., 2560:2576], z(48)], axis=-1)


def w_in_unlayout(g):
    return jnp.concatenate([g[..., :2560], g[..., 3392:3408], g[..., 2560:2944], g[..., 3072:3328], g[..., 3328:3392]], axis=-1)


def w_uq_layout(w):
    return jnp.pad(w.reshape(Q_RANK, MLA_H, QK), ((0, 0), (0, 0), (0, 256 - QK))).reshape(Q_RANK, MLA_H * 256)


def w_uq_unlayout(g):
    return g.reshape(Q_RANK, MLA_H, 256)[:, :, :QK].reshape(Q_RANK, MLA_H * QK)


def w_ukv_layout(w):
    return w.reshape(KV_RANK, MLA_H, 2, 128).transpose(0, 2, 1, 3).reshape(KV_RANK, 2 * MLA_H * 128)


def w_ukv_unlayout(g):
    return g.reshape(KV_RANK, 2, MLA_H, 128).transpose(0, 2, 1, 3).reshape(KV_RANK, 2 * MLA_H * 128)


LAYOUTS = {"w_in": (w_in_layout, w_in_unlayout), "w_uq": (w_uq_layout, w_uq_unlayout), "w_ukv": (w_ukv_layout, w_ukv_unlayout)}
FIRST, REST = BIG[:4], BIG[4:]


def layer_weights(gathered, entries):
    full = {n: shards_to_full(g, kind) for (n, kind), g in zip(entries, gathered)}
    return {n: LAYOUTS[n][0](w) if n in LAYOUTS else w for n, w in full.items()}


def layer_grad_slices(g, entries):
    return [full_to_shards(LAYOUTS[n][1](g[n]) if n in LAYOUTS else g[n], kind).astype(bf16) for n, kind in entries]


def _head_row(v):
    return jnp.zeros((1, 128), f32).at[0, DT_LANE:DT_LANE + SSD_HEADS].set(v)


def layer_fwd(x3, mod, W, P, l, cosf, sinf, comm=None, late=None, comm_up=None):
    B, S, _ = x3.shape
    T = B * S
    sv = {}
    h = normmod_fwd(x3, mod, P["norm_mix"][l][None], 0, 1)
    p = mm(h.reshape(T, D), W["w_in"], "nn", "mm_in")
    p3 = p.reshape(B, S, IN_COLS)
    bias_row, alog_row = _head_row(P["dt_bias"][l]), _head_row(P["a_log"][l])
    dcol = jnp.repeat(P["d_skip"][l], SSD_HD)[:, None]
    xc3, xpre = conv_ssd_fwd(p3, W["conv_w"], P["conv_b"][l][None])
    yc3, states = ssd_fwd(xc3, p3, bias_row, alog_row, dcol)
    y_ssd = ssd_out_fwd(yc3, p3, P["ssd_norm"][l][None])
    cqn = rms_fwd(p3, 512, OFF_CQ // 512, Q_RANK, P["q_norm"][l][None], "rms_q_fwd")
    ckvn = rms_fwd(p3, KV_RANK, OFF_CKV // KV_RANK, KV_RANK, P["kv_norm"][l][None], "rms_kv_fwd")
    q3 = mm(cqn.reshape(T, Q_RANK), W["w_uq"], "nn", "mm_uq", out_dtype=bf16,
            rope=(cosf.reshape(T, 128), sinf.reshape(T, 128))).reshape(B, S, -1)
    k3 = k_proj(ckvn, W["w_ukv"], p3, cosf, sinf)
    v3 = mm(ckvn.reshape(T, KV_RANK), W["w_ukv"][:, MLA_H * NOPE:], "nn", "mm_uv", out_dtype=bf16).reshape(B, S, -1)
    o3, lse, comm_out = attn_fwd(q3, k3, v3, comm)
    if late is not None:
        W = dict(W, **late(comm_out))
    y_att = rms_fwd(o3, D, 0, D, P["attn_norm"][l][None], "rms_o_fwd")
    cat = (y_ssd.reshape(T, D), y_att.reshape(T, D))
    x1, y1 = mm(cat, W["w_out"], "nn", "mm_out", resid=x3.reshape(T, D), gate=mod[:, 2:3, :], seq=S)
    x13 = x1.reshape(B, S, D)
    h2 = normmod_fwd(x13, mod, P["norm_mlp"][l][None], 3, 4)
    u, up_out = mm(h2.reshape(T, D), W["w_up"], "nn", "mm_up", out_dtype=bf16, comm=comm_up), []
    if comm_up is not None:
        u, up_out = u
    u3 = u.reshape(B, S, 2 * D_FF)
    a, ffg, ffv = glu_fwd(u3, W["conv_ff_w"], P["conv_ff_b"][l][None])
    x2, y2 = mm(a.reshape(T, D_FF), W["w_down"], "nn", "mm_down", resid=x1, gate=mod[:, 5:6, :], seq=S)
    sv.update(x=x3, h=h, p3=p3, xc3=xc3, xpre=xpre, yc3=yc3, states=states, cqn=cqn, ckvn=ckvn, q3=q3, k3=k3, v3=v3, o3=o3, lse=lse,
              cat=cat, y1=y1, x1=x13, h2=h2, u3=u3, ffg=ffg, ffv=ffv, a=a, y2=y2, bias_row=bias_row, alog_row=alog_row, dcol=dcol)
    return x2.reshape(B, S, D), sv, comm_out, W, up_out


def layer_bwd(dx3, sv, mod, W, P, l, cosf, sinf, comm=None, send_rest=False):
    B, S, _ = dx3.shape
    T = B * S
    g = {}
    dy2, dg2 = gate_bwd(dx3, sv["y2"].reshape(B, S, D), mod, 5)
    dy2 = dy2.reshape(T, D)
    da = mm(dy2, W["w_down"], "nt", "mm_down_dx", out_dtype=bf16)
    g["w_down"] = mm(sv["a"].reshape(T, D_FF), dy2, "tn", "mm_down_dw")
    dug, duv, dwg, dwv, dbg, dbv = glu_bwd(sv["u3"], sv["ffg"], sv["ffv"], da.reshape(B, S, D_FF), W["conv_ff_w"])
    g["conv_ff_w"] = jnp.concatenate([dwg, dwv], axis=1)
    g["conv_ff_b"] = jnp.concatenate([dbg, dbv], axis=1)[0]
    du = (dug.reshape(T, D_FF), duv.reshape(T, D_FF))
    dh2 = mm(du, W["w_up"], "nt", "mm_up_dx", out_dtype=bf16)
    g["w_up"] = jnp.concatenate([mm(sv["h2"].reshape(T, D), d, "tn", "mm_up_dw") for d in du], axis=1)
    dx1, dsh2, dsc2, dnm, dy1, dg1 = normmod_bwd(sv["x1"], dh2.reshape(B, S, D), dx3, mod, P["norm_mlp"][l][None], 4,
                                                 y3=sv["y1"].reshape(B, S, D), i_g=2)
    g["norm_mlp"] = dnm[0]
    dy1 = dy1.reshape(T, D)
    dcat = mm(dy1, W["w_out"], "nt", "mm_out_dx", out_dtype=bf16)
    g["w_out"] = jnp.concatenate([mm(part, dy1, "tn", "mm_out_dw") for part in sv["cat"]], axis=0)
    dcat3 = dcat.reshape(B, S, 2 * D)
    dyc3, dp, dsn = ssd_out_bwd(sv["yc3"], sv["p3"], dcat3, P["ssd_norm"][l][None])
    g["ssd_norm"] = dsn[0]
    do3, dan, delta = rms_o_bwd(sv["o3"], dcat3, P["attn_norm"][l][None])
    g["attn_norm"] = dan[0]
    dqraw, dkn3, dv3, dkr3, comm_out = attn_bwd(sv["q3"], sv["k3"], sv["v3"], do3, sv["lse"], delta.reshape(B, MLA_H, 1, S),
                                                cosf, sinf, comm)
    dqraw = dqraw.reshape(T, -1)
    dcqn = mm(dqraw, W["w_uq"], "nt", "mm_uq_dx")
    g["w_uq"] = mm(sv["cqn"].reshape(T, Q_RANK), dqraw, "tn", "mm_uq_dw")
    dp, dqn = rms_bwd_into(sv["p3"], 512, OFF_CQ // 512, Q_RANK, dcqn.reshape(B, S, Q_RANK), P["q_norm"][l][None], dp, "rms_q_bwd")
    g["q_norm"] = dqn[0]
    bias_col, alog_col = sv["bias_row"].reshape(128, 1), sv["alog_row"].reshape(128, 1)
    comm_rest = (layer_grad_slices(g, REST), True) if send_rest else None
    dxc3, ddt3, dalog, dd, dbias, rest_out = ssd_bwd(sv["xc3"], sv["p3"], dyc3, sv["states"], sv["bias_row"], sv["alog_row"],
                                                     bias_col, alog_col, sv["dcol"], comm_rest)
    heads = slice(DT_LANE, DT_LANE + SSD_HEADS)
    g["a_log"], g["d_skip"], g["dt_bias"] = dalog[heads, 0], dd[heads, 0], dbias[heads, 0]
    dp, dcw, dcb = conv_ssd_bwd(sv["p3"], sv["xpre"], dxc3, W["conv_w"], dp)
    g["conv_w"], g["conv_b"] = dcw, dcb[0]
    dp = kprep_bwd(dkr3, ddt3, cosf, sinf, dp)
    dkv = (dkn3.reshape(T, -1), dv3.reshape(T, -1))
    dckvn = mm(dkv, W["w_ukv"], "nt", "mm_ukv_dx")
    g["w_ukv"] = jnp.concatenate([mm(sv["ckvn"].reshape(T, KV_RANK), d, "tn", "mm_ukv_dw") for d in dkv], axis=1)
    dp, dkn = rms_bwd_into(sv["p3"], KV_RANK, OFF_CKV // KV_RANK, KV_RANK, dckvn.reshape(B, S, KV_RANK), P["kv_norm"][l][None], dp,
                           "rms_kv_bwd")
    g["kv_norm"] = dkn[0]
    dp = dp.reshape(T, IN_COLS)
    dh = mm(dp, W["w_in"], "nt", "mm_in_dx", out_dtype=bf16)
    g["w_in"] = mm(sv["h"].reshape(T, D), dp, "tn", "mm_in_dw")
    dx0, dsh1, dsc1, dnx = normmod_bwd(sv["x"], dh.reshape(B, S, D), dx1, mod, P["norm_mix"][l][None], 1)
    g["norm_mix"] = dnx[0]
    dmod = jnp.concatenate([dsh1, dsc1, dg1, dsh2, dsc2, dg2], axis=1)
    return dx0, dmod, g, comm_out, rest_out


def kernel(x, c, positions, w_ada, b_ada, norm_mix, w_in, conv_w, conv_b, dt_bias, a_log, d_skip, ssd_norm, q_norm, w_uq, kv_norm, w_ukv, attn_norm, w_out, norm_mlp, w_up, conv_ff_w, conv_ff_b, w_down, final_norm, loss_target, m_w_ada, m_b_ada, m_norm_mix, m_w_in, m_conv_w, m_conv_b, m_dt_bias, m_a_log, m_d_skip, m_ssd_norm, m_q_norm, m_w_uq, m_kv_norm, m_w_ukv, m_attn_norm, m_w_out, m_norm_mlp, m_w_up, m_conv_ff_w, m_conv_ff_b, m_w_down, m_final_norm, v_w_ada, v_b_ada, v_norm_mix, v_w_in, v_conv_w, v_conv_b, v_dt_bias, v_a_log, v_d_skip, v_ssd_norm, v_q_norm, v_w_uq, v_kv_norm, v_w_ukv, v_attn_norm, v_w_out, v_norm_mlp, v_w_up, v_conv_ff_w, v_conv_ff_b, v_w_down, v_final_norm):
    given = dict(locals())
    B, S, _ = x.shape
    me = 4 * lax.axis_index("x") + 2 * lax.axis_index("y") + lax.axis_index("c")
    P = {n: given[n] for n in SMALL}

    def shards(l, entries):
        return [given[n][l] if n in CONVS else given[n][l].astype(bf16) for n, _ in entries]

    *gathered, c_all = exchange(shards(0, FIRST) + [c], False, "gather_weights")
    W = [layer_weights(gathered, FIRST), None]

    n_ada = w_ada.shape[2]
    c_all = c_all.reshape(N_DEV * B, D)
    b_sh = lax.dynamic_slice_in_dim(b_ada, me * n_ada, n_ada, axis=1)
    mod_sh = jnp.stack([ada_fwd(c_all, w_ada[l], b_sh[l][None]) for l in range(DEPTH)])
    (mod_g,) = exchange([mod_sh], False, "gather_mod")
    mod_mine = lax.dynamic_slice_in_dim(mod_g, me * B, B, axis=2)
    mods = mod_mine.transpose(1, 2, 0, 3).reshape(DEPTH, B, 6, D)

    inv_freq = jnp.asarray(1.0 / (ROPE_BASE ** (np.arange(0, ROPE, 2, dtype=np.float32) / ROPE)))
    ang = positions.astype(f32)[..., None] * inv_freq
    zeros = jnp.zeros((B, S, 128 - ROPE), f32)
    cosf = jnp.concatenate([jnp.cos(ang), jnp.cos(ang), zeros], axis=-1)
    sinf = jnp.concatenate([jnp.sin(ang), jnp.sin(ang), zeros], axis=-1)

    saved = [None] * DEPTH
    late = lambda got: layer_weights(got, REST)
    xl, saved[0], _, W[0], gathered = layer_fwd(x, mods[0], W[0], P, 0, cosf, sinf, comm=(shards(0, REST), False), late=late,
                                                comm_up=(shards(1, FIRST), False))
    xl, saved[1], _, W[1], _ = layer_fwd(xl, mods[1], layer_weights(gathered, FIRST), P, 1, cosf, sinf,
                                         comm=(shards(1, REST), False), late=late)
    dxl, d_final, loss_part = final_loss(xl, final_norm[None], loss_target)
    grads, dmods, recv = [None] * DEPTH, [None] * DEPTH, [None] * DEPTH
    dxl, dmods[1], grads[1], _, _ = layer_bwd(dxl, saved[1], mods[1], W[1], P, 1, cosf, sinf)
    grad_x, dmods[0], grads[0], recv[1], recv_rest = layer_bwd(dxl, saved[0], mods[0], W[0], P, 0, cosf, sinf,
                                                               comm=(layer_grad_slices(grads[1], BIG), True), send_rest=True)

    stack = lambda n: jnp.stack([grads[l][n] for l in range(DEPTH)])
    small_names = [n for n in SMALL if n not in ("b_ada", "final_norm")]
    partial = pack([stack(n) for n in small_names] + [d_final[0], loss_part[0]])
    dmod_all = jnp.stack(dmods)
    *recv_first, part_g, dmod_g = exchange(layer_grad_slices(grads[0], FIRST) + [partial, dmod_all],
                                           [True] * len(FIRST) + [False, False], "exchange_tail")
    recv[0] = recv_first + recv_rest
    big_g = [jnp.concatenate([sum_slots(recv[l][i][:, None]) for l in range(DEPTH)]) for i in range(len(BIG))]
    small_sum = sum_leading(part_g, "sum_partials")
    small_g = unpack(small_sum, [given[n].shape for n in small_names] + [(D,), (128,)])
    gsmall = dict(zip(small_names + ["final_norm"], small_g[:-1]))
    loss = small_g[-1][0]
    dmod_rows = dmod_g.transpose(0, 2, 1, 3, 4).reshape(N_DEV * B, DEPTH * 6 * D)
    gsmall["b_ada"] = sum_leading(dmod_rows.reshape(N_DEV * B, -1, 128), "sum_b_ada").reshape(DEPTH, 6 * D)
    dmod_cols = dmod_rows.reshape(N_DEV * B, DEPTH, N_DEV, n_ada)
    dmod_sh = lax.dynamic_slice_in_dim(dmod_cols, me, 1, axis=2)[:, :, 0, :]
    g_w_ada = jnp.stack([ada_bwd(c_all, dmod_sh[:, l, :]) for l in range(DEPTH)])

    res = {"grad": {}, "delta": {}, "new_m": {}, "new_v": {}}
    for n, gv in zip([n for n, _ in BIG] + ["w_ada"], big_g + [g_w_ada]):
        res["grad"][n] = gv
        res["delta"][n], res["new_m"][n], res["new_v"][n] = adamw_nd(given[n], gv, given["m_" + n], given["v_" + n])
    shapes = [given[n].shape for n in SMALL]
    flat = adamw(pack([given[n] for n in SMALL]), pack([gsmall[n] for n in SMALL]), pack([given["m_" + n] for n in SMALL]),
                 pack([given["v_" + n] for n in SMALL]))
    for n in SMALL:
        res["grad"][n] = gsmall[n]
    for key, arr in zip(("delta", "new_m", "new_v"), flat):
        res[key].update(zip(SMALL, unpack(arr, shapes)))
    order = ["w_ada", "b_ada", "norm_mix", "w_in", "conv_w", "conv_b", "dt_bias", "a_log", "d_skip", "ssd_norm", "q_norm", "w_uq",
             "kv_norm", "w_ukv", "attn_norm", "w_out", "norm_mlp", "w_up", "conv_ff_w", "conv_ff_b", "w_down", "final_norm"]
    return (loss, grad_x, *[res[k][n] for k in ("grad", "delta", "new_m", "new_v") for n in order])
```

```python
import functools

import numpy as np
import jax
import jax.numpy as jnp
from jax import lax
from jax.experimental import pallas as pl
from jax.experimental.pallas import tpu as pltpu

f32, bf16 = jnp.float32, jnp.bfloat16
HIGHEST = lax.Precision.HIGHEST

D = 1024
D_SSD = 1024
SSD_HEADS = 16
SSD_HD = 64
SSD_N = 128
CHUNK = 128
D_XBC = 1536
CONV_K = 4
MLA_H = 8
NOPE = 128
ROPE = 64
VD = 128
QK = NOPE + ROPE
Q_RANK = 384
KV_RANK = 256
D_FF = 2816
FF_K = 3
EPS = 1e-6
ROPE_BASE = 10000.0
DEPTH = 2
ADAM_LR, ADAM_B1, ADAM_B2, ADAM_EPS, ADAM_WD, ADAM_STEP = 0.001, 0.9, 0.999, 1e-08, 0.01, 10

N_DEV = 8
IN_COLS = 3456
OFF_XBC, OFF_CQ, OFF_CKV, OFF_KRDT = 1024, 2560, 3072, 3328
DT_LANE = 64
VMEM_LIMIT = 48 * 1024 * 1024
MM_K_WHOLE = 4096


def _cparams(n_grid):
    return pltpu.CompilerParams(dimension_semantics=("arbitrary",) * n_grid, vmem_limit_bytes=VMEM_LIMIT)


def _pick(n, cands):
    for c in cands:
        if n % c == 0:
            return c
    return n


def _silu(x):
    return x * jax.nn.sigmoid(x)


def _dsilu(x):
    s = jax.nn.sigmoid(x)
    return s * (1.0 + x * (1.0 - s))


def _rowsum(x):
    return jnp.sum(x, axis=0, keepdims=True)


def mm(a, b, mode, name, out_dtype=f32, resid=None, gate=None, seq=None, comm=None, rope=None):
    parts = list(a) if isinstance(a, (tuple, list)) else [a]
    np_ = len(parts)
    if mode == "nn":
        (M, Kp), N = parts[0].shape, b.shape[1]
    elif mode == "nt":
        (M, Kp), N = parts[0].shape, b.shape[0]
    else:
        (Kp, M), N = parts[0].shape, b.shape[1]
    K = Kp * np_
    gated = resid is not None
    whole = np_ > 1 or K <= MM_K_WHOLE
    tm = _pick(seq if gated else M, (1024, 1408, 512, 384, 256, 128) if K <= MM_K_WHOLE else (512, 256, 128))
    tn = _pick(N, (512, 1408, 384, 256, 128))
    tk = K if whole else _pick(K, (2816, 2048, 1024, 512))
    if mode == "tn":
        tm = _pick(M, (1024, 1408, 512, 384, 256, 128))
    nk = K // tk
    dims = {"nn": ((1,), (0,)), "nt": ((1,), (1,)), "tn": ((0,), (0,))}[mode]

    def body(*refs):
        a_refs, b_ref, rest = refs[:np_], refs[np_], refs[np_ + 1:]
        if rope is not None:
            cos_ref, sin_ref, rest = rest[0], rest[1], rest[2:]
        if gated:
            r_ref, g_ref, o_ref, y_ref, acc = rest
        else:
            o_ref, acc = rest

        def finish(res):
            if gated:
                y_ref[...] = res
                o_ref[...] = r_ref[...] + g_ref[0] * res
            elif rope is not None:
                for h in range(tn // 256):
                    lo = h * 256
                    o_ref[:, lo:lo + NOPE] = (res[:, lo:lo + NOPE] * Q_FOLD).astype(out_dtype)
                    o_ref[:, lo + NOPE:lo + 256] = (_rope(res[:, lo + NOPE:lo + 256], cos_ref[...], sin_ref[...]) * Q_FOLD).astype(out_dtype)
            else:
                o_ref[...] = res.astype(out_dtype)

        prod = None
        for p, a_ref in enumerate(a_refs):
            if np_ == 1:
                bv = b_ref[...]
            else:
                bv = b_ref[:, p * Kp:(p + 1) * Kp] if mode == "nt" else b_ref[p * Kp:(p + 1) * Kp, :]
            term = lax.dot_general(a_ref[...].astype(bf16), bv.astype(bf16), (dims, ((), ())), preferred_element_type=f32)
            prod = term if prod is None else prod + term
        if nk == 1:
            finish(prod)
        else:
            k = pl.program_id(2)

            @pl.when(k == 0)
            def _():
                acc[...] = prod

            @pl.when(k > 0)
            def _():
                acc[...] += prod

            @pl.when(k == nk - 1)
            def _():
                finish(acc[...])

    if np_ > 1:
        a_spec = pl.BlockSpec((tm, Kp), lambda i, j, k: (i, 0))
    elif mode == "tn":
        a_spec = pl.BlockSpec((tk, tm), lambda i, j, k: (k, i))
    else:
        a_spec = pl.BlockSpec((tm, tk), lambda i, j, k: (i, k))
    b_spec = pl.BlockSpec((tn, tk), lambda i, j, k: (j, k)) if mode == "nt" else pl.BlockSpec((tk, tn), lambda i, j, k: (k, j))
    o_spec = pl.BlockSpec((tm, tn), lambda i, j, k: (i, j))
    in_specs, args = [a_spec] * np_ + [b_spec], parts + [b]
    if rope is not None:
        in_specs += [pl.BlockSpec((tm, 128), lambda i, j, k: (i, 0))] * 2
        args += list(rope)
    out_specs, out_shape = [o_spec], [jax.ShapeDtypeStruct((M, N), out_dtype)]
    if gated:
        per = seq // tm
        in_specs += [o_spec, pl.BlockSpec((1, 1, tn), lambda i, j, k: (i // per, 0, j))]
        args += [resid, gate]
        out_specs = [o_spec, o_spec]
        out_shape = [jax.ShapeDtypeStruct((M, N), f32), jax.ShapeDtypeStruct((M, N), f32)]
    grid = (M // tm, N // tn, nk)
    body, c_args, c_in, c_out, c_shapes, c_sems = _fuse_exchange(body, comm, len(args), len(out_specs), 1, grid)
    res = pl.pallas_call(body, name=name + "_x" if comm else name, grid=grid, in_specs=in_specs + c_in, out_specs=out_specs + c_out,
                         out_shape=out_shape + c_shapes, scratch_shapes=[pltpu.VMEM((tm, tn), f32)] + c_sems,
                         compiler_params=_cparams(3))(*args, *c_args)
    own = res[:len(out_specs)]
    own = own[0] if len(own) == 1 else tuple(own)
    return (own, list(res[len(out_specs):])) if comm else own


def _tok(ts, width, cb=0):
    return pl.BlockSpec((1, ts, width), lambda b, s: (b, s, cb))


def _perb(rows, width):
    return pl.BlockSpec((1, rows, width), lambda b, s: (b, 0, 0))


def _const(rows, width):
    return pl.BlockSpec((rows, width), lambda b, s: (0, 0))


def _row_call(body, name, B, S, ts, in_specs, out_specs, out_shape, scratch=(), aliases=None):
    return pl.pallas_call(body, name=name, grid=(B, S // ts), in_specs=in_specs, out_specs=out_specs,
                          out_shape=out_shape, scratch_shapes=list(scratch), input_output_aliases=aliases or {},
                          compiler_params=_cparams(2))


def _first():
    return (pl.program_id(0) == 0) & (pl.program_id(1) == 0)


def normmod_fwd(x3, mod, g, i_sh, i_sc):
    B, S, C = x3.shape
    ts = _pick(S, (512, 256, 128))

    def body(x_ref, mod_ref, g_ref, h_ref):
        x = x_ref[0]
        r = lax.rsqrt(jnp.mean(x * x, axis=-1, keepdims=True) + EPS)
        n = x * r * g_ref[...]
        h_ref[0] = (n * (1.0 + mod_ref[0, i_sc:i_sc + 1, :]) + mod_ref[0, i_sh:i_sh + 1, :]).astype(bf16)

    return _row_call(body, "normmod_fwd", B, S, ts, [_tok(ts, C), _perb(6, C), _const(1, C)], _tok(ts, C),
                     jax.ShapeDtypeStruct((B, S, C), bf16))(x3, mod, g)


def normmod_bwd(x3, dh3, resid3, mod, g, i_sc, y3=None, i_g=None):
    B, S, C = x3.shape
    ts = _pick(S, (512, 256, 128))
    gated = y3 is not None

    def body(x_ref, dh_ref, r_ref, mod_ref, g_ref, *rest):
        if gated:
            y_ref, dx_ref, dsh_ref, dsc_ref, dg_ref, dy_ref, dgate_ref = rest
        else:
            dx_ref, dsh_ref, dsc_ref, dg_ref = rest

        @pl.when(pl.program_id(1) == 0)
        def _():
            dsh_ref[...] = jnp.zeros_like(dsh_ref)
            dsc_ref[...] = jnp.zeros_like(dsc_ref)
            if gated:
                dgate_ref[...] = jnp.zeros_like(dgate_ref)

        @pl.when(_first())
        def _():
            dg_ref[...] = jnp.zeros_like(dg_ref)

        x, dh, gv = x_ref[0], dh_ref[0].astype(f32), g_ref[...]
        r = lax.rsqrt(jnp.mean(x * x, axis=-1, keepdims=True) + EPS)
        xh = x * r
        dn = dh * (1.0 + mod_ref[0, i_sc:i_sc + 1, :])
        dsh_ref[0] += _rowsum(dh)
        dsc_ref[0] += _rowsum(dh * xh * gv)
        dg_ref[...] += _rowsum(dn * xh)
        dxh = dn * gv
        dx = r * (dxh - xh * jnp.mean(dxh * xh, axis=-1, keepdims=True)) + r_ref[0]
        dx_ref[0] = dx
        if gated:
            dy_ref[0] = (dx * mod_ref[0, i_g:i_g + 1, :]).astype(bf16)
            dgate_ref[0] += _rowsum(dx * y_ref[0])

    in_specs = [_tok(ts, C), _tok(ts, C), _tok(ts, C), _perb(6, C), _const(1, C)]
    out_specs = [_tok(ts, C), _perb(1, C), _perb(1, C), _const(1, C)]
    out_shape = [jax.ShapeDtypeStruct((B, S, C), f32), jax.ShapeDtypeStruct((B, 1, C), f32),
                 jax.ShapeDtypeStruct((B, 1, C), f32), jax.ShapeDtypeStruct((1, C), f32)]
    args = [x3, dh3, resid3, mod, g]
    if gated:
        in_specs.append(_tok(ts, C))
        args.append(y3)
        out_specs += [_tok(ts, C), _perb(1, C)]
        out_shape += [jax.ShapeDtypeStruct((B, S, C), bf16), jax.ShapeDtypeStruct((B, 1, C), f32)]
    return _row_call(body, "normmod_gate_bwd" if gated else "normmod_bwd", B, S, ts, in_specs, out_specs, out_shape)(*args)


def gate_bwd(dx3, y3, mod, i_g):
    B, S, C = dx3.shape
    ts = _pick(S, (512, 256, 128))

    def body(dx_ref, y_ref, mod_ref, dy_ref, dgate_ref):
        @pl.when(pl.program_id(1) == 0)
        def _():
            dgate_ref[...] = jnp.zeros_like(dgate_ref)

        dx = dx_ref[0]
        dy_ref[0] = (dx * mod_ref[0, i_g:i_g + 1, :]).astype(bf16)
        dgate_ref[0] += _rowsum(dx * y_ref[0])

    return _row_call(body, "gate_bwd", B, S, ts, [_tok(ts, C), _tok(ts, C), _perb(6, C)], [_tok(ts, C), _perb(1, C)],
                     [jax.ShapeDtypeStruct((B, S, C), bf16), jax.ShapeDtypeStruct((B, 1, C), f32)])(dx3, y3, mod)


def rms_fwd(src3, width, cb, n, g, name):
    B, S, _ = src3.shape
    ts = _pick(S, (512, 256, 128))

    def body(x_ref, g_ref, o_ref):
        x = x_ref[0][:, :n]
        r = lax.rsqrt(jnp.mean(x * x, axis=-1, keepdims=True) + EPS)
        o_ref[0] = (x * r * g_ref[...]).astype(bf16)

    return _row_call(body, name, B, S, ts, [_tok(ts, width, cb), _const(1, n)], _tok(ts, n),
                     jax.ShapeDtypeStruct((B, S, n), bf16))(src3, g)


def rms_bwd_into(src3, width, cb, n, dout3, g, dp, name):
    B, S, _ = src3.shape
    ts = _pick(S, (512, 256, 128))

    def body(x_ref, do_ref, g_ref, dp_in, dp_ref, dg_ref):
        @pl.when(_first())
        def _():
            dg_ref[...] = jnp.zeros_like(dg_ref)

        x = x_ref[0][:, :n]
        do = do_ref[0]
        r = lax.rsqrt(jnp.mean(x * x, axis=-1, keepdims=True) + EPS)
        xh = x * r
        dg_ref[...] += _rowsum(do * xh)
        dxh = do * g_ref[...]
        dp_ref[0, :, :n] = (r * (dxh - xh * jnp.mean(dxh * xh, axis=-1, keepdims=True))).astype(bf16)
        if width > n:
            dp_ref[0, :, n:] = jnp.zeros((ts, width - n), bf16)

    return _row_call(body, name, B, S, ts, [_tok(ts, width, cb), _tok(ts, n), _const(1, n), pl.BlockSpec(memory_space=pl.ANY)],
                     [_tok(ts, width, cb), _const(1, n)], [jax.ShapeDtypeStruct(dp.shape, bf16), jax.ShapeDtypeStruct((1, n), f32)],
                     aliases={3: 0})(src3, dout3, g, dp)


def final_loss(x3, g, tgt3):
    B, S, C = x3.shape
    ts = _pick(S, (512, 256, 128))

    def body(x_ref, g_ref, t_ref, dx_ref, dg_ref, loss_ref):
        @pl.when(_first())
        def _():
            dg_ref[...] = jnp.zeros_like(dg_ref)
            loss_ref[...] = jnp.zeros_like(loss_ref)

        x, gv = x_ref[0], g_ref[...]
        r = lax.rsqrt(jnp.mean(x * x, axis=-1, keepdims=True) + EPS)
        xh = x * r
        e = xh * gv - t_ref[0]
        loss_ref[...] += 0.5 * jnp.sum(e * e) / C
        dout = e / C
        dg_ref[...] += _rowsum(dout * xh)
        dxh = dout * gv
        dx_ref[0] = r * (dxh - xh * jnp.mean(dxh * xh, axis=-1, keepdims=True))

    return _row_call(body, "final_loss", B, S, ts, [_tok(ts, C), _const(1, C), _tok(ts, C)],
                     [_tok(ts, C), _const(1, C), _const(1, 128)],
                     [jax.ShapeDtypeStruct((B, S, C), f32), jax.ShapeDtypeStruct((1, C), f32),
                      jax.ShapeDtypeStruct((1, 128), f32)])(x3, g, tgt3)


def ssd_out_fwd(yc3, p3, w):
    B, S, C = yc3.shape
    ts = _pick(S, (512, 256, 128))
    half = C // 2

    def body(y_ref, z_ref, w_ref, o_ref):
        y = y_ref[0] * _silu(z_ref[0])
        for lo in (0, half):
            yg = y[:, lo:lo + half]
            r = lax.rsqrt(jnp.mean(yg * yg, axis=-1, keepdims=True) + EPS)
            o_ref[0, :, lo:lo + half] = (yg * r * w_ref[:, lo:lo + half]).astype(bf16)

    return _row_call(body, "ssd_out_fwd", B, S, ts, [_tok(ts, C), _tok(ts, C, 0), _const(1, C)], _tok(ts, C),
                     jax.ShapeDtypeStruct((B, S, C), bf16))(yc3, p3, w)


def ssd_out_bwd(yc3, p3, dcat3, w):
    B, S, C = yc3.shape
    ts = _pick(S, (512, 256, 128))
    half = C // 2

    def body(y_ref, z_ref, do_ref, w_ref, dyc_ref, dz_ref, dw_ref):
        @pl.when(_first())
        def _():
            dw_ref[...] = jnp.zeros_like(dw_ref)

        yc, z, do = y_ref[0], z_ref[0], do_ref[0].astype(f32)
        sz = _silu(z)
        y = yc * sz
        for lo in (0, half):
            sl = slice(lo, lo + half)
            yg, dog, wg = y[:, sl], do[:, sl], w_ref[:, sl]
            r = lax.rsqrt(jnp.mean(yg * yg, axis=-1, keepdims=True) + EPS)
            yh = yg * r
            dw_ref[:, sl] += _rowsum(dog * yh)
            dyh = dog * wg
            dy = r * (dyh - yh * jnp.mean(dyh * yh, axis=-1, keepdims=True))
            dyc_ref[0, :, sl] = dy * sz[:, sl]
            dz_ref[0, :, sl] = (dy * yc[:, sl] * _dsilu(z[:, sl])).astype(bf16)

    return _row_call(body, "ssd_out_bwd", B, S, ts, [_tok(ts, C), _tok(ts, C, 0), _tok(ts, C, 0), _const(1, C)],
                     [_tok(ts, C), _tok(ts, C, 0), _const(1, C)],
                     [jax.ShapeDtypeStruct((B, S, C), f32), jax.ShapeDtypeStruct((B, S, IN_COLS), bf16),
                      jax.ShapeDtypeStruct((1, C), f32)])(yc3, p3, dcat3, w)


def _rot(t):
    lane = lax.broadcasted_iota(jnp.int32, t.shape, 1)
    return jnp.where(lane < ROPE // 2, -pltpu.roll(t, 128 - ROPE // 2, 1), pltpu.roll(t, ROPE // 2, 1))


def _rope(t, cosf, sinf):
    return t * cosf + _rot(t) * sinf


def _rope_t(d, cosf, sinf):
    return d * cosf - _rot(d * sinf)


def k_proj(ckvn, w, p3, cosf, sinf):
    B, S, _ = ckvn.shape
    ts = _pick(S, (1024, 512, 256, 128))

    def body(x_ref, w_ref, kr_ref, c_ref, s_ref, o_ref):
        lane = lax.broadcasted_iota(jnp.int32, (1, 128), 1)
        kr = _rope(jnp.where(lane < ROPE, kr_ref[0], 0.0), c_ref[0], s_ref[0]).astype(bf16)
        res = jnp.dot(x_ref[0], w_ref[...], preferred_element_type=f32)
        for h in range(2):
            o_ref[0, :, h * 256:h * 256 + NOPE] = res[:, h * NOPE:(h + 1) * NOPE].astype(bf16)
            o_ref[0, :, h * 256 + NOPE:(h + 1) * 256] = kr

    return pl.pallas_call(body, name="k_proj", grid=(B, S // ts, MLA_H // 2),
                          in_specs=[pl.BlockSpec((1, ts, KV_RANK), lambda b, s, j: (b, s, 0)),
                                    pl.BlockSpec((KV_RANK, 2 * NOPE), lambda b, s, j: (0, j)),
                                    pl.BlockSpec((1, ts, 128), lambda b, s, j: (b, s, OFF_KRDT // 128)),
                                    pl.BlockSpec((1, ts, 128), lambda b, s, j: (b, s, 0)),
                                    pl.BlockSpec((1, ts, 128), lambda b, s, j: (b, s, 0))],
                          out_specs=pl.BlockSpec((1, ts, 512), lambda b, s, j: (b, s, j)),
                          out_shape=jax.ShapeDtypeStruct((B, S, MLA_H * 256), bf16), compiler_params=_cparams(3))(ckvn, w, p3, cosf, sinf)


def kprep_bwd(dkr3, ddt3, cosf, sinf, dp):
    B, S, W = dkr3.shape
    ts = _pick(S, (512, 256, 128))

    def body(dk_ref, ddt_ref, c_ref, s_ref, dp_in, kr_ref):
        acc = dk_ref[0, :, 0:128]
        for h in range(1, W // 128):
            acc = acc + dk_ref[0, :, h * 128:(h + 1) * 128]
        lane = lax.broadcasted_iota(jnp.int32, (1, 128), 1)
        kr_ref[0] = jnp.where(lane < ROPE, _rope_t(acc, c_ref[0], s_ref[0]), ddt_ref[0]).astype(bf16)

    return _row_call(body, "kprep_bwd", B, S, ts,
                     [_tok(ts, W), _tok(ts, 128), _tok(ts, 128), _tok(ts, 128), pl.BlockSpec(memory_space=pl.ANY)],
                     _tok(ts, 128, OFF_KRDT // 128), jax.ShapeDtypeStruct(dp.shape, bf16), aliases={4: 0})(dkr3, ddt3, cosf, sinf, dp)


def _shift_down(u, j):
    if j == 0:
        return u
    row = lax.broadcasted_iota(jnp.int32, u.shape, 0)
    return jnp.where(row < j, 0.0, pltpu.roll(u, j, 0))


def _shift_up(u, j):
    if j == 0:
        return u
    n = u.shape[0]
    row = lax.broadcasted_iota(jnp.int32, u.shape, 0)
    return jnp.where(row >= n - j, 0.0, pltpu.roll(u, n - j, 0))


def _conv(u, w, b, K):
    out = b
    for j in range(K):
        out = out + w[K - 1 - j:K - j, :] * _shift_down(u, j)
    return out


def _conv_bwd(u, du, w, K):
    dins = w[K - 1:K, :] * du
    dws = [None] * K
    dws[K - 1] = _rowsum(du * u)
    for j in range(1, K):
        sd = _shift_up(du, j)
        dins = dins + w[K - 1 - j:K - j, :] * sd
        dws[K - 1 - j] = _rowsum(sd * u)
    return dins, dws


CW = 256


def conv_ssd_fwd(p3, w, b):
    B, S, _ = p3.shape
    nb = D_XBC // CW

    def body(u_ref, w_ref, b_ref, o_ref, pre_ref):
        pre = _conv(u_ref[0], w_ref[...], b_ref[...], CONV_K)
        o_ref[0] = _silu(pre)
        pre_ref[0] = pre.astype(bf16)

    out = pl.BlockSpec((1, S, CW), lambda b, j: (b, 0, j))
    return pl.pallas_call(body, name="conv_ssd_fwd", grid=(B, nb),
                          in_specs=[pl.BlockSpec((1, S, CW), lambda b, j: (b, 0, OFF_XBC // CW + j)),
                                    pl.BlockSpec((CONV_K, CW), lambda b, j: (0, j)),
                                    pl.BlockSpec((1, CW), lambda b, j: (0, j))],
                          out_specs=[out, out],
                          out_shape=[jax.ShapeDtypeStruct((B, S, D_XBC), f32), jax.ShapeDtypeStruct((B, S, D_XBC), bf16)],
                          compiler_params=_cparams(2))(p3, w, b)


def conv_ssd_bwd(p3, pre3, dxc3, w, dp):
    B, S, _ = p3.shape
    nb = D_XBC // CW

    def body(u_ref, pre_ref, d_ref, w_ref, dp_in, du_ref, dw_ref, db_ref):
        @pl.when(pl.program_id(1) == 0)
        def _():
            dw_ref[...] = jnp.zeros_like(dw_ref)
            db_ref[...] = jnp.zeros_like(db_ref)

        u, wv = u_ref[0], w_ref[...]
        dpre = d_ref[0] * _dsilu(pre_ref[0].astype(f32))
        dins, dws = _conv_bwd(u, dpre, wv, CONV_K)
        du_ref[0] = dins.astype(bf16)
        for k in range(CONV_K):
            dw_ref[k:k + 1, :] += dws[k]
        db_ref[...] += _rowsum(dpre)

    return pl.pallas_call(body, name="conv_ssd_bwd", grid=(nb, B),
                          in_specs=[pl.BlockSpec((1, S, CW), lambda j, b: (b, 0, OFF_XBC // CW + j)),
                                    pl.BlockSpec((1, S, CW), lambda j, b: (b, 0, j)),
                                    pl.BlockSpec((1, S, CW), lambda j, b: (b, 0, j)),
                                    pl.BlockSpec((CONV_K, CW), lambda j, b: (0, j)), pl.BlockSpec(memory_space=pl.ANY)],
                          out_specs=[pl.BlockSpec((1, S, CW), lambda j, b: (b, 0, OFF_XBC // CW + j)),
                                     pl.BlockSpec((CONV_K, CW), lambda j, b: (0, j)),
                                     pl.BlockSpec((1, CW), lambda j, b: (0, j))],
                          out_shape=[jax.ShapeDtypeStruct(dp.shape, bf16), jax.ShapeDtypeStruct((CONV_K, D_XBC), f32),
                                     jax.ShapeDtypeStruct((1, D_XBC), f32)], input_output_aliases={4: 0},
                          compiler_params=_cparams(2))(p3, pre3, dxc3, w, dp)


def glu_fwd(u3, w, b):
    B, S, _ = u3.shape
    nb = D_FF // CW

    def body(ug_ref, uv_ref, wg_ref, wv_ref, bg_ref, bv_ref, o_ref, g_ref, v_ref):
        g = _conv(ug_ref[0].astype(f32), wg_ref[...], bg_ref[...], FF_K)
        v = _conv(uv_ref[0].astype(f32), wv_ref[...], bv_ref[...], FF_K)
        o_ref[0] = (_silu(g) * v).astype(bf16)
        g_ref[0] = g.astype(bf16)
        v_ref[0] = v.astype(bf16)

    def blk(off):
        return pl.BlockSpec((1, S, CW), lambda b, j: (b, 0, off + j))

    def par(rows, off):
        return pl.BlockSpec((rows, CW), lambda b, j: (0, off + j))

    shp = jax.ShapeDtypeStruct((B, S, D_FF), bf16)
    return pl.pallas_call(body, name="glu_fwd", grid=(B, nb),
                          in_specs=[blk(0), blk(nb), par(FF_K, 0), par(FF_K, nb), par(1, 0), par(1, nb)],
                          out_specs=[blk(0)] * 3, out_shape=[shp] * 3, compiler_params=_cparams(2))(u3, u3, w, w, b, b)


def glu_bwd(u3, g3, v3, da3, w):
    B, S, _ = u3.shape
    nb = D_FF // CW

    def body(ug_ref, uv_ref, g_ref, v_ref, da_ref, wg_ref, wv_ref, dug_ref, duv_ref, dwg_ref, dwv_ref, dbg_ref, dbv_ref):
        @pl.when(pl.program_id(1) == 0)
        def _():
            for r in (dwg_ref, dwv_ref, dbg_ref, dbv_ref):
                r[...] = jnp.zeros_like(r)

        ug, uv, da, wg, wv = ug_ref[0].astype(f32), uv_ref[0].astype(f32), da_ref[0].astype(f32), wg_ref[...], wv_ref[...]
        g, v = g_ref[0].astype(f32), v_ref[0].astype(f32)
        dg = da * v * _dsilu(g)
        dv = da * _silu(g)
        ding, dwsg = _conv_bwd(ug, dg, wg, FF_K)
        dinv, dwsv = _conv_bwd(uv, dv, wv, FF_K)
        dug_ref[0] = ding.astype(bf16)
        duv_ref[0] = dinv.astype(bf16)
        for k in range(FF_K):
            dwg_ref[k:k + 1, :] += dwsg[k]
            dwv_ref[k:k + 1, :] += dwsv[k]
        dbg_ref[...] += _rowsum(dg)
        dbv_ref[...] += _rowsum(dv)

    def blk(off):
        return pl.BlockSpec((1, S, CW), lambda j, b: (b, 0, off + j))

    def par(rows, off):
        return pl.BlockSpec((rows, CW), lambda j, b: (0, off + j))

    return pl.pallas_call(body, name="glu_bwd", grid=(nb, B),
                          in_specs=[blk(0), blk(nb), blk(0), blk(0), blk(0), par(FF_K, 0), par(FF_K, nb)],
                          out_specs=[blk(0), blk(0), par(FF_K, 0), par(FF_K, 0), par(1, 0), par(1, 0)],
                          out_shape=[jax.ShapeDtypeStruct((B, S, D_FF), bf16), jax.ShapeDtypeStruct((B, S, D_FF), bf16),
                                     jax.ShapeDtypeStruct((FF_K, D_FF), f32), jax.ShapeDtypeStruct((FF_K, D_FF), f32),
                                     jax.ShapeDtypeStruct((1, D_FF), f32), jax.ShapeDtypeStruct((1, D_FF), f32)],
                          compiler_params=_cparams(2))(u3, u3, g3, v3, da3, w, w)


def _ssd_decay(dtb, bias_row, alog_row):
    lane = lax.broadcasted_iota(jnp.int32, (1, 128), 1)
    hmask = (lane >= DT_LANE) & (lane < DT_LANE + SSD_HEADS)
    dt = jnp.where(hmask, jax.nn.softplus(dtb + bias_row), 0.0)
    a = dt * jnp.where(hmask, -jnp.exp(alog_row), 0.0)
    r = lax.broadcasted_iota(jnp.int32, (CHUNK, CHUNK), 0)
    c = lax.broadcasted_iota(jnp.int32, (CHUNK, CHUNK), 1)
    cs = jnp.dot((r >= c).astype(f32), a, precision=HIGHEST, preferred_element_type=f32)
    return dt, cs


def _expand(xt):
    return jnp.concatenate([jnp.broadcast_to(xt[DT_LANE + h:DT_LANE + h + 1, :], (SSD_HD, xt.shape[1]))
                            for h in range(SSD_HEADS)], axis=0)


_NT = (((1,), (1,)), ((), ()))
_TN = (((0,), (0,)), ((), ()))
GH = SSD_HEADS // 2
GR = GH * SSD_HD


def ssd_fwd(xc3, p3, bias_row, alog_row, dcol):
    B, S, _ = xc3.shape
    nc = S // CHUNK

    def body(xs_ref, bc_ref, dtb_ref, bias_ref, alog_ref, dcol_ref, y_ref, st_ref, state, yT):
        @pl.when(pl.program_id(1) == 0)
        def _():
            state[...] = jnp.zeros_like(state)

        dt, cs = _ssd_decay(dtb_ref[0], bias_ref[...], alog_ref[...])
        csT = cs.T
        eT = jnp.exp(csT)
        decX = _expand(jnp.exp(csT[:, CHUNK - 1:CHUNK] - csT))
        eX = _expand(eT)
        elastX = eX[:, CHUNK - 1:CHUNK]
        xsT = xs_ref[0].T
        uT = xsT * _expand(dt.T)
        bc = bc_ref[0]
        st_ref[0, 0] = state[...]
        srow = lax.broadcasted_iota(jnp.int32, (CHUNK, CHUNK), 0)
        lcol = lax.broadcasted_iota(jnp.int32, (CHUNK, CHUNK), 1)
        for g in range(2):
            Bg = bc[:, g * SSD_N:(g + 1) * SSD_N].astype(bf16)
            Cg = bc[:, (2 + g) * SSD_N:(3 + g) * SSD_N].astype(bf16)
            GT = lax.dot_general(Bg, Cg, _NT, preferred_element_type=f32)
            rows = slice(g * GR, (g + 1) * GR)
            Sg = state[rows]
            yoffT = lax.dot_general(Sg.astype(bf16), Cg, _NT, preferred_element_type=f32) * eX[rows]
            state[rows] = Sg * elastX[rows] + jnp.dot((uT[rows] * decX[rows]).astype(bf16), Bg, preferred_element_type=f32)
            for k in range(GH):
                h = g * GH + k
                hr = slice(h * SSD_HD, (h + 1) * SSD_HD)
                seg = csT[DT_LANE + h:DT_LANE + h + 1, :] - cs[:, DT_LANE + h:DT_LANE + h + 1]
                LT = jnp.where(lcol >= srow, jnp.exp(jnp.minimum(seg, 0.0)), 0.0)
                yT[hr] = (jnp.dot(uT[hr].astype(bf16), (GT * LT).astype(bf16), preferred_element_type=f32)
                          + yoffT[k * SSD_HD:(k + 1) * SSD_HD] + dcol_ref[hr] * xsT[hr])
        y_ref[0] = yT[...].T

    return pl.pallas_call(body, name="ssd_fwd", grid=(B, nc),
                          in_specs=[pl.BlockSpec((1, CHUNK, D_SSD), lambda b, c: (b, c, 0)),
                                    pl.BlockSpec((1, CHUNK, 512), lambda b, c: (b, c, 2)),
                                    pl.BlockSpec((1, CHUNK, 128), lambda b, c: (b, c, OFF_KRDT // 128)),
                                    _const(1, 128), _const(1, 128), _const(D_SSD, 1)],
                          out_specs=[pl.BlockSpec((1, CHUNK, D_SSD), lambda b, c: (b, c, 0)),
                                     pl.BlockSpec((1, 1, D_SSD, SSD_N), lambda b, c: (b, c, 0, 0))],
                          out_shape=[jax.ShapeDtypeStruct((B, S, D_SSD), f32), jax.ShapeDtypeStruct((B, nc, D_SSD, SSD_N), f32)],
                          scratch_shapes=[pltpu.VMEM((D_SSD, SSD_N), f32), pltpu.VMEM((D_SSD, CHUNK), f32)],
                          compiler_params=_cparams(2))(xc3, xc3, p3, bias_row, alog_row, dcol)


def ssd_bwd(xc3, p3, dy3, states, bias_row, alog_row, bias_col, alog_col, dcol, comm=None):
    B, S, _ = xc3.shape
    nc = S // CHUNK

    def body(xs_ref, bc_ref, dtb_ref, dy_ref, st_ref, bias_ref, alog_ref, biasc_ref, alogc_ref, dcol_ref,
             dxc_ref, ddt_ref, dalog_ref, dd_ref, dbias_ref, dS, dUT, accA, accD, accB, dcs_diag):
        @pl.when(pl.program_id(1) == 0)
        def _():
            dS[...] = jnp.zeros_like(dS)

        @pl.when(_first())
        def _():
            accA[...] = jnp.zeros_like(accA)
            accD[...] = jnp.zeros_like(accD)
            accB[...] = jnp.zeros_like(accB)

        dtb = dtb_ref[0]
        dt, cs = _ssd_decay(dtb, bias_ref[...], alog_ref[...])
        dtT, csT = dt.T, cs.T
        decX = _expand(jnp.exp(csT[:, CHUNK - 1:CHUNK] - csT))
        eX = _expand(jnp.exp(csT))
        dtX = _expand(dtT)
        elastX = eX[:, CHUNK - 1:CHUNK]
        xsT = xs_ref[0].T
        uT = xsT * dtX
        dYT = dy_ref[0].T
        bc = bc_ref[0]
        lrow = lax.broadcasted_iota(jnp.int32, (CHUNK, CHUNK), 0)
        scol = lax.broadcasted_iota(jnp.int32, (CHUNK, CHUNK), 1)
        dcs_diag[...] = jnp.zeros_like(dcs_diag)
        rs_cols = jnp.zeros((CHUNK, 128), f32)
        vparts, zparts = [], []
        for g in range(2):
            Bf = bc[:, g * SSD_N:(g + 1) * SSD_N]
            Bg = Bf.astype(bf16)
            Cg = bc[:, (2 + g) * SSD_N:(3 + g) * SSD_N].astype(bf16)
            G = lax.dot_general(Cg, Bg, _NT, preferred_element_type=f32)
            BgT = Bf.T.astype(bf16)
            rows = slice(g * GR, (g + 1) * GR)
            dSg = dS[rows]
            Sg = st_ref[0, 0, rows, :]
            dUst = jnp.dot(dSg.astype(bf16), BgT, preferred_element_type=f32) * decX[rows]
            yoffT = lax.dot_general(Sg.astype(bf16), Cg, _NT, preferred_element_type=f32) * eX[rows]
            zparts.append(dYT[rows] * yoffT - dUst * uT[rows])
            dG = jnp.zeros((CHUNK, CHUNK), f32)
            for k in range(GH):
                h = g * GH + k
                hr = slice(h * SSD_HD, (h + 1) * SSD_HD)
                seg = cs[:, DT_LANE + h:DT_LANE + h + 1] - csT[DT_LANE + h:DT_LANE + h + 1, :]
                L = jnp.where(lrow >= scol, jnp.exp(jnp.minimum(seg, 0.0)), 0.0)
                M = G * L
                dYh = dYT[hr].astype(bf16)
                dUT[hr] = jnp.dot(dYh, M.astype(bf16), preferred_element_type=f32) + dUst[k * SSD_HD:(k + 1) * SSD_HD]
                dM = lax.dot_general(dYh, uT[hr].astype(bf16), _TN, preferred_element_type=f32)
                dG = dG + dM * L
                Wm = dM * M
                rs_cols = jnp.where(scol == DT_LANE + h, jnp.sum(Wm, axis=1, keepdims=True), rs_cols)
                dcs_diag[DT_LANE + h:DT_LANE + h + 1, :] = -_rowsum(Wm)
            dGb = dG.astype(bf16)
            dYe = (dYT[rows] * eX[rows]).astype(bf16)
            ude = (uT[rows] * decX[rows]).astype(bf16)
            dC = jnp.dot(dGb, Bg, preferred_element_type=f32) + lax.dot_general(dYe, Sg.astype(bf16), _TN, preferred_element_type=f32)
            dB = (lax.dot_general(dGb, Cg, _TN, preferred_element_type=f32)
                  + lax.dot_general(ude, dSg.astype(bf16), _TN, preferred_element_type=f32))
            dxc_ref[0, :, D_SSD + g * SSD_N:D_SSD + (g + 1) * SSD_N] = dB
            dxc_ref[0, :, D_SSD + (2 + g) * SSD_N:D_SSD + (3 + g) * SSD_N] = dC
            vparts.append(elastX[rows] * jnp.sum(dSg * Sg, axis=1, keepdims=True)
                          + jnp.sum(dUst * uT[rows], axis=1, keepdims=True))
            dS[rows] = elastX[rows] * dSg + jnp.dot(dYe, Cg, preferred_element_type=f32)
        dU = dUT[...]
        dcv = dcol_ref[...]
        dxc_ref[0, :, 0:D_SSD] = (dtX * dU + dcv * dYT).T
        lane = lax.broadcasted_iota(jnp.int32, (D_SSD, CHUNK), 1)
        Z = jnp.concatenate(zparts, axis=0) + jnp.where(lane == CHUNK - 1, jnp.concatenate(vparts, axis=0), 0.0)
        hr_ = lax.broadcasted_iota(jnp.int32, (128, D_SSD), 0)
        hc_ = lax.broadcasted_iota(jnp.int32, (128, D_SSD), 1)
        hsel = (hr_ - DT_LANE == jnp.right_shift(hc_, 6)).astype(bf16)
        summands = jnp.concatenate([Z, dU * xsT, dYT * xsT], axis=1)
        hi = summands.astype(bf16)
        lo = (summands - hi.astype(f32)).astype(bf16)
        red = jnp.dot(hsel, hi, preferred_element_type=f32) + jnp.dot(hsel, lo, preferred_element_type=f32)
        dcsT = red[:, 0:CHUNK] + dcs_diag[...] + rs_cols.T
        daT = jnp.dot(dcsT, (lrow >= scol).astype(f32), precision=HIGHEST, preferred_element_type=f32)
        rowi = lax.broadcasted_iota(jnp.int32, (128, 1), 0)
        hmask = (rowi >= DT_LANE) & (rowi < DT_LANE + SSD_HEADS)
        a_col = jnp.where(hmask, -jnp.exp(alogc_ref[...]), 0.0)
        ddtT = red[:, CHUNK:2 * CHUNK] + a_col * daT
        ddt_rawT = jnp.where(hmask, ddtT * jax.nn.sigmoid(dtb.T + biasc_ref[...]), 0.0)
        ddt_ref[0] = ddt_rawT.T
        accA[...] += daT * dtT
        accD[...] += red[:, 2 * CHUNK:3 * CHUNK]
        accB[...] += ddt_rawT

        @pl.when((pl.program_id(0) == B - 1) & (pl.program_id(1) == nc - 1))
        def _():
            dalog_ref[...] = jnp.broadcast_to(jnp.sum(accA[...], axis=1, keepdims=True) * a_col, (128, 128))
            dd_ref[...] = jnp.broadcast_to(jnp.sum(accD[...], axis=1, keepdims=True), (128, 128))
            dbias_ref[...] = jnp.broadcast_to(jnp.sum(accB[...], axis=1, keepdims=True), (128, 128))

    def rev(width, cb):
        return pl.BlockSpec((1, CHUNK, width), lambda b, c: (b, nc - 1 - c, cb))

    acc_spec = pl.BlockSpec((128, 128), lambda b, c: (0, 0))
    acc_shape = jax.ShapeDtypeStruct((128, 128), f32)
    body, c_args, c_in, c_out, c_shapes, c_sems = _fuse_exchange(body, comm, 10, 5, 6, (B, nc))
    res = pl.pallas_call(body, name="ssd_bwd_x" if comm else "ssd_bwd", grid=(B, nc),
                         in_specs=[rev(D_SSD, 0), rev(512, 2), rev(128, OFF_KRDT // 128), rev(D_SSD, 0),
                                   pl.BlockSpec((1, 1, D_SSD, SSD_N), lambda b, c: (b, nc - 1 - c, 0, 0)),
                                   _const(1, 128), _const(1, 128), _const(128, 1), _const(128, 1), _const(D_SSD, 1)] + c_in,
                         out_specs=[rev(D_XBC, 0), rev(128, 0), acc_spec, acc_spec, acc_spec] + c_out,
                         out_shape=[jax.ShapeDtypeStruct((B, S, D_XBC), f32), jax.ShapeDtypeStruct((B, S, 128), f32),
                                    acc_shape, acc_shape, acc_shape] + c_shapes,
                         scratch_shapes=[pltpu.VMEM((D_SSD, SSD_N), f32), pltpu.VMEM((D_SSD, CHUNK), f32),
                                         pltpu.VMEM((128, 128), f32), pltpu.VMEM((128, 128), f32), pltpu.VMEM((128, 128), f32),
                                         pltpu.VMEM((128, 128), f32)] + c_sems,
                         compiler_params=_cparams(2))(xc3, xc3, p3, dy3, states, bias_row, alog_row, bias_col, alog_col, dcol, *c_args)
    return (*res[:5], list(res[5:]))


ATT_SCALE = float(QK) ** -0.5
LOG2E = 1.4426950408889634
LN2 = 0.6931471805599453
Q_FOLD = ATT_SCALE * LOG2E
NEG = -1e30
HP = 4


def _att_block(S):
    return _pick(S, (512, 256, 128))


def _tri_rows(t, n):
    i = sum([(t >= r * (r + 1) // 2).astype(jnp.int32) for r in range(1, n)], jnp.int32(0))
    return i, t - i * (i + 1) // 2


def _tri_cols(t, n):
    j = sum([(t >= r * n - r * (r - 1) // 2).astype(jnp.int32) for r in range(1, n)], jnp.int32(0))
    return j, j + t - (j * n - j * (j - 1) // 2)


def attn_fwd(q3, k3, v3, comm=None):
    B, S, _ = q3.shape
    bq = _att_block(S)
    nq = S // bq

    def body(q_ref, k_ref, v_ref, o_ref, lse_ref, m_s, l_s, acc):
        i, j = _tri_rows(pl.program_id(2), nq)

        @pl.when(j == 0)
        def _():
            m_s[...] = jnp.full_like(m_s, NEG)
            l_s[...] = jnp.zeros_like(l_s)
            acc[...] = jnp.zeros_like(acc)

        def step(masked):
            for t in range(HP):
                qk = slice(t * 256, (t + 1) * 256)
                st = lax.dot_general(k_ref[0, :, qk], q_ref[0, :, qk], _NT, preferred_element_type=f32)
                if masked:
                    r = lax.broadcasted_iota(jnp.int32, (bq, bq), 0)
                    c = lax.broadcasted_iota(jnp.int32, (bq, bq), 1)
                    st = jnp.where(c >= r, st, NEG)
                m_old = m_s[t]
                m_new = jnp.maximum(m_old, jnp.max(st, axis=0, keepdims=True))
                alpha = jnp.exp2(m_old - m_new)
                pt = jnp.exp2(st - m_new)
                l_s[t] = alpha * l_s[t] + jnp.sum(pt, axis=0, keepdims=True)
                acc[t] = alpha * acc[t] + lax.dot_general(v_ref[0, :, t * VD:(t + 1) * VD], pt.astype(bf16), _TN,
                                                          preferred_element_type=f32)
                m_s[t] = m_new

        @pl.when(j < i)
        def _():
            step(False)

        @pl.when(j == i)
        def _():
            step(True)
            for t in range(HP):
                o_ref[0, :, t * VD:(t + 1) * VD] = (acc[t] / l_s[t]).T
                lse_ref[0, t] = m_s[t] + jnp.log2(l_s[t])

    grid = (B, MLA_H // HP, nq * (nq + 1) // 2)
    qi = lambda t: _tri_rows(t, nq)[0]
    kj = lambda t: _tri_rows(t, nq)[1]
    body, c_args, c_in, c_out, c_shapes, c_sems = _fuse_exchange(body, comm, 3, 2, 3, grid)
    res = pl.pallas_call(body, name="attn_fwd_x" if comm else "attn_fwd", grid=grid,
                         in_specs=[pl.BlockSpec((1, bq, HP * 256), lambda b, h, t: (b, qi(t), h)),
                                   pl.BlockSpec((1, bq, HP * 256), lambda b, h, t: (b, kj(t), h)),
                                   pl.BlockSpec((1, bq, HP * VD), lambda b, h, t: (b, kj(t), h))] + c_in,
                         out_specs=[pl.BlockSpec((1, bq, HP * VD), lambda b, h, t: (b, qi(t), h)),
                                    pl.BlockSpec((1, HP, 1, bq), lambda b, h, t: (b, h, 0, qi(t)))] + c_out,
                         out_shape=[jax.ShapeDtypeStruct((B, S, MLA_H * VD), f32), jax.ShapeDtypeStruct((B, MLA_H, 1, S), f32)] + c_shapes,
                         scratch_shapes=[pltpu.VMEM((HP, 1, bq), f32), pltpu.VMEM((HP, 1, bq), f32), pltpu.VMEM((HP, VD, bq), f32)] + c_sems,
                         compiler_params=_cparams(3))(q3, k3, v3, *c_args)
    return res[0], res[1], list(res[2:])


def rms_o_bwd(o3, dcat3, g):
    B, S, C = o3.shape
    ts = _pick(S, (512, 256, 128))

    def body(x_ref, do_ref, g_ref, dx_ref, dg_ref, d_ref):
        @pl.when(_first())
        def _():
            dg_ref[...] = jnp.zeros_like(dg_ref)

        x, do = x_ref[0], do_ref[0].astype(f32)
        r = lax.rsqrt(jnp.mean(x * x, axis=-1, keepdims=True) + EPS)
        xh = x * r
        dg_ref[...] += _rowsum(do * xh)
        dxh = do * g_ref[...]
        dx = r * (dxh - xh * jnp.mean(dxh * xh, axis=-1, keepdims=True))
        dx_ref[0] = dx
        for h in range(MLA_H):
            vs = slice(h * VD, (h + 1) * VD)
            d_ref[0, h] = jnp.sum(dx[:, vs] * x[:, vs], axis=-1, keepdims=True)

    return _row_call(body, "rms_o_bwd", B, S, ts, [_tok(ts, C), _tok(ts, C, 1), _const(1, C)],
                     [_tok(ts, C), _const(1, C), pl.BlockSpec((1, MLA_H, ts, 1), lambda b, s: (b, 0, s, 0))],
                     [jax.ShapeDtypeStruct((B, S, C), f32), jax.ShapeDtypeStruct((1, C), f32),
                      jax.ShapeDtypeStruct((B, MLA_H, S, 1), f32)])(o3, dcat3, g)


def attn_bwd(q3, k3, v3, do3, lse_row, delta_row, cosf, sinf, comm=None):
    B, S, _ = q3.shape
    bq = _att_block(S)
    nq = S // bq

    def body(q_ref, k_ref, v_ref, do_ref, lse_ref, dl_ref, cos_ref, sin_ref, dkn_ref, dv_ref, dkr_ref, dq_hbm, dk_acc, dv_acc,
             dq_scr, stage, dq_sem):
        b, hp = pl.program_id(0), pl.program_id(1)
        j, i = _tri_cols(pl.program_id(2), nq)
        rows = pl.ds(pl.multiple_of(i * bq, bq), bq)

        @pl.when(pl.program_id(2) == 0)
        def _():
            dq_scr[...] = jnp.zeros_like(dq_scr)

        @pl.when(i == j)
        def _():
            dk_acc[...] = jnp.zeros_like(dk_acc)
            dv_acc[...] = jnp.zeros_like(dv_acc)

        def step(masked):
            for t in range(HP):
                qk, vs = slice(t * 256, (t + 1) * 256), slice(t * VD, (t + 1) * VD)
                q, k = q_ref[0, :, qk], k_ref[0, :, qk]
                do = do_ref[0, :, vs].astype(bf16)
                pt = jnp.exp2(lax.dot_general(k, q, _NT, preferred_element_type=f32) - lse_ref[0, t])
                if masked:
                    r = lax.broadcasted_iota(jnp.int32, (bq, bq), 0)
                    c = lax.broadcasted_iota(jnp.int32, (bq, bq), 1)
                    pt = jnp.where(c >= r, pt, 0.0)
                dv_acc[t] += jnp.dot(pt.astype(bf16), do, preferred_element_type=f32)
                dpt = lax.dot_general(v_ref[0, :, vs], do, _NT, preferred_element_type=f32)
                dst = (pt * (dpt - dl_ref[0, t])).astype(bf16)
                dk_acc[t] += jnp.dot(dst, q, preferred_element_type=f32)
                dq_scr[t, rows, :] += lax.dot_general(dst, k, _TN, preferred_element_type=f32)

        @pl.when(i > j)
        def _():
            step(False)

        @pl.when(i == j)
        def _():
            step(True)
            for t in range(HP):
                d = dq_scr[t, rows, :] * (LN2 * Q_FOLD)
                stage[t, :, 0:NOPE] = d[:, 0:NOPE].astype(bf16)
                stage[t, :, NOPE:] = _rope_t(d[:, NOPE:], cos_ref[0], sin_ref[0]).astype(bf16)
                cp = pltpu.make_async_copy(stage.at[t], dq_hbm.at[b, rows, pl.ds(pl.multiple_of((hp * HP + t) * 256, 256), 256)],
                                           dq_sem.at[t])
                cp.start()
                cp.wait()

        @pl.when(i == nq - 1)
        def _():
            kr = jnp.zeros((bq, 128), f32)
            for t in range(HP):
                dkn_ref[0, :, t * NOPE:(t + 1) * NOPE] = (dk_acc[t, :, 0:NOPE] * LN2).astype(bf16)
                dv_ref[0, :, t * VD:(t + 1) * VD] = dv_acc[t].astype(bf16)
                kr = kr + dk_acc[t, :, NOPE:]
            dkr_ref[0] = kr * LN2

    kj = lambda t: _tri_cols(t, nq)[0]
    qi = lambda t: _tri_cols(t, nq)[1]
    kspec = pl.BlockSpec((1, bq, HP * 256), lambda b, h, t: (b, kj(t), h))
    vspec = pl.BlockSpec((1, bq, HP * VD), lambda b, h, t: (b, kj(t), h))
    krspec = pl.BlockSpec((1, bq, 128), lambda b, h, t: (b, kj(t), h))
    rspec = pl.BlockSpec((1, HP, 1, bq), lambda b, h, t: (b, h, 0, qi(t)))
    tspec = pl.BlockSpec((1, bq, 128), lambda b, h, t: (b, qi(t), 0))
    grid = (B, MLA_H // HP, nq * (nq + 1) // 2)
    body, c_args, c_in, c_out, c_shapes, c_sems = _fuse_exchange(body, comm, 8, 4, 5, grid)
    res = pl.pallas_call(body, name="attn_bwd_x" if comm else "attn_bwd", grid=grid,
                         in_specs=[pl.BlockSpec((1, bq, HP * 256), lambda b, h, t: (b, qi(t), h)), kspec, vspec,
                                   pl.BlockSpec((1, bq, HP * VD), lambda b, h, t: (b, qi(t), h)), rspec, rspec,
                                   tspec, tspec] + c_in,
                         out_specs=[vspec, vspec, krspec, pl.BlockSpec(memory_space=pltpu.HBM)] + c_out,
                         out_shape=[jax.ShapeDtypeStruct((B, S, MLA_H * NOPE), bf16), jax.ShapeDtypeStruct((B, S, MLA_H * VD), bf16),
                                    jax.ShapeDtypeStruct((B, S, MLA_H // HP * 128), f32),
                                    jax.ShapeDtypeStruct((B, S, MLA_H * 256), bf16)] + c_shapes,
                         scratch_shapes=[pltpu.VMEM((HP, bq, 256), f32), pltpu.VMEM((HP, bq, VD), f32), pltpu.VMEM((HP, S, 256), f32),
                                         pltpu.VMEM((HP, bq, 256), bf16), pltpu.SemaphoreType.DMA((HP,))] + c_sems,
                         compiler_params=_cparams(3))(q3, k3, v3, do3, lse_row, delta_row, cosf, sinf, *c_args)
    return res[3], res[0], res[1], res[2], list(res[4:])


def ada_fwd(c_all, w, b):
    n = w.shape[1]

    def body(c_ref, w_ref, b_ref, o_ref):
        o_ref[...] = jnp.dot(_silu(c_ref[...]).astype(bf16), w_ref[...].astype(bf16), preferred_element_type=f32) + b_ref[...]

    return pl.pallas_call(body, name="ada_fwd", out_shape=jax.ShapeDtypeStruct((c_all.shape[0], n), f32),
                          compiler_params=pltpu.CompilerParams(vmem_limit_bytes=VMEM_LIMIT))(c_all, w, b)


def ada_bwd(c_all, dmod):
    n = dmod.shape[1]

    def body(c_ref, d_ref, o_ref):
        o_ref[...] = lax.dot_general(_silu(c_ref[...]).astype(bf16), d_ref[...].astype(bf16), _TN, preferred_element_type=f32)

    return pl.pallas_call(body, name="ada_bwd", out_shape=jax.ShapeDtypeStruct((c_all.shape[1], n), f32),
                          compiler_params=pltpu.CompilerParams(vmem_limit_bytes=VMEM_LIMIT))(c_all, dmod)


def sum_leading(x, name):
    n, R, _ = x.shape
    tr = _pick(R, (512, 256, 128, 64, 32, 16, 8))

    def body(x_ref, o_ref):
        acc = x_ref[0].astype(f32)
        for k in range(1, n):
            acc = acc + x_ref[k].astype(f32)
        o_ref[...] = acc

    return pl.pallas_call(body, name=name, grid=(R // tr,), in_specs=[pl.BlockSpec((n, tr, 128), lambda i: (0, i, 0))],
                          out_specs=pl.BlockSpec((tr, 128), lambda i: (i, 0)), out_shape=jax.ShapeDtypeStruct((R, 128), f32),
                          compiler_params=_cparams(1))(x)


def _adamw_body(w_ref, g_ref, m_ref, v_ref, d_ref, mo_ref, vo_ref):
    gv = g_ref[...]
    mn = ADAM_B1 * m_ref[...] + (1.0 - ADAM_B1) * gv
    vn = ADAM_B2 * v_ref[...] + (1.0 - ADAM_B2) * jnp.square(gv)
    m_hat = mn / (1.0 - ADAM_B1 ** ADAM_STEP)
    v_hat = vn / (1.0 - ADAM_B2 ** ADAM_STEP)
    d_ref[...] = -ADAM_LR * (m_hat / (jnp.sqrt(v_hat) + ADAM_EPS) + ADAM_WD * w_ref[...])
    mo_ref[...] = mn
    vo_ref[...] = vn


def adamw(w, g, m, v):
    R = w.shape[0]
    tr = _pick(R, (512, 256, 128, 64, 32, 16, 8))
    spec = pl.BlockSpec((tr, 128), lambda i: (i, 0))
    shp = jax.ShapeDtypeStruct((R, 128), f32)
    return pl.pallas_call(functools.partial(_adamw_body), name="adamw", grid=(R // tr,), in_specs=[spec] * 4,
                          out_specs=[spec] * 3, out_shape=[shp] * 3, compiler_params=_cparams(1))(w, g, m, v)


def _row_tile(a):
    return _pick(a, (256, 128, 64, 32, 16, 8)) if a % 8 == 0 else a


def adamw_nd(w, g, m, v):
    L, a, b = w.shape
    ta = _row_tile(a)
    spec = pl.BlockSpec((1, ta, b), lambda l, i: (l, i, 0))
    shp = jax.ShapeDtypeStruct((L, a, b), f32)
    return pl.pallas_call(functools.partial(_adamw_body), name="adamw_nd", grid=(L, a // ta), in_specs=[spec] * 4,
                          out_specs=[spec] * 3, out_shape=[shp] * 3, compiler_params=_cparams(2))(w, g, m, v)


def sum_slots(x):
    n, L, a, b = x.shape
    ta = _row_tile(a)

    def body(x_ref, o_ref):
        acc = x_ref[0].astype(f32)
        for k in range(1, n):
            acc = acc + x_ref[k].astype(f32)
        o_ref[...] = acc

    return pl.pallas_call(body, name="sum_slots", grid=(L, a // ta),
                          in_specs=[pl.BlockSpec((n, 1, ta, b), lambda l, i: (0, l, i, 0))],
                          out_specs=pl.BlockSpec((1, ta, b), lambda l, i: (l, i, 0)),
                          out_shape=jax.ShapeDtypeStruct((L, a, b), f32), compiler_params=_cparams(2))(x)


def _exchange_copies(ins, outs, sems, scatter):
    send_sems, recv_sems, local_sems = sems
    x, y, c = lax.axis_index("x"), lax.axis_index("y"), lax.axis_index("c")
    me = 4 * x + 2 * y + c
    locals_, sends, recvs = [], [], []
    for a in range(len(ins)):
        locals_.append(pltpu.make_async_copy(ins[a].at[me] if scatter[a] else ins[a], outs[a].at[me], local_sems.at[a]))
        for k in range(N_DEV - 1):
            px = 1 - x if (k + 1) & 4 else x
            py = 1 - y if (k + 1) & 2 else y
            pc = 1 - c if (k + 1) & 1 else c
            pid = 4 * px + 2 * py + pc
            src = ins[a].at[pid] if scatter[a] else ins[a]
            for slot, group in ((me, sends), (pid, recvs)):
                group.append(pltpu.make_async_remote_copy(src_ref=src, dst_ref=outs[a].at[slot], send_sem=send_sems.at[a, k],
                                                          recv_sem=recv_sems.at[a, k], device_id=(px, py, pc),
                                                          device_id_type=pl.DeviceIdType.MESH))
    return locals_, sends, recvs


def _exchange_start(ins, outs, sems, scatter):
    locals_, sends, _ = _exchange_copies(ins, outs, sems, scatter)
    for cp in locals_ + sends:
        cp.start()


def _exchange_wait(ins, outs, sems, scatter):
    locals_, sends, recvs = _exchange_copies(ins, outs, sems, scatter)
    for cp in recvs:
        cp.wait_recv()
    for cp in sends:
        cp.wait_send()
    for cp in locals_:
        cp.wait()


def _exchange_shapes(arrays, scatter):
    return [jax.ShapeDtypeStruct((N_DEV,) + tuple(a.shape[1:] if s else a.shape), a.dtype) for a, s in zip(arrays, scatter)]


def _flags(scatter, n):
    return [scatter] * n if isinstance(scatter, bool) else list(scatter)


def _exchange_sems(n):
    return [pltpu.SemaphoreType.DMA((n, N_DEV - 1)), pltpu.SemaphoreType.DMA((n, N_DEV - 1)), pltpu.SemaphoreType.DMA((n,))]


def _fuse_exchange(core, comm, n_in, n_out, n_scr, grid):
    if comm is None:
        return core, [], [], [], [], []
    arrays, scatter = comm
    n = len(arrays)
    scatter = _flags(scatter, n)

    def body(*refs):
        a, b, c = n_in + n, n_in + n + n_out, n_in + 2 * n + n_out
        cin, cout, sems = refs[n_in:a], refs[b:c], refs[c + n_scr:]
        ids = [pl.program_id(d) for d in range(len(grid))]
        first = functools.reduce(lambda p, q: p & q, [i == 0 for i in ids])
        last = functools.reduce(lambda p, q: p & q, [i == g - 1 for i, g in zip(ids, grid)])

        @pl.when(first)
        def _():
            _exchange_start(cin, cout, sems, scatter)

        core(*refs[:n_in], *refs[a:b], *refs[c:c + n_scr])

        @pl.when(last)
        def _():
            _exchange_wait(cin, cout, sems, scatter)

    hbm = pl.BlockSpec(memory_space=pltpu.HBM)
    return body, list(arrays), [hbm] * n, [hbm] * n, _exchange_shapes(arrays, scatter), _exchange_sems(n)


def exchange(arrays, scatter, name):
    n = len(arrays)
    scatter = _flags(scatter, n)

    def body(*refs):
        ins, outs, sems = refs[:n], refs[n:2 * n], refs[2 * n:]
        _exchange_start(ins, outs, sems, scatter)
        _exchange_wait(ins, outs, sems, scatter)

    hbm = pl.BlockSpec(memory_space=pltpu.HBM)
    return pl.pallas_call(body, name=name, in_specs=[hbm] * n, out_specs=[hbm] * n,
                          out_shape=_exchange_shapes(arrays, scatter), scratch_shapes=_exchange_sems(n))(*arrays)


BIG = (("w_in", "col"), ("conv_w", "col"), ("w_uq", "col"), ("w_ukv", "col"), ("w_out", "row"), ("w_up", "col"),
       ("conv_ff_w", "col"), ("w_down", "row"))
SMALL = ("b_ada", "norm_mix", "conv_b", "dt_bias", "a_log", "d_skip", "ssd_norm", "q_norm", "kv_norm", "attn_norm",
         "norm_mlp", "conv_ff_b", "final_norm")
CONVS = ("conv_w", "conv_ff_w")
PACK_ALIGN = 2048


def _padded(n):
    return -(-n // PACK_ALIGN) * PACK_ALIGN


def _flat_pad(a):
    f = a.reshape(-1)
    return jnp.pad(f, (0, _padded(f.shape[0]) - f.shape[0]))


PACK_ROWS = 512


def pack(arrs):
    f = jnp.concatenate([_flat_pad(a) for a in arrs])
    n = PACK_ROWS * 128
    return jnp.pad(f, (0, -(-f.shape[0] // n) * n - f.shape[0])).reshape(-1, 128)


def unpack(flat, shapes):
    f = flat.reshape(-1)
    out, off = [], 0
    for s in shapes:
        n = int(np.prod(s))
        out.append(f[off:off + n].reshape(s))
        off += _padded(n)
    return out


def shards_to_full(g, kind):
    _, a, b = g.shape
    if kind == "col":
        return g.transpose(1, 0, 2).reshape(a, N_DEV * b)
    return g.reshape(N_DEV * a, b)


def full_to_shards(full, kind):
    if kind == "col":
        a, nb = full.shape
        return full.reshape(a, N_DEV, nb // N_DEV).transpose(1, 0, 2)
    na, b = full.shape
    return full.reshape(N_DEV, na // N_DEV, b)


def w_in_layout(w):
    z = lambda n: jnp.zeros(w.shape[:-1] + (n,), w.dtype)
    return jnp.concatenate([w[..., :2560], w[..., 2576:2960], z(128), w[..., 2960:3216], w[..., 3216:3280],
                            w[..., 2560:2576], z(48)], axis=-1)


def w_in_unlayout(g):
    return jnp.concatenate([g[..., :2560], g[..., 3392:3408], g[..., 2560:2944], g[..., 3072:3328], g[..., 3328:3392]], axis=-1)


def w_uq_layout(w):
    return jnp.pad(w.reshape(Q_RANK, MLA_H, QK), ((0, 0), (0, 0), (0, 256 - QK))).reshape(Q_RANK, MLA_H * 256)


def w_uq_unlayout(g):
    return g.reshape(Q_RANK, MLA_H, 256)[:, :, :QK].reshape(Q_RANK, MLA_H * QK)


def w_ukv_layout(w):
    return w.reshape(KV_RANK, MLA_H, 2, 128).transpose(0, 2, 1, 3).reshape(KV_RANK, 2 * MLA_H * 128)


def w_ukv_unlayout(g):
    return g.reshape(KV_RANK, 2, MLA_H, 128).transpose(0, 2, 1, 3).reshape(KV_RANK, 2 * MLA_H * 128)


LAYOUTS = {"w_in": (w_in_layout, w_in_unlayout), "w_uq": (w_uq_layout, w_uq_unlayout), "w_ukv": (w_ukv_layout, w_ukv_unlayout)}
FIRST, REST = BIG[:4], BIG[4:]


def layer_weights(gathered, entries):
    full = {n: shards_to_full(g, kind) for (n, kind), g in zip(entries, gathered)}
    return {n: LAYOUTS[n][0](w) if n in LAYOUTS else w for n, w in full.items()}


def layer_grad_slices(g, entries):
    return [full_to_shards(LAYOUTS[n][1](g[n]) if n in LAYOUTS else g[n], kind).astype(bf16) for n, kind in entries]


def _head_row(v):
    return jnp.zeros((1, 128), f32).at[0, DT_LANE:DT_LANE + SSD_HEADS].set(v)


def layer_fwd(x3, mod, W, P, l, cosf, sinf, comm=None, late=None, comm_up=None):
    B, S, _ = x3.shape
    T = B * S
    sv = {}
    h = normmod_fwd(x3, mod, P["norm_mix"][l][None], 0, 1)
    p = mm(h.reshape(T, D), W["w_in"], "nn", "mm_in")
    p3 = p.reshape(B, S, IN_COLS)
    bias_row, alog_row = _head_row(P["dt_bias"][l]), _head_row(P["a_log"][l])
    dcol = jnp.repeat(P["d_skip"][l], SSD_HD)[:, None]
    xc3, xpre = conv_ssd_fwd(p3, W["conv_w"], P["conv_b"][l][None])
    yc3, states = ssd_fwd(xc3, p3, bias_row, alog_row, dcol)
    y_ssd = ssd_out_fwd(yc3, p3, P["ssd_norm"][l][None])
    cqn = rms_fwd(p3, 512, OFF_CQ // 512, Q_RANK, P["q_norm"][l][None], "rms_q_fwd")
    ckvn = rms_fwd(p3, KV_RANK, OFF_CKV // KV_RANK, KV_RANK, P["kv_norm"][l][None], "rms_kv_fwd")
    q3 = mm(cqn.reshape(T, Q_RANK), W["w_uq"], "nn", "mm_uq", out_dtype=bf16,
            rope=(cosf.reshape(T, 128), sinf.reshape(T, 128))).reshape(B, S, -1)
    k3 = k_proj(ckvn, W["w_ukv"], p3, cosf, sinf)
    v3 = mm(ckvn.reshape(T, KV_RANK), W["w_ukv"][:, MLA_H * NOPE:], "nn", "mm_uv", out_dtype=bf16).reshape(B, S, -1)
    o3, lse, comm_out = attn_fwd(q3, k3, v3, comm)
    if late is not None:
        W = dict(W, **late(comm_out))
    y_att = rms_fwd(o3, D, 0, D, P["attn_norm"][l][None], "rms_o_fwd")
    cat = (y_ssd.reshape(T, D), y_att.reshape(T, D))
    x1, y1 = mm(cat, W["w_out"], "nn", "mm_out", resid=x3.reshape(T, D), gate=mod[:, 2:3, :], seq=S)
    x13 = x1.reshape(B, S, D)
    h2 = normmod_fwd(x13, mod, P["norm_mlp"][l][None], 3, 4)
    u, up_out = mm(h2.reshape(T, D), W["w_up"], "nn", "mm_up", out_dtype=bf16, comm=comm_up), []
    if comm_up is not None:
        u, up_out = u
    u3 = u.reshape(B, S, 2 * D_FF)
    a, ffg, ffv = glu_fwd(u3, W["conv_ff_w"], P["conv_ff_b"][l][None])
    x2, y2 = mm(a.reshape(T, D_FF), W["w_down"], "nn", "mm_down", resid=x1, gate=mod[:, 5:6, :], seq=S)
    sv.update(x=x3, h=h, p3=p3, xc3=xc3, xpre=xpre, yc3=yc3, states=states, cqn=cqn, ckvn=ckvn, q3=q3, k3=k3, v3=v3, o3=o3, lse=lse,
              cat=cat, y1=y1, x1=x13, h2=h2, u3=u3, ffg=ffg, ffv=ffv, a=a, y2=y2, bias_row=bias_row, alog_row=alog_row, dcol=dcol)
    return x2.reshape(B, S, D), sv, comm_out, W, up_out


def layer_bwd(dx3, sv, mod, W, P, l, cosf, sinf, comm=None, send_rest=False):
    B, S, _ = dx3.shape
    T = B * S
    g = {}
    dy2, dg2 = gate_bwd(dx3, sv["y2"].reshape(B, S, D), mod, 5)
    dy2 = dy2.reshape(T, D)
    da = mm(dy2, W["w_down"], "nt", "mm_down_dx", out_dtype=bf16)
    g["w_down"] = mm(sv["a"].reshape(T, D_FF), dy2, "tn", "mm_down_dw")
    dug, duv, dwg, dwv, dbg, dbv = glu_bwd(sv["u3"], sv["ffg"], sv["ffv"], da.reshape(B, S, D_FF), W["conv_ff_w"])
    g["conv_ff_w"] = jnp.concatenate([dwg, dwv], axis=1)
    g["conv_ff_b"] = jnp.concatenate([dbg, dbv], axis=1)[0]
    du = (dug.reshape(T, D_FF), duv.reshape(T, D_FF))
    dh2 = mm(du, W["w_up"], "nt", "mm_up_dx", out_dtype=bf16)
    g["w_up"] = jnp.concatenate([mm(sv["h2"].reshape(T, D), d, "tn", "mm_up_dw") for d in du], axis=1)
    dx1, dsh2, dsc2, dnm, dy1, dg1 = normmod_bwd(sv["x1"], dh2.reshape(B, S, D), dx3, mod, P["norm_mlp"][l][None], 4,
                                                 y3=sv["y1"].reshape(B, S, D), i_g=2)
    g["norm_mlp"] = dnm[0]
    dy1 = dy1.reshape(T, D)
    dcat = mm(dy1, W["w_out"], "nt", "mm_out_dx", out_dtype=bf16)
    g["w_out"] = jnp.concatenate([mm(part, dy1, "tn", "mm_out_dw") for part in sv["cat"]], axis=0)
    dcat3 = dcat.reshape(B, S, 2 * D)
    dyc3, dp, dsn = ssd_out_bwd(sv["yc3"], sv["p3"], dcat3, P["ssd_norm"][l][None])
    g["ssd_norm"] = dsn[0]
    do3, dan, delta = rms_o_bwd(sv["o3"], dcat3, P["attn_norm"][l][None])
    g["attn_norm"] = dan[0]
    dqraw, dkn3, dv3, dkr3, comm_out = attn_bwd(sv["q3"], sv["k3"], sv["v3"], do3, sv["lse"], delta.reshape(B, MLA_H, 1, S),
                                                cosf, sinf, comm)
    dqraw = dqraw.reshape(T, -1)
    dcqn = mm(dqraw, W["w_uq"], "nt", "mm_uq_dx")
    g["w_uq"] = mm(sv["cqn"].reshape(T, Q_RANK), dqraw, "tn", "mm_uq_dw")
    dp, dqn = rms_bwd_into(sv["p3"], 512, OFF_CQ // 512, Q_RANK, dcqn.reshape(B, S, Q_RANK), P["q_norm"][l][None], dp, "rms_q_bwd")
    g["q_norm"] = dqn[0]
    bias_col, alog_col = sv["bias_row"].reshape(128, 1), sv["alog_row"].reshape(128, 1)
    comm_rest = (layer_grad_slices(g, REST), True) if send_rest else None
    dxc3, ddt3, dalog, dd, dbias, rest_out = ssd_bwd(sv["xc3"], sv["p3"], dyc3, sv["states"], sv["bias_row"], sv["alog_row"],
                                                     bias_col, alog_col, sv["dcol"], comm_rest)
    heads = slice(DT_LANE, DT_LANE + SSD_HEADS)
    g["a_log"], g["d_skip"], g["dt_bias"] = dalog[heads, 0], dd[heads, 0], dbias[heads, 0]
    dp, dcw, dcb = conv_ssd_bwd(sv["p3"], sv["xpre"], dxc3, W["conv_w"], dp)
    g["conv_w"], g["conv_b"] = dcw, dcb[0]
    dp = kprep_bwd(dkr3, ddt3, cosf, sinf, dp)
    dkv = (dkn3.reshape(T, -1), dv3.reshape(T, -1))
    dckvn = mm(dkv, W["w_ukv"], "nt", "mm_ukv_dx")
    g["w_ukv"] = jnp.concatenate([mm(sv["ckvn"].reshape(T, KV_RANK), d, "tn", "mm_ukv_dw") for d in dkv], axis=1)
    dp, dkn = rms_bwd_into(sv["p3"], KV_RANK, OFF_CKV // KV_RANK, KV_RANK, dckvn.reshape(B, S, KV_RANK), P["kv_norm"][l][None], dp,
                           "rms_kv_bwd")
    g["kv_norm"] = dkn[0]
    dp = dp.reshape(T, IN_COLS)
    dh = mm(dp, W["w_in"], "nt", "mm_in_dx", out_dtype=bf16)
    g["w_in"] = mm(sv["h"].reshape(T, D), dp, "tn", "mm_in_dw")
    dx0, dsh1, dsc1, dnx = normmod_bwd(sv["x"], dh.reshape(B, S, D), dx1, mod, P["norm_mix"][l][None], 1)
    g["norm_mix"] = dnx[0]
    dmod = jnp.concatenate([dsh1, dsc1, dg1, dsh2, dsc2, dg2], axis=1)
    return dx0, dmod, g, comm_out, rest_out


def kernel(x, c, positions, w_ada, b_ada, norm_mix, w_in, conv_w, conv_b, dt_bias, a_log, d_skip, ssd_norm, q_norm, w_uq, kv_norm, w_ukv, attn_norm, w_out, norm_mlp, w_up, conv_ff_w, conv_ff_b, w_down, final_norm, loss_target, m_w_ada, m_b_ada, m_norm_mix, m_w_in, m_conv_w, m_conv_b, m_dt_bias, m_a_log, m_d_skip, m_ssd_norm, m_q_norm, m_w_uq, m_kv_norm, m_w_ukv, m_attn_norm, m_w_out, m_norm_mlp, m_w_up, m_conv_ff_w, m_conv_ff_b, m_w_down, m_final_norm, v_w_ada, v_b_ada, v_norm_mix, v_w_in, v_conv_w, v_conv_b, v_dt_bias, v_a_log, v_d_skip, v_ssd_norm, v_q_norm, v_w_uq, v_kv_norm, v_w_ukv, v_attn_norm, v_w_out, v_norm_mlp, v_w_up, v_conv_ff_w, v_conv_ff_b, v_w_down, v_final_norm):
    given = dict(locals())
    B, S, _ = x.shape
    me = 4 * lax.axis_index("x") + 2 * lax.axis_index("y") + lax.axis_index("c")
    P = {n: given[n] for n in SMALL}

    def shards(l, entries):
        return [given[n][l] if n in CONVS else given[n][l].astype(bf16) for n, _ in entries]

    *gathered, c_all = exchange(shards(0, FIRST) + [c], False, "gather_weights")
    W = [layer_weights(gathered, FIRST), None]

    n_ada = w_ada.shape[2]
    c_all = c_all.reshape(N_DEV * B, D)
    b_sh = lax.dynamic_slice_in_dim(b_ada, me * n_ada, n_ada, axis=1)
    mod_sh = jnp.stack([ada_fwd(c_all, w_ada[l], b_sh[l][None]) for l in range(DEPTH)])
    (mod_g,) = exchange([mod_sh], False, "gather_mod")
    mod_mine = lax.dynamic_slice_in_dim(mod_g, me * B, B, axis=2)
    mods = mod_mine.transpose(1, 2, 0, 3).reshape(DEPTH, B, 6, D)

    inv_freq = jnp.asarray(1.0 / (ROPE_BASE ** (np.arange(0, ROPE, 2, dtype=np.float32) / ROPE)))
    ang = positions.astype(f32)[..., None] * inv_freq
    zeros = jnp.zeros((B, S, 128 - ROPE), f32)
    cosf = jnp.concatenate([jnp.cos(ang), jnp.cos(ang), zeros], axis=-1)
    sinf = jnp.concatenate([jnp.sin(ang), jnp.sin(ang), zeros], axis=-1)

    saved = [None] * DEPTH
    late = lambda got: layer_weights(got, REST)
    xl, saved[0], _, W[0], gathered = layer_fwd(x, mods[0], W[0], P, 0, cosf, sinf, comm=(shards(0, REST), False), late=late,
                                                comm_up=(shards(1, FIRST), False))
    xl, saved[1], _, W[1], _ = layer_fwd(xl, mods[1], layer_weights(gathered, FIRST), P, 1, cosf, sinf,
                                         comm=(shards(1, REST), False), late=late)
    dxl, d_final, loss_part = final_loss(xl, final_norm[None], loss_target)
    grads, dmods, recv = [None] * DEPTH, [None] * DEPTH, [None] * DEPTH
    dxl, dmods[1], grads[1], _, _ = layer_bwd(dxl, saved[1], mods[1], W[1], P, 1, cosf, sinf)
    grad_x, dmods[0], grads[0], recv[1], recv_rest = layer_bwd(dxl, saved[0], mods[0], W[0], P, 0, cosf, sinf,
                                                               comm=(layer_grad_slices(grads[1], BIG), True), send_rest=True)

    stack = lambda n: jnp.stack([grads[l][n] for l in range(DEPTH)])
    small_names = [n for n in SMALL if n not in ("b_ada", "final_norm")]
    partial = pack([stack(n) for n in small_names] + [d_final[0], loss_part[0]])
    dmod_all = jnp.stack(dmods)
    *recv_first, part_g, dmod_g = exchange(layer_grad_slices(grads[0], FIRST) + [partial, dmod_all],
                                           [True] * len(FIRST) + [False, False], "exchange_tail")
    recv[0] = recv_first + recv_rest
    big_g = [jnp.concatenate([sum_slots(recv[l][i][:, None]) for l in range(DEPTH)]) for i in range(len(BIG))]
    small_sum = sum_leading(part_g, "sum_partials")
    small_g = unpack(small_sum, [given[n].shape for n in small_names] + [(D,), (128,)])
    gsmall = dict(zip(small_names + ["final_norm"], small_g[:-1]))
    loss = small_g[-1][0]
    dmod_rows = dmod_g.transpose(0, 2, 1, 3, 4).reshape(N_DEV * B, DEPTH * 6 * D)
    gsmall["b_ada"] = sum_leading(dmod_rows.reshape(N_DEV * B, -1, 128), "sum_b_ada").reshape(DEPTH, 6 * D)
    dmod_cols = dmod_rows.reshape(N_DEV * B, DEPTH, N_DEV, n_ada)
    dmod_sh = lax.dynamic_slice_in_dim(dmod_cols, me, 1, axis=2)[:, :, 0, :]
    g_w_ada = jnp.stack([ada_bwd(c_all, dmod_sh[:, l, :]) for l in range(DEPTH)])

    res = {"grad": {}, "delta": {}, "new_m": {}, "new_v": {}}
    for n, gv in zip([n for n, _ in BIG] + ["w_ada"], big_g + [g_w_ada]):
        res["grad"][n] = gv
        res["delta"][n], res["new_m"][n], res["new_v"][n] = adamw_nd(given[n], gv, given["m_" + n], given["v_" + n])
    shapes = [given[n].shape for n in SMALL]
    flat = adamw(pack([given[n] for n in SMALL]), pack([gsmall[n] for n in SMALL]), pack([given["m_" + n] for n in SMALL]),
                 pack([given["v_" + n] for n in SMALL]))
    for n in SMALL:
        res["grad"][n] = gsmall[n]
    for key, arr in zip(("delta", "new_m", "new_v"), flat):
        res[key].update(zip(SMALL, unpack(arr, shapes)))
    order = ["w_ada", "b_ada", "norm_mix", "w_in", "conv_w", "conv_b", "dt_bias", "a_log", "d_skip", "ssd_norm", "q_norm", "w_uq",
             "kv_norm", "w_ukv", "attn_norm", "w_out", "norm_mlp", "w_up", "conv_ff_w", "conv_ff_b", "w_down", "final_norm"]
    return (loss, grad_x, *[res[k][n] for k in ("grad", "delta", "new_m", "new_v") for n in order])
```

```python
import functools

import numpy as np
import jax
import jax.numpy as jnp
from jax import lax
from jax.experimental import pallas as pl
from jax.experimental.pallas import tpu as pltpu

f32, bf16 = jnp.float32, jnp.bfloat16
HIGHEST = lax.Precision.HIGHEST

D = 1024
D_SSD = 1024
SSD_HEADS = 16
SSD_HD = 64
SSD_N = 128
CHUNK = 128
D_XBC = 1536
CONV_K = 4
MLA_H = 8
NOPE = 128
ROPE = 64
VD = 128
QK = NOPE + ROPE
Q_RANK = 384
KV_RANK = 256
D_FF = 2816
FF_K = 3
EPS = 1e-6
ROPE_BASE = 10000.0
DEPTH = 2
ADAM_LR, ADAM_B1, ADAM_B2, ADAM_EPS, ADAM_WD, ADAM_STEP = 0.001, 0.9, 0.999, 1e-08, 0.01, 10

N_DEV = 8
IN_COLS = 3456
OFF_XBC, OFF_CQ, OFF_CKV, OFF_KRDT = 1024, 2560, 3072, 3328
DT_LANE = 64
VMEM_LIMIT = 48 * 1024 * 1024
MM_K_WHOLE = 4096


def _cparams(n_grid):
    return pltpu.CompilerParams(dimension_semantics=("arbitrary",) * n_grid, vmem_limit_bytes=VMEM_LIMIT)


def _pick(n, cands):
    for c in cands:
        if n % c == 0:
            return c
    return n


def _silu(x):
    return x * jax.nn.sigmoid(x)


def _dsilu(x):
    s = jax.nn.sigmoid(x)
    return s * (1.0 + x * (1.0 - s))


def _rowsum(x):
    return jnp.sum(x, axis=0, keepdims=True)


def mm(a, b, mode, name, out_dtype=f32, resid=None, gate=None, seq=None, comm=None, rope=None):
    parts = list(a) if isinstance(a, (tuple, list)) else [a]
    np_ = len(parts)
    if mode == "nn":
        (M, Kp), N = parts[0].shape, b.shape[1]
    elif mode == "nt":
        (M, Kp), N = parts[0].shape, b.shape[0]
    else:
        (Kp, M), N = parts[0].shape, b.shape[1]
    K = Kp * np_
    gated = resid is not None
    whole = np_ > 1 or K <= MM_K_WHOLE
    tm = _pick(seq if gated else M, (1024, 1408, 512, 384, 256, 128) if K <= MM_K_WHOLE else (512, 256, 128))
    tn = _pick(N, (512, 1408, 384, 256, 128))
    tk = K if whole else _pick(K, (2816, 2048, 1024, 512))
    if mode == "tn":
        tm = _pick(M, (1024, 1408, 512, 384, 256, 128))
    nk = K // tk
    dims = {"nn": ((1,), (0,)), "nt": ((1,), (1,)), "tn": ((0,), (0,))}[mode]

    def body(*refs):
        a_refs, b_ref, rest = refs[:np_], refs[np_], refs[np_ + 1:]
        if rope is not None:
            cos_ref, sin_ref, rest = rest[0], rest[1], rest[2:]
        if gated:
            r_ref, g_ref, o_ref, y_ref, acc = rest
        else:
            o_ref, acc = rest

        def finish(res):
            if gated:
                y_ref[...] = res
                o_ref[...] = r_ref[...] + g_ref[0] * res
            elif rope is not None:
                for h in range(tn // 256):
                    lo = h * 256
                    o_ref[:, lo:lo + NOPE] = (res[:, lo:lo + NOPE] * Q_FOLD).astype(out_dtype)
                    o_ref[:, lo + NOPE:lo + 256] = (_rope(res[:, lo + NOPE:lo + 256], cos_ref[...], sin_ref[...]) * Q_FOLD).astype(out_dtype)
            else:
                o_ref[...] = res.astype(out_dtype)

        prod = None
        for p, a_ref in enumerate(a_refs):
            if np_ == 1:
                bv = b_ref[...]
            else:
                bv = b_ref[:, p * Kp:(p + 1) * Kp] if mode == "nt" else b_ref[p * Kp:(p + 1) * Kp, :]
            term = lax.dot_general(a_ref[...].astype(bf16), bv.astype(bf16), (dims, ((), ())), preferred_element_type=f32)
            prod = term if prod is None else prod + term
        if nk == 1:
            finish(prod)
        else:
            k = pl.program_id(2)

            @pl.when(k == 0)
            def _():
                acc[...] = prod

            @pl.when(k > 0)
            def _():
                acc[...] += prod

            @pl.when(k == nk - 1)
            def _():
                finish(acc[...])

    if np_ > 1:
        a_spec = pl.BlockSpec((tm, Kp), lambda i, j, k: (i, 0))
    elif mode == "tn":
        a_spec = pl.BlockSpec((tk, tm), lambda i, j, k: (k, i))
    else:
        a_spec = pl.BlockSpec((tm, tk), lambda i, j, k: (i, k))
    b_spec = pl.BlockSpec((tn, tk), lambda i, j, k: (j, k)) if mode == "nt" else pl.BlockSpec((tk, tn), lambda i, j, k: (k, j))
    o_spec = pl.BlockSpec((tm, tn), lambda i, j, k: (i, j))
    in_specs, args = [a_spec] * np_ + [b_spec], parts + [b]
    if rope is not None:
        in_specs += [pl.BlockSpec((tm, 128), lambda i, j, k: (i, 0))] * 2
        args += list(rope)
    out_specs, out_shape = [o_spec], [jax.ShapeDtypeStruct((M, N), out_dtype)]
    if gated:
        per = seq // tm
        in_specs += [o_spec, pl.BlockSpec((1, 1, tn), lambda i, j, k: (i // per, 0, j))]
        args += [resid, gate]
        out_specs = [o_spec, o_spec]
        out_shape = [jax.ShapeDtypeStruct((M, N), f32), jax.ShapeDtypeStruct((M, N), f32)]
    grid = (M // tm, N // tn, nk)
    body, c_args, c_in, c_out, c_shapes, c_sems = _fuse_exchange(body, comm, len(args), len(out_specs), 1, grid)
    res = pl.pallas_call(body, name=name + "_x" if comm else name, grid=grid, in_specs=in_specs + c_in, out_specs=out_specs + c_out,
                         out_shape=out_shape + c_shapes, scratch_shapes=[pltpu.VMEM((tm, tn), f32)] + c_sems,
                         compiler_params=_cparams(3))(*args, *c_args)
    own = res[:len(out_specs)]
    own = own[0] if len(own) == 1 else tuple(own)
    return (own, list(res[len(out_specs):])) if comm else own


def _tok(ts, width, cb=0):
    return pl.BlockSpec((1, ts, width), lambda b, s: (b, s, cb))


def _perb(rows, width):
    return pl.BlockSpec((1, rows, width), lambda b, s: (b, 0, 0))


def _const(rows, width):
    return pl.BlockSpec((rows, width), lambda b, s: (0, 0))


def _row_call(body, name, B, S, ts, in_specs, out_specs, out_shape, scratch=(), aliases=None):
    return pl.pallas_call(body, name=name, grid=(B, S // ts), in_specs=in_specs, out_specs=out_specs,
                          out_shape=out_shape, scratch_shapes=list(scratch), input_output_aliases=aliases or {},
                          compiler_params=_cparams(2))


def _first():
    return (pl.program_id(0) == 0) & (pl.program_id(1) == 0)


def normmod_fwd(x3, mod, g, i_sh, i_sc):
    B, S, C = x3.shape
    ts = _pick(S, (512, 256, 128))

    def body(x_ref, mod_ref, g_ref, h_ref):
        x = x_ref[0]
        r = lax.rsqrt(jnp.mean(x * x, axis=-1, keepdims=True) + EPS)
        n = x * r * g_ref[...]
        h_ref[0] = (n * (1.0 + mod_ref[0, i_sc:i_sc + 1, :]) + mod_ref[0, i_sh:i_sh + 1, :]).astype(bf16)

    return _row_call(body, "normmod_fwd", B, S, ts, [_tok(ts, C), _perb(6, C), _const(1, C)], _tok(ts, C),
                     jax.ShapeDtypeStruct((B, S, C), bf16))(x3, mod, g)


def normmod_bwd(x3, dh3, resid3, mod, g, i_sc, y3=None, i_g=None):
    B, S, C = x3.shape
    ts = _pick(S, (512, 256, 128))
    gated = y3 is not None

    def body(x_ref, dh_ref, r_ref, mod_ref, g_ref, *rest):
        if gated:
            y_ref, dx_ref, dsh_ref, dsc_ref, dg_ref, dy_ref, dgate_ref = rest
        else:
            dx_ref, dsh_ref, dsc_ref, dg_ref = rest

        @pl.when(pl.program_id(1) == 0)
        def _():
            dsh_ref[...] = jnp.zeros_like(dsh_ref)
            dsc_ref[...] = jnp.zeros_like(dsc_ref)
            if gated:
                dgate_ref[...] = jnp.zeros_like(dgate_ref)

        @pl.when(_first())
        def _():
            dg_ref[...] = jnp.zeros_like(dg_ref)

        x, dh, gv = x_ref[0], dh_ref[0].astype(f32), g_ref[...]
        r = lax.rsqrt(jnp.mean(x * x, axis=-1, keepdims=True) + EPS)
        xh = x * r
        dn = dh * (1.0 + mod_ref[0, i_sc:i_sc + 1, :])
        dsh_ref[0] += _rowsum(dh)
        dsc_ref[0] += _rowsum(dh * xh * gv)
        dg_ref[...] += _rowsum(dn * xh)
        dxh = dn * gv
        dx = r * (dxh - xh * jnp.mean(dxh * xh, axis=-1, keepdims=True)) + r_ref[0]
        dx_ref[0] = dx
        if gated:
            dy_ref[0] = (dx * mod_ref[0, i_g:i_g + 1, :]).astype(bf16)
            dgate_ref[0] += _rowsum(dx * y_ref[0])

    in_specs = [_tok(ts, C), _tok(ts, C), _tok(ts, C), _perb(6, C), _const(1, C)]
    out_specs = [_tok(ts, C), _perb(1, C), _perb(1, C), _const(1, C)]
    out_shape = [jax.ShapeDtypeStruct((B, S, C), f32), jax.ShapeDtypeStruct((B, 1, C), f32),
                 jax.ShapeDtypeStruct((B, 1, C), f32), jax.ShapeDtypeStruct((1, C), f32)]
    args = [x3, dh3, resid3, mod, g]
    if gated:
        in_specs.append(_tok(ts, C))
        args.append(y3)
        out_specs += [_tok(ts, C), _perb(1, C)]
        out_shape += [jax.ShapeDtypeStruct((B, S, C), bf16), jax.ShapeDtypeStruct((B, 1, C), f32)]
    return _row_call(body, "normmod_gate_bwd" if gated else "normmod_bwd", B, S, ts, in_specs, out_specs, out_shape)(*args)


def gate_bwd(dx3, y3, mod, i_g):
    B, S, C = dx3.shape
    ts = _pick(S, (512, 256, 128))

    def body(dx_ref, y_ref, mod_ref, dy_ref, dgate_ref):
        @pl.when(pl.program_id(1) == 0)
        def _():
            dgate_ref[...] = jnp.zeros_like(dgate_ref)

        dx = dx_ref[0]
        dy_ref[0] = (dx * mod_ref[0, i_g:i_g + 1, :]).astype(bf16)
        dgate_ref[0] += _rowsum(dx * y_ref[0])

    return _row_call(body, "gate_bwd", B, S, ts, [_tok(ts, C), _tok(ts, C), _perb(6, C)], [_tok(ts, C), _perb(1, C)],
                     [jax.ShapeDtypeStruct((B, S, C), bf16), jax.ShapeDtypeStruct((B, 1, C), f32)])(dx3, y3, mod)


def rms_fwd(src3, width, cb, n, g, name):
    B, S, _ = src3.shape
    ts = _pick(S, (512, 256, 128))

    def body(x_ref, g_ref, o_ref):
        x = x_ref[0][:, :n]
        r = lax.rsqrt(jnp.mean(x * x, axis=-1, keepdims=True) + EPS)
        o_ref[0] = (x * r * g_ref[...]).astype(bf16)

    return _row_call(body, name, B, S, ts, [_tok(ts, width, cb), _const(1, n)], _tok(ts, n),
                     jax.ShapeDtypeStruct((B, S, n), bf16))(src3, g)


def rms_bwd_into(src3, width, cb, n, dout3, g, dp, name):
    B, S, _ = src3.shape
    ts = _pick(S, (512, 256, 128))

    def body(x_ref, do_ref, g_ref, dp_in, dp_ref, dg_ref):
        @pl.when(_first())
        def _():
            dg_ref[...] = jnp.zeros_like(dg_ref)

        x = x_ref[0][:, :n]
        do = do_ref[0]
        r = lax.rsqrt(jnp.mean(x * x, axis=-1, keepdims=True) + EPS)
        xh = x * r
        dg_ref[...] += _rowsum(do * xh)
        dxh = do * g_ref[...]
        dp_ref[0, :, :n] = (r * (dxh - xh * jnp.mean(dxh * xh, axis=-1, keepdims=True))).astype(bf16)
        if width > n:
            dp_ref[0, :, n:] = jnp.zeros((ts, width - n), bf16)

    return _row_call(body, name, B, S, ts, [_tok(ts, width, cb), _tok(ts, n), _const(1, n), pl.BlockSpec(memory_space=pl.ANY)],
                     [_tok(ts, width, cb), _const(1, n)], [jax.ShapeDtypeStruct(dp.shape, bf16), jax.ShapeDtypeStruct((1, n), f32)],
                     aliases={3: 0})(src3, dout3, g, dp)


def final_loss(x3, g, tgt3):
    B, S, C = x3.shape
    ts = _pick(S, (512, 256, 128))

    def body(x_ref, g_ref, t_ref, dx_ref, dg_ref, loss_ref):
        @pl.when(_first())
        def _():
            dg_ref[...] = jnp.zeros_like(dg_ref)
            loss_ref[...] = jnp.zeros_like(loss_ref)

        x, gv = x_ref[0], g_ref[...]
        r = lax.rsqrt(jnp.mean(x * x, axis=-1, keepdims=True) + EPS)
        xh = x * r
        e = xh * gv - t_ref[0]
        loss_ref[...] += 0.5 * jnp.sum(e * e) / C
        dout = e / C
        dg_ref[...] += _rowsum(dout * xh)
        dxh = dout * gv
        dx_ref[0] = r * (dxh - xh * jnp.mean(dxh * xh, axis=-1, keepdims=True))

    return _row_call(body, "final_loss", B, S, ts, [_tok(ts, C), _const(1, C), _tok(ts, C)],
                     [_tok(ts, C), _const(1, C), _const(1, 128)],
                     [jax.ShapeDtypeStruct((B, S, C), f32), jax.ShapeDtypeStruct((1, C), f32),
                      jax.ShapeDtypeStruct((1, 128), f32)])(x3, g, tgt3)


def ssd_out_fwd(yc3, p3, w):
    B, S, C = yc3.shape
    ts = _pick(S, (512, 256, 128))
    half = C // 2

    def body(y_ref, z_ref, w_ref, o_ref):
        y = y_ref[0] * _silu(z_ref[0])
        for lo in (0, half):
            yg = y[:, lo:lo + half]
            r = lax.rsqrt(jnp.mean(yg * yg, axis=-1, keepdims=True) + EPS)
            o_ref[0, :, lo:lo + half] = (yg * r * w_ref[:, lo:lo + half]).astype(bf16)

    return _row_call(body, "ssd_out_fwd", B, S, ts, [_tok(ts, C), _tok(ts, C, 0), _const(1, C)], _tok(ts, C),
                     jax.ShapeDtypeStruct((B, S, C), bf16))(yc3, p3, w)


def ssd_out_bwd(yc3, p3, dcat3, w):
    B, S, C = yc3.shape
    ts = _pick(S, (512, 256, 128))
    half = C // 2

    def body(y_ref, z_ref, do_ref, w_ref, dyc_ref, dz_ref, dw_ref):
        @pl.when(_first())
        def _():
            dw_ref[...] = jnp.zeros_like(dw_ref)

        yc, z, do = y_ref[0], z_ref[0], do_ref[0].astype(f32)
        sz = _silu(z)
        y = yc * sz
        for lo in (0, half):
            sl = slice(lo, lo + half)
            yg, dog, wg = y[:, sl], do[:, sl], w_ref[:, sl]
            r = lax.rsqrt(jnp.mean(yg * yg, axis=-1, keepdims=True) + EPS)
            yh = yg * r
            dw_ref[:, sl] += _rowsum(dog * yh)
            dyh = dog * wg
            dy = r * (dyh - yh * jnp.mean(dyh * yh, axis=-1, keepdims=True))
            dyc_ref[0, :, sl] = dy * sz[:, sl]
            dz_ref[0, :, sl] = (dy * yc[:, sl] * _dsilu(z[:, sl])).astype(bf16)

    return _row_call(body, "ssd_out_bwd", B, S, ts, [_tok(ts, C), _tok(ts, C, 0), _tok(ts, C, 0), _const(1, C)],
                     [_tok(ts, C), _tok(ts, C, 0), _const(1, C)],
                     [jax.ShapeDtypeStruct((B, S, C), f32), jax.ShapeDtypeStruct((B, S, IN_COLS), bf16),
                      jax.ShapeDtypeStruct((1, C), f32)])(yc3, p3, dcat3, w)


def _rot(t):
    lane = lax.broadcasted_iota(jnp.int32, t.shape, 1)
    return jnp.where(lane < ROPE // 2, -pltpu.roll(t, 128 - ROPE // 2, 1), pltpu.roll(t, ROPE // 2, 1))


def _rope(t, cosf, sinf):
    return t * cosf + _rot(t) * sinf


def _rope_t(d, cosf, sinf):
    return d * cosf - _rot(d * sinf)


def k_proj(ckvn, w, p3, cosf, sinf):
    B, S, _ = ckvn.shape
    ts = _pick(S, (1024, 512, 256, 128))

    def body(x_ref, w_ref, kr_ref, c_ref, s_ref, o_ref):
        lane = lax.broadcasted_iota(jnp.int32, (1, 128), 1)
        kr = _rope(jnp.where(lane < ROPE, kr_ref[0], 0.0), c_ref[0], s_ref[0]).astype(bf16)
        res = jnp.dot(x_ref[0], w_ref[...], preferred_element_type=f32)
        for h in range(2):
            o_ref[0, :, h * 256:h * 256 + NOPE] = res[:, h * NOPE:(h + 1) * NOPE].astype(bf16)
            o_ref[0, :, h * 256 + NOPE:(h + 1) * 256] = kr

    return pl.pallas_call(body, name="k_proj", grid=(B, S // ts, MLA_H // 2),
                          in_specs=[pl.BlockSpec((1, ts, KV_RANK), lambda b, s, j: (b, s, 0)),
                                    pl.BlockSpec((KV_RANK, 2 * NOPE), lambda b, s, j: (0, j)),
                                    pl.BlockSpec((1, ts, 128), lambda b, s, j: (b, s, OFF_KRDT // 128)),
                                    pl.BlockSpec((1, ts, 128), lambda b, s, j: (b, s, 0)),
                                    pl.BlockSpec((1, ts, 128), lambda b, s, j: (b, s, 0))],
                          out_specs=pl.BlockSpec((1, ts, 512), lambda b, s, j: (b, s, j)),
                          out_shape=jax.ShapeDtypeStruct((B, S, MLA_H * 256), bf16), compiler_params=_cparams(3))(ckvn, w, p3, cosf, sinf)


def kprep_bwd(dkr3, ddt3, cosf, sinf, dp):
    B, S, W = dkr3.shape
    ts = _pick(S, (512, 256, 128))

    def body(dk_ref, ddt_ref, c_ref, s_ref, dp_in, kr_ref):
        acc = dk_ref[0, :, 0:128]
        for h in range(1, W // 128):
            acc = acc + dk_ref[0, :, h * 128:(h + 1) * 128]
        lane = lax.broadcasted_iota(jnp.int32, (1, 128), 1)
        kr_ref[0] = jnp.where(lane < ROPE, _rope_t(acc, c_ref[0], s_ref[0]), ddt_ref[0]).astype(bf16)

    return _row_call(body, "kprep_bwd", B, S, ts,
                     [_tok(ts, W), _tok(ts, 128), _tok(ts, 128), _tok(ts, 128), pl.BlockSpec(memory_space=pl.ANY)],
                     _tok(ts, 128, OFF_KRDT // 128), jax.ShapeDtypeStruct(dp.shape, bf16), aliases={4: 0})(dkr3, ddt3, cosf, sinf, dp)


def _shift_down(u, j):
    if j == 0:
        return u
    row = lax.broadcasted_iota(jnp.int32, u.shape, 0)
    return jnp.where(row < j, 0.0, pltpu.roll(u, j, 0))


def _shift_up(u, j):
    if j == 0:
        return u
    n = u.shape[0]
    row = lax.broadcasted_iota(jnp.int32, u.shape, 0)
    return jnp.where(row >= n - j, 0.0, pltpu.roll(u, n - j, 0))


def _conv(u, w, b, K):
    out = b
    for j in range(K):
        out = out + w[K - 1 - j:K - j, :] * _shift_down(u, j)
    return out


def _conv_bwd(u, du, w, K):
    dins = w[K - 1:K, :] * du
    dws = [None] * K
    dws[K - 1] = _rowsum(du * u)
    for j in range(1, K):
        sd = _shift_up(du, j)
        dins = dins + w[K - 1 - j:K - j, :] * sd
        dws[K - 1 - j] = _rowsum(sd * u)
    return dins, dws


CW = 256


def conv_ssd_fwd(p3, w, b):
    B, S, _ = p3.shape
    nb = D_XBC // CW

    def body(u_ref, w_ref, b_ref, o_ref, pre_ref):
        pre = _conv(u_ref[0], w_ref[...], b_ref[...], CONV_K)
        o_ref[0] = _silu(pre)
        pre_ref[0] = pre.astype(bf16)

    out = pl.BlockSpec((1, S, CW), lambda b, j: (b, 0, j))
    return pl.pallas_call(body, name="conv_ssd_fwd", grid=(B, nb),
                          in_specs=[pl.BlockSpec((1, S, CW), lambda b, j: (b, 0, OFF_XBC // CW + j)),
                                    pl.BlockSpec((CONV_K, CW), lambda b, j: (0, j)),
                                    pl.BlockSpec((1, CW), lambda b, j: (0, j))],
                          out_specs=[out, out],
                          out_shape=[jax.ShapeDtypeStruct((B, S, D_XBC), f32), jax.ShapeDtypeStruct((B, S, D_XBC), bf16)],
                          compiler_params=_cparams(2))(p3, w, b)


def conv_ssd_bwd(p3, pre3, dxc3, w, dp):
    B, S, _ = p3.shape
    nb = D_XBC // CW

    def body(u_ref, pre_ref, d_ref, w_ref, dp_in, du_ref, dw_ref, db_ref):
        @pl.when(pl.program_id(1) == 0)
        def _():
            dw_ref[...] = jnp.zeros_like(dw_ref)
            db_ref[...] = jnp.zeros_like(db_ref)

        u, wv = u_ref[0], w_ref[...]
        dpre = d_ref[0] * _dsilu(pre_ref[0].astype(f32))
        dins, dws = _conv_bwd(u, dpre, wv, CONV_K)
        du_ref[0] = dins.astype(bf16)
        for k in range(CONV_K):
            dw_ref[k:k + 1, :] += dws[k]
        db_ref[...] += _rowsum(dpre)

    return pl.pallas_call(body, name="conv_ssd_bwd", grid=(nb, B),
                          in_specs=[pl.BlockSpec((1, S, CW), lambda j, b: (b, 0, OFF_XBC // CW + j)),
                                    pl.BlockSpec((1, S, CW), lambda j, b: (b, 0, j)),
                                    pl.BlockSpec((1, S, CW), lambda j, b: (b, 0, j)),
                                    pl.BlockSpec((CONV_K, CW), lambda j, b: (0, j)), pl.BlockSpec(memory_space=pl.ANY)],
                          out_specs=[pl.BlockSpec((1, S, CW), lambda j, b: (b, 0, OFF_XBC // CW + j)),
                                     pl.BlockSpec((CONV_K, CW), lambda j, b: (0, j)),
                                     pl.BlockSpec((1, CW), lambda j, b: (0, j))],
                          out_shape=[jax.ShapeDtypeStruct(dp.shape, bf16), jax.ShapeDtypeStruct((CONV_K, D_XBC), f32),
                                     jax.ShapeDtypeStruct((1, D_XBC), f32)], input_output_aliases={4: 0},
                          compiler_params=_cparams(2))(p3, pre3, dxc3, w, dp)


def glu_fwd(u3, w, b):
    B, S, _ = u3.shape
    nb = D_FF // CW

    def body(ug_ref, uv_ref, wg_ref, wv_ref, bg_ref, bv_ref, o_ref, g_ref, v_ref):
        g = _conv(ug_ref[0].astype(f32), wg_ref[...], bg_ref[...], FF_K)
        v = _conv(uv_ref[0].astype(f32), wv_ref[...], bv_ref[...], FF_K)
        o_ref[0] = (_silu(g) * v).astype(bf16)
        g_ref[0] = g.astype(bf16)
        v_ref[0] = v.astype(bf16)

    def blk(off):
        return pl.BlockSpec((1, S, CW), lambda b, j: (b, 0, off + j))

    def par(rows, off):
        return pl.BlockSpec((rows, CW), lambda b, j: (0, off + j))

    shp = jax.ShapeDtypeStruct((B, S, D_FF), bf16)
    return pl.pallas_call(body, name="glu_fwd", grid=(B, nb),
                          in_specs=[blk(0), blk(nb), par(FF_K, 0), par(FF_K, nb), par(1, 0), par(1, nb)],
                          out_specs=[blk(0)] * 3, out_shape=[shp] * 3, compiler_params=_cparams(2))(u3, u3, w, w, b, b)


def glu_bwd(u3, g3, v3, da3, w):
    B, S, _ = u3.shape
    nb = D_FF // CW

    def body(ug_ref, uv_ref, g_ref, v_ref, da_ref, wg_ref, wv_ref, dug_ref, duv_ref, dwg_ref, dwv_ref, dbg_ref, dbv_ref):
        @pl.when(pl.program_id(1) == 0)
        def _():
            for r in (dwg_ref, dwv_ref, dbg_ref, dbv_ref):
                r[...] = jnp.zeros_like(r)

        ug, uv, da, wg, wv = ug_ref[0].astype(f32), uv_ref[0].astype(f32), da_ref[0].astype(f32), wg_ref[...], wv_ref[...]
        g, v = g_ref[0].astype(f32), v_ref[0].astype(f32)
        dg = da * v * _dsilu(g)
        dv = da * _silu(g)
        ding, dwsg = _conv_bwd(ug, dg, wg, FF_K)
        dinv, dwsv = _conv_bwd(uv, dv, wv, FF_K)
        dug_ref[0] = ding.astype(bf16)
        duv_ref[0] = dinv.astype(bf16)
        for k in range(FF_K):
            dwg_ref[k:k + 1, :] += dwsg[k]
            dwv_ref[k:k + 1, :] += dwsv[k]
        dbg_ref[...] += _rowsum(dg)
        dbv_ref[...] += _rowsum(dv)

    def blk(off):
        return pl.BlockSpec((1, S, CW), lambda j, b: (b, 0, off + j))

    def par(rows, off):
        return pl.BlockSpec((rows, CW), lambda j, b: (0, off + j))

    return pl.pallas_call(body, name="glu_bwd", grid=(nb, B),
                          in_specs=[blk(0), blk(nb), blk(0), blk(0), blk(0), par(FF_K, 0), par(FF_K, nb)],
                          out_specs=[blk(0), blk(0), par(FF_K, 0), par(FF_K, 0), par(1, 0), par(1, 0)],
                          out_shape=[jax.ShapeDtypeStruct((B, S, D_FF), bf16), jax.ShapeDtypeStruct((B, S, D_FF), bf16),
                                     jax.ShapeDtypeStruct((FF_K, D_FF), f32), jax.ShapeDtypeStruct((FF_K, D_FF), f32),
                                     jax.ShapeDtypeStruct((1, D_FF), f32), jax.ShapeDtypeStruct((1, D_FF), f32)],
                          compiler_params=_cparams(2))(u3, u3, g3, v3, da3, w, w)


def _ssd_decay(dtb, bias_row, alog_row):
    lane = lax.broadcasted_iota(jnp.int32, (1, 128), 1)
    hmask = (lane >= DT_LANE) & (lane < DT_LANE + SSD_HEADS)
    dt = jnp.where(hmask, jax.nn.softplus(dtb + bias_row), 0.0)
    a = dt * jnp.where(hmask, -jnp.exp(alog_row), 0.0)
    r = lax.broadcasted_iota(jnp.int32, (CHUNK, CHUNK), 0)
    c = lax.broadcasted_iota(jnp.int32, (CHUNK, CHUNK), 1)
    cs = jnp.dot((r >= c).astype(f32), a, precision=HIGHEST, preferred_element_type=f32)
    return dt, cs


def _expand(xt):
    return jnp.concatenate([jnp.broadcast_to(xt[DT_LANE + h:DT_LANE + h + 1, :], (SSD_HD, xt.shape[1]))
                            for h in range(SSD_HEADS)], axis=0)


_NT = (((1,), (1,)), ((), ()))
_TN = (((0,), (0,)), ((), ()))
GH = SSD_HEADS // 2
GR = GH * SSD_HD


def ssd_fwd(xc3, p3, bias_row, alog_row, dcol):
    B, S, _ = xc3.shape
    nc = S // CHUNK

    def body(xs_ref, bc_ref, dtb_ref, bias_ref, alog_ref, dcol_ref, y_ref, st_ref, state, yT):
        @pl.when(pl.program_id(1) == 0)
        def _():
            state[...] = jnp.zeros_like(state)

        dt, cs = _ssd_decay(dtb_ref[0], bias_ref[...], alog_ref[...])
        csT = cs.T
        eT = jnp.exp(csT)
        decX = _expand(jnp.exp(csT[:, CHUNK - 1:CHUNK] - csT))
        eX = _expand(eT)
        elastX = eX[:, CHUNK - 1:CHUNK]
        xsT = xs_ref[0].T
        uT = xsT * _expand(dt.T)
        bc = bc_ref[0]
        st_ref[0, 0] = state[...]
        srow = lax.broadcasted_iota(jnp.int32, (CHUNK, CHUNK), 0)
        lcol = lax.broadcasted_iota(jnp.int32, (CHUNK, CHUNK), 1)
        for g in range(2):
            Bg = bc[:, g * SSD_N:(g + 1) * SSD_N].astype(bf16)
            Cg = bc[:, (2 + g) * SSD_N:(3 + g) * SSD_N].astype(bf16)
            GT = lax.dot_general(Bg, Cg, _NT, preferred_element_type=f32)
            rows = slice(g * GR, (g + 1) * GR)
            Sg = state[rows]
            yoffT = lax.dot_general(Sg.astype(bf16), Cg, _NT, preferred_element_type=f32) * eX[rows]
            state[rows] = Sg * elastX[rows] + jnp.dot((uT[rows] * decX[rows]).astype(bf16), Bg, preferred_element_type=f32)
            for k in range(GH):
                h = g * GH + k
                hr = slice(h * SSD_HD, (h + 1) * SSD_HD)
                seg = csT[DT_LANE + h:DT_LANE + h + 1, :] - cs[:, DT_LANE + h:DT_LANE + h + 1]
                LT = jnp.where(lcol >= srow, jnp.exp(jnp.minimum(seg, 0.0)), 0.0)
                yT[hr] = (jnp.dot(uT[hr].astype(bf16), (GT * LT).astype(bf16), preferred_element_type=f32)
                          + yoffT[k * SSD_HD:(k + 1) * SSD_HD] + dcol_ref[hr] * xsT[hr])
        y_ref[0] = yT[...].T

    return pl.pallas_call(body, name="ssd_fwd", grid=(B, nc),
                          in_specs=[pl.BlockSpec((1, CHUNK, D_SSD), lambda b, c: (b, c, 0)),
                                    pl.BlockSpec((1, CHUNK, 512), lambda b, c: (b, c, 2)),
                                    pl.BlockSpec((1, CHUNK, 128), lambda b, c: (b, c, OFF_KRDT // 128)),
                                    _const(1, 128), _const(1, 128), _const(D_SSD, 1)],
                          out_specs=[pl.BlockSpec((1, CHUNK, D_SSD), lambda b, c: (b, c, 0)),
                                     pl.BlockSpec((1, 1, D_SSD, SSD_N), lambda b, c: (b, c, 0, 0))],
                          out_shape=[jax.ShapeDtypeStruct((B, S, D_SSD), f32), jax.ShapeDtypeStruct((B, nc, D_SSD, SSD_N), f32)],
                          scratch_shapes=[pltpu.VMEM((D_SSD, SSD_N), f32), pltpu.VMEM((D_SSD, CHUNK), f32)],
                          compiler_params=_cparams(2))(xc3, xc3, p3, bias_row, alog_row, dcol)


def ssd_bwd(xc3, p3, dy3, states, bias_row, alog_row, bias_col, alog_col, dcol, comm=None):
    B, S, _ = xc3.shape
    nc = S // CHUNK

    def body(xs_ref, bc_ref, dtb_ref, dy_ref, st_ref, bias_ref, alog_ref, biasc_ref, alogc_ref, dcol_ref,
             dxc_ref, ddt_ref, dalog_ref, dd_ref, dbias_ref, dS, dUT, accA, accD, accB, dcs_diag):
        @pl.when(pl.program_id(1) == 0)
        def _():
            dS[...] = jnp.zeros_like(dS)

        @pl.when(_first())
        def _():
            accA[...] = jnp.zeros_like(accA)
            accD[...] = jnp.zeros_like(accD)
            accB[...] = jnp.zeros_like(accB)

        dtb = dtb_ref[0]
        dt, cs = _ssd_decay(dtb, bias_ref[...], alog_ref[...])
        dtT, csT = dt.T, cs.T
        decX = _expand(jnp.exp(csT[:, CHUNK - 1:CHUNK] - csT))
        eX = _expand(jnp.exp(csT))
        dtX = _expand(dtT)
        elastX = eX[:, CHUNK - 1:CHUNK]
        xsT = xs_ref[0].T
        uT = xsT * dtX
        dYT = dy_ref[0].T
        bc = bc_ref[0]
        lrow = lax.broadcasted_iota(jnp.int32, (CHUNK, CHUNK), 0)
        scol = lax.broadcasted_iota(jnp.int32, (CHUNK, CHUNK), 1)
        dcs_diag[...] = jnp.zeros_like(dcs_diag)
        rs_cols = jnp.zeros((CHUNK, 128), f32)
        vparts, zparts = [], []
        for g in range(2):
            Bf = bc[:, g * SSD_N:(g + 1) * SSD_N]
            Bg = Bf.astype(bf16)
            Cg = bc[:, (2 + g) * SSD_N:(3 + g) * SSD_N].astype(bf16)
            G = lax.dot_general(Cg, Bg, _NT, preferred_element_type=f32)
            BgT = Bf.T.astype(bf16)
            rows = slice(g * GR, (g + 1) * GR)
            dSg = dS[rows]
            Sg = st_ref[0, 0, rows, :]
            dUst = jnp.dot(dSg.astype(bf16), BgT, preferred_element_type=f32) * decX[rows]
            yoffT = lax.dot_general(Sg.astype(bf16), Cg, _NT, preferred_element_type=f32) * eX[rows]
            zparts.append(dYT[rows] * yoffT - dUst * uT[rows])
            dG = jnp.zeros((CHUNK, CHUNK), f32)
            for k in range(GH):
                h = g * GH + k
                hr = slice(h * SSD_HD, (h + 1) * SSD_HD)
                seg = cs[:, DT_LANE + h:DT_LANE + h + 1] - csT[DT_LANE + h:DT_LANE + h + 1, :]
                L = jnp.where(lrow >= scol, jnp.exp(jnp.minimum(seg, 0.0)), 0.0)
                M = G * L
                dYh = dYT[hr].astype(bf16)
                dUT[hr] = jnp.dot(dYh, M.astype(bf16), preferred_element_type=f32) + dUst[k * SSD_HD:(k + 1) * SSD_HD]
                dM = lax.dot_general(dYh, uT[hr].astype(bf16), _TN, preferred_element_type=f32)
                dG = dG + dM * L
                Wm = dM * M
                rs_cols = jnp.where(scol == DT_LANE + h, jnp.sum(Wm, axis=1, keepdims=True), rs_cols)
                dcs_diag[DT_LANE + h:DT_LANE + h + 1, :] = -_rowsum(Wm)
            dGb = dG.astype(bf16)
            dYe = (dYT[rows] * eX[rows]).astype(bf16)
            ude = (uT[rows] * decX[rows]).astype(bf16)
            dC = jnp.dot(dGb, Bg, preferred_element_type=f32) + lax.dot_general(dYe, Sg.astype(bf16), _TN, preferred_element_type=f32)
            dB = (lax.dot_general(dGb, Cg, _TN, preferred_element_type=f32)
                  + lax.dot_general(ude, dSg.astype(bf16), _TN, preferred_element_type=f32))
            dxc_ref[0, :, D_SSD + g * SSD_N:D_SSD + (g + 1) * SSD_N] = dB
            dxc_ref[0, :, D_SSD + (2 + g) * SSD_N:D_SSD + (3 + g) * SSD_N] = dC
            vparts.append(elastX[rows] * jnp.sum(dSg * Sg, axis=1, keepdims=True)
                          + jnp.sum(dUst * uT[rows], axis=1, keepdims=True))
            dS[rows] = elastX[rows] * dSg + jnp.dot(dYe, Cg, preferred_element_type=f32)
        dU = dUT[...]
        dcv = dcol_ref[...]
        dxc_ref[0, :, 0:D_SSD] = (dtX * dU + dcv * dYT).T
        lane = lax.broadcasted_iota(jnp.int32, (D_SSD, CHUNK), 1)
        Z = jnp.concatenate(zparts, axis=0) + jnp.where(lane == CHUNK - 1, jnp.concatenate(vparts, axis=0), 0.0)
        hr_ = lax.broadcasted_iota(jnp.int32, (128, D_SSD), 0)
        hc_ = lax.broadcasted_iota(jnp.int32, (128, D_SSD), 1)
        hsel = (hr_ - DT_LANE == jnp.right_shift(hc_, 6)).astype(bf16)
        summands = jnp.concatenate([Z, dU * xsT, dYT * xsT], axis=1)
        hi = summands.astype(bf16)
        lo = (summands - hi.astype(f32)).astype(bf16)
        red = jnp.dot(hsel, hi, preferred_element_type=f32) + jnp.dot(hsel, lo, preferred_element_type=f32)
        dcsT = red[:, 0:CHUNK] + dcs_diag[...] + rs_cols.T
        daT = jnp.dot(dcsT, (lrow >= scol).astype(f32), precision=HIGHEST, preferred_element_type=f32)
        rowi = lax.broadcasted_iota(jnp.int32, (128, 1), 0)
        hmask = (rowi >= DT_LANE) & (rowi < DT_LANE + SSD_HEADS)
        a_col = jnp.where(hmask, -jnp.exp(alogc_ref[...]), 0.0)
        ddtT = red[:, CHUNK:2 * CHUNK] + a_col * daT
        ddt_rawT = jnp.where(hmask, ddtT * jax.nn.sigmoid(dtb.T + biasc_ref[...]), 0.0)
        ddt_ref[0] = ddt_rawT.T
        accA[...] += daT * dtT
        accD[...] += red[:, 2 * CHUNK:3 * CHUNK]
        accB[...] += ddt_rawT

        @pl.when((pl.program_id(0) == B - 1) & (pl.program_id(1) == nc - 1))
        def _():
            dalog_ref[...] = jnp.broadcast_to(jnp.sum(accA[...], axis=1, keepdims=True) * a_col, (128, 128))
            dd_ref[...] = jnp.broadcast_to(jnp.sum(accD[...], axis=1, keepdims=True), (128, 128))
            dbias_ref[...] = jnp.broadcast_to(jnp.sum(accB[...], axis=1, keepdims=True), (128, 128))

    def rev(width, cb):
        return pl.BlockSpec((1, CHUNK, width), lambda b, c: (b, nc - 1 - c, cb))

    acc_spec = pl.BlockSpec((128, 128), lambda b, c: (0, 0))
    acc_shape = jax.ShapeDtypeStruct((128, 128), f32)
    body, c_args, c_in, c_out, c_shapes, c_sems = _fuse_exchange(body, comm, 10, 5, 6, (B, nc))
    res = pl.pallas_call(body, name="ssd_bwd_x" if comm else "ssd_bwd", grid=(B, nc),
                         in_specs=[rev(D_SSD, 0), rev(512, 2), rev(128, OFF_KRDT // 128), rev(D_SSD, 0),
                                   pl.BlockSpec((1, 1, D_SSD, SSD_N), lambda b, c: (b, nc - 1 - c, 0, 0)),
                                   _const(1, 128), _const(1, 128), _const(128, 1), _const(128, 1), _const(D_SSD, 1)] + c_in,
                         out_specs=[rev(D_XBC, 0), rev(128, 0), acc_spec, acc_spec, acc_spec] + c_out,
                         out_shape=[jax.ShapeDtypeStruct((B, S, D_XBC), f32), jax.ShapeDtypeStruct((B, S, 128), f32),
                                    acc_shape, acc_shape, acc_shape] + c_shapes,
                         scratch_shapes=[pltpu.VMEM((D_SSD, SSD_N), f32), pltpu.VMEM((D_SSD, CHUNK), f32),
                                         pltpu.VMEM((128, 128), f32), pltpu.VMEM((128, 128), f32), pltpu.VMEM((128, 128), f32),
                                         pltpu.VMEM((128, 128), f32)] + c_sems,
                         compiler_params=_cparams(2))(xc3, xc3, p3, dy3, states, bias_row, alog_row, bias_col, alog_col, dcol, *c_args)
    return (*res[:5], list(res[5:]))


ATT_SCALE = float(QK) ** -0.5
LOG2E = 1.4426950408889634
LN2 = 0.6931471805599453
Q_FOLD = ATT_SCALE * LOG2E
NEG = -1e30
HP = 4
HPF = 8


def _att_block(S):
    return _pick(S, (512, 256, 128))


def _tri_rows(t, n):
    i = sum([(t >= r * (r + 1) // 2).astype(jnp.int32) for r in range(1, n)], jnp.int32(0))
    return i, t - i * (i + 1) // 2


def _tri_cols(t, n):
    j = sum([(t >= r * n - r * (r - 1) // 2).astype(jnp.int32) for r in range(1, n)], jnp.int32(0))
    return j, j + t - (j * n - j * (j - 1) // 2)


def attn_fwd(q3, k3, v3, comm=None):
    B, S, _ = q3.shape
    bq = _att_block(S)
    nq = S // bq

    def body(q_ref, k_ref, v_ref, o_ref, lse_ref, m_s, l_s, acc):
        i, j = _tri_rows(pl.program_id(2), nq)

        @pl.when(j == 0)
        def _():
            m_s[...] = jnp.full_like(m_s, NEG)
            l_s[...] = jnp.zeros_like(l_s)
            acc[...] = jnp.zeros_like(acc)

        def step(masked):
            for t in range(HPF):
                qk = slice(t * 256, (t + 1) * 256)
                st = lax.dot_general(k_ref[0, :, qk], q_ref[0, :, qk], _NT, preferred_element_type=f32)
                if masked:
                    r = lax.broadcasted_iota(jnp.int32, (bq, bq), 0)
                    c = lax.broadcasted_iota(jnp.int32, (bq, bq), 1)
                    st = jnp.where(c >= r, st, NEG)
                m_old = m_s[t]
                m_new = jnp.maximum(m_old, jnp.max(st, axis=0, keepdims=True))
                alpha = jnp.exp2(m_old - m_new)
                pt = jnp.exp2(st - m_new)
                l_s[t] = alpha * l_s[t] + jnp.sum(pt, axis=0, keepdims=True)
                acc[t] = alpha * acc[t] + lax.dot_general(v_ref[0, :, t * VD:(t + 1) * VD], pt.astype(bf16), _TN,
                                                          preferred_element_type=f32)
                m_s[t] = m_new

        @pl.when(j < i)
        def _():
            step(False)

        @pl.when(j == i)
        def _():
            step(True)
            for t in range(HPF):
                o_ref[0, :, t * VD:(t + 1) * VD] = (acc[t] / l_s[t]).T
                lse_ref[0, t] = m_s[t] + jnp.log2(l_s[t])

    grid = (B, MLA_H // HPF, nq * (nq + 1) // 2)
    qi = lambda t: _tri_rows(t, nq)[0]
    kj = lambda t: _tri_rows(t, nq)[1]
    body, c_args, c_in, c_out, c_shapes, c_sems = _fuse_exchange(body, comm, 3, 2, 3, grid)
    res = pl.pallas_call(body, name="attn_fwd_x" if comm else "attn_fwd", grid=grid,
                         in_specs=[pl.BlockSpec((1, bq, HPF * 256), lambda b, h, t: (b, qi(t), h)),
                                   pl.BlockSpec((1, bq, HPF * 256), lambda b, h, t: (b, kj(t), h)),
                                   pl.BlockSpec((1, bq, HPF * VD), lambda b, h, t: (b, kj(t), h))] + c_in,
                         out_specs=[pl.BlockSpec((1, bq, HPF * VD), lambda b, h, t: (b, qi(t), h)),
                                    pl.BlockSpec((1, HPF, 1, bq), lambda b, h, t: (b, h, 0, qi(t)))] + c_out,
                         out_shape=[jax.ShapeDtypeStruct((B, S, MLA_H * VD), f32), jax.ShapeDtypeStruct((B, MLA_H, 1, S), f32)] + c_shapes,
                         scratch_shapes=[pltpu.VMEM((HPF, 1, bq), f32), pltpu.VMEM((HPF, 1, bq), f32), pltpu.VMEM((HPF, VD, bq), f32)] + c_sems,
                         compiler_params=_cparams(3))(q3, k3, v3, *c_args)
    return res[0], res[1], list(res[2:])


def rms_o_bwd(o3, dcat3, g):
    B, S, C = o3.shape
    ts = _pick(S, (512, 256, 128))

    def body(x_ref, do_ref, g_ref, dx_ref, dg_ref, d_ref):
        @pl.when(_first())
        def _():
            dg_ref[...] = jnp.zeros_like(dg_ref)

        x, do = x_ref[0], do_ref[0].astype(f32)
        r = lax.rsqrt(jnp.mean(x * x, axis=-1, keepdims=True) + EPS)
        xh = x * r
        dg_ref[...] += _rowsum(do * xh)
        dxh = do * g_ref[...]
        dx = r * (dxh - xh * jnp.mean(dxh * xh, axis=-1, keepdims=True))
        dx_ref[0] = dx
        for h in range(MLA_H):
            vs = slice(h * VD, (h + 1) * VD)
            d_ref[0, h] = jnp.sum(dx[:, vs] * x[:, vs], axis=-1, keepdims=True)

    return _row_call(body, "rms_o_bwd", B, S, ts, [_tok(ts, C), _tok(ts, C, 1), _const(1, C)],
                     [_tok(ts, C), _const(1, C), pl.BlockSpec((1, MLA_H, ts, 1), lambda b, s: (b, 0, s, 0))],
                     [jax.ShapeDtypeStruct((B, S, C), f32), jax.ShapeDtypeStruct((1, C), f32),
                      jax.ShapeDtypeStruct((B, MLA_H, S, 1), f32)])(o3, dcat3, g)


def attn_bwd(q3, k3, v3, do3, lse_row, delta_row, cosf, sinf, comm=None):
    B, S, _ = q3.shape
    bq = _att_block(S)
    nq = S // bq

    def body(q_ref, k_ref, v_ref, do_ref, lse_ref, dl_ref, cos_ref, sin_ref, dkn_ref, dv_ref, dkr_ref, dq_hbm, dk_acc, dv_acc,
             dq_scr, stage, dq_sem):
        b, hp = pl.program_id(0), pl.program_id(1)
        j, i = _tri_cols(pl.program_id(2), nq)
        rows = pl.ds(pl.multiple_of(i * bq, bq), bq)

        @pl.when(pl.program_id(2) == 0)
        def _():
            dq_scr[...] = jnp.zeros_like(dq_scr)

        @pl.when(i == j)
        def _():
            dk_acc[...] = jnp.zeros_like(dk_acc)
            dv_acc[...] = jnp.zeros_like(dv_acc)

        def step(masked):
            for t in range(HP):
                qk, vs = slice(t * 256, (t + 1) * 256), slice(t * VD, (t + 1) * VD)
                q, k = q_ref[0, :, qk], k_ref[0, :, qk]
                do = do_ref[0, :, vs].astype(bf16)
                pt = jnp.exp2(lax.dot_general(k, q, _NT, preferred_element_type=f32) - lse_ref[0, t])
                if masked:
                    r = lax.broadcasted_iota(jnp.int32, (bq, bq), 0)
                    c = lax.broadcasted_iota(jnp.int32, (bq, bq), 1)
                    pt = jnp.where(c >= r, pt, 0.0)
                dv_acc[t] += jnp.dot(pt.astype(bf16), do, preferred_element_type=f32)
                dpt = lax.dot_general(v_ref[0, :, vs], do, _NT, preferred_element_type=f32)
                dst = (pt * (dpt - dl_ref[0, t])).astype(bf16)
                dk_acc[t] += jnp.dot(dst, q, preferred_element_type=f32)
                dq_scr[t, rows, :] += lax.dot_general(dst, k, _TN, preferred_element_type=f32)

        @pl.when(i > j)
        def _():
            step(False)

        @pl.when(i == j)
        def _():
            step(True)
            for t in range(HP):
                d = dq_scr[t, rows, :] * (LN2 * Q_FOLD)
                stage[t, :, 0:NOPE] = d[:, 0:NOPE].astype(bf16)
                stage[t, :, NOPE:] = _rope_t(d[:, NOPE:], cos_ref[0], sin_ref[0]).astype(bf16)
                cp = pltpu.make_async_copy(stage.at[t], dq_hbm.at[b, rows, pl.ds(pl.multiple_of((hp * HP + t) * 256, 256), 256)],
                                           dq_sem.at[t])
                cp.start()
                cp.wait()

        @pl.when(i == nq - 1)
        def _():
            kr = jnp.zeros((bq, 128), f32)
            for t in range(HP):
                dkn_ref[0, :, t * NOPE:(t + 1) * NOPE] = (dk_acc[t, :, 0:NOPE] * LN2).astype(bf16)
                dv_ref[0, :, t * VD:(t + 1) * VD] = dv_acc[t].astype(bf16)
                kr = kr + dk_acc[t, :, NOPE:]
            dkr_ref[0] = kr * LN2

    kj = lambda t: _tri_cols(t, nq)[0]
    qi = lambda t: _tri_cols(t, nq)[1]
    kspec = pl.BlockSpec((1, bq, HP * 256), lambda b, h, t: (b, kj(t), h))
    vspec = pl.BlockSpec((1, bq, HP * VD), lambda b, h, t: (b, kj(t), h))
    krspec = pl.BlockSpec((1, bq, 128), lambda b, h, t: (b, kj(t), h))
    rspec = pl.BlockSpec((1, HP, 1, bq), lambda b, h, t: (b, h, 0, qi(t)))
    tspec = pl.BlockSpec((1, bq, 128), lambda b, h, t: (b, qi(t), 0))
    grid = (B, MLA_H // HP, nq * (nq + 1) // 2)
    body, c_args, c_in, c_out, c_shapes, c_sems = _fuse_exchange(body, comm, 8, 4, 5, grid)
    res = pl.pallas_call(body, name="attn_bwd_x" if comm else "attn_bwd", grid=grid,
                         in_specs=[pl.BlockSpec((1, bq, HP * 256), lambda b, h, t: (b, qi(t), h)), kspec, vspec,
                                   pl.BlockSpec((1, bq, HP * VD), lambda b, h, t: (b, qi(t), h)), rspec, rspec,
                                   tspec, tspec] + c_in,
                         out_specs=[vspec, vspec, krspec, pl.BlockSpec(memory_space=pltpu.HBM)] + c_out,
                         out_shape=[jax.ShapeDtypeStruct((B, S, MLA_H * NOPE), bf16), jax.ShapeDtypeStruct((B, S, MLA_H * VD), bf16),
                                    jax.ShapeDtypeStruct((B, S, MLA_H // HP * 128), f32),
                                    jax.ShapeDtypeStruct((B, S, MLA_H * 256), bf16)] + c_shapes,
                         scratch_shapes=[pltpu.VMEM((HP, bq, 256), f32), pltpu.VMEM((HP, bq, VD), f32), pltpu.VMEM((HP, S, 256), f32),
                                         pltpu.VMEM((HP, bq, 256), bf16), pltpu.SemaphoreType.DMA((HP,))] + c_sems,
                         compiler_params=_cparams(3))(q3, k3, v3, do3, lse_row, delta_row, cosf, sinf, *c_args)
    return res[3], res[0], res[1], res[2], list(res[4:])


def ada_fwd(c_all, w, b):
    n = w.shape[1]

    def body(c_ref, w_ref, b_ref, o_ref):
        o_ref[...] = jnp.dot(_silu(c_ref[...]).astype(bf16), w_ref[...].astype(bf16), preferred_element_type=f32) + b_ref[...]

    return pl.pallas_call(body, name="ada_fwd", out_shape=jax.ShapeDtypeStruct((c_all.shape[0], n), f32),
                          compiler_params=pltpu.CompilerParams(vmem_limit_bytes=VMEM_LIMIT))(c_all, w, b)


def ada_bwd(c_all, dmod):
    n = dmod.shape[1]

    def body(c_ref, d_ref, o_ref):
        o_ref[...] = lax.dot_general(_silu(c_ref[...]).astype(bf16), d_ref[...].astype(bf16), _TN, preferred_element_type=f32)

    return pl.pallas_call(body, name="ada_bwd", out_shape=jax.ShapeDtypeStruct((c_all.shape[1], n), f32),
                          compiler_params=pltpu.CompilerParams(vmem_limit_bytes=VMEM_LIMIT))(c_all, dmod)


def sum_leading(x, name):
    n, R, _ = x.shape
    tr = _pick(R, (512, 256, 128, 64, 32, 16, 8))

    def body(x_ref, o_ref):
        acc = x_ref[0].astype(f32)
        for k in range(1, n):
            acc = acc + x_ref[k].astype(f32)
        o_ref[...] = acc

    return pl.pallas_call(body, name=name, grid=(R // tr,), in_specs=[pl.BlockSpec((n, tr, 128), lambda i: (0, i, 0))],
                          out_specs=pl.BlockSpec((tr, 128), lambda i: (i, 0)), out_shape=jax.ShapeDtypeStruct((R, 128), f32),
                          compiler_params=_cparams(1))(x)


def _adamw_body(w_ref, g_ref, m_ref, v_ref, d_ref, mo_ref, vo_ref):
    gv = g_ref[...]
    mn = ADAM_B1 * m_ref[...] + (1.0 - ADAM_B1) * gv
    vn = ADAM_B2 * v_ref[...] + (1.0 - ADAM_B2) * jnp.square(gv)
    m_hat = mn / (1.0 - ADAM_B1 ** ADAM_STEP)
    v_hat = vn / (1.0 - ADAM_B2 ** ADAM_STEP)
    d_ref[...] = -ADAM_LR * (m_hat / (jnp.sqrt(v_hat) + ADAM_EPS) + ADAM_WD * w_ref[...])
    mo_ref[...] = mn
    vo_ref[...] = vn


def adamw(w, g, m, v):
    R = w.shape[0]
    tr = _pick(R, (512, 256, 128, 64, 32, 16, 8))
    spec = pl.BlockSpec((tr, 128), lambda i: (i, 0))
    shp = jax.ShapeDtypeStruct((R, 128), f32)
    return pl.pallas_call(functools.partial(_adamw_body), name="adamw", grid=(R // tr,), in_specs=[spec] * 4,
                          out_specs=[spec] * 3, out_shape=[shp] * 3, compiler_params=_cparams(1))(w, g, m, v)


def _row_tile(a):
    return _pick(a, (256, 128, 64, 32, 16, 8)) if a % 8 == 0 else a


def adamw_nd(w, g, m, v):
    L, a, b = w.shape
    ta = _row_tile(a)
    spec = pl.BlockSpec((1, ta, b), lambda l, i: (l, i, 0))
    shp = jax.ShapeDtypeStruct((L, a, b), f32)
    return pl.pallas_call(functools.partial(_adamw_body), name="adamw_nd", grid=(L, a // ta), in_specs=[spec] * 4,
                          out_specs=[spec] * 3, out_shape=[shp] * 3, compiler_params=_cparams(2))(w, g, m, v)


def sum_slots(x):
    n, L, a, b = x.shape
    ta = _row_tile(a)

    def body(x_ref, o_ref):
        acc = x_ref[0].astype(f32)
        for k in range(1, n):
            acc = acc + x_ref[k].astype(f32)
        o_ref[...] = acc

    return pl.pallas_call(body, name="sum_slots", grid=(L, a // ta),
                          in_specs=[pl.BlockSpec((n, 1, ta, b), lambda l, i: (0, l, i, 0))],
                          out_specs=pl.BlockSpec((1, ta, b), lambda l, i: (l, i, 0)),
                          out_shape=jax.ShapeDtypeStruct((L, a, b), f32), compiler_params=_cparams(2))(x)


def _exchange_copies(ins, outs, sems, scatter):
    send_sems, recv_sems, local_sems = sems
    x, y, c = lax.axis_index("x"), lax.axis_index("y"), lax.axis_index("c")
    me = 4 * x + 2 * y + c
    locals_, sends, recvs = [], [], []
    for a in range(len(ins)):
        locals_.append(pltpu.make_async_copy(ins[a].at[me] if scatter[a] else ins[a], outs[a].at[me], local_sems.at[a]))
        for k in range(N_DEV - 1):
            px = 1 - x if (k + 1) & 4 else x
            py = 1 - y if (k + 1) & 2 else y
            pc = 1 - c if (k + 1) & 1 else c
            pid = 4 * px + 2 * py + pc
            src = ins[a].at[pid] if scatter[a] else ins[a]
            for slot, group in ((me, sends), (pid, recvs)):
                group.append(pltpu.make_async_remote_copy(src_ref=src, dst_ref=outs[a].at[slot], send_sem=send_sems.at[a, k],
                                                          recv_sem=recv_sems.at[a, k], device_id=(px, py, pc),
                                                          device_id_type=pl.DeviceIdType.MESH))
    return locals_, sends, recvs


def _exchange_start(ins, outs, sems, scatter):
    locals_, sends, _ = _exchange_copies(ins, outs, sems, scatter)
    for cp in locals_ + sends:
        cp.start()


def _exchange_wait(ins, outs, sems, scatter):
    locals_, sends, recvs = _exchange_copies(ins, outs, sems, scatter)
    for cp in recvs:
        cp.wait_recv()
    for cp in sends:
        cp.wait_send()
    for cp in locals_:
        cp.wait()


def _exchange_shapes(arrays, scatter):
    return [jax.ShapeDtypeStruct((N_DEV,) + tuple(a.shape[1:] if s else a.shape), a.dtype) for a, s in zip(arrays, scatter)]


def _flags(scatter, n):
    return [scatter] * n if isinstance(scatter, bool) else list(scatter)


def _exchange_sems(n):
    return [pltpu.SemaphoreType.DMA((n, N_DEV - 1)), pltpu.SemaphoreType.DMA((n, N_DEV - 1)), pltpu.SemaphoreType.DMA((n,))]


def _fuse_exchange(core, comm, n_in, n_out, n_scr, grid):
    if comm is None:
        return core, [], [], [], [], []
    arrays, scatter = comm
    n = len(arrays)
    scatter = _flags(scatter, n)

    def body(*refs):
        a, b, c = n_in + n, n_in + n + n_out, n_in + 2 * n + n_out
        cin, cout, sems = refs[n_in:a], refs[b:c], refs[c + n_scr:]
        ids = [pl.program_id(d) for d in range(len(grid))]
        first = functools.reduce(lambda p, q: p & q, [i == 0 for i in ids])
        last = functools.reduce(lambda p, q: p & q, [i == g - 1 for i, g in zip(ids, grid)])

        @pl.when(first)
        def _():
            _exchange_start(cin, cout, sems, scatter)

        core(*refs[:n_in], *refs[a:b], *refs[c:c + n_scr])

        @pl.when(last)
        def _():
            _exchange_wait(cin, cout, sems, scatter)

    hbm = pl.BlockSpec(memory_space=pltpu.HBM)
    return body, list(arrays), [hbm] * n, [hbm] * n, _exchange_shapes(arrays, scatter), _exchange_sems(n)


def exchange(arrays, scatter, name):
    n = len(arrays)
    scatter = _flags(scatter, n)

    def body(*refs):
        ins, outs, sems = refs[:n], refs[n:2 * n], refs[2 * n:]
        _exchange_start(ins, outs, sems, scatter)
        _exchange_wait(ins, outs, sems, scatter)

    hbm = pl.BlockSpec(memory_space=pltpu.HBM)
    return pl.pallas_call(body, name=name, in_specs=[hbm] * n, out_specs=[hbm] * n,
                          out_shape=_exchange_shapes(arrays, scatter), scratch_shapes=_exchange_sems(n))(*arrays)


BIG = (("w_in", "col"), ("conv_w", "col"), ("w_uq", "col"), ("w_ukv", "col"), ("w_out", "row"), ("w_up", "col"),
       ("conv_ff_w", "col"), ("w_down", "row"))
SMALL = ("b_ada", "norm_mix", "conv_b", "dt_bias", "a_log", "d_skip", "ssd_norm", "q_norm", "kv_norm", "attn_norm",
         "norm_mlp", "conv_ff_b", "final_norm")
CONVS = ("conv_w", "conv_ff_w")
PACK_ALIGN = 2048


def _padded(n):
    return -(-n // PACK_ALIGN) * PACK_ALIGN


def _flat_pad(a):
    f = a.reshape(-1)
    return jnp.pad(f, (0, _padded(f.shape[0]) - f.shape[0]))


PACK_ROWS = 512


def pack(arrs):
    f = jnp.concatenate([_flat_pad(a) for a in arrs])
    n = PACK_ROWS * 128
    return jnp.pad(f, (0, -(-f.shape[0] // n) * n - f.shape[0])).reshape(-1, 128)


def unpack(flat, shapes):
    f = flat.reshape(-1)
    out, off = [], 0
    for s in shapes:
        n = int(np.prod(s))
        out.append(f[off:off + n].reshape(s))
        off += _padded(n)
    return out


def shards_to_full(g, kind):
    _, a, b = g.shape
    if kind == "col":
        return g.transpose(1, 0, 2).reshape(a, N_DEV * b)
    return g.reshape(N_DEV * a, b)


def full_to_shards(full, kind):
    if kind == "col":
        a, nb = full.shape
        return full.reshape(a, N_DEV, nb // N_DEV).transpose(1, 0, 2)
    na, b = full.shape
    return full.reshape(N_DEV, na // N_DEV, b)


def w_in_layout(w):
    z = lambda n: jnp.zeros(w.shape[:-1] + (n,), w.dtype)
    return jnp.concatenate([w[..., :2560], w[..., 2576:2960], z(128), w[..., 2960:3216], w[..., 3216:3280],
                            w[..., 2560:2576], z(48)], axis=-1)


def w_in_unlayout(g):
    return jnp.concatenate([g[..., :2560], g[..., 3392:3408], g[..., 2560:2944], g[..., 3072:3328], g[..., 3328:3392]], axis=-1)


def w_uq_layout(w):
    return jnp.pad(w.reshape(Q_RANK, MLA_H, QK), ((0, 0), (0, 0), (0, 256 - QK))).reshape(Q_RANK, MLA_H * 256)


def w_uq_unlayout(g):
    return g.reshape(Q_RANK, MLA_H, 256)[:, :, :QK].reshape(Q_RANK, MLA_H * QK)


def w_ukv_layout(w):
    return w.reshape(KV_RANK, MLA_H, 2, 128).transpose(0, 2, 1, 3).reshape(KV_RANK, 2 * MLA_H * 128)


def w_ukv_unlayout(g):
    return g.reshape(KV_RANK, 2, MLA_H, 128).transpose(0, 2, 1, 3).reshape(KV_RANK, 2 * MLA_H * 128)


LAYOUTS = {"w_in": (w_in_layout, w_in_unlayout), "w_uq": (w_uq_layout, w_uq_unlayout), "w_ukv": (w_ukv_layout, w_ukv_unlayout)}
FIRST, REST = BIG[:4], BIG[4:]


def layer_weights(gathered, entries):
    full = {n: shards_to_full(g, kind) for (n, kind), g in zip(entries, gathered)}
    return {n: LAYOUTS[n][0](w) if n in LAYOUTS else w for n, w in full.items()}


def layer_grad_slices(g, entries):
    return [full_to_shards(LAYOUTS[n][1](g[n]) if n in LAYOUTS else g[n], kind).astype(bf16) for n, kind in entries]


def _head_row(v):
    return jnp.zeros((1, 128), f32).at[0, DT_LANE:DT_LANE + SSD_HEADS].set(v)


def layer_fwd(x3, mod, W, P, l, cosf, sinf, comm=None, late=None, comm_up=None):
    B, S, _ = x3.shape
    T = B * S
    sv = {}
    h = normmod_fwd(x3, mod, P["norm_mix"][l][None], 0, 1)
    p = mm(h.reshape(T, D), W["w_in"], "nn", "mm_in")
    p3 = p.reshape(B, S, IN_COLS)
    bias_row, alog_row = _head_row(P["dt_bias"][l]), _head_row(P["a_log"][l])
    dcol = jnp.repeat(P["d_skip"][l], SSD_HD)[:, None]
    xc3, xpre = conv_ssd_fwd(p3, W["conv_w"], P["conv_b"][l][None])
    yc3, states = ssd_fwd(xc3, p3, bias_row, alog_row, dcol)
    y_ssd = ssd_out_fwd(yc3, p3, P["ssd_norm"][l][None])
    cqn = rms_fwd(p3, 512, OFF_CQ // 512, Q_RANK, P["q_norm"][l][None], "rms_q_fwd")
    ckvn = rms_fwd(p3, KV_RANK, OFF_CKV // KV_RANK, KV_RANK, P["kv_norm"][l][None], "rms_kv_fwd")
    q3 = mm(cqn.reshape(T, Q_RANK), W["w_uq"], "nn", "mm_uq", out_dtype=bf16,
            rope=(cosf.reshape(T, 128), sinf.reshape(T, 128))).reshape(B, S, -1)
    k3 = k_proj(ckvn, W["w_ukv"], p3, cosf, sinf)
    v3 = mm(ckvn.reshape(T, KV_RANK), W["w_ukv"][:, MLA_H * NOPE:], "nn", "mm_uv", out_dtype=bf16).reshape(B, S, -1)
    o3, lse, comm_out = attn_fwd(q3, k3, v3, comm)
    if late is not None:
        W = dict(W, **late(comm_out))
    y_att = rms_fwd(o3, D, 0, D, P["attn_norm"][l][None], "rms_o_fwd")
    cat = (y_ssd.reshape(T, D), y_att.reshape(T, D))
    x1, y1 = mm(cat, W["w_out"], "nn", "mm_out", resid=x3.reshape(T, D), gate=mod[:, 2:3, :], seq=S)
    x13 = x1.reshape(B, S, D)
    h2 = normmod_fwd(x13, mod, P["norm_mlp"][l][None], 3, 4)
    u, up_out = mm(h2.reshape(T, D), W["w_up"], "nn", "mm_up", out_dtype=bf16, comm=comm_up), []
    if comm_up is not None:
        u, up_out = u
    u3 = u.reshape(B, S, 2 * D_FF)
    a, ffg, ffv = glu_fwd(u3, W["conv_ff_w"], P["conv_ff_b"][l][None])
    x2, y2 = mm(a.reshape(T, D_FF), W["w_down"], "nn", "mm_down", resid=x1, gate=mod[:, 5:6, :], seq=S)
    sv.update(x=x3, h=h, p3=p3, xc3=xc3, xpre=xpre, yc3=yc3, states=states, cqn=cqn, ckvn=ckvn, q3=q3, k3=k3, v3=v3, o3=o3, lse=lse,
              cat=cat, y1=y1, x1=x13, h2=h2, u3=u3, ffg=ffg, ffv=ffv, a=a, y2=y2, bias_row=bias_row, alog_row=alog_row, dcol=dcol)
    return x2.reshape(B, S, D), sv, comm_out, W, up_out


def layer_bwd(dx3, sv, mod, W, P, l, cosf, sinf, comm=None, send_rest=False):
    B, S, _ = dx3.shape
    T = B * S
    g = {}
    dy2, dg2 = gate_bwd(dx3, sv["y2"].reshape(B, S, D), mod, 5)
    dy2 = dy2.reshape(T, D)
    da = mm(dy2, W["w_down"], "nt", "mm_down_dx", out_dtype=bf16)
    g["w_down"] = mm(sv["a"].reshape(T, D_FF), dy2, "tn", "mm_down_dw")
    dug, duv, dwg, dwv, dbg, dbv = glu_bwd(sv["u3"], sv["ffg"], sv["ffv"], da.reshape(B, S, D_FF), W["conv_ff_w"])
    g["conv_ff_w"] = jnp.concatenate([dwg, dwv], axis=1)
    g["conv_ff_b"] = jnp.concatenate([dbg, dbv], axis=1)[0]
    du = (dug.reshape(T, D_FF), duv.reshape(T, D_FF))
    dh2 = mm(du, W["w_up"], "nt", "mm_up_dx", out_dtype=bf16)
    g["w_up"] = jnp.concatenate([mm(sv["h2"].reshape(T, D), d, "tn", "mm_up_dw") for d in du], axis=1)
    dx1, dsh2, dsc2, dnm, dy1, dg1 = normmod_bwd(sv["x1"], dh2.reshape(B, S, D), dx3, mod, P["norm_mlp"][l][None], 4,
                                                 y3=sv["y1"].reshape(B, S, D), i_g=2)
    g["norm_mlp"] = dnm[0]
    dy1 = dy1.reshape(T, D)
    dcat = mm(dy1, W["w_out"], "nt", "mm_out_dx", out_dtype=bf16)
    g["w_out"] = jnp.concatenate([mm(part, dy1, "tn", "mm_out_dw") for part in sv["cat"]], axis=0)
    dcat3 = dcat.reshape(B, S, 2 * D)
    dyc3, dp, dsn = ssd_out_bwd(sv["yc3"], sv["p3"], dcat3, P["ssd_norm"][l][None])
    g["ssd_norm"] = dsn[0]
    do3, dan, delta = rms_o_bwd(sv["o3"], dcat3, P["attn_norm"][l][None])
    g["attn_norm"] = dan[0]
    dqraw, dkn3, dv3, dkr3, comm_out = attn_bwd(sv["q3"], sv["k3"], sv["v3"], do3, sv["lse"], delta.reshape(B, MLA_H, 1, S),
                                                cosf, sinf, comm)
    dqraw = dqraw.reshape(T, -1)
    dcqn = mm(dqraw, W["w_uq"], "nt", "mm_uq_dx")
    g["w_uq"] = mm(sv["cqn"].reshape(T, Q_RANK), dqraw, "tn", "mm_uq_dw")
    dp, dqn = rms_bwd_into(sv["p3"], 512, OFF_CQ // 512, Q_RANK, dcqn.reshape(B, S, Q_RANK), P["q_norm"][l][None], dp, "rms_q_bwd")
    g["q_norm"] = dqn[0]
    bias_col, alog_col = sv["bias_row"].reshape(128, 1), sv["alog_row"].reshape(128, 1)
    comm_rest = (layer_grad_slices(g, REST), True) if send_rest else None
    dxc3, ddt3, dalog, dd, dbias, rest_out = ssd_bwd(sv["xc3"], sv["p3"], dyc3, sv["states"], sv["bias_row"], sv["alog_row"],
                                                     bias_col, alog_col, sv["dcol"], comm_rest)
    heads = slice(DT_LANE, DT_LANE + SSD_HEADS)
    g["a_log"], g["d_skip"], g["dt_bias"] = dalog[heads, 0], dd[heads, 0], dbias[heads, 0]
    dp, dcw, dcb = conv_ssd_bwd(sv["p3"], sv["xpre"], dxc3, W["conv_w"], dp)
    g["conv_w"], g["conv_b"] = dcw, dcb[0]
    dp = kprep_bwd(dkr3, ddt3, cosf, sinf, dp)
    dkv = (dkn3.reshape(T, -1), dv3.reshape(T, -1))
    dckvn = mm(dkv, W["w_ukv"], "nt", "mm_ukv_dx")
    g["w_ukv"] = jnp.concatenate([mm(sv["ckvn"].reshape(T, KV_RANK), d, "tn", "mm_ukv_dw") for d in dkv], axis=1)
    dp, dkn = rms_bwd_into(sv["p3"], KV_RANK, OFF_CKV // KV_RANK, KV_RANK, dckvn.reshape(B, S, KV_RANK), P["kv_norm"][l][None], dp,
                           "rms_kv_bwd")
    g["kv_norm"] = dkn[0]
    dp = dp.reshape(T, IN_COLS)
    dh = mm(dp, W["w_in"], "nt", "mm_in_dx", out_dtype=bf16)
    g["w_in"] = mm(sv["h"].reshape(T, D), dp, "tn", "mm_in_dw")
    dx0, dsh1, dsc1, dnx = normmod_bwd(sv["x"], dh.reshape(B, S, D), dx1, mod, P["norm_mix"][l][None], 1)
    g["norm_mix"] = dnx[0]
    dmod = jnp.concatenate([dsh1, dsc1, dg1, dsh2, dsc2, dg2], axis=1)
    return dx0, dmod, g, comm_out, rest_out


def kernel(x, c, positions, w_ada, b_ada, norm_mix, w_in, conv_w, conv_b, dt_bias, a_log, d_skip, ssd_norm, q_norm, w_uq, kv_norm, w_ukv, attn_norm, w_out, norm_mlp, w_up, conv_ff_w, conv_ff_b, w_down, final_norm, loss_target, m_w_ada, m_b_ada, m_norm_mix, m_w_in, m_conv_w, m_conv_b, m_dt_bias, m_a_log, m_d_skip, m_ssd_norm, m_q_norm, m_w_uq, m_kv_norm, m_w_ukv, m_attn_norm, m_w_out, m_norm_mlp, m_w_up, m_conv_ff_w, m_conv_ff_b, m_w_down, m_final_norm, v_w_ada, v_b_ada, v_norm_mix, v_w_in, v_conv_w, v_conv_b, v_dt_bias, v_a_log, v_d_skip, v_ssd_norm, v_q_norm, v_w_uq, v_kv_norm, v_w_ukv, v_attn_norm, v_w_out, v_norm_mlp, v_w_up, v_conv_ff_w, v_conv_ff_b, v_w_down, v_final_norm):
    given = dict(locals())
    B, S, _ = x.shape
    me = 4 * lax.axis_index("x") + 2 * lax.axis_index("y") + lax.axis_index("c")
    P = {n: given[n] for n in SMALL}

    def shards(l, entries):
        return [given[n][l] if n in CONVS else given[n][l].astype(bf16) for n, _ in entries]

    *gathered, c_all = exchange(shards(0, FIRST) + [c], False, "gather_weights")
    W = [layer_weights(gathered, FIRST), None]

    n_ada = w_ada.shape[2]
    c_all = c_all.reshape(N_DEV * B, D)
    b_sh = lax.dynamic_slice_in_dim(b_ada, me * n_ada, n_ada, axis=1)
    mod_sh = jnp.stack([ada_fwd(c_all, w_ada[l], b_sh[l][None]) for l in range(DEPTH)])
    (mod_g,) = exchange([mod_sh], False, "gather_mod")
    mod_mine = lax.dynamic_slice_in_dim(mod_g, me * B, B, axis=2)
    mods = mod_mine.transpose(1, 2, 0, 3).reshape(DEPTH, B, 6, D)

    inv_freq = jnp.asarray(1.0 / (ROPE_BASE ** (np.arange(0, ROPE, 2, dtype=np.float32) / ROPE)))
    ang = positions.astype(f32)[..., None] * inv_freq
    zeros = jnp.zeros((B, S, 128 - ROPE), f32)
    cosf = jnp.concatenate([jnp.cos(ang), jnp.cos(ang), zeros], axis=-1)
    sinf = jnp.concatenate([jnp.sin(ang), jnp.sin(ang), zeros], axis=-1)

    saved = [None] * DEPTH
    late = lambda got: layer_weights(got, REST)
    xl, saved[0], _, W[0], gathered = layer_fwd(x, mods[0], W[0], P, 0, cosf, sinf, comm=(shards(0, REST), False), late=late,
                                                comm_up=(shards(1, FIRST), False))
    xl, saved[1], _, W[1], _ = layer_fwd(xl, mods[1], layer_weights(gathered, FIRST), P, 1, cosf, sinf,
                                         comm=(shards(1, REST), False), late=late)
    dxl, d_final, loss_part = final_loss(xl, final_norm[None], loss_target)
    grads, dmods, recv = [None] * DEPTH, [None] * DEPTH, [None] * DEPTH
    dxl, dmods[1], grads[1], _, _ = layer_bwd(dxl, saved[1], mods[1], W[1], P, 1, cosf, sinf)
    grad_x, dmods[0], grads[0], recv[1], recv_rest = layer_bwd(dxl, saved[0], mods[0], W[0], P, 0, cosf, sinf,
                                                               comm=(layer_grad_slices(grads[1], BIG), True), send_rest=True)

    stack = lambda n: jnp.stack([grads[l][n] for l in range(DEPTH)])
    small_names = [n for n in SMALL if n not in ("b_ada", "final_norm")]
    partial = pack([stack(n) for n in small_names] + [d_final[0], loss_part[0]])
    dmod_all = jnp.stack(dmods)
    *recv_first, part_g, dmod_g = exchange(layer_grad_slices(grads[0], FIRST) + [partial, dmod_all],
                                           [True] * len(FIRST) + [False, False], "exchange_tail")
    recv[0] = recv_first + recv_rest
    big_g = [jnp.concatenate([sum_slots(recv[l][i][:, None]) for l in range(DEPTH)]) for i in range(len(BIG))]
    small_sum = sum_leading(part_g, "sum_partials")
    small_g = unpack(small_sum, [given[n].shape for n in small_names] + [(D,), (128,)])
    gsmall = dict(zip(small_names + ["final_norm"], small_g[:-1]))
    loss = small_g[-1][0]
    dmod_rows = dmod_g.transpose(0, 2, 1, 3, 4).reshape(N_DEV * B, DEPTH * 6 * D)
    gsmall["b_ada"] = sum_leading(dmod_rows.reshape(N_DEV * B, -1, 128), "sum_b_ada").reshape(DEPTH, 6 * D)
    dmod_cols = dmod_rows.reshape(N_DEV * B, DEPTH, N_DEV, n_ada)
    dmod_sh = lax.dynamic_slice_in_dim(dmod_cols, me, 1, axis=2)[:, :, 0, :]
    g_w_ada = jnp.stack([ada_bwd(c_all, dmod_sh[:, l, :]) for l in range(DEPTH)])

    res = {"grad": {}, "delta": {}, "new_m": {}, "new_v": {}}
    for n, gv in zip([n for n, _ in BIG] + ["w_ada"], big_g + [g_w_ada]):
        res["grad"][n] = gv
        res["delta"][n], res["new_m"][n], res["new_v"][n] = adamw_nd(given[n], gv, given["m_" + n], given["v_" + n])
    shapes = [given[n].shape for n in SMALL]
    flat = adamw(pack([given[n] for n in SMALL]), pack([gsmall[n] for n in SMALL]), pack([given["m_" + n] for n in SMALL]),
                 pack([given["v_" + n] for n in SMALL]))
    for n in SMALL:
        res["grad"][n] = gsmall[n]
    for key, arr in zip(("delta", "new_m", "new_v"), flat):
        res[key].update(zip(SMALL, unpack(arr, shapes)))
    order = ["w_ada", "b_ada", "norm_mix", "w_in", "conv_w", "conv_b", "dt_bias", "a_log", "d_skip", "ssd_norm", "q_norm", "w_uq",
             "kv_norm", "w_ukv", "attn_norm", "w_out", "norm_mlp", "w_up", "conv_ff_w", "conv_ff_b", "w_down", "final_norm"]
    return (loss, grad_x, *[res[k][n] for k in ("grad", "delta", "new_m", "new_v") for n in order])
```

```python
import functools

import numpy as np
import jax
import jax.numpy as jnp
from jax import lax
from jax.experimental import pallas as pl
from jax.experimental.pallas import tpu as pltpu

f32, bf16 = jnp.float32, jnp.bfloat16
HIGHEST = lax.Precision.HIGHEST

D = 1024
D_SSD = 1024
SSD_HEADS = 16
SSD_HD = 64
SSD_N = 128
CHUNK = 128
D_XBC = 1536
CONV_K = 4
MLA_H = 8
NOPE = 128
ROPE = 64
VD = 128
QK = NOPE + ROPE
Q_RANK = 384
KV_RANK = 256
D_FF = 2816
FF_K = 3
EPS = 1e-6
ROPE_BASE = 10000.0
DEPTH = 2
ADAM_LR, ADAM_B1, ADAM_B2, ADAM_EPS, ADAM_WD, ADAM_STEP = 0.001, 0.9, 0.999, 1e-08, 0.01, 10

N_DEV = 8
IN_COLS = 3456
OFF_XBC, OFF_CQ, OFF_CKV, OFF_KRDT = 1024, 2560, 3072, 3328
DT_LANE = 64
VMEM_LIMIT = 48 * 1024 * 1024
MM_K_WHOLE = 4096


def _cparams(n_grid):
    return pltpu.CompilerParams(dimension_semantics=("arbitrary",) * n_grid, vmem_limit_bytes=VMEM_LIMIT)


def _pick(n, cands):
    for c in cands:
        if n % c == 0:
            return c
    return n


def _silu(x):
    return x * jax.nn.sigmoid(x)


def _dsilu(x):
    s = jax.nn.sigmoid(x)
    return s * (1.0 + x * (1.0 - s))


def _rowsum(x):
    return jnp.sum(x, axis=0, keepdims=True)


def mm(a, b, mode, name, out_dtype=f32, resid=None, gate=None, seq=None, comm=None, rope=None):
    parts = list(a) if isinstance(a, (tuple, list)) else [a]
    np_ = len(parts)
    if mode == "nn":
        (M, Kp), N = parts[0].shape, b.shape[1]
    elif mode == "nt":
        (M, Kp), N = parts[0].shape, b.shape[0]
    else:
        (Kp, M), N = parts[0].shape, b.shape[1]
    K = Kp * np_
    gated = resid is not None
    whole = np_ > 1 or K <= MM_K_WHOLE
    tm = _pick(seq if gated else M, (1024, 1408, 512, 384, 256, 128) if K <= MM_K_WHOLE else (512, 256, 128))
    tn = _pick(N, (512, 1408, 384, 256, 128))
    tk = K if whole else _pick(K, (2816, 2048, 1024, 512))
    if mode == "tn":
        tm = _pick(M, (1024, 1408, 512, 384, 256, 128))
    nk = K // tk
    dims = {"nn": ((1,), (0,)), "nt": ((1,), (1,)), "tn": ((0,), (0,))}[mode]

    def body(*refs):
        a_refs, b_ref, rest = refs[:np_], refs[np_], refs[np_ + 1:]
        if rope is not None:
            cos_ref, sin_ref, rest = rest[0], rest[1], rest[2:]
        if gated:
            r_ref, g_ref, o_ref, y_ref, acc = rest
        else:
            o_ref, acc = rest

        def finish(res):
            if gated:
                y_ref[...] = res
                o_ref[...] = r_ref[...] + g_ref[0] * res
            elif rope is not None:
                for h in range(tn // 256):
                    lo = h * 256
                    o_ref[:, lo:lo + NOPE] = (res[:, lo:lo + NOPE] * Q_FOLD).astype(out_dtype)
                    o_ref[:, lo + NOPE:lo + 256] = (_rope(res[:, lo + NOPE:lo + 256], cos_ref[...], sin_ref[...]) * Q_FOLD).astype(out_dtype)
            else:
                o_ref[...] = res.astype(out_dtype)

        prod = None
        for p, a_ref in enumerate(a_refs):
            if np_ == 1:
                bv = b_ref[...]
            else:
                bv = b_ref[:, p * Kp:(p + 1) * Kp] if mode == "nt" else b_ref[p * Kp:(p + 1) * Kp, :]
            term = lax.dot_general(a_ref[...].astype(bf16), bv.astype(bf16), (dims, ((), ())), preferred_element_type=f32)
            prod = term if prod is None else prod + term
        if nk == 1:
            finish(prod)
        else:
            k = pl.program_id(2)

            @pl.when(k == 0)
            def _():
                acc[...] = prod

            @pl.when(k > 0)
            def _():
                acc[...] += prod

            @pl.when(k == nk - 1)
            def _():
                finish(acc[...])

    if np_ > 1:
        a_spec = pl.BlockSpec((tm, Kp), lambda i, j, k: (i, 0))
    elif mode == "tn":
        a_spec = pl.BlockSpec((tk, tm), lambda i, j, k: (k, i))
    else:
        a_spec = pl.BlockSpec((tm, tk), lambda i, j, k: (i, k))
    b_spec = pl.BlockSpec((tn, tk), lambda i, j, k: (j, k)) if mode == "nt" else pl.BlockSpec((tk, tn), lambda i, j, k: (k, j))
    o_spec = pl.BlockSpec((tm, tn), lambda i, j, k: (i, j))
    in_specs, args = [a_spec] * np_ + [b_spec], parts + [b]
    if rope is not None:
        in_specs += [pl.BlockSpec((tm, 128), lambda i, j, k: (i, 0))] * 2
        args += list(rope)
    out_specs, out_shape = [o_spec], [jax.ShapeDtypeStruct((M, N), out_dtype)]
    if gated:
        per = seq // tm
        in_specs += [o_spec, pl.BlockSpec((1, 1, tn), lambda i, j, k: (i // per, 0, j))]
        args += [resid, gate]
        out_specs = [o_spec, o_spec]
        out_shape = [jax.ShapeDtypeStruct((M, N), f32), jax.ShapeDtypeStruct((M, N), f32)]
    grid = (M // tm, N // tn, nk)
    body, c_args, c_in, c_out, c_shapes, c_sems = _fuse_exchange(body, comm, len(args), len(out_specs), 1, grid)
    res = pl.pallas_call(body, name=name + "_x" if comm else name, grid=grid, in_specs=in_specs + c_in, out_specs=out_specs + c_out,
                         out_shape=out_shape + c_shapes, scratch_shapes=[pltpu.VMEM((tm, tn), f32)] + c_sems,
                         compiler_params=_cparams(3))(*args, *c_args)
    own = res[:len(out_specs)]
    own = own[0] if len(own) == 1 else tuple(own)
    return (own, list(res[len(out_specs):])) if comm else own


def _tok(ts, width, cb=0):
    return pl.BlockSpec((1, ts, width), lambda b, s: (b, s, cb))


def _perb(rows, width):
    return pl.BlockSpec((1, rows, width), lambda b, s: (b, 0, 0))


def _const(rows, width):
    return pl.BlockSpec((rows, width), lambda b, s: (0, 0))


def _row_call(body, name, B, S, ts, in_specs, out_specs, out_shape, scratch=(), aliases=None):
    return pl.pallas_call(body, name=name, grid=(B, S // ts), in_specs=in_specs, out_specs=out_specs,
                          out_shape=out_shape, scratch_shapes=list(scratch), input_output_aliases=aliases or {},
                          compiler_params=_cparams(2))


def _first():
    return (pl.program_id(0) == 0) & (pl.program_id(1) == 0)


def normmod_fwd(x3, mod, g, i_sh, i_sc):
    B, S, C = x3.shape
    ts = _pick(S, (512, 256, 128))

    def body(x_ref, mod_ref, g_ref, h_ref):
        x = x_ref[0]
        r = lax.rsqrt(jnp.mean(x * x, axis=-1, keepdims=True) + EPS)
        n = x * r * g_ref[...]
        h_ref[0] = (n * (1.0 + mod_ref[0, i_sc:i_sc + 1, :]) + mod_ref[0, i_sh:i_sh + 1, :]).astype(bf16)

    return _row_call(body, "normmod_fwd", B, S, ts, [_tok(ts, C), _perb(6, C), _const(1, C)], _tok(ts, C),
                     jax.ShapeDtypeStruct((B, S, C), bf16))(x3, mod, g)


def normmod_bwd(x3, dh3, resid3, mod, g, i_sc, y3=None, i_g=None, comm=None):
    B, S, C = x3.shape
    ts = _pick(S, (512, 256, 128))
    gated = y3 is not None

    def body(x_ref, dh_ref, r_ref, mod_ref, g_ref, *rest):
        if gated:
            y_ref, dx_ref, dsh_ref, dsc_ref, dg_ref, dy_ref, dgate_ref = rest
        else:
            dx_ref, dsh_ref, dsc_ref, dg_ref = rest

        @pl.when(pl.program_id(1) == 0)
        def _():
            dsh_ref[...] = jnp.zeros_like(dsh_ref)
            dsc_ref[...] = jnp.zeros_like(dsc_ref)
            if gated:
                dgate_ref[...] = jnp.zeros_like(dgate_ref)

        @pl.when(_first())
        def _():
            dg_ref[...] = jnp.zeros_like(dg_ref)

        x, dh, gv = x_ref[0], dh_ref[0].astype(f32), g_ref[...]
        r = lax.rsqrt(jnp.mean(x * x, axis=-1, keepdims=True) + EPS)
        xh = x * r
        dn = dh * (1.0 + mod_ref[0, i_sc:i_sc + 1, :])
        dsh_ref[0] += _rowsum(dh)
        dsc_ref[0] += _rowsum(dh * xh * gv)
        dg_ref[...] += _rowsum(dn * xh)
        dxh = dn * gv
        dx = r * (dxh - xh * jnp.mean(dxh * xh, axis=-1, keepdims=True)) + r_ref[0]
        dx_ref[0] = dx
        if gated:
            dy_ref[0] = (dx * mod_ref[0, i_g:i_g + 1, :]).astype(bf16)
            dgate_ref[0] += _rowsum(dx * y_ref[0])

    in_specs = [_tok(ts, C), _tok(ts, C), _tok(ts, C), _perb(6, C), _const(1, C)]
    out_specs = [_tok(ts, C), _perb(1, C), _perb(1, C), _const(1, C)]
    out_shape = [jax.ShapeDtypeStruct((B, S, C), f32), jax.ShapeDtypeStruct((B, 1, C), f32),
                 jax.ShapeDtypeStruct((B, 1, C), f32), jax.ShapeDtypeStruct((1, C), f32)]
    args = [x3, dh3, resid3, mod, g]
    if gated:
        in_specs.append(_tok(ts, C))
        args.append(y3)
        out_specs += [_tok(ts, C), _perb(1, C)]
        out_shape += [jax.ShapeDtypeStruct((B, S, C), bf16), jax.ShapeDtypeStruct((B, 1, C), f32)]
    if comm is None:
        return _row_call(body, "normmod_gate_bwd" if gated else "normmod_bwd", B, S, ts, in_specs, out_specs, out_shape)(*args)
    grid = (B, S // ts)
    body, c_args, c_in, c_out, c_shapes, c_sems = _fuse_exchange(body, comm, len(args), len(out_specs), 0, grid)
    res = pl.pallas_call(body, name="normmod_bwd_x", grid=grid, in_specs=in_specs + c_in, out_specs=out_specs + c_out,
                         out_shape=out_shape + c_shapes, scratch_shapes=c_sems, compiler_params=_cparams(2))(*args, *c_args)
    return (*res[:len(out_specs)], list(res[len(out_specs):]))


def gate_bwd(dx3, y3, mod, i_g):
    B, S, C = dx3.shape
    ts = _pick(S, (512, 256, 128))

    def body(dx_ref, y_ref, mod_ref, dy_ref, dgate_ref):
        @pl.when(pl.program_id(1) == 0)
        def _():
            dgate_ref[...] = jnp.zeros_like(dgate_ref)

        dx = dx_ref[0]
        dy_ref[0] = (dx * mod_ref[0, i_g:i_g + 1, :]).astype(bf16)
        dgate_ref[0] += _rowsum(dx * y_ref[0])

    return _row_call(body, "gate_bwd", B, S, ts, [_tok(ts, C), _tok(ts, C), _perb(6, C)], [_tok(ts, C), _perb(1, C)],
                     [jax.ShapeDtypeStruct((B, S, C), bf16), jax.ShapeDtypeStruct((B, 1, C), f32)])(dx3, y3, mod)


def rms_fwd(src3, width, cb, n, g, name):
    B, S, _ = src3.shape
    ts = _pick(S, (512, 256, 128))

    def body(x_ref, g_ref, o_ref):
        x = x_ref[0][:, :n]
        r = lax.rsqrt(jnp.mean(x * x, axis=-1, keepdims=True) + EPS)
        o_ref[0] = (x * r * g_ref[...]).astype(bf16)

    return _row_call(body, name, B, S, ts, [_tok(ts, width, cb), _const(1, n)], _tok(ts, n),
                     jax.ShapeDtypeStruct((B, S, n), bf16))(src3, g)


def rms_bwd_into(src3, width, cb, n, dout3, g, dp, name):
    B, S, _ = src3.shape
    ts = _pick(S, (512, 256, 128))

    def body(x_ref, do_ref, g_ref, dp_in, dp_ref, dg_ref):
        @pl.when(_first())
        def _():
            dg_ref[...] = jnp.zeros_like(dg_ref)

        x = x_ref[0][:, :n]
        do = do_ref[0]
        r = lax.rsqrt(jnp.mean(x * x, axis=-1, keepdims=True) + EPS)
        xh = x * r
        dg_ref[...] += _rowsum(do * xh)
        dxh = do * g_ref[...]
        dp_ref[0, :, :n] = (r * (dxh - xh * jnp.mean(dxh * xh, axis=-1, keepdims=True))).astype(bf16)
        if width > n:
            dp_ref[0, :, n:] = jnp.zeros((ts, width - n), bf16)

    return _row_call(body, name, B, S, ts, [_tok(ts, width, cb), _tok(ts, n), _const(1, n), pl.BlockSpec(memory_space=pl.ANY)],
                     [_tok(ts, width, cb), _const(1, n)], [jax.ShapeDtypeStruct(dp.shape, bf16), jax.ShapeDtypeStruct((1, n), f32)],
                     aliases={3: 0})(src3, dout3, g, dp)


def final_loss(x3, g, tgt3):
    B, S, C = x3.shape
    ts = _pick(S, (512, 256, 128))

    def body(x_ref, g_ref, t_ref, dx_ref, dg_ref, loss_ref):
        @pl.when(_first())
        def _():
            dg_ref[...] = jnp.zeros_like(dg_ref)
            loss_ref[...] = jnp.zeros_like(loss_ref)

        x, gv = x_ref[0], g_ref[...]
        r = lax.rsqrt(jnp.mean(x * x, axis=-1, keepdims=True) + EPS)
        xh = x * r
        e = xh * gv - t_ref[0]
        loss_ref[...] += 0.5 * jnp.sum(e * e) / C
        dout = e / C
        dg_ref[...] += _rowsum(dout * xh)
        dxh = dout * gv
        dx_ref[0] = r * (dxh - xh * jnp.mean(dxh * xh, axis=-1, keepdims=True))

    return _row_call(body, "final_loss", B, S, ts, [_tok(ts, C), _const(1, C), _tok(ts, C)],
                     [_tok(ts, C), _const(1, C), _const(1, 128)],
                     [jax.ShapeDtypeStruct((B, S, C), f32), jax.ShapeDtypeStruct((1, C), f32),
                      jax.ShapeDtypeStruct((1, 128), f32)])(x3, g, tgt3)


def ssd_out_fwd(yc3, p3, w):
    B, S, C = yc3.shape
    ts = _pick(S, (512, 256, 128))
    half = C // 2

    def body(y_ref, z_ref, w_ref, o_ref):
        y = y_ref[0] * _silu(z_ref[0])
        for lo in (0, half):
            yg = y[:, lo:lo + half]
            r = lax.rsqrt(jnp.mean(yg * yg, axis=-1, keepdims=True) + EPS)
            o_ref[0, :, lo:lo + half] = (yg * r * w_ref[:, lo:lo + half]).astype(bf16)

    return _row_call(body, "ssd_out_fwd", B, S, ts, [_tok(ts, C), _tok(ts, C, 0), _const(1, C)], _tok(ts, C),
                     jax.ShapeDtypeStruct((B, S, C), bf16))(yc3, p3, w)


def ssd_out_bwd(yc3, p3, dcat3, w):
    B, S, C = yc3.shape
    ts = _pick(S, (512, 256, 128))
    half = C // 2

    def body(y_ref, z_ref, do_ref, w_ref, dyc_ref, dz_ref, dw_ref):
        @pl.when(_first())
        def _():
            dw_ref[...] = jnp.zeros_like(dw_ref)

        yc, z, do = y_ref[0], z_ref[0], do_ref[0].astype(f32)
        sz = _silu(z)
        y = yc * sz
        for lo in (0, half):
            sl = slice(lo, lo + half)
            yg, dog, wg = y[:, sl], do[:, sl], w_ref[:, sl]
            r = lax.rsqrt(jnp.mean(yg * yg, axis=-1, keepdims=True) + EPS)
            yh = yg * r
            dw_ref[:, sl] += _rowsum(dog * yh)
            dyh = dog * wg
            dy = r * (dyh - yh * jnp.mean(dyh * yh, axis=-1, keepdims=True))
            dyc_ref[0, :, sl] = dy * sz[:, sl]
            dz_ref[0, :, sl] = (dy * yc[:, sl] * _dsilu(z[:, sl])).astype(bf16)

    return _row_call(body, "ssd_out_bwd", B, S, ts, [_tok(ts, C), _tok(ts, C, 0), _tok(ts, C, 0), _const(1, C)],
                     [_tok(ts, C), _tok(ts, C, 0), _const(1, C)],
                     [jax.ShapeDtypeStruct((B, S, C), f32), jax.ShapeDtypeStruct((B, S, IN_COLS), bf16),
                      jax.ShapeDtypeStruct((1, C), f32)])(yc3, p3, dcat3, w)


def _rot(t):
    lane = lax.broadcasted_iota(jnp.int32, t.shape, 1)
    return jnp.where(lane < ROPE // 2, -pltpu.roll(t, 128 - ROPE // 2, 1), pltpu.roll(t, ROPE // 2, 1))


def _rope(t, cosf, sinf):
    return t * cosf + _rot(t) * sinf


def _rope_t(d, cosf, sinf):
    return d * cosf - _rot(d * sinf)


def k_proj(ckvn, w, p3, cosf, sinf):
    B, S, _ = ckvn.shape
    ts = _pick(S, (1024, 512, 256, 128))

    def body(x_ref, w_ref, kr_ref, c_ref, s_ref, o_ref):
        lane = lax.broadcasted_iota(jnp.int32, (1, 128), 1)
        kr = _rope(jnp.where(lane < ROPE, kr_ref[0], 0.0), c_ref[0], s_ref[0]).astype(bf16)
        res = jnp.dot(x_ref[0], w_ref[...], preferred_element_type=f32)
        for h in range(2):
            o_ref[0, :, h * 256:h * 256 + NOPE] = res[:, h * NOPE:(h + 1) * NOPE].astype(bf16)
            o_ref[0, :, h * 256 + NOPE:(h + 1) * 256] = kr

    return pl.pallas_call(body, name="k_proj", grid=(B, S // ts, MLA_H // 2),
                          in_specs=[pl.BlockSpec((1, ts, KV_RANK), lambda b, s, j: (b, s, 0)),
                                    pl.BlockSpec((KV_RANK, 2 * NOPE), lambda b, s, j: (0, j)),
                                    pl.BlockSpec((1, ts, 128), lambda b, s, j: (b, s, OFF_KRDT // 128)),
                                    pl.BlockSpec((1, ts, 128), lambda b, s, j: (b, s, 0)),
                                    pl.BlockSpec((1, ts, 128), lambda b, s, j: (b, s, 0))],
                          out_specs=pl.BlockSpec((1, ts, 512), lambda b, s, j: (b, s, j)),
                          out_shape=jax.ShapeDtypeStruct((B, S, MLA_H * 256), bf16), compiler_params=_cparams(3))(ckvn, w, p3, cosf, sinf)


def kprep_bwd(dkr3, ddt3, cosf, sinf, dp):
    B, S, W = dkr3.shape
    ts = _pick(S, (512, 256, 128))

    def body(dk_ref, ddt_ref, c_ref, s_ref, dp_in, kr_ref):
        acc = dk_ref[0, :, 0:128]
        for h in range(1, W // 128):
            acc = acc + dk_ref[0, :, h * 128:(h + 1) * 128]
        lane = lax.broadcasted_iota(jnp.int32, (1, 128), 1)
        kr_ref[0] = jnp.where(lane < ROPE, _rope_t(acc, c_ref[0], s_ref[0]), ddt_ref[0]).astype(bf16)

    return _row_call(body, "kprep_bwd", B, S, ts,
                     [_tok(ts, W), _tok(ts, 128), _tok(ts, 128), _tok(ts, 128), pl.BlockSpec(memory_space=pl.ANY)],
                     _tok(ts, 128, OFF_KRDT // 128), jax.ShapeDtypeStruct(dp.shape, bf16), aliases={4: 0})(dkr3, ddt3, cosf, sinf, dp)


def _shift_down(u, j):
    if j == 0:
        return u
    row = lax.broadcasted_iota(jnp.int32, u.shape, 0)
    return jnp.where(row < j, 0.0, pltpu.roll(u, j, 0))


def _shift_up(u, j):
    if j == 0:
        return u
    n = u.shape[0]
    row = lax.broadcasted_iota(jnp.int32, u.shape, 0)
    return jnp.where(row >= n - j, 0.0, pltpu.roll(u, n - j, 0))


def _conv(u, w, b, K):
    out = b
    for j in range(K):
        out = out + w[K - 1 - j:K - j, :] * _shift_down(u, j)
    return out


def _conv_bwd(u, du, w, K):
    dins = w[K - 1:K, :] * du
    dws = [None] * K
    dws[K - 1] = _rowsum(du * u)
    for j in range(1, K):
        sd = _shift_up(du, j)
        dins = dins + w[K - 1 - j:K - j, :] * sd
        dws[K - 1 - j] = _rowsum(sd * u)
    return dins, dws


CW = 256


def conv_ssd_fwd(p3, w, b):
    B, S, _ = p3.shape
    nb = D_XBC // CW

    def body(u_ref, w_ref, b_ref, o_ref, pre_ref):
        pre = _conv(u_ref[0], w_ref[...], b_ref[...], CONV_K)
        o_ref[0] = _silu(pre)
        pre_ref[0] = pre.astype(bf16)

    out = pl.BlockSpec((1, S, CW), lambda b, j: (b, 0, j))
    return pl.pallas_call(body, name="conv_ssd_fwd", grid=(B, nb),
                          in_specs=[pl.BlockSpec((1, S, CW), lambda b, j: (b, 0, OFF_XBC // CW + j)),
                                    pl.BlockSpec((CONV_K, CW), lambda b, j: (0, j)),
                                    pl.BlockSpec((1, CW), lambda b, j: (0, j))],
                          out_specs=[out, out],
                          out_shape=[jax.ShapeDtypeStruct((B, S, D_XBC), f32), jax.ShapeDtypeStruct((B, S, D_XBC), bf16)],
                          compiler_params=_cparams(2))(p3, w, b)


def conv_ssd_bwd(p3, pre3, dxc3, w, dp):
    B, S, _ = p3.shape
    nb = D_XBC // CW

    def body(u_ref, pre_ref, d_ref, w_ref, dp_in, du_ref, dw_ref, db_ref):
        @pl.when(pl.program_id(1) == 0)
        def _():
            dw_ref[...] = jnp.zeros_like(dw_ref)
            db_ref[...] = jnp.zeros_like(db_ref)

        u, wv = u_ref[0], w_ref[...]
        dpre = d_ref[0] * _dsilu(pre_ref[0].astype(f32))
        dins, dws = _conv_bwd(u, dpre, wv, CONV_K)
        du_ref[0] = dins.astype(bf16)
        for k in range(CONV_K):
            dw_ref[k:k + 1, :] += dws[k]
        db_ref[...] += _rowsum(dpre)

    return pl.pallas_call(body, name="conv_ssd_bwd", grid=(nb, B),
                          in_specs=[pl.BlockSpec((1, S, CW), lambda j, b: (b, 0, OFF_XBC // CW + j)),
                                    pl.BlockSpec((1, S, CW), lambda j, b: (b, 0, j)),
                                    pl.BlockSpec((1, S, CW), lambda j, b: (b, 0, j)),
                                    pl.BlockSpec((CONV_K, CW), lambda j, b: (0, j)), pl.BlockSpec(memory_space=pl.ANY)],
                          out_specs=[pl.BlockSpec((1, S, CW), lambda j, b: (b, 0, OFF_XBC // CW + j)),
                                     pl.BlockSpec((CONV_K, CW), lambda j, b: (0, j)),
                                     pl.BlockSpec((1, CW), lambda j, b: (0, j))],
                          out_shape=[jax.ShapeDtypeStruct(dp.shape, bf16), jax.ShapeDtypeStruct((CONV_K, D_XBC), f32),
                                     jax.ShapeDtypeStruct((1, D_XBC), f32)], input_output_aliases={4: 0},
                          compiler_params=_cparams(2))(p3, pre3, dxc3, w, dp)


def glu_fwd(u3, w, b):
    B, S, _ = u3.shape
    nb = D_FF // CW

    def body(ug_ref, uv_ref, wg_ref, wv_ref, bg_ref, bv_ref, o_ref, g_ref, v_ref):
        g = _conv(ug_ref[0].astype(f32), wg_ref[...], bg_ref[...], FF_K)
        v = _conv(uv_ref[0].astype(f32), wv_ref[...], bv_ref[...], FF_K)
        o_ref[0] = (_silu(g) * v).astype(bf16)
        g_ref[0] = g.astype(bf16)
        v_ref[0] = v.astype(bf16)

    def blk(off):
        return pl.BlockSpec((1, S, CW), lambda b, j: (b, 0, off + j))

    def par(rows, off):
        return pl.BlockSpec((rows, CW), lambda b, j: (0, off + j))

    shp = jax.ShapeDtypeStruct((B, S, D_FF), bf16)
    return pl.pallas_call(body, name="glu_fwd", grid=(B, nb),
                          in_specs=[blk(0), blk(nb), par(FF_K, 0), par(FF_K, nb), par(1, 0), par(1, nb)],
                          out_specs=[blk(0)] * 3, out_shape=[shp] * 3, compiler_params=_cparams(2))(u3, u3, w, w, b, b)


def glu_bwd(u3, g3, v3, da3, w):
    B, S, _ = u3.shape
    nb = D_FF // CW

    def body(ug_ref, uv_ref, g_ref, v_ref, da_ref, wg_ref, wv_ref, dug_ref, duv_ref, dwg_ref, dwv_ref, dbg_ref, dbv_ref):
        @pl.when(pl.program_id(1) == 0)
        def _():
            for r in (dwg_ref, dwv_ref, dbg_ref, dbv_ref):
                r[...] = jnp.zeros_like(r)

        ug, uv, da, wg, wv = ug_ref[0].astype(f32), uv_ref[0].astype(f32), da_ref[0].astype(f32), wg_ref[...], wv_ref[...]
        g, v = g_ref[0].astype(f32), v_ref[0].astype(f32)
        dg = da * v * _dsilu(g)
        dv = da * _silu(g)
        ding, dwsg = _conv_bwd(ug, dg, wg, FF_K)
        dinv, dwsv = _conv_bwd(uv, dv, wv, FF_K)
        dug_ref[0] = ding.astype(bf16)
        duv_ref[0] = dinv.astype(bf16)
        for k in range(FF_K):
            dwg_ref[k:k + 1, :] += dwsg[k]
            dwv_ref[k:k + 1, :] += dwsv[k]
        dbg_ref[...] += _rowsum(dg)
        dbv_ref[...] += _rowsum(dv)

    def blk(off):
        return pl.BlockSpec((1, S, CW), lambda j, b: (b, 0, off + j))

    def par(rows, off):
        return pl.BlockSpec((rows, CW), lambda j, b: (0, off + j))

    return pl.pallas_call(body, name="glu_bwd", grid=(nb, B),
                          in_specs=[blk(0), blk(nb), blk(0), blk(0), blk(0), par(FF_K, 0), par(FF_K, nb)],
                          out_specs=[blk(0), blk(0), par(FF_K, 0), par(FF_K, 0), par(1, 0), par(1, 0)],
                          out_shape=[jax.ShapeDtypeStruct((B, S, D_FF), bf16), jax.ShapeDtypeStruct((B, S, D_FF), bf16),
                                     jax.ShapeDtypeStruct((FF_K, D_FF), f32), jax.ShapeDtypeStruct((FF_K, D_FF), f32),
                                     jax.ShapeDtypeStruct((1, D_FF), f32), jax.ShapeDtypeStruct((1, D_FF), f32)],
                          compiler_params=_cparams(2))(u3, u3, g3, v3, da3, w, w)


def _ssd_decay(dtb, bias_row, alog_row):
    lane = lax.broadcasted_iota(jnp.int32, (1, 128), 1)
    hmask = (lane >= DT_LANE) & (lane < DT_LANE + SSD_HEADS)
    dt = jnp.where(hmask, jax.nn.softplus(dtb + bias_row), 0.0)
    a = dt * jnp.where(hmask, -jnp.exp(alog_row), 0.0)
    r = lax.broadcasted_iota(jnp.int32, (CHUNK, CHUNK), 0)
    c = lax.broadcasted_iota(jnp.int32, (CHUNK, CHUNK), 1)
    cs = jnp.dot((r >= c).astype(f32), a, precision=HIGHEST, preferred_element_type=f32)
    return dt, cs


def _expand(xt):
    return jnp.concatenate([jnp.broadcast_to(xt[DT_LANE + h:DT_LANE + h + 1, :], (SSD_HD, xt.shape[1]))
                            for h in range(SSD_HEADS)], axis=0)


_NT = (((1,), (1,)), ((), ()))
_TN = (((0,), (0,)), ((), ()))
GH = SSD_HEADS // 2
GR = GH * SSD_HD


def ssd_fwd(xc3, p3, bias_row, alog_row, dcol):
    B, S, _ = xc3.shape
    nc = S // CHUNK

    def body(xs_ref, bc_ref, dtb_ref, bias_ref, alog_ref, dcol_ref, y_ref, st_ref, state, yT):
        @pl.when(pl.program_id(1) == 0)
        def _():
            state[...] = jnp.zeros_like(state)

        dt, cs = _ssd_decay(dtb_ref[0], bias_ref[...], alog_ref[...])
        csT = cs.T
        eT = jnp.exp(csT)
        decX = _expand(jnp.exp(csT[:, CHUNK - 1:CHUNK] - csT))
        eX = _expand(eT)
        elastX = eX[:, CHUNK - 1:CHUNK]
        xsT = xs_ref[0].T
        uT = xsT * _expand(dt.T)
        bc = bc_ref[0]
        st_ref[0, 0] = state[...]
        srow = lax.broadcasted_iota(jnp.int32, (CHUNK, CHUNK), 0)
        lcol = lax.broadcasted_iota(jnp.int32, (CHUNK, CHUNK), 1)
        for g in range(2):
            Bg = bc[:, g * SSD_N:(g + 1) * SSD_N].astype(bf16)
            Cg = bc[:, (2 + g) * SSD_N:(3 + g) * SSD_N].astype(bf16)
            GT = lax.dot_general(Bg, Cg, _NT, preferred_element_type=f32)
            rows = slice(g * GR, (g + 1) * GR)
            Sg = state[rows]
            yoffT = lax.dot_general(Sg.astype(bf16), Cg, _NT, preferred_element_type=f32) * eX[rows]
            state[rows] = Sg * elastX[rows] + jnp.dot((uT[rows] * decX[rows]).astype(bf16), Bg, preferred_element_type=f32)
            for k in range(GH):
                h = g * GH + k
                hr = slice(h * SSD_HD, (h + 1) * SSD_HD)
                seg = csT[DT_LANE + h:DT_LANE + h + 1, :] - cs[:, DT_LANE + h:DT_LANE + h + 1]
                LT = jnp.where(lcol >= srow, jnp.exp(jnp.minimum(seg, 0.0)), 0.0)
                yT[hr] = (jnp.dot(uT[hr].astype(bf16), (GT * LT).astype(bf16), preferred_element_type=f32)
                          + yoffT[k * SSD_HD:(k + 1) * SSD_HD] + dcol_ref[hr] * xsT[hr])
        y_ref[0] = yT[...].T

    return pl.pallas_call(body, name="ssd_fwd", grid=(B, nc),
                          in_specs=[pl.BlockSpec((1, CHUNK, D_SSD), lambda b, c: (b, c, 0)),
                                    pl.BlockSpec((1, CHUNK, 512), lambda b, c: (b, c, 2)),
                                    pl.BlockSpec((1, CHUNK, 128), lambda b, c: (b, c, OFF_KRDT // 128)),
                                    _const(1, 128), _const(1, 128), _const(D_SSD, 1)],
                          out_specs=[pl.BlockSpec((1, CHUNK, D_SSD), lambda b, c: (b, c, 0)),
                                     pl.BlockSpec((1, 1, D_SSD, SSD_N), lambda b, c: (b, c, 0, 0))],
                          out_shape=[jax.ShapeDtypeStruct((B, S, D_SSD), f32), jax.ShapeDtypeStruct((B, nc, D_SSD, SSD_N), f32)],
                          scratch_shapes=[pltpu.VMEM((D_SSD, SSD_N), f32), pltpu.VMEM((D_SSD, CHUNK), f32)],
                          compiler_params=_cparams(2))(xc3, xc3, p3, bias_row, alog_row, dcol)


def ssd_bwd(xc3, p3, dy3, states, bias_row, alog_row, bias_col, alog_col, dcol, comm=None):
    B, S, _ = xc3.shape
    nc = S // CHUNK

    def body(xs_ref, bc_ref, dtb_ref, dy_ref, st_ref, bias_ref, alog_ref, biasc_ref, alogc_ref, dcol_ref,
             dxc_ref, ddt_ref, dalog_ref, dd_ref, dbias_ref, dS, dUT, accA, accD, accB, dcs_diag):
        @pl.when(pl.program_id(1) == 0)
        def _():
            dS[...] = jnp.zeros_like(dS)

        @pl.when(_first())
        def _():
            accA[...] = jnp.zeros_like(accA)
            accD[...] = jnp.zeros_like(accD)
            accB[...] = jnp.zeros_like(accB)

        dtb = dtb_ref[0]
        dt, cs = _ssd_decay(dtb, bias_ref[...], alog_ref[...])
        dtT, csT = dt.T, cs.T
        decX = _expand(jnp.exp(csT[:, CHUNK - 1:CHUNK] - csT))
        eX = _expand(jnp.exp(csT))
        dtX = _expand(dtT)
        elastX = eX[:, CHUNK - 1:CHUNK]
        xsT = xs_ref[0].T
        uT = xsT * dtX
        dYT = dy_ref[0].T
        bc = bc_ref[0]
        lrow = lax.broadcasted_iota(jnp.int32, (CHUNK, CHUNK), 0)
        scol = lax.broadcasted_iota(jnp.int32, (CHUNK, CHUNK), 1)
        dcs_diag[...] = jnp.zeros_like(dcs_diag)
        rs_cols = jnp.zeros((CHUNK, 128), f32)
        vparts, zparts = [], []
        for g in range(2):
            Bf = bc[:, g * SSD_N:(g + 1) * SSD_N]
            Bg = Bf.astype(bf16)
            Cg = bc[:, (2 + g) * SSD_N:(3 + g) * SSD_N].astype(bf16)
            G = lax.dot_general(Cg, Bg, _NT, preferred_element_type=f32)
            BgT = Bf.T.astype(bf16)
            rows = slice(g * GR, (g + 1) * GR)
            dSg = dS[rows]
            Sg = st_ref[0, 0, rows, :]
            dUst = jnp.dot(dSg.astype(bf16), BgT, preferred_element_type=f32) * decX[rows]
            yoffT = lax.dot_general(Sg.astype(bf16), Cg, _NT, preferred_element_type=f32) * eX[rows]
            zparts.append(dYT[rows] * yoffT - dUst * uT[rows])
            dG = jnp.zeros((CHUNK, CHUNK), f32)
            for k in range(GH):
                h = g * GH + k
                hr = slice(h * SSD_HD, (h + 1) * SSD_HD)
                seg = cs[:, DT_LANE + h:DT_LANE + h + 1] - csT[DT_LANE + h:DT_LANE + h + 1, :]
                L = jnp.where(lrow >= scol, jnp.exp(jnp.minimum(seg, 0.0)), 0.0)
                M = G * L
                dYh = dYT[hr].astype(bf16)
                dUT[hr] = jnp.dot(dYh, M.astype(bf16), preferred_element_type=f32) + dUst[k * SSD_HD:(k + 1) * SSD_HD]
                dM = lax.dot_general(dYh, uT[hr].astype(bf16), _TN, preferred_element_type=f32)
                dG = dG + dM * L
                Wm = dM * M
                rs_cols = jnp.where(scol == DT_LANE + h, jnp.sum(Wm, axis=1, keepdims=True), rs_cols)
                dcs_diag[DT_LANE + h:DT_LANE + h + 1, :] = -_rowsum(Wm)
            dGb = dG.astype(bf16)
            dYe = (dYT[rows] * eX[rows]).astype(bf16)
            ude = (uT[rows] * decX[rows]).astype(bf16)
            dC = jnp.dot(dGb, Bg, preferred_element_type=f32) + lax.dot_general(dYe, Sg.astype(bf16), _TN, preferred_element_type=f32)
            dB = (lax.dot_general(dGb, Cg, _TN, preferred_element_type=f32)
                  + lax.dot_general(ude, dSg.astype(bf16), _TN, preferred_element_type=f32))
            dxc_ref[0, :, D_SSD + g * SSD_N:D_SSD + (g + 1) * SSD_N] = dB
            dxc_ref[0, :, D_SSD + (2 + g) * SSD_N:D_SSD + (3 + g) * SSD_N] = dC
            vparts.append(elastX[rows] * jnp.sum(dSg * Sg, axis=1, keepdims=True)
                          + jnp.sum(dUst * uT[rows], axis=1, keepdims=True))
            dS[rows] = elastX[rows] * dSg + jnp.dot(dYe, Cg, preferred_element_type=f32)
        dU = dUT[...]
        dcv = dcol_ref[...]
        dxc_ref[0, :, 0:D_SSD] = (dtX * dU + dcv * dYT).T
        lane = lax.broadcasted_iota(jnp.int32, (D_SSD, CHUNK), 1)
        Z = jnp.concatenate(zparts, axis=0) + jnp.where(lane == CHUNK - 1, jnp.concatenate(vparts, axis=0), 0.0)
        hr_ = lax.broadcasted_iota(jnp.int32, (128, D_SSD), 0)
        hc_ = lax.broadcasted_iota(jnp.int32, (128, D_SSD), 1)
        hsel = (hr_ - DT_LANE == jnp.right_shift(hc_, 6)).astype(bf16)
        summands = jnp.concatenate([Z, dU * xsT, dYT * xsT], axis=1)
        hi = summands.astype(bf16)
        lo = (summands - hi.astype(f32)).astype(bf16)
        red = jnp.dot(hsel, hi, preferred_element_type=f32) + jnp.dot(hsel, lo, preferred_element_type=f32)
        dcsT = red[:, 0:CHUNK] + dcs_diag[...] + rs_cols.T
        daT = jnp.dot(dcsT, (lrow >= scol).astype(f32), precision=HIGHEST, preferred_element_type=f32)
        rowi = lax.broadcasted_iota(jnp.int32, (128, 1), 0)
        hmask = (rowi >= DT_LANE) & (rowi < DT_LANE + SSD_HEADS)
        a_col = jnp.where(hmask, -jnp.exp(alogc_ref[...]), 0.0)
        ddtT = red[:, CHUNK:2 * CHUNK] + a_col * daT
        ddt_rawT = jnp.where(hmask, ddtT * jax.nn.sigmoid(dtb.T + biasc_ref[...]), 0.0)
        ddt_ref[0] = ddt_rawT.T
        accA[...] += daT * dtT
        accD[...] += red[:, 2 * CHUNK:3 * CHUNK]
        accB[...] += ddt_rawT

        @pl.when((pl.program_id(0) == B - 1) & (pl.program_id(1) == nc - 1))
        def _():
            dalog_ref[...] = jnp.broadcast_to(jnp.sum(accA[...], axis=1, keepdims=True) * a_col, (128, 128))
            dd_ref[...] = jnp.broadcast_to(jnp.sum(accD[...], axis=1, keepdims=True), (128, 128))
            dbias_ref[...] = jnp.broadcast_to(jnp.sum(accB[...], axis=1, keepdims=True), (128, 128))

    def rev(width, cb):
        return pl.BlockSpec((1, CHUNK, width), lambda b, c: (b, nc - 1 - c, cb))

    acc_spec = pl.BlockSpec((128, 128), lambda b, c: (0, 0))
    acc_shape = jax.ShapeDtypeStruct((128, 128), f32)
    body, c_args, c_in, c_out, c_shapes, c_sems = _fuse_exchange(body, comm, 10, 5, 6, (B, nc))
    res = pl.pallas_call(body, name="ssd_bwd_x" if comm else "ssd_bwd", grid=(B, nc),
                         in_specs=[rev(D_SSD, 0), rev(512, 2), rev(128, OFF_KRDT // 128), rev(D_SSD, 0),
                                   pl.BlockSpec((1, 1, D_SSD, SSD_N), lambda b, c: (b, nc - 1 - c, 0, 0)),
                                   _const(1, 128), _const(1, 128), _const(128, 1), _const(128, 1), _const(D_SSD, 1)] + c_in,
                         out_specs=[rev(D_XBC, 0), rev(128, 0), acc_spec, acc_spec, acc_spec] + c_out,
                         out_shape=[jax.ShapeDtypeStruct((B, S, D_XBC), f32), jax.ShapeDtypeStruct((B, S, 128), f32),
                                    acc_shape, acc_shape, acc_shape] + c_shapes,
                         scratch_shapes=[pltpu.VMEM((D_SSD, SSD_N), f32), pltpu.VMEM((D_SSD, CHUNK), f32),
                                         pltpu.VMEM((128, 128), f32), pltpu.VMEM((128, 128), f32), pltpu.VMEM((128, 128), f32),
                                         pltpu.VMEM((128, 128), f32)] + c_sems,
                         compiler_params=_cparams(2))(xc3, xc3, p3, dy3, states, bias_row, alog_row, bias_col, alog_col, dcol, *c_args)
    return (*res[:5], list(res[5:]))


ATT_SCALE = float(QK) ** -0.5
LOG2E = 1.4426950408889634
LN2 = 0.6931471805599453
Q_FOLD = ATT_SCALE * LOG2E
NEG = -1e30
HP = 4


def _att_block(S):
    return _pick(S, (512, 256, 128))


def _tri_rows(t, n):
    i = sum([(t >= r * (r + 1) // 2).astype(jnp.int32) for r in range(1, n)], jnp.int32(0))
    return i, t - i * (i + 1) // 2


def _tri_cols(t, n):
    j = sum([(t >= r * n - r * (r - 1) // 2).astype(jnp.int32) for r in range(1, n)], jnp.int32(0))
    return j, j + t - (j * n - j * (j - 1) // 2)


def attn_fwd(q3, k3, v3, comm=None):
    B, S, _ = q3.shape
    bq = _att_block(S)
    nq = S // bq

    def body(q_ref, k_ref, v_ref, o_ref, lse_ref, m_s, l_s, acc):
        i, j = _tri_rows(pl.program_id(2), nq)

        @pl.when(j == 0)
        def _():
            m_s[...] = jnp.full_like(m_s, NEG)
            l_s[...] = jnp.zeros_like(l_s)
            acc[...] = jnp.zeros_like(acc)

        def step(masked):
            for t in range(HP):
                qk = slice(t * 256, (t + 1) * 256)
                st = lax.dot_general(k_ref[0, :, qk], q_ref[0, :, qk], _NT, preferred_element_type=f32)
                if masked:
                    r = lax.broadcasted_iota(jnp.int32, (bq, bq), 0)
                    c = lax.broadcasted_iota(jnp.int32, (bq, bq), 1)
                    st = jnp.where(c >= r, st, NEG)
                m_old = m_s[t]
                m_new = jnp.maximum(m_old, jnp.max(st, axis=0, keepdims=True))
                alpha = jnp.exp2(m_old - m_new)
                pt = jnp.exp2(st - m_new)
                l_s[t] = alpha * l_s[t] + jnp.sum(pt, axis=0, keepdims=True)
                acc[t] = alpha * acc[t] + lax.dot_general(v_ref[0, :, t * VD:(t + 1) * VD], pt.astype(bf16), _TN,
                                                          preferred_element_type=f32)
                m_s[t] = m_new

        @pl.when(j < i)
        def _():
            step(False)

        @pl.when(j == i)
        def _():
            step(True)
            for t in range(HP):
                o_ref[0, :, t * VD:(t + 1) * VD] = (acc[t] / l_s[t]).T
                lse_ref[0, t] = m_s[t] + jnp.log2(l_s[t])

    grid = (B, MLA_H // HP, nq * (nq + 1) // 2)
    qi = lambda t: _tri_rows(t, nq)[0]
    kj = lambda t: _tri_rows(t, nq)[1]
    body, c_args, c_in, c_out, c_shapes, c_sems = _fuse_exchange(body, comm, 3, 2, 3, grid)
    res = pl.pallas_call(body, name="attn_fwd_x" if comm else "attn_fwd", grid=grid,
                         in_specs=[pl.BlockSpec((1, bq, HP * 256), lambda b, h, t: (b, qi(t), h)),
                                   pl.BlockSpec((1, bq, HP * 256), lambda b, h, t: (b, kj(t), h)),
                                   pl.BlockSpec((1, bq, HP * VD), lambda b, h, t: (b, kj(t), h))] + c_in,
                         out_specs=[pl.BlockSpec((1, bq, HP * VD), lambda b, h, t: (b, qi(t), h)),
                                    pl.BlockSpec((1, HP, 1, bq), lambda b, h, t: (b, h, 0, qi(t)))] + c_out,
                         out_shape=[jax.ShapeDtypeStruct((B, S, MLA_H * VD), f32), jax.ShapeDtypeStruct((B, MLA_H, 1, S), f32)] + c_shapes,
                         scratch_shapes=[pltpu.VMEM((HP, 1, bq), f32), pltpu.VMEM((HP, 1, bq), f32), pltpu.VMEM((HP, VD, bq), f32)] + c_sems,
                         compiler_params=_cparams(3))(q3, k3, v3, *c_args)
    return res[0], res[1], list(res[2:])


def rms_o_bwd(o3, dcat3, g):
    B, S, C = o3.shape
    ts = _pick(S, (512, 256, 128))

    def body(x_ref, do_ref, g_ref, dx_ref, dg_ref, d_ref):
        @pl.when(_first())
        def _():
            dg_ref[...] = jnp.zeros_like(dg_ref)

        x, do = x_ref[0], do_ref[0].astype(f32)
        r = lax.rsqrt(jnp.mean(x * x, axis=-1, keepdims=True) + EPS)
        xh = x * r
        dg_ref[...] += _rowsum(do * xh)
        dxh = do * g_ref[...]
        dx = r * (dxh - xh * jnp.mean(dxh * xh, axis=-1, keepdims=True))
        dx_ref[0] = dx
        for h in range(MLA_H):
            vs = slice(h * VD, (h + 1) * VD)
            d_ref[0, h] = jnp.sum(dx[:, vs] * x[:, vs], axis=-1, keepdims=True)

    return _row_call(body, "rms_o_bwd", B, S, ts, [_tok(ts, C), _tok(ts, C, 1), _const(1, C)],
                     [_tok(ts, C), _const(1, C), pl.BlockSpec((1, MLA_H, ts, 1), lambda b, s: (b, 0, s, 0))],
                     [jax.ShapeDtypeStruct((B, S, C), f32), jax.ShapeDtypeStruct((1, C), f32),
                      jax.ShapeDtypeStruct((B, MLA_H, S, 1), f32)])(o3, dcat3, g)


def attn_bwd(q3, k3, v3, do3, lse_row, delta_row, cosf, sinf, comm=None):
    B, S, _ = q3.shape
    bq = _att_block(S)
    nq = S // bq

    def body(q_ref, k_ref, v_ref, do_ref, lse_ref, dl_ref, cos_ref, sin_ref, dkn_ref, dv_ref, dkr_ref, dq_hbm, dk_acc, dv_acc,
             dq_scr, stage, dq_sem):
        b, hp = pl.program_id(0), pl.program_id(1)
        j, i = _tri_cols(pl.program_id(2), nq)
        rows = pl.ds(pl.multiple_of(i * bq, bq), bq)

        @pl.when(pl.program_id(2) == 0)
        def _():
            dq_scr[...] = jnp.zeros_like(dq_scr)

        @pl.when(i == j)
        def _():
            dk_acc[...] = jnp.zeros_like(dk_acc)
            dv_acc[...] = jnp.zeros_like(dv_acc)

        def step(masked):
            for t in range(HP):
                qk, vs = slice(t * 256, (t + 1) * 256), slice(t * VD, (t + 1) * VD)
                q, k = q_ref[0, :, qk], k_ref[0, :, qk]
                do = do_ref[0, :, vs].astype(bf16)
                pt = jnp.exp2(lax.dot_general(k, q, _NT, preferred_element_type=f32) - lse_ref[0, t])
                if masked:
                    r = lax.broadcasted_iota(jnp.int32, (bq, bq), 0)
                    c = lax.broadcasted_iota(jnp.int32, (bq, bq), 1)
                    pt = jnp.where(c >= r, pt, 0.0)
                dv_acc[t] += jnp.dot(pt.astype(bf16), do, preferred_element_type=f32)
                dpt = lax.dot_general(v_ref[0, :, vs], do, _NT, preferred_element_type=f32)
                dst = (pt * (dpt - dl_ref[0, t])).astype(bf16)
                dk_acc[t] += jnp.dot(dst, q, preferred_element_type=f32)
                dq_scr[t, rows, :] += lax.dot_general(dst, k, _TN, preferred_element_type=f32)

        @pl.when(i > j)
        def _():
            step(False)

        @pl.when(i == j)
        def _():
            step(True)
            for t in range(HP):
                d = dq_scr[t, rows, :] * (LN2 * Q_FOLD)
                stage[t, :, 0:NOPE] = d[:, 0:NOPE].astype(bf16)
                stage[t, :, NOPE:] = _rope_t(d[:, NOPE:], cos_ref[0], sin_ref[0]).astype(bf16)
                cp = pltpu.make_async_copy(stage.at[t], dq_hbm.at[b, rows, pl.ds(pl.multiple_of((hp * HP + t) * 256, 256), 256)],
                                           dq_sem.at[t])
                cp.start()
                cp.wait()

        @pl.when(i == nq - 1)
        def _():
            kr = jnp.zeros((bq, 128), f32)
            for t in range(HP):
                dkn_ref[0, :, t * NOPE:(t + 1) * NOPE] = (dk_acc[t, :, 0:NOPE] * LN2).astype(bf16)
                dv_ref[0, :, t * VD:(t + 1) * VD] = dv_acc[t].astype(bf16)
                kr = kr + dk_acc[t, :, NOPE:]
            dkr_ref[0] = kr * LN2

    kj = lambda t: _tri_cols(t, nq)[0]
    qi = lambda t: _tri_cols(t, nq)[1]
    kspec = pl.BlockSpec((1, bq, HP * 256), lambda b, h, t: (b, kj(t), h))
    vspec = pl.BlockSpec((1, bq, HP * VD), lambda b, h, t: (b, kj(t), h))
    krspec = pl.BlockSpec((1, bq, 128), lambda b, h, t: (b, kj(t), h))
    rspec = pl.BlockSpec((1, HP, 1, bq), lambda b, h, t: (b, h, 0, qi(t)))
    tspec = pl.BlockSpec((1, bq, 128), lambda b, h, t: (b, qi(t), 0))
    grid = (B, MLA_H // HP, nq * (nq + 1) // 2)
    body, c_args, c_in, c_out, c_shapes, c_sems = _fuse_exchange(body, comm, 8, 4, 5, grid)
    res = pl.pallas_call(body, name="attn_bwd_x" if comm else "attn_bwd", grid=grid,
                         in_specs=[pl.BlockSpec((1, bq, HP * 256), lambda b, h, t: (b, qi(t), h)), kspec, vspec,
                                   pl.BlockSpec((1, bq, HP * VD), lambda b, h, t: (b, qi(t), h)), rspec, rspec,
                                   tspec, tspec] + c_in,
                         out_specs=[vspec, vspec, krspec, pl.BlockSpec(memory_space=pltpu.HBM)] + c_out,
                         out_shape=[jax.ShapeDtypeStruct((B, S, MLA_H * NOPE), bf16), jax.ShapeDtypeStruct((B, S, MLA_H * VD), bf16),
                                    jax.ShapeDtypeStruct((B, S, MLA_H // HP * 128), f32),
                                    jax.ShapeDtypeStruct((B, S, MLA_H * 256), bf16)] + c_shapes,
                         scratch_shapes=[pltpu.VMEM((HP, bq, 256), f32), pltpu.VMEM((HP, bq, VD), f32), pltpu.VMEM((HP, S, 256), f32),
                                         pltpu.VMEM((HP, bq, 256), bf16), pltpu.SemaphoreType.DMA((HP,))] + c_sems,
                         compiler_params=_cparams(3))(q3, k3, v3, do3, lse_row, delta_row, cosf, sinf, *c_args)
    return res[3], res[0], res[1], res[2], list(res[4:])


def ada_fwd(c_all, w, b):
    n = w.shape[1]

    def body(c_ref, w_ref, b_ref, o_ref):
        o_ref[...] = jnp.dot(_silu(c_ref[...]).astype(bf16), w_ref[...].astype(bf16), preferred_element_type=f32) + b_ref[...]

    return pl.pallas_call(body, name="ada_fwd", out_shape=jax.ShapeDtypeStruct((c_all.shape[0], n), f32),
                          compiler_params=pltpu.CompilerParams(vmem_limit_bytes=VMEM_LIMIT))(c_all, w, b)


def ada_bwd(c_all, dmod):
    n = dmod.shape[1]

    def body(c_ref, d_ref, o_ref):
        o_ref[...] = lax.dot_general(_silu(c_ref[...]).astype(bf16), d_ref[...].astype(bf16), _TN, preferred_element_type=f32)

    return pl.pallas_call(body, name="ada_bwd", out_shape=jax.ShapeDtypeStruct((c_all.shape[1], n), f32),
                          compiler_params=pltpu.CompilerParams(vmem_limit_bytes=VMEM_LIMIT))(c_all, dmod)


def sum_leading(x, name):
    n, R, _ = x.shape
    tr = _pick(R, (512, 256, 128, 64, 32, 16, 8))

    def body(x_ref, o_ref):
        acc = x_ref[0].astype(f32)
        for k in range(1, n):
            acc = acc + x_ref[k].astype(f32)
        o_ref[...] = acc

    return pl.pallas_call(body, name=name, grid=(R // tr,), in_specs=[pl.BlockSpec((n, tr, 128), lambda i: (0, i, 0))],
                          out_specs=pl.BlockSpec((tr, 128), lambda i: (i, 0)), out_shape=jax.ShapeDtypeStruct((R, 128), f32),
                          compiler_params=_cparams(1))(x)


def _adamw_body(w_ref, g_ref, m_ref, v_ref, d_ref, mo_ref, vo_ref):
    gv = g_ref[...]
    mn = ADAM_B1 * m_ref[...] + (1.0 - ADAM_B1) * gv
    vn = ADAM_B2 * v_ref[...] + (1.0 - ADAM_B2) * jnp.square(gv)
    m_hat = mn / (1.0 - ADAM_B1 ** ADAM_STEP)
    v_hat = vn / (1.0 - ADAM_B2 ** ADAM_STEP)
    d_ref[...] = -ADAM_LR * (m_hat / (jnp.sqrt(v_hat) + ADAM_EPS) + ADAM_WD * w_ref[...])
    mo_ref[...] = mn
    vo_ref[...] = vn


def adamw(w, g, m, v):
    R = w.shape[0]
    tr = _pick(R, (512, 256, 128, 64, 32, 16, 8))
    spec = pl.BlockSpec((tr, 128), lambda i: (i, 0))
    shp = jax.ShapeDtypeStruct((R, 128), f32)
    return pl.pallas_call(functools.partial(_adamw_body), name="adamw", grid=(R // tr,), in_specs=[spec] * 4,
                          out_specs=[spec] * 3, out_shape=[shp] * 3, compiler_params=_cparams(1))(w, g, m, v)


def _row_tile(a):
    return _pick(a, (256, 128, 64, 32, 16, 8)) if a % 8 == 0 else a


def adamw_nd(w, g, m, v):
    L, a, b = w.shape
    ta = _row_tile(a)
    spec = pl.BlockSpec((1, ta, b), lambda l, i: (l, i, 0))
    shp = jax.ShapeDtypeStruct((L, a, b), f32)
    return pl.pallas_call(functools.partial(_adamw_body), name="adamw_nd", grid=(L, a // ta), in_specs=[spec] * 4,
                          out_specs=[spec] * 3, out_shape=[shp] * 3, compiler_params=_cparams(2))(w, g, m, v)


def sum_slots(x):
    n, L, a, b = x.shape
    ta = _row_tile(a)

    def body(x_ref, o_ref):
        acc = x_ref[0].astype(f32)
        for k in range(1, n):
            acc = acc + x_ref[k].astype(f32)
        o_ref[...] = acc

    return pl.pallas_call(body, name="sum_slots", grid=(L, a // ta),
                          in_specs=[pl.BlockSpec((n, 1, ta, b), lambda l, i: (0, l, i, 0))],
                          out_specs=pl.BlockSpec((1, ta, b), lambda l, i: (l, i, 0)),
                          out_shape=jax.ShapeDtypeStruct((L, a, b), f32), compiler_params=_cparams(2))(x)


def _exchange_copies(ins, outs, sems, scatter):
    send_sems, recv_sems, local_sems = sems
    x, y, c = lax.axis_index("x"), lax.axis_index("y"), lax.axis_index("c")
    me = 4 * x + 2 * y + c
    locals_, sends, recvs = [], [], []
    for a in range(len(ins)):
        locals_.append(pltpu.make_async_copy(ins[a].at[me] if scatter[a] else ins[a], outs[a].at[me], local_sems.at[a]))
        for k in range(N_DEV - 1):
            px = 1 - x if (k + 1) & 4 else x
            py = 1 - y if (k + 1) & 2 else y
            pc = 1 - c if (k + 1) & 1 else c
            pid = 4 * px + 2 * py + pc
            src = ins[a].at[pid] if scatter[a] else ins[a]
            for slot, group in ((me, sends), (pid, recvs)):
                group.append(pltpu.make_async_remote_copy(src_ref=src, dst_ref=outs[a].at[slot], send_sem=send_sems.at[a, k],
                                                          recv_sem=recv_sems.at[a, k], device_id=(px, py, pc),
                                                          device_id_type=pl.DeviceIdType.MESH))
    return locals_, sends, recvs


def _exchange_start(ins, outs, sems, scatter):
    locals_, sends, _ = _exchange_copies(ins, outs, sems, scatter)
    for cp in locals_ + sends:
        cp.start()


def _exchange_wait(ins, outs, sems, scatter):
    locals_, sends, recvs = _exchange_copies(ins, outs, sems, scatter)
    for cp in recvs:
        cp.wait_recv()
    for cp in sends:
        cp.wait_send()
    for cp in locals_:
        cp.wait()


def _exchange_shapes(arrays, scatter):
    return [jax.ShapeDtypeStruct((N_DEV,) + tuple(a.shape[1:] if s else a.shape), a.dtype) for a, s in zip(arrays, scatter)]


def _flags(scatter, n):
    return [scatter] * n if isinstance(scatter, bool) else list(scatter)


def _exchange_sems(n):
    return [pltpu.SemaphoreType.DMA((n, N_DEV - 1)), pltpu.SemaphoreType.DMA((n, N_DEV - 1)), pltpu.SemaphoreType.DMA((n,))]


def _fuse_exchange(core, comm, n_in, n_out, n_scr, grid):
    if comm is None:
        return core, [], [], [], [], []
    arrays, scatter = comm
    n = len(arrays)
    scatter = _flags(scatter, n)

    def body(*refs):
        a, b, c = n_in + n, n_in + n + n_out, n_in + 2 * n + n_out
        cin, cout, sems = refs[n_in:a], refs[b:c], refs[c + n_scr:]
        ids = [pl.program_id(d) for d in range(len(grid))]
        first = functools.reduce(lambda p, q: p & q, [i == 0 for i in ids])
        last = functools.reduce(lambda p, q: p & q, [i == g - 1 for i, g in zip(ids, grid)])

        @pl.when(first)
        def _():
            _exchange_start(cin, cout, sems, scatter)

        core(*refs[:n_in], *refs[a:b], *refs[c:c + n_scr])

        @pl.when(last)
        def _():
            _exchange_wait(cin, cout, sems, scatter)

    hbm = pl.BlockSpec(memory_space=pltpu.HBM)
    return body, list(arrays), [hbm] * n, [hbm] * n, _exchange_shapes(arrays, scatter), _exchange_sems(n)


def exchange(arrays, scatter, name):
    n = len(arrays)
    scatter = _flags(scatter, n)

    def body(*refs):
        ins, outs, sems = refs[:n], refs[n:2 * n], refs[2 * n:]
        _exchange_start(ins, outs, sems, scatter)
        _exchange_wait(ins, outs, sems, scatter)

    hbm = pl.BlockSpec(memory_space=pltpu.HBM)
    return pl.pallas_call(body, name=name, in_specs=[hbm] * n, out_specs=[hbm] * n,
                          out_shape=_exchange_shapes(arrays, scatter), scratch_shapes=_exchange_sems(n))(*arrays)


BIG = (("w_in", "col"), ("conv_w", "col"), ("w_uq", "col"), ("w_ukv", "col"), ("w_out", "row"), ("w_up", "col"),
       ("conv_ff_w", "col"), ("w_down", "row"))
SMALL = ("b_ada", "norm_mix", "conv_b", "dt_bias", "a_log", "d_skip", "ssd_norm", "q_norm", "kv_norm", "attn_norm",
         "norm_mlp", "conv_ff_b", "final_norm")
CONVS = ("conv_w", "conv_ff_w")
PACK_ALIGN = 2048


def _padded(n):
    return -(-n // PACK_ALIGN) * PACK_ALIGN


def _flat_pad(a):
    f = a.reshape(-1)
    return jnp.pad(f, (0, _padded(f.shape[0]) - f.shape[0]))


PACK_ROWS = 512


def pack(arrs):
    f = jnp.concatenate([_flat_pad(a) for a in arrs])
    n = PACK_ROWS * 128
    return jnp.pad(f, (0, -(-f.shape[0] // n) * n - f.shape[0])).reshape(-1, 128)


def unpack(flat, shapes):
    f = flat.reshape(-1)
    out, off = [], 0
    for s in shapes:
        n = int(np.prod(s))
        out.append(f[off:off + n].reshape(s))
        off += _padded(n)
    return out


def shards_to_full(g, kind):
    _, a, b = g.shape
    if kind == "col":
        return g.transpose(1, 0, 2).reshape(a, N_DEV * b)
    return g.reshape(N_DEV * a, b)


def full_to_shards(full, kind):
    if kind == "col":
        a, nb = full.shape
        return full.reshape(a, N_DEV, nb // N_DEV).transpose(1, 0, 2)
    na, b = full.shape
    return full.reshape(N_DEV, na // N_DEV, b)


def w_in_layout(w):
    z = lambda n: jnp.zeros(w.shape[:-1] + (n,), w.dtype)
    return jnp.concatenate([w[..., :2560], w[..., 2576:2960], z(128), w[..., 2960:3216], w[..., 3216:3280],
                            w[..., 2560:2576], z(48)], axis=-1)


def w_in_unlayout(g):
    return jnp.concatenate([g[..., :2560], g[..., 3392:3408], g[..., 2560:2944], g[..., 3072:3328], g[..., 3328:3392]], axis=-1)


def w_uq_layout(w):
    return jnp.pad(w.reshape(Q_RANK, MLA_H, QK), ((0, 0), (0, 0), (0, 256 - QK))).reshape(Q_RANK, MLA_H * 256)


def w_uq_unlayout(g):
    return g.reshape(Q_RANK, MLA_H, 256)[:, :, :QK].reshape(Q_RANK, MLA_H * QK)


def w_ukv_layout(w):
    return w.reshape(KV_RANK, MLA_H, 2, 128).transpose(0, 2, 1, 3).reshape(KV_RANK, 2 * MLA_H * 128)


def w_ukv_unlayout(g):
    return g.reshape(KV_RANK, 2, MLA_H, 128).transpose(0, 2, 1, 3).reshape(KV_RANK, 2 * MLA_H * 128)


LAYOUTS = {"w_in": (w_in_layout, w_in_unlayout), "w_uq": (w_uq_layout, w_uq_unlayout), "w_ukv": (w_ukv_layout, w_ukv_unlayout)}
FIRST, REST = BIG[:4], BIG[4:]


def layer_weights(gathered, entries):
    full = {n: shards_to_full(g, kind) for (n, kind), g in zip(entries, gathered)}
    return {n: LAYOUTS[n][0](w) if n in LAYOUTS else w for n, w in full.items()}


def layer_grad_slices(g, entries):
    return [full_to_shards(LAYOUTS[n][1](g[n]) if n in LAYOUTS else g[n], kind).astype(bf16) for n, kind in entries]


def _head_row(v):
    return jnp.zeros((1, 128), f32).at[0, DT_LANE:DT_LANE + SSD_HEADS].set(v)


def layer_fwd(x3, mod, W, P, l, cosf, sinf, comm=None, late=None, comm_up=None):
    B, S, _ = x3.shape
    T = B * S
    sv = {}
    h = normmod_fwd(x3, mod, P["norm_mix"][l][None], 0, 1)
    p = mm(h.reshape(T, D), W["w_in"], "nn", "mm_in")
    p3 = p.reshape(B, S, IN_COLS)
    bias_row, alog_row = _head_row(P["dt_bias"][l]), _head_row(P["a_log"][l])
    dcol = jnp.repeat(P["d_skip"][l], SSD_HD)[:, None]
    xc3, xpre = conv_ssd_fwd(p3, W["conv_w"], P["conv_b"][l][None])
    yc3, states = ssd_fwd(xc3, p3, bias_row, alog_row, dcol)
    y_ssd = ssd_out_fwd(yc3, p3, P["ssd_norm"][l][None])
    cqn = rms_fwd(p3, 512, OFF_CQ // 512, Q_RANK, P["q_norm"][l][None], "rms_q_fwd")
    ckvn = rms_fwd(p3, KV_RANK, OFF_CKV // KV_RANK, KV_RANK, P["kv_norm"][l][None], "rms_kv_fwd")
    q3 = mm(cqn.reshape(T, Q_RANK), W["w_uq"], "nn", "mm_uq", out_dtype=bf16,
            rope=(cosf.reshape(T, 128), sinf.reshape(T, 128))).reshape(B, S, -1)
    k3 = k_proj(ckvn, W["w_ukv"], p3, cosf, sinf)
    v3 = mm(ckvn.reshape(T, KV_RANK), W["w_ukv"][:, MLA_H * NOPE:], "nn", "mm_uv", out_dtype=bf16).reshape(B, S, -1)
    o3, lse, comm_out = attn_fwd(q3, k3, v3, comm)
    if late is not None:
        W = dict(W, **late(comm_out))
    y_att = rms_fwd(o3, D, 0, D, P["attn_norm"][l][None], "rms_o_fwd")
    cat = (y_ssd.reshape(T, D), y_att.reshape(T, D))
    x1, y1 = mm(cat, W["w_out"], "nn", "mm_out", resid=x3.reshape(T, D), gate=mod[:, 2:3, :], seq=S)
    x13 = x1.reshape(B, S, D)
    h2 = normmod_fwd(x13, mod, P["norm_mlp"][l][None], 3, 4)
    u, up_out = mm(h2.reshape(T, D), W["w_up"], "nn", "mm_up", out_dtype=bf16, comm=comm_up), []
    if comm_up is not None:
        u, up_out = u
    u3 = u.reshape(B, S, 2 * D_FF)
    a, ffg, ffv = glu_fwd(u3, W["conv_ff_w"], P["conv_ff_b"][l][None])
    x2, y2 = mm(a.reshape(T, D_FF), W["w_down"], "nn", "mm_down", resid=x1, gate=mod[:, 5:6, :], seq=S)
    sv.update(x=x3, h=h, p3=p3, xc3=xc3, xpre=xpre, yc3=yc3, states=states, cqn=cqn, ckvn=ckvn, q3=q3, k3=k3, v3=v3, o3=o3, lse=lse,
              cat=cat, y1=y1, x1=x13, h2=h2, u3=u3, ffg=ffg, ffv=ffv, a=a, y2=y2, bias_row=bias_row, alog_row=alog_row, dcol=dcol)
    return x2.reshape(B, S, D), sv, comm_out, W, up_out


def layer_bwd(dx3, sv, mod, W, P, l, cosf, sinf, comm=None, send_rest=False):
    B, S, _ = dx3.shape
    T = B * S
    g = {}
    dy2, dg2 = gate_bwd(dx3, sv["y2"].reshape(B, S, D), mod, 5)
    dy2 = dy2.reshape(T, D)
    da = mm(dy2, W["w_down"], "nt", "mm_down_dx", out_dtype=bf16)
    g["w_down"] = mm(sv["a"].reshape(T, D_FF), dy2, "tn", "mm_down_dw")
    dug, duv, dwg, dwv, dbg, dbv = glu_bwd(sv["u3"], sv["ffg"], sv["ffv"], da.reshape(B, S, D_FF), W["conv_ff_w"])
    g["conv_ff_w"] = jnp.concatenate([dwg, dwv], axis=1)
    g["conv_ff_b"] = jnp.concatenate([dbg, dbv], axis=1)[0]
    du = (dug.reshape(T, D_FF), duv.reshape(T, D_FF))
    dh2 = mm(du, W["w_up"], "nt", "mm_up_dx", out_dtype=bf16)
    g["w_up"] = jnp.concatenate([mm(sv["h2"].reshape(T, D), d, "tn", "mm_up_dw") for d in du], axis=1)
    dx1, dsh2, dsc2, dnm, dy1, dg1 = normmod_bwd(sv["x1"], dh2.reshape(B, S, D), dx3, mod, P["norm_mlp"][l][None], 4,
                                                 y3=sv["y1"].reshape(B, S, D), i_g=2)
    g["norm_mlp"] = dnm[0]
    dy1 = dy1.reshape(T, D)
    dcat = mm(dy1, W["w_out"], "nt", "mm_out_dx", out_dtype=bf16)
    g["w_out"] = jnp.concatenate([mm(part, dy1, "tn", "mm_out_dw") for part in sv["cat"]], axis=0)
    dcat3 = dcat.reshape(B, S, 2 * D)
    dyc3, dp, dsn = ssd_out_bwd(sv["yc3"], sv["p3"], dcat3, P["ssd_norm"][l][None])
    g["ssd_norm"] = dsn[0]
    do3, dan, delta = rms_o_bwd(sv["o3"], dcat3, P["attn_norm"][l][None])
    g["attn_norm"] = dan[0]
    dqraw, dkn3, dv3, dkr3, comm_out = attn_bwd(sv["q3"], sv["k3"], sv["v3"], do3, sv["lse"], delta.reshape(B, MLA_H, 1, S),
                                                cosf, sinf, comm)
    dqraw = dqraw.reshape(T, -1)
    dcqn = mm(dqraw, W["w_uq"], "nt", "mm_uq_dx")
    g["w_uq"] = mm(sv["cqn"].reshape(T, Q_RANK), dqraw, "tn", "mm_uq_dw")
    dp, dqn = rms_bwd_into(sv["p3"], 512, OFF_CQ // 512, Q_RANK, dcqn.reshape(B, S, Q_RANK), P["q_norm"][l][None], dp, "rms_q_bwd")
    g["q_norm"] = dqn[0]
    bias_col, alog_col = sv["bias_row"].reshape(128, 1), sv["alog_row"].reshape(128, 1)
    comm_rest = (layer_grad_slices(g, REST), True) if send_rest else None
    dxc3, ddt3, dalog, dd, dbias, rest_out = ssd_bwd(sv["xc3"], sv["p3"], dyc3, sv["states"], sv["bias_row"], sv["alog_row"],
                                                     bias_col, alog_col, sv["dcol"], comm_rest)
    heads = slice(DT_LANE, DT_LANE + SSD_HEADS)
    g["a_log"], g["d_skip"], g["dt_bias"] = dalog[heads, 0], dd[heads, 0], dbias[heads, 0]
    dp, dcw, dcb = conv_ssd_bwd(sv["p3"], sv["xpre"], dxc3, W["conv_w"], dp)
    g["conv_w"], g["conv_b"] = dcw, dcb[0]
    dp = kprep_bwd(dkr3, ddt3, cosf, sinf, dp)
    dkv = (dkn3.reshape(T, -1), dv3.reshape(T, -1))
    dckvn = mm(dkv, W["w_ukv"], "nt", "mm_ukv_dx")
    g["w_ukv"] = jnp.concatenate([mm(sv["ckvn"].reshape(T, KV_RANK), d, "tn", "mm_ukv_dw") for d in dkv], axis=1)
    dp, dkn = rms_bwd_into(sv["p3"], KV_RANK, OFF_CKV // KV_RANK, KV_RANK, dckvn.reshape(B, S, KV_RANK), P["kv_norm"][l][None], dp,
                           "rms_kv_bwd")
    g["kv_norm"] = dkn[0]
    dp = dp.reshape(T, IN_COLS)
    dh = mm(dp, W["w_in"], "nt", "mm_in_dx", out_dtype=bf16)
    g["w_in"] = mm(sv["h"].reshape(T, D), dp, "tn", "mm_in_dw")
    comm_first = (layer_grad_slices(g, FIRST), True) if send_rest else None
    dx0, dsh1, dsc1, dnx, *first_out = normmod_bwd(sv["x"], dh.reshape(B, S, D), dx1, mod, P["norm_mix"][l][None], 1,
                                                   comm=comm_first)
    g["norm_mix"] = dnx[0]
    dmod = jnp.concatenate([dsh1, dsc1, dg1, dsh2, dsc2, dg2], axis=1)
    return dx0, dmod, g, comm_out, (first_out[0] if first_out else []) + rest_out


def kernel(x, c, positions, w_ada, b_ada, norm_mix, w_in, conv_w, conv_b, dt_bias, a_log, d_skip, ssd_norm, q_norm, w_uq, kv_norm, w_ukv, attn_norm, w_out, norm_mlp, w_up, conv_ff_w, conv_ff_b, w_down, final_norm, loss_target, m_w_ada, m_b_ada, m_norm_mix, m_w_in, m_conv_w, m_conv_b, m_dt_bias, m_a_log, m_d_skip, m_ssd_norm, m_q_norm, m_w_uq, m_kv_norm, m_w_ukv, m_attn_norm, m_w_out, m_norm_mlp, m_w_up, m_conv_ff_w, m_conv_ff_b, m_w_down, m_final_norm, v_w_ada, v_b_ada, v_norm_mix, v_w_in, v_conv_w, v_conv_b, v_dt_bias, v_a_log, v_d_skip, v_ssd_norm, v_q_norm, v_w_uq, v_kv_norm, v_w_ukv, v_attn_norm, v_w_out, v_norm_mlp, v_w_up, v_conv_ff_w, v_conv_ff_b, v_w_down, v_final_norm):
    given = dict(locals())
    B, S, _ = x.shape
    me = 4 * lax.axis_index("x") + 2 * lax.axis_index("y") + lax.axis_index("c")
    P = {n: given[n] for n in SMALL}

    def shards(l, entries):
        return [given[n][l] if n in CONVS else given[n][l].astype(bf16) for n, _ in entries]

    *gathered, c_all = exchange(shards(0, FIRST) + [c], False, "gather_weights")
    W = [layer_weights(gathered, FIRST), None]

    n_ada = w_ada.shape[2]
    c_all = c_all.reshape(N_DEV * B, D)
    b_sh = lax.dynamic_slice_in_dim(b_ada, me * n_ada, n_ada, axis=1)
    mod_sh = jnp.stack([ada_fwd(c_all, w_ada[l], b_sh[l][None]) for l in range(DEPTH)])
    (mod_g,) = exchange([mod_sh], False, "gather_mod")
    mod_mine = lax.dynamic_slice_in_dim(mod_g, me * B, B, axis=2)
    mods = mod_mine.transpose(1, 2, 0, 3).reshape(DEPTH, B, 6, D)

    inv_freq = jnp.asarray(1.0 / (ROPE_BASE ** (np.arange(0, ROPE, 2, dtype=np.float32) / ROPE)))
    ang = positions.astype(f32)[..., None] * inv_freq
    zeros = jnp.zeros((B, S, 128 - ROPE), f32)
    cosf = jnp.concatenate([jnp.cos(ang), jnp.cos(ang), zeros], axis=-1)
    sinf = jnp.concatenate([jnp.sin(ang), jnp.sin(ang), zeros], axis=-1)

    saved = [None] * DEPTH
    late = lambda got: layer_weights(got, REST)
    xl, saved[0], _, W[0], gathered = layer_fwd(x, mods[0], W[0], P, 0, cosf, sinf, comm=(shards(0, REST), False), late=late,
                                                comm_up=(shards(1, FIRST), False))
    xl, saved[1], _, W[1], _ = layer_fwd(xl, mods[1], layer_weights(gathered, FIRST), P, 1, cosf, sinf,
                                         comm=(shards(1, REST), False), late=late)
    dxl, d_final, loss_part = final_loss(xl, final_norm[None], loss_target)
    grads, dmods, recv = [None] * DEPTH, [None] * DEPTH, [None] * DEPTH
    dxl, dmods[1], grads[1], _, _ = layer_bwd(dxl, saved[1], mods[1], W[1], P, 1, cosf, sinf)
    grad_x, dmods[0], grads[0], recv[1], recv_rest = layer_bwd(dxl, saved[0], mods[0], W[0], P, 0, cosf, sinf,
                                                               comm=(layer_grad_slices(grads[1], BIG), True), send_rest=True)

    stack = lambda n: jnp.stack([grads[l][n] for l in range(DEPTH)])
    small_names = [n for n in SMALL if n not in ("b_ada", "final_norm")]
    partial = pack([stack(n) for n in small_names] + [d_final[0], loss_part[0]])
    dmod_all = jnp.stack(dmods)
    part_g, dmod_g = exchange([partial, dmod_all], False, "exchange_tail")
    recv[0] = recv_rest
    big_g = [jnp.concatenate([sum_slots(recv[l][i][:, None]) for l in range(DEPTH)]) for i in range(len(BIG))]
    small_sum = sum_leading(part_g, "sum_partials")
    small_g = unpack(small_sum, [given[n].shape for n in small_names] + [(D,), (128,)])
    gsmall = dict(zip(small_names + ["final_norm"], small_g[:-1]))
    loss = small_g[-1][0]
    dmod_rows = dmod_g.transpose(0, 2, 1, 3, 4).reshape(N_DEV * B, DEPTH * 6 * D)
    gsmall["b_ada"] = sum_leading(dmod_rows.reshape(N_DEV * B, -1, 128), "sum_b_ada").reshape(DEPTH, 6 * D)
    dmod_cols = dmod_rows.reshape(N_DEV * B, DEPTH, N_DEV, n_ada)
    dmod_sh = lax.dynamic_slice_in_dim(dmod_cols, me, 1, axis=2)[:, :, 0, :]
    g_w_ada = jnp.stack([ada_bwd(c_all, dmod_sh[:, l, :]) for l in range(DEPTH)])

    res = {"grad": {}, "delta": {}, "new_m": {}, "new_v": {}}
    for n, gv in zip([n for n, _ in BIG] + ["w_ada"], big_g + [g_w_ada]):
        res["grad"][n] = gv
        res["delta"][n], res["new_m"][n], res["new_v"][n] = adamw_nd(given[n], gv, given["m_" + n], given["v_" + n])
    shapes = [given[n].shape for n in SMALL]
    flat = adamw(pack([given[n] for n in SMALL]), pack([gsmall[n] for n in SMALL]), pack([given["m_" + n] for n in SMALL]),
                 pack([given["v_" + n] for n in SMALL]))
    for n in SMALL:
        res["grad"][n] = gsmall[n]
    for key, arr in zip(("delta", "new_m", "new_v"), flat):
        res[key].update(zip(SMALL, unpack(arr, shapes)))
    order = ["w_ada", "b_ada", "norm_mix", "w_in", "conv_w", "conv_b", "dt_bias", "a_log", "d_skip", "ssd_norm", "q_norm", "w_uq",
             "kv_norm", "w_ukv", "attn_norm", "w_out", "norm_mlp", "w_up", "conv_ff_w", "conv_ff_b", "w_down", "final_norm"]
    return (loss, grad_x, *[res[k][n] for k in ("grad", "delta", "new_m", "new_v") for n in order])
```
